```python
import math
import jax
import jax.numpy as jnp
from jax import lax
import numpy as np

D_MODEL = 1024
BATCH = 2
SEQ = 8192
DEPTH = 2
DEC_BATCH = 32
DEC_SEQ = 1
PAST_LEN = 16384
PAGE_SIZE = 128

N_META = 16
QBLK = 128
N_BUCKETS = 32
MAX_DIST = 128
EPS = 1e-6

A_HEADS = 8
A_KV = 2
A_GROUP = A_HEADS // A_KV
A_DH = 64
A_DV = 2 * A_DH
A_WIDTH = A_HEADS * A_DV

B_HEADS = 4
B_DK = 128
B_DV = 128
B_WIDTH = B_HEADS * B_DV
B_QKV = 2 * B_HEADS * B_DK + B_WIDTH
CONV_W = 4
GDN_CHUNK = 64

C_HEADS = 8
C_KV = 2
C_GROUP = C_HEADS // C_KV
C_DH = 128
C_WIDTH = C_HEADS * C_DH
IDX_HEADS = 8
IDX_DIM = 64
TOPK_MAX = 256

N_BIAS_HEADS = A_HEADS
N_EVEN = (DEPTH + 1) // 2
N_ODD = DEPTH // 2

EVEN_COLS = (A_HEADS * 2 * A_DH, A_KV * 2 * A_DH, A_KV * A_DV, A_WIDTH, B_QKV, B_WIDTH, B_HEADS, B_HEADS)
ODD_COLS = (C_HEADS * C_DH, C_KV * C_DH, C_KV * C_DH, C_WIDTH, IDX_HEADS * IDX_DIM, IDX_DIM, IDX_HEADS)

kernel_name = 'hybrid_diffattn_gdn_dsa_step'


def _rmsnorm(x, g):
    xf = x.astype(jnp.float32)
    y = xf * lax.rsqrt(jnp.mean(xf * xf, axis=-1, keepdims=True) + EPS)
    return (y * g.astype(jnp.float32)).astype(x.dtype)


def _l2norm(x):
    return x * lax.rsqrt(jnp.sum(x * x, axis=-1, keepdims=True) + EPS)


def _split_cols(z, sizes):
    return jnp.split(z, np.cumsum(sizes)[:-1].tolist(), axis=-1)


def _split_meta(t, axis, size):
    meta = lax.slice_in_dim(t, 0, N_META, axis=axis)
    rest = lax.slice_in_dim(t, N_META, t.shape[axis], axis=axis)
    shp = rest.shape[:axis] + (rest.shape[axis] // size, size) + rest.shape[axis + 1:]
    return meta, jnp.moveaxis(rest.reshape(shp), axis, 0)


def _merge_meta(meta, rest, axis):
    rest = jnp.moveaxis(rest, 0, axis)
    shp = rest.shape[:axis] + (rest.shape[axis] * rest.shape[axis + 1],) + rest.shape[axis + 2:]
    return jnp.concatenate([meta, rest.reshape(shp).astype(meta.dtype)], axis=axis)


def _gather_rows(rows, idx):
    return jax.vmap(lambda r, i: r[i])(rows, idx)


def _t5_bias(dist, table):
    n = jnp.maximum(dist, 0)
    exact = N_BUCKETS // 2
    nf = jnp.maximum(n, 1).astype(jnp.float32)
    large = exact + (jnp.log(nf / exact) / math.log(MAX_DIST / exact) * (N_BUCKETS - exact)).astype(jnp.int32)
    bucket = jnp.where(n < exact, n, jnp.minimum(large, N_BUCKETS - 1))
    return table[bucket].astype(jnp.float32)


def _diff_lambda(lq1, lk1, lq2, lk2, lam_init):
    f32 = jnp.float32
    return (jnp.exp(jnp.sum(lq1.astype(f32) * lk1.astype(f32)))
            - jnp.exp(jnp.sum(lq2.astype(f32) * lk2.astype(f32))) + lam_init)


def _diff_attn(q, k, v, pos_q, pos_k, lam, table):
    tq, tk = pos_q.shape[0], pos_k.shape[0]
    s = jnp.einsum('bqhgcd,bkhcd->bhgcqk', q, k).astype(jnp.float32) * A_DH ** -0.5
    bias = jnp.transpose(_t5_bias(pos_q[:, None] - pos_k[None, :], table), (2, 0, 1))
    bias = bias.reshape(A_KV, A_GROUP, 1, tq, tk)
    mask = pos_k[None, :] <= pos_q[:, None]
    p = jax.nn.softmax(jnp.where(mask, s + bias, -jnp.inf), axis=-1)
    a = p[:, :, :, 0] - lam * p[:, :, :, 1]
    o = jnp.einsum('bhgqk,bkhe->bqhge', a.astype(v.dtype), v)
    return o.reshape(o.shape[0], tq, A_HEADS, A_DV)


def _a_output(o, gate, subln, lam_init):
    b, t = gate.shape[:2]
    y = _rmsnorm(o.astype(gate.dtype), subln) * (1.0 - lam_init)
    y = y * jax.nn.silu(gate.reshape(b, t, A_HEADS, A_DV))
    return y.reshape(b, t, A_WIDTH)


def _gdn_inputs(qkv, conv_buf, conv_w, a_raw, b_raw, a_log, dt_bias):
    b, t = qkv.shape[:2]
    xpad = jnp.concatenate([conv_buf.astype(qkv.dtype), qkv], axis=1)
    conv = sum(conv_w[i] * xpad[:, i:i + t] for i in range(CONV_W))
    new_buf = xpad[:, xpad.shape[1] - (CONV_W - 1):]
    act = jax.nn.silu(conv).astype(jnp.float32)
    q, k, v = _split_cols(act, (B_HEADS * B_DK, B_HEADS * B_DK, B_WIDTH))
    q = _l2norm(q.reshape(b, t, B_HEADS, B_DK)) * B_DK ** -0.5
    k = _l2norm(k.reshape(b, t, B_HEADS, B_DK))
    v = v.reshape(b, t, B_HEADS, B_DV)
    g = -jnp.exp(a_log.astype(jnp.float32)) * jax.nn.softplus(a_raw.astype(jnp.float32) + dt_bias.astype(jnp.float32))
    beta = jax.nn.sigmoid(b_raw.astype(jnp.float32))
    heads_first = lambda z: jnp.moveaxis(z, 2, 1)
    return heads_first(q), heads_first(k), heads_first(v), heads_first(g), heads_first(beta), new_buf


def _gdn_chunk(s0, q, k, v, g, beta):
    c = q.shape[2]
    ti = jnp.arange(c)
    incl = ti[:, None] >= ti[None, :]
    strict = ti[:, None] > ti[None, :]
    gcum = jnp.cumsum(g, axis=-1)
    diff = gcum[..., :, None] - gcum[..., None, :]
    decay = jnp.where(incl, jnp.exp(jnp.where(incl, diff, 0.0)), 0.0)
    kk = jnp.einsum('bhid,bhjd->bhij', k, k)
    a = jnp.where(strict, beta[..., :, None] * decay * kk, 0.0) + jnp.eye(c, dtype=q.dtype)
    rhs = beta[..., None] * (v - jnp.exp(gcum)[..., None] * jnp.einsum('bhid,bhde->bhie', k, s0))
    u = lax.linalg.triangular_solve(a, rhs, left_side=True, lower=True, unit_diagonal=True)
    qk = jnp.einsum('bhid,bhjd->bhij', q, k) * decay
    o = jnp.exp(gcum)[..., None] * jnp.einsum('bhid,bhde->bhie', q, s0) + jnp.einsum('bhij,bhje->bhie', qk, u)
    g_last = gcum[..., -1:]
    s_new = jnp.exp(g_last)[..., None] * s0 + jnp.einsum('bhid,bhie->bhde', k * jnp.exp(g_last - gcum)[..., None], u)
    return s_new, o


def _gdn_output(o, gate, norm_g):
    b, t = gate.shape[:2]
    y = _rmsnorm(jnp.moveaxis(o, 1, 2).astype(gate.dtype), norm_g)
    y = y * jax.nn.silu(gate.reshape(b, t, B_HEADS, B_DV))
    return y.reshape(b, t, B_WIDTH)


def _even_project(x, w_in):
    b, t = x.shape[:2]
    qa, ka, va, ga, qkv, gb, ab, bb = _split_cols(x @ w_in, EVEN_COLS)
    return (qa.reshape(b, t, A_KV, A_GROUP, 2, A_DH), ka.reshape(b, t, A_KV, 2, A_DH),
            va.reshape(b, t, A_KV, A_DV), ga, qkv, gb, ab, bb)


def _even_prompt(h, ep, lam, lam_init, table):
    norm, w_in, w_out, subln, conv_w, a_log, dt_bias, norm_g = ep
    b, seq_len = h.shape[:2]
    qa, ka, va, ga, qkv, gb, ab, bb = _even_project(_rmsnorm(h, norm), w_in)
    pos = jnp.arange(seq_len)
    q_meta, q_rest = _split_meta(qa, 1, QBLK)
    p_meta, p_rest = _split_meta(pos, 0, QBLK)
    o_meta = _diff_attn(q_meta, ka, va, p_meta, pos, lam, table)
    o_rest = lax.map(lambda qp: _diff_attn(qp[0], ka, va, qp[1], pos, lam, table), (q_rest, p_rest))
    ya = _a_output(_merge_meta(o_meta, o_rest, 1), ga, subln, lam_init)
    conv0 = jnp.zeros((b, CONV_W - 1, B_QKV), h.dtype)
    q, k, v, g, beta, conv_new = _gdn_inputs(qkv, conv0, conv_w, ab, bb, a_log, dt_bias)
    s0 = jnp.zeros((b, B_HEADS, B_DK, B_DV), jnp.float32)
    s_meta, ob_meta = _gdn_chunk(s0, q[:, :, :N_META], k[:, :, :N_META], v[:, :, :N_META],
                                 g[:, :, :N_META], beta[:, :, :N_META])
    chunks = tuple(_split_meta(z, 2, GDN_CHUNK)[1] for z in (q, k, v, g, beta))
    s_fin, ob_rest = lax.scan(lambda s, xs: _gdn_chunk(s, *xs), s_meta, chunks)
    yb = _gdn_output(_merge_meta(ob_meta, ob_rest, 2), gb, norm_g)
    out = jnp.concatenate([ya, yb], axis=-1) @ w_out
    return h + out, (ka, va, s_fin.astype(h.dtype), conv_new)


def _even_sample(h, cache_k, cache_v, s_prev, conv_prev, page_table, li, ep, lam, lam_init, table):
    norm, w_in, w_out, subln, conv_w, a_log, dt_bias, norm_g = ep
    b, t = h.shape[:2]
    past = page_table.shape[1] * PAGE_SIZE
    qa, ka, va, ga, qkv, gb, ab, bb = _even_project(_rmsnorm(h, norm), w_in)
    k_all = jnp.concatenate([cache_k[li, page_table].reshape((b, past) + ka.shape[2:]).astype(ka.dtype), ka], axis=1)
    v_all = jnp.concatenate([cache_v[li, page_table].reshape((b, past) + va.shape[2:]).astype(va.dtype), va], axis=1)
    pos_q = past + jnp.arange(t)
    pos_k = jnp.arange(past + t)
    ya = _a_output(_diff_attn(qa, k_all, v_all, pos_q, pos_k, lam, table), ga, subln, lam_init)
    q, k, v, g, beta, conv_new = _gdn_inputs(qkv, conv_prev, conv_w, ab, bb, a_log, dt_bias)
    s_new, ob = _gdn_chunk(s_prev.astype(jnp.float32), q, k, v, g, beta)
    yb = _gdn_output(ob, gb, norm_g)
    out = jnp.concatenate([ya, yb], axis=-1) @ w_out
    return h + out, (ka, va, s_new.astype(s_prev.dtype), conv_new)


def _odd_project(x, w_in):
    b, t = x.shape[:2]
    qc, kc, vc, gc, qi, ki, wi = _split_cols(x @ w_in, ODD_COLS)
    return (qc.reshape(b, t, C_KV, C_GROUP, C_DH), kc.reshape(b, t, C_KV, C_DH), vc.reshape(b, t, C_KV, C_DH),
            gc, qi.reshape(b, t, IDX_HEADS, IDX_DIM), ki, wi)


def _index_select(qi, wi, ki, pos_q, k_sel):
    dots = jnp.einsum('bqhd,bkd->bqhk', qi, ki).astype(jnp.float32) * IDX_DIM ** -0.5
    score = jnp.einsum('bqh,bqhk->bqk', wi.astype(jnp.float32) * IDX_HEADS ** -0.5, jax.nn.relu(dots))
    admissible = jnp.arange(ki.shape[1])[None, :] <= pos_q[:, None]
    _, idx = lax.top_k(jnp.where(admissible, score, -jnp.inf), k_sel)
    return idx, idx <= pos_q[None, :, None]


def _sparse_attn(qc, kg, vg, pos_q, idx, valid, table):
    b, tq, kn = idx.shape
    s = jnp.einsum('bqhgd,bqkhd->bhgqk', qc, kg).astype(jnp.float32) * C_DH ** -0.5
    bias = jnp.transpose(_t5_bias(pos_q[None, :, None] - idx, table), (0, 3, 1, 2)).reshape(b, C_KV, C_GROUP, tq, kn)
    p = jax.nn.softmax(jnp.where(valid[:, None, None], s + bias, -jnp.inf), axis=-1)
    o = jnp.einsum('bhgqk,bqkhd->bqhgd', p.astype(vg.dtype), vg)
    return o.reshape(b, tq, C_HEADS, C_DH)


def _c_output(o, gate, w_out):
    b, t = gate.shape[:2]
    return (o.reshape(b, t, C_WIDTH).astype(gate.dtype) * jax.nn.silu(gate)) @ w_out


def _odd_prompt(h, op, table):
    norm, w_in, w_out = op
    seq_len = h.shape[1]
    qc, kc, vc, gc, qi, ki, wi = _odd_project(_rmsnorm(h, norm), w_in)
    k_sel = min(TOPK_MAX, seq_len // 4)
    pos = jnp.arange(seq_len)

    def block(args):
        qc_b, qi_b, wi_b, pos_b = args
        idx, valid = _index_select(qi_b, wi_b, ki, pos_b, k_sel)
        return _sparse_attn(qc_b, _gather_rows(kc, idx), _gather_rows(vc, idx), pos_b, idx, valid, table)

    parts = [_split_meta(z, 1, QBLK) for z in (qc, qi, wi)] + [_split_meta(pos, 0, QBLK)]
    o_meta = block(tuple(p[0] for p in parts))
    o_rest = lax.map(block, tuple(p[1] for p in parts))
    out = _c_output(_merge_meta(o_meta, o_rest, 1), gc, w_out)
    return h + out, (kc, vc, ki)


def _odd_sample(h, cache_k, cache_v, cache_i, page_table, li, op, table):
    norm, w_in, w_out = op
    b, t = h.shape[:2]
    past = page_table.shape[1] * PAGE_SIZE
    qc, kc, vc, gc, qi, ki, wi = _odd_project(_rmsnorm(h, norm), w_in)
    ki_all = jnp.concatenate([cache_i[li, page_table].reshape(b, past, IDX_DIM).astype(ki.dtype), ki], axis=1)
    pos_q = past + jnp.arange(t)
    idx, valid = _index_select(qi, wi, ki_all, pos_q, min(TOPK_MAX, (past + t) // 4))
    in_past = (idx < past)[..., None, None]
    row = jnp.minimum(idx, past - 1)
    page = _gather_rows(page_table, row // PAGE_SIZE)
    off = row % PAGE_SIZE
    new = jnp.clip(idx - past, 0, t - 1)
    kg = jnp.where(in_past, cache_k[li, page, off].astype(kc.dtype), _gather_rows(kc, new))
    vg = jnp.where(in_past, cache_v[li, page, off].astype(vc.dtype), _gather_rows(vc, new))
    out = _c_output(_sparse_attn(qc, kg, vg, pos_q, idx, valid, table), gc, w_out)
    return h + out, (kc, vc, ki)


def setup_inputs(seed: int = 0) -> dict:
    key = jax.random.key(seed)
    ks = jax.random.split(key, 32)
    f32 = jnp.float32
    n_pages = PAST_LEN // PAGE_SIZE
    n_pool = (DEC_BATCH * n_pages * 5) // 4

    def nrm(i, shape, scale):
        return jax.random.normal(ks[i], shape, f32) * scale

    def gain(i, shape):
        return 1.0 + 0.02 * jax.random.normal(ks[i], shape, f32)

    page_table = jax.random.permutation(ks[0], n_pool)[:DEC_BATCH * n_pages].reshape(DEC_BATCH, n_pages).astype(jnp.int32)
    dt = jnp.exp(jax.random.uniform(ks[1], (N_EVEN, B_HEADS), f32, math.log(1e-3), math.log(1e-1)))
    return {
        'x_prompt': nrm(2, (BATCH, SEQ, D_MODEL), 1.0),
        'x_sample': nrm(3, (DEC_BATCH, DEC_SEQ, D_MODEL), 1.0),
        'cache_a_k': nrm(4, (N_EVEN, n_pool, PAGE_SIZE, A_KV, 2, A_DH), 1.0),
        'cache_a_v': nrm(5, (N_EVEN, n_pool, PAGE_SIZE, A_KV, A_DV), 1.0),
        'state_b_s': nrm(6, (N_EVEN, DEC_BATCH, B_HEADS, B_DK, B_DV), 0.3),
        'state_b_conv': nrm(7, (N_EVEN, DEC_BATCH, CONV_W - 1, B_QKV), 1.0),
        'cache_c_k': nrm(8, (N_ODD, n_pool, PAGE_SIZE, C_KV, C_DH), 1.0),
        'cache_c_v': nrm(9, (N_ODD, n_pool, PAGE_SIZE, C_KV, C_DH), 1.0),
        'cache_c_idx': nrm(10, (N_ODD, n_pool, PAGE_SIZE, IDX_DIM), 1.0),
        'page_table': page_table,
        'meta': nrm(11, (N_META, D_MODEL), 1.0),
        'bias_table': nrm(12, (N_BUCKETS, N_BIAS_HEADS), 0.5),
        'final_norm': gain(13, (D_MODEL,)),
        'norm_e': gain(14, (N_EVEN, D_MODEL)),
        'w_in_e': nrm(15, (N_EVEN, D_MODEL, sum(EVEN_COLS)), D_MODEL ** -0.5),
        'w_out_e': nrm(16, (N_EVEN, A_WIDTH + B_WIDTH, D_MODEL), (A_WIDTH + B_WIDTH) ** -0.5),
        'lam_q1': nrm(17, (N_EVEN, A_DH), 0.1),
        'lam_k1': nrm(18, (N_EVEN, A_DH), 0.1),
        'lam_q2': nrm(19, (N_EVEN, A_DH), 0.1),
        'lam_k2': nrm(20, (N_EVEN, A_DH), 0.1),
        'subln_a': gain(21, (N_EVEN, A_DV)),
        'conv_b': nrm(22, (N_EVEN, CONV_W, B_QKV), CONV_W ** -0.5),
        'a_log_b': jnp.log(jax.random.uniform(ks[23], (N_EVEN, B_HEADS), f32, 1.0, 16.0)),
        'dt_bias_b': dt + jnp.log(-jnp.expm1(-dt)),
        'norm_b': gain(24, (N_EVEN, B_DV)),
        'norm_o': gain(25, (N_ODD, D_MODEL)),
        'w_in_o': nrm(26, (N_ODD, D_MODEL, sum(ODD_COLS)), D_MODEL ** -0.5),
        'w_out_o': nrm(27, (N_ODD, C_WIDTH, D_MODEL), C_WIDTH ** -0.5),
    }


def reference(x_prompt, x_sample, cache_a_k, cache_a_v, state_b_s, state_b_conv, cache_c_k, cache_c_v, cache_c_idx,
              page_table, meta, bias_table, final_norm, norm_e, w_in_e, w_out_e, lam_q1, lam_k1, lam_q2, lam_k2,
              subln_a, conv_b, a_log_b, dt_bias_b, norm_b, norm_o, w_in_o, w_out_o):
    b = x_prompt.shape[0]
    hp = jnp.concatenate([jnp.broadcast_to(meta.astype(x_prompt.dtype)[None], (b, N_META, D_MODEL)), x_prompt], axis=1)
    hs = x_sample
    even_p, even_s, odd_p, odd_s = [], [], [], []
    for layer in range(DEPTH):
        li = layer // 2
        if layer % 2 == 0:
            lam_init = 0.8 - 0.6 * math.exp(-0.3 * layer)
            lam = _diff_lambda(lam_q1[li], lam_k1[li], lam_q2[li], lam_k2[li], lam_init)
            ep = (norm_e[li], w_in_e[li], w_out_e[li], subln_a[li], conv_b[li], a_log_b[li], dt_bias_b[li], norm_b[li])
            hp, rows = _even_prompt(hp, ep, lam, lam_init, bias_table)
            even_p.append(rows)
            hs, rows = _even_sample(hs, cache_a_k, cache_a_v, state_b_s[li], state_b_conv[li], page_table, li,
                                    ep, lam, lam_init, bias_table)
            even_s.append(rows)
        else:
            op = (norm_o[li], w_in_o[li], w_out_o[li])
            hp, rows = _odd_prompt(hp, op, bias_table)
            odd_p.append(rows)
            hs, rows = _odd_sample(hs, cache_c_k, cache_c_v, cache_c_idx, page_table, li, op, bias_table)
            odd_s.append(rows)
    pa_k, pa_v, pb_s, pb_conv = [jnp.stack(z) for z in zip(*even_p)]
    sa_k, sa_v, sb_s, sb_conv = [jnp.stack(z) for z in zip(*even_s)]
    pc_k, pc_v, pc_idx = [jnp.stack(z) for z in zip(*odd_p)]
    sc_k, sc_v, sc_idx = [jnp.stack(z) for z in zip(*odd_s)]
    y_prompt = _rmsnorm(hp, final_norm)[:, N_META:]
    y_sample = _rmsnorm(hs, final_norm)
    return (y_prompt, y_sample, pa_k, pa_v, pb_s, pb_conv, pc_k, pc_v, pc_idx, sa_k, sa_v, sb_s, sb_conv, sc_k, sc_v, sc_idx)
```

```python
import functools
import math

import jax
import jax.numpy as jnp
import numpy as np
from jax import lax
from jax.experimental import pallas as pl
from jax.experimental.pallas import tpu as pltpu

F32 = jnp.float32
BF16 = jnp.bfloat16
I32 = jnp.int32

EPS = 1e-6
N_BUCKETS = 32
MAX_DIST = 128
FAR_DIST = MAX_DIST

A_HEADS, A_KV, A_GROUP, A_DH, A_DV = 8, 2, 4, 64, 128
B_HEADS, B_DK, B_DV, CONV_W = 4, 128, 128, 4
C_HEADS, C_KV, C_GROUP, C_DH = 8, 2, 4, 128
IDX_HEADS, IDX_DIM, TOPK_MAX = 8, 64, 256
B_QKV = 2 * B_HEADS * B_DK + B_HEADS * B_DV

EVEN_COLS = (A_HEADS * 2 * A_DH, A_KV * 2 * A_DH, A_KV * A_DV, A_HEADS * A_DV, B_QKV, B_HEADS * B_DV, B_HEADS, B_HEADS)
ODD_COLS = (C_HEADS * C_DH, C_KV * C_DH, C_KV * C_DH, C_HEADS * C_DH, IDX_HEADS * IDX_DIM, IDX_DIM, IDX_HEADS)
EVEN_OFF = tuple(int(v) for v in np.cumsum((0,) + EVEN_COLS))
ODD_OFF = tuple(int(v) for v in np.cumsum((0,) + ODD_COLS))

LANES = 128
SUBLANES = 8
BLK = 128
NEG = -1e30
INT_MIN = -2 ** 31
VMEM_LIMIT = 56 * 1024 * 1024
HI = lax.Precision.HIGHEST

NT = (((1,), (1,)), ((), ()))


def _round_up(x, m):
    return (x + m - 1) // m * m


def _cparams(sem):
    return pltpu.CompilerParams(dimension_semantics=sem, vmem_limit_bytes=VMEM_LIMIT)


def _silu(x):
    return x * jax.nn.sigmoid(x)


def _sortable_key(score):
    score = jnp.where(score == 0.0, 0.0, score)
    bits = pltpu.bitcast(score, I32)
    return bits ^ ((bits >> 31) & jnp.int32(0x7FFFFFFF))


def _proj_kernel(x_ref, g_ref, w_ref, of_ref, ob_ref, xn_ref):
    @pl.when(pl.program_id(1) == 0)
    def _():
        xf = x_ref[...]
        y = xf * lax.rsqrt(jnp.mean(xf * xf, axis=-1, keepdims=True) + EPS)
        xn_ref[...] = (y * g_ref[...]).astype(BF16)

    acc = jnp.dot(xn_ref[...], w_ref[...], preferred_element_type=F32)
    of_ref[...] = acc
    ob_ref[...] = acc.astype(BF16)


def _norm_proj(x, gain, w_bf16, tm, tn):
    m, d = x.shape
    n = w_bf16.shape[1]
    return pl.pallas_call(
        _proj_kernel,
        grid=(m // tm, n // tn),
        in_specs=[pl.BlockSpec((tm, d), lambda i, j: (i, 0)),
                  pl.BlockSpec((1, d), lambda i, j: (0, 0)),
                  pl.BlockSpec((d, tn), lambda i, j: (0, j))],
        out_specs=[pl.BlockSpec((tm, tn), lambda i, j: (i, j)),
                   pl.BlockSpec((tm, tn), lambda i, j: (i, j))],
        out_shape=[jax.ShapeDtypeStruct((m, n), F32), jax.ShapeDtypeStruct((m, n), BF16)],
        scratch_shapes=[pltpu.VMEM((tm, d), BF16)],
        compiler_params=_cparams(("parallel", "arbitrary")),
        name="norm_proj",
    )(x, gain.reshape(1, d), w_bf16)


def _outproj_kernel(*refs, n_lhs, final_norm):
    ys = refs[:n_lhs]
    ws = refs[n_lhs:2 * n_lhs]
    h_ref = refs[2 * n_lhs]
    pos = 2 * n_lhs + 1
    fn_ref = refs[pos] if final_norm else None
    out_ref = refs[-1]
    acc = h_ref[...]
    for y_ref, w_ref in zip(ys, ws):
        acc = acc + jnp.dot(y_ref[...], w_ref[...], preferred_element_type=F32)
    if final_norm:
        y = acc * lax.rsqrt(jnp.mean(acc * acc, axis=-1, keepdims=True) + EPS)
        acc = y * fn_ref[...]
    out_ref[...] = acc


def _out_proj(ys, ws, h, tm, final_gain=None):
    m, n = h.shape
    n_lhs = len(ys)
    in_specs = [pl.BlockSpec((tm, y.shape[1]), lambda i: (i, 0)) for y in ys]
    in_specs += [pl.BlockSpec(w.shape, lambda i: (0, 0)) for w in ws]
    in_specs += [pl.BlockSpec((tm, n), lambda i: (i, 0))]
    args = list(ys) + list(ws) + [h]
    if final_gain is not None:
        in_specs += [pl.BlockSpec((1, n), lambda i: (0, 0))]
        args += [final_gain.reshape(1, n)]
    return pl.pallas_call(
        functools.partial(_outproj_kernel, n_lhs=n_lhs, final_norm=final_gain is not None),
        grid=(m // tm,),
        in_specs=in_specs,
        out_specs=pl.BlockSpec((tm, n), lambda i: (i, 0)),
        out_shape=jax.ShapeDtypeStruct((m, n), F32),
        compiler_params=_cparams(("parallel",)),
        name="out_proj",
    )(*args)


def _softmax_update(s, v, m_ref, l_ref, acc_ref, idx):
    m_old = m_ref[idx]
    m_new = jnp.maximum(m_old, jnp.max(s, axis=1, keepdims=True))
    alpha = jnp.exp(m_old - m_new)
    p = jnp.exp(s - m_new)
    l_ref[idx] = alpha * l_ref[idx] + jnp.sum(p, axis=1, keepdims=True)
    acc_ref[idx] = alpha * acc_ref[idx] + jnp.dot(p.astype(BF16), v, preferred_element_type=F32)
    m_ref[idx] = m_new


def _diff_attn_kernel(q_ref, k_ref, v_ref, gate_ref, bias_ref, lamv_ref, subln_ref, out_ref,
                      qm_ref, m_ref, l_ref, acc_ref, *, lam_init):
    qi = pl.program_id(2)
    rows = A_GROUP * BLK
    lane = lax.broadcasted_iota(I32, (BLK, 2 * A_DH), 1)
    for g in range(A_GROUP):
        qg = q_ref[:, g * 2 * A_DH:(g + 1) * 2 * A_DH] * jnp.asarray(A_DH ** -0.5, BF16)
        for c in range(2):
            keep = (lane < A_DH) if c == 0 else (lane >= A_DH)
            qm_ref[c, g * BLK:(g + 1) * BLK, :] = jnp.where(keep, qg, jnp.zeros_like(qg))
    m_ref[...] = jnp.full(m_ref.shape, NEG, F32)
    l_ref[...] = jnp.zeros(l_ref.shape, F32)
    acc_ref[...] = jnp.zeros(acc_ref.shape, F32)

    def block(kb, tile):
        start = pl.multiple_of(kb * BLK, BLK)
        k = k_ref[pl.ds(start, BLK), :]
        v = v_ref[pl.ds(start, BLK), :]
        bias = bias_ref[tile]
        for c in range(2):
            s = lax.dot_general(qm_ref[c], k, NT, preferred_element_type=F32) + bias
            _softmax_update(s, v, m_ref, l_ref, acc_ref, c)

    def far(kb, carry):
        block(kb, 0)
        return carry

    lax.fori_loop(0, jnp.maximum(qi - 1, 0), far, 0)

    @pl.when(qi >= 1)
    def _():
        block(qi - 1, 1)

    block(qi, 2)

    lv = lamv_ref[...]
    lam = (jnp.exp(jnp.sum(lv[0:1] * lv[1:2], axis=1, keepdims=True))
           - jnp.exp(jnp.sum(lv[2:3] * lv[3:4], axis=1, keepdims=True)) + lam_init)
    o = acc_ref[0] / l_ref[0] - lam * (acc_ref[1] / l_ref[1])
    y = o * lax.rsqrt(jnp.mean(o * o, axis=-1, keepdims=True) + EPS)
    y = (y * subln_ref[...]) * (1.0 - lam_init)
    for g in range(A_GROUP):
        gate = gate_ref[:, g * A_DV:(g + 1) * A_DV]
        out_ref[:, g * A_DV:(g + 1) * A_DV] = (y[g * BLK:(g + 1) * BLK] * _silu(gate)).astype(BF16)
    del rows


def _diff_attn_prompt(zf, zb, bias_tiles, lamv, subln, nb, lp, lam_init):
    nq = lp // BLK
    qw = A_GROUP * 2 * A_DH
    k_blk0 = EVEN_OFF[1] // (2 * A_DH)
    v_blk0 = EVEN_OFF[2] // A_DV
    g_blk0 = EVEN_OFF[3] // (A_GROUP * A_DV)
    rows = A_GROUP * BLK
    return pl.pallas_call(
        functools.partial(_diff_attn_kernel, lam_init=lam_init),
        grid=(nb, A_KV, nq),
        in_specs=[pl.BlockSpec((BLK, qw), lambda b, h, i: (b * nq + i, h)),
                  pl.BlockSpec((lp, 2 * A_DH), lambda b, h, i: (b, k_blk0 + h)),
                  pl.BlockSpec((lp, A_DV), lambda b, h, i: (b, v_blk0 + h)),
                  pl.BlockSpec((BLK, A_GROUP * A_DV), lambda b, h, i: (b * nq + i, g_blk0 + h)),
                  pl.BlockSpec((None, 3, rows, BLK), lambda b, h, i: (h, 0, 0, 0)),
                  pl.BlockSpec((4, A_DH), lambda b, h, i: (0, 0)),
                  pl.BlockSpec((1, A_DV), lambda b, h, i: (0, 0))],
        out_specs=pl.BlockSpec((BLK, A_GROUP * A_DV), lambda b, h, i: (b * nq + i, h)),
        out_shape=jax.ShapeDtypeStruct((nb * lp, A_HEADS * A_DV), BF16),
        scratch_shapes=[pltpu.VMEM((2, rows, 2 * A_DH), BF16),
                        pltpu.VMEM((2, rows, LANES), F32),
                        pltpu.VMEM((2, rows, LANES), F32),
                        pltpu.VMEM((2, rows, A_DV), F32)],
        compiler_params=_cparams(("parallel", "parallel", "arbitrary")),
        name="diff_attn_prompt",
    )(zb, zb, zb, zf, bias_tiles, lamv, subln.reshape(1, A_DV))


def _softplus(x):
    return jnp.maximum(x, 0.0) + jnp.log1p(jnp.exp(-jnp.abs(x)))


def _gdn_chunk_math(s0, q, k, v, gb, betab, row_ge, row_gt, eye):
    c = q.shape[0]
    ltri = jnp.where(row_ge, 1.0, 0.0).astype(F32)
    gcum = jnp.dot(ltri, gb, precision=HI, preferred_element_type=F32)
    diff = gcum - gcum.T
    decay = jnp.where(row_ge, jnp.exp(jnp.where(row_ge, diff, 0.0)), 0.0)
    kk = lax.dot_general(k, k, NT, precision=HI, preferred_element_type=F32)
    x = -jnp.where(row_gt, betab * decay * kk, 0.0)
    t = eye + x
    p = x
    for _ in range(int(math.log2(c)) - 1):
        p = jnp.dot(p, p, precision=HI, preferred_element_type=F32)
        t = t + jnp.dot(t, p, precision=HI, preferred_element_type=F32)
    eg = jnp.exp(gcum)
    ks = jnp.dot(k, s0, precision=HI, preferred_element_type=F32)
    rhs = betab * (v - eg * ks)
    u = jnp.dot(t, rhs, precision=HI, preferred_element_type=F32)
    qk = lax.dot_general(q, k, NT, precision=HI, preferred_element_type=F32) * decay
    o = eg * jnp.dot(q, s0, precision=HI, preferred_element_type=F32) + jnp.dot(
        qk, u, precision=HI, preferred_element_type=F32)
    glast = gcum[c - 1:c, :]
    kd = k * jnp.exp(glast - gcum)
    s_new = jnp.exp(glast) * s0 + jnp.dot(kd.T, u, precision=HI, preferred_element_type=F32)
    return s_new, o


def _gdn_prompt_kernel(xq_ref, xk_ref, xv_ref, gate_ref, ab_ref, convw_ref, gp_ref, normg_ref,
                       y_ref, sfin_ref, xbuf_ref, s_ref, *, seq_len):
    ci = pl.program_id(1)
    hw = B_HEADS * B_DK

    @pl.when(ci == 0)
    def _():
        xbuf_ref[0:SUBLANES, :] = jnp.zeros((SUBLANES, B_QKV), F32)
        s_ref[...] = jnp.zeros(s_ref.shape, F32)

    xbuf_ref[SUBLANES:SUBLANES + BLK, 0:hw] = xq_ref[...]
    xbuf_ref[SUBLANES:SUBLANES + BLK, hw:2 * hw] = xk_ref[...]
    xbuf_ref[SUBLANES:SUBLANES + BLK, 2 * hw:3 * hw] = xv_ref[...]
    conv = jnp.zeros((BLK, B_QKV), F32)
    for i in range(CONV_W):
        conv = conv + convw_ref[i:i + 1, :] * xbuf_ref[pl.ds(SUBLANES - (CONV_W - 1) + i, BLK), :]
    tail = xbuf_ref[BLK:BLK + SUBLANES, :]
    xbuf_ref[0:SUBLANES, :] = tail
    act = _silu(conv)

    row = lax.broadcasted_iota(I32, (BLK, BLK), 0)
    col = lax.broadcasted_iota(I32, (BLK, BLK), 1)
    row_ge = row >= col
    row_gt = row > col
    eye = jnp.where(row == col, 1.0, 0.0).astype(F32)
    valid = (ci * BLK + row) < seq_len
    ab = ab_ref[...]
    gp = gp_ref[...]
    for h in range(B_HEADS):
        q = act[:, h * B_DK:(h + 1) * B_DK]
        k = act[:, hw + h * B_DK:hw + (h + 1) * B_DK]
        v = act[:, 2 * hw + h * B_DV:2 * hw + (h + 1) * B_DV]
        q = q * lax.rsqrt(jnp.sum(q * q, axis=-1, keepdims=True) + EPS) * (B_DK ** -0.5)
        k = k * lax.rsqrt(jnp.sum(k * k, axis=-1, keepdims=True) + EPS)
        a_raw = jnp.broadcast_to(ab[:, h:h + 1], (BLK, BLK))
        b_raw = jnp.broadcast_to(ab[:, B_HEADS + h:B_HEADS + h + 1], (BLK, BLK))
        a_log = gp[0:1, h:h + 1]
        dt_b = gp[1:2, h:h + 1]
        gb = jnp.where(valid, -jnp.exp(a_log) * _softplus(a_raw + dt_b), 0.0)
        betab = jnp.where(valid, jax.nn.sigmoid(b_raw), 0.0)
        s_new, o = _gdn_chunk_math(s_ref[h], q, k, v, gb, betab, row_ge, row_gt, eye)
        s_ref[h] = s_new
        y = o * lax.rsqrt(jnp.mean(o * o, axis=-1, keepdims=True) + EPS) * normg_ref[...]
        y_ref[:, h * B_DV:(h + 1) * B_DV] = (y * _silu(gate_ref[:, h * B_DV:(h + 1) * B_DV])).astype(BF16)

    @pl.when(ci == pl.num_programs(1) - 1)
    def _():
        sfin_ref[...] = s_ref[...]


def _gdn_prompt(zf, conv_w, a_log, dt_bias, norm_g, nb, lp, seq_len):
    nc = lp // BLK
    hw = B_HEADS * B_DK
    c0 = EVEN_OFF[4] // hw
    g0 = EVEN_OFF[5] // hw
    ab0 = EVEN_OFF[6] // LANES
    gp = jnp.zeros((SUBLANES, LANES), F32).at[0, :B_HEADS].set(a_log).at[1, :B_HEADS].set(dt_bias)
    return pl.pallas_call(
        functools.partial(_gdn_prompt_kernel, seq_len=seq_len),
        grid=(nb, nc),
        in_specs=[pl.BlockSpec((BLK, hw), lambda b, i: (b * nc + i, c0)),
                  pl.BlockSpec((BLK, hw), lambda b, i: (b * nc + i, c0 + 1)),
                  pl.BlockSpec((BLK, hw), lambda b, i: (b * nc + i, c0 + 2)),
                  pl.BlockSpec((BLK, hw), lambda b, i: (b * nc + i, g0)),
                  pl.BlockSpec((BLK, LANES), lambda b, i: (b * nc + i, ab0)),
                  pl.BlockSpec((CONV_W, B_QKV), lambda b, i: (0, 0)),
                  pl.BlockSpec((SUBLANES, LANES), lambda b, i: (0, 0)),
                  pl.BlockSpec((1, B_DV), lambda b, i: (0, 0))],
        out_specs=[pl.BlockSpec((BLK, hw), lambda b, i: (b * nc + i, 0)),
                   pl.BlockSpec((None, B_HEADS, B_DK, B_DV), lambda b, i: (b, 0, 0, 0))],
        out_shape=[jax.ShapeDtypeStruct((nb * lp, hw), BF16),
                   jax.ShapeDtypeStruct((nb, B_HEADS, B_DK, B_DV), F32)],
        scratch_shapes=[pltpu.VMEM((BLK + SUBLANES, B_QKV), F32),
                        pltpu.VMEM((B_HEADS, B_DK, B_DV), F32)],
        compiler_params=_cparams(("parallel", "arbitrary")),
        name="gdn_prompt",
    )(zf, zf, zf, zf, zf, conv_w, gp, norm_g.reshape(1, B_DV))


def _select_threshold(count_fn, k_sel, n_index_bits, shape):
    def tau_step(i, tau):
        cand = tau + lax.shift_left(jnp.int32(1), 31 - i)
        cnt = count_fn(lambda key, idx: key >= cand)
        return jnp.where(cnt >= k_sel, cand, tau)

    tau = lax.fori_loop(0, 32, tau_step, jnp.full(shape, INT_MIN, I32))
    need = k_sel - count_fn(lambda key, idx: key > tau)

    def j_step(i, jm):
        cand = jm | lax.shift_left(jnp.int32(1), n_index_bits - 1 - i)
        cnt = count_fn(lambda key, idx: (key == tau) & (idx < cand))
        return jnp.where(cnt < need, cand, jm)

    jmax = lax.fori_loop(0, n_index_bits, j_step, jnp.zeros(shape, I32))
    return tau, jmax


def _sparse_attn_kernel(qi_ref, kw_ref, kidx_ref, qc_ref, kc_ref, vc_ref, g0_ref, g1_ref, bias_ref, out_ref,
                        keys_ref, qm_ref, wb_ref, qh_ref, m_ref, l_ref, acc_ref, *, k_sel, n_index_bits):
    qb = pl.program_id(1)
    lane = lax.broadcasted_iota(I32, (BLK, LANES), 1)
    rowi = lax.broadcasted_iota(I32, (BLK, LANES), 0)

    kw = kw_ref[...]
    for p in range(IDX_HEADS // 2):
        pair = qi_ref[:, p * LANES:(p + 1) * LANES] * (IDX_DIM ** -0.5)
        swapped = pltpu.roll(pair, IDX_DIM, 1)
        for e, src in enumerate((pair, swapped)):
            h = 2 * p + e
            qm_ref[h * BLK:(h + 1) * BLK, :] = jnp.where(lane < IDX_DIM, src, 0.0).astype(BF16)
            wcol = kw[:, IDX_DIM + h:IDX_DIM + h + 1] * (IDX_HEADS ** -0.5)
            wb_ref[h] = jnp.broadcast_to(wcol, (BLK, LANES))

    def scores(kb):
        start = pl.multiple_of(kb * BLK, BLK)
        d = lax.dot_general(qm_ref[...], kidx_ref[pl.ds(start, BLK), :], NT, preferred_element_type=F32)
        sc = jnp.zeros((BLK, LANES), F32)
        for h in range(IDX_HEADS):
            sc = sc + wb_ref[h] * jnp.maximum(d[h * BLK:(h + 1) * BLK], 0.0)
        return sc

    def score_body(kb, carry):
        keys_ref[kb] = _sortable_key(scores(kb))
        return carry

    lax.fori_loop(0, qb, score_body, 0)
    keys_ref[qb] = _sortable_key(jnp.where(lane <= rowi, scores(qb), -jnp.inf))

    def count_fn(pred):
        def body(kb, acc):
            return acc + jnp.where(pred(keys_ref[kb], kb * BLK + lane), 1, 0)
        acc = lax.fori_loop(0, qb + 1, body, jnp.zeros((BLK, LANES), I32))
        return jnp.sum(acc, axis=1, keepdims=True)

    tau, jmax = _select_threshold(count_fn, k_sel, n_index_bits, (BLK, LANES))

    rows = C_GROUP * BLK
    for h in range(C_KV):
        for g in range(C_GROUP):
            col = (h * C_GROUP + g) * C_DH
            qh_ref[h, g * BLK:(g + 1) * BLK, :] = (qc_ref[:, col:col + C_DH] * (C_DH ** -0.5)).astype(BF16)
    m_ref[...] = jnp.full(m_ref.shape, NEG, F32)
    l_ref[...] = jnp.zeros(l_ref.shape, F32)
    acc_ref[...] = jnp.zeros(acc_ref.shape, F32)

    def block(kb, tile):
        start = pl.multiple_of(kb * BLK, BLK)
        key = keys_ref[kb]
        sel = (key > tau) | ((key == tau) & ((kb * BLK + lane) <= jmax))
        selneg = jnp.where(sel, 0.0, NEG)
        selneg = jnp.concatenate([selneg] * C_GROUP, axis=0)
        for h in range(C_KV):
            k = kc_ref[pl.ds(start, BLK), h * C_DH:(h + 1) * C_DH]
            v = vc_ref[pl.ds(start, BLK), h * C_DH:(h + 1) * C_DH]
            s = lax.dot_general(qh_ref[h], k, NT, preferred_element_type=F32) + bias_ref[h, tile] + selneg
            _softmax_update(s, v, m_ref, l_ref, acc_ref, h)

    def far(kb, carry):
        block(kb, 0)
        return carry

    lax.fori_loop(0, jnp.maximum(qb - 1, 0), far, 0)

    @pl.when(qb >= 1)
    def _():
        block(qb - 1, 1)

    block(qb, 2)

    half = (C_HEADS // 2) * C_DH
    for h in range(C_KV):
        o = acc_ref[h] / l_ref[h]
        gref = g0_ref if h == 0 else g1_ref
        for g in range(C_GROUP):
            gate = gref[:, g * C_DH:(g + 1) * C_DH]
            out_ref[:, h * half + g * C_DH:h * half + (g + 1) * C_DH] = (
                o[g * BLK:(g + 1) * BLK] * _silu(gate)).astype(BF16)
    del rows


def _sparse_attn_prompt(zf, zb, bias_tiles, nb, lp, k_sel):
    nq = lp // BLK
    half = (C_HEADS // 2) * C_DH
    qi0 = ODD_OFF[4] // (IDX_HEADS * IDX_DIM)
    ki0 = ODD_OFF[5] // LANES
    kc0 = ODD_OFF[1] // (C_KV * C_DH)
    vc0 = ODD_OFF[2] // (C_KV * C_DH)
    g0 = ODD_OFF[3] // half
    rows = C_GROUP * BLK
    n_index_bits = max(1, int(math.ceil(math.log2(lp))))
    return pl.pallas_call(
        functools.partial(_sparse_attn_kernel, k_sel=k_sel, n_index_bits=n_index_bits),
        grid=(nb, nq),
        in_specs=[pl.BlockSpec((BLK, IDX_HEADS * IDX_DIM), lambda b, i: (b * nq + i, qi0)),
                  pl.BlockSpec((BLK, LANES), lambda b, i: (b * nq + i, ki0)),
                  pl.BlockSpec((lp, LANES), lambda b, i: (b, ki0)),
                  pl.BlockSpec((BLK, C_HEADS * C_DH), lambda b, i: (b * nq + i, 0)),
                  pl.BlockSpec((lp, C_KV * C_DH), lambda b, i: (b, kc0)),
                  pl.BlockSpec((lp, C_KV * C_DH), lambda b, i: (b, vc0)),
                  pl.BlockSpec((BLK, half), lambda b, i: (b * nq + i, g0)),
                  pl.BlockSpec((BLK, half), lambda b, i: (b * nq + i, g0 + 1)),
                  pl.BlockSpec((C_KV, 3, rows, BLK), lambda b, i: (0, 0, 0, 0))],
        out_specs=pl.BlockSpec((BLK, C_HEADS * C_DH), lambda b, i: (b * nq + i, 0)),
        out_shape=jax.ShapeDtypeStruct((nb * lp, C_HEADS * C_DH), BF16),
        scratch_shapes=[pltpu.VMEM((nq, BLK, LANES), I32),
                        pltpu.VMEM((IDX_HEADS * BLK, LANES), BF16),
                        pltpu.VMEM((IDX_HEADS, BLK, LANES), F32),
                        pltpu.VMEM((C_KV, rows, C_DH), BF16),
                        pltpu.VMEM((C_KV, rows, LANES), F32),
                        pltpu.VMEM((C_KV, rows, LANES), F32),
                        pltpu.VMEM((C_KV, rows, C_DH), F32)],
        compiler_params=_cparams(("parallel", "arbitrary")),
        name="sparse_attn_prompt",
    )(zf, zf, zb, zf, zb, zb, zf, zf, bias_tiles)


PAGES_PER_STEP = 8
DEC_ROWS = 16


def _paged_attn_kernel(*refs, n_pages_step, masked, page):
    if masked:
        pt_ref, tau_ref, jmax_ref, selnew_ref = refs[:4]
        refs = refs[4:]
    else:
        pt_ref = refs[0]
        refs = refs[1:]
    q_ref, knew_ref, vnew_ref, bfar_ref, blast_ref, b0_ref = refs[:6]
    refs = refs[6:]
    if masked:
        keys_ref = refs[0]
        refs = refs[1:]
    k_refs = refs[:n_pages_step]
    v_refs = refs[n_pages_step:2 * n_pages_step]
    out_ref, m_ref, l_ref, acc_ref = refs[2 * n_pages_step:]
    del pt_ref
    b = pl.program_id(0)
    j = pl.program_id(1)
    width = n_pages_step * page
    q = q_ref[...]

    @pl.when(j == 0)
    def _():
        s_new = jnp.sum(q.astype(F32) * knew_ref[...].astype(BF16).astype(F32), axis=1, keepdims=True) + b0_ref[...]
        v_new = jnp.broadcast_to(vnew_ref[...].astype(BF16).astype(F32), acc_ref.shape)
        if masked:
            take = selnew_ref[b] > 0
            m_ref[...] = jnp.where(take, s_new, NEG)
            l_ref[...] = jnp.where(take, 1.0, 0.0) * jnp.ones(l_ref.shape, F32)
            acc_ref[...] = jnp.where(take, v_new, 0.0)
        else:
            m_ref[...] = s_new
            l_ref[...] = jnp.ones(l_ref.shape, F32)
            acc_ref[...] = v_new

    kcat = jnp.concatenate([r[...].astype(BF16) for r in k_refs], axis=0)
    vcat = jnp.concatenate([r[...].astype(BF16) for r in v_refs], axis=0)
    s = lax.dot_general(q, kcat, NT, preferred_element_type=F32)
    s = s + jnp.where(j == pl.num_programs(1) - 1, blast_ref[...], bfar_ref[...])
    if masked:
        key = keys_ref[...]
        idx = j * width + lax.broadcasted_iota(I32, (1, width), 1)
        tau = tau_ref[b]
        sel = (key > tau) | ((key == tau) & (idx <= jmax_ref[b]))
        s = jnp.where(sel, s, NEG)
    m_old = m_ref[...]
    m_new = jnp.maximum(m_old, jnp.max(s, axis=1, keepdims=True))
    alpha = jnp.exp(m_old - m_new)
    p = jnp.exp(s - m_new[:, 0:1])
    l_ref[...] = alpha * l_ref[...] + jnp.sum(p, axis=1, keepdims=True)
    acc_ref[...] = jnp.concatenate([alpha, alpha], axis=1) * acc_ref[...] + jnp.dot(
        p.astype(BF16), vcat, preferred_element_type=F32)
    m_ref[...] = m_new

    @pl.when(j == pl.num_programs(1) - 1)
    def _():
        o = acc_ref[...]
        rowi = lax.broadcasted_iota(I32, (DEC_ROWS, LANES), 0)
        upper = (rowi >= DEC_ROWS // 2) if not masked else ((rowi >= DEC_ROWS // 4) & (rowi < DEC_ROWS // 2))
        out_ref[...] = jnp.where(upper, o[:, LANES:], o[:, :LANES]) / l_ref[...]


def _paged_attention(qprime, k_new, v_new, bias_far, bias_last, bias0, k_cache, v_cache, page_table, mask_args=None):
    db = qprime.shape[0]
    n_pages = page_table.shape[1]
    page = k_cache.shape[1]
    width = k_cache.shape[2]
    g = PAGES_PER_STEP
    while n_pages % g:
        g //= 2
    n_steps = n_pages // g
    masked = mask_args is not None
    n_pref = 4 if masked else 1

    def page_map(gi):
        return lambda b, j, pt, *_: (pt[b * n_pages + j * g + gi], 0, 0)

    in_specs = [pl.BlockSpec((None, DEC_ROWS, width), lambda b, j, *_: (b, 0, 0)),
                pl.BlockSpec((None, 1, width), lambda b, j, *_: (b, 0, 0)),
                pl.BlockSpec((None, 1, width), lambda b, j, *_: (b, 0, 0)),
                pl.BlockSpec((DEC_ROWS, g * page), lambda b, j, *_: (0, 0)),
                pl.BlockSpec((DEC_ROWS, g * page), lambda b, j, *_: (0, 0)),
                pl.BlockSpec((DEC_ROWS, LANES), lambda b, j, *_: (0, 0))]
    args = [qprime, k_new, v_new, bias_far, bias_last, bias0]
    prefetch = [page_table.reshape(-1)]
    if masked:
        keys, tau, jmax, selnew = mask_args
        prefetch += [tau, jmax, selnew]
        in_specs += [pl.BlockSpec((None, 1, g * page), lambda b, j, *_: (b, 0, j))]
        args += [keys]
    in_specs += [pl.BlockSpec((None, page, width), page_map(gi)) for gi in range(g)]
    in_specs += [pl.BlockSpec((None, page, width), page_map(gi)) for gi in range(g)]
    args += [k_cache] * g + [v_cache] * g
    grid_spec = pltpu.PrefetchScalarGridSpec(
        num_scalar_prefetch=n_pref,
        grid=(db, n_steps),
        in_specs=in_specs,
        out_specs=pl.BlockSpec((None, DEC_ROWS, LANES), lambda b, j, *_: (b, 0, 0)),
        scratch_shapes=[pltpu.VMEM((DEC_ROWS, LANES), F32),
                        pltpu.VMEM((DEC_ROWS, LANES), F32),
                        pltpu.VMEM((DEC_ROWS, width), F32)])
    return pl.pallas_call(
        functools.partial(_paged_attn_kernel, n_pages_step=g, masked=masked, page=page),
        grid_spec=grid_spec,
        out_shape=jax.ShapeDtypeStruct((db, DEC_ROWS, LANES), F32),
        compiler_params=_cparams(("parallel", "arbitrary")),
        name="paged_attn_masked" if masked else "paged_attn",
    )(*prefetch, *args)


def _paged_index_kernel(pt_ref, q_ref, w_ref, knew_ref, *refs, n_pages_step, page):
    k_refs = refs[:n_pages_step]
    out_ref = refs[n_pages_step]
    del pt_ref
    j = pl.program_id(1)
    last = pl.num_programs(1) - 1
    q = q_ref[...]
    w = w_ref[...]

    def score(kmat):
        d = lax.dot_general(q, kmat.astype(BF16), NT, preferred_element_type=F32)
        return jnp.sum(jnp.maximum(d, 0.0) * w[:, 0:1], axis=0, keepdims=True)

    @pl.when(j < last)
    def _():
        kcat = jnp.concatenate([r[...] for r in k_refs], axis=0)
        out_ref[...] = _sortable_key(score(kcat))

    @pl.when(j == last)
    def _():
        sc = score(knew_ref[...])
        lane = lax.broadcasted_iota(I32, (1, page), 1)
        sc = jnp.where(lane == 0, sc, -jnp.inf)
        pad = jnp.full((1, (n_pages_step - 1) * page), -jnp.inf, F32)
        full = jnp.concatenate([sc, pad], axis=1) if n_pages_step > 1 else sc
        out_ref[...] = _sortable_key(full)


def _paged_index_scores(qidx, wcol, k_new_tile, idx_cache, page_table):
    db = qidx.shape[0]
    n_pages = page_table.shape[1]
    page = idx_cache.shape[1]
    g = PAGES_PER_STEP
    while n_pages % g:
        g //= 2
    n_steps = n_pages // g

    def page_map(gi):
        return lambda b, j, pt: (pt[b * n_pages + jnp.minimum(j, n_steps - 1) * g + gi], 0, 0)

    grid_spec = pltpu.PrefetchScalarGridSpec(
        num_scalar_prefetch=1,
        grid=(db, n_steps + 1),
        in_specs=[pl.BlockSpec((None, IDX_HEADS, IDX_DIM), lambda b, j, pt: (b, 0, 0)),
                  pl.BlockSpec((None, IDX_HEADS, LANES), lambda b, j, pt: (b, 0, 0)),
                  pl.BlockSpec((None, page, IDX_DIM), lambda b, j, pt: (b, 0, 0))]
        + [pl.BlockSpec((None, page, IDX_DIM), page_map(gi)) for gi in range(g)],
        out_specs=pl.BlockSpec((None, 1, g * page), lambda b, j, pt: (b, 0, j)))
    return pl.pallas_call(
        functools.partial(_paged_index_kernel, n_pages_step=g, page=page),
        grid_spec=grid_spec,
        out_shape=jax.ShapeDtypeStruct((db, 1, (n_steps + 1) * g * page), I32),
        compiler_params=_cparams(("parallel", "arbitrary")),
        name="paged_index_scores",
    )(page_table.reshape(-1), qidx, wcol, k_new_tile, *([idx_cache] * g))


def _select_kernel(keys_ref, tau_ref, jmax_ref, *, k_sel, n_index_bits):
    keys = keys_ref[...]
    idx = lax.broadcasted_iota(I32, keys.shape, 1)

    def count_fn(pred):
        return jnp.sum(jnp.where(pred(keys, idx), 1, 0), axis=1, keepdims=True)

    tau, jmax = _select_threshold(count_fn, k_sel, n_index_bits, (keys.shape[0], 1))
    tau_ref[...] = jnp.broadcast_to(tau, tau_ref.shape)
    jmax_ref[...] = jnp.broadcast_to(jmax, jmax_ref.shape)


def _select_rows(keys2d, k_sel):
    rows, width = keys2d.shape
    n_index_bits = max(1, int(math.ceil(math.log2(width))))
    return pl.pallas_call(
        functools.partial(_select_kernel, k_sel=k_sel, n_index_bits=n_index_bits),
        out_shape=[jax.ShapeDtypeStruct((rows, LANES), I32), jax.ShapeDtypeStruct((rows, LANES), I32)],
        compiler_params=pltpu.CompilerParams(vmem_limit_bytes=VMEM_LIMIT),
        name="select_rows",
    )(keys2d)


def _even_tail_kernel(z_ref, conv_ref, s_ref, oa_ref, convw_ref, gp_ref, normg_ref, lamv_ref, subln_ref,
                      ya_ref, yb_ref, snew_ref, *, lam_init):
    z = z_ref[...]
    hw = B_HEADS * B_DK
    lv = lamv_ref[...]
    lam = (jnp.exp(jnp.sum(lv[0:1] * lv[1:2], axis=1, keepdims=True))
           - jnp.exp(jnp.sum(lv[2:3] * lv[3:4], axis=1, keepdims=True)) + lam_init)
    oa = oa_ref[...]
    for hg in range(A_HEADS):
        o = oa[2 * hg:2 * hg + 1] - lam * oa[2 * hg + 1:2 * hg + 2]
        y = o * lax.rsqrt(jnp.mean(o * o, axis=-1, keepdims=True) + EPS)
        y = (y * subln_ref[...]) * (1.0 - lam_init)
        gate = z[:, EVEN_OFF[3] + hg * A_DV:EVEN_OFF[3] + (hg + 1) * A_DV]
        ya_ref[:, hg * A_DV:(hg + 1) * A_DV] = (y * _silu(gate)).astype(BF16)
    x_new = z[:, EVEN_OFF[4]:EVEN_OFF[4] + B_QKV]
    conv = convw_ref[CONV_W - 1:CONV_W, :] * x_new
    cp = conv_ref[...]
    for i in range(CONV_W - 1):
        conv = conv + convw_ref[i:i + 1, :] * cp[i:i + 1, :]
    act = _silu(conv)
    ab = z[:, EVEN_OFF[6]:EVEN_OFF[6] + LANES]
    gp = gp_ref[...]
    row = lax.broadcasted_iota(I32, (B_DK, B_DV), 0)
    col = lax.broadcasted_iota(I32, (B_DK, B_DV), 1)
    eye = row == col
    for h in range(B_HEADS):
        q = act[:, h * B_DK:(h + 1) * B_DK]
        k = act[:, hw + h * B_DK:hw + (h + 1) * B_DK]
        v = act[:, 2 * hw + h * B_DV:2 * hw + (h + 1) * B_DV]
        q = q * lax.rsqrt(jnp.sum(q * q, axis=-1, keepdims=True) + EPS) * (B_DK ** -0.5)
        k = k * lax.rsqrt(jnp.sum(k * k, axis=-1, keepdims=True) + EPS)
        g = -jnp.exp(gp[0:1, h:h + 1]) * _softplus(ab[:, h:h + 1] + gp[1:2, h:h + 1])
        beta = jax.nn.sigmoid(ab[:, B_HEADS + h:B_HEADS + h + 1])
        eg = jnp.exp(g)
        s0 = s_ref[h]
        kcol = jnp.sum(jnp.where(eye, jnp.broadcast_to(k, (B_DK, B_DK)), 0.0), axis=1, keepdims=True)
        qcol = jnp.sum(jnp.where(eye, jnp.broadcast_to(q, (B_DK, B_DK)), 0.0), axis=1, keepdims=True)
        ks = jnp.sum(kcol * s0, axis=0, keepdims=True)
        qs = jnp.sum(qcol * s0, axis=0, keepdims=True)
        u = beta * (v - eg * ks)
        qk = jnp.sum(q * k, axis=1, keepdims=True)
        o = eg * qs + qk * u
        snew_ref[h] = eg * s0 + kcol * u
        y = o * lax.rsqrt(jnp.mean(o * o, axis=-1, keepdims=True) + EPS) * normg_ref[...]
        gate = z[:, EVEN_OFF[5] + h * B_DV:EVEN_OFF[5] + (h + 1) * B_DV]
        yb_ref[:, h * B_DV:(h + 1) * B_DV] = (y * _silu(gate)).astype(BF16)


def _even_tail(zf_s, conv_prev, s_prev, oa, conv_w, a_log, dt_bias, norm_g, lamv, subln, lam_init):
    db, npad = zf_s.shape
    gp = jnp.zeros((SUBLANES, LANES), F32).at[0, :B_HEADS].set(a_log).at[1, :B_HEADS].set(dt_bias)
    hw = B_HEADS * B_DV
    return pl.pallas_call(
        functools.partial(_even_tail_kernel, lam_init=lam_init),
        grid=(db,),
        in_specs=[pl.BlockSpec((None, 1, npad), lambda b: (b, 0, 0)),
                  pl.BlockSpec((None, CONV_W - 1, B_QKV), lambda b: (b, 0, 0)),
                  pl.BlockSpec((None, B_HEADS, B_DK, B_DV), lambda b: (b, 0, 0, 0)),
                  pl.BlockSpec((None, DEC_ROWS, LANES), lambda b: (b, 0, 0)),
                  pl.BlockSpec((CONV_W, B_QKV), lambda b: (0, 0)),
                  pl.BlockSpec((SUBLANES, LANES), lambda b: (0, 0)),
                  pl.BlockSpec((1, B_DV), lambda b: (0, 0)),
                  pl.BlockSpec((4, A_DH), lambda b: (0, 0)),
                  pl.BlockSpec((1, A_DV), lambda b: (0, 0))],
        out_specs=[pl.BlockSpec((None, 1, A_HEADS * A_DV), lambda b: (b, 0, 0)),
                   pl.BlockSpec((None, 1, hw), lambda b: (b, 0, 0)),
                   pl.BlockSpec((None, B_HEADS, B_DK, B_DV), lambda b: (b, 0, 0, 0))],
        out_shape=[jax.ShapeDtypeStruct((db, 1, A_HEADS * A_DV), BF16),
                   jax.ShapeDtypeStruct((db, 1, hw), BF16),
                   jax.ShapeDtypeStruct((db, B_HEADS, B_DK, B_DV), F32)],
        compiler_params=_cparams(("parallel",)),
        name="even_tail",
    )(zf_s.reshape(db, 1, npad), conv_prev, s_prev, oa, conv_w, gp, norm_g.reshape(1, B_DV), lamv,
      subln.reshape(1, A_DV))


def _odd_tail_kernel(z_ref, oc_ref, y_ref):
    z = z_ref[...]
    oc = oc_ref[...]
    for hg in range(C_HEADS):
        gate = z[:, ODD_OFF[3] + hg * C_DH:ODD_OFF[3] + (hg + 1) * C_DH]
        y_ref[:, hg * C_DH:(hg + 1) * C_DH] = (oc[hg:hg + 1] * _silu(gate)).astype(BF16)


def _odd_tail(zf_s, oc):
    db, npad = zf_s.shape
    return pl.pallas_call(
        _odd_tail_kernel,
        grid=(db,),
        in_specs=[pl.BlockSpec((None, 1, npad), lambda b: (b, 0, 0)),
                  pl.BlockSpec((None, DEC_ROWS, LANES), lambda b: (b, 0, 0))],
        out_specs=pl.BlockSpec((None, 1, C_HEADS * C_DH), lambda b: (b, 0, 0)),
        out_shape=jax.ShapeDtypeStruct((db, 1, C_HEADS * C_DH), BF16),
        compiler_params=_cparams(("parallel",)),
        name="odd_tail",
    )(zf_s.reshape(db, 1, npad), oc)


def _bias_by_distance(table):
    n = jnp.arange(FAR_DIST + 1)
    exact = N_BUCKETS // 2
    nf = jnp.maximum(n, 1).astype(F32)
    large = exact + (jnp.log(nf / exact) / math.log(MAX_DIST / exact) * (N_BUCKETS - exact)).astype(I32)
    bucket = jnp.where(n < exact, n, jnp.minimum(large, N_BUCKETS - 1))
    return table[bucket].astype(F32)


def _prompt_bias_tiles(bd, group):
    r = jnp.arange(BLK)[:, None]
    c = jnp.arange(BLK)[None, :]
    d0 = r - c
    t_far = jnp.broadcast_to(bd[FAR_DIST][None, None, :], (BLK, BLK, bd.shape[1]))
    t_prev = bd[jnp.minimum(BLK + d0, FAR_DIST)]
    t_diag = jnp.where((d0 >= 0)[..., None], bd[jnp.clip(d0, 0, FAR_DIST)], NEG)
    tiles = jnp.stack([t_far, t_prev, t_diag])
    tiles = jnp.transpose(tiles, (3, 0, 1, 2))
    n_kv = bd.shape[1] // group
    tiles = tiles.reshape(n_kv, group, 3, BLK, BLK)
    return jnp.transpose(tiles, (0, 2, 1, 3, 4)).reshape(n_kv, 3, group * BLK, BLK)


def _decode_bias(bd, row_heads, past, page, g):
    heads = jnp.asarray(row_heads, I32)
    far = jnp.broadcast_to(bd[FAR_DIST][heads][:, None], (len(row_heads), g * page))
    pos = past - g * page + jnp.arange(g * page)
    dist = jnp.minimum(past - pos, FAR_DIST)
    last = bd[dist][:, heads].T
    new = jnp.broadcast_to(bd[0][heads][:, None], (len(row_heads), LANES))
    return far.astype(F32), last.astype(F32), new.astype(F32)


def _pad_cols(w, mult):
    n = w.shape[1]
    return jnp.pad(w, ((0, 0), (0, _round_up(n, mult) - n)))


def kernel(x_prompt, x_sample, cache_a_k, cache_a_v, state_b_s, state_b_conv, cache_c_k, cache_c_v, cache_c_idx,
           page_table, meta, bias_table, final_norm, norm_e, w_in_e, w_out_e, lam_q1, lam_k1, lam_q2, lam_k2,
           subln_a, conv_b, a_log_b, dt_bias_b, norm_b, norm_o, w_in_o, w_out_o):
    nb, seq, d = x_prompt.shape
    n_meta = meta.shape[0]
    l = seq + n_meta
    lp = _round_up(l, BLK)
    db = x_sample.shape[0]
    n_pages = page_table.shape[1]
    page = cache_a_k.shape[2]
    past = n_pages * page
    n_pool = cache_a_k.shape[1]
    assert x_sample.shape[1] == 1 and norm_e.shape[0] == 1 and norm_o.shape[0] == 1
    lam_init = 0.8 - 0.6 * math.exp(-0.3 * 0)

    tn = 256
    w_e = _pad_cols(w_in_e[0], tn).astype(BF16)
    w_o = _pad_cols(w_in_o[0], tn).astype(BF16)
    w_out_a = w_out_e[0][:A_HEADS * A_DV].astype(BF16)
    w_out_b = w_out_e[0][A_HEADS * A_DV:].astype(BF16)
    w_out_c = w_out_o[0].astype(BF16)
    tm = next(t for t in (512, 256, BLK) if (nb * lp) % t == 0)
    dbp = _round_up(db, SUBLANES)

    bd = _bias_by_distance(bias_table)
    tiles_a = _prompt_bias_tiles(bd, A_GROUP)
    tiles_c = _prompt_bias_tiles(bd, C_GROUP)
    lamv = jnp.stack([lam_q1[0], lam_k1[0], lam_q2[0], lam_k2[0]]).astype(F32)

    hp = jnp.concatenate([jnp.broadcast_to(meta.astype(F32)[None], (nb, n_meta, d)), x_prompt], axis=1)
    hp = jnp.pad(hp, ((0, 0), (0, lp - l), (0, 0))).reshape(nb * lp, d)
    zf, zb = _norm_proj(hp, norm_e[0], w_e, tm, tn)
    ya = _diff_attn_prompt(zf, zb, tiles_a, lamv, subln_a[0], nb, lp, lam_init)
    yb, pb_s = _gdn_prompt(zf, conv_b[0], a_log_b[0], dt_bias_b[0], norm_b[0], nb, lp, l)
    h1 = _out_proj([ya, yb], [w_out_a, w_out_b], hp, tm)
    z3 = zf.reshape(nb, lp, -1)
    pa_k = z3[:, :l, EVEN_OFF[1]:EVEN_OFF[2]].reshape(1, nb, l, A_KV, 2, A_DH)
    pa_v = z3[:, :l, EVEN_OFF[2]:EVEN_OFF[3]].reshape(1, nb, l, A_KV, A_DV)
    pb_conv = z3[:, l - (CONV_W - 1):l, EVEN_OFF[4]:EVEN_OFF[5]][None]

    zf1, zb1 = _norm_proj(h1, norm_o[0], w_o, tm, tn)
    k_sel_p = min(TOPK_MAX, l // 4)
    yc = _sparse_attn_prompt(zf1, zb1, tiles_c, nb, lp, k_sel_p)
    yp = _out_proj([yc], [w_out_c], h1, tm, final_gain=final_norm)
    y_prompt = yp.reshape(nb, lp, d)[:, n_meta:l]
    z13 = zf1.reshape(nb, lp, -1)
    pc_k = z13[:, :l, ODD_OFF[1]:ODD_OFF[2]].reshape(1, nb, l, C_KV, C_DH)
    pc_v = z13[:, :l, ODD_OFF[2]:ODD_OFF[3]].reshape(1, nb, l, C_KV, C_DH)
    pc_idx = z13[:, :l, ODD_OFF[5]:ODD_OFF[6]][None]

    hs = jnp.pad(x_sample.reshape(db, d), ((0, dbp - db), (0, 0)))
    zs, _ = _norm_proj(hs, norm_e[0], w_e, dbp, tn)
    zs = zs[:db]
    qa = zs[:, :EVEN_OFF[1]].reshape(db, A_KV, A_GROUP, 2, A_DH) * (A_DH ** -0.5)
    qprime = jnp.einsum('bhgcd,hi,cj->bhgcijd', qa, jnp.eye(A_KV, dtype=F32), jnp.eye(2, dtype=F32))
    qprime = qprime.reshape(db, DEC_ROWS, A_KV * 2 * A_DH).astype(BF16)
    k_new = zs[:, EVEN_OFF[1]:EVEN_OFF[2]].reshape(db, 1, -1)
    v_new = zs[:, EVEN_OFF[2]:EVEN_OFF[3]].reshape(db, 1, -1)
    g_dec = PAGES_PER_STEP
    while n_pages % g_dec:
        g_dec //= 2
    rows_a = [r // 2 for r in range(DEC_ROWS)]
    bfar, blast, bnew = _decode_bias(bd, rows_a, past, page, g_dec)
    oa = _paged_attention(qprime, k_new, v_new, bfar, blast, bnew,
                          cache_a_k.reshape(n_pool, page, -1), cache_a_v.reshape(n_pool, page, -1), page_table)
    ya_s, yb_s, sb_s = _even_tail(zs, state_b_conv[0], state_b_s[0], oa, conv_b[0], a_log_b[0], dt_bias_b[0],
                                  norm_b[0], lamv, subln_a[0], lam_init)
    hs_pad = lambda y: jnp.pad(y.reshape(db, -1), ((0, dbp - db), (0, 0)))
    hs1 = _out_proj([hs_pad(ya_s), hs_pad(yb_s)], [w_out_a, w_out_b], hs, dbp)
    sa_k = zs[:, EVEN_OFF[1]:EVEN_OFF[2]].reshape(1, db, 1, A_KV, 2, A_DH)
    sa_v = zs[:, EVEN_OFF[2]:EVEN_OFF[3]].reshape(1, db, 1, A_KV, A_DV)
    sb_conv = jnp.concatenate([state_b_conv[0][:, 1:], zs[:, None, EVEN_OFF[4]:EVEN_OFF[5]]], axis=1)[None]

    zs1, _ = _norm_proj(hs1, norm_o[0], w_o, dbp, tn)
    zs1 = zs1[:db]
    qidx = (zs1[:, ODD_OFF[4]:ODD_OFF[5]].reshape(db, IDX_HEADS, IDX_DIM) * (IDX_DIM ** -0.5)).astype(BF16)
    wcol = jnp.broadcast_to((zs1[:, ODD_OFF[6]:ODD_OFF[7]] * (IDX_HEADS ** -0.5))[:, :, None],
                            (db, IDX_HEADS, LANES))
    ki_new = zs1[:, ODD_OFF[5]:ODD_OFF[6]]
    ki_tile = jnp.zeros((db, page, IDX_DIM), F32).at[:, 0].set(ki_new)
    keys = _paged_index_scores(qidx, wcol, ki_tile, cache_c_idx.reshape(n_pool, page, IDX_DIM), page_table)
    k_sel_s = min(TOPK_MAX, (past + 1) // 4)
    tau, jmax = _select_rows(keys.reshape(db, -1), k_sel_s)
    tau, jmax = tau[:, 0], jmax[:, 0]
    key_new = keys[:, 0, past]
    selnew = ((key_new > tau) | ((key_new == tau) & (past <= jmax))).astype(I32)
    qc = zs1[:, :ODD_OFF[1]].reshape(db, C_KV, C_GROUP, C_DH) * (C_DH ** -0.5)
    qcp = jnp.einsum('bhgd,hi->bhgid', qc, jnp.eye(C_KV, dtype=F32)).reshape(db, C_HEADS, C_KV * C_DH)
    qcp = jnp.pad(qcp, ((0, 0), (0, DEC_ROWS - C_HEADS), (0, 0))).astype(BF16)
    kc_new = zs1[:, ODD_OFF[1]:ODD_OFF[2]].reshape(db, 1, -1)
    vc_new = zs1[:, ODD_OFF[2]:ODD_OFF[3]].reshape(db, 1, -1)
    rows_c = [r if r < C_HEADS else 0 for r in range(DEC_ROWS)]
    cfar, clast, cnew = _decode_bias(bd, rows_c, past, page, g_dec)
    oc = _paged_attention(qcp, kc_new, vc_new, cfar, clast, cnew,
                          cache_c_k.reshape(n_pool, page, -1), cache_c_v.reshape(n_pool, page, -1), page_table,
                          mask_args=(keys, tau, jmax, selnew))
    yc_s = _odd_tail(zs1, oc)
    ys = _out_proj([hs_pad(yc_s)], [w_out_c], hs1, dbp, final_gain=final_norm)
    y_sample = ys[:db].reshape(db, 1, d)
    sc_k = zs1[:, ODD_OFF[1]:ODD_OFF[2]].reshape(1, db, 1, C_KV, C_DH)
    sc_v = zs1[:, ODD_OFF[2]:ODD_OFF[3]].reshape(1, db, 1, C_KV, C_DH)
    sc_idx = zs1[:, None, ODD_OFF[5]:ODD_OFF[6]][None]

    return (y_prompt, y_sample, pa_k, pa_v, pb_s[None], pb_conv, pc_k, pc_v, pc_idx,
            sa_k, sa_v, sb_s[None], sb_conv, sc_k, sc_v, sc_idx)
```

```python
import functools
import math

import jax
import jax.numpy as jnp
import numpy as np
from jax import lax
from jax.experimental import pallas as pl
from jax.experimental.pallas import tpu as pltpu

F32 = jnp.float32
BF16 = jnp.bfloat16
I32 = jnp.int32

EPS = 1e-6
N_BUCKETS = 32
MAX_DIST = 128
FAR_DIST = MAX_DIST

A_HEADS, A_KV, A_GROUP, A_DH, A_DV = 8, 2, 4, 64, 128
B_HEADS, B_DK, B_DV, CONV_W = 4, 128, 128, 4
C_HEADS, C_KV, C_GROUP, C_DH = 8, 2, 4, 128
IDX_HEADS, IDX_DIM, TOPK_MAX = 8, 64, 256
B_QKV = 2 * B_HEADS * B_DK + B_HEADS * B_DV

EVEN_COLS = (A_HEADS * 2 * A_DH, A_KV * 2 * A_DH, A_KV * A_DV, A_HEADS * A_DV, B_QKV, B_HEADS * B_DV, B_HEADS, B_HEADS)
ODD_COLS = (C_HEADS * C_DH, C_KV * C_DH, C_KV * C_DH, C_HEADS * C_DH, IDX_HEADS * IDX_DIM, IDX_DIM, IDX_HEADS)
EVEN_OFF = tuple(int(v) for v in np.cumsum((0,) + EVEN_COLS))
ODD_OFF = tuple(int(v) for v in np.cumsum((0,) + ODD_COLS))

LANES = 128
SUBLANES = 8
BLK = 128
CHUNK = 512
SUBS = CHUNK // BLK
NEG = -1e30
INT_MIN = -2 ** 31
INT_MAX = 2 ** 31 - 1
LOG2E = math.log2(math.e)
VMEM_LIMIT = 56 * 1024 * 1024
HI = lax.Precision.HIGHEST

NT = (((1,), (1,)), ((), ()))


def _round_up(x, m):
    return (x + m - 1) // m * m


def _cparams(sem):
    return pltpu.CompilerParams(dimension_semantics=sem, vmem_limit_bytes=VMEM_LIMIT)


def _silu(x):
    return x * jax.nn.sigmoid(x)


def _sortable_key(score):
    score = jnp.where(score == 0.0, 0.0, score)
    bits = pltpu.bitcast(score, I32)
    return bits ^ ((bits >> 31) & jnp.int32(0x7FFFFFFF))


def _lane_tile(x, n):
    return jnp.concatenate([x] * n, axis=1)


def _proj_kernel(x_ref, g_ref, w_ref, of_ref, ob_ref, xn_ref):
    @pl.when(pl.program_id(1) == 0)
    def _():
        xf = x_ref[...]
        y = xf * lax.rsqrt(jnp.mean(xf * xf, axis=-1, keepdims=True) + EPS)
        xn_ref[...] = (y * g_ref[...]).astype(BF16)

    acc = jnp.dot(xn_ref[...], w_ref[...], preferred_element_type=F32)
    of_ref[...] = acc
    ob_ref[...] = acc.astype(BF16)


def _norm_proj(x, gain, w_bf16, tm, tn):
    m, d = x.shape
    n = w_bf16.shape[1]
    return pl.pallas_call(
        _proj_kernel,
        grid=(m // tm, n // tn),
        in_specs=[pl.BlockSpec((tm, d), lambda i, j: (i, 0)),
                  pl.BlockSpec((1, d), lambda i, j: (0, 0)),
                  pl.BlockSpec((d, tn), lambda i, j: (0, j))],
        out_specs=[pl.BlockSpec((tm, tn), lambda i, j: (i, j)),
                   pl.BlockSpec((tm, tn), lambda i, j: (i, j))],
        out_shape=[jax.ShapeDtypeStruct((m, n), F32), jax.ShapeDtypeStruct((m, n), BF16)],
        scratch_shapes=[pltpu.VMEM((tm, d), BF16)],
        compiler_params=_cparams(("parallel", "arbitrary")),
        name="norm_proj",
    )(x, gain.reshape(1, d), w_bf16)


def _outproj_kernel(*refs, n_lhs, final_norm):
    ys = refs[:n_lhs]
    ws = refs[n_lhs:2 * n_lhs]
    h_ref = refs[2 * n_lhs]
    pos = 2 * n_lhs + 1
    fn_ref = refs[pos] if final_norm else None
    out_ref = refs[-1]
    acc = h_ref[...]
    for y_ref, w_ref in zip(ys, ws):
        acc = acc + jnp.dot(y_ref[...], w_ref[...], preferred_element_type=F32)
    if final_norm:
        y = acc * lax.rsqrt(jnp.mean(acc * acc, axis=-1, keepdims=True) + EPS)
        acc = y * fn_ref[...]
    out_ref[...] = acc


def _out_proj(ys, ws, h, tm, final_gain=None):
    m, n = h.shape
    n_lhs = len(ys)
    in_specs = [pl.BlockSpec((tm, y.shape[1]), lambda i: (i, 0)) for y in ys]
    in_specs += [pl.BlockSpec(w.shape, lambda i: (0, 0)) for w in ws]
    in_specs += [pl.BlockSpec((tm, n), lambda i: (i, 0))]
    args = list(ys) + list(ws) + [h]
    if final_gain is not None:
        in_specs += [pl.BlockSpec((1, n), lambda i: (0, 0))]
        args += [final_gain.reshape(1, n)]
    return pl.pallas_call(
        functools.partial(_outproj_kernel, n_lhs=n_lhs, final_norm=final_gain is not None),
        grid=(m // tm,),
        in_specs=in_specs,
        out_specs=pl.BlockSpec((tm, n), lambda i: (i, 0)),
        out_shape=jax.ShapeDtypeStruct((m, n), F32),
        compiler_params=_cparams(("parallel",)),
        name="out_proj",
    )(*args)


def _flash_update(s, v1, m_ref, acc_ref, idx):
    m_old = m_ref[idx]
    m_new = jnp.maximum(m_old, jnp.max(s, axis=1, keepdims=True))
    alpha = jnp.exp2(m_old - m_new)
    p = jnp.exp2(s - _lane_tile(m_new, s.shape[1] // LANES))
    acc_ref[idx] = _lane_tile(alpha, 2) * acc_ref[idx] + jnp.dot(p.astype(BF16), v1, preferred_element_type=F32)
    m_ref[idx] = m_new


def _build_near_bias(strip_ref, t_prev, t_diag, r, lead=()):
    for sb in range(2 * SUBS):
        rel = sb - SUBS - r
        tile = jnp.where(rel == 0, t_diag, jnp.where(rel == -1, t_prev, jnp.where(rel < -1, 0.0, NEG)))
        strip_ref[lead + (sb,)] = tile


def _diff_attn_kernel(q_ref, k_ref, v_ref, gate_ref, tiles_ref, lamv_ref, subln_ref, out_ref,
                      qm_ref, strip_ref, m_ref, acc_ref, *, lam_init, nq_real):
    qi = pl.program_id(2)

    @pl.when(qi >= nq_real)
    def _():
        out_ref[...] = jnp.zeros(out_ref.shape, out_ref.dtype)

    @pl.when(qi < nq_real)
    def _():
        lane = lax.broadcasted_iota(I32, (BLK, 2 * A_DH), 1)
        for g in range(A_GROUP):
            qg = q_ref[:, g * 2 * A_DH:(g + 1) * 2 * A_DH] * (A_DH ** -0.5 * LOG2E)
            for c in range(2):
                keep = (lane < A_DH) if c == 0 else (lane >= A_DH)
                qm_ref[c, g * BLK:(g + 1) * BLK, :] = jnp.where(keep, qg, 0.0).astype(BF16)
        m_ref[...] = jnp.full(m_ref.shape, NEG, F32)
        acc_ref[...] = jnp.zeros(acc_ref.shape, F32)
        r = jnp.bitwise_and(qi, SUBS - 1)
        jd = lax.shift_right_logical(qi, int(math.log2(SUBS)))
        _build_near_bias(strip_ref, tiles_ref[0], tiles_ref[1], r)
        ones = jnp.ones((CHUNK, A_DV), BF16)

        def chunk(j, first_sb):
            start = pl.multiple_of(j * CHUNK, CHUNK)
            k = k_ref[pl.ds(start, CHUNK), :]
            v1 = jnp.concatenate([v_ref[pl.ds(start, CHUNK), :], ones], axis=1)
            for c in range(2):
                s = lax.dot_general(qm_ref[c], k, NT, preferred_element_type=F32)
                if first_sb is not None:
                    s = s + jnp.concatenate([strip_ref[first_sb + i] for i in range(SUBS)], axis=1)
                _flash_update(s, v1, m_ref, acc_ref, c)

        def far(j, carry):
            chunk(j, None)
            return carry

        lax.fori_loop(0, jnp.maximum(jd - 1, 0), far, 0)

        @pl.when(jd >= 1)
        def _():
            chunk(jd - 1, 0)

        chunk(jd, SUBS)

        lv = lamv_ref[...]
        lam = (jnp.exp(jnp.sum(lv[0:1] * lv[1:2], axis=1, keepdims=True))
               - jnp.exp(jnp.sum(lv[2:3] * lv[3:4], axis=1, keepdims=True)) + lam_init)
        a0 = acc_ref[0]
        a1 = acc_ref[1]
        o = a0[:, :A_DV] / a0[:, A_DV:] - lam * (a1[:, :A_DV] / a1[:, A_DV:])
        y = o * lax.rsqrt(jnp.mean(o * o, axis=-1, keepdims=True) + EPS)
        y = (y * subln_ref[...]) * (1.0 - lam_init)
        for g in range(A_GROUP):
            gate = gate_ref[:, g * A_DV:(g + 1) * A_DV]
            out_ref[:, g * A_DV:(g + 1) * A_DV] = (y[g * BLK:(g + 1) * BLK] * _silu(gate)).astype(BF16)


def _diff_attn_prompt(zf, zb, near_tiles, lamv, subln, nb, lp, nq_real, lam_init):
    nq = lp // BLK
    qw = A_GROUP * 2 * A_DH
    k_blk0 = EVEN_OFF[1] // (2 * A_DH)
    v_blk0 = EVEN_OFF[2] // A_DV
    g_blk0 = EVEN_OFF[3] // (A_GROUP * A_DV)
    rows = A_GROUP * BLK
    return pl.pallas_call(
        functools.partial(_diff_attn_kernel, lam_init=lam_init, nq_real=nq_real),
        grid=(nb, A_KV, nq),
        in_specs=[pl.BlockSpec((BLK, qw), lambda b, h, i: (b * nq + i, h)),
                  pl.BlockSpec((lp, 2 * A_DH), lambda b, h, i: (b, k_blk0 + h)),
                  pl.BlockSpec((lp, A_DV), lambda b, h, i: (b, v_blk0 + h)),
                  pl.BlockSpec((BLK, A_GROUP * A_DV), lambda b, h, i: (b * nq + i, g_blk0 + h)),
                  pl.BlockSpec((None, 2, rows, BLK), lambda b, h, i: (h, 0, 0, 0)),
                  pl.BlockSpec((4, A_DH), lambda b, h, i: (0, 0)),
                  pl.BlockSpec((1, A_DV), lambda b, h, i: (0, 0))],
        out_specs=pl.BlockSpec((BLK, A_GROUP * A_DV), lambda b, h, i: (b * nq + i, h)),
        out_shape=jax.ShapeDtypeStruct((nb * lp, A_HEADS * A_DV), BF16),
        scratch_shapes=[pltpu.VMEM((2, rows, 2 * A_DH), BF16),
                        pltpu.VMEM((2 * SUBS, rows, BLK), F32),
                        pltpu.VMEM((2, rows, LANES), F32),
                        pltpu.VMEM((2, rows, 2 * A_DV), F32)],
        compiler_params=_cparams(("parallel", "parallel", "arbitrary")),
        name="diff_attn_prompt",
    )(zf, zb, zb, zf, near_tiles, lamv, subln.reshape(1, A_DV))


def _softplus(x):
    return jnp.maximum(x, 0.0) + jnp.log1p(jnp.exp(-jnp.abs(x)))


def _gdn_chunk_math(s0, q, k, v, gb, betab, row_ge, row_gt, eye):
    c = q.shape[0]
    ltri = jnp.where(row_ge, 1.0, 0.0).astype(F32)
    gcum = jnp.dot(ltri, gb, precision=HI, preferred_element_type=F32)
    diff = gcum - gcum.T
    decay = jnp.where(row_ge, jnp.exp(jnp.where(row_ge, diff, 0.0)), 0.0)
    kk = lax.dot_general(k, k, NT, precision=HI, preferred_element_type=F32)
    x = -jnp.where(row_gt, betab * decay * kk, 0.0)
    t = eye + x
    p = x
    for _ in range(int(math.log2(c)) - 1):
        p = jnp.dot(p, p, precision=HI, preferred_element_type=F32)
        t = t + jnp.dot(t, p, precision=HI, preferred_element_type=F32)
    eg = jnp.exp(gcum)
    ks = jnp.dot(k, s0, precision=HI, preferred_element_type=F32)
    rhs = betab * (v - eg * ks)
    u = jnp.dot(t, rhs, precision=HI, preferred_element_type=F32)
    qk = lax.dot_general(q, k, NT, precision=HI, preferred_element_type=F32) * decay
    o = eg * jnp.dot(q, s0, precision=HI, preferred_element_type=F32) + jnp.dot(
        qk, u, precision=HI, preferred_element_type=F32)
    glast = gcum[c - 1:c, :]
    kd = k * jnp.exp(glast - gcum)
    s_new = jnp.exp(glast) * s0 + jnp.dot(kd.T, u, precision=HI, preferred_element_type=F32)
    return s_new, o


def _gdn_prompt_kernel(xq_ref, xk_ref, xv_ref, gate_ref, ab_ref, convw_ref, gp_ref, normg_ref,
                       y_ref, sfin_ref, xbuf_ref, s_ref, *, seq_len):
    ci = pl.program_id(1)
    hw = B_HEADS * B_DK

    @pl.when(ci == 0)
    def _():
        xbuf_ref[0:SUBLANES, :] = jnp.zeros((SUBLANES, B_QKV), F32)
        s_ref[...] = jnp.zeros(s_ref.shape, F32)

    xbuf_ref[SUBLANES:SUBLANES + BLK, 0:hw] = xq_ref[...]
    xbuf_ref[SUBLANES:SUBLANES + BLK, hw:2 * hw] = xk_ref[...]
    xbuf_ref[SUBLANES:SUBLANES + BLK, 2 * hw:3 * hw] = xv_ref[...]
    conv = jnp.zeros((BLK, B_QKV), F32)
    for i in range(CONV_W):
        conv = conv + convw_ref[i:i + 1, :] * xbuf_ref[pl.ds(SUBLANES - (CONV_W - 1) + i, BLK), :]
    tail = xbuf_ref[BLK:BLK + SUBLANES, :]
    xbuf_ref[0:SUBLANES, :] = tail
    act = _silu(conv)

    row = lax.broadcasted_iota(I32, (BLK, BLK), 0)
    col = lax.broadcasted_iota(I32, (BLK, BLK), 1)
    row_ge = row >= col
    row_gt = row > col
    eye = jnp.where(row == col, 1.0, 0.0).astype(F32)
    valid = (ci * BLK + row) < seq_len
    ab = ab_ref[...]
    gp = gp_ref[...]
    for h in range(B_HEADS):
        q = act[:, h * B_DK:(h + 1) * B_DK]
        k = act[:, hw + h * B_DK:hw + (h + 1) * B_DK]
        v = act[:, 2 * hw + h * B_DV:2 * hw + (h + 1) * B_DV]
        q = q * lax.rsqrt(jnp.sum(q * q, axis=-1, keepdims=True) + EPS) * (B_DK ** -0.5)
        k = k * lax.rsqrt(jnp.sum(k * k, axis=-1, keepdims=True) + EPS)
        a_raw = jnp.broadcast_to(ab[:, h:h + 1], (BLK, BLK))
        b_raw = jnp.broadcast_to(ab[:, B_HEADS + h:B_HEADS + h + 1], (BLK, BLK))
        a_log = gp[0:1, h:h + 1]
        dt_b = gp[1:2, h:h + 1]
        gb = jnp.where(valid, -jnp.exp(a_log) * _softplus(a_raw + dt_b), 0.0)
        betab = jnp.where(valid, jax.nn.sigmoid(b_raw), 0.0)
        s_new, o = _gdn_chunk_math(s_ref[h], q, k, v, gb, betab, row_ge, row_gt, eye)
        s_ref[h] = s_new
        y = o * lax.rsqrt(jnp.mean(o * o, axis=-1, keepdims=True) + EPS) * normg_ref[...]
        y_ref[:, h * B_DV:(h + 1) * B_DV] = (y * _silu(gate_ref[:, h * B_DV:(h + 1) * B_DV])).astype(BF16)

    @pl.when(ci == pl.num_programs(1) - 1)
    def _():
        sfin_ref[...] = s_ref[...]


def _gdn_prompt(zf, conv_w, a_log, dt_bias, norm_g, nb, lp, seq_len):
    nc = lp // BLK
    hw = B_HEADS * B_DK
    c0 = EVEN_OFF[4] // hw
    g0 = EVEN_OFF[5] // hw
    ab0 = EVEN_OFF[6] // LANES
    gp = jnp.zeros((SUBLANES, LANES), F32).at[0, :B_HEADS].set(a_log).at[1, :B_HEADS].set(dt_bias)
    return pl.pallas_call(
        functools.partial(_gdn_prompt_kernel, seq_len=seq_len),
        grid=(nb, nc),
        in_specs=[pl.BlockSpec((BLK, hw), lambda b, i: (b * nc + i, c0)),
                  pl.BlockSpec((BLK, hw), lambda b, i: (b * nc + i, c0 + 1)),
                  pl.BlockSpec((BLK, hw), lambda b, i: (b * nc + i, c0 + 2)),
                  pl.BlockSpec((BLK, hw), lambda b, i: (b * nc + i, g0)),
                  pl.BlockSpec((BLK, LANES), lambda b, i: (b * nc + i, ab0)),
                  pl.BlockSpec((CONV_W, B_QKV), lambda b, i: (0, 0)),
                  pl.BlockSpec((SUBLANES, LANES), lambda b, i: (0, 0)),
                  pl.BlockSpec((1, B_DV), lambda b, i: (0, 0))],
        out_specs=[pl.BlockSpec((BLK, hw), lambda b, i: (b * nc + i, 0)),
                   pl.BlockSpec((None, B_HEADS, B_DK, B_DV), lambda b, i: (b, 0, 0, 0))],
        out_shape=[jax.ShapeDtypeStruct((nb * lp, hw), BF16),
                   jax.ShapeDtypeStruct((nb, B_HEADS, B_DK, B_DV), F32)],
        scratch_shapes=[pltpu.VMEM((BLK + SUBLANES, B_QKV), F32),
                        pltpu.VMEM((B_HEADS, B_DK, B_DV), F32)],
        compiler_params=_cparams(("parallel", "arbitrary")),
        name="gdn_prompt",
    )(zf, zf, zf, zf, zf, conv_w, gp, norm_g.reshape(1, B_DV))


def _kth_largest(count_fn, k_sel, shape):
    def step(i, tau):
        cand = tau + lax.shift_left(jnp.int32(1), 31 - i)
        cnt = count_fn(lambda key, idx: key >= cand)
        return jnp.where(cnt >= k_sel, cand, tau)

    return lax.fori_loop(0, 32, step, jnp.full(shape, INT_MIN, I32))


def _tie_cutoff(count_fn, tau, need, n_index_bits, shape):
    def step(i, jm):
        cand = jm | lax.shift_left(jnp.int32(1), n_index_bits - 1 - i)
        cnt = count_fn(lambda key, idx: (key == tau) & (idx < cand))
        return jnp.where(cnt < need, cand, jm)

    return lax.fori_loop(0, n_index_bits, step, jnp.zeros(shape, I32))


def _sparse_attn_kernel(qi_ref, kw_ref, kidx_ref, qc_ref, kc_ref, vc_ref, g0_ref, g1_ref, tiles_ref, out_ref,
                        keys_ref, qm_ref, wb_ref, qh_ref, strip_ref, tau_ref, jmax_ref, m_ref, acc_ref,
                        *, k_sel, n_index_bits, nq_real):
    qb = pl.program_id(1)

    @pl.when(qb >= nq_real)
    def _():
        out_ref[...] = jnp.zeros(out_ref.shape, out_ref.dtype)

    @pl.when(qb < nq_real)
    def _():
        lane = lax.broadcasted_iota(I32, (BLK, LANES), 1)
        lane_c = lax.broadcasted_iota(I32, (BLK, CHUNK), 1)
        row_c = lax.broadcasted_iota(I32, (BLK, CHUNK), 0)
        r = jnp.bitwise_and(qb, SUBS - 1)
        jd = lax.shift_right_logical(qb, int(math.log2(SUBS)))

        kw = kw_ref[...]
        for p in range(IDX_HEADS // 2):
            pair = qi_ref[:, p * LANES:(p + 1) * LANES] * (IDX_DIM ** -0.5)
            swapped = pltpu.roll(pair, IDX_DIM, 1)
            for e, src in enumerate((pair, swapped)):
                h = 2 * p + e
                qm_ref[h * BLK:(h + 1) * BLK, :] = jnp.where(lane < IDX_DIM, src, 0.0).astype(BF16)
                wcol = kw[:, IDX_DIM + h:IDX_DIM + h + 1] * (IDX_HEADS ** -0.5)
                wb_ref[h] = jnp.broadcast_to(wcol, (BLK, LANES))

        def scores(j):
            start = pl.multiple_of(j * CHUNK, CHUNK)
            d = lax.dot_general(qm_ref[...], kidx_ref[pl.ds(start, CHUNK), :], NT, preferred_element_type=F32)
            sc = jnp.zeros((BLK, CHUNK), F32)
            for h in range(IDX_HEADS):
                sc = sc + _lane_tile(wb_ref[h], SUBS) * jnp.maximum(d[h * BLK:(h + 1) * BLK], 0.0)
            return sc

        def score_body(j, carry):
            keys_ref[j] = _sortable_key(scores(j))
            return carry

        lax.fori_loop(0, jd, score_body, 0)
        admissible = (jd * CHUNK + lane_c) <= (qb * BLK + row_c)
        keys_ref[jd] = _sortable_key(jnp.where(admissible, scores(jd), -jnp.inf))

        def count_fn(pred):
            def body(j, acc):
                key = keys_ref[j]
                for i in range(SUBS):
                    hit = pred(key[:, i * LANES:(i + 1) * LANES], j * CHUNK + i * LANES + lane)
                    acc = acc + jnp.where(hit, 1, 0)
                return acc
            acc = lax.fori_loop(0, jd + 1, body, jnp.zeros((BLK, LANES), I32))
            return jnp.sum(acc, axis=1, keepdims=True)

        tau = _kth_largest(count_fn, k_sel, (BLK, LANES))
        tau_ref[...] = tau
        jmax_ref[...] = jnp.full((BLK, LANES), INT_MAX, I32)
        need = k_sel - count_fn(lambda key, idx: key > tau)
        n_eq = count_fn(lambda key, idx: key == tau)
        excess = jnp.max(jnp.where(n_eq > need, 1.0, 0.0)) > 0.5

        @pl.when(excess)
        def _():
            jmax_ref[...] = _tie_cutoff(count_fn, tau, need, n_index_bits, (BLK, LANES))

        tau_c = _lane_tile(tau_ref[...], SUBS)

        for h in range(C_KV):
            for g in range(C_GROUP):
                col = (h * C_GROUP + g) * C_DH
                qh_ref[h, g * BLK:(g + 1) * BLK, :] = (
                    qc_ref[:, col:col + C_DH] * (C_DH ** -0.5 * LOG2E)).astype(BF16)
            _build_near_bias(strip_ref, tiles_ref[h, 0], tiles_ref[h, 1], r, lead=(h,))
        m_ref[...] = jnp.full(m_ref.shape, NEG, F32)
        acc_ref[...] = jnp.zeros(acc_ref.shape, F32)
        ones = jnp.ones((CHUNK, C_DH), BF16)
        jmax_c = _lane_tile(jmax_ref[...], SUBS)

        def chunk(j, first_sb):
            start = pl.multiple_of(j * CHUNK, CHUNK)
            key = keys_ref[j]
            idx = j * CHUNK + lane_c
            tie_ok = jnp.where(idx <= jmax_c, 0.0, NEG)
            selneg = jnp.where(key > tau_c, 0.0, jnp.where(key == tau_c, tie_ok, NEG))
            selneg = jnp.concatenate([selneg] * C_GROUP, axis=0)
            for h in range(C_KV):
                k = kc_ref[pl.ds(start, CHUNK), h * C_DH:(h + 1) * C_DH]
                v1 = jnp.concatenate([vc_ref[pl.ds(start, CHUNK), h * C_DH:(h + 1) * C_DH], ones], axis=1)
                s = lax.dot_general(qh_ref[h], k, NT, preferred_element_type=F32) + selneg
                if first_sb is not None:
                    s = s + jnp.concatenate([strip_ref[h, first_sb + i] for i in range(SUBS)], axis=1)
                _flash_update(s, v1, m_ref, acc_ref, h)

        def far(j, carry):
            chunk(j, None)
            return carry

        lax.fori_loop(0, jnp.maximum(jd - 1, 0), far, 0)

        @pl.when(jd >= 1)
        def _():
            chunk(jd - 1, 0)

        chunk(jd, SUBS)

        half = (C_HEADS // 2) * C_DH
        for h in range(C_KV):
            a = acc_ref[h]
            o = a[:, :C_DH] / a[:, C_DH:]
            gref = g0_ref if h == 0 else g1_ref
            for g in range(C_GROUP):
                gate = gref[:, g * C_DH:(g + 1) * C_DH]
                out_ref[:, h * half + g * C_DH:h * half + (g + 1) * C_DH] = (
                    o[g * BLK:(g + 1) * BLK] * _silu(gate)).astype(BF16)


def _sparse_attn_prompt(zf, zb, near_tiles, nb, lp, nq_real, k_sel):
    nq = lp // BLK
    half = (C_HEADS // 2) * C_DH
    qi0 = ODD_OFF[4] // (IDX_HEADS * IDX_DIM)
    ki0 = ODD_OFF[5] // LANES
    kc0 = ODD_OFF[1] // (C_KV * C_DH)
    vc0 = ODD_OFF[2] // (C_KV * C_DH)
    g0 = ODD_OFF[3] // half
    rows = C_GROUP * BLK
    n_index_bits = max(1, int(math.ceil(math.log2(lp))))
    return pl.pallas_call(
        functools.partial(_sparse_attn_kernel, k_sel=k_sel, n_index_bits=n_index_bits, nq_real=nq_real),
        grid=(nb, nq),
        in_specs=[pl.BlockSpec((BLK, IDX_HEADS * IDX_DIM), lambda b, i: (b * nq + i, qi0)),
                  pl.BlockSpec((BLK, LANES), lambda b, i: (b * nq + i, ki0)),
                  pl.BlockSpec((lp, LANES), lambda b, i: (b, ki0)),
                  pl.BlockSpec((BLK, C_HEADS * C_DH), lambda b, i: (b * nq + i, 0)),
                  pl.BlockSpec((lp, C_KV * C_DH), lambda b, i: (b, kc0)),
                  pl.BlockSpec((lp, C_KV * C_DH), lambda b, i: (b, vc0)),
                  pl.BlockSpec((BLK, half), lambda b, i: (b * nq + i, g0)),
                  pl.BlockSpec((BLK, half), lambda b, i: (b * nq + i, g0 + 1)),
                  pl.BlockSpec((C_KV, 2, rows, BLK), lambda b, i: (0, 0, 0, 0))],
        out_specs=pl.BlockSpec((BLK, C_HEADS * C_DH), lambda b, i: (b * nq + i, 0)),
        out_shape=jax.ShapeDtypeStruct((nb * lp, C_HEADS * C_DH), BF16),
        scratch_shapes=[pltpu.VMEM((lp // CHUNK, BLK, CHUNK), I32),
                        pltpu.VMEM((IDX_HEADS * BLK, LANES), BF16),
                        pltpu.VMEM((IDX_HEADS, BLK, LANES), F32),
                        pltpu.VMEM((C_KV, rows, C_DH), BF16),
                        pltpu.VMEM((C_KV, 2 * SUBS, rows, BLK), F32),
                        pltpu.VMEM((BLK, LANES), I32),
                        pltpu.VMEM((BLK, LANES), I32),
                        pltpu.VMEM((C_KV, rows, LANES), F32),
                        pltpu.VMEM((C_KV, rows, 2 * C_DH), F32)],
        compiler_params=_cparams(("parallel", "arbitrary")),
        name="sparse_attn_prompt",
    )(zf, zf, zb, zf, zb, zb, zf, zf, near_tiles)


PAGES_PER_STEP = 8
DEC_ROWS = 16


def _interleaved_pages(refs, page):
    halves = [jnp.concatenate([r[pl.ds(h, page, stride=2), :] for r in refs], axis=0) for h in range(2)]
    return jnp.concatenate(halves, axis=1).astype(BF16)


def _paged_attn_kernel(*refs, n_pages_step, masked, page, k_feature_major):
    if masked:
        pt_ref, tau_ref, jmax_ref, selnew_ref = refs[:4]
        refs = refs[4:]
    else:
        pt_ref = refs[0]
        refs = refs[1:]
    q_ref, knew_ref, vnew_ref, bfar_ref, blast_ref, b0_ref = refs[:6]
    refs = refs[6:]
    if masked:
        keys_ref = refs[0]
        refs = refs[1:]
    k_refs = refs[:n_pages_step]
    v_refs = refs[n_pages_step:2 * n_pages_step]
    out_ref, m_ref, l_ref, acc_ref = refs[2 * n_pages_step:]
    del pt_ref
    b = pl.program_id(0)
    j = pl.program_id(1)
    width = n_pages_step * page
    q = q_ref[...]

    @pl.when(j == 0)
    def _():
        s_new = jnp.sum(q.astype(F32) * knew_ref[...].astype(BF16).astype(F32), axis=1, keepdims=True) + b0_ref[...]
        v_new = jnp.broadcast_to(vnew_ref[...].astype(BF16).astype(F32), acc_ref.shape)
        if masked:
            take = selnew_ref[b] > 0
            m_ref[...] = jnp.where(take, s_new, NEG)
            l_ref[...] = jnp.where(take, 1.0, 0.0) * jnp.ones(l_ref.shape, F32)
            acc_ref[...] = jnp.where(take, v_new, 0.0)
        else:
            m_ref[...] = s_new
            l_ref[...] = jnp.ones(l_ref.shape, F32)
            acc_ref[...] = v_new

    if k_feature_major:
        kcat = jnp.concatenate([r[...] for r in k_refs], axis=1).astype(BF16)
        s = jnp.dot(q, kcat, preferred_element_type=F32)
    else:
        s = lax.dot_general(q, _interleaved_pages(k_refs, page), NT, preferred_element_type=F32)
    vcat = _interleaved_pages(v_refs, page)
    s = s + jnp.where(j == pl.num_programs(1) - 1, blast_ref[...], bfar_ref[...])
    if masked:
        key = keys_ref[...]
        idx = j * width + lax.broadcasted_iota(I32, (1, width), 1)
        tau = tau_ref[b]
        sel = (key > tau) | ((key == tau) & (idx <= jmax_ref[b]))
        s = jnp.where(sel, s, NEG)
    m_old = m_ref[...]
    m_new = jnp.maximum(m_old, jnp.max(s, axis=1, keepdims=True))
    alpha = jnp.exp(m_old - m_new)
    p = jnp.exp(s - m_new[:, 0:1])
    l_ref[...] = alpha * l_ref[...] + jnp.sum(p, axis=1, keepdims=True)
    acc_ref[...] = _lane_tile(alpha, 2) * acc_ref[...] + jnp.dot(p.astype(BF16), vcat, preferred_element_type=F32)
    m_ref[...] = m_new

    @pl.when(j == pl.num_programs(1) - 1)
    def _():
        o = acc_ref[...]
        rowi = lax.broadcasted_iota(I32, (DEC_ROWS, LANES), 0)
        upper = (rowi >= DEC_ROWS // 2) if not masked else ((rowi >= DEC_ROWS // 4) & (rowi < DEC_ROWS // 2))
        out_ref[...] = jnp.where(upper, o[:, LANES:], o[:, :LANES]) / l_ref[...]


def _pages_per_step(n_pages):
    g = PAGES_PER_STEP
    while n_pages % g:
        g //= 2
    return g


def _paged_attention(qprime, k_new, v_new, bias_far, bias_last, bias0, k_cache, v_cache, page_table, page,
                     k_feature_major, mask_args=None):
    db = qprime.shape[0]
    n_pages = page_table.shape[1]
    width = qprime.shape[2]
    g = _pages_per_step(n_pages)
    n_steps = n_pages // g
    masked = mask_args is not None
    n_pref = 4 if masked else 1

    def page_map(gi):
        return lambda b, j, pt, *_: (pt[b * n_pages + j * g + gi], 0, 0)

    in_specs = [pl.BlockSpec((None, DEC_ROWS, width), lambda b, j, *_: (b, 0, 0)),
                pl.BlockSpec((None, 1, width), lambda b, j, *_: (b, 0, 0)),
                pl.BlockSpec((None, 1, width), lambda b, j, *_: (b, 0, 0)),
                pl.BlockSpec((DEC_ROWS, g * page), lambda b, j, *_: (0, 0)),
                pl.BlockSpec((DEC_ROWS, g * page), lambda b, j, *_: (0, 0)),
                pl.BlockSpec((DEC_ROWS, LANES), lambda b, j, *_: (0, 0))]
    args = [qprime, k_new, v_new, bias_far, bias_last, bias0]
    prefetch = [page_table.reshape(-1)]
    if masked:
        keys, tau, jmax, selnew = mask_args
        prefetch += [tau, jmax, selnew]
        in_specs += [pl.BlockSpec((None, 1, g * page), lambda b, j, *_: (b, 0, j))]
        args += [keys]
    in_specs += [pl.BlockSpec((None,) + k_cache.shape[1:], page_map(gi)) for gi in range(g)]
    in_specs += [pl.BlockSpec((None,) + v_cache.shape[1:], page_map(gi)) for gi in range(g)]
    args += [k_cache] * g + [v_cache] * g
    grid_spec = pltpu.PrefetchScalarGridSpec(
        num_scalar_prefetch=n_pref,
        grid=(db, n_steps),
        in_specs=in_specs,
        out_specs=pl.BlockSpec((None, DEC_ROWS, LANES), lambda b, j, *_: (b, 0, 0)),
        scratch_shapes=[pltpu.VMEM((DEC_ROWS, LANES), F32),
                        pltpu.VMEM((DEC_ROWS, LANES), F32),
                        pltpu.VMEM((DEC_ROWS, width), F32)])
    return pl.pallas_call(
        functools.partial(_paged_attn_kernel, n_pages_step=g, masked=masked, page=page,
                          k_feature_major=k_feature_major),
        grid_spec=grid_spec,
        out_shape=jax.ShapeDtypeStruct((db, DEC_ROWS, LANES), F32),
        compiler_params=_cparams(("parallel", "arbitrary")),
        name="paged_attn_masked" if masked else "paged_attn",
    )(*prefetch, *args)


def _paged_index_kernel(pt_ref, q_ref, w_ref, knew_ref, *refs, n_pages_step, page):
    k_refs = refs[:n_pages_step]
    out_ref = refs[n_pages_step]
    del pt_ref
    j = pl.program_id(1)
    last = pl.num_programs(1) - 1
    q = q_ref[...]
    w = w_ref[...]

    def score(kmat_t):
        d = jnp.dot(q, kmat_t.astype(BF16), preferred_element_type=F32)
        return jnp.sum(jnp.maximum(d, 0.0) * w[:, 0:1], axis=0, keepdims=True)

    @pl.when(j < last)
    def _():
        kcat = jnp.concatenate([r[...] for r in k_refs], axis=1)
        out_ref[...] = _sortable_key(score(kcat))

    @pl.when(j == last)
    def _():
        sc = score(knew_ref[...])
        lane = lax.broadcasted_iota(I32, (1, page), 1)
        sc = jnp.where(lane == 0, sc, -jnp.inf)
        pad = jnp.full((1, (n_pages_step - 1) * page), -jnp.inf, F32)
        full = jnp.concatenate([sc, pad], axis=1) if n_pages_step > 1 else sc
        out_ref[...] = _sortable_key(full)


def _paged_index_scores(qidx, wcol, k_new_tile, idx_cache_t, page_table):
    db = qidx.shape[0]
    n_pages = page_table.shape[1]
    page = idx_cache_t.shape[2]
    g = _pages_per_step(n_pages)
    n_steps = n_pages // g

    def page_map(gi):
        return lambda b, j, pt: (pt[b * n_pages + jnp.minimum(j, n_steps - 1) * g + gi], 0, 0)

    grid_spec = pltpu.PrefetchScalarGridSpec(
        num_scalar_prefetch=1,
        grid=(db, n_steps + 1),
        in_specs=[pl.BlockSpec((None, IDX_HEADS, IDX_DIM), lambda b, j, pt: (b, 0, 0)),
                  pl.BlockSpec((None, IDX_HEADS, LANES), lambda b, j, pt: (b, 0, 0)),
                  pl.BlockSpec((None, IDX_DIM, page), lambda b, j, pt: (b, 0, 0))]
        + [pl.BlockSpec((None, IDX_DIM, page), page_map(gi)) for gi in range(g)],
        out_specs=pl.BlockSpec((None, 1, g * page), lambda b, j, pt: (b, 0, j)))
    return pl.pallas_call(
        functools.partial(_paged_index_kernel, n_pages_step=g, page=page),
        grid_spec=grid_spec,
        out_shape=jax.ShapeDtypeStruct((db, 1, (n_steps + 1) * g * page), I32),
        compiler_params=_cparams(("parallel", "arbitrary")),
        name="paged_index_scores",
    )(page_table.reshape(-1), qidx, wcol, k_new_tile, *([idx_cache_t] * g))


def _select_kernel(keys_ref, tau_ref, jmax_ref, *, k_sel, n_index_bits):
    keys = keys_ref[...]
    idx = lax.broadcasted_iota(I32, keys.shape, 1)
    shape = (keys.shape[0], 1)

    def count_fn(pred):
        return jnp.sum(jnp.where(pred(keys, idx), 1, 0), axis=1, keepdims=True)

    tau = _kth_largest(count_fn, k_sel, shape)
    need = k_sel - count_fn(lambda key, i: key > tau)
    jmax = _tie_cutoff(count_fn, tau, need, n_index_bits, shape)
    tau_ref[...] = jnp.broadcast_to(tau, tau_ref.shape)
    jmax_ref[...] = jnp.broadcast_to(jmax, jmax_ref.shape)


def _select_rows(keys2d, k_sel):
    rows, width = keys2d.shape
    n_index_bits = max(1, int(math.ceil(math.log2(width))))
    return pl.pallas_call(
        functools.partial(_select_kernel, k_sel=k_sel, n_index_bits=n_index_bits),
        out_shape=[jax.ShapeDtypeStruct((rows, LANES), I32), jax.ShapeDtypeStruct((rows, LANES), I32)],
        compiler_params=pltpu.CompilerParams(vmem_limit_bytes=VMEM_LIMIT),
        name="select_rows",
    )(keys2d)


def _even_tail_kernel(z_ref, conv_ref, s_ref, oa_ref, convw_ref, gp_ref, normg_ref, lamv_ref, subln_ref,
                      ya_ref, yb_ref, snew_ref, *, lam_init):
    z = z_ref[...]
    hw = B_HEADS * B_DK
    lv = lamv_ref[...]
    lam = (jnp.exp(jnp.sum(lv[0:1] * lv[1:2], axis=1, keepdims=True))
           - jnp.exp(jnp.sum(lv[2:3] * lv[3:4], axis=1, keepdims=True)) + lam_init)
    oa = oa_ref[...]
    for hg in range(A_HEADS):
        o = oa[2 * hg:2 * hg + 1] - lam * oa[2 * hg + 1:2 * hg + 2]
        y = o * lax.rsqrt(jnp.mean(o * o, axis=-1, keepdims=True) + EPS)
        y = (y * subln_ref[...]) * (1.0 - lam_init)
        gate = z[:, EVEN_OFF[3] + hg * A_DV:EVEN_OFF[3] + (hg + 1) * A_DV]
        ya_ref[:, hg * A_DV:(hg + 1) * A_DV] = (y * _silu(gate)).astype(BF16)
    x_new = z[:, EVEN_OFF[4]:EVEN_OFF[4] + B_QKV]
    conv = convw_ref[CONV_W - 1:CONV_W, :] * x_new
    cp = conv_ref[...]
    for i in range(CONV_W - 1):
        conv = conv + convw_ref[i:i + 1, :] * cp[i:i + 1, :]
    act = _silu(conv)
    ab = z[:, EVEN_OFF[6]:EVEN_OFF[6] + LANES]
    gp = gp_ref[...]
    row = lax.broadcasted_iota(I32, (B_DK, B_DV), 0)
    col = lax.broadcasted_iota(I32, (B_DK, B_DV), 1)
    eye = row == col
    for h in range(B_HEADS):
        q = act[:, h * B_DK:(h + 1) * B_DK]
        k = act[:, hw + h * B_DK:hw + (h + 1) * B_DK]
        v = act[:, 2 * hw + h * B_DV:2 * hw + (h + 1) * B_DV]
        q = q * lax.rsqrt(jnp.sum(q * q, axis=-1, keepdims=True) + EPS) * (B_DK ** -0.5)
        k = k * lax.rsqrt(jnp.sum(k * k, axis=-1, keepdims=True) + EPS)
        g = -jnp.exp(gp[0:1, h:h + 1]) * _softplus(ab[:, h:h + 1] + gp[1:2, h:h + 1])
        beta = jax.nn.sigmoid(ab[:, B_HEADS + h:B_HEADS + h + 1])
        eg = jnp.exp(g)
        s0 = s_ref[h]
        kcol = jnp.sum(jnp.where(eye, jnp.broadcast_to(k, (B_DK, B_DK)), 0.0), axis=1, keepdims=True)
        qcol = jnp.sum(jnp.where(eye, jnp.broadcast_to(q, (B_DK, B_DK)), 0.0), axis=1, keepdims=True)
        ks = jnp.sum(kcol * s0, axis=0, keepdims=True)
        qs = jnp.sum(qcol * s0, axis=0, keepdims=True)
        u = beta * (v - eg * ks)
        qk = jnp.sum(q * k, axis=1, keepdims=True)
        o = eg * qs + qk * u
        snew_ref[h] = eg * s0 + kcol * u
        y = o * lax.rsqrt(jnp.mean(o * o, axis=-1, keepdims=True) + EPS) * normg_ref[...]
        gate = z[:, EVEN_OFF[5] + h * B_DV:EVEN_OFF[5] + (h + 1) * B_DV]
        yb_ref[:, h * B_DV:(h + 1) * B_DV] = (y * _silu(gate)).astype(BF16)


def _even_tail(zf_s, conv_prev, s_prev, oa, conv_w, a_log, dt_bias, norm_g, lamv, subln, lam_init):
    db, npad = zf_s.shape
    gp = jnp.zeros((SUBLANES, LANES), F32).at[0, :B_HEADS].set(a_log).at[1, :B_HEADS].set(dt_bias)
    hw = B_HEADS * B_DV
    return pl.pallas_call(
        functools.partial(_even_tail_kernel, lam_init=lam_init),
        grid=(db,),
        in_specs=[pl.BlockSpec((None, 1, npad), lambda b: (b, 0, 0)),
                  pl.BlockSpec((None, CONV_W - 1, B_QKV), lambda b: (b, 0, 0)),
                  pl.BlockSpec((None, B_HEADS, B_DK, B_DV), lambda b: (b, 0, 0, 0)),
                  pl.BlockSpec((None, DEC_ROWS, LANES), lambda b: (b, 0, 0)),
                  pl.BlockSpec((CONV_W, B_QKV), lambda b: (0, 0)),
                  pl.BlockSpec((SUBLANES, LANES), lambda b: (0, 0)),
                  pl.BlockSpec((1, B_DV), lambda b: (0, 0)),
                  pl.BlockSpec((4, A_DH), lambda b: (0, 0)),
                  pl.BlockSpec((1, A_DV), lambda b: (0, 0))],
        out_specs=[pl.BlockSpec((None, 1, A_HEADS * A_DV), lambda b: (b, 0, 0)),
                   pl.BlockSpec((None, 1, hw), lambda b: (b, 0, 0)),
                   pl.BlockSpec((None, B_HEADS, B_DK, B_DV), lambda b: (b, 0, 0, 0))],
        out_shape=[jax.ShapeDtypeStruct((db, 1, A_HEADS * A_DV), BF16),
                   jax.ShapeDtypeStruct((db, 1, hw), BF16),
                   jax.ShapeDtypeStruct((db, B_HEADS, B_DK, B_DV), F32)],
        compiler_params=_cparams(("parallel",)),
        name="even_tail",
    )(zf_s.reshape(db, 1, npad), conv_prev, s_prev, oa, conv_w, gp, norm_g.reshape(1, B_DV), lamv,
      subln.reshape(1, A_DV))


def _odd_tail_kernel(z_ref, oc_ref, y_ref):
    z = z_ref[...]
    oc = oc_ref[...]
    for hg in range(C_HEADS):
        gate = z[:, ODD_OFF[3] + hg * C_DH:ODD_OFF[3] + (hg + 1) * C_DH]
        y_ref[:, hg * C_DH:(hg + 1) * C_DH] = (oc[hg:hg + 1] * _silu(gate)).astype(BF16)


def _odd_tail(zf_s, oc):
    db, npad = zf_s.shape
    return pl.pallas_call(
        _odd_tail_kernel,
        grid=(db,),
        in_specs=[pl.BlockSpec((None, 1, npad), lambda b: (b, 0, 0)),
                  pl.BlockSpec((None, DEC_ROWS, LANES), lambda b: (b, 0, 0))],
        out_specs=pl.BlockSpec((None, 1, C_HEADS * C_DH), lambda b: (b, 0, 0)),
        out_shape=jax.ShapeDtypeStruct((db, 1, C_HEADS * C_DH), BF16),
        compiler_params=_cparams(("parallel",)),
        name="odd_tail",
    )(zf_s.reshape(db, 1, npad), oc)


def _bias_by_distance(table):
    n = jnp.arange(FAR_DIST + 1)
    exact = N_BUCKETS // 2
    nf = jnp.maximum(n, 1).astype(F32)
    large = exact + (jnp.log(nf / exact) / math.log(MAX_DIST / exact) * (N_BUCKETS - exact)).astype(I32)
    bucket = jnp.where(n < exact, n, jnp.minimum(large, N_BUCKETS - 1))
    return table[bucket].astype(F32)


def _prompt_near_tiles(bd, group):
    r = jnp.arange(BLK)[:, None]
    c = jnp.arange(BLK)[None, :]
    d0 = r - c
    rel = (bd - bd[FAR_DIST][None, :]) * LOG2E
    t_prev = rel[jnp.minimum(BLK + d0, FAR_DIST)]
    t_diag = jnp.where((d0 >= 0)[..., None], rel[jnp.clip(d0, 0, FAR_DIST)], NEG)
    tiles = jnp.stack([t_prev, t_diag])
    tiles = jnp.transpose(tiles, (3, 0, 1, 2))
    n_kv = bd.shape[1] // group
    tiles = tiles.reshape(n_kv, group, 2, BLK, BLK)
    return jnp.transpose(tiles, (0, 2, 1, 3, 4)).reshape(n_kv, 2, group * BLK, BLK)


def _decode_bias(bd, row_heads, past, page, g):
    heads = jnp.asarray(row_heads, I32)
    far = jnp.broadcast_to(bd[FAR_DIST][heads][:, None], (len(row_heads), g * page))
    pos = past - g * page + jnp.arange(g * page)
    dist = jnp.minimum(past - pos, FAR_DIST)
    last = bd[dist][:, heads].T
    new = jnp.broadcast_to(bd[0][heads][:, None], (len(row_heads), LANES))
    return far.astype(F32), last.astype(F32), new.astype(F32)


def _pad_cols(w, mult):
    n = w.shape[1]
    return jnp.pad(w, ((0, 0), (0, _round_up(n, mult) - n)))


def kernel(x_prompt, x_sample, cache_a_k, cache_a_v, state_b_s, state_b_conv, cache_c_k, cache_c_v, cache_c_idx,
           page_table, meta, bias_table, final_norm, norm_e, w_in_e, w_out_e, lam_q1, lam_k1, lam_q2, lam_k2,
           subln_a, conv_b, a_log_b, dt_bias_b, norm_b, norm_o, w_in_o, w_out_o):
    nb, seq, d = x_prompt.shape
    n_meta = meta.shape[0]
    l = seq + n_meta
    lp = _round_up(l, CHUNK)
    nq_real = pl.cdiv(l, BLK)
    db = x_sample.shape[0]
    n_pages = page_table.shape[1]
    page = cache_a_k.shape[2]
    past = n_pages * page
    n_pool = cache_a_k.shape[1]
    assert x_sample.shape[1] == 1 and norm_e.shape[0] == 1 and norm_o.shape[0] == 1
    lam_init = 0.8 - 0.6 * math.exp(-0.3 * 0)

    tn = 256
    w_e = _pad_cols(w_in_e[0], tn).astype(BF16)
    w_o = _pad_cols(w_in_o[0], tn).astype(BF16)
    w_out_a = w_out_e[0][:A_HEADS * A_DV].astype(BF16)
    w_out_b = w_out_e[0][A_HEADS * A_DV:].astype(BF16)
    w_out_c = w_out_o[0].astype(BF16)
    tm = CHUNK
    dbp = _round_up(db, SUBLANES)

    bd = _bias_by_distance(bias_table)
    tiles_a = _prompt_near_tiles(bd, A_GROUP)
    tiles_c = _prompt_near_tiles(bd, C_GROUP)
    lamv = jnp.stack([lam_q1[0], lam_k1[0], lam_q2[0], lam_k2[0]]).astype(F32)

    hp = jnp.concatenate([jnp.broadcast_to(meta.astype(F32)[None], (nb, n_meta, d)), x_prompt], axis=1)
    hp = jnp.pad(hp, ((0, 0), (0, lp - l), (0, 0))).reshape(nb * lp, d)
    zf, zb = _norm_proj(hp, norm_e[0], w_e, tm, tn)
    ya = _diff_attn_prompt(zf, zb, tiles_a, lamv, subln_a[0], nb, lp, nq_real, lam_init)
    yb, pb_s = _gdn_prompt(zf, conv_b[0], a_log_b[0], dt_bias_b[0], norm_b[0], nb, lp, l)
    h1 = _out_proj([ya, yb], [w_out_a, w_out_b], hp, tm)
    z3 = zf.reshape(nb, lp, -1)
    pa_k = z3[:, :l, EVEN_OFF[1]:EVEN_OFF[2]].reshape(1, nb, l, A_KV, 2, A_DH)
    pa_v = z3[:, :l, EVEN_OFF[2]:EVEN_OFF[3]].reshape(1, nb, l, A_KV, A_DV)
    pb_conv = z3[:, l - (CONV_W - 1):l, EVEN_OFF[4]:EVEN_OFF[5]][None]

    zf1, zb1 = _norm_proj(h1, norm_o[0], w_o, tm, tn)
    k_sel_p = min(TOPK_MAX, l // 4)
    yc = _sparse_attn_prompt(zf1, zb1, tiles_c, nb, lp, nq_real, k_sel_p)
    yp = _out_proj([yc], [w_out_c], h1, tm, final_gain=final_norm)
    y_prompt = yp.reshape(nb, lp, d)[:, n_meta:l]
    z13 = zf1.reshape(nb, lp, -1)
    pc_k = z13[:, :l, ODD_OFF[1]:ODD_OFF[2]].reshape(1, nb, l, C_KV, C_DH)
    pc_v = z13[:, :l, ODD_OFF[2]:ODD_OFF[3]].reshape(1, nb, l, C_KV, C_DH)
    pc_idx = z13[:, :l, ODD_OFF[5]:ODD_OFF[6]][None]

    hs = jnp.pad(x_sample.reshape(db, d), ((0, dbp - db), (0, 0)))
    zs, _ = _norm_proj(hs, norm_e[0], w_e, dbp, tn)
    zs = zs[:db]
    qa = zs[:, :EVEN_OFF[1]].reshape(db, A_KV, A_GROUP, 2, A_DH) * (A_DH ** -0.5)
    qprime = jnp.einsum('bhgcd,hi,cj->bhgcijd', qa, jnp.eye(A_KV, dtype=F32), jnp.eye(2, dtype=F32))
    qprime = qprime.reshape(db, DEC_ROWS, A_KV * 2 * A_DH).astype(BF16)
    k_new = zs[:, EVEN_OFF[1]:EVEN_OFF[2]].reshape(db, 1, -1)
    v_new = zs[:, EVEN_OFF[2]:EVEN_OFF[3]].reshape(db, 1, -1)
    g_dec = _pages_per_step(n_pages)
    rows_a = [r // 2 for r in range(DEC_ROWS)]
    bfar, blast, bnew = _decode_bias(bd, rows_a, past, page, g_dec)
    ak_t = jnp.transpose(cache_a_k[0], (0, 2, 3, 4, 1)).reshape(n_pool, A_KV * 2 * A_DH, page)
    av_r = cache_a_v[0].reshape(n_pool, page * A_KV, A_DV)
    oa = _paged_attention(qprime, k_new, v_new, bfar, blast, bnew, ak_t, av_r, page_table, page,
                          k_feature_major=True)
    ya_s, yb_s, sb_s = _even_tail(zs, state_b_conv[0], state_b_s[0], oa, conv_b[0], a_log_b[0], dt_bias_b[0],
                                  norm_b[0], lamv, subln_a[0], lam_init)
    hs_pad = lambda y: jnp.pad(y.reshape(db, -1), ((0, dbp - db), (0, 0)))
    hs1 = _out_proj([hs_pad(ya_s), hs_pad(yb_s)], [w_out_a, w_out_b], hs, dbp)
    sa_k = zs[:, EVEN_OFF[1]:EVEN_OFF[2]].reshape(1, db, 1, A_KV, 2, A_DH)
    sa_v = zs[:, EVEN_OFF[2]:EVEN_OFF[3]].reshape(1, db, 1, A_KV, A_DV)
    sb_conv = jnp.concatenate([state_b_conv[0][:, 1:], zs[:, None, EVEN_OFF[4]:EVEN_OFF[5]]], axis=1)[None]

    zs1, _ = _norm_proj(hs1, norm_o[0], w_o, dbp, tn)
    zs1 = zs1[:db]
    qidx = (zs1[:, ODD_OFF[4]:ODD_OFF[5]].reshape(db, IDX_HEADS, IDX_DIM) * (IDX_DIM ** -0.5)).astype(BF16)
    wcol = jnp.broadcast_to((zs1[:, ODD_OFF[6]:ODD_OFF[7]] * (IDX_HEADS ** -0.5))[:, :, None],
                            (db, IDX_HEADS, LANES))
    ki_new = zs1[:, ODD_OFF[5]:ODD_OFF[6]]
    ki_tile = jnp.zeros((db, IDX_DIM, page), F32).at[:, :, 0].set(ki_new)
    ci_t = jnp.transpose(cache_c_idx[0], (0, 2, 1))
    keys = _paged_index_scores(qidx, wcol, ki_tile, ci_t, page_table)
    k_sel_s = min(TOPK_MAX, (past + 1) // 4)
    tau, jmax = _select_rows(keys.reshape(db, -1), k_sel_s)
    tau, jmax = tau[:, 0], jmax[:, 0]
    key_new = keys[:, 0, past]
    selnew = ((key_new > tau) | ((key_new == tau) & (past <= jmax))).astype(I32)
    qc = zs1[:, :ODD_OFF[1]].reshape(db, C_KV, C_GROUP, C_DH) * (C_DH ** -0.5)
    qcp = jnp.einsum('bhgd,hi->bhgid', qc, jnp.eye(C_KV, dtype=F32)).reshape(db, C_HEADS, C_KV * C_DH)
    qcp = jnp.pad(qcp, ((0, 0), (0, DEC_ROWS - C_HEADS), (0, 0))).astype(BF16)
    kc_new = zs1[:, ODD_OFF[1]:ODD_OFF[2]].reshape(db, 1, -1)
    vc_new = zs1[:, ODD_OFF[2]:ODD_OFF[3]].reshape(db, 1, -1)
    rows_c = [r if r < C_HEADS else 0 for r in range(DEC_ROWS)]
    cfar, clast, cnew = _decode_bias(bd, rows_c, past, page, g_dec)
    ck_r = cache_c_k[0].reshape(n_pool, page * C_KV, C_DH)
    cv_r = cache_c_v[0].reshape(n_pool, page * C_KV, C_DH)
    oc = _paged_attention(qcp, kc_new, vc_new, cfar, clast, cnew, ck_r, cv_r, page_table, page,
                          k_feature_major=False, mask_args=(keys, tau, jmax, selnew))
    yc_s = _odd_tail(zs1, oc)
    ys = _out_proj([hs_pad(yc_s)], [w_out_c], hs1, dbp, final_gain=final_norm)
    y_sample = ys[:db].reshape(db, 1, d)
    sc_k = zs1[:, ODD_OFF[1]:ODD_OFF[2]].reshape(1, db, 1, C_KV, C_DH)
    sc_v = zs1[:, ODD_OFF[2]:ODD_OFF[3]].reshape(1, db, 1, C_KV, C_DH)
    sc_idx = zs1[:, None, ODD_OFF[5]:ODD_OFF[6]][None]

    return (y_prompt, y_sample, pa_k, pa_v, pb_s[None], pb_conv, pc_k, pc_v, pc_idx,
            sa_k, sa_v, sb_s[None], sb_conv, sc_k, sc_v, sc_idx)
```

```python
import functools
import math

import jax
import jax.numpy as jnp
import numpy as np
from jax import lax
from jax.experimental import pallas as pl
from jax.experimental.pallas import tpu as pltpu

F32 = jnp.float32
BF16 = jnp.bfloat16
I32 = jnp.int32

EPS = 1e-6
N_BUCKETS = 32
MAX_DIST = 128
FAR_DIST = MAX_DIST

A_HEADS, A_KV, A_GROUP, A_DH, A_DV = 8, 2, 4, 64, 128
B_HEADS, B_DK, B_DV, CONV_W = 4, 128, 128, 4
C_HEADS, C_KV, C_GROUP, C_DH = 8, 2, 4, 128
IDX_HEADS, IDX_DIM, TOPK_MAX = 8, 64, 256
B_QKV = 2 * B_HEADS * B_DK + B_HEADS * B_DV

EVEN_COLS = (A_HEADS * 2 * A_DH, A_KV * 2 * A_DH, A_KV * A_DV, A_HEADS * A_DV, B_QKV, B_HEADS * B_DV, B_HEADS, B_HEADS)
ODD_COLS = (C_HEADS * C_DH, C_KV * C_DH, C_KV * C_DH, C_HEADS * C_DH, IDX_HEADS * IDX_DIM, IDX_DIM, IDX_HEADS)
EVEN_OFF = tuple(int(v) for v in np.cumsum((0,) + EVEN_COLS))
ODD_OFF = tuple(int(v) for v in np.cumsum((0,) + ODD_COLS))

LANES = 128
SUBLANES = 8
BLK = 128
CHUNK = 512
SUBS = CHUNK // BLK
NEG = -1e30
INT_MIN = -2 ** 31
INT_MAX = 2 ** 31 - 1
LOG2E = math.log2(math.e)
VMEM_LIMIT = 56 * 1024 * 1024
HI = lax.Precision.HIGHEST

NT = (((1,), (1,)), ((), ()))


def _round_up(x, m):
    return (x + m - 1) // m * m


def _cparams(sem):
    return pltpu.CompilerParams(dimension_semantics=sem, vmem_limit_bytes=VMEM_LIMIT)


def _silu(x):
    return x * jax.nn.sigmoid(x)


def _sortable_key(score):
    score = jnp.where(score == 0.0, 0.0, score)
    bits = pltpu.bitcast(score, I32)
    return bits ^ ((bits >> 31) & jnp.int32(0x7FFFFFFF))


def _lane_tile(x, n):
    return jnp.concatenate([x] * n, axis=1)


def _proj_kernel(x_ref, g_ref, w_ref, of_ref, ob_ref, xn_ref):
    @pl.when(pl.program_id(1) == 0)
    def _():
        xf = x_ref[...]
        y = xf * lax.rsqrt(jnp.mean(xf * xf, axis=-1, keepdims=True) + EPS)
        xn_ref[...] = (y * g_ref[...]).astype(BF16)

    acc = jnp.dot(xn_ref[...], w_ref[...], preferred_element_type=F32)
    of_ref[...] = acc
    ob_ref[...] = acc.astype(BF16)


def _norm_proj(x, gain, w_bf16, tm, tn):
    m, d = x.shape
    n = w_bf16.shape[1]
    return pl.pallas_call(
        _proj_kernel,
        grid=(m // tm, n // tn),
        in_specs=[pl.BlockSpec((tm, d), lambda i, j: (i, 0)),
                  pl.BlockSpec((1, d), lambda i, j: (0, 0)),
                  pl.BlockSpec((d, tn), lambda i, j: (0, j))],
        out_specs=[pl.BlockSpec((tm, tn), lambda i, j: (i, j)),
                   pl.BlockSpec((tm, tn), lambda i, j: (i, j))],
        out_shape=[jax.ShapeDtypeStruct((m, n), F32), jax.ShapeDtypeStruct((m, n), BF16)],
        scratch_shapes=[pltpu.VMEM((tm, d), BF16)],
        compiler_params=_cparams(("parallel", "arbitrary")),
        name="norm_proj",
    )(x, gain.reshape(1, d), w_bf16)


def _outproj_kernel(*refs, n_lhs, final_norm):
    ys = refs[:n_lhs]
    ws = refs[n_lhs:2 * n_lhs]
    h_ref = refs[2 * n_lhs]
    pos = 2 * n_lhs + 1
    fn_ref = refs[pos] if final_norm else None
    out_ref = refs[-1]
    acc = h_ref[...]
    for y_ref, w_ref in zip(ys, ws):
        acc = acc + jnp.dot(y_ref[...], w_ref[...], preferred_element_type=F32)
    if final_norm:
        y = acc * lax.rsqrt(jnp.mean(acc * acc, axis=-1, keepdims=True) + EPS)
        acc = y * fn_ref[...]
    out_ref[...] = acc


def _out_proj(ys, ws, h, tm, final_gain=None):
    m, n = h.shape
    n_lhs = len(ys)
    in_specs = [pl.BlockSpec((tm, y.shape[1]), lambda i: (i, 0)) for y in ys]
    in_specs += [pl.BlockSpec(w.shape, lambda i: (0, 0)) for w in ws]
    in_specs += [pl.BlockSpec((tm, n), lambda i: (i, 0))]
    args = list(ys) + list(ws) + [h]
    if final_gain is not None:
        in_specs += [pl.BlockSpec((1, n), lambda i: (0, 0))]
        args += [final_gain.reshape(1, n)]
    return pl.pallas_call(
        functools.partial(_outproj_kernel, n_lhs=n_lhs, final_norm=final_gain is not None),
        grid=(m // tm,),
        in_specs=in_specs,
        out_specs=pl.BlockSpec((tm, n), lambda i: (i, 0)),
        out_shape=jax.ShapeDtypeStruct((m, n), F32),
        compiler_params=_cparams(("parallel",)),
        name="out_proj",
    )(*args)


def _flash_step(s, v1, m_old, acc):
    m_new = jnp.maximum(m_old, jnp.max(s, axis=1, keepdims=True))
    alpha = jnp.exp2(m_old - m_new)
    p = jnp.exp2(s - _lane_tile(m_new, s.shape[1] // LANES))
    acc = _lane_tile(alpha, 2) * acc + jnp.dot(p.astype(BF16), v1, preferred_element_type=F32)
    return m_new, acc


def _far_chunk_loop(n_far, run_chunks):
    def pair(i, carry):
        run_chunks([2 * i, 2 * i + 1])
        return carry

    lax.fori_loop(0, lax.shift_right_logical(n_far, 1), pair, 0)

    @pl.when(jnp.bitwise_and(n_far, 1) == 1)
    def _():
        run_chunks([n_far - 1])


def _build_near_bias(strip_ref, t_prev, t_diag, r, lead=()):
    for sb in range(2 * SUBS):
        rel = sb - SUBS - r
        tile = jnp.where(rel == 0, t_diag, jnp.where(rel == -1, t_prev, jnp.where(rel < -1, 0.0, NEG)))
        strip_ref[lead + (sb,)] = tile


def _diff_attn_kernel(q_ref, k_ref, v_ref, gate_ref, tiles_ref, lamv_ref, subln_ref, out_ref,
                      qm_ref, strip_ref, m_ref, acc_ref, *, lam_init, nq_real):
    qi = pl.program_id(2)

    @pl.when(qi >= nq_real)
    def _():
        out_ref[...] = jnp.zeros(out_ref.shape, out_ref.dtype)

    @pl.when(qi < nq_real)
    def _():
        lane = lax.broadcasted_iota(I32, (BLK, 2 * A_DH), 1)
        for g in range(A_GROUP):
            qg = q_ref[:, g * 2 * A_DH:(g + 1) * 2 * A_DH] * (A_DH ** -0.5 * LOG2E)
            for c in range(2):
                keep = (lane < A_DH) if c == 0 else (lane >= A_DH)
                qm_ref[c, g * BLK:(g + 1) * BLK, :] = jnp.where(keep, qg, 0.0).astype(BF16)
        m_ref[...] = jnp.full(m_ref.shape, NEG, F32)
        acc_ref[...] = jnp.zeros(acc_ref.shape, F32)
        r = jnp.bitwise_and(qi, SUBS - 1)
        jd = lax.shift_right_logical(qi, int(math.log2(SUBS)))
        _build_near_bias(strip_ref, tiles_ref[0], tiles_ref[1], r)
        ones = jnp.ones((CHUNK, A_DV), BF16)

        def run_chunks(js, first_sb=None):
            kv = []
            for j in js:
                start = pl.multiple_of(j * CHUNK, CHUNK)
                kv.append((k_ref[pl.ds(start, CHUNK), :],
                           jnp.concatenate([v_ref[pl.ds(start, CHUNK), :], ones], axis=1)))
            for c in range(2):
                m, acc = m_ref[c], acc_ref[c]
                for k, v1 in kv:
                    s = lax.dot_general(qm_ref[c], k, NT, preferred_element_type=F32)
                    if first_sb is not None:
                        s = s + jnp.concatenate([strip_ref[first_sb + i] for i in range(SUBS)], axis=1)
                    m, acc = _flash_step(s, v1, m, acc)
                m_ref[c], acc_ref[c] = m, acc

        _far_chunk_loop(jnp.maximum(jd - 1, 0), run_chunks)

        @pl.when(jd >= 1)
        def _():
            run_chunks([jd - 1], 0)

        run_chunks([jd], SUBS)

        lv = lamv_ref[...]
        lam = (jnp.exp(jnp.sum(lv[0:1] * lv[1:2], axis=1, keepdims=True))
               - jnp.exp(jnp.sum(lv[2:3] * lv[3:4], axis=1, keepdims=True)) + lam_init)
        a0 = acc_ref[0]
        a1 = acc_ref[1]
        o = a0[:, :A_DV] / a0[:, A_DV:] - lam * (a1[:, :A_DV] / a1[:, A_DV:])
        y = o * lax.rsqrt(jnp.mean(o * o, axis=-1, keepdims=True) + EPS)
        y = (y * subln_ref[...]) * (1.0 - lam_init)
        for g in range(A_GROUP):
            gate = gate_ref[:, g * A_DV:(g + 1) * A_DV]
            out_ref[:, g * A_DV:(g + 1) * A_DV] = (y[g * BLK:(g + 1) * BLK] * _silu(gate)).astype(BF16)


def _diff_attn_prompt(zf, zb, near_tiles, lamv, subln, nb, lp, nq_real, lam_init):
    nq = lp // BLK
    qw = A_GROUP * 2 * A_DH
    k_blk0 = EVEN_OFF[1] // (2 * A_DH)
    v_blk0 = EVEN_OFF[2] // A_DV
    g_blk0 = EVEN_OFF[3] // (A_GROUP * A_DV)
    rows = A_GROUP * BLK
    return pl.pallas_call(
        functools.partial(_diff_attn_kernel, lam_init=lam_init, nq_real=nq_real),
        grid=(nb, A_KV, nq),
        in_specs=[pl.BlockSpec((BLK, qw), lambda b, h, i: (b * nq + i, h)),
                  pl.BlockSpec((lp, 2 * A_DH), lambda b, h, i: (b, k_blk0 + h)),
                  pl.BlockSpec((lp, A_DV), lambda b, h, i: (b, v_blk0 + h)),
                  pl.BlockSpec((BLK, A_GROUP * A_DV), lambda b, h, i: (b * nq + i, g_blk0 + h)),
                  pl.BlockSpec((None, 2, rows, BLK), lambda b, h, i: (h, 0, 0, 0)),
                  pl.BlockSpec((4, A_DH), lambda b, h, i: (0, 0)),
                  pl.BlockSpec((1, A_DV), lambda b, h, i: (0, 0))],
        out_specs=pl.BlockSpec((BLK, A_GROUP * A_DV), lambda b, h, i: (b * nq + i, h)),
        out_shape=jax.ShapeDtypeStruct((nb * lp, A_HEADS * A_DV), BF16),
        scratch_shapes=[pltpu.VMEM((2, rows, 2 * A_DH), BF16),
                        pltpu.VMEM((2 * SUBS, rows, BLK), F32),
                        pltpu.VMEM((2, rows, LANES), F32),
                        pltpu.VMEM((2, rows, 2 * A_DV), F32)],
        compiler_params=_cparams(("parallel", "parallel", "arbitrary")),
        name="diff_attn_prompt",
    )(zf, zb, zb, zf, near_tiles, lamv, subln.reshape(1, A_DV))


def _softplus(x):
    return jnp.maximum(x, 0.0) + jnp.log1p(jnp.exp(-jnp.abs(x)))


def _split_bf16(a):
    hi = a.astype(BF16)
    return hi, (a - hi.astype(F32)).astype(BF16)


def _dot3(a, b):
    (ah, al), (bh, bl) = a, b
    return (jnp.dot(ah, bh, preferred_element_type=F32)
            + (jnp.dot(ah, bl, preferred_element_type=F32) + jnp.dot(al, bh, preferred_element_type=F32)))


def _gdn_chunk_math(s0, q, k, v, gb, betab, row_ge, row_gt):
    c = q.shape[0]
    ltri = jnp.where(row_ge, 1.0, 0.0).astype(F32)
    gcum = jnp.dot(ltri, gb, precision=HI, preferred_element_type=F32)
    diff = gcum - gcum.T
    decay = jnp.where(row_ge, jnp.exp(jnp.where(row_ge, diff, 0.0)), 0.0)
    kb = k.astype(BF16)
    qb = q.astype(BF16)
    s0b = s0.astype(BF16)
    kk = lax.dot_general(kb, kb, NT, preferred_element_type=F32)
    x = -jnp.where(row_gt, betab * decay * kk, 0.0)
    tm = x
    p = x
    for _ in range(int(math.log2(c)) - 1):
        ps = _split_bf16(p)
        p = _dot3(ps, ps)
        tm = tm + p + _dot3(_split_bf16(tm), _split_bf16(p))
    eg = jnp.exp(gcum)
    ks = jnp.dot(kb, s0b, preferred_element_type=F32)
    rhs = betab * (v - eg * ks)
    u = rhs + jnp.dot(tm.astype(BF16), rhs.astype(BF16), preferred_element_type=F32)
    ub = u.astype(BF16)
    qk = lax.dot_general(qb, kb, NT, preferred_element_type=F32) * decay
    o = eg * jnp.dot(qb, s0b, preferred_element_type=F32) + jnp.dot(qk.astype(BF16), ub, preferred_element_type=F32)
    glast = gcum[c - 1:c, :]
    kd = k * jnp.exp(glast - gcum)
    s_new = jnp.exp(glast) * s0 + jnp.dot(kd.T.astype(BF16), ub, preferred_element_type=F32)
    return s_new, o


def _gdn_prompt_kernel(xq_ref, xk_ref, xv_ref, gate_ref, ab_ref, convw_ref, gp_ref, normg_ref,
                       y_ref, sfin_ref, xbuf_ref, s_ref, *, seq_len):
    ci = pl.program_id(0)
    hw = B_HEADS * B_DK
    nb = xq_ref.shape[0]

    @pl.when(ci == 0)
    def _():
        xbuf_ref[:, 0:SUBLANES, :] = jnp.zeros((nb, SUBLANES, B_QKV), F32)
        s_ref[...] = jnp.zeros(s_ref.shape, F32)

    row = lax.broadcasted_iota(I32, (BLK, BLK), 0)
    col = lax.broadcasted_iota(I32, (BLK, BLK), 1)
    row_ge = row >= col
    row_gt = row > col
    valid = (ci * BLK + row) < seq_len
    gp = gp_ref[...]
    for b in range(nb):
        xbuf_ref[b, SUBLANES:SUBLANES + BLK, 0:hw] = xq_ref[b]
        xbuf_ref[b, SUBLANES:SUBLANES + BLK, hw:2 * hw] = xk_ref[b]
        xbuf_ref[b, SUBLANES:SUBLANES + BLK, 2 * hw:3 * hw] = xv_ref[b]
        conv = jnp.zeros((BLK, B_QKV), F32)
        for i in range(CONV_W):
            conv = conv + convw_ref[i:i + 1, :] * xbuf_ref[b, pl.ds(SUBLANES - (CONV_W - 1) + i, BLK), :]
        tail = xbuf_ref[b, BLK:BLK + SUBLANES, :]
        xbuf_ref[b, 0:SUBLANES, :] = tail
        act = _silu(conv)
        ab = ab_ref[b]
        for h in range(B_HEADS):
            q = act[:, h * B_DK:(h + 1) * B_DK]
            k = act[:, hw + h * B_DK:hw + (h + 1) * B_DK]
            v = act[:, 2 * hw + h * B_DV:2 * hw + (h + 1) * B_DV]
            q = q * lax.rsqrt(jnp.sum(q * q, axis=-1, keepdims=True) + EPS) * (B_DK ** -0.5)
            k = k * lax.rsqrt(jnp.sum(k * k, axis=-1, keepdims=True) + EPS)
            a_raw = jnp.broadcast_to(ab[:, h:h + 1], (BLK, BLK))
            b_raw = jnp.broadcast_to(ab[:, B_HEADS + h:B_HEADS + h + 1], (BLK, BLK))
            a_log = gp[0:1, h:h + 1]
            dt_b = gp[1:2, h:h + 1]
            gb = jnp.where(valid, -jnp.exp(a_log) * _softplus(a_raw + dt_b), 0.0)
            betab = jnp.where(valid, jax.nn.sigmoid(b_raw), 0.0)
            s_new, o = _gdn_chunk_math(s_ref[b, h], q, k, v, gb, betab, row_ge, row_gt)
            s_ref[b, h] = s_new
            y = o * lax.rsqrt(jnp.mean(o * o, axis=-1, keepdims=True) + EPS) * normg_ref[...]
            gate = gate_ref[b, :, h * B_DV:(h + 1) * B_DV]
            y_ref[b, :, h * B_DV:(h + 1) * B_DV] = (y * _silu(gate)).astype(BF16)

    @pl.when(ci == pl.num_programs(0) - 1)
    def _():
        sfin_ref[...] = s_ref[...]


def _gdn_prompt(zf, conv_w, a_log, dt_bias, norm_g, nb, lp, seq_len):
    nc = lp // BLK
    hw = B_HEADS * B_DK
    c0 = EVEN_OFF[4] // hw
    g0 = EVEN_OFF[5] // hw
    ab0 = EVEN_OFF[6] // LANES
    gp = jnp.zeros((SUBLANES, LANES), F32).at[0, :B_HEADS].set(a_log).at[1, :B_HEADS].set(dt_bias)
    z3 = zf.reshape(nb, lp, zf.shape[1])
    y, s_fin = pl.pallas_call(
        functools.partial(_gdn_prompt_kernel, seq_len=seq_len),
        grid=(nc,),
        in_specs=[pl.BlockSpec((nb, BLK, hw), lambda i: (0, i, c0)),
                  pl.BlockSpec((nb, BLK, hw), lambda i: (0, i, c0 + 1)),
                  pl.BlockSpec((nb, BLK, hw), lambda i: (0, i, c0 + 2)),
                  pl.BlockSpec((nb, BLK, hw), lambda i: (0, i, g0)),
                  pl.BlockSpec((nb, BLK, LANES), lambda i: (0, i, ab0)),
                  pl.BlockSpec((CONV_W, B_QKV), lambda i: (0, 0)),
                  pl.BlockSpec((SUBLANES, LANES), lambda i: (0, 0)),
                  pl.BlockSpec((1, B_DV), lambda i: (0, 0))],
        out_specs=[pl.BlockSpec((nb, BLK, hw), lambda i: (0, i, 0)),
                   pl.BlockSpec((nb, B_HEADS, B_DK, B_DV), lambda i: (0, 0, 0, 0))],
        out_shape=[jax.ShapeDtypeStruct((nb, lp, hw), BF16),
                   jax.ShapeDtypeStruct((nb, B_HEADS, B_DK, B_DV), F32)],
        scratch_shapes=[pltpu.VMEM((nb, BLK + SUBLANES, B_QKV), F32),
                        pltpu.VMEM((nb, B_HEADS, B_DK, B_DV), F32)],
        compiler_params=_cparams(("arbitrary",)),
        name="gdn_prompt",
    )(z3, z3, z3, z3, z3, conv_w, gp, norm_g.reshape(1, B_DV))
    return y.reshape(nb * lp, hw), s_fin


def _kth_largest(count_fn, k_sel, shape):
    def step(i, tau):
        cand = tau + lax.shift_left(jnp.int32(1), 31 - i)
        cnt = count_fn(lambda key, idx: key >= cand)
        return jnp.where(cnt >= k_sel, cand, tau)

    return lax.fori_loop(0, 32, step, jnp.full(shape, INT_MIN, I32))


def _tie_cutoff(count_fn, tau, need, n_index_bits, shape):
    def step(i, jm):
        cand = jm | lax.shift_left(jnp.int32(1), n_index_bits - 1 - i)
        cnt = count_fn(lambda key, idx: (key == tau) & (idx < cand))
        return jnp.where(cnt < need, cand, jm)

    return lax.fori_loop(0, n_index_bits, step, jnp.zeros(shape, I32))


def _sparse_attn_kernel(qi_ref, kw_ref, kidx_ref, qc_ref, kc_ref, vc_ref, g0_ref, g1_ref, tiles_ref, out_ref,
                        keys_ref, qm_ref, wb_ref, qh_ref, strip_ref, tau_ref, jmax_ref, m_ref, acc_ref,
                        *, k_sel, n_index_bits, nq_real):
    qb = pl.program_id(1)

    @pl.when(qb >= nq_real)
    def _():
        out_ref[...] = jnp.zeros(out_ref.shape, out_ref.dtype)

    @pl.when(qb < nq_real)
    def _():
        lane = lax.broadcasted_iota(I32, (BLK, LANES), 1)
        lane_c = lax.broadcasted_iota(I32, (BLK, CHUNK), 1)
        row_c = lax.broadcasted_iota(I32, (BLK, CHUNK), 0)
        r = jnp.bitwise_and(qb, SUBS - 1)
        jd = lax.shift_right_logical(qb, int(math.log2(SUBS)))

        kw = kw_ref[...]
        for p in range(IDX_HEADS // 2):
            pair = qi_ref[:, p * LANES:(p + 1) * LANES] * (IDX_DIM ** -0.5)
            swapped = pltpu.roll(pair, IDX_DIM, 1)
            for e, src in enumerate((pair, swapped)):
                h = 2 * p + e
                qm_ref[h * BLK:(h + 1) * BLK, :] = jnp.where(lane < IDX_DIM, src, 0.0).astype(BF16)
                wcol = kw[:, IDX_DIM + h:IDX_DIM + h + 1] * (IDX_HEADS ** -0.5)
                wb_ref[h] = jnp.broadcast_to(wcol, (BLK, LANES))

        def scores(j):
            start = pl.multiple_of(j * CHUNK, CHUNK)
            d = lax.dot_general(qm_ref[...], kidx_ref[pl.ds(start, CHUNK), :], NT, preferred_element_type=F32)
            sc = jnp.zeros((BLK, CHUNK), F32)
            for h in range(IDX_HEADS):
                sc = sc + _lane_tile(wb_ref[h], SUBS) * jnp.maximum(d[h * BLK:(h + 1) * BLK], 0.0)
            return sc

        def score_body(j, carry):
            keys_ref[j] = _sortable_key(scores(j))
            return carry

        lax.fori_loop(0, jd, score_body, 0)
        admissible = (jd * CHUNK + lane_c) <= (qb * BLK + row_c)
        keys_ref[jd] = _sortable_key(jnp.where(admissible, scores(jd), -jnp.inf))

        def count_fn(pred):
            def body(j, acc):
                key = keys_ref[j]
                for i in range(SUBS):
                    hit = pred(key[:, i * LANES:(i + 1) * LANES], j * CHUNK + i * LANES + lane)
                    acc = acc + jnp.where(hit, 1, 0)
                return acc
            acc = lax.fori_loop(0, jd + 1, body, jnp.zeros((BLK, LANES), I32))
            return jnp.sum(acc, axis=1, keepdims=True)

        tau = _kth_largest(count_fn, k_sel, (BLK, LANES))
        tau_ref[...] = tau
        jmax_ref[...] = jnp.full((BLK, LANES), INT_MAX, I32)
        need = k_sel - count_fn(lambda key, idx: key > tau)
        n_eq = count_fn(lambda key, idx: key == tau)
        excess = jnp.max(jnp.where(n_eq > need, 1.0, 0.0)) > 0.5

        @pl.when(excess)
        def _():
            jmax_ref[...] = _tie_cutoff(count_fn, tau, need, n_index_bits, (BLK, LANES))

        tau_c = _lane_tile(tau_ref[...], SUBS)

        for h in range(C_KV):
            for g in range(C_GROUP):
                col = (h * C_GROUP + g) * C_DH
                qh_ref[h, g * BLK:(g + 1) * BLK, :] = (
                    qc_ref[:, col:col + C_DH] * (C_DH ** -0.5 * LOG2E)).astype(BF16)
            _build_near_bias(strip_ref, tiles_ref[h, 0], tiles_ref[h, 1], r, lead=(h,))
        m_ref[...] = jnp.full(m_ref.shape, NEG, F32)
        acc_ref[...] = jnp.zeros(acc_ref.shape, F32)
        ones = jnp.ones((CHUNK, C_DH), BF16)
        jmax_c = _lane_tile(jmax_ref[...], SUBS)

        def run_chunks(js, first_sb=None):
            starts, masks = [], []
            for j in js:
                starts.append(pl.multiple_of(j * CHUNK, CHUNK))
                key = keys_ref[j]
                tie_ok = jnp.where((j * CHUNK + lane_c) <= jmax_c, 0.0, NEG)
                selneg = jnp.where(key > tau_c, 0.0, jnp.where(key == tau_c, tie_ok, NEG))
                masks.append(jnp.concatenate([selneg] * C_GROUP, axis=0))
            for h in range(C_KV):
                m, acc = m_ref[h], acc_ref[h]
                for start, selneg in zip(starts, masks):
                    k = kc_ref[pl.ds(start, CHUNK), h * C_DH:(h + 1) * C_DH]
                    v1 = jnp.concatenate([vc_ref[pl.ds(start, CHUNK), h * C_DH:(h + 1) * C_DH], ones], axis=1)
                    s = lax.dot_general(qh_ref[h], k, NT, preferred_element_type=F32) + selneg
                    if first_sb is not None:
                        s = s + jnp.concatenate([strip_ref[h, first_sb + i] for i in range(SUBS)], axis=1)
                    m, acc = _flash_step(s, v1, m, acc)
                m_ref[h], acc_ref[h] = m, acc

        _far_chunk_loop(jnp.maximum(jd - 1, 0), run_chunks)

        @pl.when(jd >= 1)
        def _():
            run_chunks([jd - 1], 0)

        run_chunks([jd], SUBS)

        half = (C_HEADS // 2) * C_DH
        for h in range(C_KV):
            a = acc_ref[h]
            o = a[:, :C_DH] / a[:, C_DH:]
            gref = g0_ref if h == 0 else g1_ref
            for g in range(C_GROUP):
                gate = gref[:, g * C_DH:(g + 1) * C_DH]
                out_ref[:, h * half + g * C_DH:h * half + (g + 1) * C_DH] = (
                    o[g * BLK:(g + 1) * BLK] * _silu(gate)).astype(BF16)


def _sparse_attn_prompt(zf, zb, near_tiles, nb, lp, nq_real, k_sel):
    nq = lp // BLK
    half = (C_HEADS // 2) * C_DH
    qi0 = ODD_OFF[4] // (IDX_HEADS * IDX_DIM)
    ki0 = ODD_OFF[5] // LANES
    kc0 = ODD_OFF[1] // (C_KV * C_DH)
    vc0 = ODD_OFF[2] // (C_KV * C_DH)
    g0 = ODD_OFF[3] // half
    rows = C_GROUP * BLK
    n_index_bits = max(1, int(math.ceil(math.log2(lp))))
    return pl.pallas_call(
        functools.partial(_sparse_attn_kernel, k_sel=k_sel, n_index_bits=n_index_bits, nq_real=nq_real),
        grid=(nb, nq),
        in_specs=[pl.BlockSpec((BLK, IDX_HEADS * IDX_DIM), lambda b, i: (b * nq + i, qi0)),
                  pl.BlockSpec((BLK, LANES), lambda b, i: (b * nq + i, ki0)),
                  pl.BlockSpec((lp, LANES), lambda b, i: (b, ki0)),
                  pl.BlockSpec((BLK, C_HEADS * C_DH), lambda b, i: (b * nq + i, 0)),
                  pl.BlockSpec((lp, C_KV * C_DH), lambda b, i: (b, kc0)),
                  pl.BlockSpec((lp, C_KV * C_DH), lambda b, i: (b, vc0)),
                  pl.BlockSpec((BLK, half), lambda b, i: (b * nq + i, g0)),
                  pl.BlockSpec((BLK, half), lambda b, i: (b * nq + i, g0 + 1)),
                  pl.BlockSpec((C_KV, 2, rows, BLK), lambda b, i: (0, 0, 0, 0))],
        out_specs=pl.BlockSpec((BLK, C_HEADS * C_DH), lambda b, i: (b * nq + i, 0)),
        out_shape=jax.ShapeDtypeStruct((nb * lp, C_HEADS * C_DH), BF16),
        scratch_shapes=[pltpu.VMEM((lp // CHUNK, BLK, CHUNK), I32),
                        pltpu.VMEM((IDX_HEADS * BLK, LANES), BF16),
                        pltpu.VMEM((IDX_HEADS, BLK, LANES), F32),
                        pltpu.VMEM((C_KV, rows, C_DH), BF16),
                        pltpu.VMEM((C_KV, 2 * SUBS, rows, BLK), F32),
                        pltpu.VMEM((BLK, LANES), I32),
                        pltpu.VMEM((BLK, LANES), I32),
                        pltpu.VMEM((C_KV, rows, LANES), F32),
                        pltpu.VMEM((C_KV, rows, 2 * C_DH), F32)],
        compiler_params=_cparams(("parallel", "arbitrary")),
        name="sparse_attn_prompt",
    )(zf, zf, zb, zf, zb, zb, zf, zf, near_tiles)


PAGES_PER_STEP = 16
INDEX_PAGES_PER_STEP = 32
DEC_ROWS = 16


def _interleaved_pages(refs, page):
    halves = [jnp.concatenate([r[pl.ds(h, page, stride=2), :] for r in refs], axis=0) for h in range(2)]
    return jnp.concatenate(halves, axis=1).astype(BF16)


def _paged_attn_kernel(*refs, n_pages_step, masked, page, k_feature_major):
    if masked:
        pt_ref, tau_ref, jmax_ref, selnew_ref = refs[:4]
        refs = refs[4:]
    else:
        pt_ref = refs[0]
        refs = refs[1:]
    q_ref, knew_ref, vnew_ref, bfar_ref, blast_ref, b0_ref = refs[:6]
    refs = refs[6:]
    if masked:
        keys_ref = refs[0]
        refs = refs[1:]
    k_refs = refs[:n_pages_step]
    v_refs = refs[n_pages_step:2 * n_pages_step]
    out_ref, m_ref, l_ref, acc_ref = refs[2 * n_pages_step:]
    del pt_ref
    b = pl.program_id(0)
    j = pl.program_id(1)
    width = n_pages_step * page
    q = q_ref[...]

    @pl.when(j == 0)
    def _():
        s_new = jnp.sum(q.astype(F32) * knew_ref[...].astype(BF16).astype(F32), axis=1, keepdims=True) + b0_ref[...]
        v_new = jnp.broadcast_to(vnew_ref[...].astype(BF16).astype(F32), acc_ref.shape)
        if masked:
            take = selnew_ref[b] > 0
            m_ref[...] = jnp.where(take, s_new, NEG)
            l_ref[...] = jnp.where(take, 1.0, 0.0) * jnp.ones(l_ref.shape, F32)
            acc_ref[...] = jnp.where(take, v_new, 0.0)
        else:
            m_ref[...] = s_new
            l_ref[...] = jnp.ones(l_ref.shape, F32)
            acc_ref[...] = v_new

    if k_feature_major:
        kcat = jnp.concatenate([r[...] for r in k_refs], axis=1).astype(BF16)
        s = jnp.dot(q, kcat, preferred_element_type=F32)
    else:
        s = lax.dot_general(q, _interleaved_pages(k_refs, page), NT, preferred_element_type=F32)
    vcat = _interleaved_pages(v_refs, page)
    s = s + jnp.where(j == pl.num_programs(1) - 1, blast_ref[...], bfar_ref[...])
    if masked:
        key = keys_ref[...]
        idx = j * width + lax.broadcasted_iota(I32, (1, width), 1)
        tau = tau_ref[b]
        sel = (key > tau) | ((key == tau) & (idx <= jmax_ref[b]))
        s = jnp.where(sel, s, NEG)
    m_old = m_ref[...]
    m_new = jnp.maximum(m_old, jnp.max(s, axis=1, keepdims=True))
    alpha = jnp.exp(m_old - m_new)
    p = jnp.exp(s - m_new[:, 0:1])
    l_ref[...] = alpha * l_ref[...] + jnp.sum(p, axis=1, keepdims=True)
    acc_ref[...] = _lane_tile(alpha, 2) * acc_ref[...] + jnp.dot(p.astype(BF16), vcat, preferred_element_type=F32)
    m_ref[...] = m_new

    @pl.when(j == pl.num_programs(1) - 1)
    def _():
        o = acc_ref[...]
        rowi = lax.broadcasted_iota(I32, (DEC_ROWS, LANES), 0)
        upper = (rowi >= DEC_ROWS // 2) if not masked else ((rowi >= DEC_ROWS // 4) & (rowi < DEC_ROWS // 2))
        out_ref[...] = jnp.where(upper, o[:, LANES:], o[:, :LANES]) / l_ref[...]


def _pages_per_step(n_pages, g=PAGES_PER_STEP):
    while n_pages % g:
        g //= 2
    return g


def _paged_attention(qprime, k_new, v_new, bias_far, bias_last, bias0, k_cache, v_cache, page_table, page,
                     k_feature_major, mask_args=None):
    db = qprime.shape[0]
    n_pages = page_table.shape[1]
    width = qprime.shape[2]
    g = _pages_per_step(n_pages)
    n_steps = n_pages // g
    masked = mask_args is not None
    n_pref = 4 if masked else 1

    def page_map(gi):
        return lambda b, j, pt, *_: (pt[b * n_pages + j * g + gi], 0, 0)

    in_specs = [pl.BlockSpec((None, DEC_ROWS, width), lambda b, j, *_: (b, 0, 0)),
                pl.BlockSpec((None, 1, width), lambda b, j, *_: (b, 0, 0)),
                pl.BlockSpec((None, 1, width), lambda b, j, *_: (b, 0, 0)),
                pl.BlockSpec((DEC_ROWS, g * page), lambda b, j, *_: (0, 0)),
                pl.BlockSpec((DEC_ROWS, g * page), lambda b, j, *_: (0, 0)),
                pl.BlockSpec((DEC_ROWS, LANES), lambda b, j, *_: (0, 0))]
    args = [qprime, k_new, v_new, bias_far, bias_last, bias0]
    prefetch = [page_table.reshape(-1)]
    if masked:
        keys, tau, jmax, selnew = mask_args
        prefetch += [tau, jmax, selnew]
        in_specs += [pl.BlockSpec((None, 1, g * page), lambda b, j, *_: (b, 0, j))]
        args += [keys]
    in_specs += [pl.BlockSpec((None,) + k_cache.shape[1:], page_map(gi)) for gi in range(g)]
    in_specs += [pl.BlockSpec((None,) + v_cache.shape[1:], page_map(gi)) for gi in range(g)]
    args += [k_cache] * g + [v_cache] * g
    grid_spec = pltpu.PrefetchScalarGridSpec(
        num_scalar_prefetch=n_pref,
        grid=(db, n_steps),
        in_specs=in_specs,
        out_specs=pl.BlockSpec((None, DEC_ROWS, LANES), lambda b, j, *_: (b, 0, 0)),
        scratch_shapes=[pltpu.VMEM((DEC_ROWS, LANES), F32),
                        pltpu.VMEM((DEC_ROWS, LANES), F32),
                        pltpu.VMEM((DEC_ROWS, width), F32)])
    return pl.pallas_call(
        functools.partial(_paged_attn_kernel, n_pages_step=g, masked=masked, page=page,
                          k_feature_major=k_feature_major),
        grid_spec=grid_spec,
        out_shape=jax.ShapeDtypeStruct((db, DEC_ROWS, LANES), F32),
        compiler_params=_cparams(("parallel", "arbitrary")),
        name="paged_attn_masked" if masked else "paged_attn",
    )(*prefetch, *args)


def _paged_index_kernel(pt_ref, q_ref, w_ref, knew_ref, *refs, n_pages_step, page):
    k_refs = refs[:n_pages_step]
    out_ref = refs[n_pages_step]
    del pt_ref
    j = pl.program_id(1)
    last = pl.num_programs(1) - 1
    q = q_ref[...]
    w = w_ref[...]

    def score(kmat_t):
        d = jnp.dot(q, kmat_t.astype(BF16), preferred_element_type=F32)
        return jnp.sum(jnp.maximum(d, 0.0) * w[:, 0:1], axis=0, keepdims=True)

    @pl.when(j < last)
    def _():
        kcat = jnp.concatenate([r[...] for r in k_refs], axis=1)
        out_ref[...] = _sortable_key(score(kcat))

    @pl.when(j == last)
    def _():
        sc = score(knew_ref[...])
        lane = lax.broadcasted_iota(I32, (1, page), 1)
        sc = jnp.where(lane == 0, sc, -jnp.inf)
        pad = jnp.full((1, (n_pages_step - 1) * page), -jnp.inf, F32)
        full = jnp.concatenate([sc, pad], axis=1) if n_pages_step > 1 else sc
        out_ref[...] = _sortable_key(full)


def _paged_index_scores(qidx, wcol, k_new_tile, idx_cache_t, page_table):
    db = qidx.shape[0]
    n_pages = page_table.shape[1]
    page = idx_cache_t.shape[2]
    g = _pages_per_step(n_pages, INDEX_PAGES_PER_STEP)
    n_steps = n_pages // g

    def page_map(gi):
        return lambda b, j, pt: (pt[b * n_pages + jnp.minimum(j, n_steps - 1) * g + gi], 0, 0)

    grid_spec = pltpu.PrefetchScalarGridSpec(
        num_scalar_prefetch=1,
        grid=(db, n_steps + 1),
        in_specs=[pl.BlockSpec((None, IDX_HEADS, IDX_DIM), lambda b, j, pt: (b, 0, 0)),
                  pl.BlockSpec((None, IDX_HEADS, LANES), lambda b, j, pt: (b, 0, 0)),
                  pl.BlockSpec((None, IDX_DIM, page), lambda b, j, pt: (b, 0, 0))]
        + [pl.BlockSpec((None, IDX_DIM, page), page_map(gi)) for gi in range(g)],
        out_specs=pl.BlockSpec((None, 1, g * page), lambda b, j, pt: (b, 0, j)))
    return pl.pallas_call(
        functools.partial(_paged_index_kernel, n_pages_step=g, page=page),
        grid_spec=grid_spec,
        out_shape=jax.ShapeDtypeStruct((db, 1, (n_steps + 1) * g * page), I32),
        compiler_params=_cparams(("parallel", "arbitrary")),
        name="paged_index_scores",
    )(page_table.reshape(-1), qidx, wcol, k_new_tile, *([idx_cache_t] * g))


def _select_kernel(keys_ref, tau_ref, jmax_ref, *, k_sel, n_index_bits):
    keys = keys_ref[...]
    idx = lax.broadcasted_iota(I32, keys.shape, 1)
    shape = (keys.shape[0], 1)

    def count_fn(pred):
        return jnp.sum(jnp.where(pred(keys, idx), 1, 0), axis=1, keepdims=True)

    tau = _kth_largest(count_fn, k_sel, shape)
    need = k_sel - count_fn(lambda key, i: key > tau)
    jmax = _tie_cutoff(count_fn, tau, need, n_index_bits, shape)
    tau_ref[...] = jnp.broadcast_to(tau, tau_ref.shape)
    jmax_ref[...] = jnp.broadcast_to(jmax, jmax_ref.shape)


def _select_rows(keys2d, k_sel):
    rows, width = keys2d.shape
    n_index_bits = max(1, int(math.ceil(math.log2(width))))
    return pl.pallas_call(
        functools.partial(_select_kernel, k_sel=k_sel, n_index_bits=n_index_bits),
        out_shape=[jax.ShapeDtypeStruct((rows, LANES), I32), jax.ShapeDtypeStruct((rows, LANES), I32)],
        compiler_params=pltpu.CompilerParams(vmem_limit_bytes=VMEM_LIMIT),
        name="select_rows",
    )(keys2d)


def _even_tail_kernel(z_ref, conv_ref, s_ref, oa_ref, convw_ref, gp_ref, normg_ref, lamv_ref, subln_ref,
                      ya_ref, yb_ref, snew_ref, *, lam_init):
    z = z_ref[...]
    hw = B_HEADS * B_DK
    lv = lamv_ref[...]
    lam = (jnp.exp(jnp.sum(lv[0:1] * lv[1:2], axis=1, keepdims=True))
           - jnp.exp(jnp.sum(lv[2:3] * lv[3:4], axis=1, keepdims=True)) + lam_init)
    oa = oa_ref[...]
    for hg in range(A_HEADS):
        o = oa[2 * hg:2 * hg + 1] - lam * oa[2 * hg + 1:2 * hg + 2]
        y = o * lax.rsqrt(jnp.mean(o * o, axis=-1, keepdims=True) + EPS)
        y = (y * subln_ref[...]) * (1.0 - lam_init)
        gate = z[:, EVEN_OFF[3] + hg * A_DV:EVEN_OFF[3] + (hg + 1) * A_DV]
        ya_ref[:, hg * A_DV:(hg + 1) * A_DV] = (y * _silu(gate)).astype(BF16)
    x_new = z[:, EVEN_OFF[4]:EVEN_OFF[4] + B_QKV]
    conv = convw_ref[CONV_W - 1:CONV_W, :] * x_new
    cp = conv_ref[...]
    for i in range(CONV_W - 1):
        conv = conv + convw_ref[i:i + 1, :] * cp[i:i + 1, :]
    act = _silu(conv)
    ab = z[:, EVEN_OFF[6]:EVEN_OFF[6] + LANES]
    gp = gp_ref[...]
    row = lax.broadcasted_iota(I32, (B_DK, B_DV), 0)
    col = lax.broadcasted_iota(I32, (B_DK, B_DV), 1)
    eye = row == col
    for h in range(B_HEADS):
        q = act[:, h * B_DK:(h + 1) * B_DK]
        k = act[:, hw + h * B_DK:hw + (h + 1) * B_DK]
        v = act[:, 2 * hw + h * B_DV:2 * hw + (h + 1) * B_DV]
        q = q * lax.rsqrt(jnp.sum(q * q, axis=-1, keepdims=True) + EPS) * (B_DK ** -0.5)
        k = k * lax.rsqrt(jnp.sum(k * k, axis=-1, keepdims=True) + EPS)
        g = -jnp.exp(gp[0:1, h:h + 1]) * _softplus(ab[:, h:h + 1] + gp[1:2, h:h + 1])
        beta = jax.nn.sigmoid(ab[:, B_HEADS + h:B_HEADS + h + 1])
        eg = jnp.exp(g)
        s0 = s_ref[h]
        kcol = jnp.sum(jnp.where(eye, jnp.broadcast_to(k, (B_DK, B_DK)), 0.0), axis=1, keepdims=True)
        qcol = jnp.sum(jnp.where(eye, jnp.broadcast_to(q, (B_DK, B_DK)), 0.0), axis=1, keepdims=True)
        ks = jnp.sum(kcol * s0, axis=0, keepdims=True)
        qs = jnp.sum(qcol * s0, axis=0, keepdims=True)
        u = beta * (v - eg * ks)
        qk = jnp.sum(q * k, axis=1, keepdims=True)
        o = eg * qs + qk * u
        snew_ref[h] = eg * s0 + kcol * u
        y = o * lax.rsqrt(jnp.mean(o * o, axis=-1, keepdims=True) + EPS) * normg_ref[...]
        gate = z[:, EVEN_OFF[5] + h * B_DV:EVEN_OFF[5] + (h + 1) * B_DV]
        yb_ref[:, h * B_DV:(h + 1) * B_DV] = (y * _silu(gate)).astype(BF16)


def _even_tail(zf_s, conv_prev, s_prev, oa, conv_w, a_log, dt_bias, norm_g, lamv, subln, lam_init):
    db, npad = zf_s.shape
    gp = jnp.zeros((SUBLANES, LANES), F32).at[0, :B_HEADS].set(a_log).at[1, :B_HEADS].set(dt_bias)
    hw = B_HEADS * B_DV
    return pl.pallas_call(
        functools.partial(_even_tail_kernel, lam_init=lam_init),
        grid=(db,),
        in_specs=[pl.BlockSpec((None, 1, npad), lambda b: (b, 0, 0)),
                  pl.BlockSpec((None, CONV_W - 1, B_QKV), lambda b: (b, 0, 0)),
                  pl.BlockSpec((None, B_HEADS, B_DK, B_DV), lambda b: (b, 0, 0, 0)),
                  pl.BlockSpec((None, DEC_ROWS, LANES), lambda b: (b, 0, 0)),
                  pl.BlockSpec((CONV_W, B_QKV), lambda b: (0, 0)),
                  pl.BlockSpec((SUBLANES, LANES), lambda b: (0, 0)),
                  pl.BlockSpec((1, B_DV), lambda b: (0, 0)),
                  pl.BlockSpec((4, A_DH), lambda b: (0, 0)),
                  pl.BlockSpec((1, A_DV), lambda b: (0, 0))],
        out_specs=[pl.BlockSpec((None, 1, A_HEADS * A_DV), lambda b: (b, 0, 0)),
                   pl.BlockSpec((None, 1, hw), lambda b: (b, 0, 0)),
                   pl.BlockSpec((None, B_HEADS, B_DK, B_DV), lambda b: (b, 0, 0, 0))],
        out_shape=[jax.ShapeDtypeStruct((db, 1, A_HEADS * A_DV), BF16),
                   jax.ShapeDtypeStruct((db, 1, hw), BF16),
                   jax.ShapeDtypeStruct((db, B_HEADS, B_DK, B_DV), F32)],
        compiler_params=_cparams(("parallel",)),
        name="even_tail",
    )(zf_s.reshape(db, 1, npad), conv_prev, s_prev, oa, conv_w, gp, norm_g.reshape(1, B_DV), lamv,
      subln.reshape(1, A_DV))


def _odd_tail_kernel(z_ref, oc_ref, y_ref):
    z = z_ref[...]
    oc = oc_ref[...]
    for hg in range(C_HEADS):
        gate = z[:, ODD_OFF[3] + hg * C_DH:ODD_OFF[3] + (hg + 1) * C_DH]
        y_ref[:, hg * C_DH:(hg + 1) * C_DH] = (oc[hg:hg + 1] * _silu(gate)).astype(BF16)


def _odd_tail(zf_s, oc):
    db, npad = zf_s.shape
    return pl.pallas_call(
        _odd_tail_kernel,
        grid=(db,),
        in_specs=[pl.BlockSpec((None, 1, npad), lambda b: (b, 0, 0)),
                  pl.BlockSpec((None, DEC_ROWS, LANES), lambda b: (b, 0, 0))],
        out_specs=pl.BlockSpec((None, 1, C_HEADS * C_DH), lambda b: (b, 0, 0)),
        out_shape=jax.ShapeDtypeStruct((db, 1, C_HEADS * C_DH), BF16),
        compiler_params=_cparams(("parallel",)),
        name="odd_tail",
    )(zf_s.reshape(db, 1, npad), oc)


def _bias_by_distance(table):
    n = jnp.arange(FAR_DIST + 1)
    exact = N_BUCKETS // 2
    nf = jnp.maximum(n, 1).astype(F32)
    large = exact + (jnp.log(nf / exact) / math.log(MAX_DIST / exact) * (N_BUCKETS - exact)).astype(I32)
    bucket = jnp.where(n < exact, n, jnp.minimum(large, N_BUCKETS - 1))
    return table[bucket].astype(F32)


def _prompt_near_tiles(bd, group):
    r = jnp.arange(BLK)[:, None]
    c = jnp.arange(BLK)[None, :]
    d0 = r - c
    rel = (bd - bd[FAR_DIST][None, :]) * LOG2E
    t_prev = rel[jnp.minimum(BLK + d0, FAR_DIST)]
    t_diag = jnp.where((d0 >= 0)[..., None], rel[jnp.clip(d0, 0, FAR_DIST)], NEG)
    tiles = jnp.stack([t_prev, t_diag])
    tiles = jnp.transpose(tiles, (3, 0, 1, 2))
    n_kv = bd.shape[1] // group
    tiles = tiles.reshape(n_kv, group, 2, BLK, BLK)
    return jnp.transpose(tiles, (0, 2, 1, 3, 4)).reshape(n_kv, 2, group * BLK, BLK)


def _decode_bias(bd, row_heads, past, page, g):
    heads = jnp.asarray(row_heads, I32)
    far = jnp.broadcast_to(bd[FAR_DIST][heads][:, None], (len(row_heads), g * page))
    pos = past - g * page + jnp.arange(g * page)
    dist = jnp.minimum(past - pos, FAR_DIST)
    last = bd[dist][:, heads].T
    new = jnp.broadcast_to(bd[0][heads][:, None], (len(row_heads), LANES))
    return far.astype(F32), last.astype(F32), new.astype(F32)


def _pad_cols(w, mult):
    n = w.shape[1]
    return jnp.pad(w, ((0, 0), (0, _round_up(n, mult) - n)))


def kernel(x_prompt, x_sample, cache_a_k, cache_a_v, state_b_s, state_b_conv, cache_c_k, cache_c_v, cache_c_idx,
           page_table, meta, bias_table, final_norm, norm_e, w_in_e, w_out_e, lam_q1, lam_k1, lam_q2, lam_k2,
           subln_a, conv_b, a_log_b, dt_bias_b, norm_b, norm_o, w_in_o, w_out_o):
    nb, seq, d = x_prompt.shape
    n_meta = meta.shape[0]
    l = seq + n_meta
    lp = _round_up(l, CHUNK)
    nq_real = pl.cdiv(l, BLK)
    db = x_sample.shape[0]
    n_pages = page_table.shape[1]
    page = cache_a_k.shape[2]
    past = n_pages * page
    n_pool = cache_a_k.shape[1]
    assert x_sample.shape[1] == 1 and norm_e.shape[0] == 1 and norm_o.shape[0] == 1
    lam_init = 0.8 - 0.6 * math.exp(-0.3 * 0)

    tn_e, tn_o = 8 * LANES, 5 * LANES
    w_e = _pad_cols(w_in_e[0], tn_e).astype(BF16)
    w_o = _pad_cols(w_in_o[0], tn_o).astype(BF16)
    w_out_a = w_out_e[0][:A_HEADS * A_DV].astype(BF16)
    w_out_b = w_out_e[0][A_HEADS * A_DV:].astype(BF16)
    w_out_c = w_out_o[0].astype(BF16)
    tm = CHUNK
    dbp = _round_up(db, SUBLANES)

    bd = _bias_by_distance(bias_table)
    tiles_a = _prompt_near_tiles(bd, A_GROUP)
    tiles_c = _prompt_near_tiles(bd, C_GROUP)
    lamv = jnp.stack([lam_q1[0], lam_k1[0], lam_q2[0], lam_k2[0]]).astype(F32)

    hp = jnp.concatenate([jnp.broadcast_to(meta.astype(F32)[None], (nb, n_meta, d)), x_prompt], axis=1)
    hp = jnp.pad(hp, ((0, 0), (0, lp - l), (0, 0))).reshape(nb * lp, d)
    zf, zb = _norm_proj(hp, norm_e[0], w_e, tm, tn_e)
    ya = _diff_attn_prompt(zf, zb, tiles_a, lamv, subln_a[0], nb, lp, nq_real, lam_init)
    yb, pb_s = _gdn_prompt(zf, conv_b[0], a_log_b[0], dt_bias_b[0], norm_b[0], nb, lp, l)
    h1 = _out_proj([ya, yb], [w_out_a, w_out_b], hp, tm)
    z3 = zf.reshape(nb, lp, -1)
    pa_k = z3[:, :l, EVEN_OFF[1]:EVEN_OFF[2]].reshape(1, nb, l, A_KV, 2, A_DH)
    pa_v = z3[:, :l, EVEN_OFF[2]:EVEN_OFF[3]].reshape(1, nb, l, A_KV, A_DV)
    pb_conv = z3[:, l - (CONV_W - 1):l, EVEN_OFF[4]:EVEN_OFF[5]][None]

    zf1, zb1 = _norm_proj(h1, norm_o[0], w_o, tm, tn_o)
    k_sel_p = min(TOPK_MAX, l // 4)
    yc = _sparse_attn_prompt(zf1, zb1, tiles_c, nb, lp, nq_real, k_sel_p)
    yp = _out_proj([yc], [w_out_c], h1, tm, final_gain=final_norm)
    y_prompt = yp.reshape(nb, lp, d)[:, n_meta:l]
    z13 = zf1.reshape(nb, lp, -1)
    pc_k = z13[:, :l, ODD_OFF[1]:ODD_OFF[2]].reshape(1, nb, l, C_KV, C_DH)
    pc_v = z13[:, :l, ODD_OFF[2]:ODD_OFF[3]].reshape(1, nb, l, C_KV, C_DH)
    pc_idx = z13[:, :l, ODD_OFF[5]:ODD_OFF[6]][None]

    hs = jnp.pad(x_sample.reshape(db, d), ((0, dbp - db), (0, 0)))
    zs, _ = _norm_proj(hs, norm_e[0], w_e, dbp, tn_e)
    zs = zs[:db]
    qa = zs[:, :EVEN_OFF[1]].reshape(db, A_KV, A_GROUP, 2, A_DH) * (A_DH ** -0.5)
    qprime = jnp.einsum('bhgcd,hi,cj->bhgcijd', qa, jnp.eye(A_KV, dtype=F32), jnp.eye(2, dtype=F32))
    qprime = qprime.reshape(db, DEC_ROWS, A_KV * 2 * A_DH).astype(BF16)
    k_new = zs[:, EVEN_OFF[1]:EVEN_OFF[2]].reshape(db, 1, -1)
    v_new = zs[:, EVEN_OFF[2]:EVEN_OFF[3]].reshape(db, 1, -1)
    g_dec = _pages_per_step(n_pages)
    rows_a = [r // 2 for r in range(DEC_ROWS)]
    bfar, blast, bnew = _decode_bias(bd, rows_a, past, page, g_dec)
    ak_t = jnp.transpose(cache_a_k[0], (0, 2, 3, 4, 1)).reshape(n_pool, A_KV * 2 * A_DH, page)
    av_r = cache_a_v[0].reshape(n_pool, page * A_KV, A_DV)
    oa = _paged_attention(qprime, k_new, v_new, bfar, blast, bnew, ak_t, av_r, page_table, page,
                          k_feature_major=True)
    ya_s, yb_s, sb_s = _even_tail(zs, state_b_conv[0], state_b_s[0], oa, conv_b[0], a_log_b[0], dt_bias_b[0],
                                  norm_b[0], lamv, subln_a[0], lam_init)
    hs_pad = lambda y: jnp.pad(y.reshape(db, -1), ((0, dbp - db), (0, 0)))
    hs1 = _out_proj([hs_pad(ya_s), hs_pad(yb_s)], [w_out_a, w_out_b], hs, dbp)
    sa_k = zs[:, EVEN_OFF[1]:EVEN_OFF[2]].reshape(1, db, 1, A_KV, 2, A_DH)
    sa_v = zs[:, EVEN_OFF[2]:EVEN_OFF[3]].reshape(1, db, 1, A_KV, A_DV)
    sb_conv = jnp.concatenate([state_b_conv[0][:, 1:], zs[:, None, EVEN_OFF[4]:EVEN_OFF[5]]], axis=1)[None]

    zs1, _ = _norm_proj(hs1, norm_o[0], w_o, dbp, tn_o)
    zs1 = zs1[:db]
    qidx = (zs1[:, ODD_OFF[4]:ODD_OFF[5]].reshape(db, IDX_HEADS, IDX_DIM) * (IDX_DIM ** -0.5)).astype(BF16)
    wcol = jnp.broadcast_to((zs1[:, ODD_OFF[6]:ODD_OFF[7]] * (IDX_HEADS ** -0.5))[:, :, None],
                            (db, IDX_HEADS, LANES))
    ki_new = zs1[:, ODD_OFF[5]:ODD_OFF[6]]
    ki_tile = jnp.zeros((db, IDX_DIM, page), F32).at[:, :, 0].set(ki_new)
    ci_t = jnp.transpose(cache_c_idx[0], (0, 2, 1))
    keys = _paged_index_scores(qidx, wcol, ki_tile, ci_t, page_table)
    k_sel_s = min(TOPK_MAX, (past + 1) // 4)
    tau, jmax = _select_rows(keys.reshape(db, -1), k_sel_s)
    tau, jmax = tau[:, 0], jmax[:, 0]
    key_new = keys[:, 0, past]
    selnew = ((key_new > tau) | ((key_new == tau) & (past <= jmax))).astype(I32)
    qc = zs1[:, :ODD_OFF[1]].reshape(db, C_KV, C_GROUP, C_DH) * (C_DH ** -0.5)
    qcp = jnp.einsum('bhgd,hi->bhgid', qc, jnp.eye(C_KV, dtype=F32)).reshape(db, C_HEADS, C_KV * C_DH)
    qcp = jnp.pad(qcp, ((0, 0), (0, DEC_ROWS - C_HEADS), (0, 0))).astype(BF16)
    kc_new = zs1[:, ODD_OFF[1]:ODD_OFF[2]].reshape(db, 1, -1)
    vc_new = zs1[:, ODD_OFF[2]:ODD_OFF[3]].reshape(db, 1, -1)
    rows_c = [r if r < C_HEADS else 0 for r in range(DEC_ROWS)]
    cfar, clast, cnew = _decode_bias(bd, rows_c, past, page, g_dec)
    ck_r = cache_c_k[0].reshape(n_pool, page * C_KV, C_DH)
    cv_r = cache_c_v[0].reshape(n_pool, page * C_KV, C_DH)
    oc = _paged_attention(qcp, kc_new, vc_new, cfar, clast, cnew, ck_r, cv_r, page_table, page,
                          k_feature_major=False, mask_args=(keys, tau, jmax, selnew))
    yc_s = _odd_tail(zs1, oc)
    ys = _out_proj([hs_pad(yc_s)], [w_out_c], hs1, dbp, final_gain=final_norm)
    y_sample = ys[:db].reshape(db, 1, d)
    sc_k = zs1[:, ODD_OFF[1]:ODD_OFF[2]].reshape(1, db, 1, C_KV, C_DH)
    sc_v = zs1[:, ODD_OFF[2]:ODD_OFF[3]].reshape(1, db, 1, C_KV, C_DH)
    sc_idx = zs1[:, None, ODD_OFF[5]:ODD_OFF[6]][None]

    return (y_prompt, y_sample, pa_k, pa_v, pb_s[None], pb_conv, pc_k, pc_v, pc_idx,
            sa_k, sa_v, sb_s[None], sb_conv, sc_k, sc_v, sc_idx)
```

```python
import functools
import math

import jax
import jax.numpy as jnp
import numpy as np
from jax import lax
from jax.experimental import pallas as pl
from jax.experimental.pallas import tpu as pltpu

F32 = jnp.float32
BF16 = jnp.bfloat16
I32 = jnp.int32

EPS = 1e-6
N_BUCKETS = 32
MAX_DIST = 128
FAR_DIST = MAX_DIST

A_HEADS, A_KV, A_GROUP, A_DH, A_DV = 8, 2, 4, 64, 128
B_HEADS, B_DK, B_DV, CONV_W = 4, 128, 128, 4
C_HEADS, C_KV, C_GROUP, C_DH = 8, 2, 4, 128
IDX_HEADS, IDX_DIM, TOPK_MAX = 8, 64, 256
B_QKV = 2 * B_HEADS * B_DK + B_HEADS * B_DV

EVEN_COLS = (A_HEADS * 2 * A_DH, A_KV * 2 * A_DH, A_KV * A_DV, A_HEADS * A_DV, B_QKV, B_HEADS * B_DV, B_HEADS, B_HEADS)
ODD_COLS = (C_HEADS * C_DH, C_KV * C_DH, C_KV * C_DH, C_HEADS * C_DH, IDX_HEADS * IDX_DIM, IDX_DIM, IDX_HEADS)
EVEN_OFF = tuple(int(v) for v in np.cumsum((0,) + EVEN_COLS))
ODD_OFF = tuple(int(v) for v in np.cumsum((0,) + ODD_COLS))

LANES = 128
SUBLANES = 8
BLK = 128
CHUNK = 512
SUBS = CHUNK // BLK
NEG = -1e30
INT_MIN = -2 ** 31
INT_MAX = 2 ** 31 - 1
LOG2E = math.log2(math.e)
VMEM_LIMIT = 56 * 1024 * 1024
HI = lax.Precision.HIGHEST

NT = (((1,), (1,)), ((), ()))


def _round_up(x, m):
    return (x + m - 1) // m * m


def _cparams(sem):
    return pltpu.CompilerParams(dimension_semantics=sem, vmem_limit_bytes=VMEM_LIMIT)


def _silu(x):
    return x * jax.nn.sigmoid(x)


def _sortable_key(score):
    score = jnp.where(score == 0.0, 0.0, score)
    bits = pltpu.bitcast(score, I32)
    return bits ^ ((bits >> 31) & jnp.int32(0x7FFFFFFF))


def _lane_tile(x, n):
    return jnp.concatenate([x] * n, axis=1)


def _proj_kernel(x_ref, g_ref, w_ref, of_ref, ob_ref, xn_ref):
    @pl.when(pl.program_id(1) == 0)
    def _():
        xf = x_ref[...]
        y = xf * lax.rsqrt(jnp.mean(xf * xf, axis=-1, keepdims=True) + EPS)
        xn_ref[...] = (y * g_ref[...]).astype(BF16)

    acc = jnp.dot(xn_ref[...], w_ref[...], preferred_element_type=F32)
    of_ref[...] = acc
    ob_ref[...] = acc.astype(BF16)


def _norm_proj(x, gain, w_bf16, tm, tn):
    m, d = x.shape
    n = w_bf16.shape[1]
    return pl.pallas_call(
        _proj_kernel,
        grid=(m // tm, n // tn),
        in_specs=[pl.BlockSpec((tm, d), lambda i, j: (i, 0)),
                  pl.BlockSpec((1, d), lambda i, j: (0, 0)),
                  pl.BlockSpec((d, tn), lambda i, j: (0, j))],
        out_specs=[pl.BlockSpec((tm, tn), lambda i, j: (i, j)),
                   pl.BlockSpec((tm, tn), lambda i, j: (i, j))],
        out_shape=[jax.ShapeDtypeStruct((m, n), F32), jax.ShapeDtypeStruct((m, n), BF16)],
        scratch_shapes=[pltpu.VMEM((tm, d), BF16)],
        compiler_params=_cparams(("parallel", "arbitrary")),
        name="norm_proj",
    )(x, gain.reshape(1, d), w_bf16)


def _outproj_kernel(*refs, n_lhs, final_norm):
    ys = refs[:n_lhs]
    ws = refs[n_lhs:2 * n_lhs]
    h_ref = refs[2 * n_lhs]
    pos = 2 * n_lhs + 1
    fn_ref = refs[pos] if final_norm else None
    out_ref = refs[-1]
    acc = h_ref[...]
    for y_ref, w_ref in zip(ys, ws):
        acc = acc + jnp.dot(y_ref[...], w_ref[...], preferred_element_type=F32)
    if final_norm:
        y = acc * lax.rsqrt(jnp.mean(acc * acc, axis=-1, keepdims=True) + EPS)
        acc = y * fn_ref[...]
    out_ref[...] = acc


def _out_proj(ys, ws, h, tm, final_gain=None):
    m, n = h.shape
    n_lhs = len(ys)
    in_specs = [pl.BlockSpec((tm, y.shape[1]), lambda i: (i, 0)) for y in ys]
    in_specs += [pl.BlockSpec(w.shape, lambda i: (0, 0)) for w in ws]
    in_specs += [pl.BlockSpec((tm, n), lambda i: (i, 0))]
    args = list(ys) + list(ws) + [h]
    if final_gain is not None:
        in_specs += [pl.BlockSpec((1, n), lambda i: (0, 0))]
        args += [final_gain.reshape(1, n)]
    return pl.pallas_call(
        functools.partial(_outproj_kernel, n_lhs=n_lhs, final_norm=final_gain is not None),
        grid=(m // tm,),
        in_specs=in_specs,
        out_specs=pl.BlockSpec((tm, n), lambda i: (i, 0)),
        out_shape=jax.ShapeDtypeStruct((m, n), F32),
        compiler_params=_cparams(("parallel",)),
        name="out_proj",
    )(*args)


def _flash_step(s, v1, m_old, acc):
    m_new = jnp.maximum(m_old, jnp.max(s, axis=1, keepdims=True))
    alpha = jnp.exp2(m_old - m_new)
    p = jnp.exp2(s - _lane_tile(m_new, s.shape[1] // LANES))
    acc = _lane_tile(alpha, 2) * acc + jnp.dot(p.astype(BF16), v1, preferred_element_type=F32)
    return m_new, acc


def _far_chunk_loop(n_far, run_chunks):
    def pair(i, carry):
        run_chunks([2 * i, 2 * i + 1])
        return carry

    lax.fori_loop(0, lax.shift_right_logical(n_far, 1), pair, 0)

    @pl.when(jnp.bitwise_and(n_far, 1) == 1)
    def _():
        run_chunks([n_far - 1])


def _build_near_bias(strip_ref, t_prev, t_diag, r, lead=()):
    for sb in range(2 * SUBS):
        rel = sb - SUBS - r
        tile = jnp.where(rel == 0, t_diag, jnp.where(rel == -1, t_prev, jnp.where(rel < -1, 0.0, NEG)))
        strip_ref[lead + (sb,)] = tile


def _diff_attn_kernel(q_ref, k_ref, v_ref, gate_ref, tiles_ref, lamv_ref, subln_ref, out_ref,
                      qm_ref, strip_ref, m_ref, acc_ref, *, lam_init, nq_real):
    qi = pl.program_id(2)

    @pl.when(qi >= nq_real)
    def _():
        out_ref[...] = jnp.zeros(out_ref.shape, out_ref.dtype)

    @pl.when(qi < nq_real)
    def _():
        lane = lax.broadcasted_iota(I32, (BLK, 2 * A_DH), 1)
        for g in range(A_GROUP):
            qg = q_ref[:, g * 2 * A_DH:(g + 1) * 2 * A_DH] * (A_DH ** -0.5 * LOG2E)
            for c in range(2):
                keep = (lane < A_DH) if c == 0 else (lane >= A_DH)
                qm_ref[c, g * BLK:(g + 1) * BLK, :] = jnp.where(keep, qg, 0.0).astype(BF16)
        m_ref[...] = jnp.full(m_ref.shape, NEG, F32)
        acc_ref[...] = jnp.zeros(acc_ref.shape, F32)
        r = jnp.bitwise_and(qi, SUBS - 1)
        jd = lax.shift_right_logical(qi, int(math.log2(SUBS)))
        _build_near_bias(strip_ref, tiles_ref[0], tiles_ref[1], r)
        ones = jnp.ones((CHUNK, A_DV), BF16)

        def run_chunks(js, first_sbs=None):
            first_sbs = first_sbs or [None] * len(js)
            kv = []
            for j in js:
                start = pl.multiple_of(j * CHUNK, CHUNK)
                kv.append((k_ref[pl.ds(start, CHUNK), :],
                           jnp.concatenate([v_ref[pl.ds(start, CHUNK), :], ones], axis=1)))
            for c in range(2):
                m, acc = m_ref[c], acc_ref[c]
                for (k, v1), first_sb in zip(kv, first_sbs):
                    s = lax.dot_general(qm_ref[c], k, NT, preferred_element_type=F32)
                    if first_sb is not None:
                        s = s + jnp.concatenate([strip_ref[first_sb + i] for i in range(SUBS)], axis=1)
                    m, acc = _flash_step(s, v1, m, acc)
                m_ref[c], acc_ref[c] = m, acc

        _far_chunk_loop(jnp.maximum(jd - 1, 0), run_chunks)

        @pl.when(jd >= 1)
        def _():
            run_chunks([jd - 1, jd], [0, SUBS])

        @pl.when(jd == 0)
        def _():
            run_chunks([jd], [SUBS])

        lv = lamv_ref[...]
        lam = (jnp.exp(jnp.sum(lv[0:1] * lv[1:2], axis=1, keepdims=True))
               - jnp.exp(jnp.sum(lv[2:3] * lv[3:4], axis=1, keepdims=True)) + lam_init)
        a0 = acc_ref[0]
        a1 = acc_ref[1]
        o = a0[:, :A_DV] / a0[:, A_DV:] - lam * (a1[:, :A_DV] / a1[:, A_DV:])
        y = o * lax.rsqrt(jnp.mean(o * o, axis=-1, keepdims=True) + EPS)
        y = (y * subln_ref[...]) * (1.0 - lam_init)
        for g in range(A_GROUP):
            gate = gate_ref[:, g * A_DV:(g + 1) * A_DV]
            out_ref[:, g * A_DV:(g + 1) * A_DV] = (y[g * BLK:(g + 1) * BLK] * _silu(gate)).astype(BF16)


def _diff_attn_prompt(zf, zb, near_tiles, lamv, subln, nb, lp, nq_real, lam_init):
    nq = lp // BLK
    qw = A_GROUP * 2 * A_DH
    k_blk0 = EVEN_OFF[1] // (2 * A_DH)
    v_blk0 = EVEN_OFF[2] // A_DV
    g_blk0 = EVEN_OFF[3] // (A_GROUP * A_DV)
    rows = A_GROUP * BLK
    return pl.pallas_call(
        functools.partial(_diff_attn_kernel, lam_init=lam_init, nq_real=nq_real),
        grid=(nb, A_KV, nq),
        in_specs=[pl.BlockSpec((BLK, qw), lambda b, h, i: (b * nq + i, h)),
                  pl.BlockSpec((lp, 2 * A_DH), lambda b, h, i: (b, k_blk0 + h)),
                  pl.BlockSpec((lp, A_DV), lambda b, h, i: (b, v_blk0 + h)),
                  pl.BlockSpec((BLK, A_GROUP * A_DV), lambda b, h, i: (b * nq + i, g_blk0 + h)),
                  pl.BlockSpec((None, 2, rows, BLK), lambda b, h, i: (h, 0, 0, 0)),
                  pl.BlockSpec((4, A_DH), lambda b, h, i: (0, 0)),
                  pl.BlockSpec((1, A_DV), lambda b, h, i: (0, 0))],
        out_specs=pl.BlockSpec((BLK, A_GROUP * A_DV), lambda b, h, i: (b * nq + i, h)),
        out_shape=jax.ShapeDtypeStruct((nb * lp, A_HEADS * A_DV), BF16),
        scratch_shapes=[pltpu.VMEM((2, rows, 2 * A_DH), BF16),
                        pltpu.VMEM((2 * SUBS, rows, BLK), F32),
                        pltpu.VMEM((2, rows, LANES), F32),
                        pltpu.VMEM((2, rows, 2 * A_DV), F32)],
        compiler_params=_cparams(("parallel", "parallel", "arbitrary")),
        name="diff_attn_prompt",
    )(zf, zb, zb, zf, near_tiles, lamv, subln.reshape(1, A_DV))


def _softplus(x):
    return jnp.maximum(x, 0.0) + jnp.log1p(jnp.exp(-jnp.abs(x)))


def _split_bf16(a):
    hi = a.astype(BF16)
    return hi, (a - hi.astype(F32)).astype(BF16)


def _dot3(a, b):
    (ah, al), (bh, bl) = a, b
    return (jnp.dot(ah, bh, preferred_element_type=F32)
            + (jnp.dot(ah, bl, preferred_element_type=F32) + jnp.dot(al, bh, preferred_element_type=F32)))


def _gdn_chunk_math(s0, q, k, v, gb, betab, row_ge, row_gt):
    c = q.shape[0]
    ltri = jnp.where(row_ge, 1.0, 0.0).astype(F32)
    gcum = jnp.dot(ltri, gb, precision=HI, preferred_element_type=F32)
    diff = gcum - gcum.T
    decay = jnp.where(row_ge, jnp.exp(jnp.where(row_ge, diff, 0.0)), 0.0)
    kb = k.astype(BF16)
    qb = q.astype(BF16)
    s0b = s0.astype(BF16)
    kk = lax.dot_general(kb, kb, NT, preferred_element_type=F32)
    x = -jnp.where(row_gt, betab * decay * kk, 0.0)
    tm = x
    p = x
    for _ in range(int(math.log2(c)) - 1):
        ps = _split_bf16(p)
        p = _dot3(ps, ps)
        tm = tm + p + _dot3(_split_bf16(tm), _split_bf16(p))
    eg = jnp.exp(gcum)
    ks = jnp.dot(kb, s0b, preferred_element_type=F32)
    rhs = betab * (v - eg * ks)
    u = rhs + jnp.dot(tm.astype(BF16), rhs.astype(BF16), preferred_element_type=F32)
    ub = u.astype(BF16)
    qk = lax.dot_general(qb, kb, NT, preferred_element_type=F32) * decay
    o = eg * jnp.dot(qb, s0b, preferred_element_type=F32) + jnp.dot(qk.astype(BF16), ub, preferred_element_type=F32)
    glast = gcum[c - 1:c, :]
    kd = k * jnp.exp(glast - gcum)
    s_new = jnp.exp(glast) * s0 + jnp.dot(kd.T.astype(BF16), ub, preferred_element_type=F32)
    return s_new, o


def _gdn_prompt_kernel(xq_ref, xk_ref, xv_ref, gate_ref, ab_ref, convw_ref, gp_ref, normg_ref,
                       y_ref, sfin_ref, xbuf_ref, s_ref, *, seq_len):
    ci = pl.program_id(0)
    hw = B_HEADS * B_DK
    nb = xq_ref.shape[0]

    @pl.when(ci == 0)
    def _():
        xbuf_ref[:, 0:SUBLANES, :] = jnp.zeros((nb, SUBLANES, B_QKV), F32)
        s_ref[...] = jnp.zeros(s_ref.shape, F32)

    row = lax.broadcasted_iota(I32, (BLK, BLK), 0)
    col = lax.broadcasted_iota(I32, (BLK, BLK), 1)
    row_ge = row >= col
    row_gt = row > col
    valid = (ci * BLK + row) < seq_len
    gp = gp_ref[...]
    for b in range(nb):
        xbuf_ref[b, SUBLANES:SUBLANES + BLK, 0:hw] = xq_ref[b]
        xbuf_ref[b, SUBLANES:SUBLANES + BLK, hw:2 * hw] = xk_ref[b]
        xbuf_ref[b, SUBLANES:SUBLANES + BLK, 2 * hw:3 * hw] = xv_ref[b]
        conv = jnp.zeros((BLK, B_QKV), F32)
        for i in range(CONV_W):
            conv = conv + convw_ref[i:i + 1, :] * xbuf_ref[b, pl.ds(SUBLANES - (CONV_W - 1) + i, BLK), :]
        tail = xbuf_ref[b, BLK:BLK + SUBLANES, :]
        xbuf_ref[b, 0:SUBLANES, :] = tail
        act = _silu(conv)
        ab = ab_ref[b]
        for h in range(B_HEADS):
            q = act[:, h * B_DK:(h + 1) * B_DK]
            k = act[:, hw + h * B_DK:hw + (h + 1) * B_DK]
            v = act[:, 2 * hw + h * B_DV:2 * hw + (h + 1) * B_DV]
            q = q * lax.rsqrt(jnp.sum(q * q, axis=-1, keepdims=True) + EPS) * (B_DK ** -0.5)
            k = k * lax.rsqrt(jnp.sum(k * k, axis=-1, keepdims=True) + EPS)
            a_raw = jnp.broadcast_to(ab[:, h:h + 1], (BLK, BLK))
            b_raw = jnp.broadcast_to(ab[:, B_HEADS + h:B_HEADS + h + 1], (BLK, BLK))
            a_log = gp[0:1, h:h + 1]
            dt_b = gp[1:2, h:h + 1]
            gb = jnp.where(valid, -jnp.exp(a_log) * _softplus(a_raw + dt_b), 0.0)
            betab = jnp.where(valid, jax.nn.sigmoid(b_raw), 0.0)
            s_new, o = _gdn_chunk_math(s_ref[b, h], q, k, v, gb, betab, row_ge, row_gt)
            s_ref[b, h] = s_new
            y = o * lax.rsqrt(jnp.mean(o * o, axis=-1, keepdims=True) + EPS) * normg_ref[...]
            gate = gate_ref[b, :, h * B_DV:(h + 1) * B_DV]
            y_ref[b, :, h * B_DV:(h + 1) * B_DV] = (y * _silu(gate)).astype(BF16)

    @pl.when(ci == pl.num_programs(0) - 1)
    def _():
        sfin_ref[...] = s_ref[...]


def _gdn_prompt(zf, conv_w, a_log, dt_bias, norm_g, nb, lp, seq_len):
    nc = lp // BLK
    hw = B_HEADS * B_DK
    c0 = EVEN_OFF[4] // hw
    g0 = EVEN_OFF[5] // hw
    ab0 = EVEN_OFF[6] // LANES
    gp = jnp.zeros((SUBLANES, LANES), F32).at[0, :B_HEADS].set(a_log).at[1, :B_HEADS].set(dt_bias)
    z3 = zf.reshape(nb, lp, zf.shape[1])
    y, s_fin = pl.pallas_call(
        functools.partial(_gdn_prompt_kernel, seq_len=seq_len),
        grid=(nc,),
        in_specs=[pl.BlockSpec((nb, BLK, hw), lambda i: (0, i, c0)),
                  pl.BlockSpec((nb, BLK, hw), lambda i: (0, i, c0 + 1)),
                  pl.BlockSpec((nb, BLK, hw), lambda i: (0, i, c0 + 2)),
                  pl.BlockSpec((nb, BLK, hw), lambda i: (0, i, g0)),
                  pl.BlockSpec((nb, BLK, LANES), lambda i: (0, i, ab0)),
                  pl.BlockSpec((CONV_W, B_QKV), lambda i: (0, 0)),
                  pl.BlockSpec((SUBLANES, LANES), lambda i: (0, 0)),
                  pl.BlockSpec((1, B_DV), lambda i: (0, 0))],
        out_specs=[pl.BlockSpec((nb, BLK, hw), lambda i: (0, i, 0)),
                   pl.BlockSpec((nb, B_HEADS, B_DK, B_DV), lambda i: (0, 0, 0, 0))],
        out_shape=[jax.ShapeDtypeStruct((nb, lp, hw), BF16),
                   jax.ShapeDtypeStruct((nb, B_HEADS, B_DK, B_DV), F32)],
        scratch_shapes=[pltpu.VMEM((nb, BLK + SUBLANES, B_QKV), F32),
                        pltpu.VMEM((nb, B_HEADS, B_DK, B_DV), F32)],
        compiler_params=_cparams(("arbitrary",)),
        name="gdn_prompt",
    )(z3, z3, z3, z3, z3, conv_w, gp, norm_g.reshape(1, B_DV))
    return y.reshape(nb * lp, hw), s_fin


def _kth_largest(count_fn, k_sel, shape):
    def step(i, tau):
        cand = tau + lax.shift_left(jnp.int32(1), 31 - i)
        cnt = count_fn(lambda key, idx: key >= cand)
        return jnp.where(cnt >= k_sel, cand, tau)

    return lax.fori_loop(0, 32, step, jnp.full(shape, INT_MIN, I32))


def _tie_cutoff(count_fn, tau, need, n_index_bits, shape):
    def step(i, jm):
        cand = jm | lax.shift_left(jnp.int32(1), n_index_bits - 1 - i)
        cnt = count_fn(lambda key, idx: (key == tau) & (idx < cand))
        return jnp.where(cnt < need, cand, jm)

    return lax.fori_loop(0, n_index_bits, step, jnp.zeros(shape, I32))


def _sparse_attn_kernel(qi_ref, kw_ref, kidx_ref, qc_ref, kc_ref, vc_ref, g0_ref, g1_ref, tiles_ref, out_ref,
                        keys_ref, qm_ref, qh_ref, strip_ref, tau_ref, jmax_ref, m_ref, acc_ref,
                        *, k_sel, n_index_bits, nq_real):
    qb = pl.program_id(1)

    @pl.when(qb >= nq_real)
    def _():
        out_ref[...] = jnp.zeros(out_ref.shape, out_ref.dtype)

    @pl.when(qb < nq_real)
    def _():
        lane = lax.broadcasted_iota(I32, (BLK, LANES), 1)
        key_pos = lax.broadcasted_iota(I32, (CHUNK, BLK), 0)
        qry_pos = lax.broadcasted_iota(I32, (CHUNK, BLK), 1)
        r = jnp.bitwise_and(qb, SUBS - 1)
        jd = lax.shift_right_logical(qb, int(math.log2(SUBS)))

        for p in range(IDX_HEADS // 2):
            pair = qi_ref[:, p * LANES:(p + 1) * LANES] * (IDX_DIM ** -0.5)
            swapped = pltpu.roll(pair, IDX_DIM, 1)
            for e, src in enumerate((pair, swapped)):
                h = 2 * p + e
                qm_ref[h * BLK:(h + 1) * BLK, :] = jnp.where(lane < IDX_DIM, src, 0.0).astype(BF16)
        w_rows = kw_ref[...].T * (IDX_HEADS ** -0.5)

        def scores_t(j):
            start = pl.multiple_of(j * CHUNK, CHUNK)
            d = lax.dot_general(kidx_ref[pl.ds(start, CHUNK), :], qm_ref[...], NT, preferred_element_type=F32)
            sc = jnp.zeros((CHUNK, BLK), F32)
            for h in range(IDX_HEADS):
                sc = sc + w_rows[IDX_DIM + h:IDX_DIM + h + 1, :] * jnp.maximum(d[:, h * BLK:(h + 1) * BLK], 0.0)
            return sc

        def score_body(j, carry):
            keys_ref[j] = _sortable_key(scores_t(j))
            return carry

        lax.fori_loop(0, jd, score_body, 0)
        admissible = (jd * CHUNK + key_pos) <= (qb * BLK + qry_pos)
        keys_ref[jd] = _sortable_key(jnp.where(admissible, scores_t(jd), -jnp.inf))

        def count_fn(pred):
            part = SUBLANES * SUBLANES

            def body(j, acc):
                hit = jnp.where(pred(keys_ref[j], j * CHUNK + key_pos), 1.0, 0.0)
                return acc + jnp.sum(hit.reshape(CHUNK // part, part, BLK), axis=0)
            acc = lax.fori_loop(0, jd + 1, body, jnp.zeros((part, BLK), F32))
            return jnp.sum(acc, axis=0, keepdims=True).astype(I32)

        tau = _kth_largest(count_fn, k_sel, (1, BLK))
        tau_ref[...] = tau
        jmax_ref[...] = jnp.full((1, BLK), INT_MAX, I32)
        need = k_sel - count_fn(lambda key, idx: key > tau)
        n_eq = count_fn(lambda key, idx: key == tau)
        excess = jnp.max(jnp.where(n_eq > need, 1.0, 0.0)) > 0.5

        @pl.when(excess)
        def _():
            jmax_ref[...] = _tie_cutoff(count_fn, tau, need, n_index_bits, (1, BLK))

        jmax = jmax_ref[...]

        for h in range(C_KV):
            for g in range(C_GROUP):
                col = (h * C_GROUP + g) * C_DH
                qh_ref[h, g * BLK:(g + 1) * BLK, :] = (
                    qc_ref[:, col:col + C_DH] * (C_DH ** -0.5 * LOG2E)).astype(BF16)
            _build_near_bias(strip_ref, tiles_ref[h, 0], tiles_ref[h, 1], r, lead=(h,))
        m_ref[...] = jnp.full(m_ref.shape, NEG, F32)
        acc_ref[...] = jnp.zeros(acc_ref.shape, F32)
        ones = jnp.ones((CHUNK, C_DH), BF16)

        def run_chunks(js, first_sbs=None):
            first_sbs = first_sbs or [None] * len(js)
            starts, masks = [], []
            for j in js:
                starts.append(pl.multiple_of(j * CHUNK, CHUNK))
                key = keys_ref[j]
                tie_ok = jnp.where((j * CHUNK + key_pos) <= jmax, 0.0, NEG)
                sel_t = jnp.where(key > tau, 0.0, jnp.where(key == tau, tie_ok, NEG))
                selneg = jnp.concatenate([sel_t[i * BLK:(i + 1) * BLK, :].T for i in range(SUBS)], axis=1)
                masks.append(jnp.concatenate([selneg] * C_GROUP, axis=0))
            for h in range(C_KV):
                m, acc = m_ref[h], acc_ref[h]
                for start, selneg, first_sb in zip(starts, masks, first_sbs):
                    k = kc_ref[pl.ds(start, CHUNK), h * C_DH:(h + 1) * C_DH]
                    v1 = jnp.concatenate([vc_ref[pl.ds(start, CHUNK), h * C_DH:(h + 1) * C_DH], ones], axis=1)
                    s = lax.dot_general(qh_ref[h], k, NT, preferred_element_type=F32) + selneg
                    if first_sb is not None:
                        s = s + jnp.concatenate([strip_ref[h, first_sb + i] for i in range(SUBS)], axis=1)
                    m, acc = _flash_step(s, v1, m, acc)
                m_ref[h], acc_ref[h] = m, acc

        _far_chunk_loop(jnp.maximum(jd - 1, 0), run_chunks)

        @pl.when(jd >= 1)
        def _():
            run_chunks([jd - 1, jd], [0, SUBS])

        @pl.when(jd == 0)
        def _():
            run_chunks([jd], [SUBS])

        half = (C_HEADS // 2) * C_DH
        for h in range(C_KV):
            a = acc_ref[h]
            o = a[:, :C_DH] / a[:, C_DH:]
            gref = g0_ref if h == 0 else g1_ref
            for g in range(C_GROUP):
                gate = gref[:, g * C_DH:(g + 1) * C_DH]
                out_ref[:, h * half + g * C_DH:h * half + (g + 1) * C_DH] = (
                    o[g * BLK:(g + 1) * BLK] * _silu(gate)).astype(BF16)


def _sparse_attn_prompt(zf, zb, near_tiles, nb, lp, nq_real, k_sel):
    nq = lp // BLK
    half = (C_HEADS // 2) * C_DH
    qi0 = ODD_OFF[4] // (IDX_HEADS * IDX_DIM)
    ki0 = ODD_OFF[5] // LANES
    kc0 = ODD_OFF[1] // (C_KV * C_DH)
    vc0 = ODD_OFF[2] // (C_KV * C_DH)
    g0 = ODD_OFF[3] // half
    rows = C_GROUP * BLK
    n_index_bits = max(1, int(math.ceil(math.log2(lp))))
    return pl.pallas_call(
        functools.partial(_sparse_attn_kernel, k_sel=k_sel, n_index_bits=n_index_bits, nq_real=nq_real),
        grid=(nb, nq),
        in_specs=[pl.BlockSpec((BLK, IDX_HEADS * IDX_DIM), lambda b, i: (b * nq + i, qi0)),
                  pl.BlockSpec((BLK, LANES), lambda b, i: (b * nq + i, ki0)),
                  pl.BlockSpec((lp, LANES), lambda b, i: (b, ki0)),
                  pl.BlockSpec((BLK, C_HEADS * C_DH), lambda b, i: (b * nq + i, 0)),
                  pl.BlockSpec((lp, C_KV * C_DH), lambda b, i: (b, kc0)),
                  pl.BlockSpec((lp, C_KV * C_DH), lambda b, i: (b, vc0)),
                  pl.BlockSpec((BLK, half), lambda b, i: (b * nq + i, g0)),
                  pl.BlockSpec((BLK, half), lambda b, i: (b * nq + i, g0 + 1)),
                  pl.BlockSpec((C_KV, 2, rows, BLK), lambda b, i: (0, 0, 0, 0))],
        out_specs=pl.BlockSpec((BLK, C_HEADS * C_DH), lambda b, i: (b * nq + i, 0)),
        out_shape=jax.ShapeDtypeStruct((nb * lp, C_HEADS * C_DH), BF16),
        scratch_shapes=[pltpu.VMEM((lp // CHUNK, CHUNK, BLK), I32),
                        pltpu.VMEM((IDX_HEADS * BLK, LANES), BF16),
                        pltpu.VMEM((C_KV, rows, C_DH), BF16),
                        pltpu.VMEM((C_KV, 2 * SUBS, rows, BLK), F32),
                        pltpu.VMEM((1, BLK), I32),
                        pltpu.VMEM((1, BLK), I32),
                        pltpu.VMEM((C_KV, rows, LANES), F32),
                        pltpu.VMEM((C_KV, rows, 2 * C_DH), F32)],
        compiler_params=_cparams(("parallel", "arbitrary")),
        name="sparse_attn_prompt",
    )(zf, zf, zb, zf, zb, zb, zf, zf, near_tiles)


PAGES_PER_STEP = 16
INDEX_PAGES_PER_STEP = 32
DEC_ROWS = 16


def _interleaved_pages(refs, page):
    halves = [jnp.concatenate([r[pl.ds(h, page, stride=2), :] for r in refs], axis=0) for h in range(2)]
    return jnp.concatenate(halves, axis=1).astype(BF16)


def _paged_attn_kernel(*refs, n_pages_step, masked, page, k_feature_major):
    if masked:
        pt_ref, tau_ref, jmax_ref, selnew_ref = refs[:4]
        refs = refs[4:]
    else:
        pt_ref = refs[0]
        refs = refs[1:]
    q_ref, knew_ref, vnew_ref, bfar_ref, blast_ref, b0_ref = refs[:6]
    refs = refs[6:]
    if masked:
        keys_ref = refs[0]
        refs = refs[1:]
    k_refs = refs[:n_pages_step]
    v_refs = refs[n_pages_step:2 * n_pages_step]
    out_ref, m_ref, l_ref, acc_ref = refs[2 * n_pages_step:]
    del pt_ref
    b = pl.program_id(0)
    j = pl.program_id(1)
    width = n_pages_step * page
    q = q_ref[...]

    @pl.when(j == 0)
    def _():
        s_new = jnp.sum(q.astype(F32) * knew_ref[...].astype(BF16).astype(F32), axis=1, keepdims=True) + b0_ref[...]
        v_new = jnp.broadcast_to(vnew_ref[...].astype(BF16).astype(F32), acc_ref.shape)
        if masked:
            take = selnew_ref[b] > 0
            m_ref[...] = jnp.where(take, s_new, NEG)
            l_ref[...] = jnp.where(take, 1.0, 0.0) * jnp.ones(l_ref.shape, F32)
            acc_ref[...] = jnp.where(take, v_new, 0.0)
        else:
            m_ref[...] = s_new
            l_ref[...] = jnp.ones(l_ref.shape, F32)
            acc_ref[...] = v_new

    if k_feature_major:
        kcat = jnp.concatenate([r[...] for r in k_refs], axis=1).astype(BF16)
        s = jnp.dot(q, kcat, preferred_element_type=F32)
    else:
        s = lax.dot_general(q, _interleaved_pages(k_refs, page), NT, preferred_element_type=F32)
    vcat = _interleaved_pages(v_refs, page)
    s = s + jnp.where(j == pl.num_programs(1) - 1, blast_ref[...], bfar_ref[...])
    if masked:
        key = keys_ref[...]
        idx = j * width + lax.broadcasted_iota(I32, (1, width), 1)
        tau = tau_ref[b]
        sel = (key > tau) | ((key == tau) & (idx <= jmax_ref[b]))
        s = jnp.where(sel, s, NEG)
    m_old = m_ref[...]
    m_new = jnp.maximum(m_old, jnp.max(s, axis=1, keepdims=True))
    alpha = jnp.exp(m_old - m_new)
    p = jnp.exp(s - m_new[:, 0:1])
    l_ref[...] = alpha * l_ref[...] + jnp.sum(p, axis=1, keepdims=True)
    acc_ref[...] = _lane_tile(alpha, 2) * acc_ref[...] + jnp.dot(p.astype(BF16), vcat, preferred_element_type=F32)
    m_ref[...] = m_new

    @pl.when(j == pl.num_programs(1) - 1)
    def _():
        o = acc_ref[...]
        rowi = lax.broadcasted_iota(I32, (DEC_ROWS, LANES), 0)
        upper = (rowi >= DEC_ROWS // 2) if not masked else ((rowi >= DEC_ROWS // 4) & (rowi < DEC_ROWS // 2))
        out_ref[...] = jnp.where(upper, o[:, LANES:], o[:, :LANES]) / l_ref[...]


def _pages_per_step(n_pages, g=PAGES_PER_STEP):
    while n_pages % g:
        g //= 2
    return g


def _paged_attention(qprime, k_new, v_new, bias_far, bias_last, bias0, k_cache, v_cache, page_table, page,
                     k_feature_major, mask_args=None):
    db = qprime.shape[0]
    n_pages = page_table.shape[1]
    width = qprime.shape[2]
    g = _pages_per_step(n_pages)
    n_steps = n_pages // g
    masked = mask_args is not None
    n_pref = 4 if masked else 1

    def page_map(gi):
        return lambda b, j, pt, *_: (pt[b * n_pages + j * g + gi], 0, 0)

    in_specs = [pl.BlockSpec((None, DEC_ROWS, width), lambda b, j, *_: (b, 0, 0)),
                pl.BlockSpec((None, 1, width), lambda b, j, *_: (b, 0, 0)),
                pl.BlockSpec((None, 1, width), lambda b, j, *_: (b, 0, 0)),
                pl.BlockSpec((DEC_ROWS, g * page), lambda b, j, *_: (0, 0)),
                pl.BlockSpec((DEC_ROWS, g * page), lambda b, j, *_: (0, 0)),
                pl.BlockSpec((DEC_ROWS, LANES), lambda b, j, *_: (0, 0))]
    args = [qprime, k_new, v_new, bias_far, bias_last, bias0]
    prefetch = [page_table.reshape(-1)]
    if masked:
        keys, tau, jmax, selnew = mask_args
        prefetch += [tau, jmax, selnew]
        in_specs += [pl.BlockSpec((None, 1, g * page), lambda b, j, *_: (b, 0, j))]
        args += [keys]
    in_specs += [pl.BlockSpec((None,) + k_cache.shape[1:], page_map(gi)) for gi in range(g)]
    in_specs += [pl.BlockSpec((None,) + v_cache.shape[1:], page_map(gi)) for gi in range(g)]
    args += [k_cache] * g + [v_cache] * g
    grid_spec = pltpu.PrefetchScalarGridSpec(
        num_scalar_prefetch=n_pref,
        grid=(db, n_steps),
        in_specs=in_specs,
        out_specs=pl.BlockSpec((None, DEC_ROWS, LANES), lambda b, j, *_: (b, 0, 0)),
        scratch_shapes=[pltpu.VMEM((DEC_ROWS, LANES), F32),
                        pltpu.VMEM((DEC_ROWS, LANES), F32),
                        pltpu.VMEM((DEC_ROWS, width), F32)])
    return pl.pallas_call(
        functools.partial(_paged_attn_kernel, n_pages_step=g, masked=masked, page=page,
                          k_feature_major=k_feature_major),
        grid_spec=grid_spec,
        out_shape=jax.ShapeDtypeStruct((db, DEC_ROWS, LANES), F32),
        compiler_params=_cparams(("parallel", "arbitrary")),
        name="paged_attn_masked" if masked else "paged_attn",
    )(*prefetch, *args)


def _paged_index_kernel(pt_ref, q_ref, w_ref, knew_ref, *refs, n_pages_step, page):
    k_refs = refs[:n_pages_step]
    out_ref = refs[n_pages_step]
    del pt_ref
    j = pl.program_id(1)
    last = pl.num_programs(1) - 1
    q = q_ref[...]
    w = w_ref[...]

    def score(kmat_t):
        d = jnp.dot(q, kmat_t.astype(BF16), preferred_element_type=F32)
        return jnp.sum(jnp.maximum(d, 0.0) * w[:, 0:1], axis=0, keepdims=True)

    @pl.when(j < last)
    def _():
        kcat = jnp.concatenate([r[...] for r in k_refs], axis=1)
        out_ref[...] = _sortable_key(score(kcat))

    @pl.when(j == last)
    def _():
        sc = score(knew_ref[...])
        lane = lax.broadcasted_iota(I32, (1, page), 1)
        sc = jnp.where(lane == 0, sc, -jnp.inf)
        pad = jnp.full((1, (n_pages_step - 1) * page), -jnp.inf, F32)
        full = jnp.concatenate([sc, pad], axis=1) if n_pages_step > 1 else sc
        out_ref[...] = _sortable_key(full)


def _paged_index_scores(qidx, wcol, k_new_tile, idx_cache_t, page_table):
    db = qidx.shape[0]
    n_pages = page_table.shape[1]
    page = idx_cache_t.shape[2]
    g = _pages_per_step(n_pages, INDEX_PAGES_PER_STEP)
    n_steps = n_pages // g

    def page_map(gi):
        return lambda b, j, pt: (pt[b * n_pages + jnp.minimum(j, n_steps - 1) * g + gi], 0, 0)

    grid_spec = pltpu.PrefetchScalarGridSpec(
        num_scalar_prefetch=1,
        grid=(db, n_steps + 1),
        in_specs=[pl.BlockSpec((None, IDX_HEADS, IDX_DIM), lambda b, j, pt: (b, 0, 0)),
                  pl.BlockSpec((None, IDX_HEADS, LANES), lambda b, j, pt: (b, 0, 0)),
                  pl.BlockSpec((None, IDX_DIM, page), lambda b, j, pt: (b, 0, 0))]
        + [pl.BlockSpec((None, IDX_DIM, page), page_map(gi)) for gi in range(g)],
        out_specs=pl.BlockSpec((None, 1, g * page), lambda b, j, pt: (b, 0, j)))
    return pl.pallas_call(
        functools.partial(_paged_index_kernel, n_pages_step=g, page=page),
        grid_spec=grid_spec,
        out_shape=jax.ShapeDtypeStruct((db, 1, (n_steps + 1) * g * page), I32),
        compiler_params=_cparams(("parallel", "arbitrary")),
        name="paged_index_scores",
    )(page_table.reshape(-1), qidx, wcol, k_new_tile, *([idx_cache_t] * g))


def _select_kernel(keys_ref, tau_ref, jmax_ref, *, k_sel, n_index_bits):
    keys = keys_ref[...]
    idx = lax.broadcasted_iota(I32, keys.shape, 1)
    shape = (keys.shape[0], 1)

    def count_fn(pred):
        return jnp.sum(jnp.where(pred(keys, idx), 1, 0), axis=1, keepdims=True)

    tau = _kth_largest(count_fn, k_sel, shape)
    need = k_sel - count_fn(lambda key, i: key > tau)
    jmax = _tie_cutoff(count_fn, tau, need, n_index_bits, shape)
    tau_ref[...] = jnp.broadcast_to(tau, tau_ref.shape)
    jmax_ref[...] = jnp.broadcast_to(jmax, jmax_ref.shape)


def _select_rows(keys2d, k_sel):
    rows, width = keys2d.shape
    n_index_bits = max(1, int(math.ceil(math.log2(width))))
    return pl.pallas_call(
        functools.partial(_select_kernel, k_sel=k_sel, n_index_bits=n_index_bits),
        out_shape=[jax.ShapeDtypeStruct((rows, LANES), I32), jax.ShapeDtypeStruct((rows, LANES), I32)],
        compiler_params=pltpu.CompilerParams(vmem_limit_bytes=VMEM_LIMIT),
        name="select_rows",
    )(keys2d)


def _even_tail_kernel(z_ref, conv_ref, s_ref, oa_ref, convw_ref, gp_ref, normg_ref, lamv_ref, subln_ref,
                      ya_ref, yb_ref, snew_ref, *, lam_init):
    z = z_ref[...]
    hw = B_HEADS * B_DK
    lv = lamv_ref[...]
    lam = (jnp.exp(jnp.sum(lv[0:1] * lv[1:2], axis=1, keepdims=True))
           - jnp.exp(jnp.sum(lv[2:3] * lv[3:4], axis=1, keepdims=True)) + lam_init)
    oa = oa_ref[...]
    for hg in range(A_HEADS):
        o = oa[2 * hg:2 * hg + 1] - lam * oa[2 * hg + 1:2 * hg + 2]
        y = o * lax.rsqrt(jnp.mean(o * o, axis=-1, keepdims=True) + EPS)
        y = (y * subln_ref[...]) * (1.0 - lam_init)
        gate = z[:, EVEN_OFF[3] + hg * A_DV:EVEN_OFF[3] + (hg + 1) * A_DV]
        ya_ref[:, hg * A_DV:(hg + 1) * A_DV] = (y * _silu(gate)).astype(BF16)
    x_new = z[:, EVEN_OFF[4]:EVEN_OFF[4] + B_QKV]
    conv = convw_ref[CONV_W - 1:CONV_W, :] * x_new
    cp = conv_ref[...]
    for i in range(CONV_W - 1):
        conv = conv + convw_ref[i:i + 1, :] * cp[i:i + 1, :]
    act = _silu(conv)
    ab = z[:, EVEN_OFF[6]:EVEN_OFF[6] + LANES]
    gp = gp_ref[...]
    row = lax.broadcasted_iota(I32, (B_DK, B_DV), 0)
    col = lax.broadcasted_iota(I32, (B_DK, B_DV), 1)
    eye = row == col
    for h in range(B_HEADS):
        q = act[:, h * B_DK:(h + 1) * B_DK]
        k = act[:, hw + h * B_DK:hw + (h + 1) * B_DK]
        v = act[:, 2 * hw + h * B_DV:2 * hw + (h + 1) * B_DV]
        q = q * lax.rsqrt(jnp.sum(q * q, axis=-1, keepdims=True) + EPS) * (B_DK ** -0.5)
        k = k * lax.rsqrt(jnp.sum(k * k, axis=-1, keepdims=True) + EPS)
        g = -jnp.exp(gp[0:1, h:h + 1]) * _softplus(ab[:, h:h + 1] + gp[1:2, h:h + 1])
        beta = jax.nn.sigmoid(ab[:, B_HEADS + h:B_HEADS + h + 1])
        eg = jnp.exp(g)
        s0 = s_ref[h]
        kcol = jnp.sum(jnp.where(eye, jnp.broadcast_to(k, (B_DK, B_DK)), 0.0), axis=1, keepdims=True)
        qcol = jnp.sum(jnp.where(eye, jnp.broadcast_to(q, (B_DK, B_DK)), 0.0), axis=1, keepdims=True)
        ks = jnp.sum(kcol * s0, axis=0, keepdims=True)
        qs = jnp.sum(qcol * s0, axis=0, keepdims=True)
        u = beta * (v - eg * ks)
        qk = jnp.sum(q * k, axis=1, keepdims=True)
        o = eg * qs + qk * u
        snew_ref[h] = eg * s0 + kcol * u
        y = o * lax.rsqrt(jnp.mean(o * o, axis=-1, keepdims=True) + EPS) * normg_ref[...]
        gate = z[:, EVEN_OFF[5] + h * B_DV:EVEN_OFF[5] + (h + 1) * B_DV]
        yb_ref[:, h * B_DV:(h + 1) * B_DV] = (y * _silu(gate)).astype(BF16)


def _even_tail(zf_s, conv_prev, s_prev, oa, conv_w, a_log, dt_bias, norm_g, lamv, subln, lam_init):
    db, npad = zf_s.shape
    gp = jnp.zeros((SUBLANES, LANES), F32).at[0, :B_HEADS].set(a_log).at[1, :B_HEADS].set(dt_bias)
    hw = B_HEADS * B_DV
    return pl.pallas_call(
        functools.partial(_even_tail_kernel, lam_init=lam_init),
        grid=(db,),
        in_specs=[pl.BlockSpec((None, 1, npad), lambda b: (b, 0, 0)),
                  pl.BlockSpec((None, CONV_W - 1, B_QKV), lambda b: (b, 0, 0)),
                  pl.BlockSpec((None, B_HEADS, B_DK, B_DV), lambda b: (b, 0, 0, 0)),
                  pl.BlockSpec((None, DEC_ROWS, LANES), lambda b: (b, 0, 0)),
                  pl.BlockSpec((CONV_W, B_QKV), lambda b: (0, 0)),
                  pl.BlockSpec((SUBLANES, LANES), lambda b: (0, 0)),
                  pl.BlockSpec((1, B_DV), lambda b: (0, 0)),
                  pl.BlockSpec((4, A_DH), lambda b: (0, 0)),
                  pl.BlockSpec((1, A_DV), lambda b: (0, 0))],
        out_specs=[pl.BlockSpec((None, 1, A_HEADS * A_DV), lambda b: (b, 0, 0)),
                   pl.BlockSpec((None, 1, hw), lambda b: (b, 0, 0)),
                   pl.BlockSpec((None, B_HEADS, B_DK, B_DV), lambda b: (b, 0, 0, 0))],
        out_shape=[jax.ShapeDtypeStruct((db, 1, A_HEADS * A_DV), BF16),
                   jax.ShapeDtypeStruct((db, 1, hw), BF16),
                   jax.ShapeDtypeStruct((db, B_HEADS, B_DK, B_DV), F32)],
        compiler_params=_cparams(("parallel",)),
        name="even_tail",
    )(zf_s.reshape(db, 1, npad), conv_prev, s_prev, oa, conv_w, gp, norm_g.reshape(1, B_DV), lamv,
      subln.reshape(1, A_DV))


def _odd_tail_kernel(z_ref, oc_ref, y_ref):
    z = z_ref[...]
    oc = oc_ref[...]
    for hg in range(C_HEADS):
        gate = z[:, ODD_OFF[3] + hg * C_DH:ODD_OFF[3] + (hg + 1) * C_DH]
        y_ref[:, hg * C_DH:(hg + 1) * C_DH] = (oc[hg:hg + 1] * _silu(gate)).astype(BF16)


def _odd_tail(zf_s, oc):
    db, npad = zf_s.shape
    return pl.pallas_call(
        _odd_tail_kernel,
        grid=(db,),
        in_specs=[pl.BlockSpec((None, 1, npad), lambda b: (b, 0, 0)),
                  pl.BlockSpec((None, DEC_ROWS, LANES), lambda b: (b, 0, 0))],
        out_specs=pl.BlockSpec((None, 1, C_HEADS * C_DH), lambda b: (b, 0, 0)),
        out_shape=jax.ShapeDtypeStruct((db, 1, C_HEADS * C_DH), BF16),
        compiler_params=_cparams(("parallel",)),
        name="odd_tail",
    )(zf_s.reshape(db, 1, npad), oc)


def _bias_by_distance(table):
    n = jnp.arange(FAR_DIST + 1)
    exact = N_BUCKETS // 2
    nf = jnp.maximum(n, 1).astype(F32)
    large = exact + (jnp.log(nf / exact) / math.log(MAX_DIST / exact) * (N_BUCKETS - exact)).astype(I32)
    bucket = jnp.where(n < exact, n, jnp.minimum(large, N_BUCKETS - 1))
    return table[bucket].astype(F32)


def _prompt_near_tiles(bd, group):
    r = jnp.arange(BLK)[:, None]
    c = jnp.arange(BLK)[None, :]
    d0 = r - c
    rel = (bd - bd[FAR_DIST][None, :]) * LOG2E
    t_prev = rel[jnp.minimum(BLK + d0, FAR_DIST)]
    t_diag = jnp.where((d0 >= 0)[..., None], rel[jnp.clip(d0, 0, FAR_DIST)], NEG)
    tiles = jnp.stack([t_prev, t_diag])
    tiles = jnp.transpose(tiles, (3, 0, 1, 2))
    n_kv = bd.shape[1] // group
    tiles = tiles.reshape(n_kv, group, 2, BLK, BLK)
    return jnp.transpose(tiles, (0, 2, 1, 3, 4)).reshape(n_kv, 2, group * BLK, BLK)


def _decode_bias(bd, row_heads, past, page, g):
    heads = jnp.asarray(row_heads, I32)
    far = jnp.broadcast_to(bd[FAR_DIST][heads][:, None], (len(row_heads), g * page))
    pos = past - g * page + jnp.arange(g * page)
    dist = jnp.minimum(past - pos, FAR_DIST)
    last = bd[dist][:, heads].T
    new = jnp.broadcast_to(bd[0][heads][:, None], (len(row_heads), LANES))
    return far.astype(F32), last.astype(F32), new.astype(F32)


def _pad_cols(w, mult):
    n = w.shape[1]
    return jnp.pad(w, ((0, 0), (0, _round_up(n, mult) - n)))


def kernel(x_prompt, x_sample, cache_a_k, cache_a_v, state_b_s, state_b_conv, cache_c_k, cache_c_v, cache_c_idx,
           page_table, meta, bias_table, final_norm, norm_e, w_in_e, w_out_e, lam_q1, lam_k1, lam_q2, lam_k2,
           subln_a, conv_b, a_log_b, dt_bias_b, norm_b, norm_o, w_in_o, w_out_o):
    nb, seq, d = x_prompt.shape
    n_meta = meta.shape[0]
    l = seq + n_meta
    lp = _round_up(l, CHUNK)
    nq_real = pl.cdiv(l, BLK)
    db = x_sample.shape[0]
    n_pages = page_table.shape[1]
    page = cache_a_k.shape[2]
    past = n_pages * page
    n_pool = cache_a_k.shape[1]
    assert x_sample.shape[1] == 1 and norm_e.shape[0] == 1 and norm_o.shape[0] == 1
    lam_init = 0.8 - 0.6 * math.exp(-0.3 * 0)

    tn_e, tn_o = 8 * LANES, 5 * LANES
    w_e = _pad_cols(w_in_e[0], tn_e).astype(BF16)
    w_o = _pad_cols(w_in_o[0], tn_o).astype(BF16)
    w_out_a = w_out_e[0][:A_HEADS * A_DV].astype(BF16)
    w_out_b = w_out_e[0][A_HEADS * A_DV:].astype(BF16)
    w_out_c = w_out_o[0].astype(BF16)
    tm = CHUNK
    dbp = _round_up(db, SUBLANES)

    bd = _bias_by_distance(bias_table)
    tiles_a = _prompt_near_tiles(bd, A_GROUP)
    tiles_c = tiles_a if (A_KV, A_GROUP) == (C_KV, C_GROUP) else _prompt_near_tiles(bd, C_GROUP)
    lamv = jnp.stack([lam_q1[0], lam_k1[0], lam_q2[0], lam_k2[0]]).astype(F32)

    hp = jnp.concatenate([jnp.broadcast_to(meta.astype(F32)[None], (nb, n_meta, d)), x_prompt], axis=1)
    hp = jnp.pad(hp, ((0, 0), (0, lp - l), (0, 0))).reshape(nb * lp, d)
    zf, zb = _norm_proj(hp, norm_e[0], w_e, tm, tn_e)
    ya = _diff_attn_prompt(zf, zb, tiles_a, lamv, subln_a[0], nb, lp, nq_real, lam_init)
    yb, pb_s = _gdn_prompt(zf, conv_b[0], a_log_b[0], dt_bias_b[0], norm_b[0], nb, lp, l)
    h1 = _out_proj([ya, yb], [w_out_a, w_out_b], hp, tm)
    z3 = zf.reshape(nb, lp, -1)
    pa_k = z3[:, :l, EVEN_OFF[1]:EVEN_OFF[2]].reshape(1, nb, l, A_KV, 2, A_DH)
    pa_v = z3[:, :l, EVEN_OFF[2]:EVEN_OFF[3]].reshape(1, nb, l, A_KV, A_DV)
    pb_conv = z3[:, l - (CONV_W - 1):l, EVEN_OFF[4]:EVEN_OFF[5]][None]

    zf1, zb1 = _norm_proj(h1, norm_o[0], w_o, tm, tn_o)
    k_sel_p = min(TOPK_MAX, l // 4)
    yc = _sparse_attn_prompt(zf1, zb1, tiles_c, nb, lp, nq_real, k_sel_p)
    yp = _out_proj([yc], [w_out_c], h1, tm, final_gain=final_norm)
    y_prompt = yp.reshape(nb, lp, d)[:, n_meta:l]
    z13 = zf1.reshape(nb, lp, -1)
    pc_k = z13[:, :l, ODD_OFF[1]:ODD_OFF[2]].reshape(1, nb, l, C_KV, C_DH)
    pc_v = z13[:, :l, ODD_OFF[2]:ODD_OFF[3]].reshape(1, nb, l, C_KV, C_DH)
    pc_idx = z13[:, :l, ODD_OFF[5]:ODD_OFF[6]][None]

    hs = jnp.pad(x_sample.reshape(db, d), ((0, dbp - db), (0, 0)))
    zs, _ = _norm_proj(hs, norm_e[0], w_e, dbp, tn_e)
    zs = zs[:db]
    qa = zs[:, :EVEN_OFF[1]].reshape(db, A_KV, A_GROUP, 2, A_DH) * (A_DH ** -0.5)
    qprime = jnp.einsum('bhgcd,hi,cj->bhgcijd', qa, jnp.eye(A_KV, dtype=F32), jnp.eye(2, dtype=F32))
    qprime = qprime.reshape(db, DEC_ROWS, A_KV * 2 * A_DH).astype(BF16)
    k_new = zs[:, EVEN_OFF[1]:EVEN_OFF[2]].reshape(db, 1, -1)
    v_new = zs[:, EVEN_OFF[2]:EVEN_OFF[3]].reshape(db, 1, -1)
    g_dec = _pages_per_step(n_pages)
    rows_a = [r // 2 for r in range(DEC_ROWS)]
    bfar, blast, bnew = _decode_bias(bd, rows_a, past, page, g_dec)
    ak_t = jnp.transpose(cache_a_k[0], (0, 2, 3, 4, 1)).reshape(n_pool, A_KV * 2 * A_DH, page)
    av_r = cache_a_v[0].reshape(n_pool, page * A_KV, A_DV)
    oa = _paged_attention(qprime, k_new, v_new, bfar, blast, bnew, ak_t, av_r, page_table, page,
                          k_feature_major=True)
    ya_s, yb_s, sb_s = _even_tail(zs, state_b_conv[0], state_b_s[0], oa, conv_b[0], a_log_b[0], dt_bias_b[0],
                                  norm_b[0], lamv, subln_a[0], lam_init)
    hs_pad = lambda y: jnp.pad(y.reshape(db, -1), ((0, dbp - db), (0, 0)))
    hs1 = _out_proj([hs_pad(ya_s), hs_pad(yb_s)], [w_out_a, w_out_b], hs, dbp)
    sa_k = zs[:, EVEN_OFF[1]:EVEN_OFF[2]].reshape(1, db, 1, A_KV, 2, A_DH)
    sa_v = zs[:, EVEN_OFF[2]:EVEN_OFF[3]].reshape(1, db, 1, A_KV, A_DV)
    sb_conv = jnp.concatenate([state_b_conv[0][:, 1:], zs[:, None, EVEN_OFF[4]:EVEN_OFF[5]]], axis=1)[None]

    zs1, _ = _norm_proj(hs1, norm_o[0], w_o, dbp, tn_o)
    zs1 = zs1[:db]
    qidx = (zs1[:, ODD_OFF[4]:ODD_OFF[5]].reshape(db, IDX_HEADS, IDX_DIM) * (IDX_DIM ** -0.5)).astype(BF16)
    wcol = jnp.broadcast_to((zs1[:, ODD_OFF[6]:ODD_OFF[7]] * (IDX_HEADS ** -0.5))[:, :, None],
                            (db, IDX_HEADS, LANES))
    ki_new = zs1[:, ODD_OFF[5]:ODD_OFF[6]]
    ki_tile = jnp.zeros((db, IDX_DIM, page), F32).at[:, :, 0].set(ki_new)
    ci_t = jnp.transpose(cache_c_idx[0], (0, 2, 1))
    keys = _paged_index_scores(qidx, wcol, ki_tile, ci_t, page_table)
    k_sel_s = min(TOPK_MAX, (past + 1) // 4)
    tau, jmax = _select_rows(keys.reshape(db, -1), k_sel_s)
    tau, jmax = tau[:, 0], jmax[:, 0]
    key_new = keys[:, 0, past]
    selnew = ((key_new > tau) | ((key_new == tau) & (past <= jmax))).astype(I32)
    qc = zs1[:, :ODD_OFF[1]].reshape(db, C_KV, C_GROUP, C_DH) * (C_DH ** -0.5)
    qcp = jnp.einsum('bhgd,hi->bhgid', qc, jnp.eye(C_KV, dtype=F32)).reshape(db, C_HEADS, C_KV * C_DH)
    qcp = jnp.pad(qcp, ((0, 0), (0, DEC_ROWS - C_HEADS), (0, 0))).astype(BF16)
    kc_new = zs1[:, ODD_OFF[1]:ODD_OFF[2]].reshape(db, 1, -1)
    vc_new = zs1[:, ODD_OFF[2]:ODD_OFF[3]].reshape(db, 1, -1)
    rows_c = [r if r < C_HEADS else 0 for r in range(DEC_ROWS)]
    cfar, clast, cnew = _decode_bias(bd, rows_c, past, page, g_dec)
    ck_r = cache_c_k[0].reshape(n_pool, page * C_KV, C_DH)
    cv_r = cache_c_v[0].reshape(n_pool, page * C_KV, C_DH)
    oc = _paged_attention(qcp, kc_new, vc_new, cfar, clast, cnew, ck_r, cv_r, page_table, page,
                          k_feature_major=False, mask_args=(keys, tau, jmax, selnew))
    yc_s = _odd_tail(zs1, oc)
    ys = _out_proj([hs_pad(yc_s)], [w_out_c], hs1, dbp, final_gain=final_norm)
    y_sample = ys[:db].reshape(db, 1, d)
    sc_k = zs1[:, ODD_OFF[1]:ODD_OFF[2]].reshape(1, db, 1, C_KV, C_DH)
    sc_v = zs1[:, ODD_OFF[2]:ODD_OFF[3]].reshape(1, db, 1, C_KV, C_DH)
    sc_idx = zs1[:, None, ODD_OFF[5]:ODD_OFF[6]][None]

    return (y_prompt, y_sample, pa_k, pa_v, pb_s[None], pb_conv, pc_k, pc_v, pc_idx,
            sa_k, sa_v, sb_s[None], sb_conv, sc_k, sc_v, sc_idx)
```

```python
import functools
import math

import jax
import jax.numpy as jnp
import numpy as np
from jax import lax
from jax.experimental import pallas as pl
from jax.experimental.pallas import tpu as pltpu

F32 = jnp.float32
BF16 = jnp.bfloat16
I32 = jnp.int32

EPS = 1e-6
N_BUCKETS = 32
MAX_DIST = 128
FAR_DIST = MAX_DIST

A_HEADS, A_KV, A_GROUP, A_DH, A_DV = 8, 2, 4, 64, 128
B_HEADS, B_DK, B_DV, CONV_W = 4, 128, 128, 4
C_HEADS, C_KV, C_GROUP, C_DH = 8, 2, 4, 128
IDX_HEADS, IDX_DIM, TOPK_MAX = 8, 64, 256
B_QKV = 2 * B_HEADS * B_DK + B_HEADS * B_DV

EVEN_COLS = (A_HEADS * 2 * A_DH, A_KV * 2 * A_DH, A_KV * A_DV, A_HEADS * A_DV, B_QKV, B_HEADS * B_DV, B_HEADS, B_HEADS)
ODD_COLS = (C_HEADS * C_DH, C_KV * C_DH, C_KV * C_DH, C_HEADS * C_DH, IDX_HEADS * IDX_DIM, IDX_DIM, IDX_HEADS)
EVEN_OFF = tuple(int(v) for v in np.cumsum((0,) + EVEN_COLS))
ODD_OFF = tuple(int(v) for v in np.cumsum((0,) + ODD_COLS))

LANES = 128
SUBLANES = 8
BLK = 128
CHUNK = 512
SUBS = CHUNK // BLK
NEG = -1e30
INT_MIN = -2 ** 31
INT_MAX = 2 ** 31 - 1
LOG2E = math.log2(math.e)
VMEM_LIMIT = 56 * 1024 * 1024
HI = lax.Precision.HIGHEST

NT = (((1,), (1,)), ((), ()))


def _round_up(x, m):
    return (x + m - 1) // m * m


def _cparams(sem):
    return pltpu.CompilerParams(dimension_semantics=sem, vmem_limit_bytes=VMEM_LIMIT)


def _silu(x):
    return x * jax.nn.sigmoid(x)


def _sortable_key(score):
    score = jnp.where(score == 0.0, 0.0, score)
    bits = pltpu.bitcast(score, I32)
    return bits ^ ((bits >> 31) & jnp.int32(0x7FFFFFFF))


def _lane_tile(x, n):
    return jnp.concatenate([x] * n, axis=1)


def _proj_kernel(x_ref, g_ref, w_ref, of_ref, ob_ref, xn_ref):
    @pl.when(pl.program_id(1) == 0)
    def _():
        xf = x_ref[...]
        y = xf * lax.rsqrt(jnp.mean(xf * xf, axis=-1, keepdims=True) + EPS)
        xn_ref[...] = (y * g_ref[...]).astype(BF16)

    acc = jnp.dot(xn_ref[...], w_ref[...], preferred_element_type=F32)
    of_ref[...] = acc
    ob_ref[...] = acc.astype(BF16)


def _norm_proj(x, gain, w_bf16, tm, tn):
    m, d = x.shape
    n = w_bf16.shape[1]
    return pl.pallas_call(
        _proj_kernel,
        grid=(m // tm, n // tn),
        in_specs=[pl.BlockSpec((tm, d), lambda i, j: (i, 0)),
                  pl.BlockSpec((1, d), lambda i, j: (0, 0)),
                  pl.BlockSpec((d, tn), lambda i, j: (0, j))],
        out_specs=[pl.BlockSpec((tm, tn), lambda i, j: (i, j)),
                   pl.BlockSpec((tm, tn), lambda i, j: (i, j))],
        out_shape=[jax.ShapeDtypeStruct((m, n), F32), jax.ShapeDtypeStruct((m, n), BF16)],
        scratch_shapes=[pltpu.VMEM((tm, d), BF16)],
        compiler_params=_cparams(("parallel", "arbitrary")),
        name="norm_proj",
    )(x, gain.reshape(1, d), w_bf16)


def _outproj_kernel(*refs, n_lhs, final_norm):
    ys = refs[:n_lhs]
    ws = refs[n_lhs:2 * n_lhs]
    h_ref = refs[2 * n_lhs]
    pos = 2 * n_lhs + 1
    fn_ref = refs[pos] if final_norm else None
    out_ref = refs[-1]
    acc = h_ref[...]
    for y_ref, w_ref in zip(ys, ws):
        acc = acc + jnp.dot(y_ref[...], w_ref[...], preferred_element_type=F32)
    if final_norm:
        y = acc * lax.rsqrt(jnp.mean(acc * acc, axis=-1, keepdims=True) + EPS)
        acc = y * fn_ref[...]
    out_ref[...] = acc


def _out_proj(ys, ws, h, tm, final_gain=None):
    m, n = h.shape
    n_lhs = len(ys)
    in_specs = [pl.BlockSpec((tm, y.shape[1]), lambda i: (i, 0)) for y in ys]
    in_specs += [pl.BlockSpec(w.shape, lambda i: (0, 0)) for w in ws]
    in_specs += [pl.BlockSpec((tm, n), lambda i: (i, 0))]
    args = list(ys) + list(ws) + [h]
    if final_gain is not None:
        in_specs += [pl.BlockSpec((1, n), lambda i: (0, 0))]
        args += [final_gain.reshape(1, n)]
    return pl.pallas_call(
        functools.partial(_outproj_kernel, n_lhs=n_lhs, final_norm=final_gain is not None),
        grid=(m // tm,),
        in_specs=in_specs,
        out_specs=pl.BlockSpec((tm, n), lambda i: (i, 0)),
        out_shape=jax.ShapeDtypeStruct((m, n), F32),
        compiler_params=_cparams(("parallel",)),
        name="out_proj",
    )(*args)


def _flash_steps(ss, v1s, m_refs):
    each = range(len(ss))
    m_old = [m_refs[i][0][m_refs[i][2]] for i in each]
    m_new = [jnp.maximum(m_old[i], jnp.max(ss[i], axis=1, keepdims=True)) for i in each]
    alpha = [jnp.exp2(m_old[i] - m_new[i]) for i in each]
    p = [jnp.exp2(ss[i] - _lane_tile(m_new[i], ss[i].shape[1] // LANES)).astype(BF16) for i in each]
    pv = [jnp.dot(p[i], v1s[i], preferred_element_type=F32) for i in each]
    for i in each:
        m_ref, acc_ref, idx = m_refs[i]
        acc_ref[idx] = _lane_tile(alpha[i], 2) * acc_ref[idx] + pv[i]
        m_ref[idx] = m_new[i]


def _far_chunk_loop(n_far, run_chunks):
    def pair(i, carry):
        run_chunks([2 * i, 2 * i + 1])
        return carry

    lax.fori_loop(0, lax.shift_right_logical(n_far, 1), pair, 0)

    @pl.when(jnp.bitwise_and(n_far, 1) == 1)
    def _():
        run_chunks([n_far - 1])


def _build_near_bias(strip_ref, t_prev, t_diag, r, lead=()):
    for sb in range(2 * SUBS):
        rel = sb - SUBS - r
        tile = jnp.where(rel == 0, t_diag, jnp.where(rel == -1, t_prev, jnp.where(rel < -1, 0.0, NEG)))
        strip_ref[lead + (sb,)] = tile


def _diff_attn_kernel(q_ref, k_ref, v_ref, gate_ref, tiles_ref, lamv_ref, subln_ref, out_ref,
                      qm_ref, strip_ref, m_ref, acc_ref, *, lam_init, nq_real):
    qi = pl.program_id(2)

    @pl.when(qi >= nq_real)
    def _():
        out_ref[...] = jnp.zeros(out_ref.shape, out_ref.dtype)

    @pl.when(qi < nq_real)
    def _():
        lane = lax.broadcasted_iota(I32, (BLK, 2 * A_DH), 1)
        for g in range(A_GROUP):
            qg = q_ref[:, g * 2 * A_DH:(g + 1) * 2 * A_DH] * (A_DH ** -0.5 * LOG2E)
            for c in range(2):
                keep = (lane < A_DH) if c == 0 else (lane >= A_DH)
                qm_ref[c, g * BLK:(g + 1) * BLK, :] = jnp.where(keep, qg, 0.0).astype(BF16)
        m_ref[...] = jnp.full(m_ref.shape, NEG, F32)
        acc_ref[...] = jnp.zeros(acc_ref.shape, F32)
        r = jnp.bitwise_and(qi, SUBS - 1)
        jd = lax.shift_right_logical(qi, int(math.log2(SUBS)))
        _build_near_bias(strip_ref, tiles_ref[0], tiles_ref[1], r)
        ones = jnp.ones((CHUNK, A_DV), BF16)

        def run_chunks(js, first_sbs=None):
            starts = [pl.multiple_of(j * CHUNK, CHUNK) for j in js]
            k = jnp.concatenate([k_ref[pl.ds(st, CHUNK), :] for st in starts], axis=0)
            v1 = jnp.concatenate(
                [jnp.concatenate([v_ref[pl.ds(st, CHUNK), :], ones], axis=1) for st in starts], axis=0)
            ss = [lax.dot_general(qm_ref[c], k, NT, preferred_element_type=F32) for c in range(2)]
            if first_sbs is not None:
                bias = jnp.concatenate([strip_ref[sb + i] for sb in first_sbs for i in range(SUBS)], axis=1)
                ss = [s + bias for s in ss]
            _flash_steps(ss, [v1, v1], [(m_ref, acc_ref, c) for c in range(2)])

        _far_chunk_loop(jnp.maximum(jd - 1, 0), run_chunks)

        @pl.when(jd >= 1)
        def _():
            run_chunks([jd - 1, jd], [0, SUBS])

        @pl.when(jd == 0)
        def _():
            run_chunks([jd], [SUBS])

        lv = lamv_ref[...]
        lam = (jnp.exp(jnp.sum(lv[0:1] * lv[1:2], axis=1, keepdims=True))
               - jnp.exp(jnp.sum(lv[2:3] * lv[3:4], axis=1, keepdims=True)) + lam_init)
        a0 = acc_ref[0]
        a1 = acc_ref[1]
        o = a0[:, :A_DV] / a0[:, A_DV:] - lam * (a1[:, :A_DV] / a1[:, A_DV:])
        y = o * lax.rsqrt(jnp.mean(o * o, axis=-1, keepdims=True) + EPS)
        y = (y * subln_ref[...]) * (1.0 - lam_init)
        for g in range(A_GROUP):
            gate = gate_ref[:, g * A_DV:(g + 1) * A_DV]
            out_ref[:, g * A_DV:(g + 1) * A_DV] = (y[g * BLK:(g + 1) * BLK] * _silu(gate)).astype(BF16)


def _diff_attn_prompt(zf, zb, near_tiles, lamv, subln, nb, lp, nq_real, lam_init):
    nq = lp // BLK
    qw = A_GROUP * 2 * A_DH
    k_blk0 = EVEN_OFF[1] // (2 * A_DH)
    v_blk0 = EVEN_OFF[2] // A_DV
    g_blk0 = EVEN_OFF[3] // (A_GROUP * A_DV)
    rows = A_GROUP * BLK
    return pl.pallas_call(
        functools.partial(_diff_attn_kernel, lam_init=lam_init, nq_real=nq_real),
        grid=(nb, A_KV, nq),
        in_specs=[pl.BlockSpec((BLK, qw), lambda b, h, i: (b * nq + i, h)),
                  pl.BlockSpec((lp, 2 * A_DH), lambda b, h, i: (b, k_blk0 + h)),
                  pl.BlockSpec((lp, A_DV), lambda b, h, i: (b, v_blk0 + h)),
                  pl.BlockSpec((BLK, A_GROUP * A_DV), lambda b, h, i: (b * nq + i, g_blk0 + h)),
                  pl.BlockSpec((None, 2, rows, BLK), lambda b, h, i: (h, 0, 0, 0)),
                  pl.BlockSpec((4, A_DH), lambda b, h, i: (0, 0)),
                  pl.BlockSpec((1, A_DV), lambda b, h, i: (0, 0))],
        out_specs=pl.BlockSpec((BLK, A_GROUP * A_DV), lambda b, h, i: (b * nq + i, h)),
        out_shape=jax.ShapeDtypeStruct((nb * lp, A_HEADS * A_DV), BF16),
        scratch_shapes=[pltpu.VMEM((2, rows, 2 * A_DH), BF16),
                        pltpu.VMEM((2 * SUBS, rows, BLK), F32),
                        pltpu.VMEM((2, rows, LANES), F32),
                        pltpu.VMEM((2, rows, 2 * A_DV), F32)],
        compiler_params=_cparams(("parallel", "parallel", "arbitrary")),
        name="diff_attn_prompt",
    )(zf, zb, zb, zf, near_tiles, lamv, subln.reshape(1, A_DV))


def _softplus(x):
    return jnp.maximum(x, 0.0) + jnp.log1p(jnp.exp(-jnp.abs(x)))


def _split_bf16(a):
    hi = a.astype(BF16)
    return hi, (a - hi.astype(F32)).astype(BF16)


def _dot3(a, b):
    (ah, al), (bh, bl) = a, b
    return (jnp.dot(ah, bh, preferred_element_type=F32)
            + (jnp.dot(ah, bl, preferred_element_type=F32) + jnp.dot(al, bh, preferred_element_type=F32)))


def _gdn_chunk_math(s0, q, k, v, gb, betab, row_ge, row_gt):
    n = len(q)
    c = q[0].shape[0]
    each = range(n)
    dot = functools.partial(jnp.dot, preferred_element_type=F32)
    dot_nt = functools.partial(lax.dot_general, dimension_numbers=NT, preferred_element_type=F32)
    ltri = jnp.where(row_ge, 1.0, 0.0).astype(F32)
    gcum = [jnp.dot(ltri, gb[i], precision=HI, preferred_element_type=F32) for i in each]
    decay = [jnp.where(row_ge, jnp.exp(jnp.where(row_ge, gcum[i] - gcum[i].T, 0.0)), 0.0) for i in each]
    kb = [k[i].astype(BF16) for i in each]
    qb = [q[i].astype(BF16) for i in each]
    s0b = [s0[i].astype(BF16) for i in each]
    kk = [dot_nt(kb[i], kb[i]) for i in each]
    x = [-jnp.where(row_gt, betab[i] * decay[i] * kk[i], 0.0) for i in each]
    tm = list(x)
    p = list(x)
    for _ in range(int(math.log2(c)) - 1):
        ps = [_split_bf16(p[i]) for i in each]
        p = [_dot3(ps[i], ps[i]) for i in each]
        ps = [_split_bf16(p[i]) for i in each]
        tms = [_split_bf16(tm[i]) for i in each]
        tm = [tm[i] + p[i] + _dot3(tms[i], ps[i]) for i in each]
    eg = [jnp.exp(gcum[i]) for i in each]
    ks = [dot(kb[i], s0b[i]) for i in each]
    rhs = [betab[i] * (v[i] - eg[i] * ks[i]) for i in each]
    u = [rhs[i] + dot(tm[i].astype(BF16), rhs[i].astype(BF16)) for i in each]
    ub = [u[i].astype(BF16) for i in each]
    qk = [dot_nt(qb[i], kb[i]) * decay[i] for i in each]
    o = [eg[i] * dot(qb[i], s0b[i]) + dot(qk[i].astype(BF16), ub[i]) for i in each]
    glast = [gcum[i][c - 1:c, :] for i in each]
    kd = [k[i] * jnp.exp(glast[i] - gcum[i]) for i in each]
    s_new = [jnp.exp(glast[i]) * s0[i] + dot(kd[i].T.astype(BF16), ub[i]) for i in each]
    return s_new, o


def _gdn_prompt_kernel(xq_ref, xk_ref, xv_ref, gate_ref, ab_ref, convw_ref, gp_ref, normg_ref,
                       y_ref, sfin_ref, xbuf_ref, s_ref, *, seq_len):
    ci = pl.program_id(0)
    hw = B_HEADS * B_DK
    nb = xq_ref.shape[0]

    @pl.when(ci == 0)
    def _():
        xbuf_ref[:, 0:SUBLANES, :] = jnp.zeros((nb, SUBLANES, B_QKV), F32)
        s_ref[...] = jnp.zeros(s_ref.shape, F32)

    row = lax.broadcasted_iota(I32, (BLK, BLK), 0)
    col = lax.broadcasted_iota(I32, (BLK, BLK), 1)
    row_ge = row >= col
    row_gt = row > col
    valid = (ci * BLK + row) < seq_len
    gp = gp_ref[...]
    chains = [(b, h) for b in range(nb) for h in range(B_HEADS)]
    qs, ks, vs, gbs, betabs = [], [], [], [], []
    for b in range(nb):
        xbuf_ref[b, SUBLANES:SUBLANES + BLK, 0:hw] = xq_ref[b]
        xbuf_ref[b, SUBLANES:SUBLANES + BLK, hw:2 * hw] = xk_ref[b]
        xbuf_ref[b, SUBLANES:SUBLANES + BLK, 2 * hw:3 * hw] = xv_ref[b]
        conv = jnp.zeros((BLK, B_QKV), F32)
        for i in range(CONV_W):
            conv = conv + convw_ref[i:i + 1, :] * xbuf_ref[b, pl.ds(SUBLANES - (CONV_W - 1) + i, BLK), :]
        tail = xbuf_ref[b, BLK:BLK + SUBLANES, :]
        xbuf_ref[b, 0:SUBLANES, :] = tail
        act = _silu(conv)
        ab = ab_ref[b]
        for h in range(B_HEADS):
            q = act[:, h * B_DK:(h + 1) * B_DK]
            k = act[:, hw + h * B_DK:hw + (h + 1) * B_DK]
            v = act[:, 2 * hw + h * B_DV:2 * hw + (h + 1) * B_DV]
            q = q * lax.rsqrt(jnp.sum(q * q, axis=-1, keepdims=True) + EPS) * (B_DK ** -0.5)
            k = k * lax.rsqrt(jnp.sum(k * k, axis=-1, keepdims=True) + EPS)
            a_raw = jnp.broadcast_to(ab[:, h:h + 1], (BLK, BLK))
            b_raw = jnp.broadcast_to(ab[:, B_HEADS + h:B_HEADS + h + 1], (BLK, BLK))
            a_log = gp[0:1, h:h + 1]
            dt_b = gp[1:2, h:h + 1]
            qs.append(q)
            ks.append(k)
            vs.append(v)
            gbs.append(jnp.where(valid, -jnp.exp(a_log) * _softplus(a_raw + dt_b), 0.0))
            betabs.append(jnp.where(valid, jax.nn.sigmoid(b_raw), 0.0))
    s_new, outs = _gdn_chunk_math([s_ref[b, h] for b, h in chains], qs, ks, vs, gbs, betabs, row_ge, row_gt)
    for (b, h), s_bh, o in zip(chains, s_new, outs):
        s_ref[b, h] = s_bh
        y = o * lax.rsqrt(jnp.mean(o * o, axis=-1, keepdims=True) + EPS) * normg_ref[...]
        gate = gate_ref[b, :, h * B_DV:(h + 1) * B_DV]
        y_ref[b, :, h * B_DV:(h + 1) * B_DV] = (y * _silu(gate)).astype(BF16)

    @pl.when(ci == pl.num_programs(0) - 1)
    def _():
        sfin_ref[...] = s_ref[...]


def _gdn_prompt(zf, conv_w, a_log, dt_bias, norm_g, nb, lp, seq_len):
    nc = lp // BLK
    hw = B_HEADS * B_DK
    c0 = EVEN_OFF[4] // hw
    g0 = EVEN_OFF[5] // hw
    ab0 = EVEN_OFF[6] // LANES
    gp = jnp.zeros((SUBLANES, LANES), F32).at[0, :B_HEADS].set(a_log).at[1, :B_HEADS].set(dt_bias)
    z3 = zf.reshape(nb, lp, zf.shape[1])
    y, s_fin = pl.pallas_call(
        functools.partial(_gdn_prompt_kernel, seq_len=seq_len),
        grid=(nc,),
        in_specs=[pl.BlockSpec((nb, BLK, hw), lambda i: (0, i, c0)),
                  pl.BlockSpec((nb, BLK, hw), lambda i: (0, i, c0 + 1)),
                  pl.BlockSpec((nb, BLK, hw), lambda i: (0, i, c0 + 2)),
                  pl.BlockSpec((nb, BLK, hw), lambda i: (0, i, g0)),
                  pl.BlockSpec((nb, BLK, LANES), lambda i: (0, i, ab0)),
                  pl.BlockSpec((CONV_W, B_QKV), lambda i: (0, 0)),
                  pl.BlockSpec((SUBLANES, LANES), lambda i: (0, 0)),
                  pl.BlockSpec((1, B_DV), lambda i: (0, 0))],
        out_specs=[pl.BlockSpec((nb, BLK, hw), lambda i: (0, i, 0)),
                   pl.BlockSpec((nb, B_HEADS, B_DK, B_DV), lambda i: (0, 0, 0, 0))],
        out_shape=[jax.ShapeDtypeStruct((nb, lp, hw), BF16),
                   jax.ShapeDtypeStruct((nb, B_HEADS, B_DK, B_DV), F32)],
        scratch_shapes=[pltpu.VMEM((nb, BLK + SUBLANES, B_QKV), F32),
                        pltpu.VMEM((nb, B_HEADS, B_DK, B_DV), F32)],
        compiler_params=_cparams(("arbitrary",)),
        name="gdn_prompt",
    )(z3, z3, z3, z3, z3, conv_w, gp, norm_g.reshape(1, B_DV))
    return y.reshape(nb * lp, hw), s_fin


def _kth_largest(count_fn, k_sel, shape):
    def step(i, tau):
        cand = tau + lax.shift_left(jnp.int32(1), 31 - i)
        cnt = count_fn(lambda key, idx: key >= cand)
        return jnp.where(cnt >= k_sel, cand, tau)

    return lax.fori_loop(0, 32, step, jnp.full(shape, INT_MIN, I32))


def _tie_cutoff(count_fn, tau, need, n_index_bits, shape):
    def step(i, jm):
        cand = jm | lax.shift_left(jnp.int32(1), n_index_bits - 1 - i)
        cnt = count_fn(lambda key, idx: (key == tau) & (idx < cand))
        return jnp.where(cnt < need, cand, jm)

    return lax.fori_loop(0, n_index_bits, step, jnp.zeros(shape, I32))


def _sparse_attn_kernel(qi_ref, kw_ref, kidx_ref, qc_ref, kc_ref, vc_ref, g0_ref, g1_ref, tiles_ref, out_ref,
                        keys_ref, qm_ref, qh_ref, strip_ref, tau_ref, jmax_ref, m_ref, acc_ref,
                        *, k_sel, n_index_bits, nq_real):
    qb = pl.program_id(1)

    @pl.when(qb >= nq_real)
    def _():
        out_ref[...] = jnp.zeros(out_ref.shape, out_ref.dtype)

    @pl.when(qb < nq_real)
    def _():
        lane = lax.broadcasted_iota(I32, (BLK, LANES), 1)
        key_pos = lax.broadcasted_iota(I32, (CHUNK, BLK), 0)
        qry_pos = lax.broadcasted_iota(I32, (CHUNK, BLK), 1)
        r = jnp.bitwise_and(qb, SUBS - 1)
        jd = lax.shift_right_logical(qb, int(math.log2(SUBS)))

        for p in range(IDX_HEADS // 2):
            pair = qi_ref[:, p * LANES:(p + 1) * LANES] * (IDX_DIM ** -0.5)
            swapped = pltpu.roll(pair, IDX_DIM, 1)
            for e, src in enumerate((pair, swapped)):
                h = 2 * p + e
                qm_ref[h * BLK:(h + 1) * BLK, :] = jnp.where(lane < IDX_DIM, src, 0.0).astype(BF16)
        w_rows = kw_ref[...].T * (IDX_HEADS ** -0.5)

        def scores_t(j):
            start = pl.multiple_of(j * CHUNK, CHUNK)
            d = lax.dot_general(kidx_ref[pl.ds(start, CHUNK), :], qm_ref[...], NT, preferred_element_type=F32)
            sc = jnp.zeros((CHUNK, BLK), F32)
            for h in range(IDX_HEADS):
                sc = sc + w_rows[IDX_DIM + h:IDX_DIM + h + 1, :] * jnp.maximum(d[:, h * BLK:(h + 1) * BLK], 0.0)
            return sc

        def score_body(j, carry):
            keys_ref[j] = _sortable_key(scores_t(j))
            return carry

        lax.fori_loop(0, jd, score_body, 0)
        admissible = (jd * CHUNK + key_pos) <= (qb * BLK + qry_pos)
        keys_ref[jd] = _sortable_key(jnp.where(admissible, scores_t(jd), -jnp.inf))

        def count_fn(pred):
            part = SUBLANES * SUBLANES

            def body(j, acc):
                hit = jnp.where(pred(keys_ref[j], j * CHUNK + key_pos), 1.0, 0.0)
                return acc + jnp.sum(hit.reshape(CHUNK // part, part, BLK), axis=0)
            acc = lax.fori_loop(0, jd + 1, body, jnp.zeros((part, BLK), F32))
            return jnp.sum(acc, axis=0, keepdims=True).astype(I32)

        tau = _kth_largest(count_fn, k_sel, (1, BLK))
        tau_ref[...] = tau
        jmax_ref[...] = jnp.full((1, BLK), INT_MAX, I32)
        need = k_sel - count_fn(lambda key, idx: key > tau)
        n_eq = count_fn(lambda key, idx: key == tau)
        excess = jnp.max(jnp.where(n_eq > need, 1.0, 0.0)) > 0.5

        @pl.when(excess)
        def _():
            jmax_ref[...] = _tie_cutoff(count_fn, tau, need, n_index_bits, (1, BLK))

        jmax = jmax_ref[...]

        for h in range(C_KV):
            for g in range(C_GROUP):
                col = (h * C_GROUP + g) * C_DH
                qh_ref[h, g * BLK:(g + 1) * BLK, :] = (
                    qc_ref[:, col:col + C_DH] * (C_DH ** -0.5 * LOG2E)).astype(BF16)
            _build_near_bias(strip_ref, tiles_ref[h, 0], tiles_ref[h, 1], r, lead=(h,))
        m_ref[...] = jnp.full(m_ref.shape, NEG, F32)
        acc_ref[...] = jnp.zeros(acc_ref.shape, F32)
        ones = jnp.ones((CHUNK, C_DH), BF16)

        def run_chunks(js, first_sbs=None):
            starts = [pl.multiple_of(j * CHUNK, CHUNK) for j in js]
            masks = []
            for j in js:
                key = keys_ref[j]
                tie_ok = jnp.where((j * CHUNK + key_pos) <= jmax, 0.0, NEG)
                sel_t = jnp.where(key > tau, 0.0, jnp.where(key == tau, tie_ok, NEG))
                masks += [sel_t[i * BLK:(i + 1) * BLK, :].T for i in range(SUBS)]
            selneg = jnp.concatenate(masks, axis=1)
            selneg = jnp.concatenate([selneg] * C_GROUP, axis=0)
            ss, v1s = [], []
            for h in range(C_KV):
                cols = slice(h * C_DH, (h + 1) * C_DH)
                k = jnp.concatenate([kc_ref[pl.ds(st, CHUNK), cols] for st in starts], axis=0)
                v1s.append(jnp.concatenate(
                    [jnp.concatenate([vc_ref[pl.ds(st, CHUNK), cols], ones], axis=1) for st in starts], axis=0))
                s = lax.dot_general(qh_ref[h], k, NT, preferred_element_type=F32) + selneg
                if first_sbs is not None:
                    s = s + jnp.concatenate(
                        [strip_ref[h, sb + i] for sb in first_sbs for i in range(SUBS)], axis=1)
                ss.append(s)
            _flash_steps(ss, v1s, [(m_ref, acc_ref, h) for h in range(C_KV)])

        _far_chunk_loop(jnp.maximum(jd - 1, 0), run_chunks)

        @pl.when(jd >= 1)
        def _():
            run_chunks([jd - 1, jd], [0, SUBS])

        @pl.when(jd == 0)
        def _():
            run_chunks([jd], [SUBS])

        half = (C_HEADS // 2) * C_DH
        for h in range(C_KV):
            a = acc_ref[h]
            o = a[:, :C_DH] / a[:, C_DH:]
            gref = g0_ref if h == 0 else g1_ref
            for g in range(C_GROUP):
                gate = gref[:, g * C_DH:(g + 1) * C_DH]
                out_ref[:, h * half + g * C_DH:h * half + (g + 1) * C_DH] = (
                    o[g * BLK:(g + 1) * BLK] * _silu(gate)).astype(BF16)


def _sparse_attn_prompt(zf, zb, near_tiles, nb, lp, nq_real, k_sel):
    nq = lp // BLK
    half = (C_HEADS // 2) * C_DH
    qi0 = ODD_OFF[4] // (IDX_HEADS * IDX_DIM)
    ki0 = ODD_OFF[5] // LANES
    kc0 = ODD_OFF[1] // (C_KV * C_DH)
    vc0 = ODD_OFF[2] // (C_KV * C_DH)
    g0 = ODD_OFF[3] // half
    rows = C_GROUP * BLK
    n_index_bits = max(1, int(math.ceil(math.log2(lp))))
    return pl.pallas_call(
        functools.partial(_sparse_attn_kernel, k_sel=k_sel, n_index_bits=n_index_bits, nq_real=nq_real),
        grid=(nb, nq),
        in_specs=[pl.BlockSpec((BLK, IDX_HEADS * IDX_DIM), lambda b, i: (b * nq + i, qi0)),
                  pl.BlockSpec((BLK, LANES), lambda b, i: (b * nq + i, ki0)),
                  pl.BlockSpec((lp, LANES), lambda b, i: (b, ki0)),
                  pl.BlockSpec((BLK, C_HEADS * C_DH), lambda b, i: (b * nq + i, 0)),
                  pl.BlockSpec((lp, C_KV * C_DH), lambda b, i: (b, kc0)),
                  pl.BlockSpec((lp, C_KV * C_DH), lambda b, i: (b, vc0)),
                  pl.BlockSpec((BLK, half), lambda b, i: (b * nq + i, g0)),
                  pl.BlockSpec((BLK, half), lambda b, i: (b * nq + i, g0 + 1)),
                  pl.BlockSpec((C_KV, 2, rows, BLK), lambda b, i: (0, 0, 0, 0))],
        out_specs=pl.BlockSpec((BLK, C_HEADS * C_DH), lambda b, i: (b * nq + i, 0)),
        out_shape=jax.ShapeDtypeStruct((nb * lp, C_HEADS * C_DH), BF16),
        scratch_shapes=[pltpu.VMEM((lp // CHUNK, CHUNK, BLK), I32),
                        pltpu.VMEM((IDX_HEADS * BLK, LANES), BF16),
                        pltpu.VMEM((C_KV, rows, C_DH), BF16),
                        pltpu.VMEM((C_KV, 2 * SUBS, rows, BLK), F32),
                        pltpu.VMEM((1, BLK), I32),
                        pltpu.VMEM((1, BLK), I32),
                        pltpu.VMEM((C_KV, rows, LANES), F32),
                        pltpu.VMEM((C_KV, rows, 2 * C_DH), F32)],
        compiler_params=_cparams(("parallel", "arbitrary")),
        name="sparse_attn_prompt",
    )(zf, zf, zb, zf, zb, zb, zf, zf, near_tiles)


PAGES_PER_STEP = 32
INDEX_PAGES_PER_STEP = 32
DEC_ROWS = 16


def _interleaved_pages(refs, page):
    halves = [jnp.concatenate([r[pl.ds(h, page, stride=2), :] for r in refs], axis=0) for h in range(2)]
    return jnp.concatenate(halves, axis=1).astype(BF16)


def _paged_attn_kernel(*refs, n_pages_step, masked, page, k_feature_major):
    if masked:
        pt_ref, tau_ref, jmax_ref, selnew_ref = refs[:4]
        refs = refs[4:]
    else:
        pt_ref = refs[0]
        refs = refs[1:]
    q_ref, knew_ref, vnew_ref, bfar_ref, blast_ref, b0_ref = refs[:6]
    refs = refs[6:]
    if masked:
        keys_ref = refs[0]
        refs = refs[1:]
    k_refs = refs[:n_pages_step]
    v_refs = refs[n_pages_step:2 * n_pages_step]
    out_ref, m_ref, l_ref, acc_ref = refs[2 * n_pages_step:]
    del pt_ref
    b = pl.program_id(0)
    j = pl.program_id(1)
    width = n_pages_step * page
    q = q_ref[...]

    @pl.when(j == 0)
    def _():
        s_new = jnp.sum(q.astype(F32) * knew_ref[...].astype(BF16).astype(F32), axis=1, keepdims=True) + b0_ref[...]
        v_new = jnp.broadcast_to(vnew_ref[...].astype(BF16).astype(F32), acc_ref.shape)
        if masked:
            take = selnew_ref[b] > 0
            m_ref[...] = jnp.where(take, s_new, NEG)
            l_ref[...] = jnp.where(take, 1.0, 0.0) * jnp.ones(l_ref.shape, F32)
            acc_ref[...] = jnp.where(take, v_new, 0.0)
        else:
            m_ref[...] = s_new
            l_ref[...] = jnp.ones(l_ref.shape, F32)
            acc_ref[...] = v_new

    if k_feature_major:
        kcat = jnp.concatenate([r[...] for r in k_refs], axis=1).astype(BF16)
        s = jnp.dot(q, kcat, preferred_element_type=F32)
    else:
        s = lax.dot_general(q, _interleaved_pages(k_refs, page), NT, preferred_element_type=F32)
    vcat = _interleaved_pages(v_refs, page)
    s = s + jnp.where(j == pl.num_programs(1) - 1, blast_ref[...], bfar_ref[...])
    if masked:
        key = keys_ref[...]
        idx = j * width + lax.broadcasted_iota(I32, (1, width), 1)
        tau = tau_ref[b]
        sel = (key > tau) | ((key == tau) & (idx <= jmax_ref[b]))
        s = jnp.where(sel, s, NEG)
    m_old = m_ref[...]
    m_new = jnp.maximum(m_old, jnp.max(s, axis=1, keepdims=True))
    alpha = jnp.exp(m_old - m_new)
    p = jnp.exp(s - m_new[:, 0:1])
    l_ref[...] = alpha * l_ref[...] + jnp.sum(p, axis=1, keepdims=True)
    acc_ref[...] = _lane_tile(alpha, 2) * acc_ref[...] + jnp.dot(p.astype(BF16), vcat, preferred_element_type=F32)
    m_ref[...] = m_new

    @pl.when(j == pl.num_programs(1) - 1)
    def _():
        o = acc_ref[...]
        rowi = lax.broadcasted_iota(I32, (DEC_ROWS, LANES), 0)
        upper = (rowi >= DEC_ROWS // 2) if not masked else ((rowi >= DEC_ROWS // 4) & (rowi < DEC_ROWS // 2))
        out_ref[...] = jnp.where(upper, o[:, LANES:], o[:, :LANES]) / l_ref[...]


def _pages_per_step(n_pages, g=PAGES_PER_STEP):
    while n_pages % g:
        g //= 2
    return g


def _paged_attention(qprime, k_new, v_new, bias_far, bias_last, bias0, k_cache, v_cache, page_table, page,
                     k_feature_major, mask_args=None):
    db = qprime.shape[0]
    n_pages = page_table.shape[1]
    width = qprime.shape[2]
    g = _pages_per_step(n_pages)
    n_steps = n_pages // g
    masked = mask_args is not None
    n_pref = 4 if masked else 1

    def page_map(gi):
        return lambda b, j, pt, *_: (pt[b * n_pages + j * g + gi], 0, 0)

    in_specs = [pl.BlockSpec((None, DEC_ROWS, width), lambda b, j, *_: (b, 0, 0)),
                pl.BlockSpec((None, 1, width), lambda b, j, *_: (b, 0, 0)),
                pl.BlockSpec((None, 1, width), lambda b, j, *_: (b, 0, 0)),
                pl.BlockSpec((DEC_ROWS, g * page), lambda b, j, *_: (0, 0)),
                pl.BlockSpec((DEC_ROWS, g * page), lambda b, j, *_: (0, 0)),
                pl.BlockSpec((DEC_ROWS, LANES), lambda b, j, *_: (0, 0))]
    args = [qprime, k_new, v_new, bias_far, bias_last, bias0]
    prefetch = [page_table.reshape(-1)]
    if masked:
        keys, tau, jmax, selnew = mask_args
        prefetch += [tau, jmax, selnew]
        in_specs += [pl.BlockSpec((None, 1, g * page), lambda b, j, *_: (b, 0, j))]
        args += [keys]
    in_specs += [pl.BlockSpec((None,) + k_cache.shape[1:], page_map(gi)) for gi in range(g)]
    in_specs += [pl.BlockSpec((None,) + v_cache.shape[1:], page_map(gi)) for gi in range(g)]
    args += [k_cache] * g + [v_cache] * g
    grid_spec = pltpu.PrefetchScalarGridSpec(
        num_scalar_prefetch=n_pref,
        grid=(db, n_steps),
        in_specs=in_specs,
        out_specs=pl.BlockSpec((None, DEC_ROWS, LANES), lambda b, j, *_: (b, 0, 0)),
        scratch_shapes=[pltpu.VMEM((DEC_ROWS, LANES), F32),
                        pltpu.VMEM((DEC_ROWS, LANES), F32),
                        pltpu.VMEM((DEC_ROWS, width), F32)])
    return pl.pallas_call(
        functools.partial(_paged_attn_kernel, n_pages_step=g, masked=masked, page=page,
                          k_feature_major=k_feature_major),
        grid_spec=grid_spec,
        out_shape=jax.ShapeDtypeStruct((db, DEC_ROWS, LANES), F32),
        compiler_params=_cparams(("parallel", "arbitrary")),
        name="paged_attn_masked" if masked else "paged_attn",
    )(*prefetch, *args)


def _paged_index_kernel(pt_ref, q_ref, w_ref, knew_ref, *refs, n_pages_step, page):
    k_refs = refs[:n_pages_step]
    out_ref = refs[n_pages_step]
    del pt_ref
    j = pl.program_id(1)
    last = pl.num_programs(1) - 1
    q = q_ref[...]
    w = w_ref[...]

    def score(kmat_t):
        d = jnp.dot(q, kmat_t.astype(BF16), preferred_element_type=F32)
        return jnp.sum(jnp.maximum(d, 0.0) * w[:, 0:1], axis=0, keepdims=True)

    @pl.when(j < last)
    def _():
        kcat = jnp.concatenate([r[...] for r in k_refs], axis=1)
        out_ref[...] = _sortable_key(score(kcat))

    @pl.when(j == last)
    def _():
        sc = score(knew_ref[...])
        lane = lax.broadcasted_iota(I32, (1, page), 1)
        sc = jnp.where(lane == 0, sc, -jnp.inf)
        pad = jnp.full((1, (n_pages_step - 1) * page), -jnp.inf, F32)
        full = jnp.concatenate([sc, pad], axis=1) if n_pages_step > 1 else sc
        out_ref[...] = _sortable_key(full)


def _paged_index_scores(qidx, wcol, k_new_tile, idx_cache_t, page_table):
    db = qidx.shape[0]
    n_pages = page_table.shape[1]
    page = idx_cache_t.shape[2]
    g = _pages_per_step(n_pages, INDEX_PAGES_PER_STEP)
    n_steps = n_pages // g

    def page_map(gi):
        return lambda b, j, pt: (pt[b * n_pages + jnp.minimum(j, n_steps - 1) * g + gi], 0, 0)

    grid_spec = pltpu.PrefetchScalarGridSpec(
        num_scalar_prefetch=1,
        grid=(db, n_steps + 1),
        in_specs=[pl.BlockSpec((None, IDX_HEADS, IDX_DIM), lambda b, j, pt: (b, 0, 0)),
                  pl.BlockSpec((None, IDX_HEADS, LANES), lambda b, j, pt: (b, 0, 0)),
                  pl.BlockSpec((None, IDX_DIM, page), lambda b, j, pt: (b, 0, 0))]
        + [pl.BlockSpec((None, IDX_DIM, page), page_map(gi)) for gi in range(g)],
        out_specs=pl.BlockSpec((None, 1, g * page), lambda b, j, pt: (b, 0, j)))
    return pl.pallas_call(
        functools.partial(_paged_index_kernel, n_pages_step=g, page=page),
        grid_spec=grid_spec,
        out_shape=jax.ShapeDtypeStruct((db, 1, (n_steps + 1) * g * page), I32),
        compiler_params=_cparams(("parallel", "arbitrary")),
        name="paged_index_scores",
    )(page_table.reshape(-1), qidx, wcol, k_new_tile, *([idx_cache_t] * g))


def _select_kernel(keys_ref, tau_ref, jmax_ref, *, k_sel, n_index_bits):
    keys = keys_ref[...]
    idx = lax.broadcasted_iota(I32, keys.shape, 1)
    shape = (keys.shape[0], 1)

    def count_fn(pred):
        return jnp.sum(jnp.where(pred(keys, idx), 1, 0), axis=1, keepdims=True)

    tau = _kth_largest(count_fn, k_sel, shape)
    need = k_sel - count_fn(lambda key, i: key > tau)
    jmax = _tie_cutoff(count_fn, tau, need, n_index_bits, shape)
    tau_ref[...] = jnp.broadcast_to(tau, tau_ref.shape)
    jmax_ref[...] = jnp.broadcast_to(jmax, jmax_ref.shape)


def _select_rows(keys2d, k_sel):
    rows, width = keys2d.shape
    n_index_bits = max(1, int(math.ceil(math.log2(width))))
    return pl.pallas_call(
        functools.partial(_select_kernel, k_sel=k_sel, n_index_bits=n_index_bits),
        out_shape=[jax.ShapeDtypeStruct((rows, LANES), I32), jax.ShapeDtypeStruct((rows, LANES), I32)],
        compiler_params=pltpu.CompilerParams(vmem_limit_bytes=VMEM_LIMIT),
        name="select_rows",
    )(keys2d)


def _even_tail_kernel(z_ref, conv_ref, s_ref, oa_ref, convw_ref, gp_ref, normg_ref, lamv_ref, subln_ref,
                      ya_ref, yb_ref, snew_ref, *, lam_init):
    z = z_ref[...]
    hw = B_HEADS * B_DK
    lv = lamv_ref[...]
    lam = (jnp.exp(jnp.sum(lv[0:1] * lv[1:2], axis=1, keepdims=True))
           - jnp.exp(jnp.sum(lv[2:3] * lv[3:4], axis=1, keepdims=True)) + lam_init)
    oa = oa_ref[...]
    for hg in range(A_HEADS):
        o = oa[2 * hg:2 * hg + 1] - lam * oa[2 * hg + 1:2 * hg + 2]
        y = o * lax.rsqrt(jnp.mean(o * o, axis=-1, keepdims=True) + EPS)
        y = (y * subln_ref[...]) * (1.0 - lam_init)
        gate = z[:, EVEN_OFF[3] + hg * A_DV:EVEN_OFF[3] + (hg + 1) * A_DV]
        ya_ref[:, hg * A_DV:(hg + 1) * A_DV] = (y * _silu(gate)).astype(BF16)
    x_new = z[:, EVEN_OFF[4]:EVEN_OFF[4] + B_QKV]
    conv = convw_ref[CONV_W - 1:CONV_W, :] * x_new
    cp = conv_ref[...]
    for i in range(CONV_W - 1):
        conv = conv + convw_ref[i:i + 1, :] * cp[i:i + 1, :]
    act = _silu(conv)
    ab = z[:, EVEN_OFF[6]:EVEN_OFF[6] + LANES]
    gp = gp_ref[...]
    row = lax.broadcasted_iota(I32, (B_DK, B_DV), 0)
    col = lax.broadcasted_iota(I32, (B_DK, B_DV), 1)
    eye = row == col
    for h in range(B_HEADS):
        q = act[:, h * B_DK:(h + 1) * B_DK]
        k = act[:, hw + h * B_DK:hw + (h + 1) * B_DK]
        v = act[:, 2 * hw + h * B_DV:2 * hw + (h + 1) * B_DV]
        q = q * lax.rsqrt(jnp.sum(q * q, axis=-1, keepdims=True) + EPS) * (B_DK ** -0.5)
        k = k * lax.rsqrt(jnp.sum(k * k, axis=-1, keepdims=True) + EPS)
        g = -jnp.exp(gp[0:1, h:h + 1]) * _softplus(ab[:, h:h + 1] + gp[1:2, h:h + 1])
        beta = jax.nn.sigmoid(ab[:, B_HEADS + h:B_HEADS + h + 1])
        eg = jnp.exp(g)
        s0 = s_ref[h]
        kcol = jnp.sum(jnp.where(eye, jnp.broadcast_to(k, (B_DK, B_DK)), 0.0), axis=1, keepdims=True)
        qcol = jnp.sum(jnp.where(eye, jnp.broadcast_to(q, (B_DK, B_DK)), 0.0), axis=1, keepdims=True)
        ks = jnp.sum(kcol * s0, axis=0, keepdims=True)
        qs = jnp.sum(qcol * s0, axis=0, keepdims=True)
        u = beta * (v - eg * ks)
        qk = jnp.sum(q * k, axis=1, keepdims=True)
        o = eg * qs + qk * u
        snew_ref[h] = eg * s0 + kcol * u
        y = o * lax.rsqrt(jnp.mean(o * o, axis=-1, keepdims=True) + EPS) * normg_ref[...]
        gate = z[:, EVEN_OFF[5] + h * B_DV:EVEN_OFF[5] + (h + 1) * B_DV]
        yb_ref[:, h * B_DV:(h + 1) * B_DV] = (y * _silu(gate)).astype(BF16)


def _even_tail(zf_s, conv_prev, s_prev, oa, conv_w, a_log, dt_bias, norm_g, lamv, subln, lam_init):
    db, npad = zf_s.shape
    gp = jnp.zeros((SUBLANES, LANES), F32).at[0, :B_HEADS].set(a_log).at[1, :B_HEADS].set(dt_bias)
    hw = B_HEADS * B_DV
    return pl.pallas_call(
        functools.partial(_even_tail_kernel, lam_init=lam_init),
        grid=(db,),
        in_specs=[pl.BlockSpec((None, 1, npad), lambda b: (b, 0, 0)),
                  pl.BlockSpec((None, CONV_W - 1, B_QKV), lambda b: (b, 0, 0)),
                  pl.BlockSpec((None, B_HEADS, B_DK, B_DV), lambda b: (b, 0, 0, 0)),
                  pl.BlockSpec((None, DEC_ROWS, LANES), lambda b: (b, 0, 0)),
                  pl.BlockSpec((CONV_W, B_QKV), lambda b: (0, 0)),
                  pl.BlockSpec((SUBLANES, LANES), lambda b: (0, 0)),
                  pl.BlockSpec((1, B_DV), lambda b: (0, 0)),
                  pl.BlockSpec((4, A_DH), lambda b: (0, 0)),
                  pl.BlockSpec((1, A_DV), lambda b: (0, 0))],
        out_specs=[pl.BlockSpec((None, 1, A_HEADS * A_DV), lambda b: (b, 0, 0)),
                   pl.BlockSpec((None, 1, hw), lambda b: (b, 0, 0)),
                   pl.BlockSpec((None, B_HEADS, B_DK, B_DV), lambda b: (b, 0, 0, 0))],
        out_shape=[jax.ShapeDtypeStruct((db, 1, A_HEADS * A_DV), BF16),
                   jax.ShapeDtypeStruct((db, 1, hw), BF16),
                   jax.ShapeDtypeStruct((db, B_HEADS, B_DK, B_DV), F32)],
        compiler_params=_cparams(("parallel",)),
        name="even_tail",
    )(zf_s.reshape(db, 1, npad), conv_prev, s_prev, oa, conv_w, gp, norm_g.reshape(1, B_DV), lamv,
      subln.reshape(1, A_DV))


def _odd_tail_kernel(z_ref, oc_ref, y_ref):
    z = z_ref[...]
    oc = oc_ref[...]
    for hg in range(C_HEADS):
        gate = z[:, ODD_OFF[3] + hg * C_DH:ODD_OFF[3] + (hg + 1) * C_DH]
        y_ref[:, hg * C_DH:(hg + 1) * C_DH] = (oc[hg:hg + 1] * _silu(gate)).astype(BF16)


def _odd_tail(zf_s, oc):
    db, npad = zf_s.shape
    return pl.pallas_call(
        _odd_tail_kernel,
        grid=(db,),
        in_specs=[pl.BlockSpec((None, 1, npad), lambda b: (b, 0, 0)),
                  pl.BlockSpec((None, DEC_ROWS, LANES), lambda b: (b, 0, 0))],
        out_specs=pl.BlockSpec((None, 1, C_HEADS * C_DH), lambda b: (b, 0, 0)),
        out_shape=jax.ShapeDtypeStruct((db, 1, C_HEADS * C_DH), BF16),
        compiler_params=_cparams(("parallel",)),
        name="odd_tail",
    )(zf_s.reshape(db, 1, npad), oc)


def _bias_by_distance(table):
    n = jnp.arange(FAR_DIST + 1)
    exact = N_BUCKETS // 2
    nf = jnp.maximum(n, 1).astype(F32)
    large = exact + (jnp.log(nf / exact) / math.log(MAX_DIST / exact) * (N_BUCKETS - exact)).astype(I32)
    bucket = jnp.where(n < exact, n, jnp.minimum(large, N_BUCKETS - 1))
    return table[bucket].astype(F32)


def _prompt_near_tiles(bd, group):
    r = jnp.arange(BLK)[:, None]
    c = jnp.arange(BLK)[None, :]
    d0 = r - c
    rel = (bd - bd[FAR_DIST][None, :]) * LOG2E
    t_prev = rel[jnp.minimum(BLK + d0, FAR_DIST)]
    t_diag = jnp.where((d0 >= 0)[..., None], rel[jnp.clip(d0, 0, FAR_DIST)], NEG)
    tiles = jnp.stack([t_prev, t_diag])
    tiles = jnp.transpose(tiles, (3, 0, 1, 2))
    n_kv = bd.shape[1] // group
    tiles = tiles.reshape(n_kv, group, 2, BLK, BLK)
    return jnp.transpose(tiles, (0, 2, 1, 3, 4)).reshape(n_kv, 2, group * BLK, BLK)


def _decode_bias(bd, row_heads, past, page, g):
    heads = jnp.asarray(row_heads, I32)
    far = jnp.broadcast_to(bd[FAR_DIST][heads][:, None], (len(row_heads), g * page))
    pos = past - g * page + jnp.arange(g * page)
    dist = jnp.minimum(past - pos, FAR_DIST)
    last = bd[dist][:, heads].T
    new = jnp.broadcast_to(bd[0][heads][:, None], (len(row_heads), LANES))
    return far.astype(F32), last.astype(F32), new.astype(F32)


def _pad_cols(w, mult):
    n = w.shape[1]
    return jnp.pad(w, ((0, 0), (0, _round_up(n, mult) - n)))


def kernel(x_prompt, x_sample, cache_a_k, cache_a_v, state_b_s, state_b_conv, cache_c_k, cache_c_v, cache_c_idx,
           page_table, meta, bias_table, final_norm, norm_e, w_in_e, w_out_e, lam_q1, lam_k1, lam_q2, lam_k2,
           subln_a, conv_b, a_log_b, dt_bias_b, norm_b, norm_o, w_in_o, w_out_o):
    nb, seq, d = x_prompt.shape
    n_meta = meta.shape[0]
    l = seq + n_meta
    lp = _round_up(l, CHUNK)
    nq_real = pl.cdiv(l, BLK)
    db = x_sample.shape[0]
    n_pages = page_table.shape[1]
    page = cache_a_k.shape[2]
    past = n_pages * page
    n_pool = cache_a_k.shape[1]
    assert x_sample.shape[1] == 1 and norm_e.shape[0] == 1 and norm_o.shape[0] == 1
    lam_init = 0.8 - 0.6 * math.exp(-0.3 * 0)

    tn_e, tn_o = 8 * LANES, 5 * LANES
    w_e = _pad_cols(w_in_e[0], tn_e).astype(BF16)
    w_o = _pad_cols(w_in_o[0], tn_o).astype(BF16)
    w_out_a = w_out_e[0][:A_HEADS * A_DV].astype(BF16)
    w_out_b = w_out_e[0][A_HEADS * A_DV:].astype(BF16)
    w_out_c = w_out_o[0].astype(BF16)
    tm = CHUNK
    dbp = _round_up(db, SUBLANES)

    bd = _bias_by_distance(bias_table)
    tiles_a = _prompt_near_tiles(bd, A_GROUP)
    tiles_c = tiles_a if (A_KV, A_GROUP) == (C_KV, C_GROUP) else _prompt_near_tiles(bd, C_GROUP)
    lamv = jnp.stack([lam_q1[0], lam_k1[0], lam_q2[0], lam_k2[0]]).astype(F32)

    hp = jnp.concatenate([jnp.broadcast_to(meta.astype(F32)[None], (nb, n_meta, d)), x_prompt], axis=1)
    hp = jnp.pad(hp, ((0, 0), (0, lp - l), (0, 0))).reshape(nb * lp, d)
    zf, zb = _norm_proj(hp, norm_e[0], w_e, tm, tn_e)
    ya = _diff_attn_prompt(zf, zb, tiles_a, lamv, subln_a[0], nb, lp, nq_real, lam_init)
    yb, pb_s = _gdn_prompt(zf, conv_b[0], a_log_b[0], dt_bias_b[0], norm_b[0], nb, lp, l)
    h1 = _out_proj([ya, yb], [w_out_a, w_out_b], hp, tm)
    z3 = zf.reshape(nb, lp, -1)
    pa_k = z3[:, :l, EVEN_OFF[1]:EVEN_OFF[2]].reshape(1, nb, l, A_KV, 2, A_DH)
    pa_v = z3[:, :l, EVEN_OFF[2]:EVEN_OFF[3]].reshape(1, nb, l, A_KV, A_DV)
    pb_conv = z3[:, l - (CONV_W - 1):l, EVEN_OFF[4]:EVEN_OFF[5]][None]

    zf1, zb1 = _norm_proj(h1, norm_o[0], w_o, tm, tn_o)
    k_sel_p = min(TOPK_MAX, l // 4)
    yc = _sparse_attn_prompt(zf1, zb1, tiles_c, nb, lp, nq_real, k_sel_p)
    yp = _out_proj([yc], [w_out_c], h1, tm, final_gain=final_norm)
    y_prompt = yp.reshape(nb, lp, d)[:, n_meta:l]
    z13 = zf1.reshape(nb, lp, -1)
    pc_k = z13[:, :l, ODD_OFF[1]:ODD_OFF[2]].reshape(1, nb, l, C_KV, C_DH)
    pc_v = z13[:, :l, ODD_OFF[2]:ODD_OFF[3]].reshape(1, nb, l, C_KV, C_DH)
    pc_idx = z13[:, :l, ODD_OFF[5]:ODD_OFF[6]][None]

    hs = jnp.pad(x_sample.reshape(db, d), ((0, dbp - db), (0, 0)))
    zs, _ = _norm_proj(hs, norm_e[0], w_e, dbp, tn_e)
    zs = zs[:db]
    qa = zs[:, :EVEN_OFF[1]].reshape(db, A_KV, A_GROUP, 2, A_DH) * (A_DH ** -0.5)
    qprime = jnp.einsum('bhgcd,hi,cj->bhgcijd', qa, jnp.eye(A_KV, dtype=F32), jnp.eye(2, dtype=F32))
    qprime = qprime.reshape(db, DEC_ROWS, A_KV * 2 * A_DH).astype(BF16)
    k_new = zs[:, EVEN_OFF[1]:EVEN_OFF[2]].reshape(db, 1, -1)
    v_new = zs[:, EVEN_OFF[2]:EVEN_OFF[3]].reshape(db, 1, -1)
    g_dec = _pages_per_step(n_pages)
    rows_a = [r // 2 for r in range(DEC_ROWS)]
    bfar, blast, bnew = _decode_bias(bd, rows_a, past, page, g_dec)
    ak_t = jnp.transpose(cache_a_k[0], (0, 2, 3, 4, 1)).reshape(n_pool, A_KV * 2 * A_DH, page)
    av_r = cache_a_v[0].reshape(n_pool, page * A_KV, A_DV)
    oa = _paged_attention(qprime, k_new, v_new, bfar, blast, bnew, ak_t, av_r, page_table, page,
                          k_feature_major=True)
    ya_s, yb_s, sb_s = _even_tail(zs, state_b_conv[0], state_b_s[0], oa, conv_b[0], a_log_b[0], dt_bias_b[0],
                                  norm_b[0], lamv, subln_a[0], lam_init)
    hs_pad = lambda y: jnp.pad(y.reshape(db, -1), ((0, dbp - db), (0, 0)))
    hs1 = _out_proj([hs_pad(ya_s), hs_pad(yb_s)], [w_out_a, w_out_b], hs, dbp)
    sa_k = zs[:, EVEN_OFF[1]:EVEN_OFF[2]].reshape(1, db, 1, A_KV, 2, A_DH)
    sa_v = zs[:, EVEN_OFF[2]:EVEN_OFF[3]].reshape(1, db, 1, A_KV, A_DV)
    sb_conv = jnp.concatenate([state_b_conv[0][:, 1:], zs[:, None, EVEN_OFF[4]:EVEN_OFF[5]]], axis=1)[None]

    zs1, _ = _norm_proj(hs1, norm_o[0], w_o, dbp, tn_o)
    zs1 = zs1[:db]
    qidx = (zs1[:, ODD_OFF[4]:ODD_OFF[5]].reshape(db, IDX_HEADS, IDX_DIM) * (IDX_DIM ** -0.5)).astype(BF16)
    wcol = jnp.broadcast_to((zs1[:, ODD_OFF[6]:ODD_OFF[7]] * (IDX_HEADS ** -0.5))[:, :, None],
                            (db, IDX_HEADS, LANES))
    ki_new = zs1[:, ODD_OFF[5]:ODD_OFF[6]]
    ki_tile = jnp.zeros((db, IDX_DIM, page), F32).at[:, :, 0].set(ki_new)
    ci_t = jnp.transpose(cache_c_idx[0], (0, 2, 1))
    keys = _paged_index_scores(qidx, wcol, ki_tile, ci_t, page_table)
    k_sel_s = min(TOPK_MAX, (past + 1) // 4)
    tau, jmax = _select_rows(keys.reshape(db, -1), k_sel_s)
    tau, jmax = tau[:, 0], jmax[:, 0]
    key_new = keys[:, 0, past]
    selnew = ((key_new > tau) | ((key_new == tau) & (past <= jmax))).astype(I32)
    qc = zs1[:, :ODD_OFF[1]].reshape(db, C_KV, C_GROUP, C_DH) * (C_DH ** -0.5)
    qcp = jnp.einsum('bhgd,hi->bhgid', qc, jnp.eye(C_KV, dtype=F32)).reshape(db, C_HEADS, C_KV * C_DH)
    qcp = jnp.pad(qcp, ((0, 0), (0, DEC_ROWS - C_HEADS), (0, 0))).astype(BF16)
    kc_new = zs1[:, ODD_OFF[1]:ODD_OFF[2]].reshape(db, 1, -1)
    vc_new = zs1[:, ODD_OFF[2]:ODD_OFF[3]].reshape(db, 1, -1)
    rows_c = [r if r < C_HEADS else 0 for r in range(DEC_ROWS)]
    cfar, clast, cnew = _decode_bias(bd, rows_c, past, page, g_dec)
    ck_r = cache_c_k[0].reshape(n_pool, page * C_KV, C_DH)
    cv_r = cache_c_v[0].reshape(n_pool, page * C_KV, C_DH)
    oc = _paged_attention(qcp, kc_new, vc_new, cfar, clast, cnew, ck_r, cv_r, page_table, page,
                          k_feature_major=False, mask_args=(keys, tau, jmax, selnew))
    yc_s = _odd_tail(zs1, oc)
    ys = _out_proj([hs_pad(yc_s)], [w_out_c], hs1, dbp, final_gain=final_norm)
    y_sample = ys[:db].reshape(db, 1, d)
    sc_k = zs1[:, ODD_OFF[1]:ODD_OFF[2]].reshape(1, db, 1, C_KV, C_DH)
    sc_v = zs1[:, ODD_OFF[2]:ODD_OFF[3]].reshape(1, db, 1, C_KV, C_DH)
    sc_idx = zs1[:, None, ODD_OFF[5]:ODD_OFF[6]][None]

    return (y_prompt, y_sample, pa_k, pa_v, pb_s[None], pb_conv, pc_k, pc_v, pc_idx,
            sa_k, sa_v, sb_s[None], sb_conv, sc_k, sc_v, sc_idx)
```

```python
import functools
import math

import jax
import jax.numpy as jnp
import numpy as np
from jax import lax
from jax.experimental import pallas as pl
from jax.experimental.pallas import tpu as pltpu

F32 = jnp.float32
BF16 = jnp.bfloat16
I32 = jnp.int32

EPS = 1e-6
N_BUCKETS = 32
MAX_DIST = 128
FAR_DIST = MAX_DIST

A_HEADS, A_KV, A_GROUP, A_DH, A_DV = 8, 2, 4, 64, 128
B_HEADS, B_DK, B_DV, CONV_W = 4, 128, 128, 4
C_HEADS, C_KV, C_GROUP, C_DH = 8, 2, 4, 128
IDX_HEADS, IDX_DIM, TOPK_MAX = 8, 64, 256
B_QKV = 2 * B_HEADS * B_DK + B_HEADS * B_DV

EVEN_COLS = (A_HEADS * 2 * A_DH, A_KV * 2 * A_DH, A_KV * A_DV, A_HEADS * A_DV, B_QKV, B_HEADS * B_DV, B_HEADS, B_HEADS)
ODD_COLS = (C_HEADS * C_DH, C_KV * C_DH, C_KV * C_DH, C_HEADS * C_DH, IDX_HEADS * IDX_DIM, IDX_DIM, IDX_HEADS)
EVEN_OFF = tuple(int(v) for v in np.cumsum((0,) + EVEN_COLS))
ODD_OFF = tuple(int(v) for v in np.cumsum((0,) + ODD_COLS))

LANES = 128
SUBLANES = 8
BLK = 128
CHUNK = 512
SUBS = CHUNK // BLK
NEG = -1e30
INT_MIN = -2 ** 31
INT_MAX = 2 ** 31 - 1
LOG2E = math.log2(math.e)
VMEM_LIMIT = 56 * 1024 * 1024
HI = lax.Precision.HIGHEST

NT = (((1,), (1,)), ((), ()))


def _round_up(x, m):
    return (x + m - 1) // m * m


def _cparams(sem):
    return pltpu.CompilerParams(dimension_semantics=sem, vmem_limit_bytes=VMEM_LIMIT)


def _silu(x):
    return x * jax.nn.sigmoid(x)


def _sortable_key(score):
    score = jnp.where(score == 0.0, 0.0, score)
    bits = pltpu.bitcast(score, I32)
    return bits ^ ((bits >> 31) & jnp.int32(0x7FFFFFFF))


def _lane_tile(x, n):
    return jnp.concatenate([x] * n, axis=1)


def _proj_kernel(x_ref, g_ref, w_ref, of_ref, ob_ref, xn_ref, *, bf16_tiles):
    j = pl.program_id(1)

    @pl.when(j == 0)
    def _():
        xf = x_ref[...]
        y = xf * lax.rsqrt(jnp.mean(xf * xf, axis=-1, keepdims=True) + EPS)
        xn_ref[...] = (y * g_ref[...]).astype(BF16)

    acc = jnp.dot(xn_ref[...], w_ref[...], preferred_element_type=F32)
    of_ref[...] = acc

    @pl.when((j >= bf16_tiles[0]) & (j < bf16_tiles[1]))
    def _():
        ob_ref[...] = acc.astype(BF16)


def _norm_proj(x, gain, w_bf16, tm, tn, bf16_tiles):
    m, d = x.shape
    n = w_bf16.shape[1]
    t0, t1 = bf16_tiles
    return pl.pallas_call(
        functools.partial(_proj_kernel, bf16_tiles=bf16_tiles),
        grid=(m // tm, n // tn),
        in_specs=[pl.BlockSpec((tm, d), lambda i, j: (i, 0)),
                  pl.BlockSpec((1, d), lambda i, j: (0, 0)),
                  pl.BlockSpec((d, tn), lambda i, j: (0, j))],
        out_specs=[pl.BlockSpec((tm, tn), lambda i, j: (i, j)),
                   pl.BlockSpec((tm, tn), lambda i, j: (i, jnp.clip(j - t0, 0, t1 - t0 - 1)))],
        out_shape=[jax.ShapeDtypeStruct((m, n), F32), jax.ShapeDtypeStruct((m, (t1 - t0) * tn), BF16)],
        scratch_shapes=[pltpu.VMEM((tm, d), BF16)],
        compiler_params=_cparams(("parallel", "arbitrary")),
        name="norm_proj",
    )(x, gain.reshape(1, d), w_bf16)


def _outproj_kernel(*refs, n_lhs, final_norm):
    ys = refs[:n_lhs]
    ws = refs[n_lhs:2 * n_lhs]
    h_ref = refs[2 * n_lhs]
    pos = 2 * n_lhs + 1
    fn_ref = refs[pos] if final_norm else None
    out_ref = refs[-1]
    acc = h_ref[...]
    for y_ref, w_ref in zip(ys, ws):
        acc = acc + jnp.dot(y_ref[...], w_ref[...], preferred_element_type=F32)
    if final_norm:
        y = acc * lax.rsqrt(jnp.mean(acc * acc, axis=-1, keepdims=True) + EPS)
        acc = y * fn_ref[...]
    out_ref[...] = acc


def _out_proj(ys, ws, h, tm, final_gain=None):
    m, n = h.shape
    n_lhs = len(ys)
    in_specs = [pl.BlockSpec((tm, y.shape[1]), lambda i: (i, 0)) for y in ys]
    in_specs += [pl.BlockSpec(w.shape, lambda i: (0, 0)) for w in ws]
    in_specs += [pl.BlockSpec((tm, n), lambda i: (i, 0))]
    args = list(ys) + list(ws) + [h]
    if final_gain is not None:
        in_specs += [pl.BlockSpec((1, n), lambda i: (0, 0))]
        args += [final_gain.reshape(1, n)]
    return pl.pallas_call(
        functools.partial(_outproj_kernel, n_lhs=n_lhs, final_norm=final_gain is not None),
        grid=(m // tm,),
        in_specs=in_specs,
        out_specs=pl.BlockSpec((tm, n), lambda i: (i, 0)),
        out_shape=jax.ShapeDtypeStruct((m, n), F32),
        compiler_params=_cparams(("parallel",)),
        name="out_proj",
    )(*args)


def _flash_steps(ss, v1s, m_refs):
    each = range(len(ss))
    m_old = [m_refs[i][0][m_refs[i][2]] for i in each]
    m_new = [jnp.maximum(m_old[i], jnp.max(ss[i], axis=1, keepdims=True)) for i in each]
    alpha = [jnp.exp2(m_old[i] - m_new[i]) for i in each]
    p = [jnp.exp2(ss[i] - _lane_tile(m_new[i], ss[i].shape[1] // LANES)).astype(BF16) for i in each]
    pv = [jnp.dot(p[i], v1s[i], preferred_element_type=F32) for i in each]
    for i in each:
        m_ref, acc_ref, idx = m_refs[i]
        acc_ref[idx] = _lane_tile(alpha[i], 2) * acc_ref[idx] + pv[i]
        m_ref[idx] = m_new[i]


def _far_chunk_loop(n_far, run_chunks):
    def quad(i, carry):
        run_chunks([4 * i, 4 * i + 1, 4 * i + 2, 4 * i + 3])
        return carry

    n_quad = lax.shift_right_logical(n_far, 2)
    lax.fori_loop(0, n_quad, quad, 0)

    @pl.when(jnp.bitwise_and(n_far, 2) == 2)
    def _():
        run_chunks([4 * n_quad, 4 * n_quad + 1])

    @pl.when(jnp.bitwise_and(n_far, 1) == 1)
    def _():
        run_chunks([n_far - 1])


def _build_near_bias(strip_ref, t_prev, t_diag, r, lead=()):
    for sb in range(2 * SUBS):
        rel = sb - SUBS - r
        tile = jnp.where(rel == 0, t_diag, jnp.where(rel == -1, t_prev, jnp.where(rel < -1, 0.0, NEG)))
        strip_ref[lead + (sb,)] = tile


def _diff_attn_kernel(q_ref, k_ref, v_ref, gate_ref, tiles_ref, lamv_ref, subln_ref, out_ref,
                      qm_ref, strip_ref, m_ref, acc_ref, *, lam_init, nq_real):
    qi = pl.program_id(2)

    @pl.when(qi >= nq_real)
    def _():
        out_ref[...] = jnp.zeros(out_ref.shape, out_ref.dtype)

    @pl.when(qi < nq_real)
    def _():
        lane = lax.broadcasted_iota(I32, (BLK, 2 * A_DH), 1)
        for g in range(A_GROUP):
            qg = q_ref[:, g * 2 * A_DH:(g + 1) * 2 * A_DH] * (A_DH ** -0.5 * LOG2E)
            for c in range(2):
                keep = (lane < A_DH) if c == 0 else (lane >= A_DH)
                qm_ref[c, g * BLK:(g + 1) * BLK, :] = jnp.where(keep, qg, 0.0).astype(BF16)
        m_ref[...] = jnp.full(m_ref.shape, NEG, F32)
        acc_ref[...] = jnp.zeros(acc_ref.shape, F32)
        r = jnp.bitwise_and(qi, SUBS - 1)
        jd = lax.shift_right_logical(qi, int(math.log2(SUBS)))
        _build_near_bias(strip_ref, tiles_ref[0], tiles_ref[1], r)
        ones = jnp.ones((CHUNK, A_DV), BF16)

        def run_chunks(js, first_sbs=None):
            starts = [pl.multiple_of(j * CHUNK, CHUNK) for j in js]
            k = jnp.concatenate([k_ref[pl.ds(st, CHUNK), :] for st in starts], axis=0)
            v1 = jnp.concatenate(
                [jnp.concatenate([v_ref[pl.ds(st, CHUNK), :], ones], axis=1) for st in starts], axis=0)
            ss = [lax.dot_general(qm_ref[c], k, NT, preferred_element_type=F32) for c in range(2)]
            if first_sbs is not None:
                bias = jnp.concatenate([strip_ref[sb + i] for sb in first_sbs for i in range(SUBS)], axis=1)
                ss = [s + bias for s in ss]
            _flash_steps(ss, [v1, v1], [(m_ref, acc_ref, c) for c in range(2)])

        _far_chunk_loop(jnp.maximum(jd - 1, 0), run_chunks)

        @pl.when(jd >= 1)
        def _():
            run_chunks([jd - 1, jd], [0, SUBS])

        @pl.when(jd == 0)
        def _():
            run_chunks([jd], [SUBS])

        lv = lamv_ref[...]
        lam = (jnp.exp(jnp.sum(lv[0:1] * lv[1:2], axis=1, keepdims=True))
               - jnp.exp(jnp.sum(lv[2:3] * lv[3:4], axis=1, keepdims=True)) + lam_init)
        a0 = acc_ref[0]
        a1 = acc_ref[1]
        o = a0[:, :A_DV] / a0[:, A_DV:] - lam * (a1[:, :A_DV] / a1[:, A_DV:])
        y = o * lax.rsqrt(jnp.mean(o * o, axis=-1, keepdims=True) + EPS)
        y = (y * subln_ref[...]) * (1.0 - lam_init)
        for g in range(A_GROUP):
            gate = gate_ref[:, g * A_DV:(g + 1) * A_DV]
            out_ref[:, g * A_DV:(g + 1) * A_DV] = (y[g * BLK:(g + 1) * BLK] * _silu(gate)).astype(BF16)


def _diff_attn_prompt(zf, zb, zb_col0, near_tiles, lamv, subln, nb, lp, nq_real, lam_init):
    nq = lp // BLK
    qw = A_GROUP * 2 * A_DH
    k_blk0 = (EVEN_OFF[1] - zb_col0) // (2 * A_DH)
    v_blk0 = (EVEN_OFF[2] - zb_col0) // A_DV
    g_blk0 = EVEN_OFF[3] // (A_GROUP * A_DV)
    rows = A_GROUP * BLK
    return pl.pallas_call(
        functools.partial(_diff_attn_kernel, lam_init=lam_init, nq_real=nq_real),
        grid=(nb, A_KV, nq),
        in_specs=[pl.BlockSpec((BLK, qw), lambda b, h, i: (b * nq + i, h)),
                  pl.BlockSpec((lp, 2 * A_DH), lambda b, h, i: (b, k_blk0 + h)),
                  pl.BlockSpec((lp, A_DV), lambda b, h, i: (b, v_blk0 + h)),
                  pl.BlockSpec((BLK, A_GROUP * A_DV), lambda b, h, i: (b * nq + i, g_blk0 + h)),
                  pl.BlockSpec((None, 2, rows, BLK), lambda b, h, i: (h, 0, 0, 0)),
                  pl.BlockSpec((4, A_DH), lambda b, h, i: (0, 0)),
                  pl.BlockSpec((1, A_DV), lambda b, h, i: (0, 0))],
        out_specs=pl.BlockSpec((BLK, A_GROUP * A_DV), lambda b, h, i: (b * nq + i, h)),
        out_shape=jax.ShapeDtypeStruct((nb * lp, A_HEADS * A_DV), BF16),
        scratch_shapes=[pltpu.VMEM((2, rows, 2 * A_DH), BF16),
                        pltpu.VMEM((2 * SUBS, rows, BLK), F32),
                        pltpu.VMEM((2, rows, LANES), F32),
                        pltpu.VMEM((2, rows, 2 * A_DV), F32)],
        compiler_params=_cparams(("parallel", "parallel", "arbitrary")),
        name="diff_attn_prompt",
    )(zf, zb, zb, zf, near_tiles, lamv, subln.reshape(1, A_DV))


def _softplus(x):
    return jnp.maximum(x, 0.0) + jnp.log1p(jnp.exp(-jnp.abs(x)))


def _split_bf16(a):
    hi = a.astype(BF16)
    return hi, (a - hi.astype(F32)).astype(BF16)


def _dot3(a, b):
    (ah, al), (bh, bl) = a, b
    return (jnp.dot(ah, bh, preferred_element_type=F32)
            + (jnp.dot(ah, bl, preferred_element_type=F32) + jnp.dot(al, bh, preferred_element_type=F32)))


def _gdn_chunk_math(s0, q, k, v, gb, betab, row_ge, row_gt):
    n = len(q)
    c = q[0].shape[0]
    each = range(n)
    dot = functools.partial(jnp.dot, preferred_element_type=F32)
    dot_nt = functools.partial(lax.dot_general, dimension_numbers=NT, preferred_element_type=F32)
    ltri = jnp.where(row_ge, 1.0, 0.0).astype(F32)
    gcum = [jnp.dot(ltri, gb[i], precision=HI, preferred_element_type=F32) for i in each]
    decay = [jnp.where(row_ge, jnp.exp(jnp.where(row_ge, gcum[i] - gcum[i].T, 0.0)), 0.0) for i in each]
    kb = [k[i].astype(BF16) for i in each]
    qb = [q[i].astype(BF16) for i in each]
    s0b = [s0[i].astype(BF16) for i in each]
    kk = [dot_nt(kb[i], kb[i]) for i in each]
    x = [-jnp.where(row_gt, betab[i] * decay[i] * kk[i], 0.0) for i in each]
    tm = list(x)
    p = list(x)
    for _ in range(int(math.log2(c)) - 1):
        ps = [_split_bf16(p[i]) for i in each]
        p = [_dot3(ps[i], ps[i]) for i in each]
        ps = [_split_bf16(p[i]) for i in each]
        tms = [_split_bf16(tm[i]) for i in each]
        tm = [tm[i] + p[i] + _dot3(tms[i], ps[i]) for i in each]
    eg = [jnp.exp(gcum[i]) for i in each]
    ks = [dot(kb[i], s0b[i]) for i in each]
    rhs = [betab[i] * (v[i] - eg[i] * ks[i]) for i in each]
    u = [rhs[i] + dot(tm[i].astype(BF16), rhs[i].astype(BF16)) for i in each]
    ub = [u[i].astype(BF16) for i in each]
    qk = [dot_nt(qb[i], kb[i]) * decay[i] for i in each]
    o = [eg[i] * dot(qb[i], s0b[i]) + dot(qk[i].astype(BF16), ub[i]) for i in each]
    glast = [gcum[i][c - 1:c, :] for i in each]
    kd = [k[i] * jnp.exp(glast[i] - gcum[i]) for i in each]
    s_new = [jnp.exp(glast[i]) * s0[i] + dot(kd[i].T.astype(BF16), ub[i]) for i in each]
    return s_new, o


def _gdn_prompt_kernel(xq_ref, xk_ref, xv_ref, gate_ref, ab_ref, convw_ref, gp_ref, normg_ref,
                       y_ref, sfin_ref, xbuf_ref, s_ref, *, seq_len):
    ci = pl.program_id(0)
    hw = B_HEADS * B_DK
    nb = xq_ref.shape[0]

    @pl.when(ci == 0)
    def _():
        xbuf_ref[:, 0:SUBLANES, :] = jnp.zeros((nb, SUBLANES, B_QKV), F32)
        s_ref[...] = jnp.zeros(s_ref.shape, F32)

    row = lax.broadcasted_iota(I32, (BLK, BLK), 0)
    col = lax.broadcasted_iota(I32, (BLK, BLK), 1)
    row_ge = row >= col
    row_gt = row > col
    valid = (ci * BLK + row) < seq_len
    gp = gp_ref[...]
    chains = [(b, h) for b in range(nb) for h in range(B_HEADS)]
    qs, ks, vs, gbs, betabs = [], [], [], [], []
    for b in range(nb):
        xbuf_ref[b, SUBLANES:SUBLANES + BLK, 0:hw] = xq_ref[b]
        xbuf_ref[b, SUBLANES:SUBLANES + BLK, hw:2 * hw] = xk_ref[b]
        xbuf_ref[b, SUBLANES:SUBLANES + BLK, 2 * hw:3 * hw] = xv_ref[b]
        conv = jnp.zeros((BLK, B_QKV), F32)
        for i in range(CONV_W):
            conv = conv + convw_ref[i:i + 1, :] * xbuf_ref[b, pl.ds(SUBLANES - (CONV_W - 1) + i, BLK), :]
        tail = xbuf_ref[b, BLK:BLK + SUBLANES, :]
        xbuf_ref[b, 0:SUBLANES, :] = tail
        act = _silu(conv)
        ab = ab_ref[b]
        for h in range(B_HEADS):
            q = act[:, h * B_DK:(h + 1) * B_DK]
            k = act[:, hw + h * B_DK:hw + (h + 1) * B_DK]
            v = act[:, 2 * hw + h * B_DV:2 * hw + (h + 1) * B_DV]
            q = q * lax.rsqrt(jnp.sum(q * q, axis=-1, keepdims=True) + EPS) * (B_DK ** -0.5)
            k = k * lax.rsqrt(jnp.sum(k * k, axis=-1, keepdims=True) + EPS)
            a_raw = jnp.broadcast_to(ab[:, h:h + 1], (BLK, BLK))
            b_raw = jnp.broadcast_to(ab[:, B_HEADS + h:B_HEADS + h + 1], (BLK, BLK))
            a_log = gp[0:1, h:h + 1]
            dt_b = gp[1:2, h:h + 1]
            qs.append(q)
            ks.append(k)
            vs.append(v)
            gbs.append(jnp.where(valid, -jnp.exp(a_log) * _softplus(a_raw + dt_b), 0.0))
            betabs.append(jnp.where(valid, jax.nn.sigmoid(b_raw), 0.0))
    s_new, outs = _gdn_chunk_math([s_ref[b, h] for b, h in chains], qs, ks, vs, gbs, betabs, row_ge, row_gt)
    for (b, h), s_bh, o in zip(chains, s_new, outs):
        s_ref[b, h] = s_bh
        y = o * lax.rsqrt(jnp.mean(o * o, axis=-1, keepdims=True) + EPS) * normg_ref[...]
        gate = gate_ref[b, :, h * B_DV:(h + 1) * B_DV]
        y_ref[b, :, h * B_DV:(h + 1) * B_DV] = (y * _silu(gate)).astype(BF16)

    @pl.when(ci == pl.num_programs(0) - 1)
    def _():
        sfin_ref[...] = s_ref[...]


def _gdn_prompt(zf, conv_w, a_log, dt_bias, norm_g, nb, lp, seq_len):
    nc = lp // BLK
    hw = B_HEADS * B_DK
    c0 = EVEN_OFF[4] // hw
    g0 = EVEN_OFF[5] // hw
    ab0 = EVEN_OFF[6] // LANES
    gp = jnp.zeros((SUBLANES, LANES), F32).at[0, :B_HEADS].set(a_log).at[1, :B_HEADS].set(dt_bias)
    z3 = zf.reshape(nb, lp, zf.shape[1])
    y, s_fin = pl.pallas_call(
        functools.partial(_gdn_prompt_kernel, seq_len=seq_len),
        grid=(nc,),
        in_specs=[pl.BlockSpec((nb, BLK, hw), lambda i: (0, i, c0)),
                  pl.BlockSpec((nb, BLK, hw), lambda i: (0, i, c0 + 1)),
                  pl.BlockSpec((nb, BLK, hw), lambda i: (0, i, c0 + 2)),
                  pl.BlockSpec((nb, BLK, hw), lambda i: (0, i, g0)),
                  pl.BlockSpec((nb, BLK, LANES), lambda i: (0, i, ab0)),
                  pl.BlockSpec((CONV_W, B_QKV), lambda i: (0, 0)),
                  pl.BlockSpec((SUBLANES, LANES), lambda i: (0, 0)),
                  pl.BlockSpec((1, B_DV), lambda i: (0, 0))],
        out_specs=[pl.BlockSpec((nb, BLK, hw), lambda i: (0, i, 0)),
                   pl.BlockSpec((nb, B_HEADS, B_DK, B_DV), lambda i: (0, 0, 0, 0))],
        out_shape=[jax.ShapeDtypeStruct((nb, lp, hw), BF16),
                   jax.ShapeDtypeStruct((nb, B_HEADS, B_DK, B_DV), F32)],
        scratch_shapes=[pltpu.VMEM((nb, BLK + SUBLANES, B_QKV), F32),
                        pltpu.VMEM((nb, B_HEADS, B_DK, B_DV), F32)],
        compiler_params=_cparams(("arbitrary",)),
        name="gdn_prompt",
    )(z3, z3, z3, z3, z3, conv_w, gp, norm_g.reshape(1, B_DV))
    return y.reshape(nb * lp, hw), s_fin


def _kth_largest(count_fn, k_sel, shape):
    def step(i, tau):
        cand = tau + lax.shift_left(jnp.int32(1), 31 - i)
        cnt = count_fn(lambda key, idx: key >= cand)
        return jnp.where(cnt >= k_sel, cand, tau)

    return lax.fori_loop(0, 32, step, jnp.full(shape, INT_MIN, I32))


def _tie_cutoff(count_fn, tau, need, n_index_bits, shape):
    def step(i, jm):
        cand = jm | lax.shift_left(jnp.int32(1), n_index_bits - 1 - i)
        cnt = count_fn(lambda key, idx: (key == tau) & (idx < cand))
        return jnp.where(cnt < need, cand, jm)

    return lax.fori_loop(0, n_index_bits, step, jnp.zeros(shape, I32))


def _sparse_attn_kernel(qi_ref, kw_ref, kidx_ref, qc_ref, kc_ref, vc_ref, g0_ref, g1_ref, tiles_ref, out_ref,
                        keys_ref, qm_ref, qh_ref, strip_ref, tau_ref, jmax_ref, m_ref, acc_ref,
                        *, k_sel, n_index_bits, nq_real):
    qb = pl.program_id(1)

    @pl.when(qb >= nq_real)
    def _():
        out_ref[...] = jnp.zeros(out_ref.shape, out_ref.dtype)

    @pl.when(qb < nq_real)
    def _():
        lane = lax.broadcasted_iota(I32, (BLK, LANES), 1)
        key_pos = lax.broadcasted_iota(I32, (CHUNK, BLK), 0)
        qry_pos = lax.broadcasted_iota(I32, (CHUNK, BLK), 1)
        r = jnp.bitwise_and(qb, SUBS - 1)
        jd = lax.shift_right_logical(qb, int(math.log2(SUBS)))

        for p in range(IDX_HEADS // 2):
            pair = qi_ref[:, p * LANES:(p + 1) * LANES] * (IDX_DIM ** -0.5)
            swapped = pltpu.roll(pair, IDX_DIM, 1)
            for e, src in enumerate((pair, swapped)):
                h = 2 * p + e
                qm_ref[h * BLK:(h + 1) * BLK, :] = jnp.where(lane < IDX_DIM, src, 0.0).astype(BF16)
        w_rows = kw_ref[...].T * (IDX_HEADS ** -0.5)

        def scores_t(j):
            start = pl.multiple_of(j * CHUNK, CHUNK)
            d = lax.dot_general(kidx_ref[pl.ds(start, CHUNK), :], qm_ref[...], NT, preferred_element_type=F32)
            sc = jnp.zeros((CHUNK, BLK), F32)
            for h in range(IDX_HEADS):
                sc = sc + w_rows[IDX_DIM + h:IDX_DIM + h + 1, :] * jnp.maximum(d[:, h * BLK:(h + 1) * BLK], 0.0)
            return sc

        def score_body(j, carry):
            keys_ref[j] = _sortable_key(scores_t(j))
            return carry

        lax.fori_loop(0, jd, score_body, 0)
        admissible = (jd * CHUNK + key_pos) <= (qb * BLK + qry_pos)
        keys_ref[jd] = _sortable_key(jnp.where(admissible, scores_t(jd), -jnp.inf))

        def count_fn(pred):
            part = SUBLANES * SUBLANES

            def body(j, acc):
                hit = jnp.where(pred(keys_ref[j], j * CHUNK + key_pos), 1.0, 0.0)
                return acc + jnp.sum(hit.reshape(CHUNK // part, part, BLK), axis=0)
            acc = lax.fori_loop(0, jd + 1, body, jnp.zeros((part, BLK), F32))
            return jnp.sum(acc, axis=0, keepdims=True).astype(I32)

        tau = _kth_largest(count_fn, k_sel, (1, BLK))
        tau_ref[...] = tau
        jmax_ref[...] = jnp.full((1, BLK), INT_MAX, I32)
        need = k_sel - count_fn(lambda key, idx: key > tau)
        n_eq = count_fn(lambda key, idx: key == tau)
        excess = jnp.max(jnp.where(n_eq > need, 1.0, 0.0)) > 0.5

        @pl.when(excess)
        def _():
            jmax_ref[...] = _tie_cutoff(count_fn, tau, need, n_index_bits, (1, BLK))

        jmax = jmax_ref[...]

        for h in range(C_KV):
            for g in range(C_GROUP):
                col = (h * C_GROUP + g) * C_DH
                qh_ref[h, g * BLK:(g + 1) * BLK, :] = (
                    qc_ref[:, col:col + C_DH] * (C_DH ** -0.5 * LOG2E)).astype(BF16)
            _build_near_bias(strip_ref, tiles_ref[h, 0], tiles_ref[h, 1], r, lead=(h,))
        m_ref[...] = jnp.full(m_ref.shape, NEG, F32)
        acc_ref[...] = jnp.zeros(acc_ref.shape, F32)
        ones = jnp.ones((CHUNK, C_DH), BF16)

        def run_chunks(js, first_sbs=None):
            starts = [pl.multiple_of(j * CHUNK, CHUNK) for j in js]
            masks = []
            for j in js:
                key = keys_ref[j]
                tie_ok = jnp.where((j * CHUNK + key_pos) <= jmax, 0.0, NEG)
                sel_t = jnp.where(key > tau, 0.0, jnp.where(key == tau, tie_ok, NEG))
                masks += [sel_t[i * BLK:(i + 1) * BLK, :].T for i in range(SUBS)]
            selneg = jnp.concatenate(masks, axis=1)
            selneg = jnp.concatenate([selneg] * C_GROUP, axis=0)
            ss, v1s = [], []
            for h in range(C_KV):
                cols = slice(h * C_DH, (h + 1) * C_DH)
                k = jnp.concatenate([kc_ref[pl.ds(st, CHUNK), cols] for st in starts], axis=0)
                v1s.append(jnp.concatenate(
                    [jnp.concatenate([vc_ref[pl.ds(st, CHUNK), cols], ones], axis=1) for st in starts], axis=0))
                s = lax.dot_general(qh_ref[h], k, NT, preferred_element_type=F32) + selneg
                if first_sbs is not None:
                    s = s + jnp.concatenate(
                        [strip_ref[h, sb + i] for sb in first_sbs for i in range(SUBS)], axis=1)
                ss.append(s)
            _flash_steps(ss, v1s, [(m_ref, acc_ref, h) for h in range(C_KV)])

        _far_chunk_loop(jnp.maximum(jd - 1, 0), run_chunks)

        @pl.when(jd >= 1)
        def _():
            run_chunks([jd - 1, jd], [0, SUBS])

        @pl.when(jd == 0)
        def _():
            run_chunks([jd], [SUBS])

        half = (C_HEADS // 2) * C_DH
        for h in range(C_KV):
            a = acc_ref[h]
            o = a[:, :C_DH] / a[:, C_DH:]
            gref = g0_ref if h == 0 else g1_ref
            for g in range(C_GROUP):
                gate = gref[:, g * C_DH:(g + 1) * C_DH]
                out_ref[:, h * half + g * C_DH:h * half + (g + 1) * C_DH] = (
                    o[g * BLK:(g + 1) * BLK] * _silu(gate)).astype(BF16)


def _sparse_attn_prompt(zf, zb, near_tiles, nb, lp, nq_real, k_sel):
    nq = lp // BLK
    half = (C_HEADS // 2) * C_DH
    qi0 = ODD_OFF[4] // (IDX_HEADS * IDX_DIM)
    ki0 = ODD_OFF[5] // LANES
    kc0 = ODD_OFF[1] // (C_KV * C_DH)
    vc0 = ODD_OFF[2] // (C_KV * C_DH)
    g0 = ODD_OFF[3] // half
    rows = C_GROUP * BLK
    n_index_bits = max(1, int(math.ceil(math.log2(lp))))
    return pl.pallas_call(
        functools.partial(_sparse_attn_kernel, k_sel=k_sel, n_index_bits=n_index_bits, nq_real=nq_real),
        grid=(nb, nq),
        in_specs=[pl.BlockSpec((BLK, IDX_HEADS * IDX_DIM), lambda b, i: (b * nq + i, qi0)),
                  pl.BlockSpec((BLK, LANES), lambda b, i: (b * nq + i, ki0)),
                  pl.BlockSpec((lp, LANES), lambda b, i: (b, ki0)),
                  pl.BlockSpec((BLK, C_HEADS * C_DH), lambda b, i: (b * nq + i, 0)),
                  pl.BlockSpec((lp, C_KV * C_DH), lambda b, i: (b, kc0)),
                  pl.BlockSpec((lp, C_KV * C_DH), lambda b, i: (b, vc0)),
                  pl.BlockSpec((BLK, half), lambda b, i: (b * nq + i, g0)),
                  pl.BlockSpec((BLK, half), lambda b, i: (b * nq + i, g0 + 1)),
                  pl.BlockSpec((C_KV, 2, rows, BLK), lambda b, i: (0, 0, 0, 0))],
        out_specs=pl.BlockSpec((BLK, C_HEADS * C_DH), lambda b, i: (b * nq + i, 0)),
        out_shape=jax.ShapeDtypeStruct((nb * lp, C_HEADS * C_DH), BF16),
        scratch_shapes=[pltpu.VMEM((lp // CHUNK, CHUNK, BLK), I32),
                        pltpu.VMEM((IDX_HEADS * BLK, LANES), BF16),
                        pltpu.VMEM((C_KV, rows, C_DH), BF16),
                        pltpu.VMEM((C_KV, 2 * SUBS, rows, BLK), F32),
                        pltpu.VMEM((1, BLK), I32),
                        pltpu.VMEM((1, BLK), I32),
                        pltpu.VMEM((C_KV, rows, LANES), F32),
                        pltpu.VMEM((C_KV, rows, 2 * C_DH), F32)],
        compiler_params=_cparams(("parallel", "arbitrary")),
        name="sparse_attn_prompt",
    )(zf, zf, zb, zf, zb, zb, zf, zf, near_tiles)


PAGES_PER_STEP = 32
INDEX_PAGES_PER_STEP = 64
DEC_ROWS = 16


def _interleaved_pages(refs, page):
    halves = [jnp.concatenate([r[pl.ds(h, page, stride=2), :] for r in refs], axis=0) for h in range(2)]
    return jnp.concatenate(halves, axis=1).astype(BF16)


def _paged_attn_kernel(*refs, n_pages_step, masked, page, k_feature_major):
    if masked:
        pt_ref, tau_ref, jmax_ref, selnew_ref = refs[:4]
        refs = refs[4:]
    else:
        pt_ref = refs[0]
        refs = refs[1:]
    q_ref, knew_ref, vnew_ref, bfar_ref, blast_ref, b0_ref = refs[:6]
    refs = refs[6:]
    if masked:
        keys_ref = refs[0]
        refs = refs[1:]
    k_refs = refs[:n_pages_step]
    v_refs = refs[n_pages_step:2 * n_pages_step]
    out_ref, m_ref, l_ref, acc_ref = refs[2 * n_pages_step:]
    del pt_ref
    b = pl.program_id(0)
    j = pl.program_id(1)
    width = n_pages_step * page
    q = q_ref[...]

    @pl.when(j == 0)
    def _():
        s_new = jnp.sum(q.astype(F32) * knew_ref[...].astype(BF16).astype(F32), axis=1, keepdims=True) + b0_ref[...]
        v_new = jnp.broadcast_to(vnew_ref[...].astype(BF16).astype(F32), acc_ref.shape)
        if masked:
            take = selnew_ref[b] > 0
            m_ref[...] = jnp.where(take, s_new, NEG)
            l_ref[...] = jnp.where(take, 1.0, 0.0) * jnp.ones(l_ref.shape, F32)
            acc_ref[...] = jnp.where(take, v_new, 0.0)
        else:
            m_ref[...] = s_new
            l_ref[...] = jnp.ones(l_ref.shape, F32)
            acc_ref[...] = v_new

    if k_feature_major:
        kcat = jnp.concatenate([r[...] for r in k_refs], axis=1).astype(BF16)
        s = jnp.dot(q, kcat, preferred_element_type=F32)
    else:
        s = lax.dot_general(q, _interleaved_pages(k_refs, page), NT, preferred_element_type=F32)
    vcat = _interleaved_pages(v_refs, page)
    s = s + jnp.where(j == pl.num_programs(1) - 1, blast_ref[...], bfar_ref[...])
    if masked:
        key = keys_ref[...]
        idx = j * width + lax.broadcasted_iota(I32, (1, width), 1)
        tau = tau_ref[b]
        sel = (key > tau) | ((key == tau) & (idx <= jmax_ref[b]))
        s = jnp.where(sel, s, NEG)
    m_old = m_ref[...]
    m_new = jnp.maximum(m_old, jnp.max(s, axis=1, keepdims=True))
    alpha = jnp.exp(m_old - m_new)
    p = jnp.exp(s - m_new[:, 0:1])
    l_ref[...] = alpha * l_ref[...] + jnp.sum(p, axis=1, keepdims=True)
    acc_ref[...] = _lane_tile(alpha, 2) * acc_ref[...] + jnp.dot(p.astype(BF16), vcat, preferred_element_type=F32)
    m_ref[...] = m_new

    @pl.when(j == pl.num_programs(1) - 1)
    def _():
        o = acc_ref[...]
        rowi = lax.broadcasted_iota(I32, (DEC_ROWS, LANES), 0)
        upper = (rowi >= DEC_ROWS // 2) if not masked else ((rowi >= DEC_ROWS // 4) & (rowi < DEC_ROWS // 2))
        out_ref[...] = jnp.where(upper, o[:, LANES:], o[:, :LANES]) / l_ref[...]


def _pages_per_step(n_pages, g=PAGES_PER_STEP):
    while n_pages % g:
        g //= 2
    return g


def _paged_attention(qprime, k_new, v_new, bias_far, bias_last, bias0, k_cache, v_cache, page_table, page,
                     k_feature_major, mask_args=None):
    db = qprime.shape[0]
    n_pages = page_table.shape[1]
    width = qprime.shape[2]
    g = _pages_per_step(n_pages)
    n_steps = n_pages // g
    masked = mask_args is not None
    n_pref = 4 if masked else 1

    def page_map(gi):
        return lambda b, j, pt, *_: (pt[b * n_pages + j * g + gi], 0, 0)

    in_specs = [pl.BlockSpec((None, DEC_ROWS, width), lambda b, j, *_: (b, 0, 0)),
                pl.BlockSpec((None, 1, width), lambda b, j, *_: (b, 0, 0)),
                pl.BlockSpec((None, 1, width), lambda b, j, *_: (b, 0, 0)),
                pl.BlockSpec((DEC_ROWS, g * page), lambda b, j, *_: (0, 0)),
                pl.BlockSpec((DEC_ROWS, g * page), lambda b, j, *_: (0, 0)),
                pl.BlockSpec((DEC_ROWS, LANES), lambda b, j, *_: (0, 0))]
    args = [qprime, k_new, v_new, bias_far, bias_last, bias0]
    prefetch = [page_table.reshape(-1)]
    if masked:
        keys, tau, jmax, selnew = mask_args
        prefetch += [tau, jmax, selnew]
        in_specs += [pl.BlockSpec((None, 1, g * page), lambda b, j, *_: (b, 0, j))]
        args += [keys]
    in_specs += [pl.BlockSpec((None,) + k_cache.shape[1:], page_map(gi)) for gi in range(g)]
    in_specs += [pl.BlockSpec((None,) + v_cache.shape[1:], page_map(gi)) for gi in range(g)]
    args += [k_cache] * g + [v_cache] * g
    grid_spec = pltpu.PrefetchScalarGridSpec(
        num_scalar_prefetch=n_pref,
        grid=(db, n_steps),
        in_specs=in_specs,
        out_specs=pl.BlockSpec((None, DEC_ROWS, LANES), lambda b, j, *_: (b, 0, 0)),
        scratch_shapes=[pltpu.VMEM((DEC_ROWS, LANES), F32),
                        pltpu.VMEM((DEC_ROWS, LANES), F32),
                        pltpu.VMEM((DEC_ROWS, width), F32)])
    return pl.pallas_call(
        functools.partial(_paged_attn_kernel, n_pages_step=g, masked=masked, page=page,
                          k_feature_major=k_feature_major),
        grid_spec=grid_spec,
        out_shape=jax.ShapeDtypeStruct((db, DEC_ROWS, LANES), F32),
        compiler_params=_cparams(("parallel", "arbitrary")),
        name="paged_attn_masked" if masked else "paged_attn",
    )(*prefetch, *args)


def _paged_index_kernel(pt_ref, q_ref, w_ref, knew_ref, *refs, n_pages_step, page):
    k_refs = refs[:n_pages_step]
    out_ref = refs[n_pages_step]
    del pt_ref
    j = pl.program_id(1)
    last = pl.num_programs(1) - 1
    q = q_ref[...]
    w = w_ref[...]

    def score(kmat_t):
        d = jnp.dot(q, kmat_t.astype(BF16), preferred_element_type=F32)
        return jnp.sum(jnp.maximum(d, 0.0) * w[:, 0:1], axis=0, keepdims=True)

    @pl.when(j < last)
    def _():
        kcat = jnp.concatenate([r[...] for r in k_refs], axis=1)
        out_ref[...] = _sortable_key(score(kcat))

    @pl.when(j == last)
    def _():
        sc = score(knew_ref[...])
        lane = lax.broadcasted_iota(I32, (1, page), 1)
        sc = jnp.where(lane == 0, sc, -jnp.inf)
        pad = jnp.full((1, (n_pages_step - 1) * page), -jnp.inf, F32)
        full = jnp.concatenate([sc, pad], axis=1) if n_pages_step > 1 else sc
        out_ref[...] = _sortable_key(full)


def _paged_index_scores(qidx, wcol, k_new_tile, idx_cache_t, page_table):
    db = qidx.shape[0]
    n_pages = page_table.shape[1]
    page = idx_cache_t.shape[2]
    g = _pages_per_step(n_pages, INDEX_PAGES_PER_STEP)
    n_steps = n_pages // g

    def page_map(gi):
        return lambda b, j, pt: (pt[b * n_pages + jnp.minimum(j, n_steps - 1) * g + gi], 0, 0)

    grid_spec = pltpu.PrefetchScalarGridSpec(
        num_scalar_prefetch=1,
        grid=(db, n_steps + 1),
        in_specs=[pl.BlockSpec((None, IDX_HEADS, IDX_DIM), lambda b, j, pt: (b, 0, 0)),
                  pl.BlockSpec((None, IDX_HEADS, LANES), lambda b, j, pt: (b, 0, 0)),
                  pl.BlockSpec((None, IDX_DIM, page), lambda b, j, pt: (b, 0, 0))]
        + [pl.BlockSpec((None, IDX_DIM, page), page_map(gi)) for gi in range(g)],
        out_specs=pl.BlockSpec((None, 1, g * page), lambda b, j, pt: (b, 0, j)))
    return pl.pallas_call(
        functools.partial(_paged_index_kernel, n_pages_step=g, page=page),
        grid_spec=grid_spec,
        out_shape=jax.ShapeDtypeStruct((db, 1, (n_steps + 1) * g * page), I32),
        compiler_params=_cparams(("parallel", "arbitrary")),
        name="paged_index_scores",
    )(page_table.reshape(-1), qidx, wcol, k_new_tile, *([idx_cache_t] * g))


def _select_kernel(keys_ref, tau_ref, jmax_ref, *, k_sel, n_index_bits):
    keys = keys_ref[...]
    idx = lax.broadcasted_iota(I32, keys.shape, 1)
    shape = (keys.shape[0], 1)

    def count_fn(pred):
        return jnp.sum(jnp.where(pred(keys, idx), 1, 0), axis=1, keepdims=True)

    tau = _kth_largest(count_fn, k_sel, shape)
    need = k_sel - count_fn(lambda key, i: key > tau)
    jmax = _tie_cutoff(count_fn, tau, need, n_index_bits, shape)
    tau_ref[...] = jnp.broadcast_to(tau, tau_ref.shape)
    jmax_ref[...] = jnp.broadcast_to(jmax, jmax_ref.shape)


def _select_rows(keys2d, k_sel):
    rows, width = keys2d.shape
    n_index_bits = max(1, int(math.ceil(math.log2(width))))
    return pl.pallas_call(
        functools.partial(_select_kernel, k_sel=k_sel, n_index_bits=n_index_bits),
        out_shape=[jax.ShapeDtypeStruct((rows, LANES), I32), jax.ShapeDtypeStruct((rows, LANES), I32)],
        compiler_params=pltpu.CompilerParams(vmem_limit_bytes=VMEM_LIMIT),
        name="select_rows",
    )(keys2d)


def _even_tail_kernel(z_ref, conv_ref, s_ref, oa_ref, convw_ref, gp_ref, normg_ref, lamv_ref, subln_ref,
                      ya_ref, yb_ref, snew_ref, *, lam_init):
    z = z_ref[...]
    hw = B_HEADS * B_DK
    lv = lamv_ref[...]
    lam = (jnp.exp(jnp.sum(lv[0:1] * lv[1:2], axis=1, keepdims=True))
           - jnp.exp(jnp.sum(lv[2:3] * lv[3:4], axis=1, keepdims=True)) + lam_init)
    oa = oa_ref[...]
    for hg in range(A_HEADS):
        o = oa[2 * hg:2 * hg + 1] - lam * oa[2 * hg + 1:2 * hg + 2]
        y = o * lax.rsqrt(jnp.mean(o * o, axis=-1, keepdims=True) + EPS)
        y = (y * subln_ref[...]) * (1.0 - lam_init)
        gate = z[:, EVEN_OFF[3] + hg * A_DV:EVEN_OFF[3] + (hg + 1) * A_DV]
        ya_ref[:, hg * A_DV:(hg + 1) * A_DV] = (y * _silu(gate)).astype(BF16)
    x_new = z[:, EVEN_OFF[4]:EVEN_OFF[4] + B_QKV]
    conv = convw_ref[CONV_W - 1:CONV_W, :] * x_new
    cp = conv_ref[...]
    for i in range(CONV_W - 1):
        conv = conv + convw_ref[i:i + 1, :] * cp[i:i + 1, :]
    act = _silu(conv)
    ab = z[:, EVEN_OFF[6]:EVEN_OFF[6] + LANES]
    gp = gp_ref[...]
    row = lax.broadcasted_iota(I32, (B_DK, B_DV), 0)
    col = lax.broadcasted_iota(I32, (B_DK, B_DV), 1)
    eye = row == col
    for h in range(B_HEADS):
        q = act[:, h * B_DK:(h + 1) * B_DK]
        k = act[:, hw + h * B_DK:hw + (h + 1) * B_DK]
        v = act[:, 2 * hw + h * B_DV:2 * hw + (h + 1) * B_DV]
        q = q * lax.rsqrt(jnp.sum(q * q, axis=-1, keepdims=True) + EPS) * (B_DK ** -0.5)
        k = k * lax.rsqrt(jnp.sum(k * k, axis=-1, keepdims=True) + EPS)
        g = -jnp.exp(gp[0:1, h:h + 1]) * _softplus(ab[:, h:h + 1] + gp[1:2, h:h + 1])
        beta = jax.nn.sigmoid(ab[:, B_HEADS + h:B_HEADS + h + 1])
        eg = jnp.exp(g)
        s0 = s_ref[h]
        kcol = jnp.sum(jnp.where(eye, jnp.broadcast_to(k, (B_DK, B_DK)), 0.0), axis=1, keepdims=True)
        qcol = jnp.sum(jnp.where(eye, jnp.broadcast_to(q, (B_DK, B_DK)), 0.0), axis=1, keepdims=True)
        ks = jnp.sum(kcol * s0, axis=0, keepdims=True)
        qs = jnp.sum(qcol * s0, axis=0, keepdims=True)
        u = beta * (v - eg * ks)
        qk = jnp.sum(q * k, axis=1, keepdims=True)
        o = eg * qs + qk * u
        snew_ref[h] = eg * s0 + kcol * u
        y = o * lax.rsqrt(jnp.mean(o * o, axis=-1, keepdims=True) + EPS) * normg_ref[...]
        gate = z[:, EVEN_OFF[5] + h * B_DV:EVEN_OFF[5] + (h + 1) * B_DV]
        yb_ref[:, h * B_DV:(h + 1) * B_DV] = (y * _silu(gate)).astype(BF16)


def _even_tail(zf_s, conv_prev, s_prev, oa, conv_w, a_log, dt_bias, norm_g, lamv, subln, lam_init):
    db, npad = zf_s.shape
    gp = jnp.zeros((SUBLANES, LANES), F32).at[0, :B_HEADS].set(a_log).at[1, :B_HEADS].set(dt_bias)
    hw = B_HEADS * B_DV
    return pl.pallas_call(
        functools.partial(_even_tail_kernel, lam_init=lam_init),
        grid=(db,),
        in_specs=[pl.BlockSpec((None, 1, npad), lambda b: (b, 0, 0)),
                  pl.BlockSpec((None, CONV_W - 1, B_QKV), lambda b: (b, 0, 0)),
                  pl.BlockSpec((None, B_HEADS, B_DK, B_DV), lambda b: (b, 0, 0, 0)),
                  pl.BlockSpec((None, DEC_ROWS, LANES), lambda b: (b, 0, 0)),
                  pl.BlockSpec((CONV_W, B_QKV), lambda b: (0, 0)),
                  pl.BlockSpec((SUBLANES, LANES), lambda b: (0, 0)),
                  pl.BlockSpec((1, B_DV), lambda b: (0, 0)),
                  pl.BlockSpec((4, A_DH), lambda b: (0, 0)),
                  pl.BlockSpec((1, A_DV), lambda b: (0, 0))],
        out_specs=[pl.BlockSpec((None, 1, A_HEADS * A_DV), lambda b: (b, 0, 0)),
                   pl.BlockSpec((None, 1, hw), lambda b: (b, 0, 0)),
                   pl.BlockSpec((None, B_HEADS, B_DK, B_DV), lambda b: (b, 0, 0, 0))],
        out_shape=[jax.ShapeDtypeStruct((db, 1, A_HEADS * A_DV), BF16),
                   jax.ShapeDtypeStruct((db, 1, hw), BF16),
                   jax.ShapeDtypeStruct((db, B_HEADS, B_DK, B_DV), F32)],
        compiler_params=_cparams(("parallel",)),
        name="even_tail",
    )(zf_s.reshape(db, 1, npad), conv_prev, s_prev, oa, conv_w, gp, norm_g.reshape(1, B_DV), lamv,
      subln.reshape(1, A_DV))


def _odd_tail_kernel(z_ref, oc_ref, y_ref):
    z = z_ref[...]
    oc = oc_ref[...]
    for hg in range(C_HEADS):
        gate = z[:, ODD_OFF[3] + hg * C_DH:ODD_OFF[3] + (hg + 1) * C_DH]
        y_ref[:, hg * C_DH:(hg + 1) * C_DH] = (oc[hg:hg + 1] * _silu(gate)).astype(BF16)


def _odd_tail(zf_s, oc):
    db, npad = zf_s.shape
    return pl.pallas_call(
        _odd_tail_kernel,
        grid=(db,),
        in_specs=[pl.BlockSpec((None, 1, npad), lambda b: (b, 0, 0)),
                  pl.BlockSpec((None, DEC_ROWS, LANES), lambda b: (b, 0, 0))],
        out_specs=pl.BlockSpec((None, 1, C_HEADS * C_DH), lambda b: (b, 0, 0)),
        out_shape=jax.ShapeDtypeStruct((db, 1, C_HEADS * C_DH), BF16),
        compiler_params=_cparams(("parallel",)),
        name="odd_tail",
    )(zf_s.reshape(db, 1, npad), oc)


def _bias_by_distance(table):
    n = jnp.arange(FAR_DIST + 1)
    exact = N_BUCKETS // 2
    nf = jnp.maximum(n, 1).astype(F32)
    large = exact + (jnp.log(nf / exact) / math.log(MAX_DIST / exact) * (N_BUCKETS - exact)).astype(I32)
    bucket = jnp.where(n < exact, n, jnp.minimum(large, N_BUCKETS - 1))
    return table[bucket].astype(F32)


def _prompt_near_tiles(bd, group):
    assert FAR_DIST <= BLK
    heads = bd.shape[1]
    rel = ((bd - bd[FAR_DIST][None, :]) * LOG2E).T
    f = jnp.concatenate([jnp.full((heads, BLK - 1), NEG, F32), rel,
                         jnp.broadcast_to(rel[:, FAR_DIST:], (heads, 2 * BLK - 1 - FAR_DIST))], axis=1)
    period = 3 * BLK
    g = jnp.pad(f[:, ::-1], ((0, 0), (0, period - f.shape[1])))
    wrapped = jnp.tile(g, (1, BLK + 1))[:, :BLK * (period + 1)].reshape(heads, BLK, period + 1)
    strip = wrapped[:, ::-1, :2 * BLK]
    tiles = jnp.stack([strip[:, :, :BLK], strip[:, :, BLK:]], axis=1)
    n_kv = heads // group
    tiles = tiles.reshape(n_kv, group, 2, BLK, BLK)
    return jnp.transpose(tiles, (0, 2, 1, 3, 4)).reshape(n_kv, 2, group * BLK, BLK)


def _decode_bias(bd, row_heads, past, page, g):
    heads = jnp.asarray(row_heads, I32)
    far = jnp.broadcast_to(bd[FAR_DIST][heads][:, None], (len(row_heads), g * page))
    pos = past - g * page + jnp.arange(g * page)
    dist = jnp.minimum(past - pos, FAR_DIST)
    last = bd[dist][:, heads].T
    new = jnp.broadcast_to(bd[0][heads][:, None], (len(row_heads), LANES))
    return far.astype(F32), last.astype(F32), new.astype(F32)


def _pad_cols(w, mult):
    n = w.shape[1]
    return jnp.pad(w, ((0, 0), (0, _round_up(n, mult) - n)))


def kernel(x_prompt, x_sample, cache_a_k, cache_a_v, state_b_s, state_b_conv, cache_c_k, cache_c_v, cache_c_idx,
           page_table, meta, bias_table, final_norm, norm_e, w_in_e, w_out_e, lam_q1, lam_k1, lam_q2, lam_k2,
           subln_a, conv_b, a_log_b, dt_bias_b, norm_b, norm_o, w_in_o, w_out_o):
    nb, seq, d = x_prompt.shape
    n_meta = meta.shape[0]
    l = seq + n_meta
    lp = _round_up(l, CHUNK)
    nq_real = pl.cdiv(l, BLK)
    db = x_sample.shape[0]
    n_pages = page_table.shape[1]
    page = cache_a_k.shape[2]
    past = n_pages * page
    n_pool = cache_a_k.shape[1]
    assert x_sample.shape[1] == 1 and norm_e.shape[0] == 1 and norm_o.shape[0] == 1
    lam_init = 0.8 - 0.6 * math.exp(-0.3 * 0)

    tn_e, tn_o = 8 * LANES, 5 * LANES
    w_e = _pad_cols(w_in_e[0], tn_e).astype(BF16)
    w_o = _pad_cols(w_in_o[0], tn_o).astype(BF16)
    w_out_a = w_out_e[0][:A_HEADS * A_DV].astype(BF16)
    w_out_b = w_out_e[0][A_HEADS * A_DV:].astype(BF16)
    w_out_c = w_out_o[0].astype(BF16)
    tm = CHUNK
    tm_proj = 2 * CHUNK if (nb * lp) % (2 * CHUNK) == 0 else CHUNK
    dbp = _round_up(db, SUBLANES)

    bd = _bias_by_distance(bias_table)
    tiles_a = _prompt_near_tiles(bd, A_GROUP)
    tiles_c = tiles_a if (A_KV, A_GROUP) == (C_KV, C_GROUP) else _prompt_near_tiles(bd, C_GROUP)
    lamv = jnp.stack([lam_q1[0], lam_k1[0], lam_q2[0], lam_k2[0]]).astype(F32)

    hp = jnp.concatenate([jnp.broadcast_to(meta.astype(F32)[None], (nb, n_meta, d)), x_prompt], axis=1)
    hp = jnp.pad(hp, ((0, 0), (0, lp - l), (0, 0))).reshape(nb * lp, d)
    kv_tile = EVEN_OFF[1] // tn_e
    assert EVEN_OFF[3] <= (kv_tile + 1) * tn_e
    zf, zb = _norm_proj(hp, norm_e[0], w_e, tm_proj, tn_e, (kv_tile, kv_tile + 1))
    ya = _diff_attn_prompt(zf, zb, kv_tile * tn_e, tiles_a, lamv, subln_a[0], nb, lp, nq_real, lam_init)
    yb, pb_s = _gdn_prompt(zf, conv_b[0], a_log_b[0], dt_bias_b[0], norm_b[0], nb, lp, l)
    h1 = _out_proj([ya, yb], [w_out_a, w_out_b], hp, tm)
    z3 = zf.reshape(nb, lp, -1)
    pa_k = z3[:, :l, EVEN_OFF[1]:EVEN_OFF[2]].reshape(1, nb, l, A_KV, 2, A_DH)
    pa_v = z3[:, :l, EVEN_OFF[2]:EVEN_OFF[3]].reshape(1, nb, l, A_KV, A_DV)
    pb_conv = z3[:, l - (CONV_W - 1):l, EVEN_OFF[4]:EVEN_OFF[5]][None]

    zf1, zb1 = _norm_proj(h1, norm_o[0], w_o, tm_proj, tn_o, (0, w_o.shape[1] // tn_o))
    k_sel_p = min(TOPK_MAX, l // 4)
    yc = _sparse_attn_prompt(zf1, zb1, tiles_c, nb, lp, nq_real, k_sel_p)
    yp = _out_proj([yc], [w_out_c], h1, tm, final_gain=final_norm)
    y_prompt = yp.reshape(nb, lp, d)[:, n_meta:l]
    z13 = zf1.reshape(nb, lp, -1)
    pc_k = z13[:, :l, ODD_OFF[1]:ODD_OFF[2]].reshape(1, nb, l, C_KV, C_DH)
    pc_v = z13[:, :l, ODD_OFF[2]:ODD_OFF[3]].reshape(1, nb, l, C_KV, C_DH)
    pc_idx = z13[:, :l, ODD_OFF[5]:ODD_OFF[6]][None]

    hs = jnp.pad(x_sample.reshape(db, d), ((0, dbp - db), (0, 0)))
    zs, _ = _norm_proj(hs, norm_e[0], w_e, dbp, tn_e, (0, 1))
    zs = zs[:db]
    qa = zs[:, :EVEN_OFF[1]].reshape(db, A_KV, A_GROUP, 2, A_DH) * (A_DH ** -0.5)
    qprime = jnp.einsum('bhgcd,hi,cj->bhgcijd', qa, jnp.eye(A_KV, dtype=F32), jnp.eye(2, dtype=F32))
    qprime = qprime.reshape(db, DEC_ROWS, A_KV * 2 * A_DH).astype(BF16)
    k_new = zs[:, EVEN_OFF[1]:EVEN_OFF[2]].reshape(db, 1, -1)
    v_new = zs[:, EVEN_OFF[2]:EVEN_OFF[3]].reshape(db, 1, -1)
    g_dec = _pages_per_step(n_pages)
    rows_a = [r // 2 for r in range(DEC_ROWS)]
    bfar, blast, bnew = _decode_bias(bd, rows_a, past, page, g_dec)
    ak_t = jnp.transpose(cache_a_k[0], (0, 2, 3, 4, 1)).reshape(n_pool, A_KV * 2 * A_DH, page)
    av_r = cache_a_v[0].reshape(n_pool, page * A_KV, A_DV)
    oa = _paged_attention(qprime, k_new, v_new, bfar, blast, bnew, ak_t, av_r, page_table, page,
                          k_feature_major=True)
    ya_s, yb_s, sb_s = _even_tail(zs, state_b_conv[0], state_b_s[0], oa, conv_b[0], a_log_b[0], dt_bias_b[0],
                                  norm_b[0], lamv, subln_a[0], lam_init)
    hs_pad = lambda y: jnp.pad(y.reshape(db, -1), ((0, dbp - db), (0, 0)))
    hs1 = _out_proj([hs_pad(ya_s), hs_pad(yb_s)], [w_out_a, w_out_b], hs, dbp)
    sa_k = zs[:, EVEN_OFF[1]:EVEN_OFF[2]].reshape(1, db, 1, A_KV, 2, A_DH)
    sa_v = zs[:, EVEN_OFF[2]:EVEN_OFF[3]].reshape(1, db, 1, A_KV, A_DV)
    sb_conv = jnp.concatenate([state_b_conv[0][:, 1:], zs[:, None, EVEN_OFF[4]:EVEN_OFF[5]]], axis=1)[None]

    zs1, _ = _norm_proj(hs1, norm_o[0], w_o, dbp, tn_o, (0, 1))
    zs1 = zs1[:db]
    qidx = (zs1[:, ODD_OFF[4]:ODD_OFF[5]].reshape(db, IDX_HEADS, IDX_DIM) * (IDX_DIM ** -0.5)).astype(BF16)
    wcol = jnp.broadcast_to((zs1[:, ODD_OFF[6]:ODD_OFF[7]] * (IDX_HEADS ** -0.5))[:, :, None],
                            (db, IDX_HEADS, LANES))
    ki_new = zs1[:, ODD_OFF[5]:ODD_OFF[6]]
    ki_tile = jnp.zeros((db, IDX_DIM, page), F32).at[:, :, 0].set(ki_new)
    ci_t = jnp.transpose(cache_c_idx[0], (0, 2, 1))
    keys = _paged_index_scores(qidx, wcol, ki_tile, ci_t, page_table)
    k_sel_s = min(TOPK_MAX, (past + 1) // 4)
    tau, jmax = _select_rows(keys.reshape(db, -1), k_sel_s)
    tau, jmax = tau[:, 0], jmax[:, 0]
    key_new = keys[:, 0, past]
    selnew = ((key_new > tau) | ((key_new == tau) & (past <= jmax))).astype(I32)
    qc = zs1[:, :ODD_OFF[1]].reshape(db, C_KV, C_GROUP, C_DH) * (C_DH ** -0.5)
    qcp = jnp.einsum('bhgd,hi->bhgid', qc, jnp.eye(C_KV, dtype=F32)).reshape(db, C_HEADS, C_KV * C_DH)
    qcp = jnp.pad(qcp, ((0, 0), (0, DEC_ROWS - C_HEADS), (0, 0))).astype(BF16)
    kc_new = zs1[:, ODD_OFF[1]:ODD_OFF[2]].reshape(db, 1, -1)
    vc_new = zs1[:, ODD_OFF[2]:ODD_OFF[3]].reshape(db, 1, -1)
    rows_c = [r if r < C_HEADS else 0 for r in range(DEC_ROWS)]
    cfar, clast, cnew = _decode_bias(bd, rows_c, past, page, g_dec)
    ck_r = cache_c_k[0].reshape(n_pool, page * C_KV, C_DH)
    cv_r = cache_c_v[0].reshape(n_pool, page * C_KV, C_DH)
    oc = _paged_attention(qcp, kc_new, vc_new, cfar, clast, cnew, ck_r, cv_r, page_table, page,
                          k_feature_major=False, mask_args=(keys, tau, jmax, selnew))
    yc_s = _odd_tail(zs1, oc)
    ys = _out_proj([hs_pad(yc_s)], [w_out_c], hs1, dbp, final_gain=final_norm)
    y_sample = ys[:db].reshape(db, 1, d)
    sc_k = zs1[:, ODD_OFF[1]:ODD_OFF[2]].reshape(1, db, 1, C_KV, C_DH)
    sc_v = zs1[:, ODD_OFF[2]:ODD_OFF[3]].reshape(1, db, 1, C_KV, C_DH)
    sc_idx = zs1[:, None, ODD_OFF[5]:ODD_OFF[6]][None]

    return (y_prompt, y_sample, pa_k, pa_v, pb_s[None], pb_conv, pc_k, pc_v, pc_idx,
            sa_k, sa_v, sb_s[None], sb_conv, sc_k, sc_v, sc_idx)
```

```python
import functools
import math

import jax
import jax.numpy as jnp
import numpy as np
from jax import lax
from jax.experimental import pallas as pl
from jax.experimental.pallas import tpu as pltpu

F32 = jnp.float32
BF16 = jnp.bfloat16
I32 = jnp.int32
I16 = jnp.int16

EPS = 1e-6
N_BUCKETS = 32
MAX_DIST = 128
FAR_DIST = MAX_DIST

A_HEADS, A_KV, A_GROUP, A_DH, A_DV = 8, 2, 4, 64, 128
B_HEADS, B_DK, B_DV, CONV_W = 4, 128, 128, 4
C_HEADS, C_KV, C_GROUP, C_DH = 8, 2, 4, 128
IDX_HEADS, IDX_DIM, TOPK_MAX = 8, 64, 256
B_QKV = 2 * B_HEADS * B_DK + B_HEADS * B_DV

EVEN_COLS = (A_HEADS * 2 * A_DH, A_KV * 2 * A_DH, A_KV * A_DV, A_HEADS * A_DV, B_QKV, B_HEADS * B_DV, B_HEADS, B_HEADS)
ODD_COLS = (C_HEADS * C_DH, C_KV * C_DH, C_KV * C_DH, C_HEADS * C_DH, IDX_HEADS * IDX_DIM, IDX_DIM, IDX_HEADS)
EVEN_OFF = tuple(int(v) for v in np.cumsum((0,) + EVEN_COLS))
ODD_OFF = tuple(int(v) for v in np.cumsum((0,) + ODD_COLS))

LANES = 128
SUBLANES = 8
BLK = 128
CHUNK = 512
SUBS = CHUNK // BLK
NEG = -1e30
INT_MIN = -2 ** 31
INT_MAX = 2 ** 31 - 1
LOG2E = math.log2(math.e)
VMEM_LIMIT = 56 * 1024 * 1024
HI = lax.Precision.HIGHEST

NT = (((1,), (1,)), ((), ()))


def _round_up(x, m):
    return (x + m - 1) // m * m


def _cparams(sem):
    return pltpu.CompilerParams(dimension_semantics=sem, vmem_limit_bytes=VMEM_LIMIT)


def _silu(x):
    return x * jax.nn.sigmoid(x)


def _sortable_key(score):
    score = jnp.where(score == 0.0, 0.0, score)
    bits = pltpu.bitcast(score, I32)
    return bits ^ ((bits >> 31) & jnp.int32(0x7FFFFFFF))


def _lane_tile(x, n):
    return jnp.concatenate([x] * n, axis=1)


def _proj_kernel(x_ref, g_ref, w_ref, of_ref, ob_ref, xn_ref, *, bf16_tiles):
    j = pl.program_id(1)

    @pl.when(j == 0)
    def _():
        xf = x_ref[...]
        y = xf * lax.rsqrt(jnp.mean(xf * xf, axis=-1, keepdims=True) + EPS)
        xn_ref[...] = (y * g_ref[...]).astype(BF16)

    acc = jnp.dot(xn_ref[...], w_ref[...], preferred_element_type=F32)
    of_ref[...] = acc

    @pl.when((j >= bf16_tiles[0]) & (j < bf16_tiles[1]))
    def _():
        ob_ref[...] = acc.astype(BF16)


def _norm_proj(x, gain, w_bf16, tm, tn, bf16_tiles):
    m, d = x.shape
    n = w_bf16.shape[1]
    t0, t1 = bf16_tiles
    return pl.pallas_call(
        functools.partial(_proj_kernel, bf16_tiles=bf16_tiles),
        grid=(m // tm, n // tn),
        in_specs=[pl.BlockSpec((tm, d), lambda i, j: (i, 0)),
                  pl.BlockSpec((1, d), lambda i, j: (0, 0)),
                  pl.BlockSpec((d, tn), lambda i, j: (0, j))],
        out_specs=[pl.BlockSpec((tm, tn), lambda i, j: (i, j)),
                   pl.BlockSpec((tm, tn), lambda i, j: (i, jnp.clip(j - t0, 0, t1 - t0 - 1)))],
        out_shape=[jax.ShapeDtypeStruct((m, n), F32), jax.ShapeDtypeStruct((m, (t1 - t0) * tn), BF16)],
        scratch_shapes=[pltpu.VMEM((tm, d), BF16)],
        compiler_params=_cparams(("parallel", "arbitrary")),
        name="norm_proj",
    )(x, gain.reshape(1, d), w_bf16)


def _outproj_kernel(*refs, n_lhs, final_norm):
    ys = refs[:n_lhs]
    ws = refs[n_lhs:2 * n_lhs]
    h_ref = refs[2 * n_lhs]
    pos = 2 * n_lhs + 1
    fn_ref = refs[pos] if final_norm else None
    out_ref = refs[-1]
    acc = h_ref[...]
    for y_ref, w_ref in zip(ys, ws):
        acc = acc + jnp.dot(y_ref[...], w_ref[...], preferred_element_type=F32)
    if final_norm:
        y = acc * lax.rsqrt(jnp.mean(acc * acc, axis=-1, keepdims=True) + EPS)
        acc = y * fn_ref[...]
    out_ref[...] = acc


def _out_proj(ys, ws, h, tm, final_gain=None):
    m, n = h.shape
    n_lhs = len(ys)
    in_specs = [pl.BlockSpec((tm, y.shape[1]), lambda i: (i, 0)) for y in ys]
    in_specs += [pl.BlockSpec(w.shape, lambda i: (0, 0)) for w in ws]
    in_specs += [pl.BlockSpec((tm, n), lambda i: (i, 0))]
    args = list(ys) + list(ws) + [h]
    if final_gain is not None:
        in_specs += [pl.BlockSpec((1, n), lambda i: (0, 0))]
        args += [final_gain.reshape(1, n)]
    return pl.pallas_call(
        functools.partial(_outproj_kernel, n_lhs=n_lhs, final_norm=final_gain is not None),
        grid=(m // tm,),
        in_specs=in_specs,
        out_specs=pl.BlockSpec((tm, n), lambda i: (i, 0)),
        out_shape=jax.ShapeDtypeStruct((m, n), F32),
        compiler_params=_cparams(("parallel",)),
        name="out_proj",
    )(*args)


def _flash_steps(ss, v1s, m_refs):
    each = range(len(ss))
    m_old = [m_refs[i][0][m_refs[i][2]] for i in each]
    m_new = [jnp.maximum(m_old[i], jnp.max(ss[i], axis=1, keepdims=True)) for i in each]
    alpha = [jnp.exp2(m_old[i] - m_new[i]) for i in each]
    p = [jnp.exp2(ss[i] - _lane_tile(m_new[i], ss[i].shape[1] // LANES)).astype(BF16) for i in each]
    pv = [jnp.dot(p[i], v1s[i], preferred_element_type=F32) for i in each]
    for i in each:
        m_ref, acc_ref, idx = m_refs[i]
        acc_ref[idx] = _lane_tile(alpha[i], 2) * acc_ref[idx] + pv[i]
        m_ref[idx] = m_new[i]


def _far_chunk_loop(n_far, run_chunks):
    def quad(i, carry):
        run_chunks([4 * i, 4 * i + 1, 4 * i + 2, 4 * i + 3])
        return carry

    n_quad = lax.shift_right_logical(n_far, 2)
    lax.fori_loop(0, n_quad, quad, 0)

    @pl.when(jnp.bitwise_and(n_far, 2) == 2)
    def _():
        run_chunks([4 * n_quad, 4 * n_quad + 1])

    @pl.when(jnp.bitwise_and(n_far, 1) == 1)
    def _():
        run_chunks([n_far - 1])


def _build_near_bias(strip_ref, t_prev, t_diag, r, lead=()):
    for sb in range(2 * SUBS):
        rel = sb - SUBS - r
        tile = jnp.where(rel == 0, t_diag, jnp.where(rel == -1, t_prev, jnp.where(rel < -1, 0.0, NEG)))
        strip_ref[lead + (sb,)] = tile


def _diff_attn_kernel(q_ref, k_ref, v_ref, gate_ref, tiles_ref, lamv_ref, subln_ref, out_ref,
                      qm_ref, strip_ref, m_ref, acc_ref, *, lam_init, nq_real):
    qi = pl.program_id(2)

    @pl.when(qi >= nq_real)
    def _():
        out_ref[...] = jnp.zeros(out_ref.shape, out_ref.dtype)

    @pl.when(qi < nq_real)
    def _():
        lane = lax.broadcasted_iota(I32, (BLK, 2 * A_DH), 1)
        for g in range(A_GROUP):
            qg = q_ref[:, g * 2 * A_DH:(g + 1) * 2 * A_DH] * (A_DH ** -0.5 * LOG2E)
            for c in range(2):
                keep = (lane < A_DH) if c == 0 else (lane >= A_DH)
                qm_ref[c, g * BLK:(g + 1) * BLK, :] = jnp.where(keep, qg, 0.0).astype(BF16)
        m_ref[...] = jnp.full(m_ref.shape, NEG, F32)
        acc_ref[...] = jnp.zeros(acc_ref.shape, F32)
        r = jnp.bitwise_and(qi, SUBS - 1)
        jd = lax.shift_right_logical(qi, int(math.log2(SUBS)))
        _build_near_bias(strip_ref, tiles_ref[0], tiles_ref[1], r)
        ones = jnp.ones((CHUNK, A_DV), BF16)

        def run_chunks(js, first_sbs=None):
            starts = [pl.multiple_of(j * CHUNK, CHUNK) for j in js]
            k = jnp.concatenate([k_ref[pl.ds(st, CHUNK), :] for st in starts], axis=0)
            v1 = jnp.concatenate(
                [jnp.concatenate([v_ref[pl.ds(st, CHUNK), :], ones], axis=1) for st in starts], axis=0)
            ss = [lax.dot_general(qm_ref[c], k, NT, preferred_element_type=F32) for c in range(2)]
            if first_sbs is not None:
                bias = jnp.concatenate([strip_ref[sb + i] for sb in first_sbs for i in range(SUBS)], axis=1)
                ss = [s + bias for s in ss]
            _flash_steps(ss, [v1, v1], [(m_ref, acc_ref, c) for c in range(2)])

        _far_chunk_loop(jnp.maximum(jd - 1, 0), run_chunks)

        @pl.when(jd >= 1)
        def _():
            run_chunks([jd - 1, jd], [0, SUBS])

        @pl.when(jd == 0)
        def _():
            run_chunks([jd], [SUBS])

        lv = lamv_ref[...]
        lam = (jnp.exp(jnp.sum(lv[0:1] * lv[1:2], axis=1, keepdims=True))
               - jnp.exp(jnp.sum(lv[2:3] * lv[3:4], axis=1, keepdims=True)) + lam_init)
        a0 = acc_ref[0]
        a1 = acc_ref[1]
        o = a0[:, :A_DV] / a0[:, A_DV:] - lam * (a1[:, :A_DV] / a1[:, A_DV:])
        y = o * lax.rsqrt(jnp.mean(o * o, axis=-1, keepdims=True) + EPS)
        y = (y * subln_ref[...]) * (1.0 - lam_init)
        for g in range(A_GROUP):
            gate = gate_ref[:, g * A_DV:(g + 1) * A_DV]
            out_ref[:, g * A_DV:(g + 1) * A_DV] = (y[g * BLK:(g + 1) * BLK] * _silu(gate)).astype(BF16)


def _diff_attn_prompt(zf, zb, zb_col0, near_tiles, lamv, subln, nb, lp, nq_real, lam_init):
    nq = lp // BLK
    qw = A_GROUP * 2 * A_DH
    k_blk0 = (EVEN_OFF[1] - zb_col0) // (2 * A_DH)
    v_blk0 = (EVEN_OFF[2] - zb_col0) // A_DV
    g_blk0 = EVEN_OFF[3] // (A_GROUP * A_DV)
    rows = A_GROUP * BLK
    return pl.pallas_call(
        functools.partial(_diff_attn_kernel, lam_init=lam_init, nq_real=nq_real),
        grid=(nb, A_KV, nq),
        in_specs=[pl.BlockSpec((BLK, qw), lambda b, h, i: (b * nq + i, h)),
                  pl.BlockSpec((lp, 2 * A_DH), lambda b, h, i: (b, k_blk0 + h)),
                  pl.BlockSpec((lp, A_DV), lambda b, h, i: (b, v_blk0 + h)),
                  pl.BlockSpec((BLK, A_GROUP * A_DV), lambda b, h, i: (b * nq + i, g_blk0 + h)),
                  pl.BlockSpec((None, 2, rows, BLK), lambda b, h, i: (h, 0, 0, 0)),
                  pl.BlockSpec((4, A_DH), lambda b, h, i: (0, 0)),
                  pl.BlockSpec((1, A_DV), lambda b, h, i: (0, 0))],
        out_specs=pl.BlockSpec((BLK, A_GROUP * A_DV), lambda b, h, i: (b * nq + i, h)),
        out_shape=jax.ShapeDtypeStruct((nb * lp, A_HEADS * A_DV), BF16),
        scratch_shapes=[pltpu.VMEM((2, rows, 2 * A_DH), BF16),
                        pltpu.VMEM((2 * SUBS, rows, BLK), F32),
                        pltpu.VMEM((2, rows, LANES), F32),
                        pltpu.VMEM((2, rows, 2 * A_DV), F32)],
        compiler_params=_cparams(("parallel", "parallel", "arbitrary")),
        name="diff_attn_prompt",
    )(zf, zb, zb, zf, near_tiles, lamv, subln.reshape(1, A_DV))


def _softplus(x):
    return jnp.maximum(x, 0.0) + jnp.log1p(jnp.exp(-jnp.abs(x)))


def _split_bf16(a):
    hi = a.astype(BF16)
    return hi, (a - hi.astype(F32)).astype(BF16)


def _dot3(a, b):
    (ah, al), (bh, bl) = a, b
    return (jnp.dot(ah, bh, preferred_element_type=F32)
            + (jnp.dot(ah, bl, preferred_element_type=F32) + jnp.dot(al, bh, preferred_element_type=F32)))


def _gdn_chunk_math(s0, q, k, v, gb, betab, row_ge, row_gt):
    n = len(q)
    c = q[0].shape[0]
    each = range(n)
    dot = functools.partial(jnp.dot, preferred_element_type=F32)
    dot_nt = functools.partial(lax.dot_general, dimension_numbers=NT, preferred_element_type=F32)
    ltri = jnp.where(row_ge, 1.0, 0.0).astype(F32)
    gcum = [jnp.dot(ltri, gb[i], precision=HI, preferred_element_type=F32) for i in each]
    decay = [jnp.where(row_ge, jnp.exp(jnp.where(row_ge, gcum[i] - gcum[i].T, 0.0)), 0.0) for i in each]
    kb = [k[i].astype(BF16) for i in each]
    qb = [q[i].astype(BF16) for i in each]
    s0b = [s0[i].astype(BF16) for i in each]
    kk = [dot_nt(kb[i], kb[i]) for i in each]
    x = [-jnp.where(row_gt, betab[i] * decay[i] * kk[i], 0.0) for i in each]
    tm = list(x)
    p = list(x)
    for _ in range(int(math.log2(c)) - 1):
        ps = [_split_bf16(p[i]) for i in each]
        p = [_dot3(ps[i], ps[i]) for i in each]
        ps = [_split_bf16(p[i]) for i in each]
        tms = [_split_bf16(tm[i]) for i in each]
        tm = [tm[i] + p[i] + _dot3(tms[i], ps[i]) for i in each]
    eg = [jnp.exp(gcum[i]) for i in each]
    ks = [dot(kb[i], s0b[i]) for i in each]
    rhs = [betab[i] * (v[i] - eg[i] * ks[i]) for i in each]
    u = [rhs[i] + dot(tm[i].astype(BF16), rhs[i].astype(BF16)) for i in each]
    ub = [u[i].astype(BF16) for i in each]
    qk = [dot_nt(qb[i], kb[i]) * decay[i] for i in each]
    o = [eg[i] * dot(qb[i], s0b[i]) + dot(qk[i].astype(BF16), ub[i]) for i in each]
    glast = [gcum[i][c - 1:c, :] for i in each]
    kd = [k[i] * jnp.exp(glast[i] - gcum[i]) for i in each]
    s_new = [jnp.exp(glast[i]) * s0[i] + dot(kd[i].T.astype(BF16), ub[i]) for i in each]
    return s_new, o


def _gdn_prompt_kernel(xq_ref, xk_ref, xv_ref, gate_ref, ab_ref, convw_ref, gp_ref, normg_ref,
                       y_ref, sfin_ref, xbuf_ref, s_ref, *, seq_len):
    ci = pl.program_id(0)
    hw = B_HEADS * B_DK
    nb = xq_ref.shape[0]

    @pl.when(ci == 0)
    def _():
        xbuf_ref[:, 0:SUBLANES, :] = jnp.zeros((nb, SUBLANES, B_QKV), F32)
        s_ref[...] = jnp.zeros(s_ref.shape, F32)

    row = lax.broadcasted_iota(I32, (BLK, BLK), 0)
    col = lax.broadcasted_iota(I32, (BLK, BLK), 1)
    row_ge = row >= col
    row_gt = row > col
    valid = (ci * BLK + row) < seq_len
    gp = gp_ref[...]
    chains = [(b, h) for b in range(nb) for h in range(B_HEADS)]
    qs, ks, vs, gbs, betabs = [], [], [], [], []
    for b in range(nb):
        xbuf_ref[b, SUBLANES:SUBLANES + BLK, 0:hw] = xq_ref[b]
        xbuf_ref[b, SUBLANES:SUBLANES + BLK, hw:2 * hw] = xk_ref[b]
        xbuf_ref[b, SUBLANES:SUBLANES + BLK, 2 * hw:3 * hw] = xv_ref[b]
        conv = jnp.zeros((BLK, B_QKV), F32)
        for i in range(CONV_W):
            conv = conv + convw_ref[i:i + 1, :] * xbuf_ref[b, pl.ds(SUBLANES - (CONV_W - 1) + i, BLK), :]
        tail = xbuf_ref[b, BLK:BLK + SUBLANES, :]
        xbuf_ref[b, 0:SUBLANES, :] = tail
        act = _silu(conv)
        ab = ab_ref[b]
        for h in range(B_HEADS):
            q = act[:, h * B_DK:(h + 1) * B_DK]
            k = act[:, hw + h * B_DK:hw + (h + 1) * B_DK]
            v = act[:, 2 * hw + h * B_DV:2 * hw + (h + 1) * B_DV]
            q = q * lax.rsqrt(jnp.sum(q * q, axis=-1, keepdims=True) + EPS) * (B_DK ** -0.5)
            k = k * lax.rsqrt(jnp.sum(k * k, axis=-1, keepdims=True) + EPS)
            a_raw = jnp.broadcast_to(ab[:, h:h + 1], (BLK, BLK))
            b_raw = jnp.broadcast_to(ab[:, B_HEADS + h:B_HEADS + h + 1], (BLK, BLK))
            a_log = gp[0:1, h:h + 1]
            dt_b = gp[1:2, h:h + 1]
            qs.append(q)
            ks.append(k)
            vs.append(v)
            gbs.append(jnp.where(valid, -jnp.exp(a_log) * _softplus(a_raw + dt_b), 0.0))
            betabs.append(jnp.where(valid, jax.nn.sigmoid(b_raw), 0.0))
    s_new, outs = _gdn_chunk_math([s_ref[b, h] for b, h in chains], qs, ks, vs, gbs, betabs, row_ge, row_gt)
    for (b, h), s_bh, o in zip(chains, s_new, outs):
        s_ref[b, h] = s_bh
        y = o * lax.rsqrt(jnp.mean(o * o, axis=-1, keepdims=True) + EPS) * normg_ref[...]
        gate = gate_ref[b, :, h * B_DV:(h + 1) * B_DV]
        y_ref[b, :, h * B_DV:(h + 1) * B_DV] = (y * _silu(gate)).astype(BF16)

    @pl.when(ci == pl.num_programs(0) - 1)
    def _():
        sfin_ref[...] = s_ref[...]


def _gdn_prompt(zf, conv_w, a_log, dt_bias, norm_g, nb, lp, seq_len):
    nc = lp // BLK
    hw = B_HEADS * B_DK
    c0 = EVEN_OFF[4] // hw
    g0 = EVEN_OFF[5] // hw
    ab0 = EVEN_OFF[6] // LANES
    gp = jnp.zeros((SUBLANES, LANES), F32).at[0, :B_HEADS].set(a_log).at[1, :B_HEADS].set(dt_bias)
    z3 = zf.reshape(nb, lp, zf.shape[1])
    y, s_fin = pl.pallas_call(
        functools.partial(_gdn_prompt_kernel, seq_len=seq_len),
        grid=(nc,),
        in_specs=[pl.BlockSpec((nb, BLK, hw), lambda i: (0, i, c0)),
                  pl.BlockSpec((nb, BLK, hw), lambda i: (0, i, c0 + 1)),
                  pl.BlockSpec((nb, BLK, hw), lambda i: (0, i, c0 + 2)),
                  pl.BlockSpec((nb, BLK, hw), lambda i: (0, i, g0)),
                  pl.BlockSpec((nb, BLK, LANES), lambda i: (0, i, ab0)),
                  pl.BlockSpec((CONV_W, B_QKV), lambda i: (0, 0)),
                  pl.BlockSpec((SUBLANES, LANES), lambda i: (0, 0)),
                  pl.BlockSpec((1, B_DV), lambda i: (0, 0))],
        out_specs=[pl.BlockSpec((nb, BLK, hw), lambda i: (0, i, 0)),
                   pl.BlockSpec((nb, B_HEADS, B_DK, B_DV), lambda i: (0, 0, 0, 0))],
        out_shape=[jax.ShapeDtypeStruct((nb, lp, hw), BF16),
                   jax.ShapeDtypeStruct((nb, B_HEADS, B_DK, B_DV), F32)],
        scratch_shapes=[pltpu.VMEM((nb, BLK + SUBLANES, B_QKV), F32),
                        pltpu.VMEM((nb, B_HEADS, B_DK, B_DV), F32)],
        compiler_params=_cparams(("arbitrary",)),
        name="gdn_prompt",
    )(z3, z3, z3, z3, z3, conv_w, gp, norm_g.reshape(1, B_DV))
    return y.reshape(nb * lp, hw), s_fin


def _kth_largest(count_fn, k_sel, shape):
    def step(i, tau):
        cand = tau + lax.shift_left(jnp.int32(1), 31 - i)
        cnt = count_fn(lambda key, idx: key >= cand)
        return jnp.where(cnt >= k_sel, cand, tau)

    return lax.fori_loop(0, 32, step, jnp.full(shape, INT_MIN, I32))


def _tie_cutoff(count_fn, tau, need, n_index_bits, shape):
    def step(i, jm):
        cand = jm | lax.shift_left(jnp.int32(1), n_index_bits - 1 - i)
        cnt = count_fn(lambda key, idx: (key == tau) & (idx < cand))
        return jnp.where(cnt < need, cand, jm)

    return lax.fori_loop(0, n_index_bits, step, jnp.zeros(shape, I32))


def _sparse_attn_kernel(qi_ref, kw_ref, kidx_ref, qc_ref, kc_ref, vc_ref, g0_ref, g1_ref, tiles_ref, out_ref,
                        keys_ref, hi_ref, lo_ref, qm_ref, qh_ref, strip_ref, tau_ref, jmax_ref, m_ref, acc_ref,
                        *, k_sel, n_index_bits, nq_real):
    qb = pl.program_id(1)

    @pl.when(qb >= nq_real)
    def _():
        out_ref[...] = jnp.zeros(out_ref.shape, out_ref.dtype)

    @pl.when(qb < nq_real)
    def _():
        lane = lax.broadcasted_iota(I32, (BLK, LANES), 1)
        key_pos = lax.broadcasted_iota(I32, (CHUNK, BLK), 0)
        qry_pos = lax.broadcasted_iota(I32, (CHUNK, BLK), 1)
        r = jnp.bitwise_and(qb, SUBS - 1)
        jd = lax.shift_right_logical(qb, int(math.log2(SUBS)))

        for p in range(IDX_HEADS // 2):
            pair = qi_ref[:, p * LANES:(p + 1) * LANES] * (IDX_DIM ** -0.5)
            swapped = pltpu.roll(pair, IDX_DIM, 1)
            for e, src in enumerate((pair, swapped)):
                h = 2 * p + e
                qm_ref[h * BLK:(h + 1) * BLK, :] = jnp.where(lane < IDX_DIM, src, 0.0).astype(BF16)
        w_rows = kw_ref[...].T * (IDX_HEADS ** -0.5)

        def scores_t(j):
            start = pl.multiple_of(j * CHUNK, CHUNK)
            d = lax.dot_general(kidx_ref[pl.ds(start, CHUNK), :], qm_ref[...], NT, preferred_element_type=F32)
            sc = jnp.zeros((CHUNK, BLK), F32)
            for h in range(IDX_HEADS):
                sc = sc + w_rows[IDX_DIM + h:IDX_DIM + h + 1, :] * jnp.maximum(d[:, h * BLK:(h + 1) * BLK], 0.0)
            return sc

        half_bits = 16
        half_min = -(1 << (half_bits - 1))

        def store_keys(j, sc):
            key = _sortable_key(sc)
            keys_ref[j] = key
            hi_ref[j] = lax.shift_right_arithmetic(key, half_bits).astype(I16)
            lo_ref[j] = (jnp.bitwise_and(key, (1 << half_bits) - 1) + half_min).astype(I16)

        def score_body(j, carry):
            store_keys(j, scores_t(j))
            return carry

        lax.fori_loop(0, jd, score_body, 0)
        admissible = (jd * CHUNK + key_pos) <= (qb * BLK + qry_pos)
        store_keys(jd, jnp.where(admissible, scores_t(jd), -jnp.inf))

        part = SUBLANES * SUBLANES

        def count_fn(pred):
            def body(j, acc):
                hit = jnp.where(pred(keys_ref[j], j * CHUNK + key_pos), 1.0, 0.0)
                return acc + jnp.sum(hit.reshape(CHUNK // part, part, BLK), axis=0)
            acc = lax.fori_loop(0, jd + 1, body, jnp.zeros((part, BLK), F32))
            return jnp.sum(acc, axis=0, keepdims=True).astype(I32)

        def kth_half(ref16):
            one16 = jnp.ones((), I16)
            zero16 = jnp.zeros((), I16)

            def count_ge(cand16):
                def body(j, acc):
                    hit = jnp.where(ref16[j] >= cand16, one16, zero16)
                    for i in range(CHUNK // part):
                        acc = acc + hit[i * part:(i + 1) * part]
                    return acc
                acc = lax.fori_loop(0, jd + 1, body, jnp.zeros((part, BLK), I16))
                return jnp.sum(acc.astype(I32), axis=0, keepdims=True)

            def step(i, t):
                cand = t + lax.shift_left(jnp.int32(1), half_bits - 1 - i)
                return jnp.where(count_ge(cand.astype(I16)) >= k_sel, cand, t)

            return lax.fori_loop(0, half_bits, step, jnp.full((1, BLK), half_min, I32))

        tau_hi = kth_half(hi_ref)
        tau_hi16 = tau_hi.astype(I16)

        def restrict_low(j, carry):
            hi = hi_ref[j]
            other = jnp.where(hi > tau_hi16, jnp.full((), -half_min - 1, I16), jnp.full((), half_min, I16))
            lo_ref[j] = jnp.where(hi == tau_hi16, lo_ref[j], other)
            return carry

        lax.fori_loop(0, jd + 1, restrict_low, 0)
        tau = tau_hi * (1 << half_bits) + (kth_half(lo_ref) - half_min)
        tau_ref[...] = tau
        jmax_ref[...] = jnp.full((1, BLK), INT_MAX, I32)
        need = k_sel - count_fn(lambda key, idx: key > tau)
        n_eq = count_fn(lambda key, idx: key == tau)
        excess = jnp.max(jnp.where(n_eq > need, 1.0, 0.0)) > 0.5

        @pl.when(excess)
        def _():
            jmax_ref[...] = _tie_cutoff(count_fn, tau, need, n_index_bits, (1, BLK))

        jmax = jmax_ref[...]

        for h in range(C_KV):
            for g in range(C_GROUP):
                col = (h * C_GROUP + g) * C_DH
                qh_ref[h, g * BLK:(g + 1) * BLK, :] = (
                    qc_ref[:, col:col + C_DH] * (C_DH ** -0.5 * LOG2E)).astype(BF16)
            _build_near_bias(strip_ref, tiles_ref[h, 0], tiles_ref[h, 1], r, lead=(h,))
        m_ref[...] = jnp.full(m_ref.shape, NEG, F32)
        acc_ref[...] = jnp.zeros(acc_ref.shape, F32)
        ones = jnp.ones((CHUNK, C_DH), BF16)

        def run_chunks(js, first_sbs=None):
            starts = [pl.multiple_of(j * CHUNK, CHUNK) for j in js]
            masks = []
            for j in js:
                key = keys_ref[j]
                tie_ok = jnp.where((j * CHUNK + key_pos) <= jmax, 0.0, NEG)
                sel_t = jnp.where(key > tau, 0.0, jnp.where(key == tau, tie_ok, NEG))
                masks += [sel_t[i * BLK:(i + 1) * BLK, :].T for i in range(SUBS)]
            selneg = jnp.concatenate(masks, axis=1)
            selneg = jnp.concatenate([selneg] * C_GROUP, axis=0)
            ss, v1s = [], []
            for h in range(C_KV):
                cols = slice(h * C_DH, (h + 1) * C_DH)
                k = jnp.concatenate([kc_ref[pl.ds(st, CHUNK), cols] for st in starts], axis=0)
                v1s.append(jnp.concatenate(
                    [jnp.concatenate([vc_ref[pl.ds(st, CHUNK), cols], ones], axis=1) for st in starts], axis=0))
                s = lax.dot_general(qh_ref[h], k, NT, preferred_element_type=F32) + selneg
                if first_sbs is not None:
                    s = s + jnp.concatenate(
                        [strip_ref[h, sb + i] for sb in first_sbs for i in range(SUBS)], axis=1)
                ss.append(s)
            _flash_steps(ss, v1s, [(m_ref, acc_ref, h) for h in range(C_KV)])

        _far_chunk_loop(jnp.maximum(jd - 1, 0), run_chunks)

        @pl.when(jd >= 1)
        def _():
            run_chunks([jd - 1, jd], [0, SUBS])

        @pl.when(jd == 0)
        def _():
            run_chunks([jd], [SUBS])

        half = (C_HEADS // 2) * C_DH
        for h in range(C_KV):
            a = acc_ref[h]
            o = a[:, :C_DH] / a[:, C_DH:]
            gref = g0_ref if h == 0 else g1_ref
            for g in range(C_GROUP):
                gate = gref[:, g * C_DH:(g + 1) * C_DH]
                out_ref[:, h * half + g * C_DH:h * half + (g + 1) * C_DH] = (
                    o[g * BLK:(g + 1) * BLK] * _silu(gate)).astype(BF16)


def _sparse_attn_prompt(zf, zb, near_tiles, nb, lp, nq_real, k_sel):
    nq = lp // BLK
    half = (C_HEADS // 2) * C_DH
    qi0 = ODD_OFF[4] // (IDX_HEADS * IDX_DIM)
    ki0 = ODD_OFF[5] // LANES
    kc0 = ODD_OFF[1] // (C_KV * C_DH)
    vc0 = ODD_OFF[2] // (C_KV * C_DH)
    g0 = ODD_OFF[3] // half
    rows = C_GROUP * BLK
    n_index_bits = max(1, int(math.ceil(math.log2(lp))))
    return pl.pallas_call(
        functools.partial(_sparse_attn_kernel, k_sel=k_sel, n_index_bits=n_index_bits, nq_real=nq_real),
        grid=(nb, nq),
        in_specs=[pl.BlockSpec((BLK, IDX_HEADS * IDX_DIM), lambda b, i: (b * nq + i, qi0)),
                  pl.BlockSpec((BLK, LANES), lambda b, i: (b * nq + i, ki0)),
                  pl.BlockSpec((lp, LANES), lambda b, i: (b, ki0)),
                  pl.BlockSpec((BLK, C_HEADS * C_DH), lambda b, i: (b * nq + i, 0)),
                  pl.BlockSpec((lp, C_KV * C_DH), lambda b, i: (b, kc0)),
                  pl.BlockSpec((lp, C_KV * C_DH), lambda b, i: (b, vc0)),
                  pl.BlockSpec((BLK, half), lambda b, i: (b * nq + i, g0)),
                  pl.BlockSpec((BLK, half), lambda b, i: (b * nq + i, g0 + 1)),
                  pl.BlockSpec((C_KV, 2, rows, BLK), lambda b, i: (0, 0, 0, 0))],
        out_specs=pl.BlockSpec((BLK, C_HEADS * C_DH), lambda b, i: (b * nq + i, 0)),
        out_shape=jax.ShapeDtypeStruct((nb * lp, C_HEADS * C_DH), BF16),
        scratch_shapes=[pltpu.VMEM((lp // CHUNK, CHUNK, BLK), I32),
                        pltpu.VMEM((lp // CHUNK, CHUNK, BLK), I16),
                        pltpu.VMEM((lp // CHUNK, CHUNK, BLK), I16),
                        pltpu.VMEM((IDX_HEADS * BLK, LANES), BF16),
                        pltpu.VMEM((C_KV, rows, C_DH), BF16),
                        pltpu.VMEM((C_KV, 2 * SUBS, rows, BLK), F32),
                        pltpu.VMEM((1, BLK), I32),
                        pltpu.VMEM((1, BLK), I32),
                        pltpu.VMEM((C_KV, rows, LANES), F32),
                        pltpu.VMEM((C_KV, rows, 2 * C_DH), F32)],
        compiler_params=_cparams(("parallel", "arbitrary")),
        name="sparse_attn_prompt",
    )(zf, zf, zb, zf, zb, zb, zf, zf, near_tiles)


PAGES_PER_STEP = 32
INDEX_PAGES_PER_STEP = 64
DEC_ROWS = 16


def _interleaved_pages(refs, page):
    halves = [jnp.concatenate([r[pl.ds(h, page, stride=2), :] for r in refs], axis=0) for h in range(2)]
    return jnp.concatenate(halves, axis=1).astype(BF16)


def _paged_attn_kernel(*refs, n_pages_step, masked, page, k_feature_major):
    if masked:
        pt_ref, tau_ref, jmax_ref, selnew_ref = refs[:4]
        refs = refs[4:]
    else:
        pt_ref = refs[0]
        refs = refs[1:]
    q_ref, knew_ref, vnew_ref, bfar_ref, blast_ref, b0_ref = refs[:6]
    refs = refs[6:]
    if masked:
        keys_ref = refs[0]
        refs = refs[1:]
    k_refs = refs[:n_pages_step]
    v_refs = refs[n_pages_step:2 * n_pages_step]
    out_ref, m_ref, l_ref, acc_ref = refs[2 * n_pages_step:]
    del pt_ref
    b = pl.program_id(0)
    j = pl.program_id(1)
    width = n_pages_step * page
    q = q_ref[...]

    @pl.when(j == 0)
    def _():
        s_new = jnp.sum(q.astype(F32) * knew_ref[...].astype(BF16).astype(F32), axis=1, keepdims=True) + b0_ref[...]
        v_new = jnp.broadcast_to(vnew_ref[...].astype(BF16).astype(F32), acc_ref.shape)
        if masked:
            take = selnew_ref[b] > 0
            m_ref[...] = jnp.where(take, s_new, NEG)
            l_ref[...] = jnp.where(take, 1.0, 0.0) * jnp.ones(l_ref.shape, F32)
            acc_ref[...] = jnp.where(take, v_new, 0.0)
        else:
            m_ref[...] = s_new
            l_ref[...] = jnp.ones(l_ref.shape, F32)
            acc_ref[...] = v_new

    if k_feature_major:
        kcat = jnp.concatenate([r[...] for r in k_refs], axis=1).astype(BF16)
        s = jnp.dot(q, kcat, preferred_element_type=F32)
    else:
        s = lax.dot_general(q, _interleaved_pages(k_refs, page), NT, preferred_element_type=F32)
    vcat = _interleaved_pages(v_refs, page)
    s = s + jnp.where(j == pl.num_programs(1) - 1, blast_ref[...], bfar_ref[...])
    if masked:
        key = keys_ref[...]
        idx = j * width + lax.broadcasted_iota(I32, (1, width), 1)
        tau = tau_ref[b]
        sel = (key > tau) | ((key == tau) & (idx <= jmax_ref[b]))
        s = jnp.where(sel, s, NEG)
    m_old = m_ref[...]
    m_new = jnp.maximum(m_old, jnp.max(s, axis=1, keepdims=True))
    alpha = jnp.exp(m_old - m_new)
    p = jnp.exp(s - m_new[:, 0:1])
    l_ref[...] = alpha * l_ref[...] + jnp.sum(p, axis=1, keepdims=True)
    acc_ref[...] = _lane_tile(alpha, 2) * acc_ref[...] + jnp.dot(p.astype(BF16), vcat, preferred_element_type=F32)
    m_ref[...] = m_new

    @pl.when(j == pl.num_programs(1) - 1)
    def _():
        o = acc_ref[...]
        rowi = lax.broadcasted_iota(I32, (DEC_ROWS, LANES), 0)
        upper = (rowi >= DEC_ROWS // 2) if not masked else ((rowi >= DEC_ROWS // 4) & (rowi < DEC_ROWS // 2))
        out_ref[...] = jnp.where(upper, o[:, LANES:], o[:, :LANES]) / l_ref[...]


def _pages_per_step(n_pages, g=PAGES_PER_STEP):
    while n_pages % g:
        g //= 2
    return g


def _paged_attention(qprime, k_new, v_new, bias_far, bias_last, bias0, k_cache, v_cache, page_table, page,
                     k_feature_major, mask_args=None):
    db = qprime.shape[0]
    n_pages = page_table.shape[1]
    width = qprime.shape[2]
    g = _pages_per_step(n_pages)
    n_steps = n_pages // g
    masked = mask_args is not None
    n_pref = 4 if masked else 1

    def page_map(gi):
        return lambda b, j, pt, *_: (pt[b * n_pages + j * g + gi], 0, 0)

    in_specs = [pl.BlockSpec((None, DEC_ROWS, width), lambda b, j, *_: (b, 0, 0)),
                pl.BlockSpec((None, 1, width), lambda b, j, *_: (b, 0, 0)),
                pl.BlockSpec((None, 1, width), lambda b, j, *_: (b, 0, 0)),
                pl.BlockSpec((DEC_ROWS, g * page), lambda b, j, *_: (0, 0)),
                pl.BlockSpec((DEC_ROWS, g * page), lambda b, j, *_: (0, 0)),
                pl.BlockSpec((DEC_ROWS, LANES), lambda b, j, *_: (0, 0))]
    args = [qprime, k_new, v_new, bias_far, bias_last, bias0]
    prefetch = [page_table.reshape(-1)]
    if masked:
        keys, tau, jmax, selnew = mask_args
        prefetch += [tau, jmax, selnew]
        in_specs += [pl.BlockSpec((None, 1, g * page), lambda b, j, *_: (b, 0, j))]
        args += [keys]
    in_specs += [pl.BlockSpec((None,) + k_cache.shape[1:], page_map(gi)) for gi in range(g)]
    in_specs += [pl.BlockSpec((None,) + v_cache.shape[1:], page_map(gi)) for gi in range(g)]
    args += [k_cache] * g + [v_cache] * g
    grid_spec = pltpu.PrefetchScalarGridSpec(
        num_scalar_prefetch=n_pref,
        grid=(db, n_steps),
        in_specs=in_specs,
        out_specs=pl.BlockSpec((None, DEC_ROWS, LANES), lambda b, j, *_: (b, 0, 0)),
        scratch_shapes=[pltpu.VMEM((DEC_ROWS, LANES), F32),
                        pltpu.VMEM((DEC_ROWS, LANES), F32),
                        pltpu.VMEM((DEC_ROWS, width), F32)])
    return pl.pallas_call(
        functools.partial(_paged_attn_kernel, n_pages_step=g, masked=masked, page=page,
                          k_feature_major=k_feature_major),
        grid_spec=grid_spec,
        out_shape=jax.ShapeDtypeStruct((db, DEC_ROWS, LANES), F32),
        compiler_params=_cparams(("parallel", "arbitrary")),
        name="paged_attn_masked" if masked else "paged_attn",
    )(*prefetch, *args)


def _paged_index_kernel(pt_ref, q_ref, w_ref, knew_ref, *refs, n_pages_step, page):
    k_refs = refs[:n_pages_step]
    out_ref = refs[n_pages_step]
    del pt_ref
    j = pl.program_id(1)
    last = pl.num_programs(1) - 1
    q = q_ref[...]
    w = w_ref[...]

    def score(kmat_t):
        d = jnp.dot(q, kmat_t.astype(BF16), preferred_element_type=F32)
        return jnp.sum(jnp.maximum(d, 0.0) * w[:, 0:1], axis=0, keepdims=True)

    @pl.when(j < last)
    def _():
        kcat = jnp.concatenate([r[...] for r in k_refs], axis=1)
        out_ref[...] = _sortable_key(score(kcat))

    @pl.when(j == last)
    def _():
        sc = score(knew_ref[...])
        lane = lax.broadcasted_iota(I32, (1, page), 1)
        sc = jnp.where(lane == 0, sc, -jnp.inf)
        pad = jnp.full((1, (n_pages_step - 1) * page), -jnp.inf, F32)
        full = jnp.concatenate([sc, pad], axis=1) if n_pages_step > 1 else sc
        out_ref[...] = _sortable_key(full)


def _paged_index_scores(qidx, wcol, k_new_tile, idx_cache_t, page_table):
    db = qidx.shape[0]
    n_pages = page_table.shape[1]
    page = idx_cache_t.shape[2]
    g = _pages_per_step(n_pages, INDEX_PAGES_PER_STEP)
    n_steps = n_pages // g

    def page_map(gi):
        return lambda b, j, pt: (pt[b * n_pages + jnp.minimum(j, n_steps - 1) * g + gi], 0, 0)

    grid_spec = pltpu.PrefetchScalarGridSpec(
        num_scalar_prefetch=1,
        grid=(db, n_steps + 1),
        in_specs=[pl.BlockSpec((None, IDX_HEADS, IDX_DIM), lambda b, j, pt: (b, 0, 0)),
                  pl.BlockSpec((None, IDX_HEADS, LANES), lambda b, j, pt: (b, 0, 0)),
                  pl.BlockSpec((None, IDX_DIM, page), lambda b, j, pt: (b, 0, 0))]
        + [pl.BlockSpec((None, IDX_DIM, page), page_map(gi)) for gi in range(g)],
        out_specs=pl.BlockSpec((None, 1, g * page), lambda b, j, pt: (b, 0, j)))
    return pl.pallas_call(
        functools.partial(_paged_index_kernel, n_pages_step=g, page=page),
        grid_spec=grid_spec,
        out_shape=jax.ShapeDtypeStruct((db, 1, (n_steps + 1) * g * page), I32),
        compiler_params=_cparams(("parallel", "arbitrary")),
        name="paged_index_scores",
    )(page_table.reshape(-1), qidx, wcol, k_new_tile, *([idx_cache_t] * g))


def _select_kernel(keys_ref, tau_ref, jmax_ref, *, k_sel, n_index_bits):
    keys = keys_ref[...]
    idx = lax.broadcasted_iota(I32, keys.shape, 1)
    shape = (keys.shape[0], 1)

    def count_fn(pred):
        return jnp.sum(jnp.where(pred(keys, idx), 1, 0), axis=1, keepdims=True)

    tau = _kth_largest(count_fn, k_sel, shape)
    need = k_sel - count_fn(lambda key, i: key > tau)
    jmax = _tie_cutoff(count_fn, tau, need, n_index_bits, shape)
    tau_ref[...] = jnp.broadcast_to(tau, tau_ref.shape)
    jmax_ref[...] = jnp.broadcast_to(jmax, jmax_ref.shape)


def _select_rows(keys2d, k_sel):
    rows, width = keys2d.shape
    n_index_bits = max(1, int(math.ceil(math.log2(width))))
    return pl.pallas_call(
        functools.partial(_select_kernel, k_sel=k_sel, n_index_bits=n_index_bits),
        out_shape=[jax.ShapeDtypeStruct((rows, LANES), I32), jax.ShapeDtypeStruct((rows, LANES), I32)],
        compiler_params=pltpu.CompilerParams(vmem_limit_bytes=VMEM_LIMIT),
        name="select_rows",
    )(keys2d)


def _even_tail_kernel(z_ref, conv_ref, s_ref, oa_ref, convw_ref, gp_ref, normg_ref, lamv_ref, subln_ref,
                      ya_ref, yb_ref, snew_ref, *, lam_init):
    z = z_ref[...]
    hw = B_HEADS * B_DK
    lv = lamv_ref[...]
    lam = (jnp.exp(jnp.sum(lv[0:1] * lv[1:2], axis=1, keepdims=True))
           - jnp.exp(jnp.sum(lv[2:3] * lv[3:4], axis=1, keepdims=True)) + lam_init)
    oa = oa_ref[...]
    for hg in range(A_HEADS):
        o = oa[2 * hg:2 * hg + 1] - lam * oa[2 * hg + 1:2 * hg + 2]
        y = o * lax.rsqrt(jnp.mean(o * o, axis=-1, keepdims=True) + EPS)
        y = (y * subln_ref[...]) * (1.0 - lam_init)
        gate = z[:, EVEN_OFF[3] + hg * A_DV:EVEN_OFF[3] + (hg + 1) * A_DV]
        ya_ref[:, hg * A_DV:(hg + 1) * A_DV] = (y * _silu(gate)).astype(BF16)
    x_new = z[:, EVEN_OFF[4]:EVEN_OFF[4] + B_QKV]
    conv = convw_ref[CONV_W - 1:CONV_W, :] * x_new
    cp = conv_ref[...]
    for i in range(CONV_W - 1):
        conv = conv + convw_ref[i:i + 1, :] * cp[i:i + 1, :]
    act = _silu(conv)
    ab = z[:, EVEN_OFF[6]:EVEN_OFF[6] + LANES]
    gp = gp_ref[...]
    row = lax.broadcasted_iota(I32, (B_DK, B_DV), 0)
    col = lax.broadcasted_iota(I32, (B_DK, B_DV), 1)
    eye = row == col
    for h in range(B_HEADS):
        q = act[:, h * B_DK:(h + 1) * B_DK]
        k = act[:, hw + h * B_DK:hw + (h + 1) * B_DK]
        v = act[:, 2 * hw + h * B_DV:2 * hw + (h + 1) * B_DV]
        q = q * lax.rsqrt(jnp.sum(q * q, axis=-1, keepdims=True) + EPS) * (B_DK ** -0.5)
        k = k * lax.rsqrt(jnp.sum(k * k, axis=-1, keepdims=True) + EPS)
        g = -jnp.exp(gp[0:1, h:h + 1]) * _softplus(ab[:, h:h + 1] + gp[1:2, h:h + 1])
        beta = jax.nn.sigmoid(ab[:, B_HEADS + h:B_HEADS + h + 1])
        eg = jnp.exp(g)
        s0 = s_ref[h]
        kcol = jnp.sum(jnp.where(eye, jnp.broadcast_to(k, (B_DK, B_DK)), 0.0), axis=1, keepdims=True)
        qcol = jnp.sum(jnp.where(eye, jnp.broadcast_to(q, (B_DK, B_DK)), 0.0), axis=1, keepdims=True)
        ks = jnp.sum(kcol * s0, axis=0, keepdims=True)
        qs = jnp.sum(qcol * s0, axis=0, keepdims=True)
        u = beta * (v - eg * ks)
        qk = jnp.sum(q * k, axis=1, keepdims=True)
        o = eg * qs + qk * u
        snew_ref[h] = eg * s0 + kcol * u
        y = o * lax.rsqrt(jnp.mean(o * o, axis=-1, keepdims=True) + EPS) * normg_ref[...]
        gate = z[:, EVEN_OFF[5] + h * B_DV:EVEN_OFF[5] + (h + 1) * B_DV]
        yb_ref[:, h * B_DV:(h + 1) * B_DV] = (y * _silu(gate)).astype(BF16)


def _even_tail(zf_s, conv_prev, s_prev, oa, conv_w, a_log, dt_bias, norm_g, lamv, subln, lam_init):
    db, npad = zf_s.shape
    gp = jnp.zeros((SUBLANES, LANES), F32).at[0, :B_HEADS].set(a_log).at[1, :B_HEADS].set(dt_bias)
    hw = B_HEADS * B_DV
    return pl.pallas_call(
        functools.partial(_even_tail_kernel, lam_init=lam_init),
        grid=(db,),
        in_specs=[pl.BlockSpec((None, 1, npad), lambda b: (b, 0, 0)),
                  pl.BlockSpec((None, CONV_W - 1, B_QKV), lambda b: (b, 0, 0)),
                  pl.BlockSpec((None, B_HEADS, B_DK, B_DV), lambda b: (b, 0, 0, 0)),
                  pl.BlockSpec((None, DEC_ROWS, LANES), lambda b: (b, 0, 0)),
                  pl.BlockSpec((CONV_W, B_QKV), lambda b: (0, 0)),
                  pl.BlockSpec((SUBLANES, LANES), lambda b: (0, 0)),
                  pl.BlockSpec((1, B_DV), lambda b: (0, 0)),
                  pl.BlockSpec((4, A_DH), lambda b: (0, 0)),
                  pl.BlockSpec((1, A_DV), lambda b: (0, 0))],
        out_specs=[pl.BlockSpec((None, 1, A_HEADS * A_DV), lambda b: (b, 0, 0)),
                   pl.BlockSpec((None, 1, hw), lambda b: (b, 0, 0)),
                   pl.BlockSpec((None, B_HEADS, B_DK, B_DV), lambda b: (b, 0, 0, 0))],
        out_shape=[jax.ShapeDtypeStruct((db, 1, A_HEADS * A_DV), BF16),
                   jax.ShapeDtypeStruct((db, 1, hw), BF16),
                   jax.ShapeDtypeStruct((db, B_HEADS, B_DK, B_DV), F32)],
        compiler_params=_cparams(("parallel",)),
        name="even_tail",
    )(zf_s.reshape(db, 1, npad), conv_prev, s_prev, oa, conv_w, gp, norm_g.reshape(1, B_DV), lamv,
      subln.reshape(1, A_DV))


def _odd_tail_kernel(z_ref, oc_ref, y_ref):
    z = z_ref[...]
    oc = oc_ref[...]
    for hg in range(C_HEADS):
        gate = z[:, ODD_OFF[3] + hg * C_DH:ODD_OFF[3] + (hg + 1) * C_DH]
        y_ref[:, hg * C_DH:(hg + 1) * C_DH] = (oc[hg:hg + 1] * _silu(gate)).astype(BF16)


def _odd_tail(zf_s, oc):
    db, npad = zf_s.shape
    return pl.pallas_call(
        _odd_tail_kernel,
        grid=(db,),
        in_specs=[pl.BlockSpec((None, 1, npad), lambda b: (b, 0, 0)),
                  pl.BlockSpec((None, DEC_ROWS, LANES), lambda b: (b, 0, 0))],
        out_specs=pl.BlockSpec((None, 1, C_HEADS * C_DH), lambda b: (b, 0, 0)),
        out_shape=jax.ShapeDtypeStruct((db, 1, C_HEADS * C_DH), BF16),
        compiler_params=_cparams(("parallel",)),
        name="odd_tail",
    )(zf_s.reshape(db, 1, npad), oc)


def _bias_by_distance(table):
    n = jnp.arange(FAR_DIST + 1)
    exact = N_BUCKETS // 2
    nf = jnp.maximum(n, 1).astype(F32)
    large = exact + (jnp.log(nf / exact) / math.log(MAX_DIST / exact) * (N_BUCKETS - exact)).astype(I32)
    bucket = jnp.where(n < exact, n, jnp.minimum(large, N_BUCKETS - 1))
    return table[bucket].astype(F32)


def _prompt_near_tiles(bd, group):
    assert FAR_DIST <= BLK
    heads = bd.shape[1]
    rel = ((bd - bd[FAR_DIST][None, :]) * LOG2E).T
    f = jnp.concatenate([jnp.full((heads, BLK - 1), NEG, F32), rel,
                         jnp.broadcast_to(rel[:, FAR_DIST:], (heads, 2 * BLK - 1 - FAR_DIST))], axis=1)
    period = 3 * BLK
    g = jnp.pad(f[:, ::-1], ((0, 0), (0, period - f.shape[1])))
    wrapped = jnp.tile(g, (1, BLK + 1))[:, :BLK * (period + 1)].reshape(heads, BLK, period + 1)
    strip = wrapped[:, ::-1, :2 * BLK]
    tiles = jnp.stack([strip[:, :, :BLK], strip[:, :, BLK:]], axis=1)
    n_kv = heads // group
    tiles = tiles.reshape(n_kv, group, 2, BLK, BLK)
    return jnp.transpose(tiles, (0, 2, 1, 3, 4)).reshape(n_kv, 2, group * BLK, BLK)


def _decode_bias(bd, row_heads, past, page, g):
    heads = jnp.asarray(row_heads, I32)
    far = jnp.broadcast_to(bd[FAR_DIST][heads][:, None], (len(row_heads), g * page))
    pos = past - g * page + jnp.arange(g * page)
    dist = jnp.minimum(past - pos, FAR_DIST)
    last = bd[dist][:, heads].T
    new = jnp.broadcast_to(bd[0][heads][:, None], (len(row_heads), LANES))
    return far.astype(F32), last.astype(F32), new.astype(F32)


def _pad_cols(w, mult):
    n = w.shape[1]
    return jnp.pad(w, ((0, 0), (0, _round_up(n, mult) - n)))


def kernel(x_prompt, x_sample, cache_a_k, cache_a_v, state_b_s, state_b_conv, cache_c_k, cache_c_v, cache_c_idx,
           page_table, meta, bias_table, final_norm, norm_e, w_in_e, w_out_e, lam_q1, lam_k1, lam_q2, lam_k2,
           subln_a, conv_b, a_log_b, dt_bias_b, norm_b, norm_o, w_in_o, w_out_o):
    nb, seq, d = x_prompt.shape
    n_meta = meta.shape[0]
    l = seq + n_meta
    lp = _round_up(l, CHUNK)
    nq_real = pl.cdiv(l, BLK)
    db = x_sample.shape[0]
    n_pages = page_table.shape[1]
    page = cache_a_k.shape[2]
    past = n_pages * page
    n_pool = cache_a_k.shape[1]
    assert x_sample.shape[1] == 1 and norm_e.shape[0] == 1 and norm_o.shape[0] == 1
    lam_init = 0.8 - 0.6 * math.exp(-0.3 * 0)

    tn_e, tn_o = 8 * LANES, 5 * LANES
    w_e = _pad_cols(w_in_e[0], tn_e).astype(BF16)
    w_o = _pad_cols(w_in_o[0], tn_o).astype(BF16)
    w_out_a = w_out_e[0][:A_HEADS * A_DV].astype(BF16)
    w_out_b = w_out_e[0][A_HEADS * A_DV:].astype(BF16)
    w_out_c = w_out_o[0].astype(BF16)
    tm = CHUNK
    tm_proj = 2 * CHUNK if (nb * lp) % (2 * CHUNK) == 0 else CHUNK
    dbp = _round_up(db, SUBLANES)

    bd = _bias_by_distance(bias_table)
    tiles_a = _prompt_near_tiles(bd, A_GROUP)
    tiles_c = tiles_a if (A_KV, A_GROUP) == (C_KV, C_GROUP) else _prompt_near_tiles(bd, C_GROUP)
    lamv = jnp.stack([lam_q1[0], lam_k1[0], lam_q2[0], lam_k2[0]]).astype(F32)

    hp = jnp.concatenate([jnp.broadcast_to(meta.astype(F32)[None], (nb, n_meta, d)), x_prompt], axis=1)
    hp = jnp.pad(hp, ((0, 0), (0, lp - l), (0, 0))).reshape(nb * lp, d)
    kv_tile = EVEN_OFF[1] // tn_e
    assert EVEN_OFF[3] <= (kv_tile + 1) * tn_e
    zf, zb = _norm_proj(hp, norm_e[0], w_e, tm_proj, tn_e, (kv_tile, kv_tile + 1))
    ya = _diff_attn_prompt(zf, zb, kv_tile * tn_e, tiles_a, lamv, subln_a[0], nb, lp, nq_real, lam_init)
    yb, pb_s = _gdn_prompt(zf, conv_b[0], a_log_b[0], dt_bias_b[0], norm_b[0], nb, lp, l)
    h1 = _out_proj([ya, yb], [w_out_a, w_out_b], hp, tm)
    z3 = zf.reshape(nb, lp, -1)
    pa_k = z3[:, :l, EVEN_OFF[1]:EVEN_OFF[2]].reshape(1, nb, l, A_KV, 2, A_DH)
    pa_v = z3[:, :l, EVEN_OFF[2]:EVEN_OFF[3]].reshape(1, nb, l, A_KV, A_DV)
    pb_conv = z3[:, l - (CONV_W - 1):l, EVEN_OFF[4]:EVEN_OFF[5]][None]

    zf1, zb1 = _norm_proj(h1, norm_o[0], w_o, tm_proj, tn_o, (0, w_o.shape[1] // tn_o))
    k_sel_p = min(TOPK_MAX, l // 4)
    yc = _sparse_attn_prompt(zf1, zb1, tiles_c, nb, lp, nq_real, k_sel_p)
    yp = _out_proj([yc], [w_out_c], h1, tm, final_gain=final_norm)
    y_prompt = yp.reshape(nb, lp, d)[:, n_meta:l]
    z13 = zf1.reshape(nb, lp, -1)
    pc_k = z13[:, :l, ODD_OFF[1]:ODD_OFF[2]].reshape(1, nb, l, C_KV, C_DH)
    pc_v = z13[:, :l, ODD_OFF[2]:ODD_OFF[3]].reshape(1, nb, l, C_KV, C_DH)
    pc_idx = z13[:, :l, ODD_OFF[5]:ODD_OFF[6]][None]

    hs = jnp.pad(x_sample.reshape(db, d), ((0, dbp - db), (0, 0)))
    zs, _ = _norm_proj(hs, norm_e[0], w_e, dbp, tn_e, (0, 1))
    zs = zs[:db]
    qa = zs[:, :EVEN_OFF[1]].reshape(db, A_KV, A_GROUP, 2, A_DH) * (A_DH ** -0.5)
    qprime = jnp.einsum('bhgcd,hi,cj->bhgcijd', qa, jnp.eye(A_KV, dtype=F32), jnp.eye(2, dtype=F32))
    qprime = qprime.reshape(db, DEC_ROWS, A_KV * 2 * A_DH).astype(BF16)
    k_new = zs[:, EVEN_OFF[1]:EVEN_OFF[2]].reshape(db, 1, -1)
    v_new = zs[:, EVEN_OFF[2]:EVEN_OFF[3]].reshape(db, 1, -1)
    g_dec = _pages_per_step(n_pages)
    rows_a = [r // 2 for r in range(DEC_ROWS)]
    bfar, blast, bnew = _decode_bias(bd, rows_a, past, page, g_dec)
    ak_t = jnp.transpose(cache_a_k[0], (0, 2, 3, 4, 1)).reshape(n_pool, A_KV * 2 * A_DH, page)
    av_r = cache_a_v[0].reshape(n_pool, page * A_KV, A_DV)
    oa = _paged_attention(qprime, k_new, v_new, bfar, blast, bnew, ak_t, av_r, page_table, page,
                          k_feature_major=True)
    ya_s, yb_s, sb_s = _even_tail(zs, state_b_conv[0], state_b_s[0], oa, conv_b[0], a_log_b[0], dt_bias_b[0],
                                  norm_b[0], lamv, subln_a[0], lam_init)
    hs_pad = lambda y: jnp.pad(y.reshape(db, -1), ((0, dbp - db), (0, 0)))
    hs1 = _out_proj([hs_pad(ya_s), hs_pad(yb_s)], [w_out_a, w_out_b], hs, dbp)
    sa_k = zs[:, EVEN_OFF[1]:EVEN_OFF[2]].reshape(1, db, 1, A_KV, 2, A_DH)
    sa_v = zs[:, EVEN_OFF[2]:EVEN_OFF[3]].reshape(1, db, 1, A_KV, A_DV)
    sb_conv = jnp.concatenate([state_b_conv[0][:, 1:], zs[:, None, EVEN_OFF[4]:EVEN_OFF[5]]], axis=1)[None]

    zs1, _ = _norm_proj(hs1, norm_o[0], w_o, dbp, tn_o, (0, 1))
    zs1 = zs1[:db]
    qidx = (zs1[:, ODD_OFF[4]:ODD_OFF[5]].reshape(db, IDX_HEADS, IDX_DIM) * (IDX_DIM ** -0.5)).astype(BF16)
    wcol = jnp.broadcast_to((zs1[:, ODD_OFF[6]:ODD_OFF[7]] * (IDX_HEADS ** -0.5))[:, :, None],
                            (db, IDX_HEADS, LANES))
    ki_new = zs1[:, ODD_OFF[5]:ODD_OFF[6]]
    ki_tile = jnp.zeros((db, IDX_DIM, page), F32).at[:, :, 0].set(ki_new)
    ci_t = jnp.transpose(cache_c_idx[0], (0, 2, 1))
    keys = _paged_index_scores(qidx, wcol, ki_tile, ci_t, page_table)
    k_sel_s = min(TOPK_MAX, (past + 1) // 4)
    tau, jmax = _select_rows(keys.reshape(db, -1), k_sel_s)
    tau, jmax = tau[:, 0], jmax[:, 0]
    key_new = keys[:, 0, past]
    selnew = ((key_new > tau) | ((key_new == tau) & (past <= jmax))).astype(I32)
    qc = zs1[:, :ODD_OFF[1]].reshape(db, C_KV, C_GROUP, C_DH) * (C_DH ** -0.5)
    qcp = jnp.einsum('bhgd,hi->bhgid', qc, jnp.eye(C_KV, dtype=F32)).reshape(db, C_HEADS, C_KV * C_DH)
    qcp = jnp.pad(qcp, ((0, 0), (0, DEC_ROWS - C_HEADS), (0, 0))).astype(BF16)
    kc_new = zs1[:, ODD_OFF[1]:ODD_OFF[2]].reshape(db, 1, -1)
    vc_new = zs1[:, ODD_OFF[2]:ODD_OFF[3]].reshape(db, 1, -1)
    rows_c = [r if r < C_HEADS else 0 for r in range(DEC_ROWS)]
    cfar, clast, cnew = _decode_bias(bd, rows_c, past, page, g_dec)
    ck_r = cache_c_k[0].reshape(n_pool, page * C_KV, C_DH)
    cv_r = cache_c_v[0].reshape(n_pool, page * C_KV, C_DH)
    oc = _paged_attention(qcp, kc_new, vc_new, cfar, clast, cnew, ck_r, cv_r, page_table, page,
                          k_feature_major=False, mask_args=(keys, tau, jmax, selnew))
    yc_s = _odd_tail(zs1, oc)
    ys = _out_proj([hs_pad(yc_s)], [w_out_c], hs1, dbp, final_gain=final_norm)
    y_sample = ys[:db].reshape(db, 1, d)
    sc_k = zs1[:, ODD_OFF[1]:ODD_OFF[2]].reshape(1, db, 1, C_KV, C_DH)
    sc_v = zs1[:, ODD_OFF[2]:ODD_OFF[3]].reshape(1, db, 1, C_KV, C_DH)
    sc_idx = zs1[:, None, ODD_OFF[5]:ODD_OFF[6]][None]

    return (y_prompt, y_sample, pa_k, pa_v, pb_s[None], pb_conv, pc_k, pc_v, pc_idx,
            sa_k, sa_v, sb_s[None], sb_conv, sc_k, sc_v, sc_idx)
```

```python
import functools
import math

import jax
import jax.numpy as jnp
import numpy as np
from jax import lax
from jax.experimental import pallas as pl
from jax.experimental.pallas import tpu as pltpu

F32 = jnp.float32
BF16 = jnp.bfloat16
I32 = jnp.int32

EPS = 1e-6
N_BUCKETS = 32
MAX_DIST = 128
FAR_DIST = MAX_DIST

A_HEADS, A_KV, A_GROUP, A_DH, A_DV = 8, 2, 4, 64, 128
B_HEADS, B_DK, B_DV, CONV_W = 4, 128, 128, 4
C_HEADS, C_KV, C_GROUP, C_DH = 8, 2, 4, 128
IDX_HEADS, IDX_DIM, TOPK_MAX = 8, 64, 256
B_QKV = 2 * B_HEADS * B_DK + B_HEADS * B_DV

EVEN_COLS = (A_HEADS * 2 * A_DH, A_KV * 2 * A_DH, A_KV * A_DV, A_HEADS * A_DV, B_QKV, B_HEADS * B_DV, B_HEADS, B_HEADS)
ODD_COLS = (C_HEADS * C_DH, C_KV * C_DH, C_KV * C_DH, C_HEADS * C_DH, IDX_HEADS * IDX_DIM, IDX_DIM, IDX_HEADS)
EVEN_OFF = tuple(int(v) for v in np.cumsum((0,) + EVEN_COLS))
ODD_OFF = tuple(int(v) for v in np.cumsum((0,) + ODD_COLS))

LANES = 128
SUBLANES = 8
BLK = 128
CHUNK = 512
SUBS = CHUNK // BLK
NEG = -1e30
INT_MIN = -2 ** 31
INT_MAX = 2 ** 31 - 1
LOG2E = math.log2(math.e)
VMEM_LIMIT = 56 * 1024 * 1024
HI = lax.Precision.HIGHEST

NT = (((1,), (1,)), ((), ()))


def _round_up(x, m):
    return (x + m - 1) // m * m


def _cparams(sem):
    return pltpu.CompilerParams(dimension_semantics=sem, vmem_limit_bytes=VMEM_LIMIT)


def _silu(x):
    return x * jax.nn.sigmoid(x)


def _sortable_key(score):
    score = jnp.where(score == 0.0, 0.0, score)
    bits = pltpu.bitcast(score, I32)
    return bits ^ ((bits >> 31) & jnp.int32(0x7FFFFFFF))


def _lane_tile(x, n):
    return jnp.concatenate([x] * n, axis=1)


def _proj_kernel(x_ref, g_ref, w_ref, of_ref, ob_ref, xn_ref, *, bf16_tiles):
    j = pl.program_id(1)

    @pl.when(j == 0)
    def _():
        xf = x_ref[...]
        y = xf * lax.rsqrt(jnp.mean(xf * xf, axis=-1, keepdims=True) + EPS)
        xn_ref[...] = (y * g_ref[...]).astype(BF16)

    acc = jnp.dot(xn_ref[...], w_ref[...], preferred_element_type=F32)
    of_ref[...] = acc

    @pl.when((j >= bf16_tiles[0]) & (j < bf16_tiles[1]))
    def _():
        ob_ref[...] = acc.astype(BF16)


def _norm_proj(x, gain, w_bf16, tm, tn, bf16_tiles):
    m, d = x.shape
    n = w_bf16.shape[1]
    t0, t1 = bf16_tiles
    return pl.pallas_call(
        functools.partial(_proj_kernel, bf16_tiles=bf16_tiles),
        grid=(m // tm, n // tn),
        in_specs=[pl.BlockSpec((tm, d), lambda i, j: (i, 0)),
                  pl.BlockSpec((1, d), lambda i, j: (0, 0)),
                  pl.BlockSpec((d, tn), lambda i, j: (0, j))],
        out_specs=[pl.BlockSpec((tm, tn), lambda i, j: (i, j)),
                   pl.BlockSpec((tm, tn), lambda i, j: (i, jnp.clip(j - t0, 0, t1 - t0 - 1)))],
        out_shape=[jax.ShapeDtypeStruct((m, n), F32), jax.ShapeDtypeStruct((m, (t1 - t0) * tn), BF16)],
        scratch_shapes=[pltpu.VMEM((tm, d), BF16)],
        compiler_params=_cparams(("parallel", "arbitrary")),
        name="norm_proj",
    )(x, gain.reshape(1, d), w_bf16)


def _outproj_kernel(*refs, n_lhs, final_norm):
    ys = refs[:n_lhs]
    ws = refs[n_lhs:2 * n_lhs]
    h_ref = refs[2 * n_lhs]
    pos = 2 * n_lhs + 1
    fn_ref = refs[pos] if final_norm else None
    out_ref = refs[-1]
    acc = h_ref[...]
    for y_ref, w_ref in zip(ys, ws):
        acc = acc + jnp.dot(y_ref[...], w_ref[...], preferred_element_type=F32)
    if final_norm:
        y = acc * lax.rsqrt(jnp.mean(acc * acc, axis=-1, keepdims=True) + EPS)
        acc = y * fn_ref[...]
    out_ref[...] = acc


def _out_proj(ys, ws, h, tm, final_gain=None):
    m, n = h.shape
    n_lhs = len(ys)
    in_specs = [pl.BlockSpec((tm, y.shape[1]), lambda i: (i, 0)) for y in ys]
    in_specs += [pl.BlockSpec(w.shape, lambda i: (0, 0)) for w in ws]
    in_specs += [pl.BlockSpec((tm, n), lambda i: (i, 0))]
    args = list(ys) + list(ws) + [h]
    if final_gain is not None:
        in_specs += [pl.BlockSpec((1, n), lambda i: (0, 0))]
        args += [final_gain.reshape(1, n)]
    return pl.pallas_call(
        functools.partial(_outproj_kernel, n_lhs=n_lhs, final_norm=final_gain is not None),
        grid=(m // tm,),
        in_specs=in_specs,
        out_specs=pl.BlockSpec((tm, n), lambda i: (i, 0)),
        out_shape=jax.ShapeDtypeStruct((m, n), F32),
        compiler_params=_cparams(("parallel",)),
        name="out_proj",
    )(*args)


def _flash_steps(ss, v1s, m_refs):
    each = range(len(ss))
    m_old = [m_refs[i][0][m_refs[i][2]] for i in each]
    m_new = [jnp.maximum(m_old[i], jnp.max(ss[i], axis=1, keepdims=True)) for i in each]
    alpha = [jnp.exp2(m_old[i] - m_new[i]) for i in each]
    p = [jnp.exp2(ss[i] - _lane_tile(m_new[i], ss[i].shape[1] // LANES)).astype(BF16) for i in each]
    pv = [jnp.dot(p[i], v1s[i], preferred_element_type=F32) for i in each]
    for i in each:
        m_ref, acc_ref, idx = m_refs[i]
        acc_ref[idx] = _lane_tile(alpha[i], 2) * acc_ref[idx] + pv[i]
        m_ref[idx] = m_new[i]


def _far_chunk_loop(n_far, run_chunks):
    def quad(i, carry):
        run_chunks([4 * i, 4 * i + 1, 4 * i + 2, 4 * i + 3])
        return carry

    n_quad = lax.shift_right_logical(n_far, 2)
    lax.fori_loop(0, n_quad, quad, 0)

    @pl.when(jnp.bitwise_and(n_far, 2) == 2)
    def _():
        run_chunks([4 * n_quad, 4 * n_quad + 1])

    @pl.when(jnp.bitwise_and(n_far, 1) == 1)
    def _():
        run_chunks([n_far - 1])


def _build_near_bias(strip_ref, t_prev, t_diag, r, lead=()):
    for sb in range(2 * SUBS):
        rel = sb - SUBS - r
        tile = jnp.where(rel == 0, t_diag, jnp.where(rel == -1, t_prev, jnp.where(rel < -1, 0.0, NEG)))
        strip_ref[lead + (sb,)] = tile


def _diff_attn_kernel(q_ref, k_ref, v_ref, gate_ref, tiles_ref, lamv_ref, subln_ref, out_ref,
                      qm_ref, strip_ref, m_ref, acc_ref, *, lam_init, nq_real):
    qi = pl.program_id(2)

    @pl.when(qi >= nq_real)
    def _():
        out_ref[...] = jnp.zeros(out_ref.shape, out_ref.dtype)

    @pl.when(qi < nq_real)
    def _():
        lane = lax.broadcasted_iota(I32, (BLK, 2 * A_DH), 1)
        for g in range(A_GROUP):
            qg = q_ref[:, g * 2 * A_DH:(g + 1) * 2 * A_DH] * (A_DH ** -0.5 * LOG2E)
            for c in range(2):
                keep = (lane < A_DH) if c == 0 else (lane >= A_DH)
                qm_ref[c, g * BLK:(g + 1) * BLK, :] = jnp.where(keep, qg, 0.0).astype(BF16)
        m_ref[...] = jnp.full(m_ref.shape, NEG, F32)
        acc_ref[...] = jnp.zeros(acc_ref.shape, F32)
        r = jnp.bitwise_and(qi, SUBS - 1)
        jd = lax.shift_right_logical(qi, int(math.log2(SUBS)))
        _build_near_bias(strip_ref, tiles_ref[0], tiles_ref[1], r)
        ones = jnp.ones((CHUNK, A_DV), BF16)

        def run_chunks(js, first_sbs=None):
            starts = [pl.multiple_of(j * CHUNK, CHUNK) for j in js]
            k = jnp.concatenate([k_ref[pl.ds(st, CHUNK), :] for st in starts], axis=0)
            v1 = jnp.concatenate(
                [jnp.concatenate([v_ref[pl.ds(st, CHUNK), :], ones], axis=1) for st in starts], axis=0)
            ss = [lax.dot_general(qm_ref[c], k, NT, preferred_element_type=F32) for c in range(2)]
            if first_sbs is not None:
                bias = jnp.concatenate([strip_ref[sb + i] for sb in first_sbs for i in range(SUBS)], axis=1)
                ss = [s + bias for s in ss]
            _flash_steps(ss, [v1, v1], [(m_ref, acc_ref, c) for c in range(2)])

        _far_chunk_loop(jnp.maximum(jd - 1, 0), run_chunks)

        @pl.when(jd >= 1)
        def _():
            run_chunks([jd - 1, jd], [0, SUBS])

        @pl.when(jd == 0)
        def _():
            run_chunks([jd], [SUBS])

        lv = lamv_ref[...]
        lam = (jnp.exp(jnp.sum(lv[0:1] * lv[1:2], axis=1, keepdims=True))
               - jnp.exp(jnp.sum(lv[2:3] * lv[3:4], axis=1, keepdims=True)) + lam_init)
        a0 = acc_ref[0]
        a1 = acc_ref[1]
        o = a0[:, :A_DV] / a0[:, A_DV:] - lam * (a1[:, :A_DV] / a1[:, A_DV:])
        y = o * lax.rsqrt(jnp.mean(o * o, axis=-1, keepdims=True) + EPS)
        y = (y * subln_ref[...]) * (1.0 - lam_init)
        for g in range(A_GROUP):
            gate = gate_ref[:, g * A_DV:(g + 1) * A_DV]
            out_ref[:, g * A_DV:(g + 1) * A_DV] = (y[g * BLK:(g + 1) * BLK] * _silu(gate)).astype(BF16)


def _diff_attn_prompt(zf, zb, zb_col0, near_tiles, lamv, subln, nb, lp, nq_real, lam_init):
    nq = lp // BLK
    qw = A_GROUP * 2 * A_DH
    k_blk0 = (EVEN_OFF[1] - zb_col0) // (2 * A_DH)
    v_blk0 = (EVEN_OFF[2] - zb_col0) // A_DV
    g_blk0 = EVEN_OFF[3] // (A_GROUP * A_DV)
    rows = A_GROUP * BLK
    return pl.pallas_call(
        functools.partial(_diff_attn_kernel, lam_init=lam_init, nq_real=nq_real),
        grid=(nb, A_KV, nq),
        in_specs=[pl.BlockSpec((BLK, qw), lambda b, h, i: (b * nq + i, h)),
                  pl.BlockSpec((lp, 2 * A_DH), lambda b, h, i: (b, k_blk0 + h)),
                  pl.BlockSpec((lp, A_DV), lambda b, h, i: (b, v_blk0 + h)),
                  pl.BlockSpec((BLK, A_GROUP * A_DV), lambda b, h, i: (b * nq + i, g_blk0 + h)),
                  pl.BlockSpec((None, 2, rows, BLK), lambda b, h, i: (h, 0, 0, 0)),
                  pl.BlockSpec((4, A_DH), lambda b, h, i: (0, 0)),
                  pl.BlockSpec((1, A_DV), lambda b, h, i: (0, 0))],
        out_specs=pl.BlockSpec((BLK, A_GROUP * A_DV), lambda b, h, i: (b * nq + i, h)),
        out_shape=jax.ShapeDtypeStruct((nb * lp, A_HEADS * A_DV), BF16),
        scratch_shapes=[pltpu.VMEM((2, rows, 2 * A_DH), BF16),
                        pltpu.VMEM((2 * SUBS, rows, BLK), F32),
                        pltpu.VMEM((2, rows, LANES), F32),
                        pltpu.VMEM((2, rows, 2 * A_DV), F32)],
        compiler_params=_cparams(("parallel", "parallel", "arbitrary")),
        name="diff_attn_prompt",
    )(zf, zb, zb, zf, near_tiles, lamv, subln.reshape(1, A_DV))


def _softplus(x):
    return jnp.maximum(x, 0.0) + jnp.log1p(jnp.exp(-jnp.abs(x)))


def _split_bf16(a):
    hi = a.astype(BF16)
    return hi, (a - hi.astype(F32)).astype(BF16)


def _dot3(a, b):
    (ah, al), (bh, bl) = a, b
    return (jnp.dot(ah, bh, preferred_element_type=F32)
            + (jnp.dot(ah, bl, preferred_element_type=F32) + jnp.dot(al, bh, preferred_element_type=F32)))


def _gdn_chunk_math(s0, q, k, v, gb, betab, row_ge, row_gt):
    n = len(q)
    c = q[0].shape[0]
    each = range(n)
    dot = functools.partial(jnp.dot, preferred_element_type=F32)
    dot_nt = functools.partial(lax.dot_general, dimension_numbers=NT, preferred_element_type=F32)
    ltri = jnp.where(row_ge, 1.0, 0.0).astype(F32)
    gcum = [jnp.dot(ltri, gb[i], precision=HI, preferred_element_type=F32) for i in each]
    decay = [jnp.where(row_ge, jnp.exp(jnp.where(row_ge, gcum[i] - gcum[i].T, 0.0)), 0.0) for i in each]
    kb = [k[i].astype(BF16) for i in each]
    qb = [q[i].astype(BF16) for i in each]
    s0b = [s0[i].astype(BF16) for i in each]
    kk = [dot_nt(kb[i], kb[i]) for i in each]
    x = [-jnp.where(row_gt, betab[i] * decay[i] * kk[i], 0.0) for i in each]
    tm = list(x)
    p = list(x)
    for _ in range(int(math.log2(c)) - 1):
        ps = [_split_bf16(p[i]) for i in each]
        p = [_dot3(ps[i], ps[i]) for i in each]
        ps = [_split_bf16(p[i]) for i in each]
        tms = [_split_bf16(tm[i]) for i in each]
        tm = [tm[i] + p[i] + _dot3(tms[i], ps[i]) for i in each]
    eg = [jnp.exp(gcum[i]) for i in each]
    ks = [dot(kb[i], s0b[i]) for i in each]
    rhs = [betab[i] * (v[i] - eg[i] * ks[i]) for i in each]
    u = [rhs[i] + dot(tm[i].astype(BF16), rhs[i].astype(BF16)) for i in each]
    ub = [u[i].astype(BF16) for i in each]
    qk = [dot_nt(qb[i], kb[i]) * decay[i] for i in each]
    o = [eg[i] * dot(qb[i], s0b[i]) + dot(qk[i].astype(BF16), ub[i]) for i in each]
    glast = [gcum[i][c - 1:c, :] for i in each]
    kd = [k[i] * jnp.exp(glast[i] - gcum[i]) for i in each]
    s_new = [jnp.exp(glast[i]) * s0[i] + dot(kd[i].T.astype(BF16), ub[i]) for i in each]
    return s_new, o


def _gdn_prompt_kernel(xq_ref, xk_ref, xv_ref, gate_ref, ab_ref, convw_ref, gp_ref, normg_ref,
                       y_ref, sfin_ref, xbuf_ref, s_ref, *, seq_len):
    ci = pl.program_id(0)
    hw = B_HEADS * B_DK
    nb = xq_ref.shape[0]

    @pl.when(ci == 0)
    def _():
        xbuf_ref[:, 0:SUBLANES, :] = jnp.zeros((nb, SUBLANES, B_QKV), F32)
        s_ref[...] = jnp.zeros(s_ref.shape, F32)

    row = lax.broadcasted_iota(I32, (BLK, BLK), 0)
    col = lax.broadcasted_iota(I32, (BLK, BLK), 1)
    row_ge = row >= col
    row_gt = row > col
    valid = (ci * BLK + row) < seq_len
    gp = gp_ref[...]
    chains = [(b, h) for b in range(nb) for h in range(B_HEADS)]
    qs, ks, vs, gbs, betabs = [], [], [], [], []
    for b in range(nb):
        xbuf_ref[b, SUBLANES:SUBLANES + BLK, 0:hw] = xq_ref[b]
        xbuf_ref[b, SUBLANES:SUBLANES + BLK, hw:2 * hw] = xk_ref[b]
        xbuf_ref[b, SUBLANES:SUBLANES + BLK, 2 * hw:3 * hw] = xv_ref[b]
        conv = jnp.zeros((BLK, B_QKV), F32)
        for i in range(CONV_W):
            conv = conv + convw_ref[i:i + 1, :] * xbuf_ref[b, pl.ds(SUBLANES - (CONV_W - 1) + i, BLK), :]
        tail = xbuf_ref[b, BLK:BLK + SUBLANES, :]
        xbuf_ref[b, 0:SUBLANES, :] = tail
        act = _silu(conv)
        ab = ab_ref[b]
        for h in range(B_HEADS):
            q = act[:, h * B_DK:(h + 1) * B_DK]
            k = act[:, hw + h * B_DK:hw + (h + 1) * B_DK]
            v = act[:, 2 * hw + h * B_DV:2 * hw + (h + 1) * B_DV]
            q = q * lax.rsqrt(jnp.sum(q * q, axis=-1, keepdims=True) + EPS) * (B_DK ** -0.5)
            k = k * lax.rsqrt(jnp.sum(k * k, axis=-1, keepdims=True) + EPS)
            a_raw = jnp.broadcast_to(ab[:, h:h + 1], (BLK, BLK))
            b_raw = jnp.broadcast_to(ab[:, B_HEADS + h:B_HEADS + h + 1], (BLK, BLK))
            a_log = gp[0:1, h:h + 1]
            dt_b = gp[1:2, h:h + 1]
            qs.append(q)
            ks.append(k)
            vs.append(v)
            gbs.append(jnp.where(valid, -jnp.exp(a_log) * _softplus(a_raw + dt_b), 0.0))
            betabs.append(jnp.where(valid, jax.nn.sigmoid(b_raw), 0.0))
    s_new, outs = _gdn_chunk_math([s_ref[b, h] for b, h in chains], qs, ks, vs, gbs, betabs, row_ge, row_gt)
    for (b, h), s_bh, o in zip(chains, s_new, outs):
        s_ref[b, h] = s_bh
        y = o * lax.rsqrt(jnp.mean(o * o, axis=-1, keepdims=True) + EPS) * normg_ref[...]
        gate = gate_ref[b, :, h * B_DV:(h + 1) * B_DV]
        y_ref[b, :, h * B_DV:(h + 1) * B_DV] = (y * _silu(gate)).astype(BF16)

    @pl.when(ci == pl.num_programs(0) - 1)
    def _():
        sfin_ref[...] = s_ref[...]


def _gdn_prompt(zf, conv_w, a_log, dt_bias, norm_g, nb, lp, seq_len):
    nc = lp // BLK
    hw = B_HEADS * B_DK
    c0 = EVEN_OFF[4] // hw
    g0 = EVEN_OFF[5] // hw
    ab0 = EVEN_OFF[6] // LANES
    gp = jnp.zeros((SUBLANES, LANES), F32).at[0, :B_HEADS].set(a_log).at[1, :B_HEADS].set(dt_bias)
    z3 = zf.reshape(nb, lp, zf.shape[1])
    y, s_fin = pl.pallas_call(
        functools.partial(_gdn_prompt_kernel, seq_len=seq_len),
        grid=(nc,),
        in_specs=[pl.BlockSpec((nb, BLK, hw), lambda i: (0, i, c0)),
                  pl.BlockSpec((nb, BLK, hw), lambda i: (0, i, c0 + 1)),
                  pl.BlockSpec((nb, BLK, hw), lambda i: (0, i, c0 + 2)),
                  pl.BlockSpec((nb, BLK, hw), lambda i: (0, i, g0)),
                  pl.BlockSpec((nb, BLK, LANES), lambda i: (0, i, ab0)),
                  pl.BlockSpec((CONV_W, B_QKV), lambda i: (0, 0)),
                  pl.BlockSpec((SUBLANES, LANES), lambda i: (0, 0)),
                  pl.BlockSpec((1, B_DV), lambda i: (0, 0))],
        out_specs=[pl.BlockSpec((nb, BLK, hw), lambda i: (0, i, 0)),
                   pl.BlockSpec((nb, B_HEADS, B_DK, B_DV), lambda i: (0, 0, 0, 0))],
        out_shape=[jax.ShapeDtypeStruct((nb, lp, hw), BF16),
                   jax.ShapeDtypeStruct((nb, B_HEADS, B_DK, B_DV), F32)],
        scratch_shapes=[pltpu.VMEM((nb, BLK + SUBLANES, B_QKV), F32),
                        pltpu.VMEM((nb, B_HEADS, B_DK, B_DV), F32)],
        compiler_params=_cparams(("arbitrary",)),
        name="gdn_prompt",
    )(z3, z3, z3, z3, z3, conv_w, gp, norm_g.reshape(1, B_DV))
    return y.reshape(nb * lp, hw), s_fin


def _kth_largest(count_fn, k_sel, shape):
    def step(i, tau):
        cand = tau + lax.shift_left(jnp.int32(1), 31 - i)
        cnt = count_fn(lambda key, idx: key >= cand)
        return jnp.where(cnt >= k_sel, cand, tau)

    return lax.fori_loop(0, 32, step, jnp.full(shape, INT_MIN, I32))


BITS_PER_CHECK = 4


def _kth_separator(count_fn, k_sel, n_keys):
    assert 32 % BITS_PER_CHECK == 0

    def pending(cnt_tau):
        return jnp.max(jnp.where(cnt_tau > k_sel, 1.0, 0.0)) > 0.5

    def group(carry):
        g, tau, cnt_tau, _ = carry
        for b in range(BITS_PER_CHECK):
            cand = tau + lax.shift_left(jnp.int32(1), 31 - (g * BITS_PER_CHECK + b))
            cnt = count_fn(lambda key, idx: key >= cand)
            take = cnt >= k_sel
            tau = jnp.where(take, cand, tau)
            cnt_tau = jnp.where(take, cnt, cnt_tau)
        return g + 1, tau, cnt_tau, pending(cnt_tau)

    init = (jnp.int32(0), jnp.full(n_keys.shape, INT_MIN, I32), n_keys, pending(n_keys))
    out = lax.while_loop(lambda c: (c[0] < 32 // BITS_PER_CHECK) & c[3], group, init)
    return out[1]


def _tie_cutoff(count_fn, tau, need, n_index_bits, shape):
    def step(i, jm):
        cand = jm | lax.shift_left(jnp.int32(1), n_index_bits - 1 - i)
        cnt = count_fn(lambda key, idx: (key == tau) & (idx < cand))
        return jnp.where(cnt < need, cand, jm)

    return lax.fori_loop(0, n_index_bits, step, jnp.zeros(shape, I32))


def _sparse_attn_kernel(qi_ref, kw_ref, kidx_ref, qc_ref, kc_ref, vc_ref, g0_ref, g1_ref, tiles_ref, out_ref,
                        keys_ref, qm_ref, qh_ref, strip_ref, tau_ref, jmax_ref, m_ref, acc_ref,
                        *, k_sel, n_index_bits, nq_real):
    qb = pl.program_id(1)

    @pl.when(qb >= nq_real)
    def _():
        out_ref[...] = jnp.zeros(out_ref.shape, out_ref.dtype)

    @pl.when(qb < nq_real)
    def _():
        lane = lax.broadcasted_iota(I32, (BLK, LANES), 1)
        key_pos = lax.broadcasted_iota(I32, (CHUNK, BLK), 0)
        qry_pos = lax.broadcasted_iota(I32, (CHUNK, BLK), 1)
        r = jnp.bitwise_and(qb, SUBS - 1)
        jd = lax.shift_right_logical(qb, int(math.log2(SUBS)))

        for p in range(IDX_HEADS // 2):
            pair = qi_ref[:, p * LANES:(p + 1) * LANES] * (IDX_DIM ** -0.5)
            swapped = pltpu.roll(pair, IDX_DIM, 1)
            for e, src in enumerate((pair, swapped)):
                h = 2 * p + e
                qm_ref[h * BLK:(h + 1) * BLK, :] = jnp.where(lane < IDX_DIM, src, 0.0).astype(BF16)
        w_rows = kw_ref[...].T * (IDX_HEADS ** -0.5)

        def scores_t(j):
            start = pl.multiple_of(j * CHUNK, CHUNK)
            d = lax.dot_general(kidx_ref[pl.ds(start, CHUNK), :], qm_ref[...], NT, preferred_element_type=F32)
            sc = jnp.zeros((CHUNK, BLK), F32)
            for h in range(IDX_HEADS):
                sc = sc + w_rows[IDX_DIM + h:IDX_DIM + h + 1, :] * jnp.maximum(d[:, h * BLK:(h + 1) * BLK], 0.0)
            return sc

        def score_body(j, carry):
            keys_ref[j] = _sortable_key(scores_t(j))
            return carry

        lax.fori_loop(0, jd, score_body, 0)
        admissible = (jd * CHUNK + key_pos) <= (qb * BLK + qry_pos)
        keys_ref[jd] = _sortable_key(jnp.where(admissible, scores_t(jd), -jnp.inf))

        def count_fn(pred):
            part = SUBLANES * SUBLANES

            def body(j, acc):
                hit = jnp.where(pred(keys_ref[j], j * CHUNK + key_pos), 1.0, 0.0)
                return acc + jnp.sum(hit.reshape(CHUNK // part, part, BLK), axis=0)
            acc = lax.fori_loop(0, jd + 1, body, jnp.zeros((part, BLK), F32))
            return jnp.sum(acc, axis=0, keepdims=True).astype(I32)

        n_keys = qb * BLK + lax.broadcasted_iota(I32, (1, BLK), 1) + 1
        tau = _kth_separator(count_fn, k_sel, n_keys)
        tau_ref[...] = tau
        jmax_ref[...] = jnp.full((1, BLK), INT_MAX, I32)
        need = k_sel - count_fn(lambda key, idx: key > tau)
        n_eq = count_fn(lambda key, idx: key == tau)
        excess = jnp.max(jnp.where(n_eq > need, 1.0, 0.0)) > 0.5

        @pl.when(excess)
        def _():
            jmax_ref[...] = _tie_cutoff(count_fn, tau, need, n_index_bits, (1, BLK))

        jmax = jmax_ref[...]

        for h in range(C_KV):
            for g in range(C_GROUP):
                col = (h * C_GROUP + g) * C_DH
                qh_ref[h, g * BLK:(g + 1) * BLK, :] = (
                    qc_ref[:, col:col + C_DH] * (C_DH ** -0.5 * LOG2E)).astype(BF16)
            _build_near_bias(strip_ref, tiles_ref[h, 0], tiles_ref[h, 1], r, lead=(h,))
        m_ref[...] = jnp.full(m_ref.shape, NEG, F32)
        acc_ref[...] = jnp.zeros(acc_ref.shape, F32)
        ones = jnp.ones((CHUNK, C_DH), BF16)

        def run_chunks(js, first_sbs=None):
            starts = [pl.multiple_of(j * CHUNK, CHUNK) for j in js]
            masks = []
            for j in js:
                key = keys_ref[j]
                tie_ok = jnp.where((j * CHUNK + key_pos) <= jmax, 0.0, NEG)
                sel_t = jnp.where(key > tau, 0.0, jnp.where(key == tau, tie_ok, NEG))
                masks += [sel_t[i * BLK:(i + 1) * BLK, :].T for i in range(SUBS)]
            selneg = jnp.concatenate(masks, axis=1)
            selneg = jnp.concatenate([selneg] * C_GROUP, axis=0)
            ss, v1s = [], []
            for h in range(C_KV):
                cols = slice(h * C_DH, (h + 1) * C_DH)
                k = jnp.concatenate([kc_ref[pl.ds(st, CHUNK), cols] for st in starts], axis=0)
                v1s.append(jnp.concatenate(
                    [jnp.concatenate([vc_ref[pl.ds(st, CHUNK), cols], ones], axis=1) for st in starts], axis=0))
                s = lax.dot_general(qh_ref[h], k, NT, preferred_element_type=F32) + selneg
                if first_sbs is not None:
                    s = s + jnp.concatenate(
                        [strip_ref[h, sb + i] for sb in first_sbs for i in range(SUBS)], axis=1)
                ss.append(s)
            _flash_steps(ss, v1s, [(m_ref, acc_ref, h) for h in range(C_KV)])

        _far_chunk_loop(jnp.maximum(jd - 1, 0), run_chunks)

        @pl.when(jd >= 1)
        def _():
            run_chunks([jd - 1, jd], [0, SUBS])

        @pl.when(jd == 0)
        def _():
            run_chunks([jd], [SUBS])

        half = (C_HEADS // 2) * C_DH
        for h in range(C_KV):
            a = acc_ref[h]
            o = a[:, :C_DH] / a[:, C_DH:]
            gref = g0_ref if h == 0 else g1_ref
            for g in range(C_GROUP):
                gate = gref[:, g * C_DH:(g + 1) * C_DH]
                out_ref[:, h * half + g * C_DH:h * half + (g + 1) * C_DH] = (
                    o[g * BLK:(g + 1) * BLK] * _silu(gate)).astype(BF16)


def _sparse_attn_prompt(zf, zb, near_tiles, nb, lp, nq_real, k_sel):
    nq = lp // BLK
    half = (C_HEADS // 2) * C_DH
    qi0 = ODD_OFF[4] // (IDX_HEADS * IDX_DIM)
    ki0 = ODD_OFF[5] // LANES
    kc0 = ODD_OFF[1] // (C_KV * C_DH)
    vc0 = ODD_OFF[2] // (C_KV * C_DH)
    g0 = ODD_OFF[3] // half
    rows = C_GROUP * BLK
    n_index_bits = max(1, int(math.ceil(math.log2(lp))))
    return pl.pallas_call(
        functools.partial(_sparse_attn_kernel, k_sel=k_sel, n_index_bits=n_index_bits, nq_real=nq_real),
        grid=(nb, nq),
        in_specs=[pl.BlockSpec((BLK, IDX_HEADS * IDX_DIM), lambda b, i: (b * nq + i, qi0)),
                  pl.BlockSpec((BLK, LANES), lambda b, i: (b * nq + i, ki0)),
                  pl.BlockSpec((lp, LANES), lambda b, i: (b, ki0)),
                  pl.BlockSpec((BLK, C_HEADS * C_DH), lambda b, i: (b * nq + i, 0)),
                  pl.BlockSpec((lp, C_KV * C_DH), lambda b, i: (b, kc0)),
                  pl.BlockSpec((lp, C_KV * C_DH), lambda b, i: (b, vc0)),
                  pl.BlockSpec((BLK, half), lambda b, i: (b * nq + i, g0)),
                  pl.BlockSpec((BLK, half), lambda b, i: (b * nq + i, g0 + 1)),
                  pl.BlockSpec((C_KV, 2, rows, BLK), lambda b, i: (0, 0, 0, 0))],
        out_specs=pl.BlockSpec((BLK, C_HEADS * C_DH), lambda b, i: (b * nq + i, 0)),
        out_shape=jax.ShapeDtypeStruct((nb * lp, C_HEADS * C_DH), BF16),
        scratch_shapes=[pltpu.VMEM((lp // CHUNK, CHUNK, BLK), I32),
                        pltpu.VMEM((IDX_HEADS * BLK, LANES), BF16),
                        pltpu.VMEM((C_KV, rows, C_DH), BF16),
                        pltpu.VMEM((C_KV, 2 * SUBS, rows, BLK), F32),
                        pltpu.VMEM((1, BLK), I32),
                        pltpu.VMEM((1, BLK), I32),
                        pltpu.VMEM((C_KV, rows, LANES), F32),
                        pltpu.VMEM((C_KV, rows, 2 * C_DH), F32)],
        compiler_params=_cparams(("parallel", "arbitrary")),
        name="sparse_attn_prompt",
    )(zf, zf, zb, zf, zb, zb, zf, zf, near_tiles)


PAGES_PER_STEP = 32
INDEX_PAGES_PER_STEP = 64
DEC_ROWS = 16


def _interleaved_pages(refs, page):
    halves = [jnp.concatenate([r[pl.ds(h, page, stride=2), :] for r in refs], axis=0) for h in range(2)]
    return jnp.concatenate(halves, axis=1).astype(BF16)


def _paged_attn_kernel(*refs, n_pages_step, masked, page, k_feature_major):
    if masked:
        pt_ref, tau_ref, jmax_ref, selnew_ref = refs[:4]
        refs = refs[4:]
    else:
        pt_ref = refs[0]
        refs = refs[1:]
    q_ref, knew_ref, vnew_ref, bfar_ref, blast_ref, b0_ref = refs[:6]
    refs = refs[6:]
    if masked:
        keys_ref = refs[0]
        refs = refs[1:]
    k_refs = refs[:n_pages_step]
    v_refs = refs[n_pages_step:2 * n_pages_step]
    out_ref, m_ref, l_ref, acc_ref = refs[2 * n_pages_step:]
    del pt_ref
    b = pl.program_id(0)
    j = pl.program_id(1)
    width = n_pages_step * page
    q = q_ref[...]

    @pl.when(j == 0)
    def _():
        s_new = jnp.sum(q.astype(F32) * knew_ref[...].astype(BF16).astype(F32), axis=1, keepdims=True) + b0_ref[...]
        v_new = jnp.broadcast_to(vnew_ref[...].astype(BF16).astype(F32), acc_ref.shape)
        if masked:
            take = selnew_ref[b] > 0
            m_ref[...] = jnp.where(take, s_new, NEG)
            l_ref[...] = jnp.where(take, 1.0, 0.0) * jnp.ones(l_ref.shape, F32)
            acc_ref[...] = jnp.where(take, v_new, 0.0)
        else:
            m_ref[...] = s_new
            l_ref[...] = jnp.ones(l_ref.shape, F32)
            acc_ref[...] = v_new

    if k_feature_major:
        kcat = jnp.concatenate([r[...] for r in k_refs], axis=1).astype(BF16)
        s = jnp.dot(q, kcat, preferred_element_type=F32)
    else:
        s = lax.dot_general(q, _interleaved_pages(k_refs, page), NT, preferred_element_type=F32)
    vcat = _interleaved_pages(v_refs, page)
    s = s + jnp.where(j == pl.num_programs(1) - 1, blast_ref[...], bfar_ref[...])
    if masked:
        key = keys_ref[...]
        idx = j * width + lax.broadcasted_iota(I32, (1, width), 1)
        tau = tau_ref[b]
        sel = (key > tau) | ((key == tau) & (idx <= jmax_ref[b]))
        s = jnp.where(sel, s, NEG)
    m_old = m_ref[...]
    m_new = jnp.maximum(m_old, jnp.max(s, axis=1, keepdims=True))
    alpha = jnp.exp(m_old - m_new)
    p = jnp.exp(s - m_new[:, 0:1])
    l_ref[...] = alpha * l_ref[...] + jnp.sum(p, axis=1, keepdims=True)
    acc_ref[...] = _lane_tile(alpha, 2) * acc_ref[...] + jnp.dot(p.astype(BF16), vcat, preferred_element_type=F32)
    m_ref[...] = m_new

    @pl.when(j == pl.num_programs(1) - 1)
    def _():
        o = acc_ref[...]
        rowi = lax.broadcasted_iota(I32, (DEC_ROWS, LANES), 0)
        upper = (rowi >= DEC_ROWS // 2) if not masked else ((rowi >= DEC_ROWS // 4) & (rowi < DEC_ROWS // 2))
        out_ref[...] = jnp.where(upper, o[:, LANES:], o[:, :LANES]) / l_ref[...]


def _pages_per_step(n_pages, g=PAGES_PER_STEP):
    while n_pages % g:
        g //= 2
    return g


def _paged_attention(qprime, k_new, v_new, bias_far, bias_last, bias0, k_cache, v_cache, page_table, page,
                     k_feature_major, mask_args=None):
    db = qprime.shape[0]
    n_pages = page_table.shape[1]
    width = qprime.shape[2]
    g = _pages_per_step(n_pages)
    n_steps = n_pages // g
    masked = mask_args is not None
    n_pref = 4 if masked else 1

    def page_map(gi):
        return lambda b, j, pt, *_: (pt[b * n_pages + j * g + gi], 0, 0)

    in_specs = [pl.BlockSpec((None, DEC_ROWS, width), lambda b, j, *_: (b, 0, 0)),
                pl.BlockSpec((None, 1, width), lambda b, j, *_: (b, 0, 0)),
                pl.BlockSpec((None, 1, width), lambda b, j, *_: (b, 0, 0)),
                pl.BlockSpec((DEC_ROWS, g * page), lambda b, j, *_: (0, 0)),
                pl.BlockSpec((DEC_ROWS, g * page), lambda b, j, *_: (0, 0)),
                pl.BlockSpec((DEC_ROWS, LANES), lambda b, j, *_: (0, 0))]
    args = [qprime, k_new, v_new, bias_far, bias_last, bias0]
    prefetch = [page_table.reshape(-1)]
    if masked:
        keys, tau, jmax, selnew = mask_args
        prefetch += [tau, jmax, selnew]
        in_specs += [pl.BlockSpec((None, 1, g * page), lambda b, j, *_: (b, 0, j))]
        args += [keys]
    in_specs += [pl.BlockSpec((None,) + k_cache.shape[1:], page_map(gi)) for gi in range(g)]
    in_specs += [pl.BlockSpec((None,) + v_cache.shape[1:], page_map(gi)) for gi in range(g)]
    args += [k_cache] * g + [v_cache] * g
    grid_spec = pltpu.PrefetchScalarGridSpec(
        num_scalar_prefetch=n_pref,
        grid=(db, n_steps),
        in_specs=in_specs,
        out_specs=pl.BlockSpec((None, DEC_ROWS, LANES), lambda b, j, *_: (b, 0, 0)),
        scratch_shapes=[pltpu.VMEM((DEC_ROWS, LANES), F32),
                        pltpu.VMEM((DEC_ROWS, LANES), F32),
                        pltpu.VMEM((DEC_ROWS, width), F32)])
    return pl.pallas_call(
        functools.partial(_paged_attn_kernel, n_pages_step=g, masked=masked, page=page,
                          k_feature_major=k_feature_major),
        grid_spec=grid_spec,
        out_shape=jax.ShapeDtypeStruct((db, DEC_ROWS, LANES), F32),
        compiler_params=_cparams(("parallel", "arbitrary")),
        name="paged_attn_masked" if masked else "paged_attn",
    )(*prefetch, *args)


def _paged_index_kernel(pt_ref, q_ref, w_ref, knew_ref, *refs, n_pages_step, page):
    k_refs = refs[:n_pages_step]
    out_ref = refs[n_pages_step]
    del pt_ref
    j = pl.program_id(1)
    last = pl.num_programs(1) - 1
    q = q_ref[...]
    w = w_ref[...]

    def score(kmat_t):
        d = jnp.dot(q, kmat_t.astype(BF16), preferred_element_type=F32)
        return jnp.sum(jnp.maximum(d, 0.0) * w[:, 0:1], axis=0, keepdims=True)

    @pl.when(j < last)
    def _():
        kcat = jnp.concatenate([r[...] for r in k_refs], axis=1)
        out_ref[...] = _sortable_key(score(kcat))

    @pl.when(j == last)
    def _():
        sc = score(knew_ref[...])
        lane = lax.broadcasted_iota(I32, (1, page), 1)
        sc = jnp.where(lane == 0, sc, -jnp.inf)
        pad = jnp.full((1, (n_pages_step - 1) * page), -jnp.inf, F32)
        full = jnp.concatenate([sc, pad], axis=1) if n_pages_step > 1 else sc
        out_ref[...] = _sortable_key(full)


def _paged_index_scores(qidx, wcol, k_new_tile, idx_cache_t, page_table):
    db = qidx.shape[0]
    n_pages = page_table.shape[1]
    page = idx_cache_t.shape[2]
    g = _pages_per_step(n_pages, INDEX_PAGES_PER_STEP)
    n_steps = n_pages // g

    def page_map(gi):
        return lambda b, j, pt: (pt[b * n_pages + jnp.minimum(j, n_steps - 1) * g + gi], 0, 0)

    grid_spec = pltpu.PrefetchScalarGridSpec(
        num_scalar_prefetch=1,
        grid=(db, n_steps + 1),
        in_specs=[pl.BlockSpec((None, IDX_HEADS, IDX_DIM), lambda b, j, pt: (b, 0, 0)),
                  pl.BlockSpec((None, IDX_HEADS, LANES), lambda b, j, pt: (b, 0, 0)),
                  pl.BlockSpec((None, IDX_DIM, page), lambda b, j, pt: (b, 0, 0))]
        + [pl.BlockSpec((None, IDX_DIM, page), page_map(gi)) for gi in range(g)],
        out_specs=pl.BlockSpec((None, 1, g * page), lambda b, j, pt: (b, 0, j)))
    return pl.pallas_call(
        functools.partial(_paged_index_kernel, n_pages_step=g, page=page),
        grid_spec=grid_spec,
        out_shape=jax.ShapeDtypeStruct((db, 1, (n_steps + 1) * g * page), I32),
        compiler_params=_cparams(("parallel", "arbitrary")),
        name="paged_index_scores",
    )(page_table.reshape(-1), qidx, wcol, k_new_tile, *([idx_cache_t] * g))


def _select_kernel(keys_ref, tau_ref, jmax_ref, *, k_sel, n_index_bits):
    keys = keys_ref[...]
    idx = lax.broadcasted_iota(I32, keys.shape, 1)
    shape = (keys.shape[0], 1)

    def count_fn(pred):
        return jnp.sum(jnp.where(pred(keys, idx), 1, 0), axis=1, keepdims=True)

    tau = _kth_largest(count_fn, k_sel, shape)
    need = k_sel - count_fn(lambda key, i: key > tau)
    jmax = _tie_cutoff(count_fn, tau, need, n_index_bits, shape)
    tau_ref[...] = jnp.broadcast_to(tau, tau_ref.shape)
    jmax_ref[...] = jnp.broadcast_to(jmax, jmax_ref.shape)


def _select_rows(keys2d, k_sel):
    rows, width = keys2d.shape
    n_index_bits = max(1, int(math.ceil(math.log2(width))))
    return pl.pallas_call(
        functools.partial(_select_kernel, k_sel=k_sel, n_index_bits=n_index_bits),
        out_shape=[jax.ShapeDtypeStruct((rows, LANES), I32), jax.ShapeDtypeStruct((rows, LANES), I32)],
        compiler_params=pltpu.CompilerParams(vmem_limit_bytes=VMEM_LIMIT),
        name="select_rows",
    )(keys2d)


def _even_tail_kernel(z_ref, conv_ref, s_ref, oa_ref, convw_ref, gp_ref, normg_ref, lamv_ref, subln_ref,
                      ya_ref, yb_ref, snew_ref, *, lam_init):
    z = z_ref[...]
    hw = B_HEADS * B_DK
    lv = lamv_ref[...]
    lam = (jnp.exp(jnp.sum(lv[0:1] * lv[1:2], axis=1, keepdims=True))
           - jnp.exp(jnp.sum(lv[2:3] * lv[3:4], axis=1, keepdims=True)) + lam_init)
    oa = oa_ref[...]
    for hg in range(A_HEADS):
        o = oa[2 * hg:2 * hg + 1] - lam * oa[2 * hg + 1:2 * hg + 2]
        y = o * lax.rsqrt(jnp.mean(o * o, axis=-1, keepdims=True) + EPS)
        y = (y * subln_ref[...]) * (1.0 - lam_init)
        gate = z[:, EVEN_OFF[3] + hg * A_DV:EVEN_OFF[3] + (hg + 1) * A_DV]
        ya_ref[:, hg * A_DV:(hg + 1) * A_DV] = (y * _silu(gate)).astype(BF16)
    x_new = z[:, EVEN_OFF[4]:EVEN_OFF[4] + B_QKV]
    conv = convw_ref[CONV_W - 1:CONV_W, :] * x_new
    cp = conv_ref[...]
    for i in range(CONV_W - 1):
        conv = conv + convw_ref[i:i + 1, :] * cp[i:i + 1, :]
    act = _silu(conv)
    ab = z[:, EVEN_OFF[6]:EVEN_OFF[6] + LANES]
    gp = gp_ref[...]
    row = lax.broadcasted_iota(I32, (B_DK, B_DV), 0)
    col = lax.broadcasted_iota(I32, (B_DK, B_DV), 1)
    eye = row == col
    for h in range(B_HEADS):
        q = act[:, h * B_DK:(h + 1) * B_DK]
        k = act[:, hw + h * B_DK:hw + (h + 1) * B_DK]
        v = act[:, 2 * hw + h * B_DV:2 * hw + (h + 1) * B_DV]
        q = q * lax.rsqrt(jnp.sum(q * q, axis=-1, keepdims=True) + EPS) * (B_DK ** -0.5)
        k = k * lax.rsqrt(jnp.sum(k * k, axis=-1, keepdims=True) + EPS)
        g = -jnp.exp(gp[0:1, h:h + 1]) * _softplus(ab[:, h:h + 1] + gp[1:2, h:h + 1])
        beta = jax.nn.sigmoid(ab[:, B_HEADS + h:B_HEADS + h + 1])
        eg = jnp.exp(g)
        s0 = s_ref[h]
        kcol = jnp.sum(jnp.where(eye, jnp.broadcast_to(k, (B_DK, B_DK)), 0.0), axis=1, keepdims=True)
        qcol = jnp.sum(jnp.where(eye, jnp.broadcast_to(q, (B_DK, B_DK)), 0.0), axis=1, keepdims=True)
        ks = jnp.sum(kcol * s0, axis=0, keepdims=True)
        qs = jnp.sum(qcol * s0, axis=0, keepdims=True)
        u = beta * (v - eg * ks)
        qk = jnp.sum(q * k, axis=1, keepdims=True)
        o = eg * qs + qk * u
        snew_ref[h] = eg * s0 + kcol * u
        y = o * lax.rsqrt(jnp.mean(o * o, axis=-1, keepdims=True) + EPS) * normg_ref[...]
        gate = z[:, EVEN_OFF[5] + h * B_DV:EVEN_OFF[5] + (h + 1) * B_DV]
        yb_ref[:, h * B_DV:(h + 1) * B_DV] = (y * _silu(gate)).astype(BF16)


def _even_tail(zf_s, conv_prev, s_prev, oa, conv_w, a_log, dt_bias, norm_g, lamv, subln, lam_init):
    db, npad = zf_s.shape
    gp = jnp.zeros((SUBLANES, LANES), F32).at[0, :B_HEADS].set(a_log).at[1, :B_HEADS].set(dt_bias)
    hw = B_HEADS * B_DV
    return pl.pallas_call(
        functools.partial(_even_tail_kernel, lam_init=lam_init),
        grid=(db,),
        in_specs=[pl.BlockSpec((None, 1, npad), lambda b: (b, 0, 0)),
                  pl.BlockSpec((None, CONV_W - 1, B_QKV), lambda b: (b, 0, 0)),
                  pl.BlockSpec((None, B_HEADS, B_DK, B_DV), lambda b: (b, 0, 0, 0)),
                  pl.BlockSpec((None, DEC_ROWS, LANES), lambda b: (b, 0, 0)),
                  pl.BlockSpec((CONV_W, B_QKV), lambda b: (0, 0)),
                  pl.BlockSpec((SUBLANES, LANES), lambda b: (0, 0)),
                  pl.BlockSpec((1, B_DV), lambda b: (0, 0)),
                  pl.BlockSpec((4, A_DH), lambda b: (0, 0)),
                  pl.BlockSpec((1, A_DV), lambda b: (0, 0))],
        out_specs=[pl.BlockSpec((None, 1, A_HEADS * A_DV), lambda b: (b, 0, 0)),
                   pl.BlockSpec((None, 1, hw), lambda b: (b, 0, 0)),
                   pl.BlockSpec((None, B_HEADS, B_DK, B_DV), lambda b: (b, 0, 0, 0))],
        out_shape=[jax.ShapeDtypeStruct((db, 1, A_HEADS * A_DV), BF16),
                   jax.ShapeDtypeStruct((db, 1, hw), BF16),
                   jax.ShapeDtypeStruct((db, B_HEADS, B_DK, B_DV), F32)],
        compiler_params=_cparams(("parallel",)),
        name="even_tail",
    )(zf_s.reshape(db, 1, npad), conv_prev, s_prev, oa, conv_w, gp, norm_g.reshape(1, B_DV), lamv,
      subln.reshape(1, A_DV))


def _odd_tail_kernel(z_ref, oc_ref, y_ref):
    z = z_ref[...]
    oc = oc_ref[...]
    for hg in range(C_HEADS):
        gate = z[:, ODD_OFF[3] + hg * C_DH:ODD_OFF[3] + (hg + 1) * C_DH]
        y_ref[:, hg * C_DH:(hg + 1) * C_DH] = (oc[hg:hg + 1] * _silu(gate)).astype(BF16)


def _odd_tail(zf_s, oc):
    db, npad = zf_s.shape
    return pl.pallas_call(
        _odd_tail_kernel,
        grid=(db,),
        in_specs=[pl.BlockSpec((None, 1, npad), lambda b: (b, 0, 0)),
                  pl.BlockSpec((None, DEC_ROWS, LANES), lambda b: (b, 0, 0))],
        out_specs=pl.BlockSpec((None, 1, C_HEADS * C_DH), lambda b: (b, 0, 0)),
        out_shape=jax.ShapeDtypeStruct((db, 1, C_HEADS * C_DH), BF16),
        compiler_params=_cparams(("parallel",)),
        name="odd_tail",
    )(zf_s.reshape(db, 1, npad), oc)


def _bias_by_distance(table):
    n = jnp.arange(FAR_DIST + 1)
    exact = N_BUCKETS // 2
    nf = jnp.maximum(n, 1).astype(F32)
    large = exact + (jnp.log(nf / exact) / math.log(MAX_DIST / exact) * (N_BUCKETS - exact)).astype(I32)
    bucket = jnp.where(n < exact, n, jnp.minimum(large, N_BUCKETS - 1))
    return table[bucket].astype(F32)


def _prompt_near_tiles(bd, group):
    assert FAR_DIST <= BLK
    heads = bd.shape[1]
    rel = ((bd - bd[FAR_DIST][None, :]) * LOG2E).T
    f = jnp.concatenate([jnp.full((heads, BLK - 1), NEG, F32), rel,
                         jnp.broadcast_to(rel[:, FAR_DIST:], (heads, 2 * BLK - 1 - FAR_DIST))], axis=1)
    period = 3 * BLK
    g = jnp.pad(f[:, ::-1], ((0, 0), (0, period - f.shape[1])))
    wrapped = jnp.tile(g, (1, BLK + 1))[:, :BLK * (period + 1)].reshape(heads, BLK, period + 1)
    strip = wrapped[:, ::-1, :2 * BLK]
    tiles = jnp.stack([strip[:, :, :BLK], strip[:, :, BLK:]], axis=1)
    n_kv = heads // group
    tiles = tiles.reshape(n_kv, group, 2, BLK, BLK)
    return jnp.transpose(tiles, (0, 2, 1, 3, 4)).reshape(n_kv, 2, group * BLK, BLK)


def _decode_bias(bd, row_heads, past, page, g):
    heads = jnp.asarray(row_heads, I32)
    far = jnp.broadcast_to(bd[FAR_DIST][heads][:, None], (len(row_heads), g * page))
    pos = past - g * page + jnp.arange(g * page)
    dist = jnp.minimum(past - pos, FAR_DIST)
    last = bd[dist][:, heads].T
    new = jnp.broadcast_to(bd[0][heads][:, None], (len(row_heads), LANES))
    return far.astype(F32), last.astype(F32), new.astype(F32)


def _pad_cols(w, mult):
    n = w.shape[1]
    return jnp.pad(w, ((0, 0), (0, _round_up(n, mult) - n)))


def kernel(x_prompt, x_sample, cache_a_k, cache_a_v, state_b_s, state_b_conv, cache_c_k, cache_c_v, cache_c_idx,
           page_table, meta, bias_table, final_norm, norm_e, w_in_e, w_out_e, lam_q1, lam_k1, lam_q2, lam_k2,
           subln_a, conv_b, a_log_b, dt_bias_b, norm_b, norm_o, w_in_o, w_out_o):
    nb, seq, d = x_prompt.shape
    n_meta = meta.shape[0]
    l = seq + n_meta
    lp = _round_up(l, CHUNK)
    nq_real = pl.cdiv(l, BLK)
    db = x_sample.shape[0]
    n_pages = page_table.shape[1]
    page = cache_a_k.shape[2]
    past = n_pages * page
    n_pool = cache_a_k.shape[1]
    assert x_sample.shape[1] == 1 and norm_e.shape[0] == 1 and norm_o.shape[0] == 1
    lam_init = 0.8 - 0.6 * math.exp(-0.3 * 0)

    tn_e, tn_o = 8 * LANES, 5 * LANES
    w_e = _pad_cols(w_in_e[0], tn_e).astype(BF16)
    w_o = _pad_cols(w_in_o[0], tn_o).astype(BF16)
    w_out_a = w_out_e[0][:A_HEADS * A_DV].astype(BF16)
    w_out_b = w_out_e[0][A_HEADS * A_DV:].astype(BF16)
    w_out_c = w_out_o[0].astype(BF16)
    tm = CHUNK
    tm_proj = 2 * CHUNK if (nb * lp) % (2 * CHUNK) == 0 else CHUNK
    dbp = _round_up(db, SUBLANES)

    bd = _bias_by_distance(bias_table)
    tiles_a = _prompt_near_tiles(bd, A_GROUP)
    tiles_c = tiles_a if (A_KV, A_GROUP) == (C_KV, C_GROUP) else _prompt_near_tiles(bd, C_GROUP)
    lamv = jnp.stack([lam_q1[0], lam_k1[0], lam_q2[0], lam_k2[0]]).astype(F32)

    hp = jnp.concatenate([jnp.broadcast_to(meta.astype(F32)[None], (nb, n_meta, d)), x_prompt], axis=1)
    hp = jnp.pad(hp, ((0, 0), (0, lp - l), (0, 0))).reshape(nb * lp, d)
    kv_tile = EVEN_OFF[1] // tn_e
    assert EVEN_OFF[3] <= (kv_tile + 1) * tn_e
    zf, zb = _norm_proj(hp, norm_e[0], w_e, tm_proj, tn_e, (kv_tile, kv_tile + 1))
    ya = _diff_attn_prompt(zf, zb, kv_tile * tn_e, tiles_a, lamv, subln_a[0], nb, lp, nq_real, lam_init)
    yb, pb_s = _gdn_prompt(zf, conv_b[0], a_log_b[0], dt_bias_b[0], norm_b[0], nb, lp, l)
    h1 = _out_proj([ya, yb], [w_out_a, w_out_b], hp, tm)
    z3 = zf.reshape(nb, lp, -1)
    pa_k = z3[:, :l, EVEN_OFF[1]:EVEN_OFF[2]].reshape(1, nb, l, A_KV, 2, A_DH)
    pa_v = z3[:, :l, EVEN_OFF[2]:EVEN_OFF[3]].reshape(1, nb, l, A_KV, A_DV)
    pb_conv = z3[:, l - (CONV_W - 1):l, EVEN_OFF[4]:EVEN_OFF[5]][None]

    zf1, zb1 = _norm_proj(h1, norm_o[0], w_o, tm_proj, tn_o, (0, w_o.shape[1] // tn_o))
    k_sel_p = min(TOPK_MAX, l // 4)
    yc = _sparse_attn_prompt(zf1, zb1, tiles_c, nb, lp, nq_real, k_sel_p)
    yp = _out_proj([yc], [w_out_c], h1, tm, final_gain=final_norm)
    y_prompt = yp.reshape(nb, lp, d)[:, n_meta:l]
    z13 = zf1.reshape(nb, lp, -1)
    pc_k = z13[:, :l, ODD_OFF[1]:ODD_OFF[2]].reshape(1, nb, l, C_KV, C_DH)
    pc_v = z13[:, :l, ODD_OFF[2]:ODD_OFF[3]].reshape(1, nb, l, C_KV, C_DH)
    pc_idx = z13[:, :l, ODD_OFF[5]:ODD_OFF[6]][None]

    hs = jnp.pad(x_sample.reshape(db, d), ((0, dbp - db), (0, 0)))
    zs, _ = _norm_proj(hs, norm_e[0], w_e, dbp, tn_e, (0, 1))
    zs = zs[:db]
    qa = zs[:, :EVEN_OFF[1]].reshape(db, A_KV, A_GROUP, 2, A_DH) * (A_DH ** -0.5)
    qprime = jnp.einsum('bhgcd,hi,cj->bhgcijd', qa, jnp.eye(A_KV, dtype=F32), jnp.eye(2, dtype=F32))
    qprime = qprime.reshape(db, DEC_ROWS, A_KV * 2 * A_DH).astype(BF16)
    k_new = zs[:, EVEN_OFF[1]:EVEN_OFF[2]].reshape(db, 1, -1)
    v_new = zs[:, EVEN_OFF[2]:EVEN_OFF[3]].reshape(db, 1, -1)
    g_dec = _pages_per_step(n_pages)
    rows_a = [r // 2 for r in range(DEC_ROWS)]
    bfar, blast, bnew = _decode_bias(bd, rows_a, past, page, g_dec)
    ak_t = jnp.transpose(cache_a_k[0], (0, 2, 3, 4, 1)).reshape(n_pool, A_KV * 2 * A_DH, page)
    av_r = cache_a_v[0].reshape(n_pool, page * A_KV, A_DV)
    oa = _paged_attention(qprime, k_new, v_new, bfar, blast, bnew, ak_t, av_r, page_table, page,
                          k_feature_major=True)
    ya_s, yb_s, sb_s = _even_tail(zs, state_b_conv[0], state_b_s[0], oa, conv_b[0], a_log_b[0], dt_bias_b[0],
                                  norm_b[0], lamv, subln_a[0], lam_init)
    hs_pad = lambda y: jnp.pad(y.reshape(db, -1), ((0, dbp - db), (0, 0)))
    hs1 = _out_proj([hs_pad(ya_s), hs_pad(yb_s)], [w_out_a, w_out_b], hs, dbp)
    sa_k = zs[:, EVEN_OFF[1]:EVEN_OFF[2]].reshape(1, db, 1, A_KV, 2, A_DH)
    sa_v = zs[:, EVEN_OFF[2]:EVEN_OFF[3]].reshape(1, db, 1, A_KV, A_DV)
    sb_conv = jnp.concatenate([state_b_conv[0][:, 1:], zs[:, None, EVEN_OFF[4]:EVEN_OFF[5]]], axis=1)[None]

    zs1, _ = _norm_proj(hs1, norm_o[0], w_o, dbp, tn_o, (0, 1))
    zs1 = zs1[:db]
    qidx = (zs1[:, ODD_OFF[4]:ODD_OFF[5]].reshape(db, IDX_HEADS, IDX_DIM) * (IDX_DIM ** -0.5)).astype(BF16)
    wcol = jnp.broadcast_to((zs1[:, ODD_OFF[6]:ODD_OFF[7]] * (IDX_HEADS ** -0.5))[:, :, None],
                            (db, IDX_HEADS, LANES))
    ki_new = zs1[:, ODD_OFF[5]:ODD_OFF[6]]
    ki_tile = jnp.zeros((db, IDX_DIM, page), F32).at[:, :, 0].set(ki_new)
    ci_t = jnp.transpose(cache_c_idx[0], (0, 2, 1))
    keys = _paged_index_scores(qidx, wcol, ki_tile, ci_t, page_table)
    k_sel_s = min(TOPK_MAX, (past + 1) // 4)
    tau, jmax = _select_rows(keys.reshape(db, -1), k_sel_s)
    tau, jmax = tau[:, 0], jmax[:, 0]
    key_new = keys[:, 0, past]
    selnew = ((key_new > tau) | ((key_new == tau) & (past <= jmax))).astype(I32)
    qc = zs1[:, :ODD_OFF[1]].reshape(db, C_KV, C_GROUP, C_DH) * (C_DH ** -0.5)
    qcp = jnp.einsum('bhgd,hi->bhgid', qc, jnp.eye(C_KV, dtype=F32)).reshape(db, C_HEADS, C_KV * C_DH)
    qcp = jnp.pad(qcp, ((0, 0), (0, DEC_ROWS - C_HEADS), (0, 0))).astype(BF16)
    kc_new = zs1[:, ODD_OFF[1]:ODD_OFF[2]].reshape(db, 1, -1)
    vc_new = zs1[:, ODD_OFF[2]:ODD_OFF[3]].reshape(db, 1, -1)
    rows_c = [r if r < C_HEADS else 0 for r in range(DEC_ROWS)]
    cfar, clast, cnew = _decode_bias(bd, rows_c, past, page, g_dec)
    ck_r = cache_c_k[0].reshape(n_pool, page * C_KV, C_DH)
    cv_r = cache_c_v[0].reshape(n_pool, page * C_KV, C_DH)
    oc = _paged_attention(qcp, kc_new, vc_new, cfar, clast, cnew, ck_r, cv_r, page_table, page,
                          k_feature_major=False, mask_args=(keys, tau, jmax, selnew))
    yc_s = _odd_tail(zs1, oc)
    ys = _out_proj([hs_pad(yc_s)], [w_out_c], hs1, dbp, final_gain=final_norm)
    y_sample = ys[:db].reshape(db, 1, d)
    sc_k = zs1[:, ODD_OFF[1]:ODD_OFF[2]].reshape(1, db, 1, C_KV, C_DH)
    sc_v = zs1[:, ODD_OFF[2]:ODD_OFF[3]].reshape(1, db, 1, C_KV, C_DH)
    sc_idx = zs1[:, None, ODD_OFF[5]:ODD_OFF[6]][None]

    return (y_prompt, y_sample, pa_k, pa_v, pb_s[None], pb_conv, pc_k, pc_v, pc_idx,
            sa_k, sa_v, sb_s[None], sb_conv, sc_k, sc_v, sc_idx)
```

```python
import functools
import math

import jax
import jax.numpy as jnp
import numpy as np
from jax import lax
from jax.experimental import pallas as pl
from jax.experimental.pallas import tpu as pltpu

F32 = jnp.float32
BF16 = jnp.bfloat16
I32 = jnp.int32

EPS = 1e-6
N_BUCKETS = 32
MAX_DIST = 128
FAR_DIST = MAX_DIST

A_HEADS, A_KV, A_GROUP, A_DH, A_DV = 8, 2, 4, 64, 128
B_HEADS, B_DK, B_DV, CONV_W = 4, 128, 128, 4
C_HEADS, C_KV, C_GROUP, C_DH = 8, 2, 4, 128
IDX_HEADS, IDX_DIM, TOPK_MAX = 8, 64, 256
B_QKV = 2 * B_HEADS * B_DK + B_HEADS * B_DV

EVEN_COLS = (A_HEADS * 2 * A_DH, A_KV * 2 * A_DH, A_KV * A_DV, A_HEADS * A_DV, B_QKV, B_HEADS * B_DV, B_HEADS, B_HEADS)
ODD_COLS = (C_HEADS * C_DH, C_KV * C_DH, C_KV * C_DH, C_HEADS * C_DH, IDX_HEADS * IDX_DIM, IDX_DIM, IDX_HEADS)
EVEN_OFF = tuple(int(v) for v in np.cumsum((0,) + EVEN_COLS))
ODD_OFF = tuple(int(v) for v in np.cumsum((0,) + ODD_COLS))

LANES = 128
SUBLANES = 8
BLK = 128
CHUNK = 512
SUBS = CHUNK // BLK
NEG = -1e30
INT_MIN = -2 ** 31
INT_MAX = 2 ** 31 - 1
LOG2E = math.log2(math.e)
VMEM_LIMIT = 56 * 1024 * 1024
HI = lax.Precision.HIGHEST

NT = (((1,), (1,)), ((), ()))


def _round_up(x, m):
    return (x + m - 1) // m * m


def _cparams(sem):
    return pltpu.CompilerParams(dimension_semantics=sem, vmem_limit_bytes=VMEM_LIMIT)


def _silu(x):
    return x * jax.nn.sigmoid(x)


def _sortable_key(score):
    score = jnp.where(score == 0.0, 0.0, score)
    bits = pltpu.bitcast(score, I32)
    return bits ^ ((bits >> 31) & jnp.int32(0x7FFFFFFF))


def _lane_tile(x, n):
    return jnp.concatenate([x] * n, axis=1)


def _proj_kernel(x_ref, g_ref, w_ref, of_ref, ob_ref, xn_ref, *, bf16_tiles):
    j = pl.program_id(1)

    @pl.when(j == 0)
    def _():
        xf = x_ref[...]
        y = xf * lax.rsqrt(jnp.mean(xf * xf, axis=-1, keepdims=True) + EPS)
        xn_ref[...] = (y * g_ref[...]).astype(BF16)

    acc = jnp.dot(xn_ref[...], w_ref[...], preferred_element_type=F32)
    of_ref[...] = acc

    @pl.when((j >= bf16_tiles[0]) & (j < bf16_tiles[1]))
    def _():
        ob_ref[...] = acc.astype(BF16)


def _norm_proj(x, gain, w_bf16, tm, tn, bf16_tiles):
    m, d = x.shape
    n = w_bf16.shape[1]
    t0, t1 = bf16_tiles
    return pl.pallas_call(
        functools.partial(_proj_kernel, bf16_tiles=bf16_tiles),
        grid=(m // tm, n // tn),
        in_specs=[pl.BlockSpec((tm, d), lambda i, j: (i, 0)),
                  pl.BlockSpec((1, d), lambda i, j: (0, 0)),
                  pl.BlockSpec((d, tn), lambda i, j: (0, j))],
        out_specs=[pl.BlockSpec((tm, tn), lambda i, j: (i, j)),
                   pl.BlockSpec((tm, tn), lambda i, j: (i, jnp.clip(j - t0, 0, t1 - t0 - 1)))],
        out_shape=[jax.ShapeDtypeStruct((m, n), F32), jax.ShapeDtypeStruct((m, (t1 - t0) * tn), BF16)],
        scratch_shapes=[pltpu.VMEM((tm, d), BF16)],
        compiler_params=_cparams(("parallel", "arbitrary")),
        name="norm_proj",
    )(x, gain.reshape(1, d), w_bf16)


def _outproj_kernel(*refs, n_lhs, final_norm):
    ys = refs[:n_lhs]
    ws = refs[n_lhs:2 * n_lhs]
    h_ref = refs[2 * n_lhs]
    pos = 2 * n_lhs + 1
    fn_ref = refs[pos] if final_norm else None
    out_ref = refs[-1]
    acc = h_ref[...]
    for y_ref, w_ref in zip(ys, ws):
        acc = acc + jnp.dot(y_ref[...], w_ref[...], preferred_element_type=F32)
    if final_norm:
        y = acc * lax.rsqrt(jnp.mean(acc * acc, axis=-1, keepdims=True) + EPS)
        acc = y * fn_ref[...]
    out_ref[...] = acc


def _out_proj(ys, ws, h, tm, final_gain=None):
    m, n = h.shape
    n_lhs = len(ys)
    in_specs = [pl.BlockSpec((tm, y.shape[1]), lambda i: (i, 0)) for y in ys]
    in_specs += [pl.BlockSpec(w.shape, lambda i: (0, 0)) for w in ws]
    in_specs += [pl.BlockSpec((tm, n), lambda i: (i, 0))]
    args = list(ys) + list(ws) + [h]
    if final_gain is not None:
        in_specs += [pl.BlockSpec((1, n), lambda i: (0, 0))]
        args += [final_gain.reshape(1, n)]
    return pl.pallas_call(
        functools.partial(_outproj_kernel, n_lhs=n_lhs, final_norm=final_gain is not None),
        grid=(m // tm,),
        in_specs=in_specs,
        out_specs=pl.BlockSpec((tm, n), lambda i: (i, 0)),
        out_shape=jax.ShapeDtypeStruct((m, n), F32),
        compiler_params=_cparams(("parallel",)),
        name="out_proj",
    )(*args)


def _flash_steps(ss, v1s, m_refs):
    each = range(len(ss))
    m_old = [m_refs[i][0][m_refs[i][2]] for i in each]
    m_new = [jnp.maximum(m_old[i], jnp.max(ss[i], axis=1, keepdims=True)) for i in each]
    alpha = [jnp.exp2(m_old[i] - m_new[i]) for i in each]
    p = [jnp.exp2(ss[i] - _lane_tile(m_new[i], ss[i].shape[1] // LANES)).astype(BF16) for i in each]
    pv = [jnp.dot(p[i], v1s[i], preferred_element_type=F32) for i in each]
    for i in each:
        m_ref, acc_ref, idx = m_refs[i]
        acc_ref[idx] = _lane_tile(alpha[i], 2) * acc_ref[idx] + pv[i]
        m_ref[idx] = m_new[i]


def _far_chunk_loop(n_far, run_chunks):
    def quad(i, carry):
        run_chunks([4 * i, 4 * i + 1, 4 * i + 2, 4 * i + 3])
        return carry

    n_quad = lax.shift_right_logical(n_far, 2)
    lax.fori_loop(0, n_quad, quad, 0)

    @pl.when(jnp.bitwise_and(n_far, 2) == 2)
    def _():
        run_chunks([4 * n_quad, 4 * n_quad + 1])

    @pl.when(jnp.bitwise_and(n_far, 1) == 1)
    def _():
        run_chunks([n_far - 1])


def _build_near_bias(strip_ref, t_prev, t_diag, r, lead=()):
    for sb in range(2 * SUBS):
        rel = sb - SUBS - r
        tile = jnp.where(rel == 0, t_diag, jnp.where(rel == -1, t_prev, jnp.where(rel < -1, 0.0, NEG)))
        strip_ref[lead + (sb,)] = tile


def _diff_attn_kernel(q_ref, k_ref, v_ref, gate_ref, tiles_ref, lamv_ref, subln_ref, out_ref,
                      qm_ref, strip_ref, m_ref, acc_ref, *, lam_init, nq_real):
    qi = pl.program_id(2)

    @pl.when(qi >= nq_real)
    def _():
        out_ref[...] = jnp.zeros(out_ref.shape, out_ref.dtype)

    @pl.when(qi < nq_real)
    def _():
        lane = lax.broadcasted_iota(I32, (BLK, 2 * A_DH), 1)
        for g in range(A_GROUP):
            qg = q_ref[:, g * 2 * A_DH:(g + 1) * 2 * A_DH] * (A_DH ** -0.5 * LOG2E)
            for c in range(2):
                keep = (lane < A_DH) if c == 0 else (lane >= A_DH)
                qm_ref[c, g * BLK:(g + 1) * BLK, :] = jnp.where(keep, qg, 0.0).astype(BF16)
        m_ref[...] = jnp.full(m_ref.shape, NEG, F32)
        acc_ref[...] = jnp.zeros(acc_ref.shape, F32)
        r = jnp.bitwise_and(qi, SUBS - 1)
        jd = lax.shift_right_logical(qi, int(math.log2(SUBS)))
        _build_near_bias(strip_ref, tiles_ref[0], tiles_ref[1], r)
        ones = jnp.ones((CHUNK, A_DV), BF16)

        def run_chunks(js, first_sbs=None):
            starts = [pl.multiple_of(j * CHUNK, CHUNK) for j in js]
            k = jnp.concatenate([k_ref[pl.ds(st, CHUNK), :] for st in starts], axis=0)
            v1 = jnp.concatenate(
                [jnp.concatenate([v_ref[pl.ds(st, CHUNK), :], ones], axis=1) for st in starts], axis=0)
            ss = [lax.dot_general(qm_ref[c], k, NT, preferred_element_type=F32) for c in range(2)]
            if first_sbs is not None:
                bias = jnp.concatenate([strip_ref[sb + i] for sb in first_sbs for i in range(SUBS)], axis=1)
                ss = [s + bias for s in ss]
            _flash_steps(ss, [v1, v1], [(m_ref, acc_ref, c) for c in range(2)])

        _far_chunk_loop(jnp.maximum(jd - 1, 0), run_chunks)

        @pl.when(jd >= 1)
        def _():
            run_chunks([jd - 1, jd], [0, SUBS])

        @pl.when(jd == 0)
        def _():
            run_chunks([jd], [SUBS])

        lv = lamv_ref[...]
        lam = (jnp.exp(jnp.sum(lv[0:1] * lv[1:2], axis=1, keepdims=True))
               - jnp.exp(jnp.sum(lv[2:3] * lv[3:4], axis=1, keepdims=True)) + lam_init)
        a0 = acc_ref[0]
        a1 = acc_ref[1]
        o = a0[:, :A_DV] / a0[:, A_DV:] - lam * (a1[:, :A_DV] / a1[:, A_DV:])
        y = o * lax.rsqrt(jnp.mean(o * o, axis=-1, keepdims=True) + EPS)
        y = (y * subln_ref[...]) * (1.0 - lam_init)
        for g in range(A_GROUP):
            gate = gate_ref[:, g * A_DV:(g + 1) * A_DV]
            out_ref[:, g * A_DV:(g + 1) * A_DV] = (y[g * BLK:(g + 1) * BLK] * _silu(gate)).astype(BF16)


def _diff_attn_prompt(zf, zb, zb_col0, near_tiles, lamv, subln, nb, lp, nq_real, lam_init):
    nq = lp // BLK
    qw = A_GROUP * 2 * A_DH
    k_blk0 = (EVEN_OFF[1] - zb_col0) // (2 * A_DH)
    v_blk0 = (EVEN_OFF[2] - zb_col0) // A_DV
    g_blk0 = EVEN_OFF[3] // (A_GROUP * A_DV)
    rows = A_GROUP * BLK
    return pl.pallas_call(
        functools.partial(_diff_attn_kernel, lam_init=lam_init, nq_real=nq_real),
        grid=(nb, A_KV, nq),
        in_specs=[pl.BlockSpec((BLK, qw), lambda b, h, i: (b * nq + i, h)),
                  pl.BlockSpec((lp, 2 * A_DH), lambda b, h, i: (b, k_blk0 + h)),
                  pl.BlockSpec((lp, A_DV), lambda b, h, i: (b, v_blk0 + h)),
                  pl.BlockSpec((BLK, A_GROUP * A_DV), lambda b, h, i: (b * nq + i, g_blk0 + h)),
                  pl.BlockSpec((None, 2, rows, BLK), lambda b, h, i: (h, 0, 0, 0)),
                  pl.BlockSpec((4, A_DH), lambda b, h, i: (0, 0)),
                  pl.BlockSpec((1, A_DV), lambda b, h, i: (0, 0))],
        out_specs=pl.BlockSpec((BLK, A_GROUP * A_DV), lambda b, h, i: (b * nq + i, h)),
        out_shape=jax.ShapeDtypeStruct((nb * lp, A_HEADS * A_DV), BF16),
        scratch_shapes=[pltpu.VMEM((2, rows, 2 * A_DH), BF16),
                        pltpu.VMEM((2 * SUBS, rows, BLK), F32),
                        pltpu.VMEM((2, rows, LANES), F32),
                        pltpu.VMEM((2, rows, 2 * A_DV), F32)],
        compiler_params=_cparams(("parallel", "parallel", "arbitrary")),
        name="diff_attn_prompt",
    )(zf, zb, zb, zf, near_tiles, lamv, subln.reshape(1, A_DV))


def _softplus(x):
    return jnp.maximum(x, 0.0) + jnp.log1p(jnp.exp(-jnp.abs(x)))


def _split_bf16(a):
    hi = a.astype(BF16)
    return hi, (a - hi.astype(F32)).astype(BF16)


def _dot3(a, b):
    (ah, al), (bh, bl) = a, b
    return (jnp.dot(ah, bh, preferred_element_type=F32)
            + (jnp.dot(ah, bl, preferred_element_type=F32) + jnp.dot(al, bh, preferred_element_type=F32)))


def _gdn_chunk_math(s0, q, k, v, gb, betab, row_ge, row_gt):
    n = len(q)
    c = q[0].shape[0]
    each = range(n)
    dot = functools.partial(jnp.dot, preferred_element_type=F32)
    dot_nt = functools.partial(lax.dot_general, dimension_numbers=NT, preferred_element_type=F32)
    ltri = jnp.where(row_ge, 1.0, 0.0).astype(F32)
    gcum = [jnp.dot(ltri, gb[i], precision=HI, preferred_element_type=F32) for i in each]
    decay = [jnp.where(row_ge, jnp.exp(jnp.where(row_ge, gcum[i] - gcum[i].T, 0.0)), 0.0) for i in each]
    kb = [k[i].astype(BF16) for i in each]
    qb = [q[i].astype(BF16) for i in each]
    s0b = [s0[i].astype(BF16) for i in each]
    kk = [dot_nt(kb[i], kb[i]) for i in each]
    x = [-jnp.where(row_gt, betab[i] * decay[i] * kk[i], 0.0) for i in each]
    tm = list(x)
    p = list(x)
    for _ in range(int(math.log2(c)) - 1):
        ps = [_split_bf16(p[i]) for i in each]
        p = [_dot3(ps[i], ps[i]) for i in each]
        ps = [_split_bf16(p[i]) for i in each]
        tms = [_split_bf16(tm[i]) for i in each]
        tm = [tm[i] + p[i] + _dot3(tms[i], ps[i]) for i in each]
    eg = [jnp.exp(gcum[i]) for i in each]
    ks = [dot(kb[i], s0b[i]) for i in each]
    rhs = [betab[i] * (v[i] - eg[i] * ks[i]) for i in each]
    u = [rhs[i] + dot(tm[i].astype(BF16), rhs[i].astype(BF16)) for i in each]
    ub = [u[i].astype(BF16) for i in each]
    qk = [dot_nt(qb[i], kb[i]) * decay[i] for i in each]
    o = [eg[i] * dot(qb[i], s0b[i]) + dot(qk[i].astype(BF16), ub[i]) for i in each]
    glast = [gcum[i][c - 1:c, :] for i in each]
    kd = [k[i] * jnp.exp(glast[i] - gcum[i]) for i in each]
    s_new = [jnp.exp(glast[i]) * s0[i] + dot(kd[i].T.astype(BF16), ub[i]) for i in each]
    return s_new, o


def _gdn_prompt_kernel(xq_ref, xk_ref, xv_ref, gate_ref, ab_ref, convw_ref, gp_ref, normg_ref,
                       y_ref, sfin_ref, xbuf_ref, s_ref, *, seq_len):
    ci = pl.program_id(0)
    hw = B_HEADS * B_DK
    nb = xq_ref.shape[0]

    @pl.when(ci == 0)
    def _():
        xbuf_ref[:, 0:SUBLANES, :] = jnp.zeros((nb, SUBLANES, B_QKV), F32)
        s_ref[...] = jnp.zeros(s_ref.shape, F32)

    row = lax.broadcasted_iota(I32, (BLK, BLK), 0)
    col = lax.broadcasted_iota(I32, (BLK, BLK), 1)
    row_ge = row >= col
    row_gt = row > col
    valid = (ci * BLK + row) < seq_len
    gp = gp_ref[...]
    chains = [(b, h) for b in range(nb) for h in range(B_HEADS)]
    qs, ks, vs, gbs, betabs = [], [], [], [], []
    for b in range(nb):
        xbuf_ref[b, SUBLANES:SUBLANES + BLK, 0:hw] = xq_ref[b]
        xbuf_ref[b, SUBLANES:SUBLANES + BLK, hw:2 * hw] = xk_ref[b]
        xbuf_ref[b, SUBLANES:SUBLANES + BLK, 2 * hw:3 * hw] = xv_ref[b]
        conv = jnp.zeros((BLK, B_QKV), F32)
        for i in range(CONV_W):
            conv = conv + convw_ref[i:i + 1, :] * xbuf_ref[b, pl.ds(SUBLANES - (CONV_W - 1) + i, BLK), :]
        tail = xbuf_ref[b, BLK:BLK + SUBLANES, :]
        xbuf_ref[b, 0:SUBLANES, :] = tail
        act = _silu(conv)
        ab = ab_ref[b]
        for h in range(B_HEADS):
            q = act[:, h * B_DK:(h + 1) * B_DK]
            k = act[:, hw + h * B_DK:hw + (h + 1) * B_DK]
            v = act[:, 2 * hw + h * B_DV:2 * hw + (h + 1) * B_DV]
            q = q * lax.rsqrt(jnp.sum(q * q, axis=-1, keepdims=True) + EPS) * (B_DK ** -0.5)
            k = k * lax.rsqrt(jnp.sum(k * k, axis=-1, keepdims=True) + EPS)
            a_raw = jnp.broadcast_to(ab[:, h:h + 1], (BLK, BLK))
            b_raw = jnp.broadcast_to(ab[:, B_HEADS + h:B_HEADS + h + 1], (BLK, BLK))
            a_log = gp[0:1, h:h + 1]
            dt_b = gp[1:2, h:h + 1]
            qs.append(q)
            ks.append(k)
            vs.append(v)
            gbs.append(jnp.where(valid, -jnp.exp(a_log) * _softplus(a_raw + dt_b), 0.0))
            betabs.append(jnp.where(valid, jax.nn.sigmoid(b_raw), 0.0))
    s_new, outs = _gdn_chunk_math([s_ref[b, h] for b, h in chains], qs, ks, vs, gbs, betabs, row_ge, row_gt)
    for (b, h), s_bh, o in zip(chains, s_new, outs):
        s_ref[b, h] = s_bh
        y = o * lax.rsqrt(jnp.mean(o * o, axis=-1, keepdims=True) + EPS) * normg_ref[...]
        gate = gate_ref[b, :, h * B_DV:(h + 1) * B_DV]
        y_ref[b, :, h * B_DV:(h + 1) * B_DV] = (y * _silu(gate)).astype(BF16)

    @pl.when(ci == pl.num_programs(0) - 1)
    def _():
        sfin_ref[...] = s_ref[...]


def _gdn_prompt(zf, conv_w, a_log, dt_bias, norm_g, nb, lp, seq_len):
    nc = lp // BLK
    hw = B_HEADS * B_DK
    c0 = EVEN_OFF[4] // hw
    g0 = EVEN_OFF[5] // hw
    ab0 = EVEN_OFF[6] // LANES
    gp = jnp.zeros((SUBLANES, LANES), F32).at[0, :B_HEADS].set(a_log).at[1, :B_HEADS].set(dt_bias)
    z3 = zf.reshape(nb, lp, zf.shape[1])
    y, s_fin = pl.pallas_call(
        functools.partial(_gdn_prompt_kernel, seq_len=seq_len),
        grid=(nc,),
        in_specs=[pl.BlockSpec((nb, BLK, hw), lambda i: (0, i, c0)),
                  pl.BlockSpec((nb, BLK, hw), lambda i: (0, i, c0 + 1)),
                  pl.BlockSpec((nb, BLK, hw), lambda i: (0, i, c0 + 2)),
                  pl.BlockSpec((nb, BLK, hw), lambda i: (0, i, g0)),
                  pl.BlockSpec((nb, BLK, LANES), lambda i: (0, i, ab0)),
                  pl.BlockSpec((CONV_W, B_QKV), lambda i: (0, 0)),
                  pl.BlockSpec((SUBLANES, LANES), lambda i: (0, 0)),
                  pl.BlockSpec((1, B_DV), lambda i: (0, 0))],
        out_specs=[pl.BlockSpec((nb, BLK, hw), lambda i: (0, i, 0)),
                   pl.BlockSpec((nb, B_HEADS, B_DK, B_DV), lambda i: (0, 0, 0, 0))],
        out_shape=[jax.ShapeDtypeStruct((nb, lp, hw), BF16),
                   jax.ShapeDtypeStruct((nb, B_HEADS, B_DK, B_DV), F32)],
        scratch_shapes=[pltpu.VMEM((nb, BLK + SUBLANES, B_QKV), F32),
                        pltpu.VMEM((nb, B_HEADS, B_DK, B_DV), F32)],
        compiler_params=_cparams(("arbitrary",)),
        name="gdn_prompt",
    )(z3, z3, z3, z3, z3, conv_w, gp, norm_g.reshape(1, B_DV))
    return y.reshape(nb * lp, hw), s_fin


def _kth_largest(count_fn, k_sel, shape):
    def step(i, tau):
        cand = tau + lax.shift_left(jnp.int32(1), 31 - i)
        cnt = count_fn(lambda key, idx: key >= cand)
        return jnp.where(cnt >= k_sel, cand, tau)

    return lax.fori_loop(0, 32, step, jnp.full(shape, INT_MIN, I32))


HEAD_BITS = 9
STEPS_PER_CHECK = 3


def _kth_separator(count_fn, k_sel, n_keys):
    def head(i, c):
        lo, c_lo, c_hi = c
        cand = lo + lax.shift_left(jnp.int32(1), 31 - i)
        cnt = count_fn(lambda key, idx: key >= cand)
        take = cnt >= k_sel
        return jnp.where(take, cand, lo), jnp.where(take, cnt, c_lo), jnp.where(take, c_hi, cnt)

    zeros = jnp.zeros(n_keys.shape, I32)
    lo, c_lo, c_hi = lax.fori_loop(0, HEAD_BITS, head, (jnp.full(n_keys.shape, INT_MIN, I32), n_keys, zeros))
    width = jnp.full(n_keys.shape, 1 << (32 - HEAD_BITS), I32)

    def pending(c_lo, width):
        return jnp.max(jnp.where((c_lo > k_sel) & (width > 1), 1.0, 0.0)) > 0.5

    def steps(c):
        lo, width, c_lo, c_hi, _ = c
        for s in range(STEPS_PER_CHECK):
            if s == STEPS_PER_CHECK - 1:
                off = lax.shift_right_logical(width, 1)
            else:
                frac = (c_lo - k_sel).astype(F32) + 0.5
                frac = frac / jnp.maximum(c_lo - c_hi, 1).astype(F32)
                off = jnp.round(frac * width.astype(F32)).astype(I32)
            off = jnp.clip(off, 1, jnp.maximum(width - 1, 1))
            cand = lo + off
            cnt = count_fn(lambda key, idx: key >= cand)
            take = cnt >= k_sel
            lo = jnp.where(take, cand, lo)
            c_lo = jnp.where(take, cnt, c_lo)
            c_hi = jnp.where(take, c_hi, cnt)
            width = jnp.where(take, width - off, off)
        return lo, width, c_lo, c_hi, pending(c_lo, width)

    out = lax.while_loop(lambda c: c[4], steps, (lo, width, c_lo, c_hi, pending(c_lo, width)))
    return out[0]


def _tie_cutoff(count_fn, tau, need, n_index_bits, shape):
    def step(i, jm):
        cand = jm | lax.shift_left(jnp.int32(1), n_index_bits - 1 - i)
        cnt = count_fn(lambda key, idx: (key == tau) & (idx < cand))
        return jnp.where(cnt < need, cand, jm)

    return lax.fori_loop(0, n_index_bits, step, jnp.zeros(shape, I32))


def _sparse_attn_kernel(qi_ref, kw_ref, kidx_ref, qc_ref, kc_ref, vc_ref, g0_ref, g1_ref, tiles_ref, out_ref,
                        keys_ref, qm_ref, qh_ref, strip_ref, tau_ref, jmax_ref, m_ref, acc_ref,
                        *, k_sel, n_index_bits, nq_real):
    qb = pl.program_id(1)

    @pl.when(qb >= nq_real)
    def _():
        out_ref[...] = jnp.zeros(out_ref.shape, out_ref.dtype)

    @pl.when(qb < nq_real)
    def _():
        lane = lax.broadcasted_iota(I32, (BLK, LANES), 1)
        key_pos = lax.broadcasted_iota(I32, (CHUNK, BLK), 0)
        qry_pos = lax.broadcasted_iota(I32, (CHUNK, BLK), 1)
        r = jnp.bitwise_and(qb, SUBS - 1)
        jd = lax.shift_right_logical(qb, int(math.log2(SUBS)))

        for p in range(IDX_HEADS // 2):
            pair = qi_ref[:, p * LANES:(p + 1) * LANES] * (IDX_DIM ** -0.5)
            swapped = pltpu.roll(pair, IDX_DIM, 1)
            for e, src in enumerate((pair, swapped)):
                h = 2 * p + e
                qm_ref[h * BLK:(h + 1) * BLK, :] = jnp.where(lane < IDX_DIM, src, 0.0).astype(BF16)
        w_rows = kw_ref[...].T * (IDX_HEADS ** -0.5)

        def scores_t(j):
            start = pl.multiple_of(j * CHUNK, CHUNK)
            d = lax.dot_general(kidx_ref[pl.ds(start, CHUNK), :], qm_ref[...], NT, preferred_element_type=F32)
            sc = jnp.zeros((CHUNK, BLK), F32)
            for h in range(IDX_HEADS):
                sc = sc + w_rows[IDX_DIM + h:IDX_DIM + h + 1, :] * jnp.maximum(d[:, h * BLK:(h + 1) * BLK], 0.0)
            return sc

        def score_body(j, carry):
            keys_ref[j] = _sortable_key(scores_t(j))
            return carry

        lax.fori_loop(0, jd, score_body, 0)
        admissible = (jd * CHUNK + key_pos) <= (qb * BLK + qry_pos)
        keys_ref[jd] = _sortable_key(jnp.where(admissible, scores_t(jd), -jnp.inf))

        def count_fn(pred):
            part = SUBLANES * SUBLANES

            def body(j, acc):
                hit = jnp.where(pred(keys_ref[j], j * CHUNK + key_pos), 1.0, 0.0)
                return acc + jnp.sum(hit.reshape(CHUNK // part, part, BLK), axis=0)
            acc = lax.fori_loop(0, jd + 1, body, jnp.zeros((part, BLK), F32))
            return jnp.sum(acc, axis=0, keepdims=True).astype(I32)

        n_keys = qb * BLK + lax.broadcasted_iota(I32, (1, BLK), 1) + 1
        tau = _kth_separator(count_fn, k_sel, n_keys)
        tau_ref[...] = tau
        jmax_ref[...] = jnp.full((1, BLK), INT_MAX, I32)
        need = k_sel - count_fn(lambda key, idx: key > tau)
        n_eq = count_fn(lambda key, idx: key == tau)
        excess = jnp.max(jnp.where(n_eq > need, 1.0, 0.0)) > 0.5

        @pl.when(excess)
        def _():
            jmax_ref[...] = _tie_cutoff(count_fn, tau, need, n_index_bits, (1, BLK))

        jmax = jmax_ref[...]

        for h in range(C_KV):
            for g in range(C_GROUP):
                col = (h * C_GROUP + g) * C_DH
                qh_ref[h, g * BLK:(g + 1) * BLK, :] = (
                    qc_ref[:, col:col + C_DH] * (C_DH ** -0.5 * LOG2E)).astype(BF16)
            _build_near_bias(strip_ref, tiles_ref[h, 0], tiles_ref[h, 1], r, lead=(h,))
        m_ref[...] = jnp.full(m_ref.shape, NEG, F32)
        acc_ref[...] = jnp.zeros(acc_ref.shape, F32)
        ones = jnp.ones((CHUNK, C_DH), BF16)

        def run_chunks(js, first_sbs=None):
            starts = [pl.multiple_of(j * CHUNK, CHUNK) for j in js]
            masks = []
            for j in js:
                key = keys_ref[j]
                tie_ok = jnp.where((j * CHUNK + key_pos) <= jmax, 0.0, NEG)
                sel_t = jnp.where(key > tau, 0.0, jnp.where(key == tau, tie_ok, NEG))
                masks += [sel_t[i * BLK:(i + 1) * BLK, :].T for i in range(SUBS)]
            selneg = jnp.concatenate(masks, axis=1)
            selneg = jnp.concatenate([selneg] * C_GROUP, axis=0)
            ss, v1s = [], []
            for h in range(C_KV):
                cols = slice(h * C_DH, (h + 1) * C_DH)
                k = jnp.concatenate([kc_ref[pl.ds(st, CHUNK), cols] for st in starts], axis=0)
                v1s.append(jnp.concatenate(
                    [jnp.concatenate([vc_ref[pl.ds(st, CHUNK), cols], ones], axis=1) for st in starts], axis=0))
                s = lax.dot_general(qh_ref[h], k, NT, preferred_element_type=F32) + selneg
                if first_sbs is not None:
                    s = s + jnp.concatenate(
                        [strip_ref[h, sb + i] for sb in first_sbs for i in range(SUBS)], axis=1)
                ss.append(s)
            _flash_steps(ss, v1s, [(m_ref, acc_ref, h) for h in range(C_KV)])

        _far_chunk_loop(jnp.maximum(jd - 1, 0), run_chunks)

        @pl.when(jd >= 1)
        def _():
            run_chunks([jd - 1, jd], [0, SUBS])

        @pl.when(jd == 0)
        def _():
            run_chunks([jd], [SUBS])

        half = (C_HEADS // 2) * C_DH
        for h in range(C_KV):
            a = acc_ref[h]
            o = a[:, :C_DH] / a[:, C_DH:]
            gref = g0_ref if h == 0 else g1_ref
            for g in range(C_GROUP):
                gate = gref[:, g * C_DH:(g + 1) * C_DH]
                out_ref[:, h * half + g * C_DH:h * half + (g + 1) * C_DH] = (
                    o[g * BLK:(g + 1) * BLK] * _silu(gate)).astype(BF16)


def _sparse_attn_prompt(zf, zb, near_tiles, nb, lp, nq_real, k_sel):
    nq = lp // BLK
    half = (C_HEADS // 2) * C_DH
    qi0 = ODD_OFF[4] // (IDX_HEADS * IDX_DIM)
    ki0 = ODD_OFF[5] // LANES
    kc0 = ODD_OFF[1] // (C_KV * C_DH)
    vc0 = ODD_OFF[2] // (C_KV * C_DH)
    g0 = ODD_OFF[3] // half
    rows = C_GROUP * BLK
    n_index_bits = max(1, int(math.ceil(math.log2(lp))))
    return pl.pallas_call(
        functools.partial(_sparse_attn_kernel, k_sel=k_sel, n_index_bits=n_index_bits, nq_real=nq_real),
        grid=(nb, nq),
        in_specs=[pl.BlockSpec((BLK, IDX_HEADS * IDX_DIM), lambda b, i: (b * nq + i, qi0)),
                  pl.BlockSpec((BLK, LANES), lambda b, i: (b * nq + i, ki0)),
                  pl.BlockSpec((lp, LANES), lambda b, i: (b, ki0)),
                  pl.BlockSpec((BLK, C_HEADS * C_DH), lambda b, i: (b * nq + i, 0)),
                  pl.BlockSpec((lp, C_KV * C_DH), lambda b, i: (b, kc0)),
                  pl.BlockSpec((lp, C_KV * C_DH), lambda b, i: (b, vc0)),
                  pl.BlockSpec((BLK, half), lambda b, i: (b * nq + i, g0)),
                  pl.BlockSpec((BLK, half), lambda b, i: (b * nq + i, g0 + 1)),
                  pl.BlockSpec((C_KV, 2, rows, BLK), lambda b, i: (0, 0, 0, 0))],
        out_specs=pl.BlockSpec((BLK, C_HEADS * C_DH), lambda b, i: (b * nq + i, 0)),
        out_shape=jax.ShapeDtypeStruct((nb * lp, C_HEADS * C_DH), BF16),
        scratch_shapes=[pltpu.VMEM((lp // CHUNK, CHUNK, BLK), I32),
                        pltpu.VMEM((IDX_HEADS * BLK, LANES), BF16),
                        pltpu.VMEM((C_KV, rows, C_DH), BF16),
                        pltpu.VMEM((C_KV, 2 * SUBS, rows, BLK), F32),
                        pltpu.VMEM((1, BLK), I32),
                        pltpu.VMEM((1, BLK), I32),
                        pltpu.VMEM((C_KV, rows, LANES), F32),
                        pltpu.VMEM((C_KV, rows, 2 * C_DH), F32)],
        compiler_params=_cparams(("parallel", "arbitrary")),
        name="sparse_attn_prompt",
    )(zf, zf, zb, zf, zb, zb, zf, zf, near_tiles)


PAGES_PER_STEP = 32
INDEX_PAGES_PER_STEP = 64
DEC_ROWS = 16


def _interleaved_pages(refs, page):
    halves = [jnp.concatenate([r[pl.ds(h, page, stride=2), :] for r in refs], axis=0) for h in range(2)]
    return jnp.concatenate(halves, axis=1).astype(BF16)


def _paged_attn_kernel(*refs, n_pages_step, masked, page, k_feature_major):
    if masked:
        pt_ref, tau_ref, jmax_ref, selnew_ref = refs[:4]
        refs = refs[4:]
    else:
        pt_ref = refs[0]
        refs = refs[1:]
    q_ref, knew_ref, vnew_ref, bfar_ref, blast_ref, b0_ref = refs[:6]
    refs = refs[6:]
    if masked:
        keys_ref = refs[0]
        refs = refs[1:]
    k_refs = refs[:n_pages_step]
    v_refs = refs[n_pages_step:2 * n_pages_step]
    out_ref, m_ref, l_ref, acc_ref = refs[2 * n_pages_step:]
    del pt_ref
    b = pl.program_id(0)
    j = pl.program_id(1)
    width = n_pages_step * page
    q = q_ref[...]

    @pl.when(j == 0)
    def _():
        s_new = jnp.sum(q.astype(F32) * knew_ref[...].astype(BF16).astype(F32), axis=1, keepdims=True) + b0_ref[...]
        v_new = jnp.broadcast_to(vnew_ref[...].astype(BF16).astype(F32), acc_ref.shape)
        if masked:
            take = selnew_ref[b] > 0
            m_ref[...] = jnp.where(take, s_new, NEG)
            l_ref[...] = jnp.where(take, 1.0, 0.0) * jnp.ones(l_ref.shape, F32)
            acc_ref[...] = jnp.where(take, v_new, 0.0)
        else:
            m_ref[...] = s_new
            l_ref[...] = jnp.ones(l_ref.shape, F32)
            acc_ref[...] = v_new

    if k_feature_major:
        kcat = jnp.concatenate([r[...] for r in k_refs], axis=1).astype(BF16)
        s = jnp.dot(q, kcat, preferred_element_type=F32)
    else:
        s = lax.dot_general(q, _interleaved_pages(k_refs, page), NT, preferred_element_type=F32)
    vcat = _interleaved_pages(v_refs, page)
    s = s + jnp.where(j == pl.num_programs(1) - 1, blast_ref[...], bfar_ref[...])
    if masked:
        key = keys_ref[...]
        idx = j * width + lax.broadcasted_iota(I32, (1, width), 1)
        tau = tau_ref[b]
        sel = (key > tau) | ((key == tau) & (idx <= jmax_ref[b]))
        s = jnp.where(sel, s, NEG)
    m_old = m_ref[...]
    m_new = jnp.maximum(m_old, jnp.max(s, axis=1, keepdims=True))
    alpha = jnp.exp(m_old - m_new)
    p = jnp.exp(s - m_new[:, 0:1])
    l_ref[...] = alpha * l_ref[...] + jnp.sum(p, axis=1, keepdims=True)
    acc_ref[...] = _lane_tile(alpha, 2) * acc_ref[...] + jnp.dot(p.astype(BF16), vcat, preferred_element_type=F32)
    m_ref[...] = m_new

    @pl.when(j == pl.num_programs(1) - 1)
    def _():
        o = acc_ref[...]
        rowi = lax.broadcasted_iota(I32, (DEC_ROWS, LANES), 0)
        upper = (rowi >= DEC_ROWS // 2) if not masked else ((rowi >= DEC_ROWS // 4) & (rowi < DEC_ROWS // 2))
        out_ref[...] = jnp.where(upper, o[:, LANES:], o[:, :LANES]) / l_ref[...]


def _pages_per_step(n_pages, g=PAGES_PER_STEP):
    while n_pages % g:
        g //= 2
    return g


def _paged_attention(qprime, k_new, v_new, bias_far, bias_last, bias0, k_cache, v_cache, page_table, page,
                     k_feature_major, mask_args=None):
    db = qprime.shape[0]
    n_pages = page_table.shape[1]
    width = qprime.shape[2]
    g = _pages_per_step(n_pages)
    n_steps = n_pages // g
    masked = mask_args is not None
    n_pref = 4 if masked else 1

    def page_map(gi):
        return lambda b, j, pt, *_: (pt[b * n_pages + j * g + gi], 0, 0)

    in_specs = [pl.BlockSpec((None, DEC_ROWS, width), lambda b, j, *_: (b, 0, 0)),
                pl.BlockSpec((None, 1, width), lambda b, j, *_: (b, 0, 0)),
                pl.BlockSpec((None, 1, width), lambda b, j, *_: (b, 0, 0)),
                pl.BlockSpec((DEC_ROWS, g * page), lambda b, j, *_: (0, 0)),
                pl.BlockSpec((DEC_ROWS, g * page), lambda b, j, *_: (0, 0)),
                pl.BlockSpec((DEC_ROWS, LANES), lambda b, j, *_: (0, 0))]
    args = [qprime, k_new, v_new, bias_far, bias_last, bias0]
    prefetch = [page_table.reshape(-1)]
    if masked:
        keys, tau, jmax, selnew = mask_args
        prefetch += [tau, jmax, selnew]
        in_specs += [pl.BlockSpec((None, 1, g * page), lambda b, j, *_: (b, 0, j))]
        args += [keys]
    in_specs += [pl.BlockSpec((None,) + k_cache.shape[1:], page_map(gi)) for gi in range(g)]
    in_specs += [pl.BlockSpec((None,) + v_cache.shape[1:], page_map(gi)) for gi in range(g)]
    args += [k_cache] * g + [v_cache] * g
    grid_spec = pltpu.PrefetchScalarGridSpec(
        num_scalar_prefetch=n_pref,
        grid=(db, n_steps),
        in_specs=in_specs,
        out_specs=pl.BlockSpec((None, DEC_ROWS, LANES), lambda b, j, *_: (b, 0, 0)),
        scratch_shapes=[pltpu.VMEM((DEC_ROWS, LANES), F32),
                        pltpu.VMEM((DEC_ROWS, LANES), F32),
                        pltpu.VMEM((DEC_ROWS, width), F32)])
    return pl.pallas_call(
        functools.partial(_paged_attn_kernel, n_pages_step=g, masked=masked, page=page,
                          k_feature_major=k_feature_major),
        grid_spec=grid_spec,
        out_shape=jax.ShapeDtypeStruct((db, DEC_ROWS, LANES), F32),
        compiler_params=_cparams(("parallel", "arbitrary")),
        name="paged_attn_masked" if masked else "paged_attn",
    )(*prefetch, *args)


def _paged_index_kernel(pt_ref, q_ref, w_ref, knew_ref, *refs, n_pages_step, page):
    k_refs = refs[:n_pages_step]
    out_ref = refs[n_pages_step]
    del pt_ref
    j = pl.program_id(1)
    last = pl.num_programs(1) - 1
    q = q_ref[...]
    w = w_ref[...]

    def score(kmat_t):
        d = jnp.dot(q, kmat_t.astype(BF16), preferred_element_type=F32)
        return jnp.sum(jnp.maximum(d, 0.0) * w[:, 0:1], axis=0, keepdims=True)

    @pl.when(j < last)
    def _():
        kcat = jnp.concatenate([r[...] for r in k_refs], axis=1)
        out_ref[...] = _sortable_key(score(kcat))

    @pl.when(j == last)
    def _():
        sc = score(knew_ref[...])
        lane = lax.broadcasted_iota(I32, (1, page), 1)
        sc = jnp.where(lane == 0, sc, -jnp.inf)
        pad = jnp.full((1, (n_pages_step - 1) * page), -jnp.inf, F32)
        full = jnp.concatenate([sc, pad], axis=1) if n_pages_step > 1 else sc
        out_ref[...] = _sortable_key(full)


def _paged_index_scores(qidx, wcol, k_new_tile, idx_cache_t, page_table):
    db = qidx.shape[0]
    n_pages = page_table.shape[1]
    page = idx_cache_t.shape[2]
    g = _pages_per_step(n_pages, INDEX_PAGES_PER_STEP)
    n_steps = n_pages // g

    def page_map(gi):
        return lambda b, j, pt: (pt[b * n_pages + jnp.minimum(j, n_steps - 1) * g + gi], 0, 0)

    grid_spec = pltpu.PrefetchScalarGridSpec(
        num_scalar_prefetch=1,
        grid=(db, n_steps + 1),
        in_specs=[pl.BlockSpec((None, IDX_HEADS, IDX_DIM), lambda b, j, pt: (b, 0, 0)),
                  pl.BlockSpec((None, IDX_HEADS, LANES), lambda b, j, pt: (b, 0, 0)),
                  pl.BlockSpec((None, IDX_DIM, page), lambda b, j, pt: (b, 0, 0))]
        + [pl.BlockSpec((None, IDX_DIM, page), page_map(gi)) for gi in range(g)],
        out_specs=pl.BlockSpec((None, 1, g * page), lambda b, j, pt: (b, 0, j)))
    return pl.pallas_call(
        functools.partial(_paged_index_kernel, n_pages_step=g, page=page),
        grid_spec=grid_spec,
        out_shape=jax.ShapeDtypeStruct((db, 1, (n_steps + 1) * g * page), I32),
        compiler_params=_cparams(("parallel", "arbitrary")),
        name="paged_index_scores",
    )(page_table.reshape(-1), qidx, wcol, k_new_tile, *([idx_cache_t] * g))


def _select_kernel(keys_ref, tau_ref, jmax_ref, *, k_sel, n_index_bits):
    keys = keys_ref[...]
    idx = lax.broadcasted_iota(I32, keys.shape, 1)
    shape = (keys.shape[0], 1)

    def count_fn(pred):
        return jnp.sum(jnp.where(pred(keys, idx), 1, 0), axis=1, keepdims=True)

    tau = _kth_largest(count_fn, k_sel, shape)
    need = k_sel - count_fn(lambda key, i: key > tau)
    jmax = _tie_cutoff(count_fn, tau, need, n_index_bits, shape)
    tau_ref[...] = jnp.broadcast_to(tau, tau_ref.shape)
    jmax_ref[...] = jnp.broadcast_to(jmax, jmax_ref.shape)


def _select_rows(keys2d, k_sel):
    rows, width = keys2d.shape
    n_index_bits = max(1, int(math.ceil(math.log2(width))))
    return pl.pallas_call(
        functools.partial(_select_kernel, k_sel=k_sel, n_index_bits=n_index_bits),
        out_shape=[jax.ShapeDtypeStruct((rows, LANES), I32), jax.ShapeDtypeStruct((rows, LANES), I32)],
        compiler_params=pltpu.CompilerParams(vmem_limit_bytes=VMEM_LIMIT),
        name="select_rows",
    )(keys2d)


def _even_tail_kernel(z_ref, conv_ref, s_ref, oa_ref, convw_ref, gp_ref, normg_ref, lamv_ref, subln_ref,
                      ya_ref, yb_ref, snew_ref, *, lam_init):
    z = z_ref[...]
    hw = B_HEADS * B_DK
    lv = lamv_ref[...]
    lam = (jnp.exp(jnp.sum(lv[0:1] * lv[1:2], axis=1, keepdims=True))
           - jnp.exp(jnp.sum(lv[2:3] * lv[3:4], axis=1, keepdims=True)) + lam_init)
    oa = oa_ref[...]
    for hg in range(A_HEADS):
        o = oa[2 * hg:2 * hg + 1] - lam * oa[2 * hg + 1:2 * hg + 2]
        y = o * lax.rsqrt(jnp.mean(o * o, axis=-1, keepdims=True) + EPS)
        y = (y * subln_ref[...]) * (1.0 - lam_init)
        gate = z[:, EVEN_OFF[3] + hg * A_DV:EVEN_OFF[3] + (hg + 1) * A_DV]
        ya_ref[:, hg * A_DV:(hg + 1) * A_DV] = (y * _silu(gate)).astype(BF16)
    x_new = z[:, EVEN_OFF[4]:EVEN_OFF[4] + B_QKV]
    conv = convw_ref[CONV_W - 1:CONV_W, :] * x_new
    cp = conv_ref[...]
    for i in range(CONV_W - 1):
        conv = conv + convw_ref[i:i + 1, :] * cp[i:i + 1, :]
    act = _silu(conv)
    ab = z[:, EVEN_OFF[6]:EVEN_OFF[6] + LANES]
    gp = gp_ref[...]
    row = lax.broadcasted_iota(I32, (B_DK, B_DV), 0)
    col = lax.broadcasted_iota(I32, (B_DK, B_DV), 1)
    eye = row == col
    for h in range(B_HEADS):
        q = act[:, h * B_DK:(h + 1) * B_DK]
        k = act[:, hw + h * B_DK:hw + (h + 1) * B_DK]
        v = act[:, 2 * hw + h * B_DV:2 * hw + (h + 1) * B_DV]
        q = q * lax.rsqrt(jnp.sum(q * q, axis=-1, keepdims=True) + EPS) * (B_DK ** -0.5)
        k = k * lax.rsqrt(jnp.sum(k * k, axis=-1, keepdims=True) + EPS)
        g = -jnp.exp(gp[0:1, h:h + 1]) * _softplus(ab[:, h:h + 1] + gp[1:2, h:h + 1])
        beta = jax.nn.sigmoid(ab[:, B_HEADS + h:B_HEADS + h + 1])
        eg = jnp.exp(g)
        s0 = s_ref[h]
        kcol = jnp.sum(jnp.where(eye, jnp.broadcast_to(k, (B_DK, B_DK)), 0.0), axis=1, keepdims=True)
        qcol = jnp.sum(jnp.where(eye, jnp.broadcast_to(q, (B_DK, B_DK)), 0.0), axis=1, keepdims=True)
        ks = jnp.sum(kcol * s0, axis=0, keepdims=True)
        qs = jnp.sum(qcol * s0, axis=0, keepdims=True)
        u = beta * (v - eg * ks)
        qk = jnp.sum(q * k, axis=1, keepdims=True)
        o = eg * qs + qk * u
        snew_ref[h] = eg * s0 + kcol * u
        y = o * lax.rsqrt(jnp.mean(o * o, axis=-1, keepdims=True) + EPS) * normg_ref[...]
        gate = z[:, EVEN_OFF[5] + h * B_DV:EVEN_OFF[5] + (h + 1) * B_DV]
        yb_ref[:, h * B_DV:(h + 1) * B_DV] = (y * _silu(gate)).astype(BF16)


def _even_tail(zf_s, conv_prev, s_prev, oa, conv_w, a_log, dt_bias, norm_g, lamv, subln, lam_init):
    db, npad = zf_s.shape
    gp = jnp.zeros((SUBLANES, LANES), F32).at[0, :B_HEADS].set(a_log).at[1, :B_HEADS].set(dt_bias)
    hw = B_HEADS * B_DV
    return pl.pallas_call(
        functools.partial(_even_tail_kernel, lam_init=lam_init),
        grid=(db,),
        in_specs=[pl.BlockSpec((None, 1, npad), lambda b: (b, 0, 0)),
                  pl.BlockSpec((None, CONV_W - 1, B_QKV), lambda b: (b, 0, 0)),
                  pl.BlockSpec((None, B_HEADS, B_DK, B_DV), lambda b: (b, 0, 0, 0)),
                  pl.BlockSpec((None, DEC_ROWS, LANES), lambda b: (b, 0, 0)),
                  pl.BlockSpec((CONV_W, B_QKV), lambda b: (0, 0)),
                  pl.BlockSpec((SUBLANES, LANES), lambda b: (0, 0)),
                  pl.BlockSpec((1, B_DV), lambda b: (0, 0)),
                  pl.BlockSpec((4, A_DH), lambda b: (0, 0)),
                  pl.BlockSpec((1, A_DV), lambda b: (0, 0))],
        out_specs=[pl.BlockSpec((None, 1, A_HEADS * A_DV), lambda b: (b, 0, 0)),
                   pl.BlockSpec((None, 1, hw), lambda b: (b, 0, 0)),
                   pl.BlockSpec((None, B_HEADS, B_DK, B_DV), lambda b: (b, 0, 0, 0))],
        out_shape=[jax.ShapeDtypeStruct((db, 1, A_HEADS * A_DV), BF16),
                   jax.ShapeDtypeStruct((db, 1, hw), BF16),
                   jax.ShapeDtypeStruct((db, B_HEADS, B_DK, B_DV), F32)],
        compiler_params=_cparams(("parallel",)),
        name="even_tail",
    )(zf_s.reshape(db, 1, npad), conv_prev, s_prev, oa, conv_w, gp, norm_g.reshape(1, B_DV), lamv,
      subln.reshape(1, A_DV))


def _odd_tail_kernel(z_ref, oc_ref, y_ref):
    z = z_ref[...]
    oc = oc_ref[...]
    for hg in range(C_HEADS):
        gate = z[:, ODD_OFF[3] + hg * C_DH:ODD_OFF[3] + (hg + 1) * C_DH]
        y_ref[:, hg * C_DH:(hg + 1) * C_DH] = (oc[hg:hg + 1] * _silu(gate)).astype(BF16)


def _odd_tail(zf_s, oc):
    db, npad = zf_s.shape
    return pl.pallas_call(
        _odd_tail_kernel,
        grid=(db,),
        in_specs=[pl.BlockSpec((None, 1, npad), lambda b: (b, 0, 0)),
                  pl.BlockSpec((None, DEC_ROWS, LANES), lambda b: (b, 0, 0))],
        out_specs=pl.BlockSpec((None, 1, C_HEADS * C_DH), lambda b: (b, 0, 0)),
        out_shape=jax.ShapeDtypeStruct((db, 1, C_HEADS * C_DH), BF16),
        compiler_params=_cparams(("parallel",)),
        name="odd_tail",
    )(zf_s.reshape(db, 1, npad), oc)


def _bias_by_distance(table):
    n = jnp.arange(FAR_DIST + 1)
    exact = N_BUCKETS // 2
    nf = jnp.maximum(n, 1).astype(F32)
    large = exact + (jnp.log(nf / exact) / math.log(MAX_DIST / exact) * (N_BUCKETS - exact)).astype(I32)
    bucket = jnp.where(n < exact, n, jnp.minimum(large, N_BUCKETS - 1))
    return table[bucket].astype(F32)


def _prompt_near_tiles(bd, group):
    assert FAR_DIST <= BLK
    heads = bd.shape[1]
    rel = ((bd - bd[FAR_DIST][None, :]) * LOG2E).T
    f = jnp.concatenate([jnp.full((heads, BLK - 1), NEG, F32), rel,
                         jnp.broadcast_to(rel[:, FAR_DIST:], (heads, 2 * BLK - 1 - FAR_DIST))], axis=1)
    period = 3 * BLK
    g = jnp.pad(f[:, ::-1], ((0, 0), (0, period - f.shape[1])))
    wrapped = jnp.tile(g, (1, BLK + 1))[:, :BLK * (period + 1)].reshape(heads, BLK, period + 1)
    strip = wrapped[:, ::-1, :2 * BLK]
    tiles = jnp.stack([strip[:, :, :BLK], strip[:, :, BLK:]], axis=1)
    n_kv = heads // group
    tiles = tiles.reshape(n_kv, group, 2, BLK, BLK)
    return jnp.transpose(tiles, (0, 2, 1, 3, 4)).reshape(n_kv, 2, group * BLK, BLK)


def _decode_bias(bd, row_heads, past, page, g):
    heads = jnp.asarray(row_heads, I32)
    far = jnp.broadcast_to(bd[FAR_DIST][heads][:, None], (len(row_heads), g * page))
    pos = past - g * page + jnp.arange(g * page)
    dist = jnp.minimum(past - pos, FAR_DIST)
    last = bd[dist][:, heads].T
    new = jnp.broadcast_to(bd[0][heads][:, None], (len(row_heads), LANES))
    return far.astype(F32), last.astype(F32), new.astype(F32)


def _pad_cols(w, mult):
    n = w.shape[1]
    return jnp.pad(w, ((0, 0), (0, _round_up(n, mult) - n)))


def kernel(x_prompt, x_sample, cache_a_k, cache_a_v, state_b_s, state_b_conv, cache_c_k, cache_c_v, cache_c_idx,
           page_table, meta, bias_table, final_norm, norm_e, w_in_e, w_out_e, lam_q1, lam_k1, lam_q2, lam_k2,
           subln_a, conv_b, a_log_b, dt_bias_b, norm_b, norm_o, w_in_o, w_out_o):
    nb, seq, d = x_prompt.shape
    n_meta = meta.shape[0]
    l = seq + n_meta
    lp = _round_up(l, CHUNK)
    nq_real = pl.cdiv(l, BLK)
    db = x_sample.shape[0]
    n_pages = page_table.shape[1]
    page = cache_a_k.shape[2]
    past = n_pages * page
    n_pool = cache_a_k.shape[1]
    assert x_sample.shape[1] == 1 and norm_e.shape[0] == 1 and norm_o.shape[0] == 1
    lam_init = 0.8 - 0.6 * math.exp(-0.3 * 0)

    tn_e, tn_o = 8 * LANES, 5 * LANES
    w_e = _pad_cols(w_in_e[0], tn_e).astype(BF16)
    w_o = _pad_cols(w_in_o[0], tn_o).astype(BF16)
    w_out_a = w_out_e[0][:A_HEADS * A_DV].astype(BF16)
    w_out_b = w_out_e[0][A_HEADS * A_DV:].astype(BF16)
    w_out_c = w_out_o[0].astype(BF16)
    tm = CHUNK
    tm_proj = 2 * CHUNK if (nb * lp) % (2 * CHUNK) == 0 else CHUNK
    dbp = _round_up(db, SUBLANES)

    bd = _bias_by_distance(bias_table)
    tiles_a = _prompt_near_tiles(bd, A_GROUP)
    tiles_c = tiles_a if (A_KV, A_GROUP) == (C_KV, C_GROUP) else _prompt_near_tiles(bd, C_GROUP)
    lamv = jnp.stack([lam_q1[0], lam_k1[0], lam_q2[0], lam_k2[0]]).astype(F32)

    hp = jnp.concatenate([jnp.broadcast_to(meta.astype(F32)[None], (nb, n_meta, d)), x_prompt], axis=1)
    hp = jnp.pad(hp, ((0, 0), (0, lp - l), (0, 0))).reshape(nb * lp, d)
    kv_tile = EVEN_OFF[1] // tn_e
    assert EVEN_OFF[3] <= (kv_tile + 1) * tn_e
    zf, zb = _norm_proj(hp, norm_e[0], w_e, tm_proj, tn_e, (kv_tile, kv_tile + 1))
    ya = _diff_attn_prompt(zf, zb, kv_tile * tn_e, tiles_a, lamv, subln_a[0], nb, lp, nq_real, lam_init)
    yb, pb_s = _gdn_prompt(zf, conv_b[0], a_log_b[0], dt_bias_b[0], norm_b[0], nb, lp, l)
    h1 = _out_proj([ya, yb], [w_out_a, w_out_b], hp, tm)
    z3 = zf.reshape(nb, lp, -1)
    pa_k = z3[:, :l, EVEN_OFF[1]:EVEN_OFF[2]].reshape(1, nb, l, A_KV, 2, A_DH)
    pa_v = z3[:, :l, EVEN_OFF[2]:EVEN_OFF[3]].reshape(1, nb, l, A_KV, A_DV)
    pb_conv = z3[:, l - (CONV_W - 1):l, EVEN_OFF[4]:EVEN_OFF[5]][None]

    zf1, zb1 = _norm_proj(h1, norm_o[0], w_o, tm_proj, tn_o, (0, w_o.shape[1] // tn_o))
    k_sel_p = min(TOPK_MAX, l // 4)
    yc = _sparse_attn_prompt(zf1, zb1, tiles_c, nb, lp, nq_real, k_sel_p)
    yp = _out_proj([yc], [w_out_c], h1, tm, final_gain=final_norm)
    y_prompt = yp.reshape(nb, lp, d)[:, n_meta:l]
    z13 = zf1.reshape(nb, lp, -1)
    pc_k = z13[:, :l, ODD_OFF[1]:ODD_OFF[2]].reshape(1, nb, l, C_KV, C_DH)
    pc_v = z13[:, :l, ODD_OFF[2]:ODD_OFF[3]].reshape(1, nb, l, C_KV, C_DH)
    pc_idx = z13[:, :l, ODD_OFF[5]:ODD_OFF[6]][None]

    hs = jnp.pad(x_sample.reshape(db, d), ((0, dbp - db), (0, 0)))
    zs, _ = _norm_proj(hs, norm_e[0], w_e, dbp, tn_e, (0, 1))
    zs = zs[:db]
    qa = zs[:, :EVEN_OFF[1]].reshape(db, A_KV, A_GROUP, 2, A_DH) * (A_DH ** -0.5)
    qprime = jnp.einsum('bhgcd,hi,cj->bhgcijd', qa, jnp.eye(A_KV, dtype=F32), jnp.eye(2, dtype=F32))
    qprime = qprime.reshape(db, DEC_ROWS, A_KV * 2 * A_DH).astype(BF16)
    k_new = zs[:, EVEN_OFF[1]:EVEN_OFF[2]].reshape(db, 1, -1)
    v_new = zs[:, EVEN_OFF[2]:EVEN_OFF[3]].reshape(db, 1, -1)
    g_dec = _pages_per_step(n_pages)
    rows_a = [r // 2 for r in range(DEC_ROWS)]
    bfar, blast, bnew = _decode_bias(bd, rows_a, past, page, g_dec)
    ak_t = jnp.transpose(cache_a_k[0], (0, 2, 3, 4, 1)).reshape(n_pool, A_KV * 2 * A_DH, page)
    av_r = cache_a_v[0].reshape(n_pool, page * A_KV, A_DV)
    oa = _paged_attention(qprime, k_new, v_new, bfar, blast, bnew, ak_t, av_r, page_table, page,
                          k_feature_major=True)
    ya_s, yb_s, sb_s = _even_tail(zs, state_b_conv[0], state_b_s[0], oa, conv_b[0], a_log_b[0], dt_bias_b[0],
                                  norm_b[0], lamv, subln_a[0], lam_init)
    hs_pad = lambda y: jnp.pad(y.reshape(db, -1), ((0, dbp - db), (0, 0)))
    hs1 = _out_proj([hs_pad(ya_s), hs_pad(yb_s)], [w_out_a, w_out_b], hs, dbp)
    sa_k = zs[:, EVEN_OFF[1]:EVEN_OFF[2]].reshape(1, db, 1, A_KV, 2, A_DH)
    sa_v = zs[:, EVEN_OFF[2]:EVEN_OFF[3]].reshape(1, db, 1, A_KV, A_DV)
    sb_conv = jnp.concatenate([state_b_conv[0][:, 1:], zs[:, None, EVEN_OFF[4]:EVEN_OFF[5]]], axis=1)[None]

    zs1, _ = _norm_proj(hs1, norm_o[0], w_o, dbp, tn_o, (0, 1))
    zs1 = zs1[:db]
    qidx = (zs1[:, ODD_OFF[4]:ODD_OFF[5]].reshape(db, IDX_HEADS, IDX_DIM) * (IDX_DIM ** -0.5)).astype(BF16)
    wcol = jnp.broadcast_to((zs1[:, ODD_OFF[6]:ODD_OFF[7]] * (IDX_HEADS ** -0.5))[:, :, None],
                            (db, IDX_HEADS, LANES))
    ki_new = zs1[:, ODD_OFF[5]:ODD_OFF[6]]
    ki_tile = jnp.zeros((db, IDX_DIM, page), F32).at[:, :, 0].set(ki_new)
    ci_t = jnp.transpose(cache_c_idx[0], (0, 2, 1))
    keys = _paged_index_scores(qidx, wcol, ki_tile, ci_t, page_table)
    k_sel_s = min(TOPK_MAX, (past + 1) // 4)
    tau, jmax = _select_rows(keys.reshape(db, -1), k_sel_s)
    tau, jmax = tau[:, 0], jmax[:, 0]
    key_new = keys[:, 0, past]
    selnew = ((key_new > tau) | ((key_new == tau) & (past <= jmax))).astype(I32)
    qc = zs1[:, :ODD_OFF[1]].reshape(db, C_KV, C_GROUP, C_DH) * (C_DH ** -0.5)
    qcp = jnp.einsum('bhgd,hi->bhgid', qc, jnp.eye(C_KV, dtype=F32)).reshape(db, C_HEADS, C_KV * C_DH)
    qcp = jnp.pad(qcp, ((0, 0), (0, DEC_ROWS - C_HEADS), (0, 0))).astype(BF16)
    kc_new = zs1[:, ODD_OFF[1]:ODD_OFF[2]].reshape(db, 1, -1)
    vc_new = zs1[:, ODD_OFF[2]:ODD_OFF[3]].reshape(db, 1, -1)
    rows_c = [r if r < C_HEADS else 0 for r in range(DEC_ROWS)]
    cfar, clast, cnew = _decode_bias(bd, rows_c, past, page, g_dec)
    ck_r = cache_c_k[0].reshape(n_pool, page * C_KV, C_DH)
    cv_r = cache_c_v[0].reshape(n_pool, page * C_KV, C_DH)
    oc = _paged_attention(qcp, kc_new, vc_new, cfar, clast, cnew, ck_r, cv_r, page_table, page,
                          k_feature_major=False, mask_args=(keys, tau, jmax, selnew))
    yc_s = _odd_tail(zs1, oc)
    ys = _out_proj([hs_pad(yc_s)], [w_out_c], hs1, dbp, final_gain=final_norm)
    y_sample = ys[:db].reshape(db, 1, d)
    sc_k = zs1[:, ODD_OFF[1]:ODD_OFF[2]].reshape(1, db, 1, C_KV, C_DH)
    sc_v = zs1[:, ODD_OFF[2]:ODD_OFF[3]].reshape(1, db, 1, C_KV, C_DH)
    sc_idx = zs1[:, None, ODD_OFF[5]:ODD_OFF[6]][None]

    return (y_prompt, y_sample, pa_k, pa_v, pb_s[None], pb_conv, pc_k, pc_v, pc_idx,
            sa_k, sa_v, sb_s[None], sb_conv, sc_k, sc_v, sc_idx)
```

```python
import functools
import math

import jax
import jax.numpy as jnp
import numpy as np
from jax import lax
from jax.experimental import pallas as pl
from jax.experimental.pallas import tpu as pltpu

F32 = jnp.float32
BF16 = jnp.bfloat16
I32 = jnp.int32

EPS = 1e-6
N_BUCKETS = 32
MAX_DIST = 128
FAR_DIST = MAX_DIST

A_HEADS, A_KV, A_GROUP, A_DH, A_DV = 8, 2, 4, 64, 128
B_HEADS, B_DK, B_DV, CONV_W = 4, 128, 128, 4
C_HEADS, C_KV, C_GROUP, C_DH = 8, 2, 4, 128
IDX_HEADS, IDX_DIM, TOPK_MAX = 8, 64, 256
B_QKV = 2 * B_HEADS * B_DK + B_HEADS * B_DV

EVEN_COLS = (A_HEADS * 2 * A_DH, A_KV * 2 * A_DH, A_KV * A_DV, A_HEADS * A_DV, B_QKV, B_HEADS * B_DV, B_HEADS, B_HEADS)
ODD_COLS = (C_HEADS * C_DH, C_KV * C_DH, C_KV * C_DH, C_HEADS * C_DH, IDX_HEADS * IDX_DIM, IDX_DIM, IDX_HEADS)
EVEN_OFF = tuple(int(v) for v in np.cumsum((0,) + EVEN_COLS))
ODD_OFF = tuple(int(v) for v in np.cumsum((0,) + ODD_COLS))

LANES = 128
SUBLANES = 8
BLK = 128
CHUNK = 512
SUBS = CHUNK // BLK
NEG = -1e30
INT_MIN = -2 ** 31
INT_MAX = 2 ** 31 - 1
LOG2E = math.log2(math.e)
VMEM_LIMIT = 56 * 1024 * 1024
HI = lax.Precision.HIGHEST

NT = (((1,), (1,)), ((), ()))


def _round_up(x, m):
    return (x + m - 1) // m * m


def _cparams(sem):
    return pltpu.CompilerParams(dimension_semantics=sem, vmem_limit_bytes=VMEM_LIMIT)


def _silu(x):
    return x * jax.nn.sigmoid(x)


def _sortable_key(score):
    score = jnp.where(score == 0.0, 0.0, score)
    bits = pltpu.bitcast(score, I32)
    return bits ^ ((bits >> 31) & jnp.int32(0x7FFFFFFF))


def _lane_tile(x, n):
    return jnp.concatenate([x] * n, axis=1)


def _proj_kernel(x_ref, g_ref, w_ref, of_ref, ob_ref, xn_ref, *, bf16_tiles):
    j = pl.program_id(1)

    @pl.when(j == 0)
    def _():
        xf = x_ref[...]
        y = xf * lax.rsqrt(jnp.mean(xf * xf, axis=-1, keepdims=True) + EPS)
        xn_ref[...] = (y * g_ref[...]).astype(BF16)

    acc = jnp.dot(xn_ref[...], w_ref[...], preferred_element_type=F32)
    of_ref[...] = acc

    @pl.when((j >= bf16_tiles[0]) & (j < bf16_tiles[1]))
    def _():
        ob_ref[...] = acc.astype(BF16)


def _norm_proj(x, gain, w_bf16, tm, tn, bf16_tiles):
    m, d = x.shape
    n = w_bf16.shape[1]
    t0, t1 = bf16_tiles
    return pl.pallas_call(
        functools.partial(_proj_kernel, bf16_tiles=bf16_tiles),
        grid=(m // tm, n // tn),
        in_specs=[pl.BlockSpec((tm, d), lambda i, j: (i, 0)),
                  pl.BlockSpec((1, d), lambda i, j: (0, 0)),
                  pl.BlockSpec((d, tn), lambda i, j: (0, j))],
        out_specs=[pl.BlockSpec((tm, tn), lambda i, j: (i, j)),
                   pl.BlockSpec((tm, tn), lambda i, j: (i, jnp.clip(j - t0, 0, t1 - t0 - 1)))],
        out_shape=[jax.ShapeDtypeStruct((m, n), F32), jax.ShapeDtypeStruct((m, (t1 - t0) * tn), BF16)],
        scratch_shapes=[pltpu.VMEM((tm, d), BF16)],
        compiler_params=_cparams(("parallel", "arbitrary")),
        name="norm_proj",
    )(x, gain.reshape(1, d), w_bf16)


def _outproj_kernel(*refs, n_lhs, final_norm):
    ys = refs[:n_lhs]
    ws = refs[n_lhs:2 * n_lhs]
    h_ref = refs[2 * n_lhs]
    pos = 2 * n_lhs + 1
    fn_ref = refs[pos] if final_norm else None
    out_ref = refs[-1]
    acc = h_ref[...]
    for y_ref, w_ref in zip(ys, ws):
        acc = acc + jnp.dot(y_ref[...], w_ref[...], preferred_element_type=F32)
    if final_norm:
        y = acc * lax.rsqrt(jnp.mean(acc * acc, axis=-1, keepdims=True) + EPS)
        acc = y * fn_ref[...]
    out_ref[...] = acc


def _out_proj(ys, ws, h, tm, final_gain=None):
    m, n = h.shape
    n_lhs = len(ys)
    in_specs = [pl.BlockSpec((tm, y.shape[1]), lambda i: (i, 0)) for y in ys]
    in_specs += [pl.BlockSpec(w.shape, lambda i: (0, 0)) for w in ws]
    in_specs += [pl.BlockSpec((tm, n), lambda i: (i, 0))]
    args = list(ys) + list(ws) + [h]
    if final_gain is not None:
        in_specs += [pl.BlockSpec((1, n), lambda i: (0, 0))]
        args += [final_gain.reshape(1, n)]
    return pl.pallas_call(
        functools.partial(_outproj_kernel, n_lhs=n_lhs, final_norm=final_gain is not None),
        grid=(m // tm,),
        in_specs=in_specs,
        out_specs=pl.BlockSpec((tm, n), lambda i: (i, 0)),
        out_shape=jax.ShapeDtypeStruct((m, n), F32),
        compiler_params=_cparams(("parallel",)),
        name="out_proj",
    )(*args)


def _flash_steps(ss, v1s, m_refs):
    each = range(len(ss))
    m_old = [m_refs[i][0][m_refs[i][2]] for i in each]
    m_new = [jnp.maximum(m_old[i], jnp.max(ss[i], axis=1, keepdims=True)) for i in each]
    alpha = [jnp.exp2(m_old[i] - m_new[i]) for i in each]
    p = [jnp.exp2(ss[i] - _lane_tile(m_new[i], ss[i].shape[1] // LANES)).astype(BF16) for i in each]
    pv = [jnp.dot(p[i], v1s[i], preferred_element_type=F32) for i in each]
    for i in each:
        m_ref, acc_ref, idx = m_refs[i]
        acc_ref[idx] = _lane_tile(alpha[i], 2) * acc_ref[idx] + pv[i]
        m_ref[idx] = m_new[i]


def _far_chunk_loop(n_far, run_chunks):
    def quad(i, carry):
        run_chunks([4 * i, 4 * i + 1, 4 * i + 2, 4 * i + 3])
        return carry

    n_quad = lax.shift_right_logical(n_far, 2)
    lax.fori_loop(0, n_quad, quad, 0)

    @pl.when(jnp.bitwise_and(n_far, 2) == 2)
    def _():
        run_chunks([4 * n_quad, 4 * n_quad + 1])

    @pl.when(jnp.bitwise_and(n_far, 1) == 1)
    def _():
        run_chunks([n_far - 1])


TILE_ZERO, TILE_PREV, TILE_DIAG, TILE_MASKED = range(4)


def _near_bias_kind(sb, r):
    rel = sb - SUBS - r
    return jnp.where(rel == 0, TILE_DIAG, jnp.where(rel == -1, TILE_PREV, jnp.where(rel < -1, TILE_ZERO, TILE_MASKED)))


def _diff_attn_kernel(q_ref, k_ref, v_ref, gate_ref, tiles_ref, lamv_ref, subln_ref, out_ref,
                      qm_ref, m_ref, acc_ref, *, lam_init, nq_real):
    qi = pl.program_id(2)

    @pl.when(qi >= nq_real)
    def _():
        out_ref[...] = jnp.zeros(out_ref.shape, out_ref.dtype)

    @pl.when(qi < nq_real)
    def _():
        lane = lax.broadcasted_iota(I32, (BLK, 2 * A_DH), 1)
        for g in range(A_GROUP):
            qg = q_ref[:, g * 2 * A_DH:(g + 1) * 2 * A_DH] * (A_DH ** -0.5 * LOG2E)
            for c in range(2):
                keep = (lane < A_DH) if c == 0 else (lane >= A_DH)
                qm_ref[c, g * BLK:(g + 1) * BLK, :] = jnp.where(keep, qg, 0.0).astype(BF16)
        m_ref[...] = jnp.full(m_ref.shape, NEG, F32)
        acc_ref[...] = jnp.zeros(acc_ref.shape, F32)
        r = jnp.bitwise_and(qi, SUBS - 1)
        jd = lax.shift_right_logical(qi, int(math.log2(SUBS)))
        ones = jnp.ones((CHUNK, A_DV), BF16)

        def run_chunks(js, first_sbs=None):
            starts = [pl.multiple_of(j * CHUNK, CHUNK) for j in js]
            k = jnp.concatenate([k_ref[pl.ds(st, CHUNK), :] for st in starts], axis=0)
            v1 = jnp.concatenate(
                [jnp.concatenate([v_ref[pl.ds(st, CHUNK), :], ones], axis=1) for st in starts], axis=0)
            ss = [lax.dot_general(qm_ref[c], k, NT, preferred_element_type=F32) for c in range(2)]
            if first_sbs is not None:
                bias = jnp.concatenate(
                    [tiles_ref[_near_bias_kind(sb + i, r)] for sb in first_sbs for i in range(SUBS)], axis=1)
                ss = [s + bias for s in ss]
            _flash_steps(ss, [v1, v1], [(m_ref, acc_ref, c) for c in range(2)])

        _far_chunk_loop(jnp.maximum(jd - 1, 0), run_chunks)

        @pl.when(jd >= 1)
        def _():
            run_chunks([jd - 1, jd], [0, SUBS])

        @pl.when(jd == 0)
        def _():
            run_chunks([jd], [SUBS])

        lv = lamv_ref[...]
        lam = (jnp.exp(jnp.sum(lv[0:1] * lv[1:2], axis=1, keepdims=True))
               - jnp.exp(jnp.sum(lv[2:3] * lv[3:4], axis=1, keepdims=True)) + lam_init)
        a0 = acc_ref[0]
        a1 = acc_ref[1]
        o = a0[:, :A_DV] / a0[:, A_DV:] - lam * (a1[:, :A_DV] / a1[:, A_DV:])
        y = o * lax.rsqrt(jnp.mean(o * o, axis=-1, keepdims=True) + EPS)
        y = (y * subln_ref[...]) * (1.0 - lam_init)
        for g in range(A_GROUP):
            gate = gate_ref[:, g * A_DV:(g + 1) * A_DV]
            out_ref[:, g * A_DV:(g + 1) * A_DV] = (y[g * BLK:(g + 1) * BLK] * _silu(gate)).astype(BF16)


def _diff_attn_prompt(zf, zb, zb_col0, near_tiles, lamv, subln, nb, lp, nq_real, lam_init):
    nq = lp // BLK
    qw = A_GROUP * 2 * A_DH
    k_blk0 = (EVEN_OFF[1] - zb_col0) // (2 * A_DH)
    v_blk0 = (EVEN_OFF[2] - zb_col0) // A_DV
    g_blk0 = EVEN_OFF[3] // (A_GROUP * A_DV)
    rows = A_GROUP * BLK
    return pl.pallas_call(
        functools.partial(_diff_attn_kernel, lam_init=lam_init, nq_real=nq_real),
        grid=(nb, A_KV, nq),
        in_specs=[pl.BlockSpec((BLK, qw), lambda b, h, i: (b * nq + i, h)),
                  pl.BlockSpec((lp, 2 * A_DH), lambda b, h, i: (b, k_blk0 + h)),
                  pl.BlockSpec((lp, A_DV), lambda b, h, i: (b, v_blk0 + h)),
                  pl.BlockSpec((BLK, A_GROUP * A_DV), lambda b, h, i: (b * nq + i, g_blk0 + h)),
                  pl.BlockSpec((None, 4, rows, BLK), lambda b, h, i: (h, 0, 0, 0)),
                  pl.BlockSpec((4, A_DH), lambda b, h, i: (0, 0)),
                  pl.BlockSpec((1, A_DV), lambda b, h, i: (0, 0))],
        out_specs=pl.BlockSpec((BLK, A_GROUP * A_DV), lambda b, h, i: (b * nq + i, h)),
        out_shape=jax.ShapeDtypeStruct((nb * lp, A_HEADS * A_DV), BF16),
        scratch_shapes=[pltpu.VMEM((2, rows, 2 * A_DH), BF16),
                        pltpu.VMEM((2, rows, LANES), F32),
                        pltpu.VMEM((2, rows, 2 * A_DV), F32)],
        compiler_params=_cparams(("parallel", "parallel", "arbitrary")),
        name="diff_attn_prompt",
    )(zf, zb, zb, zf, near_tiles, lamv, subln.reshape(1, A_DV))


def _softplus(x):
    return jnp.maximum(x, 0.0) + jnp.log1p(jnp.exp(-jnp.abs(x)))


def _split_bf16(a):
    hi = a.astype(BF16)
    return hi, (a - hi.astype(F32)).astype(BF16)


def _dot3(a, b):
    (ah, al), (bh, bl) = a, b
    return (jnp.dot(ah, bh, preferred_element_type=F32)
            + (jnp.dot(ah, bl, preferred_element_type=F32) + jnp.dot(al, bh, preferred_element_type=F32)))


def _gdn_chunk_math(s0, q, k, v, gb, betab, row_ge, row_gt):
    n = len(q)
    c = q[0].shape[0]
    each = range(n)
    dot = functools.partial(jnp.dot, preferred_element_type=F32)
    dot_nt = functools.partial(lax.dot_general, dimension_numbers=NT, preferred_element_type=F32)
    ltri = jnp.where(row_ge, 1.0, 0.0).astype(F32)
    gcum = [jnp.dot(ltri, gb[i], precision=HI, preferred_element_type=F32) for i in each]
    decay = [jnp.where(row_ge, jnp.exp(jnp.where(row_ge, gcum[i] - gcum[i].T, 0.0)), 0.0) for i in each]
    kb = [k[i].astype(BF16) for i in each]
    qb = [q[i].astype(BF16) for i in each]
    s0b = [s0[i].astype(BF16) for i in each]
    kk = [dot_nt(kb[i], kb[i]) for i in each]
    x = [-jnp.where(row_gt, betab[i] * decay[i] * kk[i], 0.0) for i in each]
    tm = list(x)
    p = list(x)
    for _ in range(int(math.log2(c)) - 1):
        ps = [_split_bf16(p[i]) for i in each]
        p = [_dot3(ps[i], ps[i]) for i in each]
        ps = [_split_bf16(p[i]) for i in each]
        tms = [_split_bf16(tm[i]) for i in each]
        tm = [tm[i] + p[i] + _dot3(tms[i], ps[i]) for i in each]
    eg = [jnp.exp(gcum[i]) for i in each]
    ks = [dot(kb[i], s0b[i]) for i in each]
    rhs = [betab[i] * (v[i] - eg[i] * ks[i]) for i in each]
    u = [rhs[i] + dot(tm[i].astype(BF16), rhs[i].astype(BF16)) for i in each]
    ub = [u[i].astype(BF16) for i in each]
    qk = [dot_nt(qb[i], kb[i]) * decay[i] for i in each]
    o = [eg[i] * dot(qb[i], s0b[i]) + dot(qk[i].astype(BF16), ub[i]) for i in each]
    glast = [gcum[i][c - 1:c, :] for i in each]
    kd = [k[i] * jnp.exp(glast[i] - gcum[i]) for i in each]
    s_new = [jnp.exp(glast[i]) * s0[i] + dot(kd[i].T.astype(BF16), ub[i]) for i in each]
    return s_new, o


def _gdn_prompt_kernel(xq_ref, xk_ref, xv_ref, gate_ref, ab_ref, convw_ref, gp_ref, normg_ref,
                       y_ref, sfin_ref, xbuf_ref, s_ref, *, seq_len):
    ci = pl.program_id(0)
    hw = B_HEADS * B_DK
    nb = xq_ref.shape[0]

    @pl.when(ci == 0)
    def _():
        xbuf_ref[:, 0:SUBLANES, :] = jnp.zeros((nb, SUBLANES, B_QKV), F32)
        s_ref[...] = jnp.zeros(s_ref.shape, F32)

    row = lax.broadcasted_iota(I32, (BLK, BLK), 0)
    col = lax.broadcasted_iota(I32, (BLK, BLK), 1)
    row_ge = row >= col
    row_gt = row > col
    valid = (ci * BLK + row) < seq_len
    gp = gp_ref[...]
    chains = [(b, h) for b in range(nb) for h in range(B_HEADS)]
    qs, ks, vs, gbs, betabs = [], [], [], [], []
    for b in range(nb):
        xbuf_ref[b, SUBLANES:SUBLANES + BLK, 0:hw] = xq_ref[b]
        xbuf_ref[b, SUBLANES:SUBLANES + BLK, hw:2 * hw] = xk_ref[b]
        xbuf_ref[b, SUBLANES:SUBLANES + BLK, 2 * hw:3 * hw] = xv_ref[b]
        conv = jnp.zeros((BLK, B_QKV), F32)
        for i in range(CONV_W):
            conv = conv + convw_ref[i:i + 1, :] * xbuf_ref[b, pl.ds(SUBLANES - (CONV_W - 1) + i, BLK), :]
        tail = xbuf_ref[b, BLK:BLK + SUBLANES, :]
        xbuf_ref[b, 0:SUBLANES, :] = tail
        act = _silu(conv)
        ab = ab_ref[b]
        for h in range(B_HEADS):
            q = act[:, h * B_DK:(h + 1) * B_DK]
            k = act[:, hw + h * B_DK:hw + (h + 1) * B_DK]
            v = act[:, 2 * hw + h * B_DV:2 * hw + (h + 1) * B_DV]
            q = q * lax.rsqrt(jnp.sum(q * q, axis=-1, keepdims=True) + EPS) * (B_DK ** -0.5)
            k = k * lax.rsqrt(jnp.sum(k * k, axis=-1, keepdims=True) + EPS)
            a_raw = jnp.broadcast_to(ab[:, h:h + 1], (BLK, BLK))
            b_raw = jnp.broadcast_to(ab[:, B_HEADS + h:B_HEADS + h + 1], (BLK, BLK))
            a_log = gp[0:1, h:h + 1]
            dt_b = gp[1:2, h:h + 1]
            qs.append(q)
            ks.append(k)
            vs.append(v)
            gbs.append(jnp.where(valid, -jnp.exp(a_log) * _softplus(a_raw + dt_b), 0.0))
            betabs.append(jnp.where(valid, jax.nn.sigmoid(b_raw), 0.0))
    s_new, outs = _gdn_chunk_math([s_ref[b, h] for b, h in chains], qs, ks, vs, gbs, betabs, row_ge, row_gt)
    for (b, h), s_bh, o in zip(chains, s_new, outs):
        s_ref[b, h] = s_bh
        y = o * lax.rsqrt(jnp.mean(o * o, axis=-1, keepdims=True) + EPS) * normg_ref[...]
        gate = gate_ref[b, :, h * B_DV:(h + 1) * B_DV]
        y_ref[b, :, h * B_DV:(h + 1) * B_DV] = (y * _silu(gate)).astype(BF16)

    @pl.when(ci == pl.num_programs(0) - 1)
    def _():
        sfin_ref[...] = s_ref[...]


def _gdn_prompt(zf, conv_w, a_log, dt_bias, norm_g, nb, lp, seq_len):
    nc = lp // BLK
    hw = B_HEADS * B_DK
    c0 = EVEN_OFF[4] // hw
    g0 = EVEN_OFF[5] // hw
    ab0 = EVEN_OFF[6] // LANES
    gp = jnp.zeros((SUBLANES, LANES), F32).at[0, :B_HEADS].set(a_log).at[1, :B_HEADS].set(dt_bias)
    z3 = zf.reshape(nb, lp, zf.shape[1])
    y, s_fin = pl.pallas_call(
        functools.partial(_gdn_prompt_kernel, seq_len=seq_len),
        grid=(nc,),
        in_specs=[pl.BlockSpec((nb, BLK, hw), lambda i: (0, i, c0)),
                  pl.BlockSpec((nb, BLK, hw), lambda i: (0, i, c0 + 1)),
                  pl.BlockSpec((nb, BLK, hw), lambda i: (0, i, c0 + 2)),
                  pl.BlockSpec((nb, BLK, hw), lambda i: (0, i, g0)),
                  pl.BlockSpec((nb, BLK, LANES), lambda i: (0, i, ab0)),
                  pl.BlockSpec((CONV_W, B_QKV), lambda i: (0, 0)),
                  pl.BlockSpec((SUBLANES, LANES), lambda i: (0, 0)),
                  pl.BlockSpec((1, B_DV), lambda i: (0, 0))],
        out_specs=[pl.BlockSpec((nb, BLK, hw), lambda i: (0, i, 0)),
                   pl.BlockSpec((nb, B_HEADS, B_DK, B_DV), lambda i: (0, 0, 0, 0))],
        out_shape=[jax.ShapeDtypeStruct((nb, lp, hw), BF16),
                   jax.ShapeDtypeStruct((nb, B_HEADS, B_DK, B_DV), F32)],
        scratch_shapes=[pltpu.VMEM((nb, BLK + SUBLANES, B_QKV), F32),
                        pltpu.VMEM((nb, B_HEADS, B_DK, B_DV), F32)],
        compiler_params=_cparams(("arbitrary",)),
        name="gdn_prompt",
    )(z3, z3, z3, z3, z3, conv_w, gp, norm_g.reshape(1, B_DV))
    return y.reshape(nb * lp, hw), s_fin


def _kth_largest(count_fn, k_sel, shape):
    def step(i, tau):
        cand = tau + lax.shift_left(jnp.int32(1), 31 - i)
        cnt = count_fn(lambda key, idx: key >= cand)
        return jnp.where(cnt >= k_sel, cand, tau)

    return lax.fori_loop(0, 32, step, jnp.full(shape, INT_MIN, I32))


BITS_PER_CHECK = 4


def _kth_separator(count_fn, k_sel, n_keys):
    assert 32 % BITS_PER_CHECK == 0

    def pending(cnt_tau):
        return jnp.max(jnp.where(cnt_tau > k_sel, 1.0, 0.0)) > 0.5

    def group(carry):
        g, tau, cnt_tau, _ = carry
        for b in range(BITS_PER_CHECK):
            cand = tau + lax.shift_left(jnp.int32(1), 31 - (g * BITS_PER_CHECK + b))
            cnt = count_fn(lambda key, idx: key >= cand)
            take = cnt >= k_sel
            tau = jnp.where(take, cand, tau)
            cnt_tau = jnp.where(take, cnt, cnt_tau)
        return g + 1, tau, cnt_tau, pending(cnt_tau)

    init = (jnp.int32(0), jnp.full(n_keys.shape, INT_MIN, I32), n_keys, pending(n_keys))
    out = lax.while_loop(lambda c: (c[0] < 32 // BITS_PER_CHECK) & c[3], group, init)
    return out[1], out[2]


def _tie_cutoff(count_fn, tau, need, n_index_bits, shape):
    def step(i, jm):
        cand = jm | lax.shift_left(jnp.int32(1), n_index_bits - 1 - i)
        cnt = count_fn(lambda key, idx: (key == tau) & (idx < cand))
        return jnp.where(cnt < need, cand, jm)

    return lax.fori_loop(0, n_index_bits, step, jnp.zeros(shape, I32))


def _sparse_attn_kernel(qi_ref, kw_ref, kidx_ref, qc_ref, kc_ref, vc_ref, g0_ref, g1_ref, tiles_ref, out_ref,
                        keys_ref, qm_ref, qh_ref, tau_ref, jmax_ref, m_ref, acc_ref,
                        *, k_sel, n_index_bits, nq_real):
    qb = pl.program_id(1)

    @pl.when(qb >= nq_real)
    def _():
        out_ref[...] = jnp.zeros(out_ref.shape, out_ref.dtype)

    @pl.when(qb < nq_real)
    def _():
        lane = lax.broadcasted_iota(I32, (BLK, LANES), 1)
        key_pos = lax.broadcasted_iota(I32, (CHUNK, BLK), 0)
        qry_pos = lax.broadcasted_iota(I32, (CHUNK, BLK), 1)
        r = jnp.bitwise_and(qb, SUBS - 1)
        jd = lax.shift_right_logical(qb, int(math.log2(SUBS)))

        for p in range(IDX_HEADS // 2):
            pair = qi_ref[:, p * LANES:(p + 1) * LANES] * (IDX_DIM ** -0.5)
            swapped = pltpu.roll(pair, IDX_DIM, 1)
            for e, src in enumerate((pair, swapped)):
                h = 2 * p + e
                qm_ref[h * BLK:(h + 1) * BLK, :] = jnp.where(lane < IDX_DIM, src, 0.0).astype(BF16)
        w_rows = kw_ref[...].T * (IDX_HEADS ** -0.5)

        def scores_t(j):
            start = pl.multiple_of(j * CHUNK, CHUNK)
            d = lax.dot_general(kidx_ref[pl.ds(start, CHUNK), :], qm_ref[...], NT, preferred_element_type=F32)
            sc = jnp.zeros((CHUNK, BLK), F32)
            for h in range(IDX_HEADS):
                sc = sc + w_rows[IDX_DIM + h:IDX_DIM + h + 1, :] * jnp.maximum(d[:, h * BLK:(h + 1) * BLK], 0.0)
            return sc

        def score_pair(i, carry):
            for j in (2 * i, 2 * i + 1):
                keys_ref[j] = _sortable_key(scores_t(j))
            return carry

        lax.fori_loop(0, lax.shift_right_logical(jd, 1), score_pair, 0)

        @pl.when(jnp.bitwise_and(jd, 1) == 1)
        def _():
            keys_ref[jd - 1] = _sortable_key(scores_t(jd - 1))

        admissible = (jd * CHUNK + key_pos) <= (qb * BLK + qry_pos)
        keys_ref[jd] = _sortable_key(jnp.where(admissible, scores_t(jd), -jnp.inf))

        def count_fn(pred):
            part = SUBLANES * SUBLANES

            def body(j, acc):
                hit = jnp.where(pred(keys_ref[j], j * CHUNK + key_pos), 1.0, 0.0)
                return acc + jnp.sum(hit.reshape(CHUNK // part, part, BLK), axis=0)
            acc = lax.fori_loop(0, jd + 1, body, jnp.zeros((part, BLK), F32))
            return jnp.sum(acc, axis=0, keepdims=True).astype(I32)

        n_keys = qb * BLK + lax.broadcasted_iota(I32, (1, BLK), 1) + 1
        tau, n_ge = _kth_separator(count_fn, k_sel, n_keys)
        tau_ref[...] = tau
        jmax_ref[...] = jnp.full((1, BLK), INT_MAX, I32)
        need = k_sel - count_fn(lambda key, idx: key > tau)
        excess = jnp.max(jnp.where(n_ge > k_sel, 1.0, 0.0)) > 0.5

        @pl.when(excess)
        def _():
            jmax_ref[...] = _tie_cutoff(count_fn, tau, need, n_index_bits, (1, BLK))

        jmax = jmax_ref[...]

        for h in range(C_KV):
            for g in range(C_GROUP):
                col = (h * C_GROUP + g) * C_DH
                qh_ref[h, g * BLK:(g + 1) * BLK, :] = (
                    qc_ref[:, col:col + C_DH] * (C_DH ** -0.5 * LOG2E)).astype(BF16)
        m_ref[...] = jnp.full(m_ref.shape, NEG, F32)
        acc_ref[...] = jnp.zeros(acc_ref.shape, F32)
        ones = jnp.ones((CHUNK, C_DH), BF16)

        def run_chunks(js, first_sbs=None):
            starts = [pl.multiple_of(j * CHUNK, CHUNK) for j in js]
            masks = []
            for j in js:
                key = keys_ref[j]
                tie_ok = jnp.where((j * CHUNK + key_pos) <= jmax, 0.0, NEG)
                sel_t = jnp.where(key > tau, 0.0, jnp.where(key == tau, tie_ok, NEG))
                masks += [sel_t[i * BLK:(i + 1) * BLK, :].T for i in range(SUBS)]
            selneg = jnp.concatenate(masks, axis=1)
            selneg = jnp.concatenate([selneg] * C_GROUP, axis=0)
            ss, v1s = [], []
            for h in range(C_KV):
                cols = slice(h * C_DH, (h + 1) * C_DH)
                k = jnp.concatenate([kc_ref[pl.ds(st, CHUNK), cols] for st in starts], axis=0)
                v1s.append(jnp.concatenate(
                    [jnp.concatenate([vc_ref[pl.ds(st, CHUNK), cols], ones], axis=1) for st in starts], axis=0))
                s = lax.dot_general(qh_ref[h], k, NT, preferred_element_type=F32) + selneg
                if first_sbs is not None:
                    s = s + jnp.concatenate(
                        [tiles_ref[h, _near_bias_kind(sb + i, r)] for sb in first_sbs for i in range(SUBS)], axis=1)
                ss.append(s)
            _flash_steps(ss, v1s, [(m_ref, acc_ref, h) for h in range(C_KV)])

        _far_chunk_loop(jnp.maximum(jd - 1, 0), run_chunks)

        @pl.when(jd >= 1)
        def _():
            run_chunks([jd - 1, jd], [0, SUBS])

        @pl.when(jd == 0)
        def _():
            run_chunks([jd], [SUBS])

        half = (C_HEADS // 2) * C_DH
        for h in range(C_KV):
            a = acc_ref[h]
            o = a[:, :C_DH] / a[:, C_DH:]
            gref = g0_ref if h == 0 else g1_ref
            for g in range(C_GROUP):
                gate = gref[:, g * C_DH:(g + 1) * C_DH]
                out_ref[:, h * half + g * C_DH:h * half + (g + 1) * C_DH] = (
                    o[g * BLK:(g + 1) * BLK] * _silu(gate)).astype(BF16)


def _sparse_attn_prompt(zf, zb, near_tiles, nb, lp, nq_real, k_sel):
    nq = lp // BLK
    half = (C_HEADS // 2) * C_DH
    qi0 = ODD_OFF[4] // (IDX_HEADS * IDX_DIM)
    ki0 = ODD_OFF[5] // LANES
    kc0 = ODD_OFF[1] // (C_KV * C_DH)
    vc0 = ODD_OFF[2] // (C_KV * C_DH)
    g0 = ODD_OFF[3] // half
    rows = C_GROUP * BLK
    n_index_bits = max(1, int(math.ceil(math.log2(lp))))
    return pl.pallas_call(
        functools.partial(_sparse_attn_kernel, k_sel=k_sel, n_index_bits=n_index_bits, nq_real=nq_real),
        grid=(nb, nq),
        in_specs=[pl.BlockSpec((BLK, IDX_HEADS * IDX_DIM), lambda b, i: (b * nq + i, qi0)),
                  pl.BlockSpec((BLK, LANES), lambda b, i: (b * nq + i, ki0)),
                  pl.BlockSpec((lp, LANES), lambda b, i: (b, ki0)),
                  pl.BlockSpec((BLK, C_HEADS * C_DH), lambda b, i: (b * nq + i, 0)),
                  pl.BlockSpec((lp, C_KV * C_DH), lambda b, i: (b, kc0)),
                  pl.BlockSpec((lp, C_KV * C_DH), lambda b, i: (b, vc0)),
                  pl.BlockSpec((BLK, half), lambda b, i: (b * nq + i, g0)),
                  pl.BlockSpec((BLK, half), lambda b, i: (b * nq + i, g0 + 1)),
                  pl.BlockSpec((C_KV, 4, rows, BLK), lambda b, i: (0, 0, 0, 0))],
        out_specs=pl.BlockSpec((BLK, C_HEADS * C_DH), lambda b, i: (b * nq + i, 0)),
        out_shape=jax.ShapeDtypeStruct((nb * lp, C_HEADS * C_DH), BF16),
        scratch_shapes=[pltpu.VMEM((lp // CHUNK, CHUNK, BLK), I32),
                        pltpu.VMEM((IDX_HEADS * BLK, LANES), BF16),
                        pltpu.VMEM((C_KV, rows, C_DH), BF16),
                        pltpu.VMEM((1, BLK), I32),
                        pltpu.VMEM((1, BLK), I32),
                        pltpu.VMEM((C_KV, rows, LANES), F32),
                        pltpu.VMEM((C_KV, rows, 2 * C_DH), F32)],
        compiler_params=_cparams(("parallel", "arbitrary")),
        name="sparse_attn_prompt",
    )(zf, zf, zb, zf, zb, zb, zf, zf, near_tiles)


PAGES_PER_STEP = 32
INDEX_PAGES_PER_STEP = 64
DEC_ROWS = 16


def _interleaved_pages(refs, page):
    halves = [jnp.concatenate([r[pl.ds(h, page, stride=2), :] for r in refs], axis=0) for h in range(2)]
    return jnp.concatenate(halves, axis=1).astype(BF16)


def _paged_attn_kernel(*refs, n_pages_step, masked, page, k_feature_major):
    if masked:
        pt_ref, tau_ref, jmax_ref, selnew_ref = refs[:4]
        refs = refs[4:]
    else:
        pt_ref = refs[0]
        refs = refs[1:]
    q_ref, knew_ref, vnew_ref, bfar_ref, blast_ref, b0_ref = refs[:6]
    refs = refs[6:]
    if masked:
        keys_ref = refs[0]
        refs = refs[1:]
    k_refs = refs[:n_pages_step]
    v_refs = refs[n_pages_step:2 * n_pages_step]
    out_ref, m_ref, l_ref, acc_ref = refs[2 * n_pages_step:]
    del pt_ref
    b = pl.program_id(0)
    j = pl.program_id(1)
    width = n_pages_step * page
    q = q_ref[...]

    @pl.when(j == 0)
    def _():
        s_new = jnp.sum(q.astype(F32) * knew_ref[...].astype(BF16).astype(F32), axis=1, keepdims=True) + b0_ref[...]
        v_new = jnp.broadcast_to(vnew_ref[...].astype(BF16).astype(F32), acc_ref.shape)
        if masked:
            take = selnew_ref[b] > 0
            m_ref[...] = jnp.where(take, s_new, NEG)
            l_ref[...] = jnp.where(take, 1.0, 0.0) * jnp.ones(l_ref.shape, F32)
            acc_ref[...] = jnp.where(take, v_new, 0.0)
        else:
            m_ref[...] = s_new
            l_ref[...] = jnp.ones(l_ref.shape, F32)
            acc_ref[...] = v_new

    if k_feature_major:
        kcat = jnp.concatenate([r[...] for r in k_refs], axis=1).astype(BF16)
        s = jnp.dot(q, kcat, preferred_element_type=F32)
    else:
        s = lax.dot_general(q, _interleaved_pages(k_refs, page), NT, preferred_element_type=F32)
    vcat = _interleaved_pages(v_refs, page)
    s = s + jnp.where(j == pl.num_programs(1) - 1, blast_ref[...], bfar_ref[...])
    if masked:
        key = keys_ref[...]
        idx = j * width + lax.broadcasted_iota(I32, (1, width), 1)
        tau = tau_ref[b]
        sel = (key > tau) | ((key == tau) & (idx <= jmax_ref[b]))
        s = jnp.where(sel, s, NEG)
    m_old = m_ref[...]
    m_new = jnp.maximum(m_old, jnp.max(s, axis=1, keepdims=True))
    alpha = jnp.exp(m_old - m_new)
    p = jnp.exp(s - m_new[:, 0:1])
    l_ref[...] = alpha * l_ref[...] + jnp.sum(p, axis=1, keepdims=True)
    acc_ref[...] = _lane_tile(alpha, 2) * acc_ref[...] + jnp.dot(p.astype(BF16), vcat, preferred_element_type=F32)
    m_ref[...] = m_new

    @pl.when(j == pl.num_programs(1) - 1)
    def _():
        o = acc_ref[...]
        rowi = lax.broadcasted_iota(I32, (DEC_ROWS, LANES), 0)
        upper = (rowi >= DEC_ROWS // 2) if not masked else ((rowi >= DEC_ROWS // 4) & (rowi < DEC_ROWS // 2))
        out_ref[...] = jnp.where(upper, o[:, LANES:], o[:, :LANES]) / l_ref[...]


def _pages_per_step(n_pages, g=PAGES_PER_STEP):
    while n_pages % g:
        g //= 2
    return g


def _paged_attention(qprime, k_new, v_new, bias_far, bias_last, bias0, k_cache, v_cache, page_table, page,
                     k_feature_major, mask_args=None):
    db = qprime.shape[0]
    n_pages = page_table.shape[1]
    width = qprime.shape[2]
    g = _pages_per_step(n_pages)
    n_steps = n_pages // g
    masked = mask_args is not None
    n_pref = 4 if masked else 1

    def page_map(gi):
        return lambda b, j, pt, *_: (pt[b * n_pages + j * g + gi], 0, 0)

    in_specs = [pl.BlockSpec((None, DEC_ROWS, width), lambda b, j, *_: (b, 0, 0)),
                pl.BlockSpec((None, 1, width), lambda b, j, *_: (b, 0, 0)),
                pl.BlockSpec((None, 1, width), lambda b, j, *_: (b, 0, 0)),
                pl.BlockSpec((DEC_ROWS, g * page), lambda b, j, *_: (0, 0)),
                pl.BlockSpec((DEC_ROWS, g * page), lambda b, j, *_: (0, 0)),
                pl.BlockSpec((DEC_ROWS, LANES), lambda b, j, *_: (0, 0))]
    args = [qprime, k_new, v_new, bias_far, bias_last, bias0]
    prefetch = [page_table.reshape(-1)]
    if masked:
        keys, tau, jmax, selnew = mask_args
        prefetch += [tau, jmax, selnew]
        in_specs += [pl.BlockSpec((None, 1, g * page), lambda b, j, *_: (b, 0, j))]
        args += [keys]
    in_specs += [pl.BlockSpec((None,) + k_cache.shape[1:], page_map(gi)) for gi in range(g)]
    in_specs += [pl.BlockSpec((None,) + v_cache.shape[1:], page_map(gi)) for gi in range(g)]
    args += [k_cache] * g + [v_cache] * g
    grid_spec = pltpu.PrefetchScalarGridSpec(
        num_scalar_prefetch=n_pref,
        grid=(db, n_steps),
        in_specs=in_specs,
        out_specs=pl.BlockSpec((None, DEC_ROWS, LANES), lambda b, j, *_: (b, 0, 0)),
        scratch_shapes=[pltpu.VMEM((DEC_ROWS, LANES), F32),
                        pltpu.VMEM((DEC_ROWS, LANES), F32),
                        pltpu.VMEM((DEC_ROWS, width), F32)])
    return pl.pallas_call(
        functools.partial(_paged_attn_kernel, n_pages_step=g, masked=masked, page=page,
                          k_feature_major=k_feature_major),
        grid_spec=grid_spec,
        out_shape=jax.ShapeDtypeStruct((db, DEC_ROWS, LANES), F32),
        compiler_params=_cparams(("parallel", "arbitrary")),
        name="paged_attn_masked" if masked else "paged_attn",
    )(*prefetch, *args)


def _paged_index_kernel(pt_ref, q_ref, w_ref, knew_ref, *refs, n_pages_step, page):
    k_refs = refs[:n_pages_step]
    out_ref = refs[n_pages_step]
    del pt_ref
    j = pl.program_id(1)
    last = pl.num_programs(1) - 1
    q = q_ref[...]
    w = w_ref[...]

    def score(kmat_t):
        d = jnp.dot(q, kmat_t.astype(BF16), preferred_element_type=F32)
        return jnp.sum(jnp.maximum(d, 0.0) * w[:, 0:1], axis=0, keepdims=True)

    @pl.when(j < last)
    def _():
        kcat = jnp.concatenate([r[...] for r in k_refs], axis=1)
        out_ref[...] = _sortable_key(score(kcat))

    @pl.when(j == last)
    def _():
        sc = score(knew_ref[...])
        lane = lax.broadcasted_iota(I32, (1, page), 1)
        sc = jnp.where(lane == 0, sc, -jnp.inf)
        pad = jnp.full((1, (n_pages_step - 1) * page), -jnp.inf, F32)
        full = jnp.concatenate([sc, pad], axis=1) if n_pages_step > 1 else sc
        out_ref[...] = _sortable_key(full)


def _paged_index_scores(qidx, wcol, k_new_tile, idx_cache_t, page_table):
    db = qidx.shape[0]
    n_pages = page_table.shape[1]
    page = idx_cache_t.shape[2]
    g = _pages_per_step(n_pages, INDEX_PAGES_PER_STEP)
    n_steps = n_pages // g

    def page_map(gi):
        return lambda b, j, pt: (pt[b * n_pages + jnp.minimum(j, n_steps - 1) * g + gi], 0, 0)

    grid_spec = pltpu.PrefetchScalarGridSpec(
        num_scalar_prefetch=1,
        grid=(db, n_steps + 1),
        in_specs=[pl.BlockSpec((None, IDX_HEADS, IDX_DIM), lambda b, j, pt: (b, 0, 0)),
                  pl.BlockSpec((None, IDX_HEADS, LANES), lambda b, j, pt: (b, 0, 0)),
                  pl.BlockSpec((None, IDX_DIM, page), lambda b, j, pt: (b, 0, 0))]
        + [pl.BlockSpec((None, IDX_DIM, page), page_map(gi)) for gi in range(g)],
        out_specs=pl.BlockSpec((None, 1, g * page), lambda b, j, pt: (b, 0, j)))
    return pl.pallas_call(
        functools.partial(_paged_index_kernel, n_pages_step=g, page=page),
        grid_spec=grid_spec,
        out_shape=jax.ShapeDtypeStruct((db, 1, (n_steps + 1) * g * page), I32),
        compiler_params=_cparams(("parallel", "arbitrary")),
        name="paged_index_scores",
    )(page_table.reshape(-1), qidx, wcol, k_new_tile, *([idx_cache_t] * g))


def _select_kernel(keys_ref, tau_ref, jmax_ref, *, k_sel, n_index_bits):
    keys = keys_ref[...]
    idx = lax.broadcasted_iota(I32, keys.shape, 1)
    shape = (keys.shape[0], 1)

    def count_fn(pred):
        return jnp.sum(jnp.where(pred(keys, idx), 1, 0), axis=1, keepdims=True)

    tau = _kth_largest(count_fn, k_sel, shape)
    need = k_sel - count_fn(lambda key, i: key > tau)
    jmax = _tie_cutoff(count_fn, tau, need, n_index_bits, shape)
    tau_ref[...] = jnp.broadcast_to(tau, tau_ref.shape)
    jmax_ref[...] = jnp.broadcast_to(jmax, jmax_ref.shape)


def _select_rows(keys2d, k_sel):
    rows, width = keys2d.shape
    n_index_bits = max(1, int(math.ceil(math.log2(width))))
    return pl.pallas_call(
        functools.partial(_select_kernel, k_sel=k_sel, n_index_bits=n_index_bits),
        out_shape=[jax.ShapeDtypeStruct((rows, LANES), I32), jax.ShapeDtypeStruct((rows, LANES), I32)],
        compiler_params=pltpu.CompilerParams(vmem_limit_bytes=VMEM_LIMIT),
        name="select_rows",
    )(keys2d)


def _even_tail_kernel(z_ref, conv_ref, s_ref, oa_ref, convw_ref, gp_ref, normg_ref, lamv_ref, subln_ref,
                      ya_ref, yb_ref, snew_ref, *, lam_init):
    z = z_ref[...]
    hw = B_HEADS * B_DK
    lv = lamv_ref[...]
    lam = (jnp.exp(jnp.sum(lv[0:1] * lv[1:2], axis=1, keepdims=True))
           - jnp.exp(jnp.sum(lv[2:3] * lv[3:4], axis=1, keepdims=True)) + lam_init)
    oa = oa_ref[...]
    for hg in range(A_HEADS):
        o = oa[2 * hg:2 * hg + 1] - lam * oa[2 * hg + 1:2 * hg + 2]
        y = o * lax.rsqrt(jnp.mean(o * o, axis=-1, keepdims=True) + EPS)
        y = (y * subln_ref[...]) * (1.0 - lam_init)
        gate = z[:, EVEN_OFF[3] + hg * A_DV:EVEN_OFF[3] + (hg + 1) * A_DV]
        ya_ref[:, hg * A_DV:(hg + 1) * A_DV] = (y * _silu(gate)).astype(BF16)
    x_new = z[:, EVEN_OFF[4]:EVEN_OFF[4] + B_QKV]
    conv = convw_ref[CONV_W - 1:CONV_W, :] * x_new
    cp = conv_ref[...]
    for i in range(CONV_W - 1):
        conv = conv + convw_ref[i:i + 1, :] * cp[i:i + 1, :]
    act = _silu(conv)
    ab = z[:, EVEN_OFF[6]:EVEN_OFF[6] + LANES]
    gp = gp_ref[...]
    row = lax.broadcasted_iota(I32, (B_DK, B_DV), 0)
    col = lax.broadcasted_iota(I32, (B_DK, B_DV), 1)
    eye = row == col
    for h in range(B_HEADS):
        q = act[:, h * B_DK:(h + 1) * B_DK]
        k = act[:, hw + h * B_DK:hw + (h + 1) * B_DK]
        v = act[:, 2 * hw + h * B_DV:2 * hw + (h + 1) * B_DV]
        q = q * lax.rsqrt(jnp.sum(q * q, axis=-1, keepdims=True) + EPS) * (B_DK ** -0.5)
        k = k * lax.rsqrt(jnp.sum(k * k, axis=-1, keepdims=True) + EPS)
        g = -jnp.exp(gp[0:1, h:h + 1]) * _softplus(ab[:, h:h + 1] + gp[1:2, h:h + 1])
        beta = jax.nn.sigmoid(ab[:, B_HEADS + h:B_HEADS + h + 1])
        eg = jnp.exp(g)
        s0 = s_ref[h]
        kcol = jnp.sum(jnp.where(eye, jnp.broadcast_to(k, (B_DK, B_DK)), 0.0), axis=1, keepdims=True)
        qcol = jnp.sum(jnp.where(eye, jnp.broadcast_to(q, (B_DK, B_DK)), 0.0), axis=1, keepdims=True)
        ks = jnp.sum(kcol * s0, axis=0, keepdims=True)
        qs = jnp.sum(qcol * s0, axis=0, keepdims=True)
        u = beta * (v - eg * ks)
        qk = jnp.sum(q * k, axis=1, keepdims=True)
        o = eg * qs + qk * u
        snew_ref[h] = eg * s0 + kcol * u
        y = o * lax.rsqrt(jnp.mean(o * o, axis=-1, keepdims=True) + EPS) * normg_ref[...]
        gate = z[:, EVEN_OFF[5] + h * B_DV:EVEN_OFF[5] + (h + 1) * B_DV]
        yb_ref[:, h * B_DV:(h + 1) * B_DV] = (y * _silu(gate)).astype(BF16)


def _even_tail(zf_s, conv_prev, s_prev, oa, conv_w, a_log, dt_bias, norm_g, lamv, subln, lam_init):
    db, npad = zf_s.shape
    gp = jnp.zeros((SUBLANES, LANES), F32).at[0, :B_HEADS].set(a_log).at[1, :B_HEADS].set(dt_bias)
    hw = B_HEADS * B_DV
    return pl.pallas_call(
        functools.partial(_even_tail_kernel, lam_init=lam_init),
        grid=(db,),
        in_specs=[pl.BlockSpec((None, 1, npad), lambda b: (b, 0, 0)),
                  pl.BlockSpec((None, CONV_W - 1, B_QKV), lambda b: (b, 0, 0)),
                  pl.BlockSpec((None, B_HEADS, B_DK, B_DV), lambda b: (b, 0, 0, 0)),
                  pl.BlockSpec((None, DEC_ROWS, LANES), lambda b: (b, 0, 0)),
                  pl.BlockSpec((CONV_W, B_QKV), lambda b: (0, 0)),
                  pl.BlockSpec((SUBLANES, LANES), lambda b: (0, 0)),
                  pl.BlockSpec((1, B_DV), lambda b: (0, 0)),
                  pl.BlockSpec((4, A_DH), lambda b: (0, 0)),
                  pl.BlockSpec((1, A_DV), lambda b: (0, 0))],
        out_specs=[pl.BlockSpec((None, 1, A_HEADS * A_DV), lambda b: (b, 0, 0)),
                   pl.BlockSpec((None, 1, hw), lambda b: (b, 0, 0)),
                   pl.BlockSpec((None, B_HEADS, B_DK, B_DV), lambda b: (b, 0, 0, 0))],
        out_shape=[jax.ShapeDtypeStruct((db, 1, A_HEADS * A_DV), BF16),
                   jax.ShapeDtypeStruct((db, 1, hw), BF16),
                   jax.ShapeDtypeStruct((db, B_HEADS, B_DK, B_DV), F32)],
        compiler_params=_cparams(("parallel",)),
        name="even_tail",
    )(zf_s.reshape(db, 1, npad), conv_prev, s_prev, oa, conv_w, gp, norm_g.reshape(1, B_DV), lamv,
      subln.reshape(1, A_DV))


def _odd_tail_kernel(z_ref, oc_ref, y_ref):
    z = z_ref[...]
    oc = oc_ref[...]
    for hg in range(C_HEADS):
        gate = z[:, ODD_OFF[3] + hg * C_DH:ODD_OFF[3] + (hg + 1) * C_DH]
        y_ref[:, hg * C_DH:(hg + 1) * C_DH] = (oc[hg:hg + 1] * _silu(gate)).astype(BF16)


def _odd_tail(zf_s, oc):
    db, npad = zf_s.shape
    return pl.pallas_call(
        _odd_tail_kernel,
        grid=(db,),
        in_specs=[pl.BlockSpec((None, 1, npad), lambda b: (b, 0, 0)),
                  pl.BlockSpec((None, DEC_ROWS, LANES), lambda b: (b, 0, 0))],
        out_specs=pl.BlockSpec((None, 1, C_HEADS * C_DH), lambda b: (b, 0, 0)),
        out_shape=jax.ShapeDtypeStruct((db, 1, C_HEADS * C_DH), BF16),
        compiler_params=_cparams(("parallel",)),
        name="odd_tail",
    )(zf_s.reshape(db, 1, npad), oc)


def _bias_by_distance(table):
    n = jnp.arange(FAR_DIST + 1)
    exact = N_BUCKETS // 2
    nf = jnp.maximum(n, 1).astype(F32)
    large = exact + (jnp.log(nf / exact) / math.log(MAX_DIST / exact) * (N_BUCKETS - exact)).astype(I32)
    bucket = jnp.where(n < exact, n, jnp.minimum(large, N_BUCKETS - 1))
    return table[bucket].astype(F32)


def _prompt_near_tiles(bd, group):
    assert FAR_DIST <= BLK
    heads = bd.shape[1]
    rel = ((bd - bd[FAR_DIST][None, :]) * LOG2E).T
    f = jnp.concatenate([jnp.full((heads, BLK - 1), NEG, F32), rel,
                         jnp.broadcast_to(rel[:, FAR_DIST:], (heads, 2 * BLK - 1 - FAR_DIST))], axis=1)
    period = 3 * BLK
    g = jnp.pad(f[:, ::-1], ((0, 0), (0, period - f.shape[1])))
    wrapped = jnp.tile(g, (1, BLK + 1))[:, :BLK * (period + 1)].reshape(heads, BLK, period + 1)
    strip = wrapped[:, ::-1, :2 * BLK]
    kinds = [None] * 4
    kinds[TILE_ZERO] = jnp.zeros((heads, BLK, BLK), F32)
    kinds[TILE_PREV] = strip[:, :, :BLK]
    kinds[TILE_DIAG] = strip[:, :, BLK:]
    kinds[TILE_MASKED] = jnp.full((heads, BLK, BLK), NEG, F32)
    tiles = jnp.stack(kinds, axis=1)
    n_kv = heads // group
    tiles = tiles.reshape(n_kv, group, 4, BLK, BLK)
    return jnp.transpose(tiles, (0, 2, 1, 3, 4)).reshape(n_kv, 4, group * BLK, BLK)


def _decode_bias(bd, row_heads, past, page, g):
    heads = jnp.asarray(row_heads, I32)
    far = jnp.broadcast_to(bd[FAR_DIST][heads][:, None], (len(row_heads), g * page))
    pos = past - g * page + jnp.arange(g * page)
    dist = jnp.minimum(past - pos, FAR_DIST)
    last = bd[dist][:, heads].T
    new = jnp.broadcast_to(bd[0][heads][:, None], (len(row_heads), LANES))
    return far.astype(F32), last.astype(F32), new.astype(F32)


def _pad_cols(w, mult):
    n = w.shape[1]
    return jnp.pad(w, ((0, 0), (0, _round_up(n, mult) - n)))


def kernel(x_prompt, x_sample, cache_a_k, cache_a_v, state_b_s, state_b_conv, cache_c_k, cache_c_v, cache_c_idx,
           page_table, meta, bias_table, final_norm, norm_e, w_in_e, w_out_e, lam_q1, lam_k1, lam_q2, lam_k2,
           subln_a, conv_b, a_log_b, dt_bias_b, norm_b, norm_o, w_in_o, w_out_o):
    nb, seq, d = x_prompt.shape
    n_meta = meta.shape[0]
    l = seq + n_meta
    lp = _round_up(l, CHUNK)
    nq_real = pl.cdiv(l, BLK)
    db = x_sample.shape[0]
    n_pages = page_table.shape[1]
    page = cache_a_k.shape[2]
    past = n_pages * page
    n_pool = cache_a_k.shape[1]
    assert x_sample.shape[1] == 1 and norm_e.shape[0] == 1 and norm_o.shape[0] == 1
    lam_init = 0.8 - 0.6 * math.exp(-0.3 * 0)

    tn_e, tn_o = 8 * LANES, 5 * LANES
    w_e = _pad_cols(w_in_e[0], tn_e).astype(BF16)
    w_o = _pad_cols(w_in_o[0], tn_o).astype(BF16)
    w_out_a = w_out_e[0][:A_HEADS * A_DV].astype(BF16)
    w_out_b = w_out_e[0][A_HEADS * A_DV:].astype(BF16)
    w_out_c = w_out_o[0].astype(BF16)
    tm = CHUNK
    tm_proj = 2 * CHUNK if (nb * lp) % (2 * CHUNK) == 0 else CHUNK
    dbp = _round_up(db, SUBLANES)

    bd = _bias_by_distance(bias_table)
    tiles_a = _prompt_near_tiles(bd, A_GROUP)
    tiles_c = tiles_a if (A_KV, A_GROUP) == (C_KV, C_GROUP) else _prompt_near_tiles(bd, C_GROUP)
    lamv = jnp.stack([lam_q1[0], lam_k1[0], lam_q2[0], lam_k2[0]]).astype(F32)

    hp = jnp.concatenate([jnp.broadcast_to(meta.astype(F32)[None], (nb, n_meta, d)), x_prompt], axis=1)
    hp = jnp.pad(hp, ((0, 0), (0, lp - l), (0, 0))).reshape(nb * lp, d)
    kv_tile = EVEN_OFF[1] // tn_e
    assert EVEN_OFF[3] <= (kv_tile + 1) * tn_e
    zf, zb = _norm_proj(hp, norm_e[0], w_e, tm_proj, tn_e, (kv_tile, kv_tile + 1))
    ya = _diff_attn_prompt(zf, zb, kv_tile * tn_e, tiles_a, lamv, subln_a[0], nb, lp, nq_real, lam_init)
    yb, pb_s = _gdn_prompt(zf, conv_b[0], a_log_b[0], dt_bias_b[0], norm_b[0], nb, lp, l)
    h1 = _out_proj([ya, yb], [w_out_a, w_out_b], hp, tm)
    z3 = zf.reshape(nb, lp, -1)
    pa_k = z3[:, :l, EVEN_OFF[1]:EVEN_OFF[2]].reshape(1, nb, l, A_KV, 2, A_DH)
    pa_v = z3[:, :l, EVEN_OFF[2]:EVEN_OFF[3]].reshape(1, nb, l, A_KV, A_DV)
    pb_conv = z3[:, l - (CONV_W - 1):l, EVEN_OFF[4]:EVEN_OFF[5]][None]

    zf1, zb1 = _norm_proj(h1, norm_o[0], w_o, tm_proj, tn_o, (0, w_o.shape[1] // tn_o))
    k_sel_p = min(TOPK_MAX, l // 4)
    yc = _sparse_attn_prompt(zf1, zb1, tiles_c, nb, lp, nq_real, k_sel_p)
    yp = _out_proj([yc], [w_out_c], h1, tm, final_gain=final_norm)
    y_prompt = yp.reshape(nb, lp, d)[:, n_meta:l]
    z13 = zf1.reshape(nb, lp, -1)
    pc_k = z13[:, :l, ODD_OFF[1]:ODD_OFF[2]].reshape(1, nb, l, C_KV, C_DH)
    pc_v = z13[:, :l, ODD_OFF[2]:ODD_OFF[3]].reshape(1, nb, l, C_KV, C_DH)
    pc_idx = z13[:, :l, ODD_OFF[5]:ODD_OFF[6]][None]

    hs = jnp.pad(x_sample.reshape(db, d), ((0, dbp - db), (0, 0)))
    zs, _ = _norm_proj(hs, norm_e[0], w_e, dbp, tn_e, (0, 1))
    zs = zs[:db]
    qa = zs[:, :EVEN_OFF[1]].reshape(db, A_KV, A_GROUP, 2, A_DH) * (A_DH ** -0.5)
    qprime = jnp.einsum('bhgcd,hi,cj->bhgcijd', qa, jnp.eye(A_KV, dtype=F32), jnp.eye(2, dtype=F32))
    qprime = qprime.reshape(db, DEC_ROWS, A_KV * 2 * A_DH).astype(BF16)
    k_new = zs[:, EVEN_OFF[1]:EVEN_OFF[2]].reshape(db, 1, -1)
    v_new = zs[:, EVEN_OFF[2]:EVEN_OFF[3]].reshape(db, 1, -1)
    g_dec = _pages_per_step(n_pages)
    rows_a = [r // 2 for r in range(DEC_ROWS)]
    bfar, blast, bnew = _decode_bias(bd, rows_a, past, page, g_dec)
    ak_t = jnp.transpose(cache_a_k[0], (0, 2, 3, 4, 1)).reshape(n_pool, A_KV * 2 * A_DH, page)
    av_r = cache_a_v[0].reshape(n_pool, page * A_KV, A_DV)
    oa = _paged_attention(qprime, k_new, v_new, bfar, blast, bnew, ak_t, av_r, page_table, page,
                          k_feature_major=True)
    ya_s, yb_s, sb_s = _even_tail(zs, state_b_conv[0], state_b_s[0], oa, conv_b[0], a_log_b[0], dt_bias_b[0],
                                  norm_b[0], lamv, subln_a[0], lam_init)
    hs_pad = lambda y: jnp.pad(y.reshape(db, -1), ((0, dbp - db), (0, 0)))
    hs1 = _out_proj([hs_pad(ya_s), hs_pad(yb_s)], [w_out_a, w_out_b], hs, dbp)
    sa_k = zs[:, EVEN_OFF[1]:EVEN_OFF[2]].reshape(1, db, 1, A_KV, 2, A_DH)
    sa_v = zs[:, EVEN_OFF[2]:EVEN_OFF[3]].reshape(1, db, 1, A_KV, A_DV)
    sb_conv = jnp.concatenate([state_b_conv[0][:, 1:], zs[:, None, EVEN_OFF[4]:EVEN_OFF[5]]], axis=1)[None]

    zs1, _ = _norm_proj(hs1, norm_o[0], w_o, dbp, tn_o, (0, 1))
    zs1 = zs1[:db]
    qidx = (zs1[:, ODD_OFF[4]:ODD_OFF[5]].reshape(db, IDX_HEADS, IDX_DIM) * (IDX_DIM ** -0.5)).astype(BF16)
    wcol = jnp.broadcast_to((zs1[:, ODD_OFF[6]:ODD_OFF[7]] * (IDX_HEADS ** -0.5))[:, :, None],
                            (db, IDX_HEADS, LANES))
    ki_new = zs1[:, ODD_OFF[5]:ODD_OFF[6]]
    ki_tile = jnp.zeros((db, IDX_DIM, page), F32).at[:, :, 0].set(ki_new)
    ci_t = jnp.transpose(cache_c_idx[0], (0, 2, 1))
    keys = _paged_index_scores(qidx, wcol, ki_tile, ci_t, page_table)
    k_sel_s = min(TOPK_MAX, (past + 1) // 4)
    tau, jmax = _select_rows(keys.reshape(db, -1), k_sel_s)
    tau, jmax = tau[:, 0], jmax[:, 0]
    key_new = keys[:, 0, past]
    selnew = ((key_new > tau) | ((key_new == tau) & (past <= jmax))).astype(I32)
    qc = zs1[:, :ODD_OFF[1]].reshape(db, C_KV, C_GROUP, C_DH) * (C_DH ** -0.5)
    qcp = jnp.einsum('bhgd,hi->bhgid', qc, jnp.eye(C_KV, dtype=F32)).reshape(db, C_HEADS, C_KV * C_DH)
    qcp = jnp.pad(qcp, ((0, 0), (0, DEC_ROWS - C_HEADS), (0, 0))).astype(BF16)
    kc_new = zs1[:, ODD_OFF[1]:ODD_OFF[2]].reshape(db, 1, -1)
    vc_new = zs1[:, ODD_OFF[2]:ODD_OFF[3]].reshape(db, 1, -1)
    rows_c = [r if r < C_HEADS else 0 for r in range(DEC_ROWS)]
    cfar, clast, cnew = _decode_bias(bd, rows_c, past, page, g_dec)
    ck_r = cache_c_k[0].reshape(n_pool, page * C_KV, C_DH)
    cv_r = cache_c_v[0].reshape(n_pool, page * C_KV, C_DH)
    oc = _paged_attention(qcp, kc_new, vc_new, cfar, clast, cnew, ck_r, cv_r, page_table, page,
                          k_feature_major=False, mask_args=(keys, tau, jmax, selnew))
    yc_s = _odd_tail(zs1, oc)
    ys = _out_proj([hs_pad(yc_s)], [w_out_c], hs1, dbp, final_gain=final_norm)
    y_sample = ys[:db].reshape(db, 1, d)
    sc_k = zs1[:, ODD_OFF[1]:ODD_OFF[2]].reshape(1, db, 1, C_KV, C_DH)
    sc_v = zs1[:, ODD_OFF[2]:ODD_OFF[3]].reshape(1, db, 1, C_KV, C_DH)
    sc_idx = zs1[:, None, ODD_OFF[5]:ODD_OFF[6]][None]

    return (y_prompt, y_sample, pa_k, pa_v, pb_s[None], pb_conv, pc_k, pc_v, pc_idx,
            sa_k, sa_v, sb_s[None], sb_conv, sc_k, sc_v, sc_idx)
```

```python
import functools
import math

import jax
import jax.numpy as jnp
import numpy as np
from jax import lax
from jax.experimental import pallas as pl
from jax.experimental.pallas import tpu as pltpu

F32 = jnp.float32
BF16 = jnp.bfloat16
I32 = jnp.int32

EPS = 1e-6
N_BUCKETS = 32
MAX_DIST = 128
FAR_DIST = MAX_DIST

A_HEADS, A_KV, A_GROUP, A_DH, A_DV = 8, 2, 4, 64, 128
B_HEADS, B_DK, B_DV, CONV_W = 4, 128, 128, 4
C_HEADS, C_KV, C_GROUP, C_DH = 8, 2, 4, 128
IDX_HEADS, IDX_DIM, TOPK_MAX = 8, 64, 256
B_QKV = 2 * B_HEADS * B_DK + B_HEADS * B_DV

EVEN_COLS = (A_HEADS * 2 * A_DH, A_KV * 2 * A_DH, A_KV * A_DV, A_HEADS * A_DV, B_QKV, B_HEADS * B_DV, B_HEADS, B_HEADS)
ODD_COLS = (C_HEADS * C_DH, C_KV * C_DH, C_KV * C_DH, C_HEADS * C_DH, IDX_HEADS * IDX_DIM, IDX_DIM, IDX_HEADS)
EVEN_OFF = tuple(int(v) for v in np.cumsum((0,) + EVEN_COLS))
ODD_OFF = tuple(int(v) for v in np.cumsum((0,) + ODD_COLS))

LANES = 128
SUBLANES = 8
BLK = 128
CHUNK = 512
SUBS = CHUNK // BLK
NEG = -1e30
INT_MIN = -2 ** 31
INT_MAX = 2 ** 31 - 1
LOG2E = math.log2(math.e)
VMEM_LIMIT = 56 * 1024 * 1024
HI = lax.Precision.HIGHEST

NT = (((1,), (1,)), ((), ()))


def _round_up(x, m):
    return (x + m - 1) // m * m


def _cparams(sem):
    return pltpu.CompilerParams(dimension_semantics=sem, vmem_limit_bytes=VMEM_LIMIT)


def _silu(x):
    return x * jax.nn.sigmoid(x)


def _sortable_key(score):
    score = jnp.where(score == 0.0, 0.0, score)
    bits = pltpu.bitcast(score, I32)
    return bits ^ ((bits >> 31) & jnp.int32(0x7FFFFFFF))


def _lane_tile(x, n):
    return jnp.concatenate([x] * n, axis=1)


def _proj_kernel(x_ref, g_ref, w_ref, of_ref, ob_ref, xn_ref, *, bf16_tiles):
    j = pl.program_id(1)

    @pl.when(j == 0)
    def _():
        xf = x_ref[...]
        y = xf * lax.rsqrt(jnp.mean(xf * xf, axis=-1, keepdims=True) + EPS)
        xn_ref[...] = (y * g_ref[...]).astype(BF16)

    acc = jnp.dot(xn_ref[...], w_ref[...], preferred_element_type=F32)
    of_ref[...] = acc

    @pl.when((j >= bf16_tiles[0]) & (j < bf16_tiles[1]))
    def _():
        ob_ref[...] = acc.astype(BF16)


def _norm_proj(x, gain, w_bf16, tm, tn, bf16_tiles):
    m, d = x.shape
    n = w_bf16.shape[1]
    t0, t1 = bf16_tiles
    return pl.pallas_call(
        functools.partial(_proj_kernel, bf16_tiles=bf16_tiles),
        grid=(m // tm, n // tn),
        in_specs=[pl.BlockSpec((tm, d), lambda i, j: (i, 0)),
                  pl.BlockSpec((1, d), lambda i, j: (0, 0)),
                  pl.BlockSpec((d, tn), lambda i, j: (0, j))],
        out_specs=[pl.BlockSpec((tm, tn), lambda i, j: (i, j)),
                   pl.BlockSpec((tm, tn), lambda i, j: (i, jnp.clip(j - t0, 0, t1 - t0 - 1)))],
        out_shape=[jax.ShapeDtypeStruct((m, n), F32), jax.ShapeDtypeStruct((m, (t1 - t0) * tn), BF16)],
        scratch_shapes=[pltpu.VMEM((tm, d), BF16)],
        compiler_params=_cparams(("parallel", "arbitrary")),
        name="norm_proj",
    )(x, gain.reshape(1, d), w_bf16)


def _outproj_kernel(*refs, n_lhs, final_norm):
    ys = refs[:n_lhs]
    ws = refs[n_lhs:2 * n_lhs]
    h_ref = refs[2 * n_lhs]
    pos = 2 * n_lhs + 1
    fn_ref = refs[pos] if final_norm else None
    out_ref = refs[-1]
    acc = h_ref[...]
    for y_ref, w_ref in zip(ys, ws):
        acc = acc + jnp.dot(y_ref[...], w_ref[...], preferred_element_type=F32)
    if final_norm:
        y = acc * lax.rsqrt(jnp.mean(acc * acc, axis=-1, keepdims=True) + EPS)
        acc = y * fn_ref[...]
    out_ref[...] = acc


def _out_proj(ys, ws, h, tm, final_gain=None):
    m, n = h.shape
    n_lhs = len(ys)
    in_specs = [pl.BlockSpec((tm, y.shape[1]), lambda i: (i, 0)) for y in ys]
    in_specs += [pl.BlockSpec(w.shape, lambda i: (0, 0)) for w in ws]
    in_specs += [pl.BlockSpec((tm, n), lambda i: (i, 0))]
    args = list(ys) + list(ws) + [h]
    if final_gain is not None:
        in_specs += [pl.BlockSpec((1, n), lambda i: (0, 0))]
        args += [final_gain.reshape(1, n)]
    return pl.pallas_call(
        functools.partial(_outproj_kernel, n_lhs=n_lhs, final_norm=final_gain is not None),
        grid=(m // tm,),
        in_specs=in_specs,
        out_specs=pl.BlockSpec((tm, n), lambda i: (i, 0)),
        out_shape=jax.ShapeDtypeStruct((m, n), F32),
        compiler_params=_cparams(("parallel",)),
        name="out_proj",
    )(*args)


def _flash_steps(ss, v1s, m_refs):
    each = range(len(ss))
    m_old = [m_refs[i][0][m_refs[i][2]] for i in each]
    m_new = [jnp.maximum(m_old[i], jnp.max(ss[i], axis=1, keepdims=True)) for i in each]
    alpha = [jnp.exp2(m_old[i] - m_new[i]) for i in each]
    p = [jnp.exp2(ss[i] - _lane_tile(m_new[i], ss[i].shape[1] // LANES)).astype(BF16) for i in each]
    pv = [jnp.dot(p[i], v1s[i], preferred_element_type=F32) for i in each]
    for i in each:
        m_ref, acc_ref, idx = m_refs[i]
        acc_ref[idx] = _lane_tile(alpha[i], 2) * acc_ref[idx] + pv[i]
        m_ref[idx] = m_new[i]


def _far_chunk_loop(n_far, run_chunks):
    def quad(i, carry):
        run_chunks([4 * i, 4 * i + 1, 4 * i + 2, 4 * i + 3])
        return carry

    n_quad = lax.shift_right_logical(n_far, 2)
    lax.fori_loop(0, n_quad, quad, 0)

    @pl.when(jnp.bitwise_and(n_far, 2) == 2)
    def _():
        run_chunks([4 * n_quad, 4 * n_quad + 1])

    @pl.when(jnp.bitwise_and(n_far, 1) == 1)
    def _():
        run_chunks([n_far - 1])


TILE_ZERO, TILE_PREV, TILE_DIAG, TILE_MASKED = range(4)


def _near_bias_kind(sb, r):
    rel = sb - SUBS - r
    return jnp.where(rel == 0, TILE_DIAG, jnp.where(rel == -1, TILE_PREV, jnp.where(rel < -1, TILE_ZERO, TILE_MASKED)))


def _diff_attn_kernel(q_ref, k_ref, v_ref, gate_ref, tiles_ref, lamv_ref, subln_ref, out_ref,
                      qm_ref, m_ref, acc_ref, *, lam_init, nq_real):
    qi = pl.program_id(2)

    @pl.when(qi >= nq_real)
    def _():
        out_ref[...] = jnp.zeros(out_ref.shape, out_ref.dtype)

    @pl.when(qi < nq_real)
    def _():
        lane = lax.broadcasted_iota(I32, (BLK, 2 * A_DH), 1)
        for g in range(A_GROUP):
            qg = q_ref[:, g * 2 * A_DH:(g + 1) * 2 * A_DH] * (A_DH ** -0.5 * LOG2E)
            for c in range(2):
                keep = (lane < A_DH) if c == 0 else (lane >= A_DH)
                qm_ref[c, g * BLK:(g + 1) * BLK, :] = jnp.where(keep, qg, 0.0).astype(BF16)
        m_ref[...] = jnp.full(m_ref.shape, NEG, F32)
        acc_ref[...] = jnp.zeros(acc_ref.shape, F32)
        r = jnp.bitwise_and(qi, SUBS - 1)
        jd = lax.shift_right_logical(qi, int(math.log2(SUBS)))
        ones = jnp.ones((CHUNK, A_DV), BF16)

        def run_chunks(js, first_sbs=None):
            starts = [pl.multiple_of(j * CHUNK, CHUNK) for j in js]
            k = jnp.concatenate([k_ref[pl.ds(st, CHUNK), :] for st in starts], axis=0)
            v1 = jnp.concatenate(
                [jnp.concatenate([v_ref[pl.ds(st, CHUNK), :], ones], axis=1) for st in starts], axis=0)
            ss = [lax.dot_general(qm_ref[c], k, NT, preferred_element_type=F32) for c in range(2)]
            if first_sbs is not None:
                bias = jnp.concatenate(
                    [tiles_ref[_near_bias_kind(sb + i, r)] for sb in first_sbs for i in range(SUBS)], axis=1)
                ss = [s + bias for s in ss]
            _flash_steps(ss, [v1, v1], [(m_ref, acc_ref, c) for c in range(2)])

        _far_chunk_loop(jnp.maximum(jd - 1, 0), run_chunks)

        @pl.when(jd >= 1)
        def _():
            run_chunks([jd - 1, jd], [0, SUBS])

        @pl.when(jd == 0)
        def _():
            run_chunks([jd], [SUBS])

        lv = lamv_ref[...]
        lam = (jnp.exp(jnp.sum(lv[0:1] * lv[1:2], axis=1, keepdims=True))
               - jnp.exp(jnp.sum(lv[2:3] * lv[3:4], axis=1, keepdims=True)) + lam_init)
        a0 = acc_ref[0]
        a1 = acc_ref[1]
        o = a0[:, :A_DV] / a0[:, A_DV:] - lam * (a1[:, :A_DV] / a1[:, A_DV:])
        y = o * lax.rsqrt(jnp.mean(o * o, axis=-1, keepdims=True) + EPS)
        y = (y * subln_ref[...]) * (1.0 - lam_init)
        for g in range(A_GROUP):
            gate = gate_ref[:, g * A_DV:(g + 1) * A_DV]
            out_ref[:, g * A_DV:(g + 1) * A_DV] = (y[g * BLK:(g + 1) * BLK] * _silu(gate)).astype(BF16)


def _diff_attn_prompt(zf, zb, zb_col0, near_tiles, lamv, subln, nb, lp, nq_real, lam_init):
    nq = lp // BLK
    qw = A_GROUP * 2 * A_DH
    k_blk0 = (EVEN_OFF[1] - zb_col0) // (2 * A_DH)
    v_blk0 = (EVEN_OFF[2] - zb_col0) // A_DV
    g_blk0 = EVEN_OFF[3] // (A_GROUP * A_DV)
    rows = A_GROUP * BLK
    return pl.pallas_call(
        functools.partial(_diff_attn_kernel, lam_init=lam_init, nq_real=nq_real),
        grid=(nb, A_KV, nq),
        in_specs=[pl.BlockSpec((BLK, qw), lambda b, h, i: (b * nq + i, h)),
                  pl.BlockSpec((lp, 2 * A_DH), lambda b, h, i: (b, k_blk0 + h)),
                  pl.BlockSpec((lp, A_DV), lambda b, h, i: (b, v_blk0 + h)),
                  pl.BlockSpec((BLK, A_GROUP * A_DV), lambda b, h, i: (b * nq + i, g_blk0 + h)),
                  pl.BlockSpec((None, 4, rows, BLK), lambda b, h, i: (h, 0, 0, 0)),
                  pl.BlockSpec((4, A_DH), lambda b, h, i: (0, 0)),
                  pl.BlockSpec((1, A_DV), lambda b, h, i: (0, 0))],
        out_specs=pl.BlockSpec((BLK, A_GROUP * A_DV), lambda b, h, i: (b * nq + i, h)),
        out_shape=jax.ShapeDtypeStruct((nb * lp, A_HEADS * A_DV), BF16),
        scratch_shapes=[pltpu.VMEM((2, rows, 2 * A_DH), BF16),
                        pltpu.VMEM((2, rows, LANES), F32),
                        pltpu.VMEM((2, rows, 2 * A_DV), F32)],
        compiler_params=_cparams(("parallel", "parallel", "arbitrary")),
        name="diff_attn_prompt",
    )(zf, zb, zb, zf, near_tiles, lamv, subln.reshape(1, A_DV))


def _softplus(x):
    return jnp.maximum(x, 0.0) + jnp.log1p(jnp.exp(-jnp.abs(x)))


def _split_bf16(a):
    hi = a.astype(BF16)
    return hi, (a - hi.astype(F32)).astype(BF16)


def _dot3(a, b):
    (ah, al), (bh, bl) = a, b
    return (jnp.dot(ah, bh, preferred_element_type=F32)
            + (jnp.dot(ah, bl, preferred_element_type=F32) + jnp.dot(al, bh, preferred_element_type=F32)))


def _gdn_chunk_math(s0, q, k, v, gb, betab, row_ge, row_gt):
    n = len(q)
    c = q[0].shape[0]
    each = range(n)
    dot = functools.partial(jnp.dot, preferred_element_type=F32)
    dot_nt = functools.partial(lax.dot_general, dimension_numbers=NT, preferred_element_type=F32)
    ltri = jnp.where(row_ge, 1.0, 0.0).astype(F32)
    gcum = [jnp.dot(ltri, gb[i], precision=HI, preferred_element_type=F32) for i in each]
    decay = [jnp.where(row_ge, jnp.exp(jnp.where(row_ge, gcum[i] - gcum[i].T, 0.0)), 0.0) for i in each]
    kb = [k[i].astype(BF16) for i in each]
    qb = [q[i].astype(BF16) for i in each]
    s0b = [s0[i].astype(BF16) for i in each]
    kk = [dot_nt(kb[i], kb[i]) for i in each]
    x = [-jnp.where(row_gt, betab[i] * decay[i] * kk[i], 0.0) for i in each]
    tm = list(x)
    ps = [_split_bf16(x[i]) for i in each]
    for _ in range(int(math.log2(c)) - 1):
        p = [_dot3(ps[i], ps[i]) for i in each]
        ps = [_split_bf16(p[i]) for i in each]
        tms = [_split_bf16(tm[i]) for i in each]
        tm = [tm[i] + p[i] + _dot3(tms[i], ps[i]) for i in each]
    eg = [jnp.exp(gcum[i]) for i in each]
    ks = [dot(kb[i], s0b[i]) for i in each]
    rhs = [betab[i] * (v[i] - eg[i] * ks[i]) for i in each]
    u = [rhs[i] + dot(tm[i].astype(BF16), rhs[i].astype(BF16)) for i in each]
    ub = [u[i].astype(BF16) for i in each]
    qk = [dot_nt(qb[i], kb[i]) * decay[i] for i in each]
    o = [eg[i] * dot(qb[i], s0b[i]) + dot(qk[i].astype(BF16), ub[i]) for i in each]
    glast = [gcum[i][c - 1:c, :] for i in each]
    kd = [k[i] * jnp.exp(glast[i] - gcum[i]) for i in each]
    s_new = [jnp.exp(glast[i]) * s0[i] + dot(kd[i].T.astype(BF16), ub[i]) for i in each]
    return s_new, o


def _gdn_prompt_kernel(xq_ref, xk_ref, xv_ref, gate_ref, ab_ref, convw_ref, gp_ref, normg_ref,
                       y_ref, sfin_ref, xbuf_ref, s_ref, *, seq_len):
    ci = pl.program_id(0)
    hw = B_HEADS * B_DK
    nb = xq_ref.shape[0]

    @pl.when(ci == 0)
    def _():
        xbuf_ref[:, 0:SUBLANES, :] = jnp.zeros((nb, SUBLANES, B_QKV), F32)
        s_ref[...] = jnp.zeros(s_ref.shape, F32)

    row = lax.broadcasted_iota(I32, (BLK, BLK), 0)
    col = lax.broadcasted_iota(I32, (BLK, BLK), 1)
    row_ge = row >= col
    row_gt = row > col
    valid = (ci * BLK + row) < seq_len
    gp = gp_ref[...]
    chains = [(b, h) for b in range(nb) for h in range(B_HEADS)]
    qs, ks, vs, gbs, betabs = [], [], [], [], []
    for b in range(nb):
        xbuf_ref[b, SUBLANES:SUBLANES + BLK, 0:hw] = xq_ref[b]
        xbuf_ref[b, SUBLANES:SUBLANES + BLK, hw:2 * hw] = xk_ref[b]
        xbuf_ref[b, SUBLANES:SUBLANES + BLK, 2 * hw:3 * hw] = xv_ref[b]
        conv = jnp.zeros((BLK, B_QKV), F32)
        for i in range(CONV_W):
            conv = conv + convw_ref[i:i + 1, :] * xbuf_ref[b, pl.ds(SUBLANES - (CONV_W - 1) + i, BLK), :]
        tail = xbuf_ref[b, BLK:BLK + SUBLANES, :]
        xbuf_ref[b, 0:SUBLANES, :] = tail
        act = _silu(conv)
        ab = ab_ref[b]
        for h in range(B_HEADS):
            q = act[:, h * B_DK:(h + 1) * B_DK]
            k = act[:, hw + h * B_DK:hw + (h + 1) * B_DK]
            v = act[:, 2 * hw + h * B_DV:2 * hw + (h + 1) * B_DV]
            q = q * lax.rsqrt(jnp.sum(q * q, axis=-1, keepdims=True) + EPS) * (B_DK ** -0.5)
            k = k * lax.rsqrt(jnp.sum(k * k, axis=-1, keepdims=True) + EPS)
            a_raw = jnp.broadcast_to(ab[:, h:h + 1], (BLK, BLK))
            b_raw = jnp.broadcast_to(ab[:, B_HEADS + h:B_HEADS + h + 1], (BLK, BLK))
            a_log = gp[0:1, h:h + 1]
            dt_b = gp[1:2, h:h + 1]
            qs.append(q)
            ks.append(k)
            vs.append(v)
            gbs.append(jnp.where(valid, -jnp.exp(a_log) * _softplus(a_raw + dt_b), 0.0))
            betabs.append(jnp.where(valid, jax.nn.sigmoid(b_raw), 0.0))
    s_new, outs = _gdn_chunk_math([s_ref[b, h] for b, h in chains], qs, ks, vs, gbs, betabs, row_ge, row_gt)
    for (b, h), s_bh, o in zip(chains, s_new, outs):
        s_ref[b, h] = s_bh
        y = o * lax.rsqrt(jnp.mean(o * o, axis=-1, keepdims=True) + EPS) * normg_ref[...]
        gate = gate_ref[b, :, h * B_DV:(h + 1) * B_DV]
        y_ref[b, :, h * B_DV:(h + 1) * B_DV] = (y * _silu(gate)).astype(BF16)

    @pl.when(ci == pl.num_programs(0) - 1)
    def _():
        sfin_ref[...] = s_ref[...]


def _gdn_prompt(zf, conv_w, a_log, dt_bias, norm_g, nb, lp, seq_len):
    nc = lp // BLK
    hw = B_HEADS * B_DK
    c0 = EVEN_OFF[4] // hw
    g0 = EVEN_OFF[5] // hw
    ab0 = EVEN_OFF[6] // LANES
    gp = jnp.zeros((SUBLANES, LANES), F32).at[0, :B_HEADS].set(a_log).at[1, :B_HEADS].set(dt_bias)
    z3 = zf.reshape(nb, lp, zf.shape[1])
    y, s_fin = pl.pallas_call(
        functools.partial(_gdn_prompt_kernel, seq_len=seq_len),
        grid=(nc,),
        in_specs=[pl.BlockSpec((nb, BLK, hw), lambda i: (0, i, c0)),
                  pl.BlockSpec((nb, BLK, hw), lambda i: (0, i, c0 + 1)),
                  pl.BlockSpec((nb, BLK, hw), lambda i: (0, i, c0 + 2)),
                  pl.BlockSpec((nb, BLK, hw), lambda i: (0, i, g0)),
                  pl.BlockSpec((nb, BLK, LANES), lambda i: (0, i, ab0)),
                  pl.BlockSpec((CONV_W, B_QKV), lambda i: (0, 0)),
                  pl.BlockSpec((SUBLANES, LANES), lambda i: (0, 0)),
                  pl.BlockSpec((1, B_DV), lambda i: (0, 0))],
        out_specs=[pl.BlockSpec((nb, BLK, hw), lambda i: (0, i, 0)),
                   pl.BlockSpec((nb, B_HEADS, B_DK, B_DV), lambda i: (0, 0, 0, 0))],
        out_shape=[jax.ShapeDtypeStruct((nb, lp, hw), BF16),
                   jax.ShapeDtypeStruct((nb, B_HEADS, B_DK, B_DV), F32)],
        scratch_shapes=[pltpu.VMEM((nb, BLK + SUBLANES, B_QKV), F32),
                        pltpu.VMEM((nb, B_HEADS, B_DK, B_DV), F32)],
        compiler_params=_cparams(("arbitrary",)),
        name="gdn_prompt",
    )(z3, z3, z3, z3, z3, conv_w, gp, norm_g.reshape(1, B_DV))
    return y.reshape(nb * lp, hw), s_fin


def _kth_largest(count_fn, k_sel, shape):
    def step(i, tau):
        cand = tau + lax.shift_left(jnp.int32(1), 31 - i)
        cnt = count_fn(lambda key, idx: key >= cand)
        return jnp.where(cnt >= k_sel, cand, tau)

    return lax.fori_loop(0, 32, step, jnp.full(shape, INT_MIN, I32))


BITS_PER_CHECK = 4


def _kth_separator(count_fn, k_sel, n_keys):
    assert 32 % BITS_PER_CHECK == 0

    def pending(cnt_tau):
        return jnp.max(jnp.where(cnt_tau > k_sel, 1.0, 0.0)) > 0.5

    def group(carry):
        g, tau, cnt_tau, _ = carry
        for b in range(BITS_PER_CHECK):
            cand = tau + lax.shift_left(jnp.int32(1), 31 - (g * BITS_PER_CHECK + b))
            cnt = count_fn(lambda key, idx: key >= cand)
            take = cnt >= k_sel
            tau = jnp.where(take, cand, tau)
            cnt_tau = jnp.where(take, cnt, cnt_tau)
        return g + 1, tau, cnt_tau, pending(cnt_tau)

    init = (jnp.int32(0), jnp.full(n_keys.shape, INT_MIN, I32), n_keys, pending(n_keys))
    out = lax.while_loop(lambda c: (c[0] < 32 // BITS_PER_CHECK) & c[3], group, init)
    return out[1], out[2]


def _tie_cutoff(count_fn, tau, need, n_index_bits, shape):
    def step(i, jm):
        cand = jm | lax.shift_left(jnp.int32(1), n_index_bits - 1 - i)
        cnt = count_fn(lambda key, idx: (key == tau) & (idx < cand))
        return jnp.where(cnt < need, cand, jm)

    return lax.fori_loop(0, n_index_bits, step, jnp.zeros(shape, I32))


def _sparse_attn_kernel(qi_ref, kw_ref, kidx_ref, qc_ref, kc_ref, vc_ref, g0_ref, g1_ref, tiles_ref, out_ref,
                        keys_ref, qm_ref, qh_ref, tau_ref, jmax_ref, m_ref, acc_ref,
                        *, k_sel, n_index_bits, nq_real):
    qb = pl.program_id(1)

    @pl.when(qb >= nq_real)
    def _():
        out_ref[...] = jnp.zeros(out_ref.shape, out_ref.dtype)

    @pl.when(qb < nq_real)
    def _():
        lane = lax.broadcasted_iota(I32, (BLK, LANES), 1)
        key_pos = lax.broadcasted_iota(I32, (CHUNK, BLK), 0)
        qry_pos = lax.broadcasted_iota(I32, (CHUNK, BLK), 1)
        r = jnp.bitwise_and(qb, SUBS - 1)
        jd = lax.shift_right_logical(qb, int(math.log2(SUBS)))

        for p in range(IDX_HEADS // 2):
            pair = qi_ref[:, p * LANES:(p + 1) * LANES] * (IDX_DIM ** -0.5)
            swapped = pltpu.roll(pair, IDX_DIM, 1)
            for e, src in enumerate((pair, swapped)):
                h = 2 * p + e
                qm_ref[h * BLK:(h + 1) * BLK, :] = jnp.where(lane < IDX_DIM, src, 0.0).astype(BF16)
        w_rows = kw_ref[...].T * (IDX_HEADS ** -0.5)

        def scores_t(j):
            start = pl.multiple_of(j * CHUNK, CHUNK)
            d = lax.dot_general(kidx_ref[pl.ds(start, CHUNK), :], qm_ref[...], NT, preferred_element_type=F32)
            sc = jnp.zeros((CHUNK, BLK), F32)
            for h in range(IDX_HEADS):
                sc = sc + w_rows[IDX_DIM + h:IDX_DIM + h + 1, :] * jnp.maximum(d[:, h * BLK:(h + 1) * BLK], 0.0)
            return sc

        def score_quad(i, carry):
            for j in (4 * i, 4 * i + 1, 4 * i + 2, 4 * i + 3):
                keys_ref[j] = _sortable_key(scores_t(j))
            return carry

        n_quad = lax.shift_right_logical(jd, 2)
        lax.fori_loop(0, n_quad, score_quad, 0)

        @pl.when(jnp.bitwise_and(jd, 2) == 2)
        def _():
            for j in (4 * n_quad, 4 * n_quad + 1):
                keys_ref[j] = _sortable_key(scores_t(j))

        @pl.when(jnp.bitwise_and(jd, 1) == 1)
        def _():
            keys_ref[jd - 1] = _sortable_key(scores_t(jd - 1))

        admissible = (jd * CHUNK + key_pos) <= (qb * BLK + qry_pos)
        keys_ref[jd] = _sortable_key(jnp.where(admissible, scores_t(jd), -jnp.inf))

        def count_fn(pred):
            part = SUBLANES * SUBLANES

            def body(j, acc):
                hit = jnp.where(pred(keys_ref[j], j * CHUNK + key_pos), 1.0, 0.0)
                return acc + jnp.sum(hit.reshape(CHUNK // part, part, BLK), axis=0)
            acc = lax.fori_loop(0, jd + 1, body, jnp.zeros((part, BLK), F32))
            return jnp.sum(acc, axis=0, keepdims=True).astype(I32)

        n_keys = qb * BLK + lax.broadcasted_iota(I32, (1, BLK), 1) + 1
        tau, n_ge = _kth_separator(count_fn, k_sel, n_keys)
        tau_ref[...] = tau
        jmax_ref[...] = jnp.full((1, BLK), INT_MAX, I32)
        need = k_sel - count_fn(lambda key, idx: key > tau)
        excess = jnp.max(jnp.where(n_ge > k_sel, 1.0, 0.0)) > 0.5

        @pl.when(excess)
        def _():
            jmax_ref[...] = _tie_cutoff(count_fn, tau, need, n_index_bits, (1, BLK))

        jmax = jmax_ref[...]

        for h in range(C_KV):
            for g in range(C_GROUP):
                col = (h * C_GROUP + g) * C_DH
                qh_ref[h, g * BLK:(g + 1) * BLK, :] = (
                    qc_ref[:, col:col + C_DH] * (C_DH ** -0.5 * LOG2E)).astype(BF16)
        m_ref[...] = jnp.full(m_ref.shape, NEG, F32)
        acc_ref[...] = jnp.zeros(acc_ref.shape, F32)
        ones = jnp.ones((CHUNK, C_DH), BF16)

        def run_chunks(js, first_sbs=None):
            starts = [pl.multiple_of(j * CHUNK, CHUNK) for j in js]
            masks = []
            for j in js:
                key = keys_ref[j]
                tie_ok = jnp.where((j * CHUNK + key_pos) <= jmax, 0.0, NEG)
                sel_t = jnp.where(key > tau, 0.0, jnp.where(key == tau, tie_ok, NEG))
                masks += [sel_t[i * BLK:(i + 1) * BLK, :].T for i in range(SUBS)]
            selneg = jnp.concatenate(masks, axis=1)
            selneg = jnp.concatenate([selneg] * C_GROUP, axis=0)
            ss, v1s = [], []
            for h in range(C_KV):
                cols = slice(h * C_DH, (h + 1) * C_DH)
                k = jnp.concatenate([kc_ref[pl.ds(st, CHUNK), cols] for st in starts], axis=0)
                v1s.append(jnp.concatenate(
                    [jnp.concatenate([vc_ref[pl.ds(st, CHUNK), cols], ones], axis=1) for st in starts], axis=0))
                s = lax.dot_general(qh_ref[h], k, NT, preferred_element_type=F32) + selneg
                if first_sbs is not None:
                    s = s + jnp.concatenate(
                        [tiles_ref[h, _near_bias_kind(sb + i, r)] for sb in first_sbs for i in range(SUBS)], axis=1)
                ss.append(s)
            _flash_steps(ss, v1s, [(m_ref, acc_ref, h) for h in range(C_KV)])

        _far_chunk_loop(jnp.maximum(jd - 1, 0), run_chunks)

        @pl.when(jd >= 1)
        def _():
            run_chunks([jd - 1, jd], [0, SUBS])

        @pl.when(jd == 0)
        def _():
            run_chunks([jd], [SUBS])

        half = (C_HEADS // 2) * C_DH
        for h in range(C_KV):
            a = acc_ref[h]
            o = a[:, :C_DH] / a[:, C_DH:]
            gref = g0_ref if h == 0 else g1_ref
            for g in range(C_GROUP):
                gate = gref[:, g * C_DH:(g + 1) * C_DH]
                out_ref[:, h * half + g * C_DH:h * half + (g + 1) * C_DH] = (
                    o[g * BLK:(g + 1) * BLK] * _silu(gate)).astype(BF16)


def _sparse_attn_prompt(zf, zb, near_tiles, nb, lp, nq_real, k_sel):
    nq = lp // BLK
    half = (C_HEADS // 2) * C_DH
    qi0 = ODD_OFF[4] // (IDX_HEADS * IDX_DIM)
    ki0 = ODD_OFF[5] // LANES
    kc0 = ODD_OFF[1] // (C_KV * C_DH)
    vc0 = ODD_OFF[2] // (C_KV * C_DH)
    g0 = ODD_OFF[3] // half
    rows = C_GROUP * BLK
    n_index_bits = max(1, int(math.ceil(math.log2(lp))))
    return pl.pallas_call(
        functools.partial(_sparse_attn_kernel, k_sel=k_sel, n_index_bits=n_index_bits, nq_real=nq_real),
        grid=(nb, nq),
        in_specs=[pl.BlockSpec((BLK, IDX_HEADS * IDX_DIM), lambda b, i: (b * nq + i, qi0)),
                  pl.BlockSpec((BLK, LANES), lambda b, i: (b * nq + i, ki0)),
                  pl.BlockSpec((lp, LANES), lambda b, i: (b, ki0)),
                  pl.BlockSpec((BLK, C_HEADS * C_DH), lambda b, i: (b * nq + i, 0)),
                  pl.BlockSpec((lp, C_KV * C_DH), lambda b, i: (b, kc0)),
                  pl.BlockSpec((lp, C_KV * C_DH), lambda b, i: (b, vc0)),
                  pl.BlockSpec((BLK, half), lambda b, i: (b * nq + i, g0)),
                  pl.BlockSpec((BLK, half), lambda b, i: (b * nq + i, g0 + 1)),
                  pl.BlockSpec((C_KV, 4, rows, BLK), lambda b, i: (0, 0, 0, 0))],
        out_specs=pl.BlockSpec((BLK, C_HEADS * C_DH), lambda b, i: (b * nq + i, 0)),
        out_shape=jax.ShapeDtypeStruct((nb * lp, C_HEADS * C_DH), BF16),
        scratch_shapes=[pltpu.VMEM((lp // CHUNK, CHUNK, BLK), I32),
                        pltpu.VMEM((IDX_HEADS * BLK, LANES), BF16),
                        pltpu.VMEM((C_KV, rows, C_DH), BF16),
                        pltpu.VMEM((1, BLK), I32),
                        pltpu.VMEM((1, BLK), I32),
                        pltpu.VMEM((C_KV, rows, LANES), F32),
                        pltpu.VMEM((C_KV, rows, 2 * C_DH), F32)],
        compiler_params=_cparams(("parallel", "arbitrary")),
        name="sparse_attn_prompt",
    )(zf, zf, zb, zf, zb, zb, zf, zf, near_tiles)


PAGES_PER_STEP = 32
INDEX_PAGES_PER_STEP = 64
DEC_ROWS = 16


def _interleaved_pages(refs, page):
    halves = [jnp.concatenate([r[pl.ds(h, page, stride=2), :] for r in refs], axis=0) for h in range(2)]
    return jnp.concatenate(halves, axis=1).astype(BF16)


def _paged_attn_kernel(*refs, n_pages_step, masked, page, k_feature_major):
    if masked:
        pt_ref, tau_ref, jmax_ref, selnew_ref = refs[:4]
        refs = refs[4:]
    else:
        pt_ref = refs[0]
        refs = refs[1:]
    q_ref, knew_ref, vnew_ref, bfar_ref, blast_ref, b0_ref = refs[:6]
    refs = refs[6:]
    if masked:
        keys_ref = refs[0]
        refs = refs[1:]
    k_refs = refs[:n_pages_step]
    v_refs = refs[n_pages_step:2 * n_pages_step]
    out_ref, m_ref, l_ref, acc_ref = refs[2 * n_pages_step:]
    del pt_ref
    b = pl.program_id(0)
    j = pl.program_id(1)
    width = n_pages_step * page
    q = q_ref[...]

    @pl.when(j == 0)
    def _():
        s_new = jnp.sum(q.astype(F32) * knew_ref[...].astype(BF16).astype(F32), axis=1, keepdims=True) + b0_ref[...]
        v_new = jnp.broadcast_to(vnew_ref[...].astype(BF16).astype(F32), acc_ref.shape)
        if masked:
            take = selnew_ref[b] > 0
            m_ref[...] = jnp.where(take, s_new, NEG)
            l_ref[...] = jnp.where(take, 1.0, 0.0) * jnp.ones(l_ref.shape, F32)
            acc_ref[...] = jnp.where(take, v_new, 0.0)
        else:
            m_ref[...] = s_new
            l_ref[...] = jnp.ones(l_ref.shape, F32)
            acc_ref[...] = v_new

    if k_feature_major:
        kcat = jnp.concatenate([r[...] for r in k_refs], axis=1).astype(BF16)
        s = jnp.dot(q, kcat, preferred_element_type=F32)
    else:
        s = lax.dot_general(q, _interleaved_pages(k_refs, page), NT, preferred_element_type=F32)
    vcat = _interleaved_pages(v_refs, page)
    s = s + jnp.where(j == pl.num_programs(1) - 1, blast_ref[...], bfar_ref[...])
    if masked:
        key = keys_ref[...]
        idx = j * width + lax.broadcasted_iota(I32, (1, width), 1)
        tau = tau_ref[b]
        sel = (key > tau) | ((key == tau) & (idx <= jmax_ref[b]))
        s = jnp.where(sel, s, NEG)
    m_old = m_ref[...]
    m_new = jnp.maximum(m_old, jnp.max(s, axis=1, keepdims=True))
    alpha = jnp.exp(m_old - m_new)
    p = jnp.exp(s - m_new[:, 0:1])
    l_ref[...] = alpha * l_ref[...] + jnp.sum(p, axis=1, keepdims=True)
    acc_ref[...] = _lane_tile(alpha, 2) * acc_ref[...] + jnp.dot(p.astype(BF16), vcat, preferred_element_type=F32)
    m_ref[...] = m_new

    @pl.when(j == pl.num_programs(1) - 1)
    def _():
        o = acc_ref[...]
        rowi = lax.broadcasted_iota(I32, (DEC_ROWS, LANES), 0)
        upper = (rowi >= DEC_ROWS // 2) if not masked else ((rowi >= DEC_ROWS // 4) & (rowi < DEC_ROWS // 2))
        out_ref[...] = jnp.where(upper, o[:, LANES:], o[:, :LANES]) / l_ref[...]


def _pages_per_step(n_pages, g=PAGES_PER_STEP):
    while n_pages % g:
        g //= 2
    return g


def _paged_attention(qprime, k_new, v_new, bias_far, bias_last, bias0, k_cache, v_cache, page_table, page,
                     k_feature_major, mask_args=None):
    db = qprime.shape[0]
    n_pages = page_table.shape[1]
    width = qprime.shape[2]
    g = _pages_per_step(n_pages)
    n_steps = n_pages // g
    masked = mask_args is not None
    n_pref = 4 if masked else 1

    def page_map(gi):
        return lambda b, j, pt, *_: (pt[b * n_pages + j * g + gi], 0, 0)

    in_specs = [pl.BlockSpec((None, DEC_ROWS, width), lambda b, j, *_: (b, 0, 0)),
                pl.BlockSpec((None, 1, width), lambda b, j, *_: (b, 0, 0)),
                pl.BlockSpec((None, 1, width), lambda b, j, *_: (b, 0, 0)),
                pl.BlockSpec((DEC_ROWS, g * page), lambda b, j, *_: (0, 0)),
                pl.BlockSpec((DEC_ROWS, g * page), lambda b, j, *_: (0, 0)),
                pl.BlockSpec((DEC_ROWS, LANES), lambda b, j, *_: (0, 0))]
    args = [qprime, k_new, v_new, bias_far, bias_last, bias0]
    prefetch = [page_table.reshape(-1)]
    if masked:
        keys, tau, jmax, selnew = mask_args
        prefetch += [tau, jmax, selnew]
        in_specs += [pl.BlockSpec((None, 1, g * page), lambda b, j, *_: (b, 0, j))]
        args += [keys]
    in_specs += [pl.BlockSpec((None,) + k_cache.shape[1:], page_map(gi)) for gi in range(g)]
    in_specs += [pl.BlockSpec((None,) + v_cache.shape[1:], page_map(gi)) for gi in range(g)]
    args += [k_cache] * g + [v_cache] * g
    grid_spec = pltpu.PrefetchScalarGridSpec(
        num_scalar_prefetch=n_pref,
        grid=(db, n_steps),
        in_specs=in_specs,
        out_specs=pl.BlockSpec((None, DEC_ROWS, LANES), lambda b, j, *_: (b, 0, 0)),
        scratch_shapes=[pltpu.VMEM((DEC_ROWS, LANES), F32),
                        pltpu.VMEM((DEC_ROWS, LANES), F32),
                        pltpu.VMEM((DEC_ROWS, width), F32)])
    return pl.pallas_call(
        functools.partial(_paged_attn_kernel, n_pages_step=g, masked=masked, page=page,
                          k_feature_major=k_feature_major),
        grid_spec=grid_spec,
        out_shape=jax.ShapeDtypeStruct((db, DEC_ROWS, LANES), F32),
        compiler_params=_cparams(("parallel", "arbitrary")),
        name="paged_attn_masked" if masked else "paged_attn",
    )(*prefetch, *args)


def _paged_index_kernel(pt_ref, q_ref, w_ref, knew_ref, *refs, n_pages_step, page):
    k_refs = refs[:n_pages_step]
    out_ref = refs[n_pages_step]
    del pt_ref
    j = pl.program_id(1)
    last = pl.num_programs(1) - 1
    q = q_ref[...]
    w = w_ref[...]

    def score(kmat_t):
        d = jnp.dot(q, kmat_t.astype(BF16), preferred_element_type=F32)
        return jnp.sum(jnp.maximum(d, 0.0) * w[:, 0:1], axis=0, keepdims=True)

    @pl.when(j < last)
    def _():
        kcat = jnp.concatenate([r[...] for r in k_refs], axis=1)
        out_ref[...] = _sortable_key(score(kcat))

    @pl.when(j == last)
    def _():
        sc = score(knew_ref[...])
        lane = lax.broadcasted_iota(I32, (1, page), 1)
        sc = jnp.where(lane == 0, sc, -jnp.inf)
        pad = jnp.full((1, (n_pages_step - 1) * page), -jnp.inf, F32)
        full = jnp.concatenate([sc, pad], axis=1) if n_pages_step > 1 else sc
        out_ref[...] = _sortable_key(full)


def _paged_index_scores(qidx, wcol, k_new_tile, idx_cache_t, page_table):
    db = qidx.shape[0]
    n_pages = page_table.shape[1]
    page = idx_cache_t.shape[2]
    g = _pages_per_step(n_pages, INDEX_PAGES_PER_STEP)
    n_steps = n_pages // g

    def page_map(gi):
        return lambda b, j, pt: (pt[b * n_pages + jnp.minimum(j, n_steps - 1) * g + gi], 0, 0)

    grid_spec = pltpu.PrefetchScalarGridSpec(
        num_scalar_prefetch=1,
        grid=(db, n_steps + 1),
        in_specs=[pl.BlockSpec((None, IDX_HEADS, IDX_DIM), lambda b, j, pt: (b, 0, 0)),
                  pl.BlockSpec((None, IDX_HEADS, LANES), lambda b, j, pt: (b, 0, 0)),
                  pl.BlockSpec((None, IDX_DIM, page), lambda b, j, pt: (b, 0, 0))]
        + [pl.BlockSpec((None, IDX_DIM, page), page_map(gi)) for gi in range(g)],
        out_specs=pl.BlockSpec((None, 1, g * page), lambda b, j, pt: (b, 0, j)))
    return pl.pallas_call(
        functools.partial(_paged_index_kernel, n_pages_step=g, page=page),
        grid_spec=grid_spec,
        out_shape=jax.ShapeDtypeStruct((db, 1, (n_steps + 1) * g * page), I32),
        compiler_params=_cparams(("parallel", "arbitrary")),
        name="paged_index_scores",
    )(page_table.reshape(-1), qidx, wcol, k_new_tile, *([idx_cache_t] * g))


def _select_kernel(keys_ref, tau_ref, jmax_ref, *, k_sel, n_index_bits):
    keys = keys_ref[...]
    idx = lax.broadcasted_iota(I32, keys.shape, 1)
    shape = (keys.shape[0], 1)

    def count_fn(pred):
        return jnp.sum(jnp.where(pred(keys, idx), 1, 0), axis=1, keepdims=True)

    tau = _kth_largest(count_fn, k_sel, shape)
    need = k_sel - count_fn(lambda key, i: key > tau)
    jmax = _tie_cutoff(count_fn, tau, need, n_index_bits, shape)
    tau_ref[...] = jnp.broadcast_to(tau, tau_ref.shape)
    jmax_ref[...] = jnp.broadcast_to(jmax, jmax_ref.shape)


def _select_rows(keys2d, k_sel):
    rows, width = keys2d.shape
    n_index_bits = max(1, int(math.ceil(math.log2(width))))
    return pl.pallas_call(
        functools.partial(_select_kernel, k_sel=k_sel, n_index_bits=n_index_bits),
        out_shape=[jax.ShapeDtypeStruct((rows, LANES), I32), jax.ShapeDtypeStruct((rows, LANES), I32)],
        compiler_params=pltpu.CompilerParams(vmem_limit_bytes=VMEM_LIMIT),
        name="select_rows",
    )(keys2d)


def _even_tail_kernel(z_ref, conv_ref, s_ref, oa_ref, convw_ref, gp_ref, normg_ref, lamv_ref, subln_ref,
                      ya_ref, yb_ref, snew_ref, *, lam_init):
    z = z_ref[...]
    hw = B_HEADS * B_DK
    lv = lamv_ref[...]
    lam = (jnp.exp(jnp.sum(lv[0:1] * lv[1:2], axis=1, keepdims=True))
           - jnp.exp(jnp.sum(lv[2:3] * lv[3:4], axis=1, keepdims=True)) + lam_init)
    oa = oa_ref[...]
    for hg in range(A_HEADS):
        o = oa[2 * hg:2 * hg + 1] - lam * oa[2 * hg + 1:2 * hg + 2]
        y = o * lax.rsqrt(jnp.mean(o * o, axis=-1, keepdims=True) + EPS)
        y = (y * subln_ref[...]) * (1.0 - lam_init)
        gate = z[:, EVEN_OFF[3] + hg * A_DV:EVEN_OFF[3] + (hg + 1) * A_DV]
        ya_ref[:, hg * A_DV:(hg + 1) * A_DV] = (y * _silu(gate)).astype(BF16)
    x_new = z[:, EVEN_OFF[4]:EVEN_OFF[4] + B_QKV]
    conv = convw_ref[CONV_W - 1:CONV_W, :] * x_new
    cp = conv_ref[...]
    for i in range(CONV_W - 1):
        conv = conv + convw_ref[i:i + 1, :] * cp[i:i + 1, :]
    act = _silu(conv)
    ab = z[:, EVEN_OFF[6]:EVEN_OFF[6] + LANES]
    gp = gp_ref[...]
    row = lax.broadcasted_iota(I32, (B_DK, B_DV), 0)
    col = lax.broadcasted_iota(I32, (B_DK, B_DV), 1)
    eye = row == col
    for h in range(B_HEADS):
        q = act[:, h * B_DK:(h + 1) * B_DK]
        k = act[:, hw + h * B_DK:hw + (h + 1) * B_DK]
        v = act[:, 2 * hw + h * B_DV:2 * hw + (h + 1) * B_DV]
        q = q * lax.rsqrt(jnp.sum(q * q, axis=-1, keepdims=True) + EPS) * (B_DK ** -0.5)
        k = k * lax.rsqrt(jnp.sum(k * k, axis=-1, keepdims=True) + EPS)
        g = -jnp.exp(gp[0:1, h:h + 1]) * _softplus(ab[:, h:h + 1] + gp[1:2, h:h + 1])
        beta = jax.nn.sigmoid(ab[:, B_HEADS + h:B_HEADS + h + 1])
        eg = jnp.exp(g)
        s0 = s_ref[h]
        kcol = jnp.sum(jnp.where(eye, jnp.broadcast_to(k, (B_DK, B_DK)), 0.0), axis=1, keepdims=True)
        qcol = jnp.sum(jnp.where(eye, jnp.broadcast_to(q, (B_DK, B_DK)), 0.0), axis=1, keepdims=True)
        ks = jnp.sum(kcol * s0, axis=0, keepdims=True)
        qs = jnp.sum(qcol * s0, axis=0, keepdims=True)
        u = beta * (v - eg * ks)
        qk = jnp.sum(q * k, axis=1, keepdims=True)
        o = eg * qs + qk * u
        snew_ref[h] = eg * s0 + kcol * u
        y = o * lax.rsqrt(jnp.mean(o * o, axis=-1, keepdims=True) + EPS) * normg_ref[...]
        gate = z[:, EVEN_OFF[5] + h * B_DV:EVEN_OFF[5] + (h + 1) * B_DV]
        yb_ref[:, h * B_DV:(h + 1) * B_DV] = (y * _silu(gate)).astype(BF16)


def _even_tail(zf_s, conv_prev, s_prev, oa, conv_w, a_log, dt_bias, norm_g, lamv, subln, lam_init):
    db, npad = zf_s.shape
    gp = jnp.zeros((SUBLANES, LANES), F32).at[0, :B_HEADS].set(a_log).at[1, :B_HEADS].set(dt_bias)
    hw = B_HEADS * B_DV
    return pl.pallas_call(
        functools.partial(_even_tail_kernel, lam_init=lam_init),
        grid=(db,),
        in_specs=[pl.BlockSpec((None, 1, npad), lambda b: (b, 0, 0)),
                  pl.BlockSpec((None, CONV_W - 1, B_QKV), lambda b: (b, 0, 0)),
                  pl.BlockSpec((None, B_HEADS, B_DK, B_DV), lambda b: (b, 0, 0, 0)),
                  pl.BlockSpec((None, DEC_ROWS, LANES), lambda b: (b, 0, 0)),
                  pl.BlockSpec((CONV_W, B_QKV), lambda b: (0, 0)),
                  pl.BlockSpec((SUBLANES, LANES), lambda b: (0, 0)),
                  pl.BlockSpec((1, B_DV), lambda b: (0, 0)),
                  pl.BlockSpec((4, A_DH), lambda b: (0, 0)),
                  pl.BlockSpec((1, A_DV), lambda b: (0, 0))],
        out_specs=[pl.BlockSpec((None, 1, A_HEADS * A_DV), lambda b: (b, 0, 0)),
                   pl.BlockSpec((None, 1, hw), lambda b: (b, 0, 0)),
                   pl.BlockSpec((None, B_HEADS, B_DK, B_DV), lambda b: (b, 0, 0, 0))],
        out_shape=[jax.ShapeDtypeStruct((db, 1, A_HEADS * A_DV), BF16),
                   jax.ShapeDtypeStruct((db, 1, hw), BF16),
                   jax.ShapeDtypeStruct((db, B_HEADS, B_DK, B_DV), F32)],
        compiler_params=_cparams(("parallel",)),
        name="even_tail",
    )(zf_s.reshape(db, 1, npad), conv_prev, s_prev, oa, conv_w, gp, norm_g.reshape(1, B_DV), lamv,
      subln.reshape(1, A_DV))


def _odd_tail_kernel(z_ref, oc_ref, y_ref):
    z = z_ref[...]
    oc = oc_ref[...]
    for hg in range(C_HEADS):
        gate = z[:, ODD_OFF[3] + hg * C_DH:ODD_OFF[3] + (hg + 1) * C_DH]
        y_ref[:, hg * C_DH:(hg + 1) * C_DH] = (oc[hg:hg + 1] * _silu(gate)).astype(BF16)


def _odd_tail(zf_s, oc):
    db, npad = zf_s.shape
    return pl.pallas_call(
        _odd_tail_kernel,
        grid=(db,),
        in_specs=[pl.BlockSpec((None, 1, npad), lambda b: (b, 0, 0)),
                  pl.BlockSpec((None, DEC_ROWS, LANES), lambda b: (b, 0, 0))],
        out_specs=pl.BlockSpec((None, 1, C_HEADS * C_DH), lambda b: (b, 0, 0)),
        out_shape=jax.ShapeDtypeStruct((db, 1, C_HEADS * C_DH), BF16),
        compiler_params=_cparams(("parallel",)),
        name="odd_tail",
    )(zf_s.reshape(db, 1, npad), oc)


def _bias_by_distance(table):
    n = jnp.arange(FAR_DIST + 1)
    exact = N_BUCKETS // 2
    nf = jnp.maximum(n, 1).astype(F32)
    large = exact + (jnp.log(nf / exact) / math.log(MAX_DIST / exact) * (N_BUCKETS - exact)).astype(I32)
    bucket = jnp.where(n < exact, n, jnp.minimum(large, N_BUCKETS - 1))
    return table[bucket].astype(F32)


def _prompt_near_tiles(bd, group):
    assert FAR_DIST <= BLK
    heads = bd.shape[1]
    rel = ((bd - bd[FAR_DIST][None, :]) * LOG2E).T
    f = jnp.concatenate([jnp.full((heads, BLK - 1), NEG, F32), rel,
                         jnp.broadcast_to(rel[:, FAR_DIST:], (heads, 2 * BLK - 1 - FAR_DIST))], axis=1)
    period = 3 * BLK
    g = jnp.pad(f[:, ::-1], ((0, 0), (0, period - f.shape[1])))
    wrapped = jnp.tile(g, (1, BLK + 1))[:, :BLK * (period + 1)].reshape(heads, BLK, period + 1)
    strip = wrapped[:, ::-1, :2 * BLK]
    kinds = [None] * 4
    kinds[TILE_ZERO] = jnp.zeros((heads, BLK, BLK), F32)
    kinds[TILE_PREV] = strip[:, :, :BLK]
    kinds[TILE_DIAG] = strip[:, :, BLK:]
    kinds[TILE_MASKED] = jnp.full((heads, BLK, BLK), NEG, F32)
    tiles = jnp.stack(kinds, axis=1)
    n_kv = heads // group
    tiles = tiles.reshape(n_kv, group, 4, BLK, BLK)
    return jnp.transpose(tiles, (0, 2, 1, 3, 4)).reshape(n_kv, 4, group * BLK, BLK)


def _decode_bias(bd, row_heads, past, page, g):
    heads = jnp.asarray(row_heads, I32)
    far = jnp.broadcast_to(bd[FAR_DIST][heads][:, None], (len(row_heads), g * page))
    pos = past - g * page + jnp.arange(g * page)
    dist = jnp.minimum(past - pos, FAR_DIST)
    last = bd[dist][:, heads].T
    new = jnp.broadcast_to(bd[0][heads][:, None], (len(row_heads), LANES))
    return far.astype(F32), last.astype(F32), new.astype(F32)


def _pad_cols(w, mult):
    n = w.shape[1]
    return jnp.pad(w, ((0, 0), (0, _round_up(n, mult) - n)))


def kernel(x_prompt, x_sample, cache_a_k, cache_a_v, state_b_s, state_b_conv, cache_c_k, cache_c_v, cache_c_idx,
           page_table, meta, bias_table, final_norm, norm_e, w_in_e, w_out_e, lam_q1, lam_k1, lam_q2, lam_k2,
           subln_a, conv_b, a_log_b, dt_bias_b, norm_b, norm_o, w_in_o, w_out_o):
    nb, seq, d = x_prompt.shape
    n_meta = meta.shape[0]
    l = seq + n_meta
    lp = _round_up(l, CHUNK)
    nq_real = pl.cdiv(l, BLK)
    db = x_sample.shape[0]
    n_pages = page_table.shape[1]
    page = cache_a_k.shape[2]
    past = n_pages * page
    n_pool = cache_a_k.shape[1]
    assert x_sample.shape[1] == 1 and norm_e.shape[0] == 1 and norm_o.shape[0] == 1
    lam_init = 0.8 - 0.6 * math.exp(-0.3 * 0)

    tn_e, tn_o = 8 * LANES, 5 * LANES
    w_e = _pad_cols(w_in_e[0], tn_e).astype(BF16)
    w_o = _pad_cols(w_in_o[0], tn_o).astype(BF16)
    w_out_a = w_out_e[0][:A_HEADS * A_DV].astype(BF16)
    w_out_b = w_out_e[0][A_HEADS * A_DV:].astype(BF16)
    w_out_c = w_out_o[0].astype(BF16)
    tm = CHUNK
    tm_proj = 2 * CHUNK if (nb * lp) % (2 * CHUNK) == 0 else CHUNK
    dbp = _round_up(db, SUBLANES)

    bd = _bias_by_distance(bias_table)
    tiles_a = _prompt_near_tiles(bd, A_GROUP)
    tiles_c = tiles_a if (A_KV, A_GROUP) == (C_KV, C_GROUP) else _prompt_near_tiles(bd, C_GROUP)
    lamv = jnp.stack([lam_q1[0], lam_k1[0], lam_q2[0], lam_k2[0]]).astype(F32)

    hp = jnp.concatenate([jnp.broadcast_to(meta.astype(F32)[None], (nb, n_meta, d)), x_prompt], axis=1)
    hp = jnp.pad(hp, ((0, 0), (0, lp - l), (0, 0))).reshape(nb * lp, d)
    kv_tile = EVEN_OFF[1] // tn_e
    assert EVEN_OFF[3] <= (kv_tile + 1) * tn_e
    zf, zb = _norm_proj(hp, norm_e[0], w_e, tm_proj, tn_e, (kv_tile, kv_tile + 1))
    ya = _diff_attn_prompt(zf, zb, kv_tile * tn_e, tiles_a, lamv, subln_a[0], nb, lp, nq_real, lam_init)
    yb, pb_s = _gdn_prompt(zf, conv_b[0], a_log_b[0], dt_bias_b[0], norm_b[0], nb, lp, l)
    h1 = _out_proj([ya, yb], [w_out_a, w_out_b], hp, tm)
    z3 = zf.reshape(nb, lp, -1)
    pa_k = z3[:, :l, EVEN_OFF[1]:EVEN_OFF[2]].reshape(1, nb, l, A_KV, 2, A_DH)
    pa_v = z3[:, :l, EVEN_OFF[2]:EVEN_OFF[3]].reshape(1, nb, l, A_KV, A_DV)
    pb_conv = z3[:, l - (CONV_W - 1):l, EVEN_OFF[4]:EVEN_OFF[5]][None]

    zf1, zb1 = _norm_proj(h1, norm_o[0], w_o, tm_proj, tn_o, (0, w_o.shape[1] // tn_o))
    k_sel_p = min(TOPK_MAX, l // 4)
    yc = _sparse_attn_prompt(zf1, zb1, tiles_c, nb, lp, nq_real, k_sel_p)
    yp = _out_proj([yc], [w_out_c], h1, tm, final_gain=final_norm)
    y_prompt = yp.reshape(nb, lp, d)[:, n_meta:l]
    z13 = zf1.reshape(nb, lp, -1)
    pc_k = z13[:, :l, ODD_OFF[1]:ODD_OFF[2]].reshape(1, nb, l, C_KV, C_DH)
    pc_v = z13[:, :l, ODD_OFF[2]:ODD_OFF[3]].reshape(1, nb, l, C_KV, C_DH)
    pc_idx = z13[:, :l, ODD_OFF[5]:ODD_OFF[6]][None]

    hs = jnp.pad(x_sample.reshape(db, d), ((0, dbp - db), (0, 0)))
    zs, _ = _norm_proj(hs, norm_e[0], w_e, dbp, tn_e, (0, 1))
    zs = zs[:db]
    qa = zs[:, :EVEN_OFF[1]].reshape(db, A_KV, A_GROUP, 2, A_DH) * (A_DH ** -0.5)
    qprime = jnp.einsum('bhgcd,hi,cj->bhgcijd', qa, jnp.eye(A_KV, dtype=F32), jnp.eye(2, dtype=F32))
    qprime = qprime.reshape(db, DEC_ROWS, A_KV * 2 * A_DH).astype(BF16)
    k_new = zs[:, EVEN_OFF[1]:EVEN_OFF[2]].reshape(db, 1, -1)
    v_new = zs[:, EVEN_OFF[2]:EVEN_OFF[3]].reshape(db, 1, -1)
    g_dec = _pages_per_step(n_pages)
    rows_a = [r // 2 for r in range(DEC_ROWS)]
    bfar, blast, bnew = _decode_bias(bd, rows_a, past, page, g_dec)
    ak_t = jnp.transpose(cache_a_k[0], (0, 2, 3, 4, 1)).reshape(n_pool, A_KV * 2 * A_DH, page)
    av_r = cache_a_v[0].reshape(n_pool, page * A_KV, A_DV)
    oa = _paged_attention(qprime, k_new, v_new, bfar, blast, bnew, ak_t, av_r, page_table, page,
                          k_feature_major=True)
    ya_s, yb_s, sb_s = _even_tail(zs, state_b_conv[0], state_b_s[0], oa, conv_b[0], a_log_b[0], dt_bias_b[0],
                                  norm_b[0], lamv, subln_a[0], lam_init)
    hs_pad = lambda y: jnp.pad(y.reshape(db, -1), ((0, dbp - db), (0, 0)))
    hs1 = _out_proj([hs_pad(ya_s), hs_pad(yb_s)], [w_out_a, w_out_b], hs, dbp)
    sa_k = zs[:, EVEN_OFF[1]:EVEN_OFF[2]].reshape(1, db, 1, A_KV, 2, A_DH)
    sa_v = zs[:, EVEN_OFF[2]:EVEN_OFF[3]].reshape(1, db, 1, A_KV, A_DV)
    sb_conv = jnp.concatenate([state_b_conv[0][:, 1:], zs[:, None, EVEN_OFF[4]:EVEN_OFF[5]]], axis=1)[None]

    zs1, _ = _norm_proj(hs1, norm_o[0], w_o, dbp, tn_o, (0, 1))
    zs1 = zs1[:db]
    qidx = (zs1[:, ODD_OFF[4]:ODD_OFF[5]].reshape(db, IDX_HEADS, IDX_DIM) * (IDX_DIM ** -0.5)).astype(BF16)
    wcol = jnp.broadcast_to((zs1[:, ODD_OFF[6]:ODD_OFF[7]] * (IDX_HEADS ** -0.5))[:, :, None],
                            (db, IDX_HEADS, LANES))
    ki_new = zs1[:, ODD_OFF[5]:ODD_OFF[6]]
    ki_tile = jnp.zeros((db, IDX_DIM, page), F32).at[:, :, 0].set(ki_new)
    ci_t = jnp.transpose(cache_c_idx[0], (0, 2, 1))
    keys = _paged_index_scores(qidx, wcol, ki_tile, ci_t, page_table)
    k_sel_s = min(TOPK_MAX, (past + 1) // 4)
    tau, jmax = _select_rows(keys.reshape(db, -1), k_sel_s)
    tau, jmax = tau[:, 0], jmax[:, 0]
    key_new = keys[:, 0, past]
    selnew = ((key_new > tau) | ((key_new == tau) & (past <= jmax))).astype(I32)
    qc = zs1[:, :ODD_OFF[1]].reshape(db, C_KV, C_GROUP, C_DH) * (C_DH ** -0.5)
    qcp = jnp.einsum('bhgd,hi->bhgid', qc, jnp.eye(C_KV, dtype=F32)).reshape(db, C_HEADS, C_KV * C_DH)
    qcp = jnp.pad(qcp, ((0, 0), (0, DEC_ROWS - C_HEADS), (0, 0))).astype(BF16)
    kc_new = zs1[:, ODD_OFF[1]:ODD_OFF[2]].reshape(db, 1, -1)
    vc_new = zs1[:, ODD_OFF[2]:ODD_OFF[3]].reshape(db, 1, -1)
    rows_c = [r if r < C_HEADS else 0 for r in range(DEC_ROWS)]
    cfar, clast, cnew = _decode_bias(bd, rows_c, past, page, g_dec)
    ck_r = cache_c_k[0].reshape(n_pool, page * C_KV, C_DH)
    cv_r = cache_c_v[0].reshape(n_pool, page * C_KV, C_DH)
    oc = _paged_attention(qcp, kc_new, vc_new, cfar, clast, cnew, ck_r, cv_r, page_table, page,
                          k_feature_major=False, mask_args=(keys, tau, jmax, selnew))
    yc_s = _odd_tail(zs1, oc)
    ys = _out_proj([hs_pad(yc_s)], [w_out_c], hs1, dbp, final_gain=final_norm)
    y_sample = ys[:db].reshape(db, 1, d)
    sc_k = zs1[:, ODD_OFF[1]:ODD_OFF[2]].reshape(1, db, 1, C_KV, C_DH)
    sc_v = zs1[:, ODD_OFF[2]:ODD_OFF[3]].reshape(1, db, 1, C_KV, C_DH)
    sc_idx = zs1[:, None, ODD_OFF[5]:ODD_OFF[6]][None]

    return (y_prompt, y_sample, pa_k, pa_v, pb_s[None], pb_conv, pc_k, pc_v, pc_idx,
            sa_k, sa_v, sb_s[None], sb_conv, sc_k, sc_v, sc_idx)
```

```python
import functools
import math

import jax
import jax.numpy as jnp
import numpy as np
from jax import lax
from jax.experimental import pallas as pl
from jax.experimental.pallas import tpu as pltpu

F32 = jnp.float32
BF16 = jnp.bfloat16
I32 = jnp.int32

EPS = 1e-6
N_BUCKETS = 32
MAX_DIST = 128
FAR_DIST = MAX_DIST

A_HEADS, A_KV, A_GROUP, A_DH, A_DV = 8, 2, 4, 64, 128
B_HEADS, B_DK, B_DV, CONV_W = 4, 128, 128, 4
C_HEADS, C_KV, C_GROUP, C_DH = 8, 2, 4, 128
IDX_HEADS, IDX_DIM, TOPK_MAX = 8, 64, 256
B_QKV = 2 * B_HEADS * B_DK + B_HEADS * B_DV

EVEN_COLS = (A_HEADS * 2 * A_DH, A_KV * 2 * A_DH, A_KV * A_DV, A_HEADS * A_DV, B_QKV, B_HEADS * B_DV, B_HEADS, B_HEADS)
ODD_COLS = (C_HEADS * C_DH, C_KV * C_DH, C_KV * C_DH, C_HEADS * C_DH, IDX_HEADS * IDX_DIM, IDX_DIM, IDX_HEADS)
EVEN_OFF = tuple(int(v) for v in np.cumsum((0,) + EVEN_COLS))
ODD_OFF = tuple(int(v) for v in np.cumsum((0,) + ODD_COLS))

LANES = 128
SUBLANES = 8
BLK = 128
CHUNK = 512
SUBS = CHUNK // BLK
NEG = -1e30
INT_MIN = -2 ** 31
INT_MAX = 2 ** 31 - 1
LOG2E = math.log2(math.e)
VMEM_LIMIT = 56 * 1024 * 1024
HI = lax.Precision.HIGHEST

NT = (((1,), (1,)), ((), ()))


def _round_up(x, m):
    return (x + m - 1) // m * m


def _cparams(sem):
    return pltpu.CompilerParams(dimension_semantics=sem, vmem_limit_bytes=VMEM_LIMIT)


def _silu(x):
    return x * jax.nn.sigmoid(x)


def _sortable_key(score):
    score = jnp.where(score == 0.0, 0.0, score)
    bits = pltpu.bitcast(score, I32)
    return bits ^ ((bits >> 31) & jnp.int32(0x7FFFFFFF))


def _lane_tile(x, n):
    return jnp.concatenate([x] * n, axis=1)


def _proj_kernel(x_ref, g_ref, w_ref, of_ref, ob_ref, xn_ref, *, bf16_tiles):
    j = pl.program_id(1)

    @pl.when(j == 0)
    def _():
        xf = x_ref[...]
        y = xf * lax.rsqrt(jnp.mean(xf * xf, axis=-1, keepdims=True) + EPS)
        xn_ref[...] = (y * g_ref[...]).astype(BF16)

    acc = jnp.dot(xn_ref[...], w_ref[...], preferred_element_type=F32)
    of_ref[...] = acc

    @pl.when((j >= bf16_tiles[0]) & (j < bf16_tiles[1]))
    def _():
        ob_ref[...] = acc.astype(BF16)


def _norm_proj(x, gain, w_bf16, tm, tn, bf16_tiles):
    m, d = x.shape
    n = w_bf16.shape[1]
    t0, t1 = bf16_tiles
    return pl.pallas_call(
        functools.partial(_proj_kernel, bf16_tiles=bf16_tiles),
        grid=(m // tm, n // tn),
        in_specs=[pl.BlockSpec((tm, d), lambda i, j: (i, 0)),
                  pl.BlockSpec((1, d), lambda i, j: (0, 0)),
                  pl.BlockSpec((d, tn), lambda i, j: (0, j))],
        out_specs=[pl.BlockSpec((tm, tn), lambda i, j: (i, j)),
                   pl.BlockSpec((tm, tn), lambda i, j: (i, jnp.clip(j - t0, 0, t1 - t0 - 1)))],
        out_shape=[jax.ShapeDtypeStruct((m, n), F32), jax.ShapeDtypeStruct((m, (t1 - t0) * tn), BF16)],
        scratch_shapes=[pltpu.VMEM((tm, d), BF16)],
        compiler_params=_cparams(("parallel", "arbitrary")),
        name="norm_proj",
    )(x, gain.reshape(1, d), w_bf16)


def _outproj_kernel(*refs, n_lhs, final_norm):
    ys = refs[:n_lhs]
    ws = refs[n_lhs:2 * n_lhs]
    h_ref = refs[2 * n_lhs]
    pos = 2 * n_lhs + 1
    fn_ref = refs[pos] if final_norm else None
    out_ref = refs[-1]
    acc = h_ref[...]
    for y_ref, w_ref in zip(ys, ws):
        acc = acc + jnp.dot(y_ref[...], w_ref[...], preferred_element_type=F32)
    if final_norm:
        y = acc * lax.rsqrt(jnp.mean(acc * acc, axis=-1, keepdims=True) + EPS)
        acc = y * fn_ref[...]
    out_ref[...] = acc


def _out_proj(ys, ws, h, tm, final_gain=None):
    m, n = h.shape
    n_lhs = len(ys)
    in_specs = [pl.BlockSpec((tm, y.shape[1]), lambda i: (i, 0)) for y in ys]
    in_specs += [pl.BlockSpec(w.shape, lambda i: (0, 0)) for w in ws]
    in_specs += [pl.BlockSpec((tm, n), lambda i: (i, 0))]
    args = list(ys) + list(ws) + [h]
    if final_gain is not None:
        in_specs += [pl.BlockSpec((1, n), lambda i: (0, 0))]
        args += [final_gain.reshape(1, n)]
    return pl.pallas_call(
        functools.partial(_outproj_kernel, n_lhs=n_lhs, final_norm=final_gain is not None),
        grid=(m // tm,),
        in_specs=in_specs,
        out_specs=pl.BlockSpec((tm, n), lambda i: (i, 0)),
        out_shape=jax.ShapeDtypeStruct((m, n), F32),
        compiler_params=_cparams(("parallel",)),
        name="out_proj",
    )(*args)


def _flash_steps(ss, v1s, m_refs):
    each = range(len(ss))
    m_old = [m_refs[i][0][m_refs[i][2]] for i in each]
    m_new = [jnp.maximum(m_old[i], jnp.max(ss[i], axis=1, keepdims=True)) for i in each]
    alpha = [jnp.exp2(m_old[i] - m_new[i]) for i in each]
    p = [jnp.exp2(ss[i] - _lane_tile(m_new[i], ss[i].shape[1] // LANES)).astype(BF16) for i in each]
    pv = [jnp.dot(p[i], v1s[i], preferred_element_type=F32) for i in each]
    for i in each:
        m_ref, acc_ref, idx = m_refs[i]
        acc_ref[idx] = _lane_tile(alpha[i], 2) * acc_ref[idx] + pv[i]
        m_ref[idx] = m_new[i]


def _far_chunk_loop(n_far, run_chunks):
    def quad(i, carry):
        run_chunks([4 * i, 4 * i + 1, 4 * i + 2, 4 * i + 3])
        return carry

    n_quad = lax.shift_right_logical(n_far, 2)
    lax.fori_loop(0, n_quad, quad, 0)

    @pl.when(jnp.bitwise_and(n_far, 2) == 2)
    def _():
        run_chunks([4 * n_quad, 4 * n_quad + 1])

    @pl.when(jnp.bitwise_and(n_far, 1) == 1)
    def _():
        run_chunks([n_far - 1])


TILE_ZERO, TILE_PREV, TILE_DIAG, TILE_MASKED = range(4)


def _near_bias_kind(sb, r):
    rel = sb - SUBS - r
    return jnp.where(rel == 0, TILE_DIAG, jnp.where(rel == -1, TILE_PREV, jnp.where(rel < -1, TILE_ZERO, TILE_MASKED)))


def _diff_attn_kernel(q_ref, k_ref, v_ref, gate_ref, tiles_ref, lamv_ref, subln_ref, out_ref,
                      qm_ref, m_ref, acc_ref, *, lam_init, nq_real):
    qi = pl.program_id(2)

    @pl.when(qi >= nq_real)
    def _():
        out_ref[...] = jnp.zeros(out_ref.shape, out_ref.dtype)

    @pl.when(qi < nq_real)
    def _():
        lane = lax.broadcasted_iota(I32, (BLK, 2 * A_DH), 1)
        for g in range(A_GROUP):
            qg = q_ref[:, g * 2 * A_DH:(g + 1) * 2 * A_DH] * (A_DH ** -0.5 * LOG2E)
            for c in range(2):
                keep = (lane < A_DH) if c == 0 else (lane >= A_DH)
                qm_ref[c, g * BLK:(g + 1) * BLK, :] = jnp.where(keep, qg, 0.0).astype(BF16)
        m_ref[...] = jnp.full(m_ref.shape, NEG, F32)
        acc_ref[...] = jnp.zeros(acc_ref.shape, F32)
        r = jnp.bitwise_and(qi, SUBS - 1)
        jd = lax.shift_right_logical(qi, int(math.log2(SUBS)))
        ones = jnp.ones((CHUNK, A_DV), BF16)

        def run_chunks(js, first_sbs=None):
            starts = [pl.multiple_of(j * CHUNK, CHUNK) for j in js]
            k = jnp.concatenate([k_ref[pl.ds(st, CHUNK), :] for st in starts], axis=0)
            v1 = jnp.concatenate(
                [jnp.concatenate([v_ref[pl.ds(st, CHUNK), :], ones], axis=1) for st in starts], axis=0)
            ss = [lax.dot_general(qm_ref[c], k, NT, preferred_element_type=F32) for c in range(2)]
            if first_sbs is not None:
                bias = jnp.concatenate(
                    [tiles_ref[_near_bias_kind(sb + i, r)] for sb in first_sbs for i in range(SUBS)], axis=1)
                ss = [s + bias for s in ss]
            _flash_steps(ss, [v1, v1], [(m_ref, acc_ref, c) for c in range(2)])

        def near_pair(rr):
            n_diag = (rr + 1) * BLK
            st_prev = pl.multiple_of((jd - 1) * CHUNK, CHUNK)
            st_diag = pl.multiple_of(jd * CHUNK, CHUNK)
            k = jnp.concatenate([k_ref[pl.ds(st_prev, CHUNK), :], k_ref[pl.ds(st_diag, n_diag), :]], axis=0)
            v = jnp.concatenate([v_ref[pl.ds(st_prev, CHUNK), :], v_ref[pl.ds(st_diag, n_diag), :]], axis=0)
            v1 = jnp.concatenate([v, jnp.ones((CHUNK + n_diag, A_DV), BF16)], axis=1)
            bias = jnp.concatenate([tiles_ref[TILE_PREV], tiles_ref[TILE_DIAG]], axis=1)
            first = CHUNK + (rr - 1) * BLK
            ss = []
            for c in range(2):
                s = lax.dot_general(qm_ref[c], k, NT, preferred_element_type=F32)
                ss.append(jnp.concatenate([s[:, :first], s[:, first:] + bias], axis=1))
            _flash_steps(ss, [v1, v1], [(m_ref, acc_ref, c) for c in range(2)])

        _far_chunk_loop(jnp.maximum(jd - 1, 0), run_chunks)

        for rr in range(SUBS):
            @pl.when((jd >= 1) & (r == rr))
            def _(rr=rr):
                near_pair(rr)

        @pl.when(jd == 0)
        def _():
            run_chunks([jd], [SUBS])

        lv = lamv_ref[...]
        lam = (jnp.exp(jnp.sum(lv[0:1] * lv[1:2], axis=1, keepdims=True))
               - jnp.exp(jnp.sum(lv[2:3] * lv[3:4], axis=1, keepdims=True)) + lam_init)
        a0 = acc_ref[0]
        a1 = acc_ref[1]
        o = a0[:, :A_DV] / a0[:, A_DV:] - lam * (a1[:, :A_DV] / a1[:, A_DV:])
        y = o * lax.rsqrt(jnp.mean(o * o, axis=-1, keepdims=True) + EPS)
        y = (y * subln_ref[...]) * (1.0 - lam_init)
        for g in range(A_GROUP):
            gate = gate_ref[:, g * A_DV:(g + 1) * A_DV]
            out_ref[:, g * A_DV:(g + 1) * A_DV] = (y[g * BLK:(g + 1) * BLK] * _silu(gate)).astype(BF16)


def _diff_attn_prompt(zf, zb, zb_col0, near_tiles, lamv, subln, nb, lp, nq_real, lam_init):
    nq = lp // BLK
    qw = A_GROUP * 2 * A_DH
    k_blk0 = (EVEN_OFF[1] - zb_col0) // (2 * A_DH)
    v_blk0 = (EVEN_OFF[2] - zb_col0) // A_DV
    g_blk0 = EVEN_OFF[3] // (A_GROUP * A_DV)
    rows = A_GROUP * BLK
    return pl.pallas_call(
        functools.partial(_diff_attn_kernel, lam_init=lam_init, nq_real=nq_real),
        grid=(nb, A_KV, nq),
        in_specs=[pl.BlockSpec((BLK, qw), lambda b, h, i: (b * nq + i, h)),
                  pl.BlockSpec((lp, 2 * A_DH), lambda b, h, i: (b, k_blk0 + h)),
                  pl.BlockSpec((lp, A_DV), lambda b, h, i: (b, v_blk0 + h)),
                  pl.BlockSpec((BLK, A_GROUP * A_DV), lambda b, h, i: (b * nq + i, g_blk0 + h)),
                  pl.BlockSpec((None, 4, rows, BLK), lambda b, h, i: (h, 0, 0, 0)),
                  pl.BlockSpec((4, A_DH), lambda b, h, i: (0, 0)),
                  pl.BlockSpec((1, A_DV), lambda b, h, i: (0, 0))],
        out_specs=pl.BlockSpec((BLK, A_GROUP * A_DV), lambda b, h, i: (b * nq + i, h)),
        out_shape=jax.ShapeDtypeStruct((nb * lp, A_HEADS * A_DV), BF16),
        scratch_shapes=[pltpu.VMEM((2, rows, 2 * A_DH), BF16),
                        pltpu.VMEM((2, rows, LANES), F32),
                        pltpu.VMEM((2, rows, 2 * A_DV), F32)],
        compiler_params=_cparams(("parallel", "parallel", "arbitrary")),
        name="diff_attn_prompt",
    )(zf, zb, zb, zf, near_tiles, lamv, subln.reshape(1, A_DV))


def _softplus(x):
    return jnp.maximum(x, 0.0) + jnp.log1p(jnp.exp(-jnp.abs(x)))


def _split_bf16(a):
    hi = a.astype(BF16)
    return hi, (a - hi.astype(F32)).astype(BF16)


def _dot3(a, b):
    (ah, al), (bh, bl) = a, b
    return (jnp.dot(ah, bh, preferred_element_type=F32)
            + (jnp.dot(ah, bl, preferred_element_type=F32) + jnp.dot(al, bh, preferred_element_type=F32)))


def _gdn_chunk_math(s0, q, k, v, gb, betab, row_ge, row_gt):
    n = len(q)
    c = q[0].shape[0]
    each = range(n)
    dot = functools.partial(jnp.dot, preferred_element_type=F32)
    dot_nt = functools.partial(lax.dot_general, dimension_numbers=NT, preferred_element_type=F32)
    ltri = jnp.where(row_ge, 1.0, 0.0).astype(F32)
    gcum = [jnp.dot(ltri, gb[i], precision=HI, preferred_element_type=F32) for i in each]
    decay = [jnp.where(row_ge, jnp.exp(jnp.where(row_ge, gcum[i] - gcum[i].T, 0.0)), 0.0) for i in each]
    kb = [k[i].astype(BF16) for i in each]
    qb = [q[i].astype(BF16) for i in each]
    s0b = [s0[i].astype(BF16) for i in each]
    kk = [dot_nt(kb[i], kb[i]) for i in each]
    x = [-jnp.where(row_gt, betab[i] * decay[i] * kk[i], 0.0) for i in each]
    tm = list(x)
    ps = [_split_bf16(x[i]) for i in each]
    for _ in range(int(math.log2(c)) - 1):
        p = [_dot3(ps[i], ps[i]) for i in each]
        ps = [_split_bf16(p[i]) for i in each]
        tms = [_split_bf16(tm[i]) for i in each]
        tm = [tm[i] + p[i] + _dot3(tms[i], ps[i]) for i in each]
    eg = [jnp.exp(gcum[i]) for i in each]
    ks = [dot(kb[i], s0b[i]) for i in each]
    rhs = [betab[i] * (v[i] - eg[i] * ks[i]) for i in each]
    u = [rhs[i] + dot(tm[i].astype(BF16), rhs[i].astype(BF16)) for i in each]
    ub = [u[i].astype(BF16) for i in each]
    qk = [dot_nt(qb[i], kb[i]) * decay[i] for i in each]
    o = [eg[i] * dot(qb[i], s0b[i]) + dot(qk[i].astype(BF16), ub[i]) for i in each]
    glast = [gcum[i][c - 1:c, :] for i in each]
    kd = [k[i] * jnp.exp(glast[i] - gcum[i]) for i in each]
    s_new = [jnp.exp(glast[i]) * s0[i] + dot(kd[i].T.astype(BF16), ub[i]) for i in each]
    return s_new, o


def _gdn_prompt_kernel(xq_ref, xk_ref, xv_ref, gate_ref, ab_ref, convw_ref, gp_ref, normg_ref,
                       y_ref, sfin_ref, xbuf_ref, s_ref, *, seq_len):
    ci = pl.program_id(0)
    hw = B_HEADS * B_DK
    nb = xq_ref.shape[0]

    @pl.when(ci == 0)
    def _():
        xbuf_ref[:, 0:SUBLANES, :] = jnp.zeros((nb, SUBLANES, B_QKV), F32)
        s_ref[...] = jnp.zeros(s_ref.shape, F32)

    row = lax.broadcasted_iota(I32, (BLK, BLK), 0)
    col = lax.broadcasted_iota(I32, (BLK, BLK), 1)
    row_ge = row >= col
    row_gt = row > col
    valid = (ci * BLK + row) < seq_len
    gp = gp_ref[...]
    chains = [(b, h) for b in range(nb) for h in range(B_HEADS)]
    qs, ks, vs, gbs, betabs = [], [], [], [], []
    for b in range(nb):
        xbuf_ref[b, SUBLANES:SUBLANES + BLK, 0:hw] = xq_ref[b]
        xbuf_ref[b, SUBLANES:SUBLANES + BLK, hw:2 * hw] = xk_ref[b]
        xbuf_ref[b, SUBLANES:SUBLANES + BLK, 2 * hw:3 * hw] = xv_ref[b]
        conv = jnp.zeros((BLK, B_QKV), F32)
        for i in range(CONV_W):
            conv = conv + convw_ref[i:i + 1, :] * xbuf_ref[b, pl.ds(SUBLANES - (CONV_W - 1) + i, BLK), :]
        tail = xbuf_ref[b, BLK:BLK + SUBLANES, :]
        xbuf_ref[b, 0:SUBLANES, :] = tail
        act = _silu(conv)
        ab = ab_ref[b]
        for h in range(B_HEADS):
            q = act[:, h * B_DK:(h + 1) * B_DK]
            k = act[:, hw + h * B_DK:hw + (h + 1) * B_DK]
            v = act[:, 2 * hw + h * B_DV:2 * hw + (h + 1) * B_DV]
            q = q * lax.rsqrt(jnp.sum(q * q, axis=-1, keepdims=True) + EPS) * (B_DK ** -0.5)
            k = k * lax.rsqrt(jnp.sum(k * k, axis=-1, keepdims=True) + EPS)
            a_raw = jnp.broadcast_to(ab[:, h:h + 1], (BLK, BLK))
            b_raw = jnp.broadcast_to(ab[:, B_HEADS + h:B_HEADS + h + 1], (BLK, BLK))
            a_log = gp[0:1, h:h + 1]
            dt_b = gp[1:2, h:h + 1]
            qs.append(q)
            ks.append(k)
            vs.append(v)
            gbs.append(jnp.where(valid, -jnp.exp(a_log) * _softplus(a_raw + dt_b), 0.0))
            betabs.append(jnp.where(valid, jax.nn.sigmoid(b_raw), 0.0))
    s_new, outs = _gdn_chunk_math([s_ref[b, h] for b, h in chains], qs, ks, vs, gbs, betabs, row_ge, row_gt)
    for (b, h), s_bh, o in zip(chains, s_new, outs):
        s_ref[b, h] = s_bh
        y = o * lax.rsqrt(jnp.mean(o * o, axis=-1, keepdims=True) + EPS) * normg_ref[...]
        gate = gate_ref[b, :, h * B_DV:(h + 1) * B_DV]
        y_ref[b, :, h * B_DV:(h + 1) * B_DV] = (y * _silu(gate)).astype(BF16)

    @pl.when(ci == pl.num_programs(0) - 1)
    def _():
        sfin_ref[...] = s_ref[...]


def _gdn_prompt(zf, conv_w, a_log, dt_bias, norm_g, nb, lp, seq_len):
    nc = lp // BLK
    hw = B_HEADS * B_DK
    c0 = EVEN_OFF[4] // hw
    g0 = EVEN_OFF[5] // hw
    ab0 = EVEN_OFF[6] // LANES
    gp = jnp.zeros((SUBLANES, LANES), F32).at[0, :B_HEADS].set(a_log).at[1, :B_HEADS].set(dt_bias)
    z3 = zf.reshape(nb, lp, zf.shape[1])
    y, s_fin = pl.pallas_call(
        functools.partial(_gdn_prompt_kernel, seq_len=seq_len),
        grid=(nc,),
        in_specs=[pl.BlockSpec((nb, BLK, hw), lambda i: (0, i, c0)),
                  pl.BlockSpec((nb, BLK, hw), lambda i: (0, i, c0 + 1)),
                  pl.BlockSpec((nb, BLK, hw), lambda i: (0, i, c0 + 2)),
                  pl.BlockSpec((nb, BLK, hw), lambda i: (0, i, g0)),
                  pl.BlockSpec((nb, BLK, LANES), lambda i: (0, i, ab0)),
                  pl.BlockSpec((CONV_W, B_QKV), lambda i: (0, 0)),
                  pl.BlockSpec((SUBLANES, LANES), lambda i: (0, 0)),
                  pl.BlockSpec((1, B_DV), lambda i: (0, 0))],
        out_specs=[pl.BlockSpec((nb, BLK, hw), lambda i: (0, i, 0)),
                   pl.BlockSpec((nb, B_HEADS, B_DK, B_DV), lambda i: (0, 0, 0, 0))],
        out_shape=[jax.ShapeDtypeStruct((nb, lp, hw), BF16),
                   jax.ShapeDtypeStruct((nb, B_HEADS, B_DK, B_DV), F32)],
        scratch_shapes=[pltpu.VMEM((nb, BLK + SUBLANES, B_QKV), F32),
                        pltpu.VMEM((nb, B_HEADS, B_DK, B_DV), F32)],
        compiler_params=_cparams(("arbitrary",)),
        name="gdn_prompt",
    )(z3, z3, z3, z3, z3, conv_w, gp, norm_g.reshape(1, B_DV))
    return y.reshape(nb * lp, hw), s_fin


def _kth_largest(count_fn, k_sel, shape):
    def step(i, tau):
        cand = tau + lax.shift_left(jnp.int32(1), 31 - i)
        cnt = count_fn(lambda key, idx: key >= cand)
        return jnp.where(cnt >= k_sel, cand, tau)

    return lax.fori_loop(0, 32, step, jnp.full(shape, INT_MIN, I32))


BITS_PER_CHECK = 4


def _kth_separator(count_fn, k_sel, n_keys):
    assert 32 % BITS_PER_CHECK == 0

    def pending(cnt_tau):
        return jnp.max(jnp.where(cnt_tau > k_sel, 1.0, 0.0)) > 0.5

    def group(carry):
        g, tau, cnt_tau, _ = carry
        for b in range(BITS_PER_CHECK):
            cand = tau + lax.shift_left(jnp.int32(1), 31 - (g * BITS_PER_CHECK + b))
            cnt = count_fn(lambda key, idx: key >= cand)
            take = cnt >= k_sel
            tau = jnp.where(take, cand, tau)
            cnt_tau = jnp.where(take, cnt, cnt_tau)
        return g + 1, tau, cnt_tau, pending(cnt_tau)

    init = (jnp.int32(0), jnp.full(n_keys.shape, INT_MIN, I32), n_keys, pending(n_keys))
    out = lax.while_loop(lambda c: (c[0] < 32 // BITS_PER_CHECK) & c[3], group, init)
    return out[1], out[2]


def _tie_cutoff(count_fn, tau, need, n_index_bits, shape):
    def step(i, jm):
        cand = jm | lax.shift_left(jnp.int32(1), n_index_bits - 1 - i)
        cnt = count_fn(lambda key, idx: (key == tau) & (idx < cand))
        return jnp.where(cnt < need, cand, jm)

    return lax.fori_loop(0, n_index_bits, step, jnp.zeros(shape, I32))


def _sparse_attn_kernel(qi_ref, kw_ref, kidx_ref, qc_ref, kc_ref, vc_ref, g0_ref, g1_ref, tiles_ref, out_ref,
                        keys_ref, qm_ref, qh_ref, tau_ref, jmax_ref, m_ref, acc_ref,
                        *, k_sel, n_index_bits, nq_real):
    qb = pl.program_id(1)

    @pl.when(qb >= nq_real)
    def _():
        out_ref[...] = jnp.zeros(out_ref.shape, out_ref.dtype)

    @pl.when(qb < nq_real)
    def _():
        lane = lax.broadcasted_iota(I32, (BLK, LANES), 1)
        key_pos = lax.broadcasted_iota(I32, (CHUNK, BLK), 0)
        qry_pos = lax.broadcasted_iota(I32, (CHUNK, BLK), 1)
        r = jnp.bitwise_and(qb, SUBS - 1)
        jd = lax.shift_right_logical(qb, int(math.log2(SUBS)))

        for p in range(IDX_HEADS // 2):
            pair = qi_ref[:, p * LANES:(p + 1) * LANES] * (IDX_DIM ** -0.5)
            swapped = pltpu.roll(pair, IDX_DIM, 1)
            for e, src in enumerate((pair, swapped)):
                h = 2 * p + e
                qm_ref[h * BLK:(h + 1) * BLK, :] = jnp.where(lane < IDX_DIM, src, 0.0).astype(BF16)
        w_rows = kw_ref[...].T * (IDX_HEADS ** -0.5)

        def scores_t(j):
            start = pl.multiple_of(j * CHUNK, CHUNK)
            d = lax.dot_general(kidx_ref[pl.ds(start, CHUNK), :], qm_ref[...], NT, preferred_element_type=F32)
            sc = jnp.zeros((CHUNK, BLK), F32)
            for h in range(IDX_HEADS):
                sc = sc + w_rows[IDX_DIM + h:IDX_DIM + h + 1, :] * jnp.maximum(d[:, h * BLK:(h + 1) * BLK], 0.0)
            return sc

        def score_quad(i, carry):
            for j in (4 * i, 4 * i + 1, 4 * i + 2, 4 * i + 3):
                keys_ref[j] = _sortable_key(scores_t(j))
            return carry

        n_quad = lax.shift_right_logical(jd, 2)
        lax.fori_loop(0, n_quad, score_quad, 0)

        @pl.when(jnp.bitwise_and(jd, 2) == 2)
        def _():
            for j in (4 * n_quad, 4 * n_quad + 1):
                keys_ref[j] = _sortable_key(scores_t(j))

        @pl.when(jnp.bitwise_and(jd, 1) == 1)
        def _():
            keys_ref[jd - 1] = _sortable_key(scores_t(jd - 1))

        admissible = (jd * CHUNK + key_pos) <= (qb * BLK + qry_pos)
        keys_ref[jd] = _sortable_key(jnp.where(admissible, scores_t(jd), -jnp.inf))

        def count_fn(pred):
            part = SUBLANES * SUBLANES

            def body(j, acc):
                hit = jnp.where(pred(keys_ref[j], j * CHUNK + key_pos), 1.0, 0.0)
                return acc + jnp.sum(hit.reshape(CHUNK // part, part, BLK), axis=0)
            acc = lax.fori_loop(0, jd + 1, body, jnp.zeros((part, BLK), F32))
            return jnp.sum(acc, axis=0, keepdims=True).astype(I32)

        n_keys = qb * BLK + lax.broadcasted_iota(I32, (1, BLK), 1) + 1
        tau, n_ge = _kth_separator(count_fn, k_sel, n_keys)
        tau_ref[...] = tau
        jmax_ref[...] = jnp.full((1, BLK), INT_MAX, I32)
        need = k_sel - count_fn(lambda key, idx: key > tau)
        excess = jnp.max(jnp.where(n_ge > k_sel, 1.0, 0.0)) > 0.5

        @pl.when(excess)
        def _():
            jmax_ref[...] = _tie_cutoff(count_fn, tau, need, n_index_bits, (1, BLK))

        jmax = jmax_ref[...]

        for h in range(C_KV):
            for g in range(C_GROUP):
                col = (h * C_GROUP + g) * C_DH
                qh_ref[h, g * BLK:(g + 1) * BLK, :] = (
                    qc_ref[:, col:col + C_DH] * (C_DH ** -0.5 * LOG2E)).astype(BF16)
        m_ref[...] = jnp.full(m_ref.shape, NEG, F32)
        acc_ref[...] = jnp.zeros(acc_ref.shape, F32)
        ones = jnp.ones((CHUNK, C_DH), BF16)

        def run_chunks(js, first_sbs=None):
            starts = [pl.multiple_of(j * CHUNK, CHUNK) for j in js]
            masks = []
            for j in js:
                key = keys_ref[j]
                tie_ok = jnp.where((j * CHUNK + key_pos) <= jmax, 0.0, NEG)
                sel_t = jnp.where(key > tau, 0.0, jnp.where(key == tau, tie_ok, NEG))
                masks += [sel_t[i * BLK:(i + 1) * BLK, :].T for i in range(SUBS)]
            selneg = jnp.concatenate(masks, axis=1)
            selneg = jnp.concatenate([selneg] * C_GROUP, axis=0)
            ss, v1s = [], []
            for h in range(C_KV):
                cols = slice(h * C_DH, (h + 1) * C_DH)
                k = jnp.concatenate([kc_ref[pl.ds(st, CHUNK), cols] for st in starts], axis=0)
                v1s.append(jnp.concatenate(
                    [jnp.concatenate([vc_ref[pl.ds(st, CHUNK), cols], ones], axis=1) for st in starts], axis=0))
                s = lax.dot_general(qh_ref[h], k, NT, preferred_element_type=F32) + selneg
                if first_sbs is not None:
                    s = s + jnp.concatenate(
                        [tiles_ref[h, _near_bias_kind(sb + i, r)] for sb in first_sbs for i in range(SUBS)], axis=1)
                ss.append(s)
            _flash_steps(ss, v1s, [(m_ref, acc_ref, h) for h in range(C_KV)])

        def mask_blocks(j, n_blocks):
            key = keys_ref[j][:n_blocks * BLK]
            tie_ok = jnp.where((j * CHUNK + key_pos[:n_blocks * BLK]) <= jmax, 0.0, NEG)
            sel_t = jnp.where(key > tau, 0.0, jnp.where(key == tau, tie_ok, NEG))
            return [sel_t[i * BLK:(i + 1) * BLK, :].T for i in range(n_blocks)]

        def near_pair(rr):
            n_diag = (rr + 1) * BLK
            st_prev = pl.multiple_of((jd - 1) * CHUNK, CHUNK)
            st_diag = pl.multiple_of(jd * CHUNK, CHUNK)
            selneg = jnp.concatenate(mask_blocks(jd - 1, SUBS) + mask_blocks(jd, rr + 1), axis=1)
            selneg = jnp.concatenate([selneg] * C_GROUP, axis=0)
            first = CHUNK + (rr - 1) * BLK
            ones_n = jnp.ones((CHUNK + n_diag, C_DH), BF16)
            ss, v1s = [], []
            for h in range(C_KV):
                cols = slice(h * C_DH, (h + 1) * C_DH)
                k = jnp.concatenate([kc_ref[pl.ds(st_prev, CHUNK), cols], kc_ref[pl.ds(st_diag, n_diag), cols]], axis=0)
                v = jnp.concatenate([vc_ref[pl.ds(st_prev, CHUNK), cols], vc_ref[pl.ds(st_diag, n_diag), cols]], axis=0)
                v1s.append(jnp.concatenate([v, ones_n], axis=1))
                s = lax.dot_general(qh_ref[h], k, NT, preferred_element_type=F32) + selneg
                bias = jnp.concatenate([tiles_ref[h, TILE_PREV], tiles_ref[h, TILE_DIAG]], axis=1)
                ss.append(jnp.concatenate([s[:, :first], s[:, first:] + bias], axis=1))
            _flash_steps(ss, v1s, [(m_ref, acc_ref, h) for h in range(C_KV)])

        _far_chunk_loop(jnp.maximum(jd - 1, 0), run_chunks)

        for rr in range(SUBS):
            @pl.when((jd >= 1) & (r == rr))
            def _(rr=rr):
                near_pair(rr)

        @pl.when(jd == 0)
        def _():
            run_chunks([jd], [SUBS])

        half = (C_HEADS // 2) * C_DH
        for h in range(C_KV):
            a = acc_ref[h]
            o = a[:, :C_DH] / a[:, C_DH:]
            gref = g0_ref if h == 0 else g1_ref
            for g in range(C_GROUP):
                gate = gref[:, g * C_DH:(g + 1) * C_DH]
                out_ref[:, h * half + g * C_DH:h * half + (g + 1) * C_DH] = (
                    o[g * BLK:(g + 1) * BLK] * _silu(gate)).astype(BF16)


def _sparse_attn_prompt(zf, zb, near_tiles, nb, lp, nq_real, k_sel):
    nq = lp // BLK
    half = (C_HEADS // 2) * C_DH
    qi0 = ODD_OFF[4] // (IDX_HEADS * IDX_DIM)
    ki0 = ODD_OFF[5] // LANES
    kc0 = ODD_OFF[1] // (C_KV * C_DH)
    vc0 = ODD_OFF[2] // (C_KV * C_DH)
    g0 = ODD_OFF[3] // half
    rows = C_GROUP * BLK
    n_index_bits = max(1, int(math.ceil(math.log2(lp))))
    return pl.pallas_call(
        functools.partial(_sparse_attn_kernel, k_sel=k_sel, n_index_bits=n_index_bits, nq_real=nq_real),
        grid=(nb, nq),
        in_specs=[pl.BlockSpec((BLK, IDX_HEADS * IDX_DIM), lambda b, i: (b * nq + i, qi0)),
                  pl.BlockSpec((BLK, LANES), lambda b, i: (b * nq + i, ki0)),
                  pl.BlockSpec((lp, LANES), lambda b, i: (b, ki0)),
                  pl.BlockSpec((BLK, C_HEADS * C_DH), lambda b, i: (b * nq + i, 0)),
                  pl.BlockSpec((lp, C_KV * C_DH), lambda b, i: (b, kc0)),
                  pl.BlockSpec((lp, C_KV * C_DH), lambda b, i: (b, vc0)),
                  pl.BlockSpec((BLK, half), lambda b, i: (b * nq + i, g0)),
                  pl.BlockSpec((BLK, half), lambda b, i: (b * nq + i, g0 + 1)),
                  pl.BlockSpec((C_KV, 4, rows, BLK), lambda b, i: (0, 0, 0, 0))],
        out_specs=pl.BlockSpec((BLK, C_HEADS * C_DH), lambda b, i: (b * nq + i, 0)),
        out_shape=jax.ShapeDtypeStruct((nb * lp, C_HEADS * C_DH), BF16),
        scratch_shapes=[pltpu.VMEM((lp // CHUNK, CHUNK, BLK), I32),
                        pltpu.VMEM((IDX_HEADS * BLK, LANES), BF16),
                        pltpu.VMEM((C_KV, rows, C_DH), BF16),
                        pltpu.VMEM((1, BLK), I32),
                        pltpu.VMEM((1, BLK), I32),
                        pltpu.VMEM((C_KV, rows, LANES), F32),
                        pltpu.VMEM((C_KV, rows, 2 * C_DH), F32)],
        compiler_params=_cparams(("parallel", "arbitrary")),
        name="sparse_attn_prompt",
    )(zf, zf, zb, zf, zb, zb, zf, zf, near_tiles)


PAGES_PER_STEP = 32
INDEX_PAGES_PER_STEP = 64
DEC_ROWS = 16


def _interleaved_pages(refs, page):
    halves = [jnp.concatenate([r[pl.ds(h, page, stride=2), :] for r in refs], axis=0) for h in range(2)]
    return jnp.concatenate(halves, axis=1).astype(BF16)


def _paged_attn_kernel(*refs, n_pages_step, masked, page, k_feature_major):
    if masked:
        pt_ref, tau_ref, jmax_ref, selnew_ref = refs[:4]
        refs = refs[4:]
    else:
        pt_ref = refs[0]
        refs = refs[1:]
    q_ref, knew_ref, vnew_ref, bfar_ref, blast_ref, b0_ref = refs[:6]
    refs = refs[6:]
    if masked:
        keys_ref = refs[0]
        refs = refs[1:]
    k_refs = refs[:n_pages_step]
    v_refs = refs[n_pages_step:2 * n_pages_step]
    out_ref, m_ref, l_ref, acc_ref = refs[2 * n_pages_step:]
    del pt_ref
    b = pl.program_id(0)
    j = pl.program_id(1)
    width = n_pages_step * page
    q = q_ref[...]

    @pl.when(j == 0)
    def _():
        s_new = jnp.sum(q.astype(F32) * knew_ref[...].astype(BF16).astype(F32), axis=1, keepdims=True) + b0_ref[...]
        v_new = jnp.broadcast_to(vnew_ref[...].astype(BF16).astype(F32), acc_ref.shape)
        if masked:
            take = selnew_ref[b] > 0
            m_ref[...] = jnp.where(take, s_new, NEG)
            l_ref[...] = jnp.where(take, 1.0, 0.0) * jnp.ones(l_ref.shape, F32)
            acc_ref[...] = jnp.where(take, v_new, 0.0)
        else:
            m_ref[...] = s_new
            l_ref[...] = jnp.ones(l_ref.shape, F32)
            acc_ref[...] = v_new

    if k_feature_major:
        kcat = jnp.concatenate([r[...] for r in k_refs], axis=1).astype(BF16)
        s = jnp.dot(q, kcat, preferred_element_type=F32)
    else:
        s = lax.dot_general(q, _interleaved_pages(k_refs, page), NT, preferred_element_type=F32)
    vcat = _interleaved_pages(v_refs, page)
    s = s + jnp.where(j == pl.num_programs(1) - 1, blast_ref[...], bfar_ref[...])
    if masked:
        key = keys_ref[...]
        idx = j * width + lax.broadcasted_iota(I32, (1, width), 1)
        tau = tau_ref[b]
        sel = (key > tau) | ((key == tau) & (idx <= jmax_ref[b]))
        s = jnp.where(sel, s, NEG)
    m_old = m_ref[...]
    m_new = jnp.maximum(m_old, jnp.max(s, axis=1, keepdims=True))
    alpha = jnp.exp(m_old - m_new)
    p = jnp.exp(s - m_new[:, 0:1])
    l_ref[...] = alpha * l_ref[...] + jnp.sum(p, axis=1, keepdims=True)
    acc_ref[...] = _lane_tile(alpha, 2) * acc_ref[...] + jnp.dot(p.astype(BF16), vcat, preferred_element_type=F32)
    m_ref[...] = m_new

    @pl.when(j == pl.num_programs(1) - 1)
    def _():
        o = acc_ref[...]
        rowi = lax.broadcasted_iota(I32, (DEC_ROWS, LANES), 0)
        upper = (rowi >= DEC_ROWS // 2) if not masked else ((rowi >= DEC_ROWS // 4) & (rowi < DEC_ROWS // 2))
        out_ref[...] = jnp.where(upper, o[:, LANES:], o[:, :LANES]) / l_ref[...]


def _pages_per_step(n_pages, g=PAGES_PER_STEP):
    while n_pages % g:
        g //= 2
    return g


def _paged_attention(qprime, k_new, v_new, bias_far, bias_last, bias0, k_cache, v_cache, page_table, page,
                     k_feature_major, mask_args=None):
    db = qprime.shape[0]
    n_pages = page_table.shape[1]
    width = qprime.shape[2]
    g = _pages_per_step(n_pages)
    n_steps = n_pages // g
    masked = mask_args is not None
    n_pref = 4 if masked else 1

    def page_map(gi):
        return lambda b, j, pt, *_: (pt[b * n_pages + j * g + gi], 0, 0)

    in_specs = [pl.BlockSpec((None, DEC_ROWS, width), lambda b, j, *_: (b, 0, 0)),
                pl.BlockSpec((None, 1, width), lambda b, j, *_: (b, 0, 0)),
                pl.BlockSpec((None, 1, width), lambda b, j, *_: (b, 0, 0)),
                pl.BlockSpec((DEC_ROWS, g * page), lambda b, j, *_: (0, 0)),
                pl.BlockSpec((DEC_ROWS, g * page), lambda b, j, *_: (0, 0)),
                pl.BlockSpec((DEC_ROWS, LANES), lambda b, j, *_: (0, 0))]
    args = [qprime, k_new, v_new, bias_far, bias_last, bias0]
    prefetch = [page_table.reshape(-1)]
    if masked:
        keys, tau, jmax, selnew = mask_args
        prefetch += [tau, jmax, selnew]
        in_specs += [pl.BlockSpec((None, 1, g * page), lambda b, j, *_: (b, 0, j))]
        args += [keys]
    in_specs += [pl.BlockSpec((None,) + k_cache.shape[1:], page_map(gi)) for gi in range(g)]
    in_specs += [pl.BlockSpec((None,) + v_cache.shape[1:], page_map(gi)) for gi in range(g)]
    args += [k_cache] * g + [v_cache] * g
    grid_spec = pltpu.PrefetchScalarGridSpec(
        num_scalar_prefetch=n_pref,
        grid=(db, n_steps),
        in_specs=in_specs,
        out_specs=pl.BlockSpec((None, DEC_ROWS, LANES), lambda b, j, *_: (b, 0, 0)),
        scratch_shapes=[pltpu.VMEM((DEC_ROWS, LANES), F32),
                        pltpu.VMEM((DEC_ROWS, LANES), F32),
                        pltpu.VMEM((DEC_ROWS, width), F32)])
    return pl.pallas_call(
        functools.partial(_paged_attn_kernel, n_pages_step=g, masked=masked, page=page,
                          k_feature_major=k_feature_major),
        grid_spec=grid_spec,
        out_shape=jax.ShapeDtypeStruct((db, DEC_ROWS, LANES), F32),
        compiler_params=_cparams(("parallel", "arbitrary")),
        name="paged_attn_masked" if masked else "paged_attn",
    )(*prefetch, *args)


def _paged_index_kernel(pt_ref, q_ref, w_ref, knew_ref, *refs, n_pages_step, page):
    k_refs = refs[:n_pages_step]
    out_ref = refs[n_pages_step]
    del pt_ref
    j = pl.program_id(1)
    last = pl.num_programs(1) - 1
    q = q_ref[...]
    w = w_ref[...]

    def score(kmat_t):
        d = jnp.dot(q, kmat_t.astype(BF16), preferred_element_type=F32)
        return jnp.sum(jnp.maximum(d, 0.0) * w[:, 0:1], axis=0, keepdims=True)

    @pl.when(j < last)
    def _():
        kcat = jnp.concatenate([r[...] for r in k_refs], axis=1)
        out_ref[...] = _sortable_key(score(kcat))

    @pl.when(j == last)
    def _():
        sc = score(knew_ref[...])
        lane = lax.broadcasted_iota(I32, (1, page), 1)
        sc = jnp.where(lane == 0, sc, -jnp.inf)
        pad = jnp.full((1, (n_pages_step - 1) * page), -jnp.inf, F32)
        full = jnp.concatenate([sc, pad], axis=1) if n_pages_step > 1 else sc
        out_ref[...] = _sortable_key(full)


def _paged_index_scores(qidx, wcol, k_new_tile, idx_cache_t, page_table):
    db = qidx.shape[0]
    n_pages = page_table.shape[1]
    page = idx_cache_t.shape[2]
    g = _pages_per_step(n_pages, INDEX_PAGES_PER_STEP)
    n_steps = n_pages // g

    def page_map(gi):
        return lambda b, j, pt: (pt[b * n_pages + jnp.minimum(j, n_steps - 1) * g + gi], 0, 0)

    grid_spec = pltpu.PrefetchScalarGridSpec(
        num_scalar_prefetch=1,
        grid=(db, n_steps + 1),
        in_specs=[pl.BlockSpec((None, IDX_HEADS, IDX_DIM), lambda b, j, pt: (b, 0, 0)),
                  pl.BlockSpec((None, IDX_HEADS, LANES), lambda b, j, pt: (b, 0, 0)),
                  pl.BlockSpec((None, IDX_DIM, page), lambda b, j, pt: (b, 0, 0))]
        + [pl.BlockSpec((None, IDX_DIM, page), page_map(gi)) for gi in range(g)],
        out_specs=pl.BlockSpec((None, 1, g * page), lambda b, j, pt: (b, 0, j)))
    return pl.pallas_call(
        functools.partial(_paged_index_kernel, n_pages_step=g, page=page),
        grid_spec=grid_spec,
        out_shape=jax.ShapeDtypeStruct((db, 1, (n_steps + 1) * g * page), I32),
        compiler_params=_cparams(("parallel", "arbitrary")),
        name="paged_index_scores",
    )(page_table.reshape(-1), qidx, wcol, k_new_tile, *([idx_cache_t] * g))


def _select_kernel(keys_ref, tau_ref, jmax_ref, *, k_sel, n_index_bits):
    keys = keys_ref[...]
    idx = lax.broadcasted_iota(I32, keys.shape, 1)
    shape = (keys.shape[0], 1)

    def count_fn(pred):
        return jnp.sum(jnp.where(pred(keys, idx), 1, 0), axis=1, keepdims=True)

    tau = _kth_largest(count_fn, k_sel, shape)
    need = k_sel - count_fn(lambda key, i: key > tau)
    jmax = _tie_cutoff(count_fn, tau, need, n_index_bits, shape)
    tau_ref[...] = jnp.broadcast_to(tau, tau_ref.shape)
    jmax_ref[...] = jnp.broadcast_to(jmax, jmax_ref.shape)


def _select_rows(keys2d, k_sel):
    rows, width = keys2d.shape
    n_index_bits = max(1, int(math.ceil(math.log2(width))))
    return pl.pallas_call(
        functools.partial(_select_kernel, k_sel=k_sel, n_index_bits=n_index_bits),
        out_shape=[jax.ShapeDtypeStruct((rows, LANES), I32), jax.ShapeDtypeStruct((rows, LANES), I32)],
        compiler_params=pltpu.CompilerParams(vmem_limit_bytes=VMEM_LIMIT),
        name="select_rows",
    )(keys2d)


def _even_tail_kernel(z_ref, conv_ref, s_ref, oa_ref, convw_ref, gp_ref, normg_ref, lamv_ref, subln_ref,
                      ya_ref, yb_ref, snew_ref, *, lam_init):
    z = z_ref[...]
    hw = B_HEADS * B_DK
    lv = lamv_ref[...]
    lam = (jnp.exp(jnp.sum(lv[0:1] * lv[1:2], axis=1, keepdims=True))
           - jnp.exp(jnp.sum(lv[2:3] * lv[3:4], axis=1, keepdims=True)) + lam_init)
    oa = oa_ref[...]
    for hg in range(A_HEADS):
        o = oa[2 * hg:2 * hg + 1] - lam * oa[2 * hg + 1:2 * hg + 2]
        y = o * lax.rsqrt(jnp.mean(o * o, axis=-1, keepdims=True) + EPS)
        y = (y * subln_ref[...]) * (1.0 - lam_init)
        gate = z[:, EVEN_OFF[3] + hg * A_DV:EVEN_OFF[3] + (hg + 1) * A_DV]
        ya_ref[:, hg * A_DV:(hg + 1) * A_DV] = (y * _silu(gate)).astype(BF16)
    x_new = z[:, EVEN_OFF[4]:EVEN_OFF[4] + B_QKV]
    conv = convw_ref[CONV_W - 1:CONV_W, :] * x_new
    cp = conv_ref[...]
    for i in range(CONV_W - 1):
        conv = conv + convw_ref[i:i + 1, :] * cp[i:i + 1, :]
    act = _silu(conv)
    ab = z[:, EVEN_OFF[6]:EVEN_OFF[6] + LANES]
    gp = gp_ref[...]
    row = lax.broadcasted_iota(I32, (B_DK, B_DV), 0)
    col = lax.broadcasted_iota(I32, (B_DK, B_DV), 1)
    eye = row == col
    for h in range(B_HEADS):
        q = act[:, h * B_DK:(h + 1) * B_DK]
        k = act[:, hw + h * B_DK:hw + (h + 1) * B_DK]
        v = act[:, 2 * hw + h * B_DV:2 * hw + (h + 1) * B_DV]
        q = q * lax.rsqrt(jnp.sum(q * q, axis=-1, keepdims=True) + EPS) * (B_DK ** -0.5)
        k = k * lax.rsqrt(jnp.sum(k * k, axis=-1, keepdims=True) + EPS)
        g = -jnp.exp(gp[0:1, h:h + 1]) * _softplus(ab[:, h:h + 1] + gp[1:2, h:h + 1])
        beta = jax.nn.sigmoid(ab[:, B_HEADS + h:B_HEADS + h + 1])
        eg = jnp.exp(g)
        s0 = s_ref[h]
        kcol = jnp.sum(jnp.where(eye, jnp.broadcast_to(k, (B_DK, B_DK)), 0.0), axis=1, keepdims=True)
        qcol = jnp.sum(jnp.where(eye, jnp.broadcast_to(q, (B_DK, B_DK)), 0.0), axis=1, keepdims=True)
        ks = jnp.sum(kcol * s0, axis=0, keepdims=True)
        qs = jnp.sum(qcol * s0, axis=0, keepdims=True)
        u = beta * (v - eg * ks)
        qk = jnp.sum(q * k, axis=1, keepdims=True)
        o = eg * qs + qk * u
        snew_ref[h] = eg * s0 + kcol * u
        y = o * lax.rsqrt(jnp.mean(o * o, axis=-1, keepdims=True) + EPS) * normg_ref[...]
        gate = z[:, EVEN_OFF[5] + h * B_DV:EVEN_OFF[5] + (h + 1) * B_DV]
        yb_ref[:, h * B_DV:(h + 1) * B_DV] = (y * _silu(gate)).astype(BF16)


def _even_tail(zf_s, conv_prev, s_prev, oa, conv_w, a_log, dt_bias, norm_g, lamv, subln, lam_init):
    db, npad = zf_s.shape
    gp = jnp.zeros((SUBLANES, LANES), F32).at[0, :B_HEADS].set(a_log).at[1, :B_HEADS].set(dt_bias)
    hw = B_HEADS * B_DV
    return pl.pallas_call(
        functools.partial(_even_tail_kernel, lam_init=lam_init),
        grid=(db,),
        in_specs=[pl.BlockSpec((None, 1, npad), lambda b: (b, 0, 0)),
                  pl.BlockSpec((None, CONV_W - 1, B_QKV), lambda b: (b, 0, 0)),
                  pl.BlockSpec((None, B_HEADS, B_DK, B_DV), lambda b: (b, 0, 0, 0)),
                  pl.BlockSpec((None, DEC_ROWS, LANES), lambda b: (b, 0, 0)),
                  pl.BlockSpec((CONV_W, B_QKV), lambda b: (0, 0)),
                  pl.BlockSpec((SUBLANES, LANES), lambda b: (0, 0)),
                  pl.BlockSpec((1, B_DV), lambda b: (0, 0)),
                  pl.BlockSpec((4, A_DH), lambda b: (0, 0)),
                  pl.BlockSpec((1, A_DV), lambda b: (0, 0))],
        out_specs=[pl.BlockSpec((None, 1, A_HEADS * A_DV), lambda b: (b, 0, 0)),
                   pl.BlockSpec((None, 1, hw), lambda b: (b, 0, 0)),
                   pl.BlockSpec((None, B_HEADS, B_DK, B_DV), lambda b: (b, 0, 0, 0))],
        out_shape=[jax.ShapeDtypeStruct((db, 1, A_HEADS * A_DV), BF16),
                   jax.ShapeDtypeStruct((db, 1, hw), BF16),
                   jax.ShapeDtypeStruct((db, B_HEADS, B_DK, B_DV), F32)],
        compiler_params=_cparams(("parallel",)),
        name="even_tail",
    )(zf_s.reshape(db, 1, npad), conv_prev, s_prev, oa, conv_w, gp, norm_g.reshape(1, B_DV), lamv,
      subln.reshape(1, A_DV))


def _odd_tail_kernel(z_ref, oc_ref, y_ref):
    z = z_ref[...]
    oc = oc_ref[...]
    for hg in range(C_HEADS):
        gate = z[:, ODD_OFF[3] + hg * C_DH:ODD_OFF[3] + (hg + 1) * C_DH]
        y_ref[:, hg * C_DH:(hg + 1) * C_DH] = (oc[hg:hg + 1] * _silu(gate)).astype(BF16)


def _odd_tail(zf_s, oc):
    db, npad = zf_s.shape
    return pl.pallas_call(
        _odd_tail_kernel,
        grid=(db,),
        in_specs=[pl.BlockSpec((None, 1, npad), lambda b: (b, 0, 0)),
                  pl.BlockSpec((None, DEC_ROWS, LANES), lambda b: (b, 0, 0))],
        out_specs=pl.BlockSpec((None, 1, C_HEADS * C_DH), lambda b: (b, 0, 0)),
        out_shape=jax.ShapeDtypeStruct((db, 1, C_HEADS * C_DH), BF16),
        compiler_params=_cparams(("parallel",)),
        name="odd_tail",
    )(zf_s.reshape(db, 1, npad), oc)


def _bias_by_distance(table):
    n = jnp.arange(FAR_DIST + 1)
    exact = N_BUCKETS // 2
    nf = jnp.maximum(n, 1).astype(F32)
    large = exact + (jnp.log(nf / exact) / math.log(MAX_DIST / exact) * (N_BUCKETS - exact)).astype(I32)
    bucket = jnp.where(n < exact, n, jnp.minimum(large, N_BUCKETS - 1))
    return table[bucket].astype(F32)


def _prompt_near_tiles(bd, group):
    assert FAR_DIST <= BLK
    heads = bd.shape[1]
    rel = ((bd - bd[FAR_DIST][None, :]) * LOG2E).T
    f = jnp.concatenate([jnp.full((heads, BLK - 1), NEG, F32), rel,
                         jnp.broadcast_to(rel[:, FAR_DIST:], (heads, 2 * BLK - 1 - FAR_DIST))], axis=1)
    period = 3 * BLK
    g = jnp.pad(f[:, ::-1], ((0, 0), (0, period - f.shape[1])))
    wrapped = jnp.tile(g, (1, BLK + 1))[:, :BLK * (period + 1)].reshape(heads, BLK, period + 1)
    strip = wrapped[:, ::-1, :2 * BLK]
    kinds = [None] * 4
    kinds[TILE_ZERO] = jnp.zeros((heads, BLK, BLK), F32)
    kinds[TILE_PREV] = strip[:, :, :BLK]
    kinds[TILE_DIAG] = strip[:, :, BLK:]
    kinds[TILE_MASKED] = jnp.full((heads, BLK, BLK), NEG, F32)
    tiles = jnp.stack(kinds, axis=1)
    n_kv = heads // group
    tiles = tiles.reshape(n_kv, group, 4, BLK, BLK)
    return jnp.transpose(tiles, (0, 2, 1, 3, 4)).reshape(n_kv, 4, group * BLK, BLK)


def _decode_bias(bd, row_heads, past, page, g):
    heads = jnp.asarray(row_heads, I32)
    far = jnp.broadcast_to(bd[FAR_DIST][heads][:, None], (len(row_heads), g * page))
    pos = past - g * page + jnp.arange(g * page)
    dist = jnp.minimum(past - pos, FAR_DIST)
    last = bd[dist][:, heads].T
    new = jnp.broadcast_to(bd[0][heads][:, None], (len(row_heads), LANES))
    return far.astype(F32), last.astype(F32), new.astype(F32)


def _pad_cols(w, mult):
    n = w.shape[1]
    return jnp.pad(w, ((0, 0), (0, _round_up(n, mult) - n)))


def kernel(x_prompt, x_sample, cache_a_k, cache_a_v, state_b_s, state_b_conv, cache_c_k, cache_c_v, cache_c_idx,
           page_table, meta, bias_table, final_norm, norm_e, w_in_e, w_out_e, lam_q1, lam_k1, lam_q2, lam_k2,
           subln_a, conv_b, a_log_b, dt_bias_b, norm_b, norm_o, w_in_o, w_out_o):
    nb, seq, d = x_prompt.shape
    n_meta = meta.shape[0]
    l = seq + n_meta
    lp = _round_up(l, CHUNK)
    nq_real = pl.cdiv(l, BLK)
    db = x_sample.shape[0]
    n_pages = page_table.shape[1]
    page = cache_a_k.shape[2]
    past = n_pages * page
    n_pool = cache_a_k.shape[1]
    assert x_sample.shape[1] == 1 and norm_e.shape[0] == 1 and norm_o.shape[0] == 1
    lam_init = 0.8 - 0.6 * math.exp(-0.3 * 0)

    tn_e, tn_o = 8 * LANES, 5 * LANES
    w_e = _pad_cols(w_in_e[0], tn_e).astype(BF16)
    w_o = _pad_cols(w_in_o[0], tn_o).astype(BF16)
    w_out_a = w_out_e[0][:A_HEADS * A_DV].astype(BF16)
    w_out_b = w_out_e[0][A_HEADS * A_DV:].astype(BF16)
    w_out_c = w_out_o[0].astype(BF16)
    tm = CHUNK
    tm_proj = 2 * CHUNK if (nb * lp) % (2 * CHUNK) == 0 else CHUNK
    dbp = _round_up(db, SUBLANES)

    bd = _bias_by_distance(bias_table)
    tiles_a = _prompt_near_tiles(bd, A_GROUP)
    tiles_c = tiles_a if (A_KV, A_GROUP) == (C_KV, C_GROUP) else _prompt_near_tiles(bd, C_GROUP)
    lamv = jnp.stack([lam_q1[0], lam_k1[0], lam_q2[0], lam_k2[0]]).astype(F32)

    hp = jnp.concatenate([jnp.broadcast_to(meta.astype(F32)[None], (nb, n_meta, d)), x_prompt], axis=1)
    hp = jnp.pad(hp, ((0, 0), (0, lp - l), (0, 0))).reshape(nb * lp, d)
    kv_tile = EVEN_OFF[1] // tn_e
    assert EVEN_OFF[3] <= (kv_tile + 1) * tn_e
    zf, zb = _norm_proj(hp, norm_e[0], w_e, tm_proj, tn_e, (kv_tile, kv_tile + 1))
    ya = _diff_attn_prompt(zf, zb, kv_tile * tn_e, tiles_a, lamv, subln_a[0], nb, lp, nq_real, lam_init)
    yb, pb_s = _gdn_prompt(zf, conv_b[0], a_log_b[0], dt_bias_b[0], norm_b[0], nb, lp, l)
    h1 = _out_proj([ya, yb], [w_out_a, w_out_b], hp, tm)
    z3 = zf.reshape(nb, lp, -1)
    pa_k = z3[:, :l, EVEN_OFF[1]:EVEN_OFF[2]].reshape(1, nb, l, A_KV, 2, A_DH)
    pa_v = z3[:, :l, EVEN_OFF[2]:EVEN_OFF[3]].reshape(1, nb, l, A_KV, A_DV)
    pb_conv = z3[:, l - (CONV_W - 1):l, EVEN_OFF[4]:EVEN_OFF[5]][None]

    zf1, zb1 = _norm_proj(h1, norm_o[0], w_o, tm_proj, tn_o, (0, w_o.shape[1] // tn_o))
    k_sel_p = min(TOPK_MAX, l // 4)
    yc = _sparse_attn_prompt(zf1, zb1, tiles_c, nb, lp, nq_real, k_sel_p)
    yp = _out_proj([yc], [w_out_c], h1, tm, final_gain=final_norm)
    y_prompt = yp.reshape(nb, lp, d)[:, n_meta:l]
    z13 = zf1.reshape(nb, lp, -1)
    pc_k = z13[:, :l, ODD_OFF[1]:ODD_OFF[2]].reshape(1, nb, l, C_KV, C_DH)
    pc_v = z13[:, :l, ODD_OFF[2]:ODD_OFF[3]].reshape(1, nb, l, C_KV, C_DH)
    pc_idx = z13[:, :l, ODD_OFF[5]:ODD_OFF[6]][None]

    hs = jnp.pad(x_sample.reshape(db, d), ((0, dbp - db), (0, 0)))
    zs, _ = _norm_proj(hs, norm_e[0], w_e, dbp, tn_e, (0, 1))
    zs = zs[:db]
    qa = zs[:, :EVEN_OFF[1]].reshape(db, A_KV, A_GROUP, 2, A_DH) * (A_DH ** -0.5)
    qprime = jnp.einsum('bhgcd,hi,cj->bhgcijd', qa, jnp.eye(A_KV, dtype=F32), jnp.eye(2, dtype=F32))
    qprime = qprime.reshape(db, DEC_ROWS, A_KV * 2 * A_DH).astype(BF16)
    k_new = zs[:, EVEN_OFF[1]:EVEN_OFF[2]].reshape(db, 1, -1)
    v_new = zs[:, EVEN_OFF[2]:EVEN_OFF[3]].reshape(db, 1, -1)
    g_dec = _pages_per_step(n_pages)
    rows_a = [r // 2 for r in range(DEC_ROWS)]
    bfar, blast, bnew = _decode_bias(bd, rows_a, past, page, g_dec)
    ak_t = jnp.transpose(cache_a_k[0], (0, 2, 3, 4, 1)).reshape(n_pool, A_KV * 2 * A_DH, page)
    av_r = cache_a_v[0].reshape(n_pool, page * A_KV, A_DV)
    oa = _paged_attention(qprime, k_new, v_new, bfar, blast, bnew, ak_t, av_r, page_table, page,
                          k_feature_major=True)
    ya_s, yb_s, sb_s = _even_tail(zs, state_b_conv[0], state_b_s[0], oa, conv_b[0], a_log_b[0], dt_bias_b[0],
                                  norm_b[0], lamv, subln_a[0], lam_init)
    hs_pad = lambda y: jnp.pad(y.reshape(db, -1), ((0, dbp - db), (0, 0)))
    hs1 = _out_proj([hs_pad(ya_s), hs_pad(yb_s)], [w_out_a, w_out_b], hs, dbp)
    sa_k = zs[:, EVEN_OFF[1]:EVEN_OFF[2]].reshape(1, db, 1, A_KV, 2, A_DH)
    sa_v = zs[:, EVEN_OFF[2]:EVEN_OFF[3]].reshape(1, db, 1, A_KV, A_DV)
    sb_conv = jnp.concatenate([state_b_conv[0][:, 1:], zs[:, None, EVEN_OFF[4]:EVEN_OFF[5]]], axis=1)[None]

    zs1, _ = _norm_proj(hs1, norm_o[0], w_o, dbp, tn_o, (0, 1))
    zs1 = zs1[:db]
    qidx = (zs1[:, ODD_OFF[4]:ODD_OFF[5]].reshape(db, IDX_HEADS, IDX_DIM) * (IDX_DIM ** -0.5)).astype(BF16)
    wcol = jnp.broadcast_to((zs1[:, ODD_OFF[6]:ODD_OFF[7]] * (IDX_HEADS ** -0.5))[:, :, None],
                            (db, IDX_HEADS, LANES))
    ki_new = zs1[:, ODD_OFF[5]:ODD_OFF[6]]
    ki_tile = jnp.zeros((db, IDX_DIM, page), F32).at[:, :, 0].set(ki_new)
    ci_t = jnp.transpose(cache_c_idx[0], (0, 2, 1))
    keys = _paged_index_scores(qidx, wcol, ki_tile, ci_t, page_table)
    k_sel_s = min(TOPK_MAX, (past + 1) // 4)
    tau, jmax = _select_rows(keys.reshape(db, -1), k_sel_s)
    tau, jmax = tau[:, 0], jmax[:, 0]
    key_new = keys[:, 0, past]
    selnew = ((key_new > tau) | ((key_new == tau) & (past <= jmax))).astype(I32)
    qc = zs1[:, :ODD_OFF[1]].reshape(db, C_KV, C_GROUP, C_DH) * (C_DH ** -0.5)
    qcp = jnp.einsum('bhgd,hi->bhgid', qc, jnp.eye(C_KV, dtype=F32)).reshape(db, C_HEADS, C_KV * C_DH)
    qcp = jnp.pad(qcp, ((0, 0), (0, DEC_ROWS - C_HEADS), (0, 0))).astype(BF16)
    kc_new = zs1[:, ODD_OFF[1]:ODD_OFF[2]].reshape(db, 1, -1)
    vc_new = zs1[:, ODD_OFF[2]:ODD_OFF[3]].reshape(db, 1, -1)
    rows_c = [r if r < C_HEADS else 0 for r in range(DEC_ROWS)]
    cfar, clast, cnew = _decode_bias(bd, rows_c, past, page, g_dec)
    ck_r = cache_c_k[0].reshape(n_pool, page * C_KV, C_DH)
    cv_r = cache_c_v[0].reshape(n_pool, page * C_KV, C_DH)
    oc = _paged_attention(qcp, kc_new, vc_new, cfar, clast, cnew, ck_r, cv_r, page_table, page,
                          k_feature_major=False, mask_args=(keys, tau, jmax, selnew))
    yc_s = _odd_tail(zs1, oc)
    ys = _out_proj([hs_pad(yc_s)], [w_out_c], hs1, dbp, final_gain=final_norm)
    y_sample = ys[:db].reshape(db, 1, d)
    sc_k = zs1[:, ODD_OFF[1]:ODD_OFF[2]].reshape(1, db, 1, C_KV, C_DH)
    sc_v = zs1[:, ODD_OFF[2]:ODD_OFF[3]].reshape(1, db, 1, C_KV, C_DH)
    sc_idx = zs1[:, None, ODD_OFF[5]:ODD_OFF[6]][None]

    return (y_prompt, y_sample, pa_k, pa_v, pb_s[None], pb_conv, pc_k, pc_v, pc_idx,
            sa_k, sa_v, sb_s[None], sb_conv, sc_k, sc_v, sc_idx)
```

```python
import functools
import math

import jax
import jax.numpy as jnp
import numpy as np
from jax import lax
from jax.experimental import pallas as pl
from jax.experimental.pallas import tpu as pltpu

F32 = jnp.float32
BF16 = jnp.bfloat16
I32 = jnp.int32

EPS = 1e-6
N_BUCKETS = 32
MAX_DIST = 128
FAR_DIST = MAX_DIST

A_HEADS, A_KV, A_GROUP, A_DH, A_DV = 8, 2, 4, 64, 128
B_HEADS, B_DK, B_DV, CONV_W = 4, 128, 128, 4
C_HEADS, C_KV, C_GROUP, C_DH = 8, 2, 4, 128
IDX_HEADS, IDX_DIM, TOPK_MAX = 8, 64, 256
B_QKV = 2 * B_HEADS * B_DK + B_HEADS * B_DV

EVEN_COLS = (A_HEADS * 2 * A_DH, A_KV * 2 * A_DH, A_KV * A_DV, A_HEADS * A_DV, B_QKV, B_HEADS * B_DV, B_HEADS, B_HEADS)
ODD_COLS = (C_HEADS * C_DH, C_KV * C_DH, C_KV * C_DH, C_HEADS * C_DH, IDX_HEADS * IDX_DIM, IDX_DIM, IDX_HEADS)
EVEN_OFF = tuple(int(v) for v in np.cumsum((0,) + EVEN_COLS))
ODD_OFF = tuple(int(v) for v in np.cumsum((0,) + ODD_COLS))

LANES = 128
SUBLANES = 8
BLK = 128
CHUNK = 512
SUBS = CHUNK // BLK
NEG = -1e30
INT_MIN = -2 ** 31
INT_MAX = 2 ** 31 - 1
LOG2E = math.log2(math.e)
VMEM_LIMIT = 56 * 1024 * 1024
HI = lax.Precision.HIGHEST

NT = (((1,), (1,)), ((), ()))


def _round_up(x, m):
    return (x + m - 1) // m * m


def _cparams(sem):
    return pltpu.CompilerParams(dimension_semantics=sem, vmem_limit_bytes=VMEM_LIMIT)


def _silu(x):
    return x * jax.nn.sigmoid(x)


def _sortable_key(score):
    score = jnp.where(score == 0.0, 0.0, score)
    bits = pltpu.bitcast(score, I32)
    return bits ^ ((bits >> 31) & jnp.int32(0x7FFFFFFF))


def _lane_tile(x, n):
    return jnp.concatenate([x] * n, axis=1)


def _proj_kernel(x_ref, g_ref, w_ref, of_ref, ob_ref, xn_ref, *, bf16_tiles):
    j = pl.program_id(1)

    @pl.when(j == 0)
    def _():
        xf = x_ref[...]
        y = xf * lax.rsqrt(jnp.mean(xf * xf, axis=-1, keepdims=True) + EPS)
        xn_ref[...] = (y * g_ref[...]).astype(BF16)

    acc = jnp.dot(xn_ref[...], w_ref[...], preferred_element_type=F32)
    of_ref[...] = acc

    @pl.when((j >= bf16_tiles[0]) & (j < bf16_tiles[1]))
    def _():
        ob_ref[...] = acc.astype(BF16)


def _norm_proj(x, gain, w_bf16, tm, tn, bf16_tiles):
    m, d = x.shape
    n = w_bf16.shape[1]
    t0, t1 = bf16_tiles
    return pl.pallas_call(
        functools.partial(_proj_kernel, bf16_tiles=bf16_tiles),
        grid=(m // tm, n // tn),
        in_specs=[pl.BlockSpec((tm, d), lambda i, j: (i, 0)),
                  pl.BlockSpec((1, d), lambda i, j: (0, 0)),
                  pl.BlockSpec((d, tn), lambda i, j: (0, j))],
        out_specs=[pl.BlockSpec((tm, tn), lambda i, j: (i, j)),
                   pl.BlockSpec((tm, tn), lambda i, j: (i, jnp.clip(j - t0, 0, t1 - t0 - 1)))],
        out_shape=[jax.ShapeDtypeStruct((m, n), F32), jax.ShapeDtypeStruct((m, (t1 - t0) * tn), BF16)],
        scratch_shapes=[pltpu.VMEM((tm, d), BF16)],
        compiler_params=_cparams(("parallel", "arbitrary")),
        name="norm_proj",
    )(x, gain.reshape(1, d), w_bf16)


def _outproj_kernel(*refs, n_lhs, final_norm):
    ys = refs[:n_lhs]
    ws = refs[n_lhs:2 * n_lhs]
    h_ref = refs[2 * n_lhs]
    pos = 2 * n_lhs + 1
    fn_ref = refs[pos] if final_norm else None
    out_ref = refs[-1]
    acc = h_ref[...]
    for y_ref, w_ref in zip(ys, ws):
        acc = acc + jnp.dot(y_ref[...], w_ref[...], preferred_element_type=F32)
    if final_norm:
        y = acc * lax.rsqrt(jnp.mean(acc * acc, axis=-1, keepdims=True) + EPS)
        acc = y * fn_ref[...]
    out_ref[...] = acc


def _out_proj(ys, ws, h, tm, final_gain=None):
    m, n = h.shape
    n_lhs = len(ys)
    in_specs = [pl.BlockSpec((tm, y.shape[1]), lambda i: (i, 0)) for y in ys]
    in_specs += [pl.BlockSpec(w.shape, lambda i: (0, 0)) for w in ws]
    in_specs += [pl.BlockSpec((tm, n), lambda i: (i, 0))]
    args = list(ys) + list(ws) + [h]
    if final_gain is not None:
        in_specs += [pl.BlockSpec((1, n), lambda i: (0, 0))]
        args += [final_gain.reshape(1, n)]
    return pl.pallas_call(
        functools.partial(_outproj_kernel, n_lhs=n_lhs, final_norm=final_gain is not None),
        grid=(m // tm,),
        in_specs=in_specs,
        out_specs=pl.BlockSpec((tm, n), lambda i: (i, 0)),
        out_shape=jax.ShapeDtypeStruct((m, n), F32),
        compiler_params=_cparams(("parallel",)),
        name="out_proj",
    )(*args)


def _flash_steps(ss, v1s, m_refs):
    each = range(len(ss))
    m_old = [m_refs[i][0][m_refs[i][2]] for i in each]
    m_new = [jnp.maximum(m_old[i], jnp.max(ss[i], axis=1, keepdims=True)) for i in each]
    alpha = [jnp.exp2(m_old[i] - m_new[i]) for i in each]
    p = [jnp.exp2(ss[i] - _lane_tile(m_new[i], ss[i].shape[1] // LANES)).astype(BF16) for i in each]
    pv = [jnp.dot(p[i], v1s[i], preferred_element_type=F32) for i in each]
    for i in each:
        m_ref, acc_ref, idx = m_refs[i]
        acc_ref[idx] = _lane_tile(alpha[i], 2) * acc_ref[idx] + pv[i]
        m_ref[idx] = m_new[i]


def _far_chunk_loop(n_far, run_chunks):
    def quad(i, carry):
        run_chunks([4 * i, 4 * i + 1, 4 * i + 2, 4 * i + 3])
        return carry

    n_quad = lax.shift_right_logical(n_far, 2)
    lax.fori_loop(0, n_quad, quad, 0)

    @pl.when(jnp.bitwise_and(n_far, 2) == 2)
    def _():
        run_chunks([4 * n_quad, 4 * n_quad + 1])

    @pl.when(jnp.bitwise_and(n_far, 1) == 1)
    def _():
        run_chunks([n_far - 1])


TILE_ZERO, TILE_PREV, TILE_DIAG, TILE_MASKED = range(4)


def _near_bias_kind(sb, r):
    rel = sb - SUBS - r
    return jnp.where(rel == 0, TILE_DIAG, jnp.where(rel == -1, TILE_PREV, jnp.where(rel < -1, TILE_ZERO, TILE_MASKED)))


def _diff_attn_kernel(q_ref, k_ref, v_ref, gate_ref, tiles_ref, lamv_ref, subln_ref, out_ref,
                      qm_ref, m_ref, acc_ref, *, lam_init, nq_real):
    qi = pl.program_id(2)

    @pl.when(qi >= nq_real)
    def _():
        out_ref[...] = jnp.zeros(out_ref.shape, out_ref.dtype)

    @pl.when(qi < nq_real)
    def _():
        lane = lax.broadcasted_iota(I32, (BLK, 2 * A_DH), 1)
        for g in range(A_GROUP):
            qg = q_ref[:, g * 2 * A_DH:(g + 1) * 2 * A_DH] * (A_DH ** -0.5 * LOG2E)
            for c in range(2):
                keep = (lane < A_DH) if c == 0 else (lane >= A_DH)
                qm_ref[c, g * BLK:(g + 1) * BLK, :] = jnp.where(keep, qg, 0.0).astype(BF16)
        m_ref[...] = jnp.full(m_ref.shape, NEG, F32)
        acc_ref[...] = jnp.zeros(acc_ref.shape, F32)
        r = jnp.bitwise_and(qi, SUBS - 1)
        jd = lax.shift_right_logical(qi, int(math.log2(SUBS)))
        ones = jnp.ones((CHUNK, A_DV), BF16)

        def run_chunks(js, first_sbs=None):
            starts = [pl.multiple_of(j * CHUNK, CHUNK) for j in js]
            k = jnp.concatenate([k_ref[pl.ds(st, CHUNK), :] for st in starts], axis=0)
            v1 = jnp.concatenate(
                [jnp.concatenate([v_ref[pl.ds(st, CHUNK), :], ones], axis=1) for st in starts], axis=0)
            ss = [lax.dot_general(qm_ref[c], k, NT, preferred_element_type=F32) for c in range(2)]
            if first_sbs is not None:
                bias = jnp.concatenate(
                    [tiles_ref[_near_bias_kind(sb + i, r)] for sb in first_sbs for i in range(SUBS)], axis=1)
                ss = [s + bias for s in ss]
            _flash_steps(ss, [v1, v1], [(m_ref, acc_ref, c) for c in range(2)])

        def near_pair(rr):
            n_diag = (rr + 1) * BLK
            st_prev = pl.multiple_of((jd - 1) * CHUNK, CHUNK)
            st_diag = pl.multiple_of(jd * CHUNK, CHUNK)
            k = jnp.concatenate([k_ref[pl.ds(st_prev, CHUNK), :], k_ref[pl.ds(st_diag, n_diag), :]], axis=0)
            v = jnp.concatenate([v_ref[pl.ds(st_prev, CHUNK), :], v_ref[pl.ds(st_diag, n_diag), :]], axis=0)
            v1 = jnp.concatenate([v, jnp.ones((CHUNK + n_diag, A_DV), BF16)], axis=1)
            bias = jnp.concatenate([tiles_ref[TILE_PREV], tiles_ref[TILE_DIAG]], axis=1)
            first = CHUNK + (rr - 1) * BLK
            ss = []
            for c in range(2):
                s = lax.dot_general(qm_ref[c], k, NT, preferred_element_type=F32)
                ss.append(jnp.concatenate([s[:, :first], s[:, first:] + bias], axis=1))
            _flash_steps(ss, [v1, v1], [(m_ref, acc_ref, c) for c in range(2)])

        _far_chunk_loop(jnp.maximum(jd - 1, 0), run_chunks)

        for rr in range(SUBS):
            @pl.when((jd >= 1) & (r == rr))
            def _(rr=rr):
                near_pair(rr)

        @pl.when(jd == 0)
        def _():
            run_chunks([jd], [SUBS])

        lv = lamv_ref[...]
        lam = (jnp.exp(jnp.sum(lv[0:1] * lv[1:2], axis=1, keepdims=True))
               - jnp.exp(jnp.sum(lv[2:3] * lv[3:4], axis=1, keepdims=True)) + lam_init)
        a0 = acc_ref[0]
        a1 = acc_ref[1]
        o = a0[:, :A_DV] / a0[:, A_DV:] - lam * (a1[:, :A_DV] / a1[:, A_DV:])
        y = o * lax.rsqrt(jnp.mean(o * o, axis=-1, keepdims=True) + EPS)
        y = (y * subln_ref[...]) * (1.0 - lam_init)
        for g in range(A_GROUP):
            gate = gate_ref[:, g * A_DV:(g + 1) * A_DV]
            out_ref[:, g * A_DV:(g + 1) * A_DV] = (y[g * BLK:(g + 1) * BLK] * _silu(gate)).astype(BF16)


def _diff_attn_prompt(zf, zb, zb_col0, near_tiles, lamv, subln, nb, lp, nq_real, lam_init):
    nq = lp // BLK
    qw = A_GROUP * 2 * A_DH
    k_blk0 = (EVEN_OFF[1] - zb_col0) // (2 * A_DH)
    v_blk0 = (EVEN_OFF[2] - zb_col0) // A_DV
    g_blk0 = EVEN_OFF[3] // (A_GROUP * A_DV)
    rows = A_GROUP * BLK
    return pl.pallas_call(
        functools.partial(_diff_attn_kernel, lam_init=lam_init, nq_real=nq_real),
        grid=(nb, A_KV, nq),
        in_specs=[pl.BlockSpec((BLK, qw), lambda b, h, i: (b * nq + i, h)),
                  pl.BlockSpec((lp, 2 * A_DH), lambda b, h, i: (b, k_blk0 + h)),
                  pl.BlockSpec((lp, A_DV), lambda b, h, i: (b, v_blk0 + h)),
                  pl.BlockSpec((BLK, A_GROUP * A_DV), lambda b, h, i: (b * nq + i, g_blk0 + h)),
                  pl.BlockSpec((None, 4, rows, BLK), lambda b, h, i: (h, 0, 0, 0)),
                  pl.BlockSpec((4, A_DH), lambda b, h, i: (0, 0)),
                  pl.BlockSpec((1, A_DV), lambda b, h, i: (0, 0))],
        out_specs=pl.BlockSpec((BLK, A_GROUP * A_DV), lambda b, h, i: (b * nq + i, h)),
        out_shape=jax.ShapeDtypeStruct((nb * lp, A_HEADS * A_DV), BF16),
        scratch_shapes=[pltpu.VMEM((2, rows, 2 * A_DH), BF16),
                        pltpu.VMEM((2, rows, LANES), F32),
                        pltpu.VMEM((2, rows, 2 * A_DV), F32)],
        compiler_params=_cparams(("parallel", "parallel", "arbitrary")),
        name="diff_attn_prompt",
    )(zf, zb, zb, zf, near_tiles, lamv, subln.reshape(1, A_DV))


def _softplus(x):
    return jnp.maximum(x, 0.0) + jnp.log1p(jnp.exp(-jnp.abs(x)))


def _split_bf16(a):
    hi = a.astype(BF16)
    return hi, (a - hi.astype(F32)).astype(BF16)


def _dot3(a, b):
    (ah, al), (bh, bl) = a, b
    return (jnp.dot(ah, bh, preferred_element_type=F32)
            + (jnp.dot(ah, bl, preferred_element_type=F32) + jnp.dot(al, bh, preferred_element_type=F32)))


def _gdn_chunk_math(s0, q, k, v, gb, betab, row_ge, row_gt):
    n = len(q)
    c = q[0].shape[0]
    each = range(n)
    dot = functools.partial(jnp.dot, preferred_element_type=F32)
    dot_nt = functools.partial(lax.dot_general, dimension_numbers=NT, preferred_element_type=F32)
    ltri = jnp.where(row_ge, 1.0, 0.0).astype(F32)
    gcum = [jnp.dot(ltri, gb[i], precision=HI, preferred_element_type=F32) for i in each]
    decay = [jnp.where(row_ge, jnp.exp(jnp.where(row_ge, gcum[i] - gcum[i].T, 0.0)), 0.0) for i in each]
    kb = [k[i].astype(BF16) for i in each]
    qb = [q[i].astype(BF16) for i in each]
    s0b = [s0[i].astype(BF16) for i in each]
    kk = [dot_nt(kb[i], kb[i]) for i in each]
    x = [-jnp.where(row_gt, betab[i] * decay[i] * kk[i], 0.0) for i in each]
    tm = list(x)
    ps = [_split_bf16(x[i]) for i in each]
    for _ in range(int(math.log2(c)) - 1):
        p = [_dot3(ps[i], ps[i]) for i in each]
        ps = [_split_bf16(p[i]) for i in each]
        tms = [_split_bf16(tm[i]) for i in each]
        tm = [tm[i] + p[i] + _dot3(tms[i], ps[i]) for i in each]
    eg = [jnp.exp(gcum[i]) for i in each]
    ks = [dot(kb[i], s0b[i]) for i in each]
    rhs = [betab[i] * (v[i] - eg[i] * ks[i]) for i in each]
    u = [rhs[i] + dot(tm[i].astype(BF16), rhs[i].astype(BF16)) for i in each]
    ub = [u[i].astype(BF16) for i in each]
    qk = [dot_nt(qb[i], kb[i]) * decay[i] for i in each]
    o = [eg[i] * dot(qb[i], s0b[i]) + dot(qk[i].astype(BF16), ub[i]) for i in each]
    glast = [gcum[i][c - 1:c, :] for i in each]
    kd = [k[i] * jnp.exp(glast[i] - gcum[i]) for i in each]
    s_new = [jnp.exp(glast[i]) * s0[i] + dot(kd[i].T.astype(BF16), ub[i]) for i in each]
    return s_new, o


def _gdn_prompt_kernel(xq_ref, xk_ref, xv_ref, gate_ref, ab_ref, convw_ref, gp_ref, normg_ref,
                       y_ref, sfin_ref, xbuf_ref, s_ref, *, seq_len):
    ci = pl.program_id(0)
    hw = B_HEADS * B_DK
    nb = xq_ref.shape[0]

    @pl.when(ci == 0)
    def _():
        xbuf_ref[:, 0:SUBLANES, :] = jnp.zeros((nb, SUBLANES, B_QKV), F32)
        s_ref[...] = jnp.zeros(s_ref.shape, F32)

    row = lax.broadcasted_iota(I32, (BLK, BLK), 0)
    col = lax.broadcasted_iota(I32, (BLK, BLK), 1)
    row_ge = row >= col
    row_gt = row > col
    valid = (ci * BLK + row) < seq_len
    gp = gp_ref[...]
    chains = [(b, h) for b in range(nb) for h in range(B_HEADS)]
    qs, ks, vs, gbs, betabs = [], [], [], [], []
    for b in range(nb):
        xbuf_ref[b, SUBLANES:SUBLANES + BLK, 0:hw] = xq_ref[b]
        xbuf_ref[b, SUBLANES:SUBLANES + BLK, hw:2 * hw] = xk_ref[b]
        xbuf_ref[b, SUBLANES:SUBLANES + BLK, 2 * hw:3 * hw] = xv_ref[b]
        conv = jnp.zeros((BLK, B_QKV), F32)
        for i in range(CONV_W):
            conv = conv + convw_ref[i:i + 1, :] * xbuf_ref[b, pl.ds(SUBLANES - (CONV_W - 1) + i, BLK), :]
        tail = xbuf_ref[b, BLK:BLK + SUBLANES, :]
        xbuf_ref[b, 0:SUBLANES, :] = tail
        act = _silu(conv)
        ab = ab_ref[b]
        for h in range(B_HEADS):
            q = act[:, h * B_DK:(h + 1) * B_DK]
            k = act[:, hw + h * B_DK:hw + (h + 1) * B_DK]
            v = act[:, 2 * hw + h * B_DV:2 * hw + (h + 1) * B_DV]
            q = q * lax.rsqrt(jnp.sum(q * q, axis=-1, keepdims=True) + EPS) * (B_DK ** -0.5)
            k = k * lax.rsqrt(jnp.sum(k * k, axis=-1, keepdims=True) + EPS)
            a_raw = jnp.broadcast_to(ab[:, h:h + 1], (BLK, BLK))
            b_raw = jnp.broadcast_to(ab[:, B_HEADS + h:B_HEADS + h + 1], (BLK, BLK))
            a_log = gp[0:1, h:h + 1]
            dt_b = gp[1:2, h:h + 1]
            qs.append(q)
            ks.append(k)
            vs.append(v)
            gbs.append(jnp.where(valid, -jnp.exp(a_log) * _softplus(a_raw + dt_b), 0.0))
            betabs.append(jnp.where(valid, jax.nn.sigmoid(b_raw), 0.0))
    s_new, outs = _gdn_chunk_math([s_ref[b, h] for b, h in chains], qs, ks, vs, gbs, betabs, row_ge, row_gt)
    for (b, h), s_bh, o in zip(chains, s_new, outs):
        s_ref[b, h] = s_bh
        y = o * lax.rsqrt(jnp.mean(o * o, axis=-1, keepdims=True) + EPS) * normg_ref[...]
        gate = gate_ref[b, :, h * B_DV:(h + 1) * B_DV]
        y_ref[b, :, h * B_DV:(h + 1) * B_DV] = (y * _silu(gate)).astype(BF16)

    @pl.when(ci == pl.num_programs(0) - 1)
    def _():
        sfin_ref[...] = s_ref[...]


def _gdn_prompt(zf, conv_w, a_log, dt_bias, norm_g, nb, lp, seq_len):
    nc = lp // BLK
    hw = B_HEADS * B_DK
    c0 = EVEN_OFF[4] // hw
    g0 = EVEN_OFF[5] // hw
    ab0 = EVEN_OFF[6] // LANES
    gp = jnp.zeros((SUBLANES, LANES), F32).at[0, :B_HEADS].set(a_log).at[1, :B_HEADS].set(dt_bias)
    z3 = zf.reshape(nb, lp, zf.shape[1])
    y, s_fin = pl.pallas_call(
        functools.partial(_gdn_prompt_kernel, seq_len=seq_len),
        grid=(nc,),
        in_specs=[pl.BlockSpec((nb, BLK, hw), lambda i: (0, i, c0)),
                  pl.BlockSpec((nb, BLK, hw), lambda i: (0, i, c0 + 1)),
                  pl.BlockSpec((nb, BLK, hw), lambda i: (0, i, c0 + 2)),
                  pl.BlockSpec((nb, BLK, hw), lambda i: (0, i, g0)),
                  pl.BlockSpec((nb, BLK, LANES), lambda i: (0, i, ab0)),
                  pl.BlockSpec((CONV_W, B_QKV), lambda i: (0, 0)),
                  pl.BlockSpec((SUBLANES, LANES), lambda i: (0, 0)),
                  pl.BlockSpec((1, B_DV), lambda i: (0, 0))],
        out_specs=[pl.BlockSpec((nb, BLK, hw), lambda i: (0, i, 0)),
                   pl.BlockSpec((nb, B_HEADS, B_DK, B_DV), lambda i: (0, 0, 0, 0))],
        out_shape=[jax.ShapeDtypeStruct((nb, lp, hw), BF16),
                   jax.ShapeDtypeStruct((nb, B_HEADS, B_DK, B_DV), F32)],
        scratch_shapes=[pltpu.VMEM((nb, BLK + SUBLANES, B_QKV), F32),
                        pltpu.VMEM((nb, B_HEADS, B_DK, B_DV), F32)],
        compiler_params=_cparams(("arbitrary",)),
        name="gdn_prompt",
    )(z3, z3, z3, z3, z3, conv_w, gp, norm_g.reshape(1, B_DV))
    return y.reshape(nb * lp, hw), s_fin


def _kth_largest(count_fn, k_sel, shape):
    def step(i, tau):
        cand = tau + lax.shift_left(jnp.int32(1), 31 - i)
        cnt = count_fn(lambda key, idx: key >= cand)
        return jnp.where(cnt >= k_sel, cand, tau)

    return lax.fori_loop(0, 32, step, jnp.full(shape, INT_MIN, I32))


BITS_PER_CHECK = 4


def _kth_separator(count_fn, k_sel, n_keys):
    assert 32 % BITS_PER_CHECK == 0

    def pending(cnt_tau):
        return jnp.max(jnp.where(cnt_tau > k_sel, 1.0, 0.0)) > 0.5

    def group(carry):
        g, tau, cnt_tau, _ = carry
        for b in range(BITS_PER_CHECK):
            cand = tau + lax.shift_left(jnp.int32(1), 31 - (g * BITS_PER_CHECK + b))
            cnt = count_fn(lambda key, idx: key >= cand)
            take = cnt >= k_sel
            tau = jnp.where(take, cand, tau)
            cnt_tau = jnp.where(take, cnt, cnt_tau)
        return g + 1, tau, cnt_tau, pending(cnt_tau)

    init = (jnp.int32(0), jnp.full(n_keys.shape, INT_MIN, I32), n_keys, pending(n_keys))
    out = lax.while_loop(lambda c: (c[0] < 32 // BITS_PER_CHECK) & c[3], group, init)
    return out[1], out[2]


def _tie_cutoff(count_fn, tau, need, n_index_bits, shape):
    def step(i, jm):
        cand = jm | lax.shift_left(jnp.int32(1), n_index_bits - 1 - i)
        cnt = count_fn(lambda key, idx: (key == tau) & (idx < cand))
        return jnp.where(cnt < need, cand, jm)

    return lax.fori_loop(0, n_index_bits, step, jnp.zeros(shape, I32))


def _sparse_attn_kernel(qi_ref, kw_ref, kidx_ref, qc_ref, kc_ref, vc_ref, g0_ref, g1_ref, tiles_ref, out_ref,
                        keys_ref, qm_ref, qh_ref, tau_ref, jmax_ref, m_ref, acc_ref,
                        *, k_sel, n_index_bits, nq_real):
    qb = pl.program_id(1)

    @pl.when(qb >= nq_real)
    def _():
        out_ref[...] = jnp.zeros(out_ref.shape, out_ref.dtype)

    @pl.when(qb < nq_real)
    def _():
        lane = lax.broadcasted_iota(I32, (BLK, LANES), 1)
        key_pos = lax.broadcasted_iota(I32, (CHUNK, BLK), 0)
        qry_pos = lax.broadcasted_iota(I32, (CHUNK, BLK), 1)
        r = jnp.bitwise_and(qb, SUBS - 1)
        jd = lax.shift_right_logical(qb, int(math.log2(SUBS)))

        for p in range(IDX_HEADS // 2):
            pair = qi_ref[:, p * LANES:(p + 1) * LANES] * (IDX_DIM ** -0.5)
            swapped = pltpu.roll(pair, IDX_DIM, 1)
            for e, src in enumerate((pair, swapped)):
                h = 2 * p + e
                qm_ref[h * BLK:(h + 1) * BLK, :] = jnp.where(lane < IDX_DIM, src, 0.0).astype(BF16)
        w_rows = kw_ref[...].T * (IDX_HEADS ** -0.5)

        def scores_t(j):
            start = pl.multiple_of(j * CHUNK, CHUNK)
            d = lax.dot_general(kidx_ref[pl.ds(start, CHUNK), :], qm_ref[...], NT, preferred_element_type=F32)
            sc = jnp.zeros((CHUNK, BLK), F32)
            for h in range(IDX_HEADS):
                sc = sc + w_rows[IDX_DIM + h:IDX_DIM + h + 1, :] * jnp.maximum(d[:, h * BLK:(h + 1) * BLK], 0.0)
            return sc

        def score_quad(i, carry):
            for j in (4 * i, 4 * i + 1, 4 * i + 2, 4 * i + 3):
                keys_ref[j] = _sortable_key(scores_t(j))
            return carry

        n_quad = lax.shift_right_logical(jd, 2)
        lax.fori_loop(0, n_quad, score_quad, 0)

        @pl.when(jnp.bitwise_and(jd, 2) == 2)
        def _():
            for j in (4 * n_quad, 4 * n_quad + 1):
                keys_ref[j] = _sortable_key(scores_t(j))

        @pl.when(jnp.bitwise_and(jd, 1) == 1)
        def _():
            keys_ref[jd - 1] = _sortable_key(scores_t(jd - 1))

        admissible = (jd * CHUNK + key_pos) <= (qb * BLK + qry_pos)
        keys_ref[jd] = _sortable_key(jnp.where(admissible, scores_t(jd), -jnp.inf))

        def count_fn(pred):
            part = SUBLANES * SUBLANES

            def body(j, acc):
                hit = jnp.where(pred(keys_ref[j], j * CHUNK + key_pos), 1.0, 0.0)
                return acc + jnp.sum(hit.reshape(CHUNK // part, part, BLK), axis=0)
            acc = lax.fori_loop(0, jd + 1, body, jnp.zeros((part, BLK), F32))
            return jnp.sum(acc, axis=0, keepdims=True).astype(I32)

        n_keys = qb * BLK + lax.broadcasted_iota(I32, (1, BLK), 1) + 1
        tau, n_ge = _kth_separator(count_fn, k_sel, n_keys)
        tau_ref[...] = tau
        jmax_ref[...] = jnp.full((1, BLK), INT_MAX, I32)
        excess = jnp.max(jnp.where(n_ge > k_sel, 1.0, 0.0)) > 0.5

        @pl.when(excess)
        def _():
            need = k_sel - count_fn(lambda key, idx: key > tau)
            jmax_ref[...] = _tie_cutoff(count_fn, tau, need, n_index_bits, (1, BLK))

        jmax = jmax_ref[...]

        for h in range(C_KV):
            for g in range(C_GROUP):
                col = (h * C_GROUP + g) * C_DH
                qh_ref[h, g * BLK:(g + 1) * BLK, :] = (
                    qc_ref[:, col:col + C_DH] * (C_DH ** -0.5 * LOG2E)).astype(BF16)
        m_ref[...] = jnp.full(m_ref.shape, NEG, F32)
        acc_ref[...] = jnp.zeros(acc_ref.shape, F32)
        ones = jnp.ones((CHUNK, C_DH), BF16)

        def run_chunks(js, first_sbs=None):
            starts = [pl.multiple_of(j * CHUNK, CHUNK) for j in js]
            masks = []
            for j in js:
                key = keys_ref[j]
                tie_ok = jnp.where((j * CHUNK + key_pos) <= jmax, 0.0, NEG)
                sel_t = jnp.where(key > tau, 0.0, jnp.where(key == tau, tie_ok, NEG))
                masks += [sel_t[i * BLK:(i + 1) * BLK, :].T for i in range(SUBS)]
            selneg = jnp.concatenate(masks, axis=1)
            selneg = jnp.concatenate([selneg] * C_GROUP, axis=0)
            ss, v1s = [], []
            for h in range(C_KV):
                cols = slice(h * C_DH, (h + 1) * C_DH)
                k = jnp.concatenate([kc_ref[pl.ds(st, CHUNK), cols] for st in starts], axis=0)
                v1s.append(jnp.concatenate(
                    [jnp.concatenate([vc_ref[pl.ds(st, CHUNK), cols], ones], axis=1) for st in starts], axis=0))
                s = lax.dot_general(qh_ref[h], k, NT, preferred_element_type=F32) + selneg
                if first_sbs is not None:
                    s = s + jnp.concatenate(
                        [tiles_ref[h, _near_bias_kind(sb + i, r)] for sb in first_sbs for i in range(SUBS)], axis=1)
                ss.append(s)
            _flash_steps(ss, v1s, [(m_ref, acc_ref, h) for h in range(C_KV)])

        def mask_blocks(j, n_blocks):
            key = keys_ref[j][:n_blocks * BLK]
            tie_ok = jnp.where((j * CHUNK + key_pos[:n_blocks * BLK]) <= jmax, 0.0, NEG)
            sel_t = jnp.where(key > tau, 0.0, jnp.where(key == tau, tie_ok, NEG))
            return [sel_t[i * BLK:(i + 1) * BLK, :].T for i in range(n_blocks)]

        def near_pair(rr):
            n_diag = (rr + 1) * BLK
            st_prev = pl.multiple_of((jd - 1) * CHUNK, CHUNK)
            st_diag = pl.multiple_of(jd * CHUNK, CHUNK)
            selneg = jnp.concatenate(mask_blocks(jd - 1, SUBS) + mask_blocks(jd, rr + 1), axis=1)
            selneg = jnp.concatenate([selneg] * C_GROUP, axis=0)
            first = CHUNK + (rr - 1) * BLK
            ones_n = jnp.ones((CHUNK + n_diag, C_DH), BF16)
            ss, v1s = [], []
            for h in range(C_KV):
                cols = slice(h * C_DH, (h + 1) * C_DH)
                k = jnp.concatenate([kc_ref[pl.ds(st_prev, CHUNK), cols], kc_ref[pl.ds(st_diag, n_diag), cols]], axis=0)
                v = jnp.concatenate([vc_ref[pl.ds(st_prev, CHUNK), cols], vc_ref[pl.ds(st_diag, n_diag), cols]], axis=0)
                v1s.append(jnp.concatenate([v, ones_n], axis=1))
                s = lax.dot_general(qh_ref[h], k, NT, preferred_element_type=F32) + selneg
                bias = jnp.concatenate([tiles_ref[h, TILE_PREV], tiles_ref[h, TILE_DIAG]], axis=1)
                ss.append(jnp.concatenate([s[:, :first], s[:, first:] + bias], axis=1))
            _flash_steps(ss, v1s, [(m_ref, acc_ref, h) for h in range(C_KV)])

        _far_chunk_loop(jnp.maximum(jd - 1, 0), run_chunks)

        for rr in range(SUBS):
            @pl.when((jd >= 1) & (r == rr))
            def _(rr=rr):
                near_pair(rr)

        @pl.when(jd == 0)
        def _():
            run_chunks([jd], [SUBS])

        half = (C_HEADS // 2) * C_DH
        for h in range(C_KV):
            a = acc_ref[h]
            o = a[:, :C_DH] / a[:, C_DH:]
            gref = g0_ref if h == 0 else g1_ref
            for g in range(C_GROUP):
                gate = gref[:, g * C_DH:(g + 1) * C_DH]
                out_ref[:, h * half + g * C_DH:h * half + (g + 1) * C_DH] = (
                    o[g * BLK:(g + 1) * BLK] * _silu(gate)).astype(BF16)


def _sparse_attn_prompt(zf, zb, near_tiles, nb, lp, nq_real, k_sel):
    nq = lp // BLK
    half = (C_HEADS // 2) * C_DH
    qi0 = ODD_OFF[4] // (IDX_HEADS * IDX_DIM)
    ki0 = ODD_OFF[5] // LANES
    kc0 = ODD_OFF[1] // (C_KV * C_DH)
    vc0 = ODD_OFF[2] // (C_KV * C_DH)
    g0 = ODD_OFF[3] // half
    rows = C_GROUP * BLK
    n_index_bits = max(1, int(math.ceil(math.log2(lp))))
    return pl.pallas_call(
        functools.partial(_sparse_attn_kernel, k_sel=k_sel, n_index_bits=n_index_bits, nq_real=nq_real),
        grid=(nb, nq),
        in_specs=[pl.BlockSpec((BLK, IDX_HEADS * IDX_DIM), lambda b, i: (b * nq + i, qi0)),
                  pl.BlockSpec((BLK, LANES), lambda b, i: (b * nq + i, ki0)),
                  pl.BlockSpec((lp, LANES), lambda b, i: (b, ki0)),
                  pl.BlockSpec((BLK, C_HEADS * C_DH), lambda b, i: (b * nq + i, 0)),
                  pl.BlockSpec((lp, C_KV * C_DH), lambda b, i: (b, kc0)),
                  pl.BlockSpec((lp, C_KV * C_DH), lambda b, i: (b, vc0)),
                  pl.BlockSpec((BLK, half), lambda b, i: (b * nq + i, g0)),
                  pl.BlockSpec((BLK, half), lambda b, i: (b * nq + i, g0 + 1)),
                  pl.BlockSpec((C_KV, 4, rows, BLK), lambda b, i: (0, 0, 0, 0))],
        out_specs=pl.BlockSpec((BLK, C_HEADS * C_DH), lambda b, i: (b * nq + i, 0)),
        out_shape=jax.ShapeDtypeStruct((nb * lp, C_HEADS * C_DH), BF16),
        scratch_shapes=[pltpu.VMEM((lp // CHUNK, CHUNK, BLK), I32),
                        pltpu.VMEM((IDX_HEADS * BLK, LANES), BF16),
                        pltpu.VMEM((C_KV, rows, C_DH), BF16),
                        pltpu.VMEM((1, BLK), I32),
                        pltpu.VMEM((1, BLK), I32),
                        pltpu.VMEM((C_KV, rows, LANES), F32),
                        pltpu.VMEM((C_KV, rows, 2 * C_DH), F32)],
        compiler_params=_cparams(("parallel", "arbitrary")),
        name="sparse_attn_prompt",
    )(zf, zf, zb, zf, zb, zb, zf, zf, near_tiles)


PAGES_PER_STEP = 32
INDEX_PAGES_PER_STEP = 64
DEC_ROWS = 16


def _interleaved_pages(refs, page):
    halves = [jnp.concatenate([r[pl.ds(h, page, stride=2), :] for r in refs], axis=0) for h in range(2)]
    return jnp.concatenate(halves, axis=1).astype(BF16)


def _paged_attn_kernel(*refs, n_pages_step, masked, page, k_feature_major):
    if masked:
        pt_ref, tau_ref, jmax_ref, selnew_ref = refs[:4]
        refs = refs[4:]
    else:
        pt_ref = refs[0]
        refs = refs[1:]
    q_ref, knew_ref, vnew_ref, bfar_ref, blast_ref, b0_ref = refs[:6]
    refs = refs[6:]
    if masked:
        keys_ref = refs[0]
        refs = refs[1:]
    k_refs = refs[:n_pages_step]
    v_refs = refs[n_pages_step:2 * n_pages_step]
    out_ref, m_ref, l_ref, acc_ref = refs[2 * n_pages_step:]
    del pt_ref
    b = pl.program_id(0)
    j = pl.program_id(1)
    width = n_pages_step * page
    q = q_ref[...]

    @pl.when(j == 0)
    def _():
        s_new = jnp.sum(q.astype(F32) * knew_ref[...].astype(BF16).astype(F32), axis=1, keepdims=True) + b0_ref[...]
        v_new = jnp.broadcast_to(vnew_ref[...].astype(BF16).astype(F32), acc_ref.shape)
        if masked:
            take = selnew_ref[b] > 0
            m_ref[...] = jnp.where(take, s_new, NEG)
            l_ref[...] = jnp.where(take, 1.0, 0.0) * jnp.ones(l_ref.shape, F32)
            acc_ref[...] = jnp.where(take, v_new, 0.0)
        else:
            m_ref[...] = s_new
            l_ref[...] = jnp.ones(l_ref.shape, F32)
            acc_ref[...] = v_new

    if k_feature_major:
        kcat = jnp.concatenate([r[...] for r in k_refs], axis=1).astype(BF16)
        s = jnp.dot(q, kcat, preferred_element_type=F32)
    else:
        s = lax.dot_general(q, _interleaved_pages(k_refs, page), NT, preferred_element_type=F32)
    vcat = _interleaved_pages(v_refs, page)
    s = s + jnp.where(j == pl.num_programs(1) - 1, blast_ref[...], bfar_ref[...])
    if masked:
        key = keys_ref[...]
        idx = j * width + lax.broadcasted_iota(I32, (1, width), 1)
        tau = tau_ref[b]
        sel = (key > tau) | ((key == tau) & (idx <= jmax_ref[b]))
        s = jnp.where(sel, s, NEG)
    m_old = m_ref[...]
    m_new = jnp.maximum(m_old, jnp.max(s, axis=1, keepdims=True))
    alpha = jnp.exp(m_old - m_new)
    p = jnp.exp(s - m_new[:, 0:1])
    l_ref[...] = alpha * l_ref[...] + jnp.sum(p, axis=1, keepdims=True)
    acc_ref[...] = _lane_tile(alpha, 2) * acc_ref[...] + jnp.dot(p.astype(BF16), vcat, preferred_element_type=F32)
    m_ref[...] = m_new

    @pl.when(j == pl.num_programs(1) - 1)
    def _():
        o = acc_ref[...]
        rowi = lax.broadcasted_iota(I32, (DEC_ROWS, LANES), 0)
        upper = (rowi >= DEC_ROWS // 2) if not masked else ((rowi >= DEC_ROWS // 4) & (rowi < DEC_ROWS // 2))
        out_ref[...] = jnp.where(upper, o[:, LANES:], o[:, :LANES]) / l_ref[...]


def _pages_per_step(n_pages, g=PAGES_PER_STEP):
    while n_pages % g:
        g //= 2
    return g


def _paged_attention(qprime, k_new, v_new, bias_far, bias_last, bias0, k_cache, v_cache, page_table, page,
                     k_feature_major, mask_args=None):
    db = qprime.shape[0]
    n_pages = page_table.shape[1]
    width = qprime.shape[2]
    g = _pages_per_step(n_pages)
    n_steps = n_pages // g
    masked = mask_args is not None
    n_pref = 4 if masked else 1

    def page_map(gi):
        return lambda b, j, pt, *_: (pt[b * n_pages + j * g + gi], 0, 0)

    in_specs = [pl.BlockSpec((None, DEC_ROWS, width), lambda b, j, *_: (b, 0, 0)),
                pl.BlockSpec((None, 1, width), lambda b, j, *_: (b, 0, 0)),
                pl.BlockSpec((None, 1, width), lambda b, j, *_: (b, 0, 0)),
                pl.BlockSpec((DEC_ROWS, g * page), lambda b, j, *_: (0, 0)),
                pl.BlockSpec((DEC_ROWS, g * page), lambda b, j, *_: (0, 0)),
                pl.BlockSpec((DEC_ROWS, LANES), lambda b, j, *_: (0, 0))]
    args = [qprime, k_new, v_new, bias_far, bias_last, bias0]
    prefetch = [page_table.reshape(-1)]
    if masked:
        keys, tau, jmax, selnew = mask_args
        prefetch += [tau, jmax, selnew]
        in_specs += [pl.BlockSpec((None, 1, g * page), lambda b, j, *_: (b, 0, j))]
        args += [keys]
    in_specs += [pl.BlockSpec((None,) + k_cache.shape[1:], page_map(gi)) for gi in range(g)]
    in_specs += [pl.BlockSpec((None,) + v_cache.shape[1:], page_map(gi)) for gi in range(g)]
    args += [k_cache] * g + [v_cache] * g
    grid_spec = pltpu.PrefetchScalarGridSpec(
        num_scalar_prefetch=n_pref,
        grid=(db, n_steps),
        in_specs=in_specs,
        out_specs=pl.BlockSpec((None, DEC_ROWS, LANES), lambda b, j, *_: (b, 0, 0)),
        scratch_shapes=[pltpu.VMEM((DEC_ROWS, LANES), F32),
                        pltpu.VMEM((DEC_ROWS, LANES), F32),
                        pltpu.VMEM((DEC_ROWS, width), F32)])
    return pl.pallas_call(
        functools.partial(_paged_attn_kernel, n_pages_step=g, masked=masked, page=page,
                          k_feature_major=k_feature_major),
        grid_spec=grid_spec,
        out_shape=jax.ShapeDtypeStruct((db, DEC_ROWS, LANES), F32),
        compiler_params=_cparams(("parallel", "arbitrary")),
        name="paged_attn_masked" if masked else "paged_attn",
    )(*prefetch, *args)


def _paged_index_kernel(pt_ref, q_ref, w_ref, knew_ref, *refs, n_pages_step, page):
    k_refs = refs[:n_pages_step]
    out_ref = refs[n_pages_step]
    del pt_ref
    j = pl.program_id(1)
    last = pl.num_programs(1) - 1
    q = q_ref[...]
    w = w_ref[...]

    def score(kmat_t):
        d = jnp.dot(q, kmat_t.astype(BF16), preferred_element_type=F32)
        return jnp.sum(jnp.maximum(d, 0.0) * w[:, 0:1], axis=0, keepdims=True)

    @pl.when(j < last)
    def _():
        kcat = jnp.concatenate([r[...] for r in k_refs], axis=1)
        out_ref[...] = _sortable_key(score(kcat))

    @pl.when(j == last)
    def _():
        sc = score(knew_ref[...])
        lane = lax.broadcasted_iota(I32, (1, page), 1)
        sc = jnp.where(lane == 0, sc, -jnp.inf)
        pad = jnp.full((1, (n_pages_step - 1) * page), -jnp.inf, F32)
        full = jnp.concatenate([sc, pad], axis=1) if n_pages_step > 1 else sc
        out_ref[...] = _sortable_key(full)


def _paged_index_scores(qidx, wcol, k_new_tile, idx_cache_t, page_table):
    db = qidx.shape[0]
    n_pages = page_table.shape[1]
    page = idx_cache_t.shape[2]
    g = _pages_per_step(n_pages, INDEX_PAGES_PER_STEP)
    n_steps = n_pages // g

    def page_map(gi):
        return lambda b, j, pt: (pt[b * n_pages + jnp.minimum(j, n_steps - 1) * g + gi], 0, 0)

    grid_spec = pltpu.PrefetchScalarGridSpec(
        num_scalar_prefetch=1,
        grid=(db, n_steps + 1),
        in_specs=[pl.BlockSpec((None, IDX_HEADS, IDX_DIM), lambda b, j, pt: (b, 0, 0)),
                  pl.BlockSpec((None, IDX_HEADS, LANES), lambda b, j, pt: (b, 0, 0)),
                  pl.BlockSpec((None, IDX_DIM, page), lambda b, j, pt: (b, 0, 0))]
        + [pl.BlockSpec((None, IDX_DIM, page), page_map(gi)) for gi in range(g)],
        out_specs=pl.BlockSpec((None, 1, g * page), lambda b, j, pt: (b, 0, j)))
    return pl.pallas_call(
        functools.partial(_paged_index_kernel, n_pages_step=g, page=page),
        grid_spec=grid_spec,
        out_shape=jax.ShapeDtypeStruct((db, 1, (n_steps + 1) * g * page), I32),
        compiler_params=_cparams(("parallel", "arbitrary")),
        name="paged_index_scores",
    )(page_table.reshape(-1), qidx, wcol, k_new_tile, *([idx_cache_t] * g))


def _select_kernel(keys_ref, tau_ref, jmax_ref, *, k_sel, n_index_bits):
    keys = keys_ref[...]
    idx = lax.broadcasted_iota(I32, keys.shape, 1)
    shape = (keys.shape[0], 1)

    def count_fn(pred):
        return jnp.sum(jnp.where(pred(keys, idx), 1, 0), axis=1, keepdims=True)

    tau = _kth_largest(count_fn, k_sel, shape)
    need = k_sel - count_fn(lambda key, i: key > tau)
    jmax = _tie_cutoff(count_fn, tau, need, n_index_bits, shape)
    tau_ref[...] = jnp.broadcast_to(tau, tau_ref.shape)
    jmax_ref[...] = jnp.broadcast_to(jmax, jmax_ref.shape)


def _select_rows(keys2d, k_sel):
    rows, width = keys2d.shape
    n_index_bits = max(1, int(math.ceil(math.log2(width))))
    return pl.pallas_call(
        functools.partial(_select_kernel, k_sel=k_sel, n_index_bits=n_index_bits),
        out_shape=[jax.ShapeDtypeStruct((rows, LANES), I32), jax.ShapeDtypeStruct((rows, LANES), I32)],
        compiler_params=pltpu.CompilerParams(vmem_limit_bytes=VMEM_LIMIT),
        name="select_rows",
    )(keys2d)


def _even_tail_kernel(z_ref, conv_ref, s_ref, oa_ref, convw_ref, gp_ref, normg_ref, lamv_ref, subln_ref,
                      ya_ref, yb_ref, snew_ref, *, lam_init):
    z = z_ref[...]
    hw = B_HEADS * B_DK
    lv = lamv_ref[...]
    lam = (jnp.exp(jnp.sum(lv[0:1] * lv[1:2], axis=1, keepdims=True))
           - jnp.exp(jnp.sum(lv[2:3] * lv[3:4], axis=1, keepdims=True)) + lam_init)
    oa = oa_ref[...]
    for hg in range(A_HEADS):
        o = oa[2 * hg:2 * hg + 1] - lam * oa[2 * hg + 1:2 * hg + 2]
        y = o * lax.rsqrt(jnp.mean(o * o, axis=-1, keepdims=True) + EPS)
        y = (y * subln_ref[...]) * (1.0 - lam_init)
        gate = z[:, EVEN_OFF[3] + hg * A_DV:EVEN_OFF[3] + (hg + 1) * A_DV]
        ya_ref[:, hg * A_DV:(hg + 1) * A_DV] = (y * _silu(gate)).astype(BF16)
    x_new = z[:, EVEN_OFF[4]:EVEN_OFF[4] + B_QKV]
    conv = convw_ref[CONV_W - 1:CONV_W, :] * x_new
    cp = conv_ref[...]
    for i in range(CONV_W - 1):
        conv = conv + convw_ref[i:i + 1, :] * cp[i:i + 1, :]
    act = _silu(conv)
    ab = z[:, EVEN_OFF[6]:EVEN_OFF[6] + LANES]
    gp = gp_ref[...]
    row = lax.broadcasted_iota(I32, (B_DK, B_DV), 0)
    col = lax.broadcasted_iota(I32, (B_DK, B_DV), 1)
    eye = row == col
    for h in range(B_HEADS):
        q = act[:, h * B_DK:(h + 1) * B_DK]
        k = act[:, hw + h * B_DK:hw + (h + 1) * B_DK]
        v = act[:, 2 * hw + h * B_DV:2 * hw + (h + 1) * B_DV]
        q = q * lax.rsqrt(jnp.sum(q * q, axis=-1, keepdims=True) + EPS) * (B_DK ** -0.5)
        k = k * lax.rsqrt(jnp.sum(k * k, axis=-1, keepdims=True) + EPS)
        g = -jnp.exp(gp[0:1, h:h + 1]) * _softplus(ab[:, h:h + 1] + gp[1:2, h:h + 1])
        beta = jax.nn.sigmoid(ab[:, B_HEADS + h:B_HEADS + h + 1])
        eg = jnp.exp(g)
        s0 = s_ref[h]
        kcol = jnp.sum(jnp.where(eye, jnp.broadcast_to(k, (B_DK, B_DK)), 0.0), axis=1, keepdims=True)
        qcol = jnp.sum(jnp.where(eye, jnp.broadcast_to(q, (B_DK, B_DK)), 0.0), axis=1, keepdims=True)
        ks = jnp.sum(kcol * s0, axis=0, keepdims=True)
        qs = jnp.sum(qcol * s0, axis=0, keepdims=True)
        u = beta * (v - eg * ks)
        qk = jnp.sum(q * k, axis=1, keepdims=True)
        o = eg * qs + qk * u
        snew_ref[h] = eg * s0 + kcol * u
        y = o * lax.rsqrt(jnp.mean(o * o, axis=-1, keepdims=True) + EPS) * normg_ref[...]
        gate = z[:, EVEN_OFF[5] + h * B_DV:EVEN_OFF[5] + (h + 1) * B_DV]
        yb_ref[:, h * B_DV:(h + 1) * B_DV] = (y * _silu(gate)).astype(BF16)


def _even_tail(zf_s, conv_prev, s_prev, oa, conv_w, a_log, dt_bias, norm_g, lamv, subln, lam_init):
    db, npad = zf_s.shape
    gp = jnp.zeros((SUBLANES, LANES), F32).at[0, :B_HEADS].set(a_log).at[1, :B_HEADS].set(dt_bias)
    hw = B_HEADS * B_DV
    return pl.pallas_call(
        functools.partial(_even_tail_kernel, lam_init=lam_init),
        grid=(db,),
        in_specs=[pl.BlockSpec((None, 1, npad), lambda b: (b, 0, 0)),
                  pl.BlockSpec((None, CONV_W - 1, B_QKV), lambda b: (b, 0, 0)),
                  pl.BlockSpec((None, B_HEADS, B_DK, B_DV), lambda b: (b, 0, 0, 0)),
                  pl.BlockSpec((None, DEC_ROWS, LANES), lambda b: (b, 0, 0)),
                  pl.BlockSpec((CONV_W, B_QKV), lambda b: (0, 0)),
                  pl.BlockSpec((SUBLANES, LANES), lambda b: (0, 0)),
                  pl.BlockSpec((1, B_DV), lambda b: (0, 0)),
                  pl.BlockSpec((4, A_DH), lambda b: (0, 0)),
                  pl.BlockSpec((1, A_DV), lambda b: (0, 0))],
        out_specs=[pl.BlockSpec((None, 1, A_HEADS * A_DV), lambda b: (b, 0, 0)),
                   pl.BlockSpec((None, 1, hw), lambda b: (b, 0, 0)),
                   pl.BlockSpec((None, B_HEADS, B_DK, B_DV), lambda b: (b, 0, 0, 0))],
        out_shape=[jax.ShapeDtypeStruct((db, 1, A_HEADS * A_DV), BF16),
                   jax.ShapeDtypeStruct((db, 1, hw), BF16),
                   jax.ShapeDtypeStruct((db, B_HEADS, B_DK, B_DV), F32)],
        compiler_params=_cparams(("parallel",)),
        name="even_tail",
    )(zf_s.reshape(db, 1, npad), conv_prev, s_prev, oa, conv_w, gp, norm_g.reshape(1, B_DV), lamv,
      subln.reshape(1, A_DV))


def _odd_tail_kernel(z_ref, oc_ref, y_ref):
    z = z_ref[...]
    oc = oc_ref[...]
    for hg in range(C_HEADS):
        gate = z[:, ODD_OFF[3] + hg * C_DH:ODD_OFF[3] + (hg + 1) * C_DH]
        y_ref[:, hg * C_DH:(hg + 1) * C_DH] = (oc[hg:hg + 1] * _silu(gate)).astype(BF16)


def _odd_tail(zf_s, oc):
    db, npad = zf_s.shape
    return pl.pallas_call(
        _odd_tail_kernel,
        grid=(db,),
        in_specs=[pl.BlockSpec((None, 1, npad), lambda b: (b, 0, 0)),
                  pl.BlockSpec((None, DEC_ROWS, LANES), lambda b: (b, 0, 0))],
        out_specs=pl.BlockSpec((None, 1, C_HEADS * C_DH), lambda b: (b, 0, 0)),
        out_shape=jax.ShapeDtypeStruct((db, 1, C_HEADS * C_DH), BF16),
        compiler_params=_cparams(("parallel",)),
        name="odd_tail",
    )(zf_s.reshape(db, 1, npad), oc)


def _bias_by_distance(table):
    n = jnp.arange(FAR_DIST + 1)
    exact = N_BUCKETS // 2
    nf = jnp.maximum(n, 1).astype(F32)
    large = exact + (jnp.log(nf / exact) / math.log(MAX_DIST / exact) * (N_BUCKETS - exact)).astype(I32)
    bucket = jnp.where(n < exact, n, jnp.minimum(large, N_BUCKETS - 1))
    return table[bucket].astype(F32)


def _prompt_near_tiles(bd, group):
    assert FAR_DIST <= BLK
    heads = bd.shape[1]
    rel = ((bd - bd[FAR_DIST][None, :]) * LOG2E).T
    f = jnp.concatenate([jnp.full((heads, BLK - 1), NEG, F32), rel,
                         jnp.broadcast_to(rel[:, FAR_DIST:], (heads, 2 * BLK - 1 - FAR_DIST))], axis=1)
    period = 3 * BLK
    g = jnp.pad(f[:, ::-1], ((0, 0), (0, period - f.shape[1])))
    wrapped = jnp.tile(g, (1, BLK + 1))[:, :BLK * (period + 1)].reshape(heads, BLK, period + 1)
    strip = wrapped[:, ::-1, :2 * BLK]
    kinds = [None] * 4
    kinds[TILE_ZERO] = jnp.zeros((heads, BLK, BLK), F32)
    kinds[TILE_PREV] = strip[:, :, :BLK]
    kinds[TILE_DIAG] = strip[:, :, BLK:]
    kinds[TILE_MASKED] = jnp.full((heads, BLK, BLK), NEG, F32)
    tiles = jnp.stack(kinds, axis=1)
    n_kv = heads // group
    tiles = tiles.reshape(n_kv, group, 4, BLK, BLK)
    return jnp.transpose(tiles, (0, 2, 1, 3, 4)).reshape(n_kv, 4, group * BLK, BLK)


def _decode_bias(bd, row_heads, past, page, g):
    heads = jnp.asarray(row_heads, I32)
    far = jnp.broadcast_to(bd[FAR_DIST][heads][:, None], (len(row_heads), g * page))
    pos = past - g * page + jnp.arange(g * page)
    dist = jnp.minimum(past - pos, FAR_DIST)
    last = bd[dist][:, heads].T
    new = jnp.broadcast_to(bd[0][heads][:, None], (len(row_heads), LANES))
    return far.astype(F32), last.astype(F32), new.astype(F32)


def _pad_cols(w, mult):
    n = w.shape[1]
    return jnp.pad(w, ((0, 0), (0, _round_up(n, mult) - n)))


def kernel(x_prompt, x_sample, cache_a_k, cache_a_v, state_b_s, state_b_conv, cache_c_k, cache_c_v, cache_c_idx,
           page_table, meta, bias_table, final_norm, norm_e, w_in_e, w_out_e, lam_q1, lam_k1, lam_q2, lam_k2,
           subln_a, conv_b, a_log_b, dt_bias_b, norm_b, norm_o, w_in_o, w_out_o):
    nb, seq, d = x_prompt.shape
    n_meta = meta.shape[0]
    l = seq + n_meta
    lp = _round_up(l, CHUNK)
    nq_real = pl.cdiv(l, BLK)
    db = x_sample.shape[0]
    n_pages = page_table.shape[1]
    page = cache_a_k.shape[2]
    past = n_pages * page
    n_pool = cache_a_k.shape[1]
    assert x_sample.shape[1] == 1 and norm_e.shape[0] == 1 and norm_o.shape[0] == 1
    lam_init = 0.8 - 0.6 * math.exp(-0.3 * 0)

    tn_e, tn_o = 8 * LANES, 5 * LANES
    w_e = _pad_cols(w_in_e[0], tn_e).astype(BF16)
    w_o = _pad_cols(w_in_o[0], tn_o).astype(BF16)
    w_out_a = w_out_e[0][:A_HEADS * A_DV].astype(BF16)
    w_out_b = w_out_e[0][A_HEADS * A_DV:].astype(BF16)
    w_out_c = w_out_o[0].astype(BF16)
    tm = CHUNK
    tm_proj = 2 * CHUNK if (nb * lp) % (2 * CHUNK) == 0 else CHUNK
    dbp = _round_up(db, SUBLANES)

    bd = _bias_by_distance(bias_table)
    tiles_a = _prompt_near_tiles(bd, A_GROUP)
    tiles_c = tiles_a if (A_KV, A_GROUP) == (C_KV, C_GROUP) else _prompt_near_tiles(bd, C_GROUP)
    lamv = jnp.stack([lam_q1[0], lam_k1[0], lam_q2[0], lam_k2[0]]).astype(F32)

    hp = jnp.concatenate([jnp.broadcast_to(meta.astype(F32)[None], (nb, n_meta, d)), x_prompt], axis=1)
    hp = jnp.pad(hp, ((0, 0), (0, lp - l), (0, 0))).reshape(nb * lp, d)
    kv_tile = EVEN_OFF[1] // tn_e
    assert EVEN_OFF[3] <= (kv_tile + 1) * tn_e
    zf, zb = _norm_proj(hp, norm_e[0], w_e, tm_proj, tn_e, (kv_tile, kv_tile + 1))
    ya = _diff_attn_prompt(zf, zb, kv_tile * tn_e, tiles_a, lamv, subln_a[0], nb, lp, nq_real, lam_init)
    yb, pb_s = _gdn_prompt(zf, conv_b[0], a_log_b[0], dt_bias_b[0], norm_b[0], nb, lp, l)
    h1 = _out_proj([ya, yb], [w_out_a, w_out_b], hp, tm)
    z3 = zf.reshape(nb, lp, -1)
    pa_k = z3[:, :l, EVEN_OFF[1]:EVEN_OFF[2]].reshape(1, nb, l, A_KV, 2, A_DH)
    pa_v = z3[:, :l, EVEN_OFF[2]:EVEN_OFF[3]].reshape(1, nb, l, A_KV, A_DV)
    pb_conv = z3[:, l - (CONV_W - 1):l, EVEN_OFF[4]:EVEN_OFF[5]][None]

    zf1, zb1 = _norm_proj(h1, norm_o[0], w_o, tm_proj, tn_o, (0, w_o.shape[1] // tn_o))
    k_sel_p = min(TOPK_MAX, l // 4)
    yc = _sparse_attn_prompt(zf1, zb1, tiles_c, nb, lp, nq_real, k_sel_p)
    yp = _out_proj([yc], [w_out_c], h1, tm, final_gain=final_norm)
    y_prompt = yp.reshape(nb, lp, d)[:, n_meta:l]
    z13 = zf1.reshape(nb, lp, -1)
    pc_k = z13[:, :l, ODD_OFF[1]:ODD_OFF[2]].reshape(1, nb, l, C_KV, C_DH)
    pc_v = z13[:, :l, ODD_OFF[2]:ODD_OFF[3]].reshape(1, nb, l, C_KV, C_DH)
    pc_idx = z13[:, :l, ODD_OFF[5]:ODD_OFF[6]][None]

    hs = jnp.pad(x_sample.reshape(db, d), ((0, dbp - db), (0, 0)))
    zs, _ = _norm_proj(hs, norm_e[0], w_e, dbp, tn_e, (0, 1))
    zs = zs[:db]
    qa = zs[:, :EVEN_OFF[1]].reshape(db, A_KV, A_GROUP, 2, A_DH) * (A_DH ** -0.5)
    qprime = jnp.einsum('bhgcd,hi,cj->bhgcijd', qa, jnp.eye(A_KV, dtype=F32), jnp.eye(2, dtype=F32))
    qprime = qprime.reshape(db, DEC_ROWS, A_KV * 2 * A_DH).astype(BF16)
    k_new = zs[:, EVEN_OFF[1]:EVEN_OFF[2]].reshape(db, 1, -1)
    v_new = zs[:, EVEN_OFF[2]:EVEN_OFF[3]].reshape(db, 1, -1)
    g_dec = _pages_per_step(n_pages)
    rows_a = [r // 2 for r in range(DEC_ROWS)]
    bfar, blast, bnew = _decode_bias(bd, rows_a, past, page, g_dec)
    ak_t = jnp.transpose(cache_a_k[0], (0, 2, 3, 4, 1)).reshape(n_pool, A_KV * 2 * A_DH, page)
    av_r = cache_a_v[0].reshape(n_pool, page * A_KV, A_DV)
    oa = _paged_attention(qprime, k_new, v_new, bfar, blast, bnew, ak_t, av_r, page_table, page,
                          k_feature_major=True)
    ya_s, yb_s, sb_s = _even_tail(zs, state_b_conv[0], state_b_s[0], oa, conv_b[0], a_log_b[0], dt_bias_b[0],
                                  norm_b[0], lamv, subln_a[0], lam_init)
    hs_pad = lambda y: jnp.pad(y.reshape(db, -1), ((0, dbp - db), (0, 0)))
    hs1 = _out_proj([hs_pad(ya_s), hs_pad(yb_s)], [w_out_a, w_out_b], hs, dbp)
    sa_k = zs[:, EVEN_OFF[1]:EVEN_OFF[2]].reshape(1, db, 1, A_KV, 2, A_DH)
    sa_v = zs[:, EVEN_OFF[2]:EVEN_OFF[3]].reshape(1, db, 1, A_KV, A_DV)
    sb_conv = jnp.concatenate([state_b_conv[0][:, 1:], zs[:, None, EVEN_OFF[4]:EVEN_OFF[5]]], axis=1)[None]

    zs1, _ = _norm_proj(hs1, norm_o[0], w_o, dbp, tn_o, (0, 1))
    zs1 = zs1[:db]
    qidx = (zs1[:, ODD_OFF[4]:ODD_OFF[5]].reshape(db, IDX_HEADS, IDX_DIM) * (IDX_DIM ** -0.5)).astype(BF16)
    wcol = jnp.broadcast_to((zs1[:, ODD_OFF[6]:ODD_OFF[7]] * (IDX_HEADS ** -0.5))[:, :, None],
                            (db, IDX_HEADS, LANES))
    ki_new = zs1[:, ODD_OFF[5]:ODD_OFF[6]]
    ki_tile = jnp.zeros((db, IDX_DIM, page), F32).at[:, :, 0].set(ki_new)
    ci_t = jnp.transpose(cache_c_idx[0], (0, 2, 1))
    keys = _paged_index_scores(qidx, wcol, ki_tile, ci_t, page_table)
    k_sel_s = min(TOPK_MAX, (past + 1) // 4)
    tau, jmax = _select_rows(keys.reshape(db, -1), k_sel_s)
    tau, jmax = tau[:, 0], jmax[:, 0]
    key_new = keys[:, 0, past]
    selnew = ((key_new > tau) | ((key_new == tau) & (past <= jmax))).astype(I32)
    qc = zs1[:, :ODD_OFF[1]].reshape(db, C_KV, C_GROUP, C_DH) * (C_DH ** -0.5)
    qcp = jnp.einsum('bhgd,hi->bhgid', qc, jnp.eye(C_KV, dtype=F32)).reshape(db, C_HEADS, C_KV * C_DH)
    qcp = jnp.pad(qcp, ((0, 0), (0, DEC_ROWS - C_HEADS), (0, 0))).astype(BF16)
    kc_new = zs1[:, ODD_OFF[1]:ODD_OFF[2]].reshape(db, 1, -1)
    vc_new = zs1[:, ODD_OFF[2]:ODD_OFF[3]].reshape(db, 1, -1)
    rows_c = [r if r < C_HEADS else 0 for r in range(DEC_ROWS)]
    cfar, clast, cnew = _decode_bias(bd, rows_c, past, page, g_dec)
    ck_r = cache_c_k[0].reshape(n_pool, page * C_KV, C_DH)
    cv_r = cache_c_v[0].reshape(n_pool, page * C_KV, C_DH)
    oc = _paged_attention(qcp, kc_new, vc_new, cfar, clast, cnew, ck_r, cv_r, page_table, page,
                          k_feature_major=False, mask_args=(keys, tau, jmax, selnew))
    yc_s = _odd_tail(zs1, oc)
    ys = _out_proj([hs_pad(yc_s)], [w_out_c], hs1, dbp, final_gain=final_norm)
    y_sample = ys[:db].reshape(db, 1, d)
    sc_k = zs1[:, ODD_OFF[1]:ODD_OFF[2]].reshape(1, db, 1, C_KV, C_DH)
    sc_v = zs1[:, ODD_OFF[2]:ODD_OFF[3]].reshape(1, db, 1, C_KV, C_DH)
    sc_idx = zs1[:, None, ODD_OFF[5]:ODD_OFF[6]][None]

    return (y_prompt, y_sample, pa_k, pa_v, pb_s[None], pb_conv, pc_k, pc_v, pc_idx,
            sa_k, sa_v, sb_s[None], sb_conv, sc_k, sc_v, sc_idx)
```

```python
import functools
import math

import jax
import jax.numpy as jnp
import numpy as np
from jax import lax
from jax.experimental import pallas as pl
from jax.experimental.pallas import tpu as pltpu

F32 = jnp.float32
BF16 = jnp.bfloat16
I32 = jnp.int32

EPS = 1e-6
N_BUCKETS = 32
MAX_DIST = 128
FAR_DIST = MAX_DIST

A_HEADS, A_KV, A_GROUP, A_DH, A_DV = 8, 2, 4, 64, 128
B_HEADS, B_DK, B_DV, CONV_W = 4, 128, 128, 4
C_HEADS, C_KV, C_GROUP, C_DH = 8, 2, 4, 128
IDX_HEADS, IDX_DIM, TOPK_MAX = 8, 64, 256
B_QKV = 2 * B_HEADS * B_DK + B_HEADS * B_DV

EVEN_COLS = (A_HEADS * 2 * A_DH, A_KV * 2 * A_DH, A_KV * A_DV, A_HEADS * A_DV, B_QKV, B_HEADS * B_DV, B_HEADS, B_HEADS)
ODD_COLS = (C_HEADS * C_DH, C_KV * C_DH, C_KV * C_DH, C_HEADS * C_DH, IDX_HEADS * IDX_DIM, IDX_DIM, IDX_HEADS)
EVEN_OFF = tuple(int(v) for v in np.cumsum((0,) + EVEN_COLS))
ODD_OFF = tuple(int(v) for v in np.cumsum((0,) + ODD_COLS))

LANES = 128
SUBLANES = 8
BLK = 128
CHUNK = 512
SUBS = CHUNK // BLK
NEG = -1e30
INT_MIN = -2 ** 31
INT_MAX = 2 ** 31 - 1
LOG2E = math.log2(math.e)
VMEM_LIMIT = 56 * 1024 * 1024
HI = lax.Precision.HIGHEST

NT = (((1,), (1,)), ((), ()))


def _round_up(x, m):
    return (x + m - 1) // m * m


def _cparams(sem):
    return pltpu.CompilerParams(dimension_semantics=sem, vmem_limit_bytes=VMEM_LIMIT)


def _silu(x):
    return x * jax.nn.sigmoid(x)


def _sortable_key(score):
    score = jnp.where(score == 0.0, 0.0, score)
    bits = pltpu.bitcast(score, I32)
    return bits ^ ((bits >> 31) & jnp.int32(0x7FFFFFFF))


def _lane_tile(x, n):
    return jnp.concatenate([x] * n, axis=1)


def _proj_kernel(x_ref, g_ref, w_ref, of_ref, ob_ref, xn_ref, *, bf16_tiles):
    j = pl.program_id(1)

    @pl.when(j == 0)
    def _():
        xf = x_ref[...]
        y = xf * lax.rsqrt(jnp.mean(xf * xf, axis=-1, keepdims=True) + EPS)
        xn_ref[...] = (y * g_ref[...]).astype(BF16)

    acc = jnp.dot(xn_ref[...], w_ref[...], preferred_element_type=F32)
    of_ref[...] = acc

    @pl.when((j >= bf16_tiles[0]) & (j < bf16_tiles[1]))
    def _():
        ob_ref[...] = acc.astype(BF16)


def _norm_proj(x, gain, w_bf16, tm, tn, bf16_tiles):
    m, d = x.shape
    n = w_bf16.shape[1]
    t0, t1 = bf16_tiles
    return pl.pallas_call(
        functools.partial(_proj_kernel, bf16_tiles=bf16_tiles),
        grid=(m // tm, n // tn),
        in_specs=[pl.BlockSpec((tm, d), lambda i, j: (i, 0)),
                  pl.BlockSpec((1, d), lambda i, j: (0, 0)),
                  pl.BlockSpec((d, tn), lambda i, j: (0, j))],
        out_specs=[pl.BlockSpec((tm, tn), lambda i, j: (i, j)),
                   pl.BlockSpec((tm, tn), lambda i, j: (i, jnp.clip(j - t0, 0, t1 - t0 - 1)))],
        out_shape=[jax.ShapeDtypeStruct((m, n), F32), jax.ShapeDtypeStruct((m, (t1 - t0) * tn), BF16)],
        scratch_shapes=[pltpu.VMEM((tm, d), BF16)],
        compiler_params=_cparams(("parallel", "arbitrary")),
        name="norm_proj",
    )(x, gain.reshape(1, d), w_bf16)


def _outproj_kernel(*refs, n_lhs, final_norm):
    ys = refs[:n_lhs]
    ws = refs[n_lhs:2 * n_lhs]
    h_ref = refs[2 * n_lhs]
    pos = 2 * n_lhs + 1
    fn_ref = refs[pos] if final_norm else None
    out_ref = refs[-1]
    acc = h_ref[...]
    for y_ref, w_ref in zip(ys, ws):
        acc = acc + jnp.dot(y_ref[...], w_ref[...], preferred_element_type=F32)
    if final_norm:
        y = acc * lax.rsqrt(jnp.mean(acc * acc, axis=-1, keepdims=True) + EPS)
        acc = y * fn_ref[...]
    out_ref[...] = acc


def _out_proj(ys, ws, h, tm, final_gain=None):
    m, n = h.shape
    n_lhs = len(ys)
    in_specs = [pl.BlockSpec((tm, y.shape[1]), lambda i: (i, 0)) for y in ys]
    in_specs += [pl.BlockSpec(w.shape, lambda i: (0, 0)) for w in ws]
    in_specs += [pl.BlockSpec((tm, n), lambda i: (i, 0))]
    args = list(ys) + list(ws) + [h]
    if final_gain is not None:
        in_specs += [pl.BlockSpec((1, n), lambda i: (0, 0))]
        args += [final_gain.reshape(1, n)]
    return pl.pallas_call(
        functools.partial(_outproj_kernel, n_lhs=n_lhs, final_norm=final_gain is not None),
        grid=(m // tm,),
        in_specs=in_specs,
        out_specs=pl.BlockSpec((tm, n), lambda i: (i, 0)),
        out_shape=jax.ShapeDtypeStruct((m, n), F32),
        compiler_params=_cparams(("parallel",)),
        name="out_proj",
    )(*args)


def _flash_steps(ss, v1s, m_refs):
    each = range(len(ss))
    m_old = [m_refs[i][0][m_refs[i][2]] for i in each]
    m_new = [jnp.maximum(m_old[i], jnp.max(ss[i], axis=1, keepdims=True)) for i in each]
    alpha = [jnp.exp2(m_old[i] - m_new[i]) for i in each]
    p = [jnp.exp2(ss[i] - _lane_tile(m_new[i], ss[i].shape[1] // LANES)).astype(BF16) for i in each]
    pv = [jnp.dot(p[i], v1s[i], preferred_element_type=F32) for i in each]
    for i in each:
        m_ref, acc_ref, idx = m_refs[i]
        acc_ref[idx] = _lane_tile(alpha[i], 2) * acc_ref[idx] + pv[i]
        m_ref[idx] = m_new[i]


def _far_chunk_loop(n_far, run_chunks):
    def quad(i, carry):
        run_chunks([4 * i, 4 * i + 1, 4 * i + 2, 4 * i + 3])
        return carry

    n_quad = lax.shift_right_logical(n_far, 2)
    lax.fori_loop(0, n_quad, quad, 0)

    @pl.when(jnp.bitwise_and(n_far, 2) == 2)
    def _():
        run_chunks([4 * n_quad, 4 * n_quad + 1])

    @pl.when(jnp.bitwise_and(n_far, 1) == 1)
    def _():
        run_chunks([n_far - 1])


TILE_ZERO, TILE_PREV, TILE_DIAG, TILE_MASKED = range(4)


def _near_bias_kind(sb, r):
    rel = sb - SUBS - r
    return jnp.where(rel == 0, TILE_DIAG, jnp.where(rel == -1, TILE_PREV, jnp.where(rel < -1, TILE_ZERO, TILE_MASKED)))


def _diff_attn_kernel(q_ref, k_ref, v_ref, gate_ref, tiles_ref, lamv_ref, subln_ref, out_ref,
                      qm_ref, m_ref, acc_ref, *, lam_init, nq_real):
    qi = pl.program_id(2)

    @pl.when(qi >= nq_real)
    def _():
        out_ref[...] = jnp.zeros(out_ref.shape, out_ref.dtype)

    @pl.when(qi < nq_real)
    def _():
        lane = lax.broadcasted_iota(I32, (BLK, 2 * A_DH), 1)
        for g in range(A_GROUP):
            qg = q_ref[:, g * 2 * A_DH:(g + 1) * 2 * A_DH] * (A_DH ** -0.5 * LOG2E)
            for c in range(2):
                keep = (lane < A_DH) if c == 0 else (lane >= A_DH)
                qm_ref[c, g * BLK:(g + 1) * BLK, :] = jnp.where(keep, qg, 0.0).astype(BF16)
        m_ref[...] = jnp.full(m_ref.shape, NEG, F32)
        acc_ref[...] = jnp.zeros(acc_ref.shape, F32)
        r = jnp.bitwise_and(qi, SUBS - 1)
        jd = lax.shift_right_logical(qi, int(math.log2(SUBS)))
        ones = jnp.ones((CHUNK, A_DV), BF16)

        def run_chunks(js, first_sbs=None):
            starts = [pl.multiple_of(j * CHUNK, CHUNK) for j in js]
            k = jnp.concatenate([k_ref[pl.ds(st, CHUNK), :] for st in starts], axis=0)
            v1 = jnp.concatenate(
                [jnp.concatenate([v_ref[pl.ds(st, CHUNK), :], ones], axis=1) for st in starts], axis=0)
            ss = [lax.dot_general(qm_ref[c], k, NT, preferred_element_type=F32) for c in range(2)]
            if first_sbs is not None:
                bias = jnp.concatenate(
                    [tiles_ref[_near_bias_kind(sb + i, r)] for sb in first_sbs for i in range(SUBS)], axis=1)
                ss = [s + bias for s in ss]
            _flash_steps(ss, [v1, v1], [(m_ref, acc_ref, c) for c in range(2)])

        def near_pair(rr):
            n_diag = (rr + 1) * BLK
            st_prev = pl.multiple_of((jd - 1) * CHUNK, CHUNK)
            st_diag = pl.multiple_of(jd * CHUNK, CHUNK)
            k = jnp.concatenate([k_ref[pl.ds(st_prev, CHUNK), :], k_ref[pl.ds(st_diag, n_diag), :]], axis=0)
            v = jnp.concatenate([v_ref[pl.ds(st_prev, CHUNK), :], v_ref[pl.ds(st_diag, n_diag), :]], axis=0)
            v1 = jnp.concatenate([v, jnp.ones((CHUNK + n_diag, A_DV), BF16)], axis=1)
            bias = jnp.concatenate([tiles_ref[TILE_PREV], tiles_ref[TILE_DIAG]], axis=1)
            first = CHUNK + (rr - 1) * BLK
            ss = []
            for c in range(2):
                s = lax.dot_general(qm_ref[c], k, NT, preferred_element_type=F32)
                ss.append(jnp.concatenate([s[:, :first], s[:, first:] + bias], axis=1))
            _flash_steps(ss, [v1, v1], [(m_ref, acc_ref, c) for c in range(2)])

        _far_chunk_loop(jnp.maximum(jd - 1, 0), run_chunks)

        for rr in range(SUBS):
            @pl.when((jd >= 1) & (r == rr))
            def _(rr=rr):
                near_pair(rr)

        @pl.when(jd == 0)
        def _():
            run_chunks([jd], [SUBS])

        lv = lamv_ref[...]
        lam = (jnp.exp(jnp.sum(lv[0:1] * lv[1:2], axis=1, keepdims=True))
               - jnp.exp(jnp.sum(lv[2:3] * lv[3:4], axis=1, keepdims=True)) + lam_init)
        a0 = acc_ref[0]
        a1 = acc_ref[1]
        o = a0[:, :A_DV] / a0[:, A_DV:] - lam * (a1[:, :A_DV] / a1[:, A_DV:])
        y = o * lax.rsqrt(jnp.mean(o * o, axis=-1, keepdims=True) + EPS)
        y = (y * subln_ref[...]) * (1.0 - lam_init)
        for g in range(A_GROUP):
            gate = gate_ref[:, g * A_DV:(g + 1) * A_DV]
            out_ref[:, g * A_DV:(g + 1) * A_DV] = (y[g * BLK:(g + 1) * BLK] * _silu(gate)).astype(BF16)


def _diff_attn_prompt(zf, zb, zb_col0, near_tiles, lamv, subln, nb, lp, nq_real, lam_init):
    nq = lp // BLK
    qw = A_GROUP * 2 * A_DH
    k_blk0 = (EVEN_OFF[1] - zb_col0) // (2 * A_DH)
    v_blk0 = (EVEN_OFF[2] - zb_col0) // A_DV
    g_blk0 = EVEN_OFF[3] // (A_GROUP * A_DV)
    rows = A_GROUP * BLK
    return pl.pallas_call(
        functools.partial(_diff_attn_kernel, lam_init=lam_init, nq_real=nq_real),
        grid=(nb, A_KV, nq),
        in_specs=[pl.BlockSpec((BLK, qw), lambda b, h, i: (b * nq + i, h)),
                  pl.BlockSpec((lp, 2 * A_DH), lambda b, h, i: (b, k_blk0 + h)),
                  pl.BlockSpec((lp, A_DV), lambda b, h, i: (b, v_blk0 + h)),
                  pl.BlockSpec((BLK, A_GROUP * A_DV), lambda b, h, i: (b * nq + i, g_blk0 + h)),
                  pl.BlockSpec((None, 4, rows, BLK), lambda b, h, i: (h, 0, 0, 0)),
                  pl.BlockSpec((4, A_DH), lambda b, h, i: (0, 0)),
                  pl.BlockSpec((1, A_DV), lambda b, h, i: (0, 0))],
        out_specs=pl.BlockSpec((BLK, A_GROUP * A_DV), lambda b, h, i: (b * nq + i, h)),
        out_shape=jax.ShapeDtypeStruct((nb * lp, A_HEADS * A_DV), BF16),
        scratch_shapes=[pltpu.VMEM((2, rows, 2 * A_DH), BF16),
                        pltpu.VMEM((2, rows, LANES), F32),
                        pltpu.VMEM((2, rows, 2 * A_DV), F32)],
        compiler_params=_cparams(("parallel", "parallel", "arbitrary")),
        name="diff_attn_prompt",
    )(zf, zb, zb, zf, near_tiles, lamv, subln.reshape(1, A_DV))


def _softplus(x):
    return jnp.maximum(x, 0.0) + jnp.log1p(jnp.exp(-jnp.abs(x)))


def _split_bf16(a):
    hi = a.astype(BF16)
    return hi, (a - hi.astype(F32)).astype(BF16)


def _dot3(a, b):
    (ah, al), (bh, bl) = a, b
    return (jnp.dot(ah, bh, preferred_element_type=F32)
            + (jnp.dot(ah, bl, preferred_element_type=F32) + jnp.dot(al, bh, preferred_element_type=F32)))


def _gdn_chunk_math(s0, q, k, v, gb, betab, row_ge, row_gt):
    n = len(q)
    c = q[0].shape[0]
    each = range(n)
    dot = functools.partial(jnp.dot, preferred_element_type=F32)
    dot_nt = functools.partial(lax.dot_general, dimension_numbers=NT, preferred_element_type=F32)
    ltri = jnp.where(row_ge, 1.0, 0.0).astype(F32)
    gcum = [jnp.dot(ltri, gb[i], precision=HI, preferred_element_type=F32) for i in each]
    decay = [jnp.where(row_ge, jnp.exp(jnp.where(row_ge, gcum[i] - gcum[i].T, 0.0)), 0.0) for i in each]
    kb = [k[i].astype(BF16) for i in each]
    qb = [q[i].astype(BF16) for i in each]
    s0b = [s0[i].astype(BF16) for i in each]
    kk = [dot_nt(kb[i], kb[i]) for i in each]
    x = [-jnp.where(row_gt, betab[i] * decay[i] * kk[i], 0.0) for i in each]
    tm = list(x)
    ps = [_split_bf16(x[i]) for i in each]
    for _ in range(int(math.log2(c)) - 1):
        p = [_dot3(ps[i], ps[i]) for i in each]
        ps = [_split_bf16(p[i]) for i in each]
        tms = [_split_bf16(tm[i]) for i in each]
        tm = [tm[i] + p[i] + _dot3(tms[i], ps[i]) for i in each]
    eg = [jnp.exp(gcum[i]) for i in each]
    ks = [dot(kb[i], s0b[i]) for i in each]
    rhs = [betab[i] * (v[i] - eg[i] * ks[i]) for i in each]
    u = [rhs[i] + dot(tm[i].astype(BF16), rhs[i].astype(BF16)) for i in each]
    ub = [u[i].astype(BF16) for i in each]
    qk = [dot_nt(qb[i], kb[i]) * decay[i] for i in each]
    o = [eg[i] * dot(qb[i], s0b[i]) + dot(qk[i].astype(BF16), ub[i]) for i in each]
    glast = [gcum[i][c - 1:c, :] for i in each]
    kd = [k[i] * jnp.exp(glast[i] - gcum[i]) for i in each]
    s_new = [jnp.exp(glast[i]) * s0[i] + dot(kd[i].T.astype(BF16), ub[i]) for i in each]
    return s_new, o


def _gdn_prompt_kernel(xq_ref, xk_ref, xv_ref, gate_ref, ab_ref, convw_ref, gp_ref, normg_ref,
                       y_ref, sfin_ref, xbuf_ref, s_ref, *, seq_len):
    ci = pl.program_id(0)
    hw = B_HEADS * B_DK
    nb = xq_ref.shape[0]

    @pl.when(ci == 0)
    def _():
        xbuf_ref[:, 0:SUBLANES, :] = jnp.zeros((nb, SUBLANES, B_QKV), F32)
        s_ref[...] = jnp.zeros(s_ref.shape, F32)

    row = lax.broadcasted_iota(I32, (BLK, BLK), 0)
    col = lax.broadcasted_iota(I32, (BLK, BLK), 1)
    row_ge = row >= col
    row_gt = row > col
    valid = (ci * BLK + row) < seq_len
    gp = gp_ref[...]
    chains = [(b, h) for b in range(nb) for h in range(B_HEADS)]
    qs, ks, vs, gbs, betabs = [], [], [], [], []
    for b in range(nb):
        xbuf_ref[b, SUBLANES:SUBLANES + BLK, 0:hw] = xq_ref[b]
        xbuf_ref[b, SUBLANES:SUBLANES + BLK, hw:2 * hw] = xk_ref[b]
        xbuf_ref[b, SUBLANES:SUBLANES + BLK, 2 * hw:3 * hw] = xv_ref[b]
        conv = jnp.zeros((BLK, B_QKV), F32)
        for i in range(CONV_W):
            conv = conv + convw_ref[i:i + 1, :] * xbuf_ref[b, pl.ds(SUBLANES - (CONV_W - 1) + i, BLK), :]
        tail = xbuf_ref[b, BLK:BLK + SUBLANES, :]
        xbuf_ref[b, 0:SUBLANES, :] = tail
        act = _silu(conv)
        ab = ab_ref[b]
        for h in range(B_HEADS):
            q = act[:, h * B_DK:(h + 1) * B_DK]
            k = act[:, hw + h * B_DK:hw + (h + 1) * B_DK]
            v = act[:, 2 * hw + h * B_DV:2 * hw + (h + 1) * B_DV]
            q = q * lax.rsqrt(jnp.sum(q * q, axis=-1, keepdims=True) + EPS) * (B_DK ** -0.5)
            k = k * lax.rsqrt(jnp.sum(k * k, axis=-1, keepdims=True) + EPS)
            a_raw = jnp.broadcast_to(ab[:, h:h + 1], (BLK, BLK))
            b_raw = jnp.broadcast_to(ab[:, B_HEADS + h:B_HEADS + h + 1], (BLK, BLK))
            a_log = gp[0:1, h:h + 1]
            dt_b = gp[1:2, h:h + 1]
            qs.append(q)
            ks.append(k)
            vs.append(v)
            gbs.append(jnp.where(valid, -jnp.exp(a_log) * _softplus(a_raw + dt_b), 0.0))
            betabs.append(jnp.where(valid, jax.nn.sigmoid(b_raw), 0.0))
    s_new, outs = _gdn_chunk_math([s_ref[b, h] for b, h in chains], qs, ks, vs, gbs, betabs, row_ge, row_gt)
    for (b, h), s_bh, o in zip(chains, s_new, outs):
        s_ref[b, h] = s_bh
        y = o * lax.rsqrt(jnp.mean(o * o, axis=-1, keepdims=True) + EPS) * normg_ref[...]
        gate = gate_ref[b, :, h * B_DV:(h + 1) * B_DV]
        y_ref[b, :, h * B_DV:(h + 1) * B_DV] = (y * _silu(gate)).astype(BF16)

    @pl.when(ci == pl.num_programs(0) - 1)
    def _():
        sfin_ref[...] = s_ref[...]


def _gdn_prompt(zf, conv_w, a_log, dt_bias, norm_g, nb, lp, seq_len):
    nc = lp // BLK
    hw = B_HEADS * B_DK
    c0 = EVEN_OFF[4] // hw
    g0 = EVEN_OFF[5] // hw
    ab0 = EVEN_OFF[6] // LANES
    gp = jnp.zeros((SUBLANES, LANES), F32).at[0, :B_HEADS].set(a_log).at[1, :B_HEADS].set(dt_bias)
    z3 = zf.reshape(nb, lp, zf.shape[1])
    y, s_fin = pl.pallas_call(
        functools.partial(_gdn_prompt_kernel, seq_len=seq_len),
        grid=(nc,),
        in_specs=[pl.BlockSpec((nb, BLK, hw), lambda i: (0, i, c0)),
                  pl.BlockSpec((nb, BLK, hw), lambda i: (0, i, c0 + 1)),
                  pl.BlockSpec((nb, BLK, hw), lambda i: (0, i, c0 + 2)),
                  pl.BlockSpec((nb, BLK, hw), lambda i: (0, i, g0)),
                  pl.BlockSpec((nb, BLK, LANES), lambda i: (0, i, ab0)),
                  pl.BlockSpec((CONV_W, B_QKV), lambda i: (0, 0)),
                  pl.BlockSpec((SUBLANES, LANES), lambda i: (0, 0)),
                  pl.BlockSpec((1, B_DV), lambda i: (0, 0))],
        out_specs=[pl.BlockSpec((nb, BLK, hw), lambda i: (0, i, 0)),
                   pl.BlockSpec((nb, B_HEADS, B_DK, B_DV), lambda i: (0, 0, 0, 0))],
        out_shape=[jax.ShapeDtypeStruct((nb, lp, hw), BF16),
                   jax.ShapeDtypeStruct((nb, B_HEADS, B_DK, B_DV), F32)],
        scratch_shapes=[pltpu.VMEM((nb, BLK + SUBLANES, B_QKV), F32),
                        pltpu.VMEM((nb, B_HEADS, B_DK, B_DV), F32)],
        compiler_params=_cparams(("arbitrary",)),
        name="gdn_prompt",
    )(z3, z3, z3, z3, z3, conv_w, gp, norm_g.reshape(1, B_DV))
    return y.reshape(nb * lp, hw), s_fin


def _kth_largest(count_fn, k_sel, shape):
    def step(i, tau):
        cand = tau + lax.shift_left(jnp.int32(1), 31 - i)
        cnt = count_fn(lambda key, idx: key >= cand)
        return jnp.where(cnt >= k_sel, cand, tau)

    return lax.fori_loop(0, 32, step, jnp.full(shape, INT_MIN, I32))


BITS_PER_CHECK = 4


def _kth_separator(count_fn, k_sel, n_keys):
    assert 32 % BITS_PER_CHECK == 0

    def pending(cnt_tau):
        return jnp.max(jnp.where(cnt_tau > k_sel, 1.0, 0.0)) > 0.5

    def group(carry):
        g, tau, cnt_tau, _ = carry
        for b in range(BITS_PER_CHECK):
            cand = tau + lax.shift_left(jnp.int32(1), 31 - (g * BITS_PER_CHECK + b))
            cnt = count_fn(lambda key, idx: key >= cand)
            take = cnt >= k_sel
            tau = jnp.where(take, cand, tau)
            cnt_tau = jnp.where(take, cnt, cnt_tau)
        return g + 1, tau, cnt_tau, pending(cnt_tau)

    init = (jnp.int32(0), jnp.full(n_keys.shape, INT_MIN, I32), n_keys, pending(n_keys))
    out = lax.while_loop(lambda c: (c[0] < 32 // BITS_PER_CHECK) & c[3], group, init)
    return out[1], out[2]


def _tie_cutoff(count_fn, tau, need, n_index_bits, shape):
    def step(i, jm):
        cand = jm | lax.shift_left(jnp.int32(1), n_index_bits - 1 - i)
        cnt = count_fn(lambda key, idx: (key == tau) & (idx < cand))
        return jnp.where(cnt < need, cand, jm)

    return lax.fori_loop(0, n_index_bits, step, jnp.zeros(shape, I32))


def _sparse_attn_kernel(qi_ref, kw_ref, kidx_ref, qc_ref, kc_ref, vc_ref, g0_ref, g1_ref, tiles_ref, out_ref,
                        keys_ref, qm_ref, qh_ref, tau_ref, jmax_ref, m_ref, acc_ref,
                        *, k_sel, n_index_bits, nq_real):
    qb = pl.program_id(1)

    @pl.when(qb >= nq_real)
    def _():
        out_ref[...] = jnp.zeros(out_ref.shape, out_ref.dtype)

    @pl.when(qb < nq_real)
    def _():
        lane = lax.broadcasted_iota(I32, (BLK, LANES), 1)
        key_pos = lax.broadcasted_iota(I32, (CHUNK, BLK), 0)
        qry_pos = lax.broadcasted_iota(I32, (CHUNK, BLK), 1)
        r = jnp.bitwise_and(qb, SUBS - 1)
        jd = lax.shift_right_logical(qb, int(math.log2(SUBS)))

        for p in range(IDX_HEADS // 2):
            pair = qi_ref[:, p * LANES:(p + 1) * LANES] * (IDX_DIM ** -0.5)
            swapped = pltpu.roll(pair, IDX_DIM, 1)
            for e, src in enumerate((pair, swapped)):
                h = 2 * p + e
                qm_ref[h * BLK:(h + 1) * BLK, :] = jnp.where(lane < IDX_DIM, src, 0.0).astype(BF16)
        w_rows = kw_ref[...].T * (IDX_HEADS ** -0.5)

        def scores_t(j):
            start = pl.multiple_of(j * CHUNK, CHUNK)
            d = lax.dot_general(kidx_ref[pl.ds(start, CHUNK), :], qm_ref[...], NT, preferred_element_type=F32)
            sc = jnp.zeros((CHUNK, BLK), F32)
            for h in range(IDX_HEADS):
                sc = sc + w_rows[IDX_DIM + h:IDX_DIM + h + 1, :] * jnp.maximum(d[:, h * BLK:(h + 1) * BLK], 0.0)
            return sc

        def score_quad(i, carry):
            for j in (4 * i, 4 * i + 1, 4 * i + 2, 4 * i + 3):
                keys_ref[j] = _sortable_key(scores_t(j))
            return carry

        n_quad = lax.shift_right_logical(jd, 2)
        lax.fori_loop(0, n_quad, score_quad, 0)

        @pl.when(jnp.bitwise_and(jd, 2) == 2)
        def _():
            for j in (4 * n_quad, 4 * n_quad + 1):
                keys_ref[j] = _sortable_key(scores_t(j))

        @pl.when(jnp.bitwise_and(jd, 1) == 1)
        def _():
            keys_ref[jd - 1] = _sortable_key(scores_t(jd - 1))

        admissible = (jd * CHUNK + key_pos) <= (qb * BLK + qry_pos)
        keys_ref[jd] = _sortable_key(jnp.where(admissible, scores_t(jd), -jnp.inf))

        def count_fn(pred):
            part = SUBLANES * SUBLANES

            def body(j, acc):
                hit = jnp.where(pred(keys_ref[j], j * CHUNK + key_pos), 1.0, 0.0)
                return acc + jnp.sum(hit.reshape(CHUNK // part, part, BLK), axis=0)
            acc = lax.fori_loop(0, jd + 1, body, jnp.zeros((part, BLK), F32))
            return jnp.sum(acc, axis=0, keepdims=True).astype(I32)

        n_keys = qb * BLK + lax.broadcasted_iota(I32, (1, BLK), 1) + 1
        tau, n_ge = _kth_separator(count_fn, k_sel, n_keys)
        tau_ref[...] = tau
        jmax_ref[...] = jnp.full((1, BLK), INT_MAX, I32)
        excess = jnp.max(jnp.where(n_ge > k_sel, 1.0, 0.0)) > 0.5

        @pl.when(excess)
        def _():
            need = k_sel - count_fn(lambda key, idx: key > tau)
            jmax_ref[...] = _tie_cutoff(count_fn, tau, need, n_index_bits, (1, BLK))

        jmax = jmax_ref[...]

        for h in range(C_KV):
            for g in range(C_GROUP):
                col = (h * C_GROUP + g) * C_DH
                qh_ref[h, g * BLK:(g + 1) * BLK, :] = (
                    qc_ref[:, col:col + C_DH] * (C_DH ** -0.5 * LOG2E)).astype(BF16)
        m_ref[...] = jnp.full(m_ref.shape, NEG, F32)
        acc_ref[...] = jnp.zeros(acc_ref.shape, F32)
        ones = jnp.ones((CHUNK, C_DH), BF16)

        def run_chunks(js, first_sbs=None):
            starts = [pl.multiple_of(j * CHUNK, CHUNK) for j in js]
            masks = []
            for j in js:
                key = keys_ref[j]
                tie_ok = jnp.where((j * CHUNK + key_pos) <= jmax, 0.0, NEG)
                sel_t = jnp.where(key > tau, 0.0, jnp.where(key == tau, tie_ok, NEG))
                masks += [sel_t[i * BLK:(i + 1) * BLK, :].T for i in range(SUBS)]
            selneg = jnp.concatenate(masks, axis=1)
            selneg = jnp.concatenate([selneg] * C_GROUP, axis=0)
            ss, v1s = [], []
            for h in range(C_KV):
                cols = slice(h * C_DH, (h + 1) * C_DH)
                k = jnp.concatenate([kc_ref[pl.ds(st, CHUNK), cols] for st in starts], axis=0)
                v1s.append(jnp.concatenate(
                    [jnp.concatenate([vc_ref[pl.ds(st, CHUNK), cols], ones], axis=1) for st in starts], axis=0))
                s = lax.dot_general(qh_ref[h], k, NT, preferred_element_type=F32) + selneg
                if first_sbs is not None:
                    s = s + jnp.concatenate(
                        [tiles_ref[h, _near_bias_kind(sb + i, r)] for sb in first_sbs for i in range(SUBS)], axis=1)
                ss.append(s)
            _flash_steps(ss, v1s, [(m_ref, acc_ref, h) for h in range(C_KV)])

        def mask_blocks(j, n_blocks):
            key = keys_ref[j][:n_blocks * BLK]
            tie_ok = jnp.where((j * CHUNK + key_pos[:n_blocks * BLK]) <= jmax, 0.0, NEG)
            sel_t = jnp.where(key > tau, 0.0, jnp.where(key == tau, tie_ok, NEG))
            return [sel_t[i * BLK:(i + 1) * BLK, :].T for i in range(n_blocks)]

        def near_pair(rr):
            n_diag = (rr + 1) * BLK
            st_prev = pl.multiple_of((jd - 1) * CHUNK, CHUNK)
            st_diag = pl.multiple_of(jd * CHUNK, CHUNK)
            selneg = jnp.concatenate(mask_blocks(jd - 1, SUBS) + mask_blocks(jd, rr + 1), axis=1)
            selneg = jnp.concatenate([selneg] * C_GROUP, axis=0)
            first = CHUNK + (rr - 1) * BLK
            ones_n = jnp.ones((CHUNK + n_diag, C_DH), BF16)
            ss, v1s = [], []
            for h in range(C_KV):
                cols = slice(h * C_DH, (h + 1) * C_DH)
                k = jnp.concatenate([kc_ref[pl.ds(st_prev, CHUNK), cols], kc_ref[pl.ds(st_diag, n_diag), cols]], axis=0)
                v = jnp.concatenate([vc_ref[pl.ds(st_prev, CHUNK), cols], vc_ref[pl.ds(st_diag, n_diag), cols]], axis=0)
                v1s.append(jnp.concatenate([v, ones_n], axis=1))
                s = lax.dot_general(qh_ref[h], k, NT, preferred_element_type=F32) + selneg
                bias = jnp.concatenate([tiles_ref[h, TILE_PREV], tiles_ref[h, TILE_DIAG]], axis=1)
                ss.append(jnp.concatenate([s[:, :first], s[:, first:] + bias], axis=1))
            _flash_steps(ss, v1s, [(m_ref, acc_ref, h) for h in range(C_KV)])

        _far_chunk_loop(jnp.maximum(jd - 1, 0), run_chunks)

        for rr in range(SUBS):
            @pl.when((jd >= 1) & (r == rr))
            def _(rr=rr):
                near_pair(rr)

        @pl.when(jd == 0)
        def _():
            run_chunks([jd], [SUBS])

        half = (C_HEADS // 2) * C_DH
        for h in range(C_KV):
            a = acc_ref[h]
            o = a[:, :C_DH] / a[:, C_DH:]
            gref = g0_ref if h == 0 else g1_ref
            for g in range(C_GROUP):
                gate = gref[:, g * C_DH:(g + 1) * C_DH]
                out_ref[:, h * half + g * C_DH:h * half + (g + 1) * C_DH] = (
                    o[g * BLK:(g + 1) * BLK] * _silu(gate)).astype(BF16)


def _sparse_attn_prompt(zf, zb, near_tiles, nb, lp, nq_real, k_sel):
    nq = lp // BLK
    half = (C_HEADS // 2) * C_DH
    qi0 = ODD_OFF[4] // (IDX_HEADS * IDX_DIM)
    ki0 = ODD_OFF[5] // LANES
    kc0 = ODD_OFF[1] // (C_KV * C_DH)
    vc0 = ODD_OFF[2] // (C_KV * C_DH)
    g0 = ODD_OFF[3] // half
    rows = C_GROUP * BLK
    n_index_bits = max(1, int(math.ceil(math.log2(lp))))
    return pl.pallas_call(
        functools.partial(_sparse_attn_kernel, k_sel=k_sel, n_index_bits=n_index_bits, nq_real=nq_real),
        grid=(nb, nq),
        in_specs=[pl.BlockSpec((BLK, IDX_HEADS * IDX_DIM), lambda b, i: (b * nq + i, qi0)),
                  pl.BlockSpec((BLK, LANES), lambda b, i: (b * nq + i, ki0)),
                  pl.BlockSpec((lp, LANES), lambda b, i: (b, ki0)),
                  pl.BlockSpec((BLK, C_HEADS * C_DH), lambda b, i: (b * nq + i, 0)),
                  pl.BlockSpec((lp, C_KV * C_DH), lambda b, i: (b, kc0)),
                  pl.BlockSpec((lp, C_KV * C_DH), lambda b, i: (b, vc0)),
                  pl.BlockSpec((BLK, half), lambda b, i: (b * nq + i, g0)),
                  pl.BlockSpec((BLK, half), lambda b, i: (b * nq + i, g0 + 1)),
                  pl.BlockSpec((C_KV, 4, rows, BLK), lambda b, i: (0, 0, 0, 0))],
        out_specs=pl.BlockSpec((BLK, C_HEADS * C_DH), lambda b, i: (b * nq + i, 0)),
        out_shape=jax.ShapeDtypeStruct((nb * lp, C_HEADS * C_DH), BF16),
        scratch_shapes=[pltpu.VMEM((lp // CHUNK, CHUNK, BLK), I32),
                        pltpu.VMEM((IDX_HEADS * BLK, LANES), BF16),
                        pltpu.VMEM((C_KV, rows, C_DH), BF16),
                        pltpu.VMEM((1, BLK), I32),
                        pltpu.VMEM((1, BLK), I32),
                        pltpu.VMEM((C_KV, rows, LANES), F32),
                        pltpu.VMEM((C_KV, rows, 2 * C_DH), F32)],
        compiler_params=_cparams(("parallel", "arbitrary")),
        name="sparse_attn_prompt",
    )(zf, zf, zb, zf, zb, zb, zf, zf, near_tiles)


PAGES_PER_STEP = 64
INDEX_PAGES_PER_STEP = 64
DEC_ROWS = 16


def _interleaved_pages(refs, page):
    halves = [jnp.concatenate([r[pl.ds(h, page, stride=2), :] for r in refs], axis=0) for h in range(2)]
    return jnp.concatenate(halves, axis=1).astype(BF16)


def _paged_attn_kernel(*refs, n_pages_step, masked, page, k_feature_major):
    if masked:
        pt_ref, tau_ref, jmax_ref, selnew_ref = refs[:4]
        refs = refs[4:]
    else:
        pt_ref = refs[0]
        refs = refs[1:]
    q_ref, knew_ref, vnew_ref, bfar_ref, blast_ref, b0_ref = refs[:6]
    refs = refs[6:]
    if masked:
        keys_ref = refs[0]
        refs = refs[1:]
    k_refs = refs[:n_pages_step]
    v_refs = refs[n_pages_step:2 * n_pages_step]
    out_ref, m_ref, l_ref, acc_ref = refs[2 * n_pages_step:]
    del pt_ref
    b = pl.program_id(0)
    j = pl.program_id(1)
    width = n_pages_step * page
    q = q_ref[...]

    @pl.when(j == 0)
    def _():
        s_new = jnp.sum(q.astype(F32) * knew_ref[...].astype(BF16).astype(F32), axis=1, keepdims=True) + b0_ref[...]
        v_new = jnp.broadcast_to(vnew_ref[...].astype(BF16).astype(F32), acc_ref.shape)
        if masked:
            take = selnew_ref[b] > 0
            m_ref[...] = jnp.where(take, s_new, NEG)
            l_ref[...] = jnp.where(take, 1.0, 0.0) * jnp.ones(l_ref.shape, F32)
            acc_ref[...] = jnp.where(take, v_new, 0.0)
        else:
            m_ref[...] = s_new
            l_ref[...] = jnp.ones(l_ref.shape, F32)
            acc_ref[...] = v_new

    if k_feature_major:
        kcat = jnp.concatenate([r[...] for r in k_refs], axis=1).astype(BF16)
        s = jnp.dot(q, kcat, preferred_element_type=F32)
    else:
        s = lax.dot_general(q, _interleaved_pages(k_refs, page), NT, preferred_element_type=F32)
    vcat = _interleaved_pages(v_refs, page)
    s = s + jnp.where(j == pl.num_programs(1) - 1, blast_ref[...], bfar_ref[...])
    if masked:
        key = keys_ref[...]
        idx = j * width + lax.broadcasted_iota(I32, (1, width), 1)
        tau = tau_ref[b]
        sel = (key > tau) | ((key == tau) & (idx <= jmax_ref[b]))
        s = jnp.where(sel, s, NEG)
    m_old = m_ref[...]
    m_new = jnp.maximum(m_old, jnp.max(s, axis=1, keepdims=True))
    alpha = jnp.exp(m_old - m_new)
    p = jnp.exp(s - m_new[:, 0:1])
    l_ref[...] = alpha * l_ref[...] + jnp.sum(p, axis=1, keepdims=True)
    acc_ref[...] = _lane_tile(alpha, 2) * acc_ref[...] + jnp.dot(p.astype(BF16), vcat, preferred_element_type=F32)
    m_ref[...] = m_new

    @pl.when(j == pl.num_programs(1) - 1)
    def _():
        o = acc_ref[...]
        rowi = lax.broadcasted_iota(I32, (DEC_ROWS, LANES), 0)
        upper = (rowi >= DEC_ROWS // 2) if not masked else ((rowi >= DEC_ROWS // 4) & (rowi < DEC_ROWS // 2))
        out_ref[...] = jnp.where(upper, o[:, LANES:], o[:, :LANES]) / l_ref[...]


def _pages_per_step(n_pages, g=PAGES_PER_STEP):
    while n_pages % g:
        g //= 2
    return g


def _paged_attention(qprime, k_new, v_new, bias_far, bias_last, bias0, k_cache, v_cache, page_table, page,
                     k_feature_major, mask_args=None):
    db = qprime.shape[0]
    n_pages = page_table.shape[1]
    width = qprime.shape[2]
    g = _pages_per_step(n_pages)
    n_steps = n_pages // g
    masked = mask_args is not None
    n_pref = 4 if masked else 1

    def page_map(gi):
        return lambda b, j, pt, *_: (pt[b * n_pages + j * g + gi], 0, 0)

    in_specs = [pl.BlockSpec((None, DEC_ROWS, width), lambda b, j, *_: (b, 0, 0)),
                pl.BlockSpec((None, 1, width), lambda b, j, *_: (b, 0, 0)),
                pl.BlockSpec((None, 1, width), lambda b, j, *_: (b, 0, 0)),
                pl.BlockSpec((DEC_ROWS, g * page), lambda b, j, *_: (0, 0)),
                pl.BlockSpec((DEC_ROWS, g * page), lambda b, j, *_: (0, 0)),
                pl.BlockSpec((DEC_ROWS, LANES), lambda b, j, *_: (0, 0))]
    args = [qprime, k_new, v_new, bias_far, bias_last, bias0]
    prefetch = [page_table.reshape(-1)]
    if masked:
        keys, tau, jmax, selnew = mask_args
        prefetch += [tau, jmax, selnew]
        in_specs += [pl.BlockSpec((None, 1, g * page), lambda b, j, *_: (b, 0, j))]
        args += [keys]
    in_specs += [pl.BlockSpec((None,) + k_cache.shape[1:], page_map(gi)) for gi in range(g)]
    in_specs += [pl.BlockSpec((None,) + v_cache.shape[1:], page_map(gi)) for gi in range(g)]
    args += [k_cache] * g + [v_cache] * g
    grid_spec = pltpu.PrefetchScalarGridSpec(
        num_scalar_prefetch=n_pref,
        grid=(db, n_steps),
        in_specs=in_specs,
        out_specs=pl.BlockSpec((None, DEC_ROWS, LANES), lambda b, j, *_: (b, 0, 0)),
        scratch_shapes=[pltpu.VMEM((DEC_ROWS, LANES), F32),
                        pltpu.VMEM((DEC_ROWS, LANES), F32),
                        pltpu.VMEM((DEC_ROWS, width), F32)])
    return pl.pallas_call(
        functools.partial(_paged_attn_kernel, n_pages_step=g, masked=masked, page=page,
                          k_feature_major=k_feature_major),
        grid_spec=grid_spec,
        out_shape=jax.ShapeDtypeStruct((db, DEC_ROWS, LANES), F32),
        compiler_params=_cparams(("parallel", "arbitrary")),
        name="paged_attn_masked" if masked else "paged_attn",
    )(*prefetch, *args)


def _paged_index_kernel(pt_ref, q_ref, w_ref, knew_ref, *refs, n_pages_step, page):
    k_refs = refs[:n_pages_step]
    out_ref = refs[n_pages_step]
    del pt_ref
    j = pl.program_id(1)
    last = pl.num_programs(1) - 1
    q = q_ref[...]
    w = w_ref[...]

    def score(kmat_t):
        d = jnp.dot(q, kmat_t.astype(BF16), preferred_element_type=F32)
        return jnp.sum(jnp.maximum(d, 0.0) * w[:, 0:1], axis=0, keepdims=True)

    @pl.when(j < last)
    def _():
        kcat = jnp.concatenate([r[...] for r in k_refs], axis=1)
        out_ref[...] = _sortable_key(score(kcat))

    @pl.when(j == last)
    def _():
        sc = score(knew_ref[...])
        lane = lax.broadcasted_iota(I32, (1, page), 1)
        sc = jnp.where(lane == 0, sc, -jnp.inf)
        pad = jnp.full((1, (n_pages_step - 1) * page), -jnp.inf, F32)
        full = jnp.concatenate([sc, pad], axis=1) if n_pages_step > 1 else sc
        out_ref[...] = _sortable_key(full)


def _paged_index_scores(qidx, wcol, k_new_tile, idx_cache_t, page_table):
    db = qidx.shape[0]
    n_pages = page_table.shape[1]
    page = idx_cache_t.shape[2]
    g = _pages_per_step(n_pages, INDEX_PAGES_PER_STEP)
    n_steps = n_pages // g

    def page_map(gi):
        return lambda b, j, pt: (pt[b * n_pages + jnp.minimum(j, n_steps - 1) * g + gi], 0, 0)

    grid_spec = pltpu.PrefetchScalarGridSpec(
        num_scalar_prefetch=1,
        grid=(db, n_steps + 1),
        in_specs=[pl.BlockSpec((None, IDX_HEADS, IDX_DIM), lambda b, j, pt: (b, 0, 0)),
                  pl.BlockSpec((None, IDX_HEADS, LANES), lambda b, j, pt: (b, 0, 0)),
                  pl.BlockSpec((None, IDX_DIM, page), lambda b, j, pt: (b, 0, 0))]
        + [pl.BlockSpec((None, IDX_DIM, page), page_map(gi)) for gi in range(g)],
        out_specs=pl.BlockSpec((None, 1, g * page), lambda b, j, pt: (b, 0, j)))
    return pl.pallas_call(
        functools.partial(_paged_index_kernel, n_pages_step=g, page=page),
        grid_spec=grid_spec,
        out_shape=jax.ShapeDtypeStruct((db, 1, (n_steps + 1) * g * page), I32),
        compiler_params=_cparams(("parallel", "arbitrary")),
        name="paged_index_scores",
    )(page_table.reshape(-1), qidx, wcol, k_new_tile, *([idx_cache_t] * g))


def _select_kernel(keys_ref, tau_ref, jmax_ref, *, k_sel, n_index_bits):
    keys = keys_ref[...]
    idx = lax.broadcasted_iota(I32, keys.shape, 1)
    shape = (keys.shape[0], 1)

    def count_fn(pred):
        return jnp.sum(jnp.where(pred(keys, idx), 1, 0), axis=1, keepdims=True)

    tau = _kth_largest(count_fn, k_sel, shape)
    need = k_sel - count_fn(lambda key, i: key > tau)
    jmax = _tie_cutoff(count_fn, tau, need, n_index_bits, shape)
    tau_ref[...] = jnp.broadcast_to(tau, tau_ref.shape)
    jmax_ref[...] = jnp.broadcast_to(jmax, jmax_ref.shape)


def _select_rows(keys2d, k_sel):
    rows, width = keys2d.shape
    n_index_bits = max(1, int(math.ceil(math.log2(width))))
    return pl.pallas_call(
        functools.partial(_select_kernel, k_sel=k_sel, n_index_bits=n_index_bits),
        out_shape=[jax.ShapeDtypeStruct((rows, LANES), I32), jax.ShapeDtypeStruct((rows, LANES), I32)],
        compiler_params=pltpu.CompilerParams(vmem_limit_bytes=VMEM_LIMIT),
        name="select_rows",
    )(keys2d)


def _even_tail_kernel(z_ref, conv_ref, s_ref, oa_ref, convw_ref, gp_ref, normg_ref, lamv_ref, subln_ref,
                      ya_ref, yb_ref, snew_ref, *, lam_init):
    z = z_ref[...]
    hw = B_HEADS * B_DK
    lv = lamv_ref[...]
    lam = (jnp.exp(jnp.sum(lv[0:1] * lv[1:2], axis=1, keepdims=True))
           - jnp.exp(jnp.sum(lv[2:3] * lv[3:4], axis=1, keepdims=True)) + lam_init)
    oa = oa_ref[...]
    for hg in range(A_HEADS):
        o = oa[2 * hg:2 * hg + 1] - lam * oa[2 * hg + 1:2 * hg + 2]
        y = o * lax.rsqrt(jnp.mean(o * o, axis=-1, keepdims=True) + EPS)
        y = (y * subln_ref[...]) * (1.0 - lam_init)
        gate = z[:, EVEN_OFF[3] + hg * A_DV:EVEN_OFF[3] + (hg + 1) * A_DV]
        ya_ref[:, hg * A_DV:(hg + 1) * A_DV] = (y * _silu(gate)).astype(BF16)
    x_new = z[:, EVEN_OFF[4]:EVEN_OFF[4] + B_QKV]
    conv = convw_ref[CONV_W - 1:CONV_W, :] * x_new
    cp = conv_ref[...]
    for i in range(CONV_W - 1):
        conv = conv + convw_ref[i:i + 1, :] * cp[i:i + 1, :]
    act = _silu(conv)
    ab = z[:, EVEN_OFF[6]:EVEN_OFF[6] + LANES]
    gp = gp_ref[...]
    row = lax.broadcasted_iota(I32, (B_DK, B_DV), 0)
    col = lax.broadcasted_iota(I32, (B_DK, B_DV), 1)
    eye = row == col
    for h in range(B_HEADS):
        q = act[:, h * B_DK:(h + 1) * B_DK]
        k = act[:, hw + h * B_DK:hw + (h + 1) * B_DK]
        v = act[:, 2 * hw + h * B_DV:2 * hw + (h + 1) * B_DV]
        q = q * lax.rsqrt(jnp.sum(q * q, axis=-1, keepdims=True) + EPS) * (B_DK ** -0.5)
        k = k * lax.rsqrt(jnp.sum(k * k, axis=-1, keepdims=True) + EPS)
        g = -jnp.exp(gp[0:1, h:h + 1]) * _softplus(ab[:, h:h + 1] + gp[1:2, h:h + 1])
        beta = jax.nn.sigmoid(ab[:, B_HEADS + h:B_HEADS + h + 1])
        eg = jnp.exp(g)
        s0 = s_ref[h]
        kcol = jnp.sum(jnp.where(eye, jnp.broadcast_to(k, (B_DK, B_DK)), 0.0), axis=1, keepdims=True)
        qcol = jnp.sum(jnp.where(eye, jnp.broadcast_to(q, (B_DK, B_DK)), 0.0), axis=1, keepdims=True)
        ks = jnp.sum(kcol * s0, axis=0, keepdims=True)
        qs = jnp.sum(qcol * s0, axis=0, keepdims=True)
        u = beta * (v - eg * ks)
        qk = jnp.sum(q * k, axis=1, keepdims=True)
        o = eg * qs + qk * u
        snew_ref[h] = eg * s0 + kcol * u
        y = o * lax.rsqrt(jnp.mean(o * o, axis=-1, keepdims=True) + EPS) * normg_ref[...]
        gate = z[:, EVEN_OFF[5] + h * B_DV:EVEN_OFF[5] + (h + 1) * B_DV]
        yb_ref[:, h * B_DV:(h + 1) * B_DV] = (y * _silu(gate)).astype(BF16)


def _even_tail(zf_s, conv_prev, s_prev, oa, conv_w, a_log, dt_bias, norm_g, lamv, subln, lam_init):
    db, npad = zf_s.shape
    gp = jnp.zeros((SUBLANES, LANES), F32).at[0, :B_HEADS].set(a_log).at[1, :B_HEADS].set(dt_bias)
    hw = B_HEADS * B_DV
    return pl.pallas_call(
        functools.partial(_even_tail_kernel, lam_init=lam_init),
        grid=(db,),
        in_specs=[pl.BlockSpec((None, 1, npad), lambda b: (b, 0, 0)),
                  pl.BlockSpec((None, CONV_W - 1, B_QKV), lambda b: (b, 0, 0)),
                  pl.BlockSpec((None, B_HEADS, B_DK, B_DV), lambda b: (b, 0, 0, 0)),
                  pl.BlockSpec((None, DEC_ROWS, LANES), lambda b: (b, 0, 0)),
                  pl.BlockSpec((CONV_W, B_QKV), lambda b: (0, 0)),
                  pl.BlockSpec((SUBLANES, LANES), lambda b: (0, 0)),
                  pl.BlockSpec((1, B_DV), lambda b: (0, 0)),
                  pl.BlockSpec((4, A_DH), lambda b: (0, 0)),
                  pl.BlockSpec((1, A_DV), lambda b: (0, 0))],
        out_specs=[pl.BlockSpec((None, 1, A_HEADS * A_DV), lambda b: (b, 0, 0)),
                   pl.BlockSpec((None, 1, hw), lambda b: (b, 0, 0)),
                   pl.BlockSpec((None, B_HEADS, B_DK, B_DV), lambda b: (b, 0, 0, 0))],
        out_shape=[jax.ShapeDtypeStruct((db, 1, A_HEADS * A_DV), BF16),
                   jax.ShapeDtypeStruct((db, 1, hw), BF16),
                   jax.ShapeDtypeStruct((db, B_HEADS, B_DK, B_DV), F32)],
        compiler_params=_cparams(("parallel",)),
        name="even_tail",
    )(zf_s.reshape(db, 1, npad), conv_prev, s_prev, oa, conv_w, gp, norm_g.reshape(1, B_DV), lamv,
      subln.reshape(1, A_DV))


def _odd_tail_kernel(z_ref, oc_ref, y_ref):
    z = z_ref[...]
    oc = oc_ref[...]
    for hg in range(C_HEADS):
        gate = z[:, ODD_OFF[3] + hg * C_DH:ODD_OFF[3] + (hg + 1) * C_DH]
        y_ref[:, hg * C_DH:(hg + 1) * C_DH] = (oc[hg:hg + 1] * _silu(gate)).astype(BF16)


def _odd_tail(zf_s, oc):
    db, npad = zf_s.shape
    return pl.pallas_call(
        _odd_tail_kernel,
        grid=(db,),
        in_specs=[pl.BlockSpec((None, 1, npad), lambda b: (b, 0, 0)),
                  pl.BlockSpec((None, DEC_ROWS, LANES), lambda b: (b, 0, 0))],
        out_specs=pl.BlockSpec((None, 1, C_HEADS * C_DH), lambda b: (b, 0, 0)),
        out_shape=jax.ShapeDtypeStruct((db, 1, C_HEADS * C_DH), BF16),
        compiler_params=_cparams(("parallel",)),
        name="odd_tail",
    )(zf_s.reshape(db, 1, npad), oc)


def _bias_by_distance(table):
    n = jnp.arange(FAR_DIST + 1)
    exact = N_BUCKETS // 2
    nf = jnp.maximum(n, 1).astype(F32)
    large = exact + (jnp.log(nf / exact) / math.log(MAX_DIST / exact) * (N_BUCKETS - exact)).astype(I32)
    bucket = jnp.where(n < exact, n, jnp.minimum(large, N_BUCKETS - 1))
    return table[bucket].astype(F32)


def _prompt_near_tiles(bd, group):
    assert FAR_DIST <= BLK
    heads = bd.shape[1]
    rel = ((bd - bd[FAR_DIST][None, :]) * LOG2E).T
    f = jnp.concatenate([jnp.full((heads, BLK - 1), NEG, F32), rel,
                         jnp.broadcast_to(rel[:, FAR_DIST:], (heads, 2 * BLK - 1 - FAR_DIST))], axis=1)
    period = 3 * BLK
    g = jnp.pad(f[:, ::-1], ((0, 0), (0, period - f.shape[1])))
    wrapped = jnp.tile(g, (1, BLK + 1))[:, :BLK * (period + 1)].reshape(heads, BLK, period + 1)
    strip = wrapped[:, ::-1, :2 * BLK]
    kinds = [None] * 4
    kinds[TILE_ZERO] = jnp.zeros((heads, BLK, BLK), F32)
    kinds[TILE_PREV] = strip[:, :, :BLK]
    kinds[TILE_DIAG] = strip[:, :, BLK:]
    kinds[TILE_MASKED] = jnp.full((heads, BLK, BLK), NEG, F32)
    tiles = jnp.stack(kinds, axis=1)
    n_kv = heads // group
    tiles = tiles.reshape(n_kv, group, 4, BLK, BLK)
    return jnp.transpose(tiles, (0, 2, 1, 3, 4)).reshape(n_kv, 4, group * BLK, BLK)


def _decode_bias(bd, row_heads, past, page, g):
    heads = jnp.asarray(row_heads, I32)
    far = jnp.broadcast_to(bd[FAR_DIST][heads][:, None], (len(row_heads), g * page))
    pos = past - g * page + jnp.arange(g * page)
    dist = jnp.minimum(past - pos, FAR_DIST)
    last = bd[dist][:, heads].T
    new = jnp.broadcast_to(bd[0][heads][:, None], (len(row_heads), LANES))
    return far.astype(F32), last.astype(F32), new.astype(F32)


def _pad_cols(w, mult):
    n = w.shape[1]
    return jnp.pad(w, ((0, 0), (0, _round_up(n, mult) - n)))


def kernel(x_prompt, x_sample, cache_a_k, cache_a_v, state_b_s, state_b_conv, cache_c_k, cache_c_v, cache_c_idx,
           page_table, meta, bias_table, final_norm, norm_e, w_in_e, w_out_e, lam_q1, lam_k1, lam_q2, lam_k2,
           subln_a, conv_b, a_log_b, dt_bias_b, norm_b, norm_o, w_in_o, w_out_o):
    nb, seq, d = x_prompt.shape
    n_meta = meta.shape[0]
    l = seq + n_meta
    lp = _round_up(l, CHUNK)
    nq_real = pl.cdiv(l, BLK)
    db = x_sample.shape[0]
    n_pages = page_table.shape[1]
    page = cache_a_k.shape[2]
    past = n_pages * page
    n_pool = cache_a_k.shape[1]
    assert x_sample.shape[1] == 1 and norm_e.shape[0] == 1 and norm_o.shape[0] == 1
    lam_init = 0.8 - 0.6 * math.exp(-0.3 * 0)

    tn_e, tn_o = 8 * LANES, 5 * LANES
    w_e = _pad_cols(w_in_e[0], tn_e).astype(BF16)
    w_o = _pad_cols(w_in_o[0], tn_o).astype(BF16)
    w_out_a = w_out_e[0][:A_HEADS * A_DV].astype(BF16)
    w_out_b = w_out_e[0][A_HEADS * A_DV:].astype(BF16)
    w_out_c = w_out_o[0].astype(BF16)
    tm = CHUNK
    tm_proj = 2 * CHUNK if (nb * lp) % (2 * CHUNK) == 0 else CHUNK
    dbp = _round_up(db, SUBLANES)

    bd = _bias_by_distance(bias_table)
    tiles_a = _prompt_near_tiles(bd, A_GROUP)
    tiles_c = tiles_a if (A_KV, A_GROUP) == (C_KV, C_GROUP) else _prompt_near_tiles(bd, C_GROUP)
    lamv = jnp.stack([lam_q1[0], lam_k1[0], lam_q2[0], lam_k2[0]]).astype(F32)

    hp = jnp.concatenate([jnp.broadcast_to(meta.astype(F32)[None], (nb, n_meta, d)), x_prompt], axis=1)
    hp = jnp.pad(hp, ((0, 0), (0, lp - l), (0, 0))).reshape(nb * lp, d)
    kv_tile = EVEN_OFF[1] // tn_e
    assert EVEN_OFF[3] <= (kv_tile + 1) * tn_e
    zf, zb = _norm_proj(hp, norm_e[0], w_e, tm_proj, tn_e, (kv_tile, kv_tile + 1))
    ya = _diff_attn_prompt(zf, zb, kv_tile * tn_e, tiles_a, lamv, subln_a[0], nb, lp, nq_real, lam_init)
    yb, pb_s = _gdn_prompt(zf, conv_b[0], a_log_b[0], dt_bias_b[0], norm_b[0], nb, lp, l)
    h1 = _out_proj([ya, yb], [w_out_a, w_out_b], hp, tm)
    z3 = zf.reshape(nb, lp, -1)
    pa_k = z3[:, :l, EVEN_OFF[1]:EVEN_OFF[2]].reshape(1, nb, l, A_KV, 2, A_DH)
    pa_v = z3[:, :l, EVEN_OFF[2]:EVEN_OFF[3]].reshape(1, nb, l, A_KV, A_DV)
    pb_conv = z3[:, l - (CONV_W - 1):l, EVEN_OFF[4]:EVEN_OFF[5]][None]

    zf1, zb1 = _norm_proj(h1, norm_o[0], w_o, tm_proj, tn_o, (0, w_o.shape[1] // tn_o))
    k_sel_p = min(TOPK_MAX, l // 4)
    yc = _sparse_attn_prompt(zf1, zb1, tiles_c, nb, lp, nq_real, k_sel_p)
    yp = _out_proj([yc], [w_out_c], h1, tm, final_gain=final_norm)
    y_prompt = yp.reshape(nb, lp, d)[:, n_meta:l]
    z13 = zf1.reshape(nb, lp, -1)
    pc_k = z13[:, :l, ODD_OFF[1]:ODD_OFF[2]].reshape(1, nb, l, C_KV, C_DH)
    pc_v = z13[:, :l, ODD_OFF[2]:ODD_OFF[3]].reshape(1, nb, l, C_KV, C_DH)
    pc_idx = z13[:, :l, ODD_OFF[5]:ODD_OFF[6]][None]

    hs = jnp.pad(x_sample.reshape(db, d), ((0, dbp - db), (0, 0)))
    zs, _ = _norm_proj(hs, norm_e[0], w_e, dbp, tn_e, (0, 1))
    zs = zs[:db]
    qa = zs[:, :EVEN_OFF[1]].reshape(db, A_KV, A_GROUP, 2, A_DH) * (A_DH ** -0.5)
    qprime = jnp.einsum('bhgcd,hi,cj->bhgcijd', qa, jnp.eye(A_KV, dtype=F32), jnp.eye(2, dtype=F32))
    qprime = qprime.reshape(db, DEC_ROWS, A_KV * 2 * A_DH).astype(BF16)
    k_new = zs[:, EVEN_OFF[1]:EVEN_OFF[2]].reshape(db, 1, -1)
    v_new = zs[:, EVEN_OFF[2]:EVEN_OFF[3]].reshape(db, 1, -1)
    g_dec = _pages_per_step(n_pages)
    rows_a = [r // 2 for r in range(DEC_ROWS)]
    bfar, blast, bnew = _decode_bias(bd, rows_a, past, page, g_dec)
    ak_t = jnp.transpose(cache_a_k[0], (0, 2, 3, 4, 1)).reshape(n_pool, A_KV * 2 * A_DH, page)
    av_r = cache_a_v[0].reshape(n_pool, page * A_KV, A_DV)
    oa = _paged_attention(qprime, k_new, v_new, bfar, blast, bnew, ak_t, av_r, page_table, page,
                          k_feature_major=True)
    ya_s, yb_s, sb_s = _even_tail(zs, state_b_conv[0], state_b_s[0], oa, conv_b[0], a_log_b[0], dt_bias_b[0],
                                  norm_b[0], lamv, subln_a[0], lam_init)
    hs_pad = lambda y: jnp.pad(y.reshape(db, -1), ((0, dbp - db), (0, 0)))
    hs1 = _out_proj([hs_pad(ya_s), hs_pad(yb_s)], [w_out_a, w_out_b], hs, dbp)
    sa_k = zs[:, EVEN_OFF[1]:EVEN_OFF[2]].reshape(1, db, 1, A_KV, 2, A_DH)
    sa_v = zs[:, EVEN_OFF[2]:EVEN_OFF[3]].reshape(1, db, 1, A_KV, A_DV)
    sb_conv = jnp.concatenate([state_b_conv[0][:, 1:], zs[:, None, EVEN_OFF[4]:EVEN_OFF[5]]], axis=1)[None]

    zs1, _ = _norm_proj(hs1, norm_o[0], w_o, dbp, tn_o, (0, 1))
    zs1 = zs1[:db]
    qidx = (zs1[:, ODD_OFF[4]:ODD_OFF[5]].reshape(db, IDX_HEADS, IDX_DIM) * (IDX_DIM ** -0.5)).astype(BF16)
    wcol = jnp.broadcast_to((zs1[:, ODD_OFF[6]:ODD_OFF[7]] * (IDX_HEADS ** -0.5))[:, :, None],
                            (db, IDX_HEADS, LANES))
    ki_new = zs1[:, ODD_OFF[5]:ODD_OFF[6]]
    ki_tile = jnp.zeros((db, IDX_DIM, page), F32).at[:, :, 0].set(ki_new)
    ci_t = jnp.transpose(cache_c_idx[0], (0, 2, 1))
    keys = _paged_index_scores(qidx, wcol, ki_tile, ci_t, page_table)
    k_sel_s = min(TOPK_MAX, (past + 1) // 4)
    tau, jmax = _select_rows(keys.reshape(db, -1), k_sel_s)
    tau, jmax = tau[:, 0], jmax[:, 0]
    key_new = keys[:, 0, past]
    selnew = ((key_new > tau) | ((key_new == tau) & (past <= jmax))).astype(I32)
    qc = zs1[:, :ODD_OFF[1]].reshape(db, C_KV, C_GROUP, C_DH) * (C_DH ** -0.5)
    qcp = jnp.einsum('bhgd,hi->bhgid', qc, jnp.eye(C_KV, dtype=F32)).reshape(db, C_HEADS, C_KV * C_DH)
    qcp = jnp.pad(qcp, ((0, 0), (0, DEC_ROWS - C_HEADS), (0, 0))).astype(BF16)
    kc_new = zs1[:, ODD_OFF[1]:ODD_OFF[2]].reshape(db, 1, -1)
    vc_new = zs1[:, ODD_OFF[2]:ODD_OFF[3]].reshape(db, 1, -1)
    rows_c = [r if r < C_HEADS else 0 for r in range(DEC_ROWS)]
    cfar, clast, cnew = _decode_bias(bd, rows_c, past, page, g_dec)
    ck_r = cache_c_k[0].reshape(n_pool, page * C_KV, C_DH)
    cv_r = cache_c_v[0].reshape(n_pool, page * C_KV, C_DH)
    oc = _paged_attention(qcp, kc_new, vc_new, cfar, clast, cnew, ck_r, cv_r, page_table, page,
                          k_feature_major=False, mask_args=(keys, tau, jmax, selnew))
    yc_s = _odd_tail(zs1, oc)
    ys = _out_proj([hs_pad(yc_s)], [w_out_c], hs1, dbp, final_gain=final_norm)
    y_sample = ys[:db].reshape(db, 1, d)
    sc_k = zs1[:, ODD_OFF[1]:ODD_OFF[2]].reshape(1, db, 1, C_KV, C_DH)
    sc_v = zs1[:, ODD_OFF[2]:ODD_OFF[3]].reshape(1, db, 1, C_KV, C_DH)
    sc_idx = zs1[:, None, ODD_OFF[5]:ODD_OFF[6]][None]

    return (y_prompt, y_sample, pa_k, pa_v, pb_s[None], pb_conv, pc_k, pc_v, pc_idx,
            sa_k, sa_v, sb_s[None], sb_conv, sc_k, sc_v, sc_idx)
```

```python
import functools
import math

import jax
import jax.numpy as jnp
import numpy as np
from jax import lax
from jax.experimental import pallas as pl
from jax.experimental.pallas import tpu as pltpu

F32 = jnp.float32
BF16 = jnp.bfloat16
I32 = jnp.int32

EPS = 1e-6
N_BUCKETS = 32
MAX_DIST = 128
FAR_DIST = MAX_DIST

A_HEADS, A_KV, A_GROUP, A_DH, A_DV = 8, 2, 4, 64, 128
B_HEADS, B_DK, B_DV, CONV_W = 4, 128, 128, 4
C_HEADS, C_KV, C_GROUP, C_DH = 8, 2, 4, 128
IDX_HEADS, IDX_DIM, TOPK_MAX = 8, 64, 256
B_QKV = 2 * B_HEADS * B_DK + B_HEADS * B_DV

EVEN_COLS = (A_HEADS * 2 * A_DH, A_KV * 2 * A_DH, A_KV * A_DV, A_HEADS * A_DV, B_QKV, B_HEADS * B_DV, B_HEADS, B_HEADS)
ODD_COLS = (C_HEADS * C_DH, C_KV * C_DH, C_KV * C_DH, C_HEADS * C_DH, IDX_HEADS * IDX_DIM, IDX_DIM, IDX_HEADS)
EVEN_OFF = tuple(int(v) for v in np.cumsum((0,) + EVEN_COLS))
ODD_OFF = tuple(int(v) for v in np.cumsum((0,) + ODD_COLS))

LANES = 128
SUBLANES = 8
BLK = 128
CHUNK = 512
SUBS = CHUNK // BLK
NEG = -1e30
INT_MIN = -2 ** 31
INT_MAX = 2 ** 31 - 1
LOG2E = math.log2(math.e)
VMEM_LIMIT = 56 * 1024 * 1024
HI = lax.Precision.HIGHEST

NT = (((1,), (1,)), ((), ()))


def _round_up(x, m):
    return (x + m - 1) // m * m


def _cparams(sem):
    return pltpu.CompilerParams(dimension_semantics=sem, vmem_limit_bytes=VMEM_LIMIT)


def _silu(x):
    return x * jax.nn.sigmoid(x)


def _sortable_key(score):
    score = jnp.where(score == 0.0, 0.0, score)
    bits = pltpu.bitcast(score, I32)
    return bits ^ ((bits >> 31) & jnp.int32(0x7FFFFFFF))


def _lane_tile(x, n):
    return jnp.concatenate([x] * n, axis=1)


def _proj_kernel(x_ref, g_ref, w_ref, of_ref, ob_ref, xn_ref, *, bf16_tiles):
    j = pl.program_id(1)

    @pl.when(j == 0)
    def _():
        xf = x_ref[...]
        y = xf * lax.rsqrt(jnp.mean(xf * xf, axis=-1, keepdims=True) + EPS)
        xn_ref[...] = (y * g_ref[...]).astype(BF16)

    acc = jnp.dot(xn_ref[...], w_ref[...], preferred_element_type=F32)
    of_ref[...] = acc

    @pl.when((j >= bf16_tiles[0]) & (j < bf16_tiles[1]))
    def _():
        ob_ref[...] = acc.astype(BF16)


def _norm_proj(x, gain, w_bf16, tm, tn, bf16_tiles):
    m, d = x.shape
    n = w_bf16.shape[1]
    t0, t1 = bf16_tiles
    return pl.pallas_call(
        functools.partial(_proj_kernel, bf16_tiles=bf16_tiles),
        grid=(m // tm, n // tn),
        in_specs=[pl.BlockSpec((tm, d), lambda i, j: (i, 0)),
                  pl.BlockSpec((1, d), lambda i, j: (0, 0)),
                  pl.BlockSpec((d, tn), lambda i, j: (0, j))],
        out_specs=[pl.BlockSpec((tm, tn), lambda i, j: (i, j)),
                   pl.BlockSpec((tm, tn), lambda i, j: (i, jnp.clip(j - t0, 0, t1 - t0 - 1)))],
        out_shape=[jax.ShapeDtypeStruct((m, n), F32), jax.ShapeDtypeStruct((m, (t1 - t0) * tn), BF16)],
        scratch_shapes=[pltpu.VMEM((tm, d), BF16)],
        compiler_params=_cparams(("parallel", "arbitrary")),
        name="norm_proj",
    )(x, gain.reshape(1, d), w_bf16)


def _outproj_kernel(*refs, n_lhs, final_norm):
    ys = refs[:n_lhs]
    ws = refs[n_lhs:2 * n_lhs]
    h_ref = refs[2 * n_lhs]
    pos = 2 * n_lhs + 1
    fn_ref = refs[pos] if final_norm else None
    out_ref = refs[-1]
    acc = h_ref[...]
    for y_ref, w_ref in zip(ys, ws):
        acc = acc + jnp.dot(y_ref[...], w_ref[...], preferred_element_type=F32)
    if final_norm:
        y = acc * lax.rsqrt(jnp.mean(acc * acc, axis=-1, keepdims=True) + EPS)
        acc = y * fn_ref[...]
    out_ref[...] = acc


def _out_proj(ys, ws, h, tm, final_gain=None):
    m, n = h.shape
    n_lhs = len(ys)
    in_specs = [pl.BlockSpec((tm, y.shape[1]), lambda i: (i, 0)) for y in ys]
    in_specs += [pl.BlockSpec(w.shape, lambda i: (0, 0)) for w in ws]
    in_specs += [pl.BlockSpec((tm, n), lambda i: (i, 0))]
    args = list(ys) + list(ws) + [h]
    if final_gain is not None:
        in_specs += [pl.BlockSpec((1, n), lambda i: (0, 0))]
        args += [final_gain.reshape(1, n)]
    return pl.pallas_call(
        functools.partial(_outproj_kernel, n_lhs=n_lhs, final_norm=final_gain is not None),
        grid=(m // tm,),
        in_specs=in_specs,
        out_specs=pl.BlockSpec((tm, n), lambda i: (i, 0)),
        out_shape=jax.ShapeDtypeStruct((m, n), F32),
        compiler_params=_cparams(("parallel",)),
        name="out_proj",
    )(*args)


def _flash_steps(ss, v1s, m_refs):
    each = range(len(ss))
    m_old = [m_refs[i][0][m_refs[i][2]] for i in each]
    m_new = [jnp.maximum(m_old[i], jnp.max(ss[i], axis=1, keepdims=True)) for i in each]
    alpha = [jnp.exp2(m_old[i] - m_new[i]) for i in each]
    p = [jnp.exp2(ss[i] - _lane_tile(m_new[i], ss[i].shape[1] // LANES)).astype(BF16) for i in each]
    pv = [jnp.dot(p[i], v1s[i], preferred_element_type=F32) for i in each]
    for i in each:
        m_ref, acc_ref, idx = m_refs[i]
        acc_ref[idx] = _lane_tile(alpha[i], 2) * acc_ref[idx] + pv[i]
        m_ref[idx] = m_new[i]


def _far_chunk_loop(n_far, run_chunks):
    def quad(i, carry):
        run_chunks([4 * i, 4 * i + 1, 4 * i + 2, 4 * i + 3])
        return carry

    n_quad = lax.shift_right_logical(n_far, 2)
    lax.fori_loop(0, n_quad, quad, 0)

    @pl.when(jnp.bitwise_and(n_far, 2) == 2)
    def _():
        run_chunks([4 * n_quad, 4 * n_quad + 1])

    @pl.when(jnp.bitwise_and(n_far, 1) == 1)
    def _():
        run_chunks([n_far - 1])


TILE_ZERO, TILE_PREV, TILE_DIAG, TILE_MASKED = range(4)


def _near_bias_kind(sb, r):
    rel = sb - SUBS - r
    return jnp.where(rel == 0, TILE_DIAG, jnp.where(rel == -1, TILE_PREV, jnp.where(rel < -1, TILE_ZERO, TILE_MASKED)))


def _diff_attn_kernel(q_ref, k_ref, v_ref, gate_ref, tiles_ref, lamv_ref, subln_ref, out_ref,
                      qm_ref, m_ref, acc_ref, *, lam_init, nq_real):
    qi = pl.program_id(2)

    @pl.when(qi >= nq_real)
    def _():
        out_ref[...] = jnp.zeros(out_ref.shape, out_ref.dtype)

    @pl.when(qi < nq_real)
    def _():
        lane = lax.broadcasted_iota(I32, (BLK, 2 * A_DH), 1)
        for g in range(A_GROUP):
            qg = q_ref[:, g * 2 * A_DH:(g + 1) * 2 * A_DH] * (A_DH ** -0.5 * LOG2E)
            for c in range(2):
                keep = (lane < A_DH) if c == 0 else (lane >= A_DH)
                qm_ref[c, g * BLK:(g + 1) * BLK, :] = jnp.where(keep, qg, 0.0).astype(BF16)
        m_ref[...] = jnp.full(m_ref.shape, NEG, F32)
        acc_ref[...] = jnp.zeros(acc_ref.shape, F32)
        r = jnp.bitwise_and(qi, SUBS - 1)
        jd = lax.shift_right_logical(qi, int(math.log2(SUBS)))
        ones = jnp.ones((CHUNK, A_DV), BF16)

        def run_chunks(js, first_sbs=None):
            starts = [pl.multiple_of(j * CHUNK, CHUNK) for j in js]
            k = jnp.concatenate([k_ref[pl.ds(st, CHUNK), :] for st in starts], axis=0)
            v1 = jnp.concatenate(
                [jnp.concatenate([v_ref[pl.ds(st, CHUNK), :], ones], axis=1) for st in starts], axis=0)
            ss = [lax.dot_general(qm_ref[c], k, NT, preferred_element_type=F32) for c in range(2)]
            if first_sbs is not None:
                bias = jnp.concatenate(
                    [tiles_ref[_near_bias_kind(sb + i, r)] for sb in first_sbs for i in range(SUBS)], axis=1)
                ss = [s + bias for s in ss]
            _flash_steps(ss, [v1, v1], [(m_ref, acc_ref, c) for c in range(2)])

        def near_pair(rr):
            n_diag = (rr + 1) * BLK
            st_prev = pl.multiple_of((jd - 1) * CHUNK, CHUNK)
            st_diag = pl.multiple_of(jd * CHUNK, CHUNK)
            k = jnp.concatenate([k_ref[pl.ds(st_prev, CHUNK), :], k_ref[pl.ds(st_diag, n_diag), :]], axis=0)
            v = jnp.concatenate([v_ref[pl.ds(st_prev, CHUNK), :], v_ref[pl.ds(st_diag, n_diag), :]], axis=0)
            v1 = jnp.concatenate([v, jnp.ones((CHUNK + n_diag, A_DV), BF16)], axis=1)
            bias = jnp.concatenate([tiles_ref[TILE_PREV], tiles_ref[TILE_DIAG]], axis=1)
            first = CHUNK + (rr - 1) * BLK
            ss = []
            for c in range(2):
                s = lax.dot_general(qm_ref[c], k, NT, preferred_element_type=F32)
                ss.append(jnp.concatenate([s[:, :first], s[:, first:] + bias], axis=1))
            _flash_steps(ss, [v1, v1], [(m_ref, acc_ref, c) for c in range(2)])

        _far_chunk_loop(jnp.maximum(jd - 1, 0), run_chunks)

        for rr in range(SUBS):
            @pl.when((jd >= 1) & (r == rr))
            def _(rr=rr):
                near_pair(rr)

        @pl.when(jd == 0)
        def _():
            run_chunks([jd], [SUBS])

        lv = lamv_ref[...]
        lam = (jnp.exp(jnp.sum(lv[0:1] * lv[1:2], axis=1, keepdims=True))
               - jnp.exp(jnp.sum(lv[2:3] * lv[3:4], axis=1, keepdims=True)) + lam_init)
        a0 = acc_ref[0]
        a1 = acc_ref[1]
        o = a0[:, :A_DV] / a0[:, A_DV:] - lam * (a1[:, :A_DV] / a1[:, A_DV:])
        y = o * lax.rsqrt(jnp.mean(o * o, axis=-1, keepdims=True) + EPS)
        y = (y * subln_ref[...]) * (1.0 - lam_init)
        for g in range(A_GROUP):
            gate = gate_ref[:, g * A_DV:(g + 1) * A_DV]
            out_ref[:, g * A_DV:(g + 1) * A_DV] = (y[g * BLK:(g + 1) * BLK] * _silu(gate)).astype(BF16)


def _diff_attn_prompt(zf, zb, zb_col0, near_tiles, lamv, subln, nb, lp, nq_real, lam_init):
    nq = lp // BLK
    qw = A_GROUP * 2 * A_DH
    k_blk0 = (EVEN_OFF[1] - zb_col0) // (2 * A_DH)
    v_blk0 = (EVEN_OFF[2] - zb_col0) // A_DV
    g_blk0 = EVEN_OFF[3] // (A_GROUP * A_DV)
    rows = A_GROUP * BLK
    return pl.pallas_call(
        functools.partial(_diff_attn_kernel, lam_init=lam_init, nq_real=nq_real),
        grid=(nb, A_KV, nq),
        in_specs=[pl.BlockSpec((BLK, qw), lambda b, h, i: (b * nq + i, h)),
                  pl.BlockSpec((lp, 2 * A_DH), lambda b, h, i: (b, k_blk0 + h)),
                  pl.BlockSpec((lp, A_DV), lambda b, h, i: (b, v_blk0 + h)),
                  pl.BlockSpec((BLK, A_GROUP * A_DV), lambda b, h, i: (b * nq + i, g_blk0 + h)),
                  pl.BlockSpec((None, 4, rows, BLK), lambda b, h, i: (h, 0, 0, 0)),
                  pl.BlockSpec((4, A_DH), lambda b, h, i: (0, 0)),
                  pl.BlockSpec((1, A_DV), lambda b, h, i: (0, 0))],
        out_specs=pl.BlockSpec((BLK, A_GROUP * A_DV), lambda b, h, i: (b * nq + i, h)),
        out_shape=jax.ShapeDtypeStruct((nb * lp, A_HEADS * A_DV), BF16),
        scratch_shapes=[pltpu.VMEM((2, rows, 2 * A_DH), BF16),
                        pltpu.VMEM((2, rows, LANES), F32),
                        pltpu.VMEM((2, rows, 2 * A_DV), F32)],
        compiler_params=_cparams(("parallel", "parallel", "arbitrary")),
        name="diff_attn_prompt",
    )(zf, zb, zb, zf, near_tiles, lamv, subln.reshape(1, A_DV))


def _softplus(x):
    return jnp.maximum(x, 0.0) + jnp.log1p(jnp.exp(-jnp.abs(x)))


def _split_bf16(a):
    hi = a.astype(BF16)
    return hi, (a - hi.astype(F32)).astype(BF16)


def _dot3(a, b):
    (ah, al), (bh, bl) = a, b
    return (jnp.dot(ah, bh, preferred_element_type=F32)
            + (jnp.dot(ah, bl, preferred_element_type=F32) + jnp.dot(al, bh, preferred_element_type=F32)))


def _gdn_chunk_math(s0, q, k, v, gb, betab, row_ge, row_gt):
    n = len(q)
    c = q[0].shape[0]
    each = range(n)
    dot = functools.partial(jnp.dot, preferred_element_type=F32)
    dot_nt = functools.partial(lax.dot_general, dimension_numbers=NT, preferred_element_type=F32)
    ltri = jnp.where(row_ge, 1.0, 0.0).astype(F32)
    gcum = [jnp.dot(ltri, gb[i], precision=HI, preferred_element_type=F32) for i in each]
    decay = [jnp.where(row_ge, jnp.exp(jnp.where(row_ge, gcum[i] - gcum[i].T, 0.0)), 0.0) for i in each]
    kb = [k[i].astype(BF16) for i in each]
    qb = [q[i].astype(BF16) for i in each]
    s0b = [s0[i].astype(BF16) for i in each]
    kk = [dot_nt(kb[i], kb[i]) for i in each]
    x = [-jnp.where(row_gt, betab[i] * decay[i] * kk[i], 0.0) for i in each]
    tm = list(x)
    ps = [_split_bf16(x[i]) for i in each]
    for _ in range(int(math.log2(c)) - 1):
        p = [_dot3(ps[i], ps[i]) for i in each]
        ps = [_split_bf16(p[i]) for i in each]
        tms = [_split_bf16(tm[i]) for i in each]
        tm = [tm[i] + p[i] + _dot3(tms[i], ps[i]) for i in each]
    eg = [jnp.exp(gcum[i]) for i in each]
    ks = [dot(kb[i], s0b[i]) for i in each]
    rhs = [betab[i] * (v[i] - eg[i] * ks[i]) for i in each]
    u = [rhs[i] + dot(tm[i].astype(BF16), rhs[i].astype(BF16)) for i in each]
    ub = [u[i].astype(BF16) for i in each]
    qk = [dot_nt(qb[i], kb[i]) * decay[i] for i in each]
    o = [eg[i] * dot(qb[i], s0b[i]) + dot(qk[i].astype(BF16), ub[i]) for i in each]
    glast = [gcum[i][c - 1:c, :] for i in each]
    kd = [k[i] * jnp.exp(glast[i] - gcum[i]) for i in each]
    s_new = [jnp.exp(glast[i]) * s0[i] + dot(kd[i].T.astype(BF16), ub[i]) for i in each]
    return s_new, o


def _gdn_prompt_kernel(xq_ref, xk_ref, xv_ref, gate_ref, ab_ref, convw_ref, gp_ref, normg_ref,
                       y_ref, sfin_ref, xbuf_ref, s_ref, *, seq_len):
    ci = pl.program_id(0)
    hw = B_HEADS * B_DK
    nb = xq_ref.shape[0]

    @pl.when(ci == 0)
    def _():
        xbuf_ref[:, 0:SUBLANES, :] = jnp.zeros((nb, SUBLANES, B_QKV), F32)
        s_ref[...] = jnp.zeros(s_ref.shape, F32)

    row = lax.broadcasted_iota(I32, (BLK, BLK), 0)
    col = lax.broadcasted_iota(I32, (BLK, BLK), 1)
    row_ge = row >= col
    row_gt = row > col
    valid = (ci * BLK + row) < seq_len
    gp = gp_ref[...]
    chains = [(b, h) for b in range(nb) for h in range(B_HEADS)]
    qs, ks, vs, gbs, betabs = [], [], [], [], []
    for b in range(nb):
        xbuf_ref[b, SUBLANES:SUBLANES + BLK, 0:hw] = xq_ref[b]
        xbuf_ref[b, SUBLANES:SUBLANES + BLK, hw:2 * hw] = xk_ref[b]
        xbuf_ref[b, SUBLANES:SUBLANES + BLK, 2 * hw:3 * hw] = xv_ref[b]
        conv = jnp.zeros((BLK, B_QKV), F32)
        for i in range(CONV_W):
            conv = conv + convw_ref[i:i + 1, :] * xbuf_ref[b, pl.ds(SUBLANES - (CONV_W - 1) + i, BLK), :]
        tail = xbuf_ref[b, BLK:BLK + SUBLANES, :]
        xbuf_ref[b, 0:SUBLANES, :] = tail
        act = _silu(conv)
        ab = ab_ref[b]
        for h in range(B_HEADS):
            q = act[:, h * B_DK:(h + 1) * B_DK]
            k = act[:, hw + h * B_DK:hw + (h + 1) * B_DK]
            v = act[:, 2 * hw + h * B_DV:2 * hw + (h + 1) * B_DV]
            q = q * lax.rsqrt(jnp.sum(q * q, axis=-1, keepdims=True) + EPS) * (B_DK ** -0.5)
            k = k * lax.rsqrt(jnp.sum(k * k, axis=-1, keepdims=True) + EPS)
            a_raw = jnp.broadcast_to(ab[:, h:h + 1], (BLK, BLK))
            b_raw = jnp.broadcast_to(ab[:, B_HEADS + h:B_HEADS + h + 1], (BLK, BLK))
            a_log = gp[0:1, h:h + 1]
            dt_b = gp[1:2, h:h + 1]
            qs.append(q)
            ks.append(k)
            vs.append(v)
            gbs.append(jnp.where(valid, -jnp.exp(a_log) * _softplus(a_raw + dt_b), 0.0))
            betabs.append(jnp.where(valid, jax.nn.sigmoid(b_raw), 0.0))
    s_new, outs = _gdn_chunk_math([s_ref[b, h] for b, h in chains], qs, ks, vs, gbs, betabs, row_ge, row_gt)
    for (b, h), s_bh, o in zip(chains, s_new, outs):
        s_ref[b, h] = s_bh
        y = o * lax.rsqrt(jnp.mean(o * o, axis=-1, keepdims=True) + EPS) * normg_ref[...]
        gate = gate_ref[b, :, h * B_DV:(h + 1) * B_DV]
        y_ref[b, :, h * B_DV:(h + 1) * B_DV] = (y * _silu(gate)).astype(BF16)

    @pl.when(ci == pl.num_programs(0) - 1)
    def _():
        sfin_ref[...] = s_ref[...]


def _gdn_prompt(zf, conv_w, a_log, dt_bias, norm_g, nb, lp, seq_len):
    nc = lp // BLK
    hw = B_HEADS * B_DK
    c0 = EVEN_OFF[4] // hw
    g0 = EVEN_OFF[5] // hw
    ab0 = EVEN_OFF[6] // LANES
    gp = jnp.zeros((SUBLANES, LANES), F32).at[0, :B_HEADS].set(a_log).at[1, :B_HEADS].set(dt_bias)
    z3 = zf.reshape(nb, lp, zf.shape[1])
    y, s_fin = pl.pallas_call(
        functools.partial(_gdn_prompt_kernel, seq_len=seq_len),
        grid=(nc,),
        in_specs=[pl.BlockSpec((nb, BLK, hw), lambda i: (0, i, c0)),
                  pl.BlockSpec((nb, BLK, hw), lambda i: (0, i, c0 + 1)),
                  pl.BlockSpec((nb, BLK, hw), lambda i: (0, i, c0 + 2)),
                  pl.BlockSpec((nb, BLK, hw), lambda i: (0, i, g0)),
                  pl.BlockSpec((nb, BLK, LANES), lambda i: (0, i, ab0)),
                  pl.BlockSpec((CONV_W, B_QKV), lambda i: (0, 0)),
                  pl.BlockSpec((SUBLANES, LANES), lambda i: (0, 0)),
                  pl.BlockSpec((1, B_DV), lambda i: (0, 0))],
        out_specs=[pl.BlockSpec((nb, BLK, hw), lambda i: (0, i, 0)),
                   pl.BlockSpec((nb, B_HEADS, B_DK, B_DV), lambda i: (0, 0, 0, 0))],
        out_shape=[jax.ShapeDtypeStruct((nb, lp, hw), BF16),
                   jax.ShapeDtypeStruct((nb, B_HEADS, B_DK, B_DV), F32)],
        scratch_shapes=[pltpu.VMEM((nb, BLK + SUBLANES, B_QKV), F32),
                        pltpu.VMEM((nb, B_HEADS, B_DK, B_DV), F32)],
        compiler_params=_cparams(("arbitrary",)),
        name="gdn_prompt",
    )(z3, z3, z3, z3, z3, conv_w, gp, norm_g.reshape(1, B_DV))
    return y.reshape(nb * lp, hw), s_fin


def _kth_largest(count_fn, k_sel, shape):
    def step(i, tau):
        cand = tau + lax.shift_left(jnp.int32(1), 31 - i)
        cnt = count_fn(lambda key, idx: key >= cand)
        return jnp.where(cnt >= k_sel, cand, tau)

    return lax.fori_loop(0, 32, step, jnp.full(shape, INT_MIN, I32))


UNCHECKED_BITS = 24
BITS_PER_CHECK = 2


def _kth_separator(count_fn, k_sel, n_keys):
    assert (32 - UNCHECKED_BITS) % BITS_PER_CHECK == 0

    def pending(cnt_tau):
        return jnp.max(jnp.where(cnt_tau > k_sel, 1.0, 0.0)) > 0.5

    def bit_step(bit, tau, cnt_tau):
        cand = tau + lax.shift_left(jnp.int32(1), 31 - bit)
        cnt = count_fn(lambda key, idx: key >= cand)
        take = cnt >= k_sel
        return jnp.where(take, cand, tau), jnp.where(take, cnt, cnt_tau)

    tau, cnt_tau = lax.fori_loop(0, UNCHECKED_BITS, lambda i, c: bit_step(i, *c),
                                 (jnp.full(n_keys.shape, INT_MIN, I32), n_keys))

    def group(carry):
        bit, tau, cnt_tau, _ = carry
        for b in range(BITS_PER_CHECK):
            tau, cnt_tau = bit_step(bit + b, tau, cnt_tau)
        return bit + BITS_PER_CHECK, tau, cnt_tau, pending(cnt_tau)

    init = (jnp.int32(UNCHECKED_BITS), tau, cnt_tau, pending(cnt_tau))
    out = lax.while_loop(lambda c: (c[0] < 32) & c[3], group, init)
    return out[1], out[2]


def _tie_cutoff(count_fn, tau, need, n_index_bits, shape):
    def step(i, jm):
        cand = jm | lax.shift_left(jnp.int32(1), n_index_bits - 1 - i)
        cnt = count_fn(lambda key, idx: (key == tau) & (idx < cand))
        return jnp.where(cnt < need, cand, jm)

    return lax.fori_loop(0, n_index_bits, step, jnp.zeros(shape, I32))


def _sparse_attn_kernel(qi_ref, kw_ref, kidx_ref, qc_ref, kc_ref, vc_ref, g0_ref, g1_ref, tiles_ref, out_ref,
                        keys_ref, qm_ref, qh_ref, tau_ref, jmax_ref, m_ref, acc_ref,
                        *, k_sel, n_index_bits, nq_real):
    qb = pl.program_id(1)

    @pl.when(qb >= nq_real)
    def _():
        out_ref[...] = jnp.zeros(out_ref.shape, out_ref.dtype)

    @pl.when(qb < nq_real)
    def _():
        lane = lax.broadcasted_iota(I32, (BLK, LANES), 1)
        key_pos = lax.broadcasted_iota(I32, (CHUNK, BLK), 0)
        qry_pos = lax.broadcasted_iota(I32, (CHUNK, BLK), 1)
        r = jnp.bitwise_and(qb, SUBS - 1)
        jd = lax.shift_right_logical(qb, int(math.log2(SUBS)))

        for p in range(IDX_HEADS // 2):
            pair = qi_ref[:, p * LANES:(p + 1) * LANES] * (IDX_DIM ** -0.5)
            swapped = pltpu.roll(pair, IDX_DIM, 1)
            for e, src in enumerate((pair, swapped)):
                h = 2 * p + e
                qm_ref[h * BLK:(h + 1) * BLK, :] = jnp.where(lane < IDX_DIM, src, 0.0).astype(BF16)
        w_rows = kw_ref[...].T * (IDX_HEADS ** -0.5)

        def scores_t(j):
            start = pl.multiple_of(j * CHUNK, CHUNK)
            d = lax.dot_general(kidx_ref[pl.ds(start, CHUNK), :], qm_ref[...], NT, preferred_element_type=F32)
            sc = jnp.zeros((CHUNK, BLK), F32)
            for h in range(IDX_HEADS):
                sc = sc + w_rows[IDX_DIM + h:IDX_DIM + h + 1, :] * jnp.maximum(d[:, h * BLK:(h + 1) * BLK], 0.0)
            return sc

        def score_quad(i, carry):
            for j in (4 * i, 4 * i + 1, 4 * i + 2, 4 * i + 3):
                keys_ref[j] = _sortable_key(scores_t(j))
            return carry

        n_quad = lax.shift_right_logical(jd, 2)
        lax.fori_loop(0, n_quad, score_quad, 0)

        @pl.when(jnp.bitwise_and(jd, 2) == 2)
        def _():
            for j in (4 * n_quad, 4 * n_quad + 1):
                keys_ref[j] = _sortable_key(scores_t(j))

        @pl.when(jnp.bitwise_and(jd, 1) == 1)
        def _():
            keys_ref[jd - 1] = _sortable_key(scores_t(jd - 1))

        admissible = (jd * CHUNK + key_pos) <= (qb * BLK + qry_pos)
        keys_ref[jd] = _sortable_key(jnp.where(admissible, scores_t(jd), -jnp.inf))

        def count_fn(pred):
            part = SUBLANES * SUBLANES

            def body(j, acc):
                hit = jnp.where(pred(keys_ref[j], j * CHUNK + key_pos), 1.0, 0.0)
                return acc + jnp.sum(hit.reshape(CHUNK // part, part, BLK), axis=0)
            acc = lax.fori_loop(0, jd + 1, body, jnp.zeros((part, BLK), F32))
            return jnp.sum(acc, axis=0, keepdims=True).astype(I32)

        n_keys = qb * BLK + lax.broadcasted_iota(I32, (1, BLK), 1) + 1
        tau, n_ge = _kth_separator(count_fn, k_sel, n_keys)
        tau_ref[...] = tau
        jmax_ref[...] = jnp.full((1, BLK), INT_MAX, I32)
        excess = jnp.max(jnp.where(n_ge > k_sel, 1.0, 0.0)) > 0.5

        @pl.when(excess)
        def _():
            need = k_sel - count_fn(lambda key, idx: key > tau)
            jmax_ref[...] = _tie_cutoff(count_fn, tau, need, n_index_bits, (1, BLK))

        jmax = jmax_ref[...]

        for h in range(C_KV):
            for g in range(C_GROUP):
                col = (h * C_GROUP + g) * C_DH
                qh_ref[h, g * BLK:(g + 1) * BLK, :] = (
                    qc_ref[:, col:col + C_DH] * (C_DH ** -0.5 * LOG2E)).astype(BF16)
        m_ref[...] = jnp.full(m_ref.shape, NEG, F32)
        acc_ref[...] = jnp.zeros(acc_ref.shape, F32)
        ones = jnp.ones((CHUNK, C_DH), BF16)

        def run_chunks(js, first_sbs=None):
            starts = [pl.multiple_of(j * CHUNK, CHUNK) for j in js]
            masks = []
            for j in js:
                key = keys_ref[j]
                tie_ok = jnp.where((j * CHUNK + key_pos) <= jmax, 0.0, NEG)
                sel_t = jnp.where(key > tau, 0.0, jnp.where(key == tau, tie_ok, NEG))
                masks += [sel_t[i * BLK:(i + 1) * BLK, :].T for i in range(SUBS)]
            selneg = jnp.concatenate(masks, axis=1)
            selneg = jnp.concatenate([selneg] * C_GROUP, axis=0)
            ss, v1s = [], []
            for h in range(C_KV):
                cols = slice(h * C_DH, (h + 1) * C_DH)
                k = jnp.concatenate([kc_ref[pl.ds(st, CHUNK), cols] for st in starts], axis=0)
                v1s.append(jnp.concatenate(
                    [jnp.concatenate([vc_ref[pl.ds(st, CHUNK), cols], ones], axis=1) for st in starts], axis=0))
                s = lax.dot_general(qh_ref[h], k, NT, preferred_element_type=F32) + selneg
                if first_sbs is not None:
                    s = s + jnp.concatenate(
                        [tiles_ref[h, _near_bias_kind(sb + i, r)] for sb in first_sbs for i in range(SUBS)], axis=1)
                ss.append(s)
            _flash_steps(ss, v1s, [(m_ref, acc_ref, h) for h in range(C_KV)])

        def mask_blocks(j, n_blocks):
            key = keys_ref[j][:n_blocks * BLK]
            tie_ok = jnp.where((j * CHUNK + key_pos[:n_blocks * BLK]) <= jmax, 0.0, NEG)
            sel_t = jnp.where(key > tau, 0.0, jnp.where(key == tau, tie_ok, NEG))
            return [sel_t[i * BLK:(i + 1) * BLK, :].T for i in range(n_blocks)]

        def near_pair(rr):
            n_diag = (rr + 1) * BLK
            st_prev = pl.multiple_of((jd - 1) * CHUNK, CHUNK)
            st_diag = pl.multiple_of(jd * CHUNK, CHUNK)
            selneg = jnp.concatenate(mask_blocks(jd - 1, SUBS) + mask_blocks(jd, rr + 1), axis=1)
            selneg = jnp.concatenate([selneg] * C_GROUP, axis=0)
            first = CHUNK + (rr - 1) * BLK
            ones_n = jnp.ones((CHUNK + n_diag, C_DH), BF16)
            ss, v1s = [], []
            for h in range(C_KV):
                cols = slice(h * C_DH, (h + 1) * C_DH)
                k = jnp.concatenate([kc_ref[pl.ds(st_prev, CHUNK), cols], kc_ref[pl.ds(st_diag, n_diag), cols]], axis=0)
                v = jnp.concatenate([vc_ref[pl.ds(st_prev, CHUNK), cols], vc_ref[pl.ds(st_diag, n_diag), cols]], axis=0)
                v1s.append(jnp.concatenate([v, ones_n], axis=1))
                s = lax.dot_general(qh_ref[h], k, NT, preferred_element_type=F32) + selneg
                bias = jnp.concatenate([tiles_ref[h, TILE_PREV], tiles_ref[h, TILE_DIAG]], axis=1)
                ss.append(jnp.concatenate([s[:, :first], s[:, first:] + bias], axis=1))
            _flash_steps(ss, v1s, [(m_ref, acc_ref, h) for h in range(C_KV)])

        _far_chunk_loop(jnp.maximum(jd - 1, 0), run_chunks)

        for rr in range(SUBS):
            @pl.when((jd >= 1) & (r == rr))
            def _(rr=rr):
                near_pair(rr)

        @pl.when(jd == 0)
        def _():
            run_chunks([jd], [SUBS])

        half = (C_HEADS // 2) * C_DH
        for h in range(C_KV):
            a = acc_ref[h]
            o = a[:, :C_DH] / a[:, C_DH:]
            gref = g0_ref if h == 0 else g1_ref
            for g in range(C_GROUP):
                gate = gref[:, g * C_DH:(g + 1) * C_DH]
                out_ref[:, h * half + g * C_DH:h * half + (g + 1) * C_DH] = (
                    o[g * BLK:(g + 1) * BLK] * _silu(gate)).astype(BF16)


def _sparse_attn_prompt(zf, zb, near_tiles, nb, lp, nq_real, k_sel):
    nq = lp // BLK
    half = (C_HEADS // 2) * C_DH
    qi0 = ODD_OFF[4] // (IDX_HEADS * IDX_DIM)
    ki0 = ODD_OFF[5] // LANES
    kc0 = ODD_OFF[1] // (C_KV * C_DH)
    vc0 = ODD_OFF[2] // (C_KV * C_DH)
    g0 = ODD_OFF[3] // half
    rows = C_GROUP * BLK
    n_index_bits = max(1, int(math.ceil(math.log2(lp))))
    return pl.pallas_call(
        functools.partial(_sparse_attn_kernel, k_sel=k_sel, n_index_bits=n_index_bits, nq_real=nq_real),
        grid=(nb, nq),
        in_specs=[pl.BlockSpec((BLK, IDX_HEADS * IDX_DIM), lambda b, i: (b * nq + i, qi0)),
                  pl.BlockSpec((BLK, LANES), lambda b, i: (b * nq + i, ki0)),
                  pl.BlockSpec((lp, LANES), lambda b, i: (b, ki0)),
                  pl.BlockSpec((BLK, C_HEADS * C_DH), lambda b, i: (b * nq + i, 0)),
                  pl.BlockSpec((lp, C_KV * C_DH), lambda b, i: (b, kc0)),
                  pl.BlockSpec((lp, C_KV * C_DH), lambda b, i: (b, vc0)),
                  pl.BlockSpec((BLK, half), lambda b, i: (b * nq + i, g0)),
                  pl.BlockSpec((BLK, half), lambda b, i: (b * nq + i, g0 + 1)),
                  pl.BlockSpec((C_KV, 4, rows, BLK), lambda b, i: (0, 0, 0, 0))],
        out_specs=pl.BlockSpec((BLK, C_HEADS * C_DH), lambda b, i: (b * nq + i, 0)),
        out_shape=jax.ShapeDtypeStruct((nb * lp, C_HEADS * C_DH), BF16),
        scratch_shapes=[pltpu.VMEM((lp // CHUNK, CHUNK, BLK), I32),
                        pltpu.VMEM((IDX_HEADS * BLK, LANES), BF16),
                        pltpu.VMEM((C_KV, rows, C_DH), BF16),
                        pltpu.VMEM((1, BLK), I32),
                        pltpu.VMEM((1, BLK), I32),
                        pltpu.VMEM((C_KV, rows, LANES), F32),
                        pltpu.VMEM((C_KV, rows, 2 * C_DH), F32)],
        compiler_params=_cparams(("parallel", "arbitrary")),
        name="sparse_attn_prompt",
    )(zf, zf, zb, zf, zb, zb, zf, zf, near_tiles)


PAGES_PER_STEP = 64
INDEX_PAGES_PER_STEP = 64
DEC_ROWS = 16


def _interleaved_pages(refs, page):
    halves = [jnp.concatenate([r[pl.ds(h, page, stride=2), :] for r in refs], axis=0) for h in range(2)]
    return jnp.concatenate(halves, axis=1).astype(BF16)


def _paged_attn_kernel(*refs, n_pages_step, masked, page, k_feature_major):
    if masked:
        pt_ref, tau_ref, jmax_ref, selnew_ref = refs[:4]
        refs = refs[4:]
    else:
        pt_ref = refs[0]
        refs = refs[1:]
    q_ref, knew_ref, vnew_ref, bfar_ref, blast_ref, b0_ref = refs[:6]
    refs = refs[6:]
    if masked:
        keys_ref = refs[0]
        refs = refs[1:]
    k_refs = refs[:n_pages_step]
    v_refs = refs[n_pages_step:2 * n_pages_step]
    out_ref, m_ref, l_ref, acc_ref = refs[2 * n_pages_step:]
    del pt_ref
    b = pl.program_id(0)
    j = pl.program_id(1)
    width = n_pages_step * page
    q = q_ref[...]

    @pl.when(j == 0)
    def _():
        s_new = jnp.sum(q.astype(F32) * knew_ref[...].astype(BF16).astype(F32), axis=1, keepdims=True) + b0_ref[...]
        v_new = jnp.broadcast_to(vnew_ref[...].astype(BF16).astype(F32), acc_ref.shape)
        if masked:
            take = selnew_ref[b] > 0
            m_ref[...] = jnp.where(take, s_new, NEG)
            l_ref[...] = jnp.where(take, 1.0, 0.0) * jnp.ones(l_ref.shape, F32)
            acc_ref[...] = jnp.where(take, v_new, 0.0)
        else:
            m_ref[...] = s_new
            l_ref[...] = jnp.ones(l_ref.shape, F32)
            acc_ref[...] = v_new

    if k_feature_major:
        kcat = jnp.concatenate([r[...] for r in k_refs], axis=1).astype(BF16)
        s = jnp.dot(q, kcat, preferred_element_type=F32)
    else:
        s = lax.dot_general(q, _interleaved_pages(k_refs, page), NT, preferred_element_type=F32)
    vcat = _interleaved_pages(v_refs, page)
    s = s + jnp.where(j == pl.num_programs(1) - 1, blast_ref[...], bfar_ref[...])
    if masked:
        key = keys_ref[...]
        idx = j * width + lax.broadcasted_iota(I32, (1, width), 1)
        tau = tau_ref[b]
        sel = (key > tau) | ((key == tau) & (idx <= jmax_ref[b]))
        s = jnp.where(sel, s, NEG)
    m_old = m_ref[...]
    m_new = jnp.maximum(m_old, jnp.max(s, axis=1, keepdims=True))
    alpha = jnp.exp(m_old - m_new)
    p = jnp.exp(s - m_new[:, 0:1])
    l_ref[...] = alpha * l_ref[...] + jnp.sum(p, axis=1, keepdims=True)
    acc_ref[...] = _lane_tile(alpha, 2) * acc_ref[...] + jnp.dot(p.astype(BF16), vcat, preferred_element_type=F32)
    m_ref[...] = m_new

    @pl.when(j == pl.num_programs(1) - 1)
    def _():
        o = acc_ref[...]
        rowi = lax.broadcasted_iota(I32, (DEC_ROWS, LANES), 0)
        upper = (rowi >= DEC_ROWS // 2) if not masked else ((rowi >= DEC_ROWS // 4) & (rowi < DEC_ROWS // 2))
        out_ref[...] = jnp.where(upper, o[:, LANES:], o[:, :LANES]) / l_ref[...]


def _pages_per_step(n_pages, g=PAGES_PER_STEP):
    while n_pages % g:
        g //= 2
    return g


def _paged_attention(qprime, k_new, v_new, bias_far, bias_last, bias0, k_cache, v_cache, page_table, page,
                     k_feature_major, mask_args=None):
    db = qprime.shape[0]
    n_pages = page_table.shape[1]
    width = qprime.shape[2]
    g = _pages_per_step(n_pages)
    n_steps = n_pages // g
    masked = mask_args is not None
    n_pref = 4 if masked else 1

    def page_map(gi):
        return lambda b, j, pt, *_: (pt[b * n_pages + j * g + gi], 0, 0)

    in_specs = [pl.BlockSpec((None, DEC_ROWS, width), lambda b, j, *_: (b, 0, 0)),
                pl.BlockSpec((None, 1, width), lambda b, j, *_: (b, 0, 0)),
                pl.BlockSpec((None, 1, width), lambda b, j, *_: (b, 0, 0)),
                pl.BlockSpec((DEC_ROWS, g * page), lambda b, j, *_: (0, 0)),
                pl.BlockSpec((DEC_ROWS, g * page), lambda b, j, *_: (0, 0)),
                pl.BlockSpec((DEC_ROWS, LANES), lambda b, j, *_: (0, 0))]
    args = [qprime, k_new, v_new, bias_far, bias_last, bias0]
    prefetch = [page_table.reshape(-1)]
    if masked:
        keys, tau, jmax, selnew = mask_args
        prefetch += [tau, jmax, selnew]
        in_specs += [pl.BlockSpec((None, 1, g * page), lambda b, j, *_: (b, 0, j))]
        args += [keys]
    in_specs += [pl.BlockSpec((None,) + k_cache.shape[1:], page_map(gi)) for gi in range(g)]
    in_specs += [pl.BlockSpec((None,) + v_cache.shape[1:], page_map(gi)) for gi in range(g)]
    args += [k_cache] * g + [v_cache] * g
    grid_spec = pltpu.PrefetchScalarGridSpec(
        num_scalar_prefetch=n_pref,
        grid=(db, n_steps),
        in_specs=in_specs,
        out_specs=pl.BlockSpec((None, DEC_ROWS, LANES), lambda b, j, *_: (b, 0, 0)),
        scratch_shapes=[pltpu.VMEM((DEC_ROWS, LANES), F32),
                        pltpu.VMEM((DEC_ROWS, LANES), F32),
                        pltpu.VMEM((DEC_ROWS, width), F32)])
    return pl.pallas_call(
        functools.partial(_paged_attn_kernel, n_pages_step=g, masked=masked, page=page,
                          k_feature_major=k_feature_major),
        grid_spec=grid_spec,
        out_shape=jax.ShapeDtypeStruct((db, DEC_ROWS, LANES), F32),
        compiler_params=_cparams(("parallel", "arbitrary")),
        name="paged_attn_masked" if masked else "paged_attn",
    )(*prefetch, *args)


def _paged_index_kernel(pt_ref, q_ref, w_ref, knew_ref, *refs, n_pages_step, page):
    k_refs = refs[:n_pages_step]
    out_ref = refs[n_pages_step]
    del pt_ref
    j = pl.program_id(1)
    last = pl.num_programs(1) - 1
    q = q_ref[...]
    w = w_ref[...]

    def score(kmat_t):
        d = jnp.dot(q, kmat_t.astype(BF16), preferred_element_type=F32)
        return jnp.sum(jnp.maximum(d, 0.0) * w[:, 0:1], axis=0, keepdims=True)

    @pl.when(j < last)
    def _():
        kcat = jnp.concatenate([r[...] for r in k_refs], axis=1)
        out_ref[...] = _sortable_key(score(kcat))

    @pl.when(j == last)
    def _():
        sc = score(knew_ref[...])
        lane = lax.broadcasted_iota(I32, (1, page), 1)
        sc = jnp.where(lane == 0, sc, -jnp.inf)
        pad = jnp.full((1, (n_pages_step - 1) * page), -jnp.inf, F32)
        full = jnp.concatenate([sc, pad], axis=1) if n_pages_step > 1 else sc
        out_ref[...] = _sortable_key(full)


def _paged_index_scores(qidx, wcol, k_new_tile, idx_cache_t, page_table):
    db = qidx.shape[0]
    n_pages = page_table.shape[1]
    page = idx_cache_t.shape[2]
    g = _pages_per_step(n_pages, INDEX_PAGES_PER_STEP)
    n_steps = n_pages // g

    def page_map(gi):
        return lambda b, j, pt: (pt[b * n_pages + jnp.minimum(j, n_steps - 1) * g + gi], 0, 0)

    grid_spec = pltpu.PrefetchScalarGridSpec(
        num_scalar_prefetch=1,
        grid=(db, n_steps + 1),
        in_specs=[pl.BlockSpec((None, IDX_HEADS, IDX_DIM), lambda b, j, pt: (b, 0, 0)),
                  pl.BlockSpec((None, IDX_HEADS, LANES), lambda b, j, pt: (b, 0, 0)),
                  pl.BlockSpec((None, IDX_DIM, page), lambda b, j, pt: (b, 0, 0))]
        + [pl.BlockSpec((None, IDX_DIM, page), page_map(gi)) for gi in range(g)],
        out_specs=pl.BlockSpec((None, 1, g * page), lambda b, j, pt: (b, 0, j)))
    return pl.pallas_call(
        functools.partial(_paged_index_kernel, n_pages_step=g, page=page),
        grid_spec=grid_spec,
        out_shape=jax.ShapeDtypeStruct((db, 1, (n_steps + 1) * g * page), I32),
        compiler_params=_cparams(("parallel", "arbitrary")),
        name="paged_index_scores",
    )(page_table.reshape(-1), qidx, wcol, k_new_tile, *([idx_cache_t] * g))


def _select_kernel(keys_ref, tau_ref, jmax_ref, *, k_sel, n_index_bits):
    keys = keys_ref[...]
    idx = lax.broadcasted_iota(I32, keys.shape, 1)
    shape = (keys.shape[0], 1)

    def count_fn(pred):
        return jnp.sum(jnp.where(pred(keys, idx), 1, 0), axis=1, keepdims=True)

    tau = _kth_largest(count_fn, k_sel, shape)
    need = k_sel - count_fn(lambda key, i: key > tau)
    jmax = _tie_cutoff(count_fn, tau, need, n_index_bits, shape)
    tau_ref[...] = jnp.broadcast_to(tau, tau_ref.shape)
    jmax_ref[...] = jnp.broadcast_to(jmax, jmax_ref.shape)


def _select_rows(keys2d, k_sel):
    rows, width = keys2d.shape
    n_index_bits = max(1, int(math.ceil(math.log2(width))))
    return pl.pallas_call(
        functools.partial(_select_kernel, k_sel=k_sel, n_index_bits=n_index_bits),
        out_shape=[jax.ShapeDtypeStruct((rows, LANES), I32), jax.ShapeDtypeStruct((rows, LANES), I32)],
        compiler_params=pltpu.CompilerParams(vmem_limit_bytes=VMEM_LIMIT),
        name="select_rows",
    )(keys2d)


def _even_tail_kernel(z_ref, conv_ref, s_ref, oa_ref, convw_ref, gp_ref, normg_ref, lamv_ref, subln_ref,
                      ya_ref, yb_ref, snew_ref, *, lam_init):
    z = z_ref[...]
    hw = B_HEADS * B_DK
    lv = lamv_ref[...]
    lam = (jnp.exp(jnp.sum(lv[0:1] * lv[1:2], axis=1, keepdims=True))
           - jnp.exp(jnp.sum(lv[2:3] * lv[3:4], axis=1, keepdims=True)) + lam_init)
    oa = oa_ref[...]
    for hg in range(A_HEADS):
        o = oa[2 * hg:2 * hg + 1] - lam * oa[2 * hg + 1:2 * hg + 2]
        y = o * lax.rsqrt(jnp.mean(o * o, axis=-1, keepdims=True) + EPS)
        y = (y * subln_ref[...]) * (1.0 - lam_init)
        gate = z[:, EVEN_OFF[3] + hg * A_DV:EVEN_OFF[3] + (hg + 1) * A_DV]
        ya_ref[:, hg * A_DV:(hg + 1) * A_DV] = (y * _silu(gate)).astype(BF16)
    x_new = z[:, EVEN_OFF[4]:EVEN_OFF[4] + B_QKV]
    conv = convw_ref[CONV_W - 1:CONV_W, :] * x_new
    cp = conv_ref[...]
    for i in range(CONV_W - 1):
        conv = conv + convw_ref[i:i + 1, :] * cp[i:i + 1, :]
    act = _silu(conv)
    ab = z[:, EVEN_OFF[6]:EVEN_OFF[6] + LANES]
    gp = gp_ref[...]
    row = lax.broadcasted_iota(I32, (B_DK, B_DV), 0)
    col = lax.broadcasted_iota(I32, (B_DK, B_DV), 1)
    eye = row == col
    for h in range(B_HEADS):
        q = act[:, h * B_DK:(h + 1) * B_DK]
        k = act[:, hw + h * B_DK:hw + (h + 1) * B_DK]
        v = act[:, 2 * hw + h * B_DV:2 * hw + (h + 1) * B_DV]
        q = q * lax.rsqrt(jnp.sum(q * q, axis=-1, keepdims=True) + EPS) * (B_DK ** -0.5)
        k = k * lax.rsqrt(jnp.sum(k * k, axis=-1, keepdims=True) + EPS)
        g = -jnp.exp(gp[0:1, h:h + 1]) * _softplus(ab[:, h:h + 1] + gp[1:2, h:h + 1])
        beta = jax.nn.sigmoid(ab[:, B_HEADS + h:B_HEADS + h + 1])
        eg = jnp.exp(g)
        s0 = s_ref[h]
        kcol = jnp.sum(jnp.where(eye, jnp.broadcast_to(k, (B_DK, B_DK)), 0.0), axis=1, keepdims=True)
        qcol = jnp.sum(jnp.where(eye, jnp.broadcast_to(q, (B_DK, B_DK)), 0.0), axis=1, keepdims=True)
        ks = jnp.sum(kcol * s0, axis=0, keepdims=True)
        qs = jnp.sum(qcol * s0, axis=0, keepdims=True)
        u = beta * (v - eg * ks)
        qk = jnp.sum(q * k, axis=1, keepdims=True)
        o = eg * qs + qk * u
        snew_ref[h] = eg * s0 + kcol * u
        y = o * lax.rsqrt(jnp.mean(o * o, axis=-1, keepdims=True) + EPS) * normg_ref[...]
        gate = z[:, EVEN_OFF[5] + h * B_DV:EVEN_OFF[5] + (h + 1) * B_DV]
        yb_ref[:, h * B_DV:(h + 1) * B_DV] = (y * _silu(gate)).astype(BF16)


def _even_tail(zf_s, conv_prev, s_prev, oa, conv_w, a_log, dt_bias, norm_g, lamv, subln, lam_init):
    db, npad = zf_s.shape
    gp = jnp.zeros((SUBLANES, LANES), F32).at[0, :B_HEADS].set(a_log).at[1, :B_HEADS].set(dt_bias)
    hw = B_HEADS * B_DV
    return pl.pallas_call(
        functools.partial(_even_tail_kernel, lam_init=lam_init),
        grid=(db,),
        in_specs=[pl.BlockSpec((None, 1, npad), lambda b: (b, 0, 0)),
                  pl.BlockSpec((None, CONV_W - 1, B_QKV), lambda b: (b, 0, 0)),
                  pl.BlockSpec((None, B_HEADS, B_DK, B_DV), lambda b: (b, 0, 0, 0)),
                  pl.BlockSpec((None, DEC_ROWS, LANES), lambda b: (b, 0, 0)),
                  pl.BlockSpec((CONV_W, B_QKV), lambda b: (0, 0)),
                  pl.BlockSpec((SUBLANES, LANES), lambda b: (0, 0)),
                  pl.BlockSpec((1, B_DV), lambda b: (0, 0)),
                  pl.BlockSpec((4, A_DH), lambda b: (0, 0)),
                  pl.BlockSpec((1, A_DV), lambda b: (0, 0))],
        out_specs=[pl.BlockSpec((None, 1, A_HEADS * A_DV), lambda b: (b, 0, 0)),
                   pl.BlockSpec((None, 1, hw), lambda b: (b, 0, 0)),
                   pl.BlockSpec((None, B_HEADS, B_DK, B_DV), lambda b: (b, 0, 0, 0))],
        out_shape=[jax.ShapeDtypeStruct((db, 1, A_HEADS * A_DV), BF16),
                   jax.ShapeDtypeStruct((db, 1, hw), BF16),
                   jax.ShapeDtypeStruct((db, B_HEADS, B_DK, B_DV), F32)],
        compiler_params=_cparams(("parallel",)),
        name="even_tail",
    )(zf_s.reshape(db, 1, npad), conv_prev, s_prev, oa, conv_w, gp, norm_g.reshape(1, B_DV), lamv,
      subln.reshape(1, A_DV))


def _odd_tail_kernel(z_ref, oc_ref, y_ref):
    z = z_ref[...]
    oc = oc_ref[...]
    for hg in range(C_HEADS):
        gate = z[:, ODD_OFF[3] + hg * C_DH:ODD_OFF[3] + (hg + 1) * C_DH]
        y_ref[:, hg * C_DH:(hg + 1) * C_DH] = (oc[hg:hg + 1] * _silu(gate)).astype(BF16)


def _odd_tail(zf_s, oc):
    db, npad = zf_s.shape
    return pl.pallas_call(
        _odd_tail_kernel,
        grid=(db,),
        in_specs=[pl.BlockSpec((None, 1, npad), lambda b: (b, 0, 0)),
                  pl.BlockSpec((None, DEC_ROWS, LANES), lambda b: (b, 0, 0))],
        out_specs=pl.BlockSpec((None, 1, C_HEADS * C_DH), lambda b: (b, 0, 0)),
        out_shape=jax.ShapeDtypeStruct((db, 1, C_HEADS * C_DH), BF16),
        compiler_params=_cparams(("parallel",)),
        name="odd_tail",
    )(zf_s.reshape(db, 1, npad), oc)


def _bias_by_distance(table):
    n = jnp.arange(FAR_DIST + 1)
    exact = N_BUCKETS // 2
    nf = jnp.maximum(n, 1).astype(F32)
    large = exact + (jnp.log(nf / exact) / math.log(MAX_DIST / exact) * (N_BUCKETS - exact)).astype(I32)
    bucket = jnp.where(n < exact, n, jnp.minimum(large, N_BUCKETS - 1))
    return table[bucket].astype(F32)


def _prompt_near_tiles(bd, group):
    assert FAR_DIST <= BLK
    heads = bd.shape[1]
    rel = ((bd - bd[FAR_DIST][None, :]) * LOG2E).T
    f = jnp.concatenate([jnp.full((heads, BLK - 1), NEG, F32), rel,
                         jnp.broadcast_to(rel[:, FAR_DIST:], (heads, 2 * BLK - 1 - FAR_DIST))], axis=1)
    period = 3 * BLK
    g = jnp.pad(f[:, ::-1], ((0, 0), (0, period - f.shape[1])))
    wrapped = jnp.tile(g, (1, BLK + 1))[:, :BLK * (period + 1)].reshape(heads, BLK, period + 1)
    strip = wrapped[:, ::-1, :2 * BLK]
    kinds = [None] * 4
    kinds[TILE_ZERO] = jnp.zeros((heads, BLK, BLK), F32)
    kinds[TILE_PREV] = strip[:, :, :BLK]
    kinds[TILE_DIAG] = strip[:, :, BLK:]
    kinds[TILE_MASKED] = jnp.full((heads, BLK, BLK), NEG, F32)
    tiles = jnp.stack(kinds, axis=1)
    n_kv = heads // group
    tiles = tiles.reshape(n_kv, group, 4, BLK, BLK)
    return jnp.transpose(tiles, (0, 2, 1, 3, 4)).reshape(n_kv, 4, group * BLK, BLK)


def _decode_bias(bd, row_heads, past, page, g):
    heads = jnp.asarray(row_heads, I32)
    far = jnp.broadcast_to(bd[FAR_DIST][heads][:, None], (len(row_heads), g * page))
    pos = past - g * page + jnp.arange(g * page)
    dist = jnp.minimum(past - pos, FAR_DIST)
    last = bd[dist][:, heads].T
    new = jnp.broadcast_to(bd[0][heads][:, None], (len(row_heads), LANES))
    return far.astype(F32), last.astype(F32), new.astype(F32)


def _pad_cols(w, mult):
    n = w.shape[1]
    return jnp.pad(w, ((0, 0), (0, _round_up(n, mult) - n)))


def kernel(x_prompt, x_sample, cache_a_k, cache_a_v, state_b_s, state_b_conv, cache_c_k, cache_c_v, cache_c_idx,
           page_table, meta, bias_table, final_norm, norm_e, w_in_e, w_out_e, lam_q1, lam_k1, lam_q2, lam_k2,
           subln_a, conv_b, a_log_b, dt_bias_b, norm_b, norm_o, w_in_o, w_out_o):
    nb, seq, d = x_prompt.shape
    n_meta = meta.shape[0]
    l = seq + n_meta
    lp = _round_up(l, CHUNK)
    nq_real = pl.cdiv(l, BLK)
    db = x_sample.shape[0]
    n_pages = page_table.shape[1]
    page = cache_a_k.shape[2]
    past = n_pages * page
    n_pool = cache_a_k.shape[1]
    assert x_sample.shape[1] == 1 and norm_e.shape[0] == 1 and norm_o.shape[0] == 1
    lam_init = 0.8 - 0.6 * math.exp(-0.3 * 0)

    tn_e, tn_o = 8 * LANES, 5 * LANES
    w_e = _pad_cols(w_in_e[0], tn_e).astype(BF16)
    w_o = _pad_cols(w_in_o[0], tn_o).astype(BF16)
    w_out_a = w_out_e[0][:A_HEADS * A_DV].astype(BF16)
    w_out_b = w_out_e[0][A_HEADS * A_DV:].astype(BF16)
    w_out_c = w_out_o[0].astype(BF16)
    tm = CHUNK
    tm_proj = 2 * CHUNK if (nb * lp) % (2 * CHUNK) == 0 else CHUNK
    dbp = _round_up(db, SUBLANES)

    bd = _bias_by_distance(bias_table)
    tiles_a = _prompt_near_tiles(bd, A_GROUP)
    tiles_c = tiles_a if (A_KV, A_GROUP) == (C_KV, C_GROUP) else _prompt_near_tiles(bd, C_GROUP)
    lamv = jnp.stack([lam_q1[0], lam_k1[0], lam_q2[0], lam_k2[0]]).astype(F32)

    hp = jnp.concatenate([jnp.broadcast_to(meta.astype(F32)[None], (nb, n_meta, d)), x_prompt], axis=1)
    hp = jnp.pad(hp, ((0, 0), (0, lp - l), (0, 0))).reshape(nb * lp, d)
    kv_tile = EVEN_OFF[1] // tn_e
    assert EVEN_OFF[3] <= (kv_tile + 1) * tn_e
    zf, zb = _norm_proj(hp, norm_e[0], w_e, tm_proj, tn_e, (kv_tile, kv_tile + 1))
    ya = _diff_attn_prompt(zf, zb, kv_tile * tn_e, tiles_a, lamv, subln_a[0], nb, lp, nq_real, lam_init)
    yb, pb_s = _gdn_prompt(zf, conv_b[0], a_log_b[0], dt_bias_b[0], norm_b[0], nb, lp, l)
    h1 = _out_proj([ya, yb], [w_out_a, w_out_b], hp, tm)
    z3 = zf.reshape(nb, lp, -1)
    pa_k = z3[:, :l, EVEN_OFF[1]:EVEN_OFF[2]].reshape(1, nb, l, A_KV, 2, A_DH)
    pa_v = z3[:, :l, EVEN_OFF[2]:EVEN_OFF[3]].reshape(1, nb, l, A_KV, A_DV)
    pb_conv = z3[:, l - (CONV_W - 1):l, EVEN_OFF[4]:EVEN_OFF[5]][None]

    zf1, zb1 = _norm_proj(h1, norm_o[0], w_o, tm_proj, tn_o, (0, w_o.shape[1] // tn_o))
    k_sel_p = min(TOPK_MAX, l // 4)
    yc = _sparse_attn_prompt(zf1, zb1, tiles_c, nb, lp, nq_real, k_sel_p)
    yp = _out_proj([yc], [w_out_c], h1, tm, final_gain=final_norm)
    y_prompt = yp.reshape(nb, lp, d)[:, n_meta:l]
    z13 = zf1.reshape(nb, lp, -1)
    pc_k = z13[:, :l, ODD_OFF[1]:ODD_OFF[2]].reshape(1, nb, l, C_KV, C_DH)
    pc_v = z13[:, :l, ODD_OFF[2]:ODD_OFF[3]].reshape(1, nb, l, C_KV, C_DH)
    pc_idx = z13[:, :l, ODD_OFF[5]:ODD_OFF[6]][None]

    hs = jnp.pad(x_sample.reshape(db, d), ((0, dbp - db), (0, 0)))
    zs, _ = _norm_proj(hs, norm_e[0], w_e, dbp, tn_e, (0, 1))
    zs = zs[:db]
    qa = zs[:, :EVEN_OFF[1]].reshape(db, A_KV, A_GROUP, 2, A_DH) * (A_DH ** -0.5)
    qprime = jnp.einsum('bhgcd,hi,cj->bhgcijd', qa, jnp.eye(A_KV, dtype=F32), jnp.eye(2, dtype=F32))
    qprime = qprime.reshape(db, DEC_ROWS, A_KV * 2 * A_DH).astype(BF16)
    k_new = zs[:, EVEN_OFF[1]:EVEN_OFF[2]].reshape(db, 1, -1)
    v_new = zs[:, EVEN_OFF[2]:EVEN_OFF[3]].reshape(db, 1, -1)
    g_dec = _pages_per_step(n_pages)
    rows_a = [r // 2 for r in range(DEC_ROWS)]
    bfar, blast, bnew = _decode_bias(bd, rows_a, past, page, g_dec)
    ak_t = jnp.transpose(cache_a_k[0], (0, 2, 3, 4, 1)).reshape(n_pool, A_KV * 2 * A_DH, page)
    av_r = cache_a_v[0].reshape(n_pool, page * A_KV, A_DV)
    oa = _paged_attention(qprime, k_new, v_new, bfar, blast, bnew, ak_t, av_r, page_table, page,
                          k_feature_major=True)
    ya_s, yb_s, sb_s = _even_tail(zs, state_b_conv[0], state_b_s[0], oa, conv_b[0], a_log_b[0], dt_bias_b[0],
                                  norm_b[0], lamv, subln_a[0], lam_init)
    hs_pad = lambda y: jnp.pad(y.reshape(db, -1), ((0, dbp - db), (0, 0)))
    hs1 = _out_proj([hs_pad(ya_s), hs_pad(yb_s)], [w_out_a, w_out_b], hs, dbp)
    sa_k = zs[:, EVEN_OFF[1]:EVEN_OFF[2]].reshape(1, db, 1, A_KV, 2, A_DH)
    sa_v = zs[:, EVEN_OFF[2]:EVEN_OFF[3]].reshape(1, db, 1, A_KV, A_DV)
    sb_conv = jnp.concatenate([state_b_conv[0][:, 1:], zs[:, None, EVEN_OFF[4]:EVEN_OFF[5]]], axis=1)[None]

    zs1, _ = _norm_proj(hs1, norm_o[0], w_o, dbp, tn_o, (0, 1))
    zs1 = zs1[:db]
    qidx = (zs1[:, ODD_OFF[4]:ODD_OFF[5]].reshape(db, IDX_HEADS, IDX_DIM) * (IDX_DIM ** -0.5)).astype(BF16)
    wcol = jnp.broadcast_to((zs1[:, ODD_OFF[6]:ODD_OFF[7]] * (IDX_HEADS ** -0.5))[:, :, None],
                            (db, IDX_HEADS, LANES))
    ki_new = zs1[:, ODD_OFF[5]:ODD_OFF[6]]
    ki_tile = jnp.zeros((db, IDX_DIM, page), F32).at[:, :, 0].set(ki_new)
    ci_t = jnp.transpose(cache_c_idx[0], (0, 2, 1))
    keys = _paged_index_scores(qidx, wcol, ki_tile, ci_t, page_table)
    k_sel_s = min(TOPK_MAX, (past + 1) // 4)
    tau, jmax = _select_rows(keys.reshape(db, -1), k_sel_s)
    tau, jmax = tau[:, 0], jmax[:, 0]
    key_new = keys[:, 0, past]
    selnew = ((key_new > tau) | ((key_new == tau) & (past <= jmax))).astype(I32)
    qc = zs1[:, :ODD_OFF[1]].reshape(db, C_KV, C_GROUP, C_DH) * (C_DH ** -0.5)
    qcp = jnp.einsum('bhgd,hi->bhgid', qc, jnp.eye(C_KV, dtype=F32)).reshape(db, C_HEADS, C_KV * C_DH)
    qcp = jnp.pad(qcp, ((0, 0), (0, DEC_ROWS - C_HEADS), (0, 0))).astype(BF16)
    kc_new = zs1[:, ODD_OFF[1]:ODD_OFF[2]].reshape(db, 1, -1)
    vc_new = zs1[:, ODD_OFF[2]:ODD_OFF[3]].reshape(db, 1, -1)
    rows_c = [r if r < C_HEADS else 0 for r in range(DEC_ROWS)]
    cfar, clast, cnew = _decode_bias(bd, rows_c, past, page, g_dec)
    ck_r = cache_c_k[0].reshape(n_pool, page * C_KV, C_DH)
    cv_r = cache_c_v[0].reshape(n_pool, page * C_KV, C_DH)
    oc = _paged_attention(qcp, kc_new, vc_new, cfar, clast, cnew, ck_r, cv_r, page_table, page,
                          k_feature_major=False, mask_args=(keys, tau, jmax, selnew))
    yc_s = _odd_tail(zs1, oc)
    ys = _out_proj([hs_pad(yc_s)], [w_out_c], hs1, dbp, final_gain=final_norm)
    y_sample = ys[:db].reshape(db, 1, d)
    sc_k = zs1[:, ODD_OFF[1]:ODD_OFF[2]].reshape(1, db, 1, C_KV, C_DH)
    sc_v = zs1[:, ODD_OFF[2]:ODD_OFF[3]].reshape(1, db, 1, C_KV, C_DH)
    sc_idx = zs1[:, None, ODD_OFF[5]:ODD_OFF[6]][None]

    return (y_prompt, y_sample, pa_k, pa_v, pb_s[None], pb_conv, pc_k, pc_v, pc_idx,
            sa_k, sa_v, sb_s[None], sb_conv, sc_k, sc_v, sc_idx)
```

```python
import functools
import math

import jax
import jax.numpy as jnp
import numpy as np
from jax import lax
from jax.experimental import pallas as pl
from jax.experimental.pallas import tpu as pltpu

F32 = jnp.float32
BF16 = jnp.bfloat16
I32 = jnp.int32

EPS = 1e-6
N_BUCKETS = 32
MAX_DIST = 128
FAR_DIST = MAX_DIST

A_HEADS, A_KV, A_GROUP, A_DH, A_DV = 8, 2, 4, 64, 128
B_HEADS, B_DK, B_DV, CONV_W = 4, 128, 128, 4
C_HEADS, C_KV, C_GROUP, C_DH = 8, 2, 4, 128
IDX_HEADS, IDX_DIM, TOPK_MAX = 8, 64, 256
B_QKV = 2 * B_HEADS * B_DK + B_HEADS * B_DV

EVEN_COLS = (A_HEADS * 2 * A_DH, A_KV * 2 * A_DH, A_KV * A_DV, A_HEADS * A_DV, B_QKV, B_HEADS * B_DV, B_HEADS, B_HEADS)
ODD_COLS = (C_HEADS * C_DH, C_KV * C_DH, C_KV * C_DH, C_HEADS * C_DH, IDX_HEADS * IDX_DIM, IDX_DIM, IDX_HEADS)
EVEN_OFF = tuple(int(v) for v in np.cumsum((0,) + EVEN_COLS))
ODD_OFF = tuple(int(v) for v in np.cumsum((0,) + ODD_COLS))

LANES = 128
SUBLANES = 8
BLK = 128
CHUNK = 512
SUBS = CHUNK // BLK
NEG = -1e30
INT_MIN = -2 ** 31
INT_MAX = 2 ** 31 - 1
LOG2E = math.log2(math.e)
VMEM_LIMIT = 56 * 1024 * 1024
HI = lax.Precision.HIGHEST

NT = (((1,), (1,)), ((), ()))


def _round_up(x, m):
    return (x + m - 1) // m * m


def _cparams(sem):
    return pltpu.CompilerParams(dimension_semantics=sem, vmem_limit_bytes=VMEM_LIMIT)


def _silu(x):
    return x * jax.nn.sigmoid(x)


def _sortable_key(score):
    score = jnp.where(score == 0.0, 0.0, score)
    bits = pltpu.bitcast(score, I32)
    return bits ^ ((bits >> 31) & jnp.int32(0x7FFFFFFF))


def _lane_tile(x, n):
    return jnp.concatenate([x] * n, axis=1)


def _proj_kernel(x_ref, g_ref, w_ref, of_ref, ob_ref, xn_ref, *, bf16_tiles):
    j = pl.program_id(1)

    @pl.when(j == 0)
    def _():
        xf = x_ref[...]
        y = xf * lax.rsqrt(jnp.mean(xf * xf, axis=-1, keepdims=True) + EPS)
        xn_ref[...] = (y * g_ref[...]).astype(BF16)

    acc = jnp.dot(xn_ref[...], w_ref[...], preferred_element_type=F32)
    of_ref[...] = acc

    @pl.when((j >= bf16_tiles[0]) & (j < bf16_tiles[1]))
    def _():
        ob_ref[...] = acc.astype(BF16)


def _norm_proj(x, gain, w_bf16, tm, tn, bf16_tiles):
    m, d = x.shape
    n = w_bf16.shape[1]
    t0, t1 = bf16_tiles
    return pl.pallas_call(
        functools.partial(_proj_kernel, bf16_tiles=bf16_tiles),
        grid=(m // tm, n // tn),
        in_specs=[pl.BlockSpec((tm, d), lambda i, j: (i, 0)),
                  pl.BlockSpec((1, d), lambda i, j: (0, 0)),
                  pl.BlockSpec((d, tn), lambda i, j: (0, j))],
        out_specs=[pl.BlockSpec((tm, tn), lambda i, j: (i, j)),
                   pl.BlockSpec((tm, tn), lambda i, j: (i, jnp.clip(j - t0, 0, t1 - t0 - 1)))],
        out_shape=[jax.ShapeDtypeStruct((m, n), F32), jax.ShapeDtypeStruct((m, (t1 - t0) * tn), BF16)],
        scratch_shapes=[pltpu.VMEM((tm, d), BF16)],
        compiler_params=_cparams(("parallel", "arbitrary")),
        name="norm_proj",
    )(x, gain.reshape(1, d), w_bf16)


def _outproj_kernel(*refs, n_lhs, final_norm, gated):
    ys = refs[:n_lhs]
    ws = refs[n_lhs:2 * n_lhs]
    h_ref = refs[2 * n_lhs]
    pos = 2 * n_lhs + 1
    fn_ref = refs[pos] if final_norm else None
    gate_ref = refs[-2] if gated else None
    out_ref = refs[-1]
    acc = h_ref[...]
    for y_ref, w_ref in zip(ys, ws):
        y = y_ref[...]
        if gated:
            y = (y * _silu(gate_ref[...])).astype(BF16)
        acc = acc + jnp.dot(y, w_ref[...], preferred_element_type=F32)
    if final_norm:
        y = acc * lax.rsqrt(jnp.mean(acc * acc, axis=-1, keepdims=True) + EPS)
        acc = y * fn_ref[...]
    out_ref[...] = acc


def _out_proj(ys, ws, h, tm, final_gain=None, gate=None):
    m, n = h.shape
    n_lhs = len(ys)
    assert gate is None or n_lhs == 1
    in_specs = [pl.BlockSpec((tm, y.shape[1]), lambda i: (i, 0)) for y in ys]
    in_specs += [pl.BlockSpec(w.shape, lambda i: (0, 0)) for w in ws]
    in_specs += [pl.BlockSpec((tm, n), lambda i: (i, 0))]
    args = list(ys) + list(ws) + [h]
    if final_gain is not None:
        in_specs += [pl.BlockSpec((1, n), lambda i: (0, 0))]
        args += [final_gain.reshape(1, n)]
    if gate is not None:
        in_specs += [pl.BlockSpec((tm, gate.shape[1]), lambda i: (i, 0))]
        args += [gate]
    return pl.pallas_call(
        functools.partial(_outproj_kernel, n_lhs=n_lhs, final_norm=final_gain is not None,
                          gated=gate is not None),
        grid=(m // tm,),
        in_specs=in_specs,
        out_specs=pl.BlockSpec((tm, n), lambda i: (i, 0)),
        out_shape=jax.ShapeDtypeStruct((m, n), F32),
        compiler_params=_cparams(("parallel",)),
        name="out_proj",
    )(*args)


def _flash_steps(ss, v1s, m_refs):
    each = range(len(ss))
    m_old = [m_refs[i][0][m_refs[i][2]] for i in each]
    m_new = [jnp.maximum(m_old[i], jnp.max(ss[i], axis=1, keepdims=True)) for i in each]
    alpha = [jnp.exp2(m_old[i] - m_new[i]) for i in each]
    p = [jnp.exp2(ss[i] - _lane_tile(m_new[i], ss[i].shape[1] // LANES)).astype(BF16) for i in each]
    pv = [jnp.dot(p[i], v1s[i], preferred_element_type=F32) for i in each]
    for i in each:
        m_ref, acc_ref, idx = m_refs[i]
        acc_ref[idx] = _lane_tile(alpha[i], 2) * acc_ref[idx] + pv[i]
        m_ref[idx] = m_new[i]


def _far_chunk_loop(n_far, run_chunks):
    def quad(i, carry):
        run_chunks([4 * i, 4 * i + 1, 4 * i + 2, 4 * i + 3])
        return carry

    n_quad = lax.shift_right_logical(n_far, 2)
    lax.fori_loop(0, n_quad, quad, 0)

    @pl.when(jnp.bitwise_and(n_far, 2) == 2)
    def _():
        run_chunks([4 * n_quad, 4 * n_quad + 1])

    @pl.when(jnp.bitwise_and(n_far, 1) == 1)
    def _():
        run_chunks([n_far - 1])


TILE_ZERO, TILE_PREV, TILE_DIAG, TILE_MASKED = range(4)


def _near_bias_kind(sb, r):
    rel = sb - SUBS - r
    return jnp.where(rel == 0, TILE_DIAG, jnp.where(rel == -1, TILE_PREV, jnp.where(rel < -1, TILE_ZERO, TILE_MASKED)))


def _diff_attn_kernel(q_ref, k_ref, v_ref, gate_ref, tiles_ref, lamv_ref, subln_ref, out_ref,
                      qm_ref, m_ref, acc_ref, *, lam_init, nq_real):
    qi = pl.program_id(2)

    @pl.when(qi >= nq_real)
    def _():
        out_ref[...] = jnp.zeros(out_ref.shape, out_ref.dtype)

    @pl.when(qi < nq_real)
    def _():
        lane = lax.broadcasted_iota(I32, (BLK, 2 * A_DH), 1)
        for g in range(A_GROUP):
            qg = q_ref[:, g * 2 * A_DH:(g + 1) * 2 * A_DH] * (A_DH ** -0.5 * LOG2E)
            for c in range(2):
                keep = (lane < A_DH) if c == 0 else (lane >= A_DH)
                qm_ref[c, g * BLK:(g + 1) * BLK, :] = jnp.where(keep, qg, 0.0).astype(BF16)
        m_ref[...] = jnp.full(m_ref.shape, NEG, F32)
        acc_ref[...] = jnp.zeros(acc_ref.shape, F32)
        r = jnp.bitwise_and(qi, SUBS - 1)
        jd = lax.shift_right_logical(qi, int(math.log2(SUBS)))
        ones = jnp.ones((CHUNK, A_DV), BF16)

        def run_chunks(js, first_sbs=None):
            starts = [pl.multiple_of(j * CHUNK, CHUNK) for j in js]
            k = jnp.concatenate([k_ref[pl.ds(st, CHUNK), :] for st in starts], axis=0)
            v1 = jnp.concatenate(
                [jnp.concatenate([v_ref[pl.ds(st, CHUNK), :], ones], axis=1) for st in starts], axis=0)
            ss = [lax.dot_general(qm_ref[c], k, NT, preferred_element_type=F32) for c in range(2)]
            if first_sbs is not None:
                bias = jnp.concatenate(
                    [tiles_ref[_near_bias_kind(sb + i, r)] for sb in first_sbs for i in range(SUBS)], axis=1)
                ss = [s + bias for s in ss]
            _flash_steps(ss, [v1, v1], [(m_ref, acc_ref, c) for c in range(2)])

        def near_pair(rr):
            n_diag = (rr + 1) * BLK
            st_prev = pl.multiple_of((jd - 1) * CHUNK, CHUNK)
            st_diag = pl.multiple_of(jd * CHUNK, CHUNK)
            k = jnp.concatenate([k_ref[pl.ds(st_prev, CHUNK), :], k_ref[pl.ds(st_diag, n_diag), :]], axis=0)
            v = jnp.concatenate([v_ref[pl.ds(st_prev, CHUNK), :], v_ref[pl.ds(st_diag, n_diag), :]], axis=0)
            v1 = jnp.concatenate([v, jnp.ones((CHUNK + n_diag, A_DV), BF16)], axis=1)
            bias = jnp.concatenate([tiles_ref[TILE_PREV], tiles_ref[TILE_DIAG]], axis=1)
            first = CHUNK + (rr - 1) * BLK
            ss = []
            for c in range(2):
                s = lax.dot_general(qm_ref[c], k, NT, preferred_element_type=F32)
                ss.append(jnp.concatenate([s[:, :first], s[:, first:] + bias], axis=1))
            _flash_steps(ss, [v1, v1], [(m_ref, acc_ref, c) for c in range(2)])

        _far_chunk_loop(jnp.maximum(jd - 1, 0), run_chunks)

        for rr in range(SUBS):
            @pl.when((jd >= 1) & (r == rr))
            def _(rr=rr):
                near_pair(rr)

        @pl.when(jd == 0)
        def _():
            run_chunks([jd], [SUBS])

        lv = lamv_ref[...]
        lam = (jnp.exp(jnp.sum(lv[0:1] * lv[1:2], axis=1, keepdims=True))
               - jnp.exp(jnp.sum(lv[2:3] * lv[3:4], axis=1, keepdims=True)) + lam_init)
        a0 = acc_ref[0]
        a1 = acc_ref[1]
        o = a0[:, :A_DV] / a0[:, A_DV:] - lam * (a1[:, :A_DV] / a1[:, A_DV:])
        y = o * lax.rsqrt(jnp.mean(o * o, axis=-1, keepdims=True) + EPS)
        y = (y * subln_ref[...]) * (1.0 - lam_init)
        for g in range(A_GROUP):
            gate = gate_ref[:, g * A_DV:(g + 1) * A_DV]
            out_ref[:, g * A_DV:(g + 1) * A_DV] = (y[g * BLK:(g + 1) * BLK] * _silu(gate)).astype(BF16)


def _diff_attn_prompt(zf, zb, zb_col0, near_tiles, lamv, subln, nb, lp, nq_real, lam_init):
    nq = lp // BLK
    qw = A_GROUP * 2 * A_DH
    k_blk0 = (EVEN_OFF[1] - zb_col0) // (2 * A_DH)
    v_blk0 = (EVEN_OFF[2] - zb_col0) // A_DV
    g_blk0 = EVEN_OFF[3] // (A_GROUP * A_DV)
    rows = A_GROUP * BLK
    return pl.pallas_call(
        functools.partial(_diff_attn_kernel, lam_init=lam_init, nq_real=nq_real),
        grid=(nb, A_KV, nq),
        in_specs=[pl.BlockSpec((BLK, qw), lambda b, h, i: (b * nq + i, h)),
                  pl.BlockSpec((lp, 2 * A_DH), lambda b, h, i: (b, k_blk0 + h)),
                  pl.BlockSpec((lp, A_DV), lambda b, h, i: (b, v_blk0 + h)),
                  pl.BlockSpec((BLK, A_GROUP * A_DV), lambda b, h, i: (b * nq + i, g_blk0 + h)),
                  pl.BlockSpec((None, 4, rows, BLK), lambda b, h, i: (h, 0, 0, 0)),
                  pl.BlockSpec((4, A_DH), lambda b, h, i: (0, 0)),
                  pl.BlockSpec((1, A_DV), lambda b, h, i: (0, 0))],
        out_specs=pl.BlockSpec((BLK, A_GROUP * A_DV), lambda b, h, i: (b * nq + i, h)),
        out_shape=jax.ShapeDtypeStruct((nb * lp, A_HEADS * A_DV), BF16),
        scratch_shapes=[pltpu.VMEM((2, rows, 2 * A_DH), BF16),
                        pltpu.VMEM((2, rows, LANES), F32),
                        pltpu.VMEM((2, rows, 2 * A_DV), F32)],
        compiler_params=_cparams(("parallel", "parallel", "arbitrary")),
        name="diff_attn_prompt",
    )(zf, zb, zb, zf, near_tiles, lamv, subln.reshape(1, A_DV))


def _softplus(x):
    return jnp.maximum(x, 0.0) + jnp.log1p(jnp.exp(-jnp.abs(x)))


def _split_bf16(a):
    hi = a.astype(BF16)
    return hi, (a - hi.astype(F32)).astype(BF16)


def _dot3(a, b):
    (ah, al), (bh, bl) = a, b
    return (jnp.dot(ah, bh, preferred_element_type=F32)
            + (jnp.dot(ah, bl, preferred_element_type=F32) + jnp.dot(al, bh, preferred_element_type=F32)))


def _gdn_chunk_math(s0, q, k, v, gb, betab, row_ge, row_gt):
    n = len(q)
    c = q[0].shape[0]
    each = range(n)
    dot = functools.partial(jnp.dot, preferred_element_type=F32)
    dot_nt = functools.partial(lax.dot_general, dimension_numbers=NT, preferred_element_type=F32)
    ltri = jnp.where(row_ge, 1.0, 0.0).astype(F32)
    gcum = [jnp.dot(ltri, gb[i], precision=HI, preferred_element_type=F32) for i in each]
    decay = [jnp.where(row_ge, jnp.exp(jnp.where(row_ge, gcum[i] - gcum[i].T, 0.0)), 0.0) for i in each]
    kb = [k[i].astype(BF16) for i in each]
    qb = [q[i].astype(BF16) for i in each]
    s0b = [s0[i].astype(BF16) for i in each]
    kk = [dot_nt(kb[i], kb[i]) for i in each]
    x = [-jnp.where(row_gt, betab[i] * decay[i] * kk[i], 0.0) for i in each]
    tm = list(x)
    ps = [_split_bf16(x[i]) for i in each]
    for _ in range(int(math.log2(c)) - 1):
        p = [_dot3(ps[i], ps[i]) for i in each]
        ps = [_split_bf16(p[i]) for i in each]
        tms = [_split_bf16(tm[i]) for i in each]
        tm = [tm[i] + p[i] + _dot3(tms[i], ps[i]) for i in each]
    eg = [jnp.exp(gcum[i]) for i in each]
    ks = [dot(kb[i], s0b[i]) for i in each]
    rhs = [betab[i] * (v[i] - eg[i] * ks[i]) for i in each]
    u = [rhs[i] + dot(tm[i].astype(BF16), rhs[i].astype(BF16)) for i in each]
    ub = [u[i].astype(BF16) for i in each]
    qk = [dot_nt(qb[i], kb[i]) * decay[i] for i in each]
    o = [eg[i] * dot(qb[i], s0b[i]) + dot(qk[i].astype(BF16), ub[i]) for i in each]
    glast = [gcum[i][c - 1:c, :] for i in each]
    kd = [k[i] * jnp.exp(glast[i] - gcum[i]) for i in each]
    s_new = [jnp.exp(glast[i]) * s0[i] + dot(kd[i].T.astype(BF16), ub[i]) for i in each]
    return s_new, o


def _gdn_prompt_kernel(xq_ref, xk_ref, xv_ref, gate_ref, ab_ref, convw_ref, gp_ref, normg_ref,
                       y_ref, sfin_ref, xbuf_ref, s_ref, *, seq_len):
    ci = pl.program_id(0)
    hw = B_HEADS * B_DK
    nb = xq_ref.shape[0]

    @pl.when(ci == 0)
    def _():
        xbuf_ref[:, 0:SUBLANES, :] = jnp.zeros((nb, SUBLANES, B_QKV), F32)
        s_ref[...] = jnp.zeros(s_ref.shape, F32)

    row = lax.broadcasted_iota(I32, (BLK, BLK), 0)
    col = lax.broadcasted_iota(I32, (BLK, BLK), 1)
    row_ge = row >= col
    row_gt = row > col
    valid = (ci * BLK + row) < seq_len
    gp = gp_ref[...]
    chains = [(b, h) for b in range(nb) for h in range(B_HEADS)]
    qs, ks, vs, gbs, betabs = [], [], [], [], []
    for b in range(nb):
        xbuf_ref[b, SUBLANES:SUBLANES + BLK, 0:hw] = xq_ref[b]
        xbuf_ref[b, SUBLANES:SUBLANES + BLK, hw:2 * hw] = xk_ref[b]
        xbuf_ref[b, SUBLANES:SUBLANES + BLK, 2 * hw:3 * hw] = xv_ref[b]
        conv = jnp.zeros((BLK, B_QKV), F32)
        for i in range(CONV_W):
            conv = conv + convw_ref[i:i + 1, :] * xbuf_ref[b, pl.ds(SUBLANES - (CONV_W - 1) + i, BLK), :]
        tail = xbuf_ref[b, BLK:BLK + SUBLANES, :]
        xbuf_ref[b, 0:SUBLANES, :] = tail
        act = _silu(conv)
        ab = ab_ref[b]
        for h in range(B_HEADS):
            q = act[:, h * B_DK:(h + 1) * B_DK]
            k = act[:, hw + h * B_DK:hw + (h + 1) * B_DK]
            v = act[:, 2 * hw + h * B_DV:2 * hw + (h + 1) * B_DV]
            q = q * lax.rsqrt(jnp.sum(q * q, axis=-1, keepdims=True) + EPS) * (B_DK ** -0.5)
            k = k * lax.rsqrt(jnp.sum(k * k, axis=-1, keepdims=True) + EPS)
            a_raw = jnp.broadcast_to(ab[:, h:h + 1], (BLK, BLK))
            b_raw = jnp.broadcast_to(ab[:, B_HEADS + h:B_HEADS + h + 1], (BLK, BLK))
            a_log = gp[0:1, h:h + 1]
            dt_b = gp[1:2, h:h + 1]
            qs.append(q)
            ks.append(k)
            vs.append(v)
            gbs.append(jnp.where(valid, -jnp.exp(a_log) * _softplus(a_raw + dt_b), 0.0))
            betabs.append(jnp.where(valid, jax.nn.sigmoid(b_raw), 0.0))
    s_new, outs = _gdn_chunk_math([s_ref[b, h] for b, h in chains], qs, ks, vs, gbs, betabs, row_ge, row_gt)
    for (b, h), s_bh, o in zip(chains, s_new, outs):
        s_ref[b, h] = s_bh
        y = o * lax.rsqrt(jnp.mean(o * o, axis=-1, keepdims=True) + EPS) * normg_ref[...]
        gate = gate_ref[b, :, h * B_DV:(h + 1) * B_DV]
        y_ref[b, :, h * B_DV:(h + 1) * B_DV] = (y * _silu(gate)).astype(BF16)

    @pl.when(ci == pl.num_programs(0) - 1)
    def _():
        sfin_ref[...] = s_ref[...]


def _gdn_prompt(zf, conv_w, a_log, dt_bias, norm_g, nb, lp, seq_len):
    nc = lp // BLK
    hw = B_HEADS * B_DK
    c0 = EVEN_OFF[4] // hw
    g0 = EVEN_OFF[5] // hw
    ab0 = EVEN_OFF[6] // LANES
    gp = jnp.zeros((SUBLANES, LANES), F32).at[0, :B_HEADS].set(a_log).at[1, :B_HEADS].set(dt_bias)
    z3 = zf.reshape(nb, lp, zf.shape[1])
    y, s_fin = pl.pallas_call(
        functools.partial(_gdn_prompt_kernel, seq_len=seq_len),
        grid=(nc,),
        in_specs=[pl.BlockSpec((nb, BLK, hw), lambda i: (0, i, c0)),
                  pl.BlockSpec((nb, BLK, hw), lambda i: (0, i, c0 + 1)),
                  pl.BlockSpec((nb, BLK, hw), lambda i: (0, i, c0 + 2)),
                  pl.BlockSpec((nb, BLK, hw), lambda i: (0, i, g0)),
                  pl.BlockSpec((nb, BLK, LANES), lambda i: (0, i, ab0)),
                  pl.BlockSpec((CONV_W, B_QKV), lambda i: (0, 0)),
                  pl.BlockSpec((SUBLANES, LANES), lambda i: (0, 0)),
                  pl.BlockSpec((1, B_DV), lambda i: (0, 0))],
        out_specs=[pl.BlockSpec((nb, BLK, hw), lambda i: (0, i, 0)),
                   pl.BlockSpec((nb, B_HEADS, B_DK, B_DV), lambda i: (0, 0, 0, 0))],
        out_shape=[jax.ShapeDtypeStruct((nb, lp, hw), BF16),
                   jax.ShapeDtypeStruct((nb, B_HEADS, B_DK, B_DV), F32)],
        scratch_shapes=[pltpu.VMEM((nb, BLK + SUBLANES, B_QKV), F32),
                        pltpu.VMEM((nb, B_HEADS, B_DK, B_DV), F32)],
        compiler_params=_cparams(("arbitrary",)),
        name="gdn_prompt",
    )(z3, z3, z3, z3, z3, conv_w, gp, norm_g.reshape(1, B_DV))
    return y.reshape(nb * lp, hw), s_fin


def _kth_largest(count_fn, k_sel, shape):
    def step(i, tau):
        cand = tau + lax.shift_left(jnp.int32(1), 31 - i)
        cnt = count_fn(lambda key, idx: key >= cand)
        return jnp.where(cnt >= k_sel, cand, tau)

    return lax.fori_loop(0, 32, step, jnp.full(shape, INT_MIN, I32))


UNCHECKED_BITS = 24
BITS_PER_CHECK = 2


def _kth_separator(count_fn, k_sel, n_keys):
    assert (32 - UNCHECKED_BITS) % BITS_PER_CHECK == 0

    def pending(cnt_tau):
        return jnp.max(jnp.where(cnt_tau > k_sel, 1.0, 0.0)) > 0.5

    def bit_step(bit, tau, cnt_tau):
        cand = tau + lax.shift_left(jnp.int32(1), 31 - bit)
        cnt = count_fn(lambda key, idx: key >= cand)
        take = cnt >= k_sel
        return jnp.where(take, cand, tau), jnp.where(take, cnt, cnt_tau)

    tau, cnt_tau = lax.fori_loop(0, UNCHECKED_BITS, lambda i, c: bit_step(i, *c),
                                 (jnp.full(n_keys.shape, INT_MIN, I32), n_keys))

    def group(carry):
        bit, tau, cnt_tau, _ = carry
        for b in range(BITS_PER_CHECK):
            tau, cnt_tau = bit_step(bit + b, tau, cnt_tau)
        return bit + BITS_PER_CHECK, tau, cnt_tau, pending(cnt_tau)

    init = (jnp.int32(UNCHECKED_BITS), tau, cnt_tau, pending(cnt_tau))
    out = lax.while_loop(lambda c: (c[0] < 32) & c[3], group, init)
    return out[1], out[2]


def _tie_cutoff(count_fn, tau, need, n_index_bits, shape):
    def step(i, jm):
        cand = jm | lax.shift_left(jnp.int32(1), n_index_bits - 1 - i)
        cnt = count_fn(lambda key, idx: (key == tau) & (idx < cand))
        return jnp.where(cnt < need, cand, jm)

    return lax.fori_loop(0, n_index_bits, step, jnp.zeros(shape, I32))


def _sparse_attn_kernel(qi_ref, kw_ref, kidx_ref, qc_ref, kc_ref, vc_ref, g0_ref, g1_ref, tiles_ref, out_ref,
                        keys_ref, qm_ref, qh_ref, tau_ref, jmax_ref, m_ref, acc_ref,
                        *, k_sel, n_index_bits, nq_real):
    qb = pl.program_id(1)

    @pl.when(qb >= nq_real)
    def _():
        out_ref[...] = jnp.zeros(out_ref.shape, out_ref.dtype)

    @pl.when(qb < nq_real)
    def _():
        lane = lax.broadcasted_iota(I32, (BLK, LANES), 1)
        key_pos = lax.broadcasted_iota(I32, (CHUNK, BLK), 0)
        qry_pos = lax.broadcasted_iota(I32, (CHUNK, BLK), 1)
        r = jnp.bitwise_and(qb, SUBS - 1)
        jd = lax.shift_right_logical(qb, int(math.log2(SUBS)))

        for p in range(IDX_HEADS // 2):
            pair = qi_ref[:, p * LANES:(p + 1) * LANES] * (IDX_DIM ** -0.5)
            swapped = pltpu.roll(pair, IDX_DIM, 1)
            for e, src in enumerate((pair, swapped)):
                h = 2 * p + e
                qm_ref[h * BLK:(h + 1) * BLK, :] = jnp.where(lane < IDX_DIM, src, 0.0).astype(BF16)
        w_rows = kw_ref[...].T * (IDX_HEADS ** -0.5)

        def scores_t(j):
            start = pl.multiple_of(j * CHUNK, CHUNK)
            d = lax.dot_general(kidx_ref[pl.ds(start, CHUNK), :], qm_ref[...], NT, preferred_element_type=F32)
            sc = jnp.zeros((CHUNK, BLK), F32)
            for h in range(IDX_HEADS):
                sc = sc + w_rows[IDX_DIM + h:IDX_DIM + h + 1, :] * jnp.maximum(d[:, h * BLK:(h + 1) * BLK], 0.0)
            return sc

        def score_quad(i, carry):
            for j in (4 * i, 4 * i + 1, 4 * i + 2, 4 * i + 3):
                keys_ref[j] = _sortable_key(scores_t(j))
            return carry

        n_quad = lax.shift_right_logical(jd, 2)
        lax.fori_loop(0, n_quad, score_quad, 0)

        @pl.when(jnp.bitwise_and(jd, 2) == 2)
        def _():
            for j in (4 * n_quad, 4 * n_quad + 1):
                keys_ref[j] = _sortable_key(scores_t(j))

        @pl.when(jnp.bitwise_and(jd, 1) == 1)
        def _():
            keys_ref[jd - 1] = _sortable_key(scores_t(jd - 1))

        admissible = (jd * CHUNK + key_pos) <= (qb * BLK + qry_pos)
        keys_ref[jd] = _sortable_key(jnp.where(admissible, scores_t(jd), -jnp.inf))

        def count_fn(pred):
            part = SUBLANES * SUBLANES

            def body(j, acc):
                hit = jnp.where(pred(keys_ref[j], j * CHUNK + key_pos), 1.0, 0.0)
                return acc + jnp.sum(hit.reshape(CHUNK // part, part, BLK), axis=0)
            acc = lax.fori_loop(0, jd + 1, body, jnp.zeros((part, BLK), F32))
            return jnp.sum(acc, axis=0, keepdims=True).astype(I32)

        n_keys = qb * BLK + lax.broadcasted_iota(I32, (1, BLK), 1) + 1
        tau, n_ge = _kth_separator(count_fn, k_sel, n_keys)
        tau_ref[...] = tau
        jmax_ref[...] = jnp.full((1, BLK), INT_MAX, I32)
        excess = jnp.max(jnp.where(n_ge > k_sel, 1.0, 0.0)) > 0.5

        @pl.when(excess)
        def _():
            need = k_sel - count_fn(lambda key, idx: key > tau)
            jmax_ref[...] = _tie_cutoff(count_fn, tau, need, n_index_bits, (1, BLK))

        jmax = jmax_ref[...]

        for h in range(C_KV):
            for g in range(C_GROUP):
                col = (h * C_GROUP + g) * C_DH
                qh_ref[h, g * BLK:(g + 1) * BLK, :] = (
                    qc_ref[:, col:col + C_DH] * (C_DH ** -0.5 * LOG2E)).astype(BF16)
        m_ref[...] = jnp.full(m_ref.shape, NEG, F32)
        acc_ref[...] = jnp.zeros(acc_ref.shape, F32)
        ones = jnp.ones((CHUNK, C_DH), BF16)

        def run_chunks(js, first_sbs=None):
            starts = [pl.multiple_of(j * CHUNK, CHUNK) for j in js]
            masks = []
            for j in js:
                key = keys_ref[j]
                tie_ok = jnp.where((j * CHUNK + key_pos) <= jmax, 0.0, NEG)
                sel_t = jnp.where(key > tau, 0.0, jnp.where(key == tau, tie_ok, NEG))
                masks += [sel_t[i * BLK:(i + 1) * BLK, :].T for i in range(SUBS)]
            selneg = jnp.concatenate(masks, axis=1)
            selneg = jnp.concatenate([selneg] * C_GROUP, axis=0)
            ss, v1s = [], []
            for h in range(C_KV):
                cols = slice(h * C_DH, (h + 1) * C_DH)
                k = jnp.concatenate([kc_ref[pl.ds(st, CHUNK), cols] for st in starts], axis=0)
                v1s.append(jnp.concatenate(
                    [jnp.concatenate([vc_ref[pl.ds(st, CHUNK), cols], ones], axis=1) for st in starts], axis=0))
                s = lax.dot_general(qh_ref[h], k, NT, preferred_element_type=F32) + selneg
                if first_sbs is not None:
                    s = s + jnp.concatenate(
                        [tiles_ref[h, _near_bias_kind(sb + i, r)] for sb in first_sbs for i in range(SUBS)], axis=1)
                ss.append(s)
            _flash_steps(ss, v1s, [(m_ref, acc_ref, h) for h in range(C_KV)])

        def mask_blocks(j, n_blocks):
            key = keys_ref[j][:n_blocks * BLK]
            tie_ok = jnp.where((j * CHUNK + key_pos[:n_blocks * BLK]) <= jmax, 0.0, NEG)
            sel_t = jnp.where(key > tau, 0.0, jnp.where(key == tau, tie_ok, NEG))
            return [sel_t[i * BLK:(i + 1) * BLK, :].T for i in range(n_blocks)]

        def near_pair(rr):
            n_diag = (rr + 1) * BLK
            st_prev = pl.multiple_of((jd - 1) * CHUNK, CHUNK)
            st_diag = pl.multiple_of(jd * CHUNK, CHUNK)
            selneg = jnp.concatenate(mask_blocks(jd - 1, SUBS) + mask_blocks(jd, rr + 1), axis=1)
            selneg = jnp.concatenate([selneg] * C_GROUP, axis=0)
            first = CHUNK + (rr - 1) * BLK
            ones_n = jnp.ones((CHUNK + n_diag, C_DH), BF16)
            ss, v1s = [], []
            for h in range(C_KV):
                cols = slice(h * C_DH, (h + 1) * C_DH)
                k = jnp.concatenate([kc_ref[pl.ds(st_prev, CHUNK), cols], kc_ref[pl.ds(st_diag, n_diag), cols]], axis=0)
                v = jnp.concatenate([vc_ref[pl.ds(st_prev, CHUNK), cols], vc_ref[pl.ds(st_diag, n_diag), cols]], axis=0)
                v1s.append(jnp.concatenate([v, ones_n], axis=1))
                s = lax.dot_general(qh_ref[h], k, NT, preferred_element_type=F32) + selneg
                bias = jnp.concatenate([tiles_ref[h, TILE_PREV], tiles_ref[h, TILE_DIAG]], axis=1)
                ss.append(jnp.concatenate([s[:, :first], s[:, first:] + bias], axis=1))
            _flash_steps(ss, v1s, [(m_ref, acc_ref, h) for h in range(C_KV)])

        _far_chunk_loop(jnp.maximum(jd - 1, 0), run_chunks)

        for rr in range(SUBS):
            @pl.when((jd >= 1) & (r == rr))
            def _(rr=rr):
                near_pair(rr)

        @pl.when(jd == 0)
        def _():
            run_chunks([jd], [SUBS])

        half = (C_HEADS // 2) * C_DH
        for h in range(C_KV):
            a = acc_ref[h]
            o = a[:, :C_DH] / a[:, C_DH:]
            gref = g0_ref if h == 0 else g1_ref
            for g in range(C_GROUP):
                gate = gref[:, g * C_DH:(g + 1) * C_DH]
                out_ref[:, h * half + g * C_DH:h * half + (g + 1) * C_DH] = (
                    o[g * BLK:(g + 1) * BLK] * _silu(gate)).astype(BF16)


def _sparse_attn_prompt(zf, zb, near_tiles, nb, lp, nq_real, k_sel):
    nq = lp // BLK
    half = (C_HEADS // 2) * C_DH
    qi0 = ODD_OFF[4] // (IDX_HEADS * IDX_DIM)
    ki0 = ODD_OFF[5] // LANES
    kc0 = ODD_OFF[1] // (C_KV * C_DH)
    vc0 = ODD_OFF[2] // (C_KV * C_DH)
    g0 = ODD_OFF[3] // half
    rows = C_GROUP * BLK
    n_index_bits = max(1, int(math.ceil(math.log2(lp))))
    return pl.pallas_call(
        functools.partial(_sparse_attn_kernel, k_sel=k_sel, n_index_bits=n_index_bits, nq_real=nq_real),
        grid=(nb, nq),
        in_specs=[pl.BlockSpec((BLK, IDX_HEADS * IDX_DIM), lambda b, i: (b * nq + i, qi0)),
                  pl.BlockSpec((BLK, LANES), lambda b, i: (b * nq + i, ki0)),
                  pl.BlockSpec((lp, LANES), lambda b, i: (b, ki0)),
                  pl.BlockSpec((BLK, C_HEADS * C_DH), lambda b, i: (b * nq + i, 0)),
                  pl.BlockSpec((lp, C_KV * C_DH), lambda b, i: (b, kc0)),
                  pl.BlockSpec((lp, C_KV * C_DH), lambda b, i: (b, vc0)),
                  pl.BlockSpec((BLK, half), lambda b, i: (b * nq + i, g0)),
                  pl.BlockSpec((BLK, half), lambda b, i: (b * nq + i, g0 + 1)),
                  pl.BlockSpec((C_KV, 4, rows, BLK), lambda b, i: (0, 0, 0, 0))],
        out_specs=pl.BlockSpec((BLK, C_HEADS * C_DH), lambda b, i: (b * nq + i, 0)),
        out_shape=jax.ShapeDtypeStruct((nb * lp, C_HEADS * C_DH), BF16),
        scratch_shapes=[pltpu.VMEM((lp // CHUNK, CHUNK, BLK), I32),
                        pltpu.VMEM((IDX_HEADS * BLK, LANES), BF16),
                        pltpu.VMEM((C_KV, rows, C_DH), BF16),
                        pltpu.VMEM((1, BLK), I32),
                        pltpu.VMEM((1, BLK), I32),
                        pltpu.VMEM((C_KV, rows, LANES), F32),
                        pltpu.VMEM((C_KV, rows, 2 * C_DH), F32)],
        compiler_params=_cparams(("parallel", "arbitrary")),
        name="sparse_attn_prompt",
    )(zf, zf, zb, zf, zb, zb, zf, zf, near_tiles)


PAGES_PER_STEP = 64
INDEX_PAGES_PER_STEP = 64
DEC_ROWS = 16


def _interleaved_pages(refs, page):
    halves = [jnp.concatenate([r[pl.ds(h, page, stride=2), :] for r in refs], axis=0) for h in range(2)]
    return jnp.concatenate(halves, axis=1).astype(BF16)


def _paged_attn_kernel(*refs, n_pages_step, masked, page, k_feature_major):
    if masked:
        pt_ref, tau_ref, jmax_ref, selnew_ref = refs[:4]
        refs = refs[4:]
    else:
        pt_ref = refs[0]
        refs = refs[1:]
    q_ref, knew_ref, vnew_ref, bfar_ref, blast_ref, b0_ref = refs[:6]
    refs = refs[6:]
    if masked:
        keys_ref = refs[0]
        refs = refs[1:]
    k_refs = refs[:n_pages_step]
    v_refs = refs[n_pages_step:2 * n_pages_step]
    out_ref, m_ref, l_ref, acc_ref = refs[2 * n_pages_step:]
    del pt_ref
    b = pl.program_id(0)
    j = pl.program_id(1)
    width = n_pages_step * page
    q = q_ref[...]

    @pl.when(j == 0)
    def _():
        s_new = jnp.sum(q.astype(F32) * knew_ref[...].astype(BF16).astype(F32), axis=1, keepdims=True) + b0_ref[...]
        v_new = jnp.broadcast_to(vnew_ref[...].astype(BF16).astype(F32), acc_ref.shape)
        if masked:
            take = selnew_ref[b] > 0
            m_ref[...] = jnp.where(take, s_new, NEG)
            l_ref[...] = jnp.where(take, 1.0, 0.0) * jnp.ones(l_ref.shape, F32)
            acc_ref[...] = jnp.where(take, v_new, 0.0)
        else:
            m_ref[...] = s_new
            l_ref[...] = jnp.ones(l_ref.shape, F32)
            acc_ref[...] = v_new

    if k_feature_major:
        kcat = jnp.concatenate([r[...] for r in k_refs], axis=1).astype(BF16)
        s = jnp.dot(q, kcat, preferred_element_type=F32)
    else:
        s = lax.dot_general(q, _interleaved_pages(k_refs, page), NT, preferred_element_type=F32)
    vcat = _interleaved_pages(v_refs, page)
    s = s + jnp.where(j == pl.num_programs(1) - 1, blast_ref[...], bfar_ref[...])
    if masked:
        key = keys_ref[...]
        idx = j * width + lax.broadcasted_iota(I32, (1, width), 1)
        tau = tau_ref[b]
        sel = (key > tau) | ((key == tau) & (idx <= jmax_ref[b]))
        s = jnp.where(sel, s, NEG)
    m_old = m_ref[...]
    m_new = jnp.maximum(m_old, jnp.max(s, axis=1, keepdims=True))
    alpha = jnp.exp(m_old - m_new)
    p = jnp.exp(s - m_new[:, 0:1])
    l_ref[...] = alpha * l_ref[...] + jnp.sum(p, axis=1, keepdims=True)
    acc_ref[...] = _lane_tile(alpha, 2) * acc_ref[...] + jnp.dot(p.astype(BF16), vcat, preferred_element_type=F32)
    m_ref[...] = m_new

    @pl.when(j == pl.num_programs(1) - 1)
    def _():
        o = acc_ref[...]
        rowi = lax.broadcasted_iota(I32, (DEC_ROWS, LANES), 0)
        upper = (rowi >= DEC_ROWS // 2) if not masked else ((rowi >= DEC_ROWS // 4) & (rowi < DEC_ROWS // 2))
        out_ref[...] = jnp.where(upper, o[:, LANES:], o[:, :LANES]) / l_ref[...]


def _pages_per_step(n_pages, g=PAGES_PER_STEP):
    while n_pages % g:
        g //= 2
    return g


def _paged_attention(qprime, k_new, v_new, bias_far, bias_last, bias0, k_cache, v_cache, page_table, page,
                     k_feature_major, mask_args=None):
    db = qprime.shape[0]
    n_pages = page_table.shape[1]
    width = qprime.shape[2]
    g = _pages_per_step(n_pages)
    n_steps = n_pages // g
    masked = mask_args is not None
    n_pref = 4 if masked else 1

    def page_map(gi):
        return lambda b, j, pt, *_: (pt[b * n_pages + j * g + gi], 0, 0)

    in_specs = [pl.BlockSpec((None, DEC_ROWS, width), lambda b, j, *_: (b, 0, 0)),
                pl.BlockSpec((None, 1, width), lambda b, j, *_: (b, 0, 0)),
                pl.BlockSpec((None, 1, width), lambda b, j, *_: (b, 0, 0)),
                pl.BlockSpec((DEC_ROWS, g * page), lambda b, j, *_: (0, 0)),
                pl.BlockSpec((DEC_ROWS, g * page), lambda b, j, *_: (0, 0)),
                pl.BlockSpec((DEC_ROWS, LANES), lambda b, j, *_: (0, 0))]
    args = [qprime, k_new, v_new, bias_far, bias_last, bias0]
    prefetch = [page_table.reshape(-1)]
    if masked:
        keys, tau, jmax, selnew = mask_args
        prefetch += [tau, jmax, selnew]
        in_specs += [pl.BlockSpec((None, 1, g * page), lambda b, j, *_: (b, 0, j))]
        args += [keys]
    in_specs += [pl.BlockSpec((None,) + k_cache.shape[1:], page_map(gi)) for gi in range(g)]
    in_specs += [pl.BlockSpec((None,) + v_cache.shape[1:], page_map(gi)) for gi in range(g)]
    args += [k_cache] * g + [v_cache] * g
    grid_spec = pltpu.PrefetchScalarGridSpec(
        num_scalar_prefetch=n_pref,
        grid=(db, n_steps),
        in_specs=in_specs,
        out_specs=pl.BlockSpec((None, DEC_ROWS, LANES), lambda b, j, *_: (b, 0, 0)),
        scratch_shapes=[pltpu.VMEM((DEC_ROWS, LANES), F32),
                        pltpu.VMEM((DEC_ROWS, LANES), F32),
                        pltpu.VMEM((DEC_ROWS, width), F32)])
    return pl.pallas_call(
        functools.partial(_paged_attn_kernel, n_pages_step=g, masked=masked, page=page,
                          k_feature_major=k_feature_major),
        grid_spec=grid_spec,
        out_shape=jax.ShapeDtypeStruct((db, DEC_ROWS, LANES), F32),
        compiler_params=_cparams(("parallel", "arbitrary")),
        name="paged_attn_masked" if masked else "paged_attn",
    )(*prefetch, *args)


def _paged_index_kernel(pt_ref, q_ref, w_ref, knew_ref, *refs, n_pages_step, page):
    k_refs = refs[:n_pages_step]
    out_ref = refs[n_pages_step]
    del pt_ref
    j = pl.program_id(1)
    last = pl.num_programs(1) - 1
    q = q_ref[...]
    w = w_ref[...]

    def score(kmat_t):
        d = jnp.dot(q, kmat_t.astype(BF16), preferred_element_type=F32)
        return jnp.sum(jnp.maximum(d, 0.0) * w[:, 0:1], axis=0, keepdims=True)

    @pl.when(j < last)
    def _():
        kcat = jnp.concatenate([r[...] for r in k_refs], axis=1)
        out_ref[...] = _sortable_key(score(kcat))

    @pl.when(j == last)
    def _():
        sc = score(knew_ref[...])
        lane = lax.broadcasted_iota(I32, (1, page), 1)
        sc = jnp.where(lane == 0, sc, -jnp.inf)
        pad = jnp.full((1, (n_pages_step - 1) * page), -jnp.inf, F32)
        full = jnp.concatenate([sc, pad], axis=1) if n_pages_step > 1 else sc
        out_ref[...] = _sortable_key(full)


def _paged_index_scores(qidx, wcol, k_new_tile, idx_cache_t, page_table):
    db = qidx.shape[0]
    n_pages = page_table.shape[1]
    page = idx_cache_t.shape[2]
    g = _pages_per_step(n_pages, INDEX_PAGES_PER_STEP)
    n_steps = n_pages // g

    def page_map(gi):
        return lambda b, j, pt: (pt[b * n_pages + jnp.minimum(j, n_steps - 1) * g + gi], 0, 0)

    grid_spec = pltpu.PrefetchScalarGridSpec(
        num_scalar_prefetch=1,
        grid=(db, n_steps + 1),
        in_specs=[pl.BlockSpec((None, IDX_HEADS, IDX_DIM), lambda b, j, pt: (b, 0, 0)),
                  pl.BlockSpec((None, IDX_HEADS, LANES), lambda b, j, pt: (b, 0, 0)),
                  pl.BlockSpec((None, IDX_DIM, page), lambda b, j, pt: (b, 0, 0))]
        + [pl.BlockSpec((None, IDX_DIM, page), page_map(gi)) for gi in range(g)],
        out_specs=pl.BlockSpec((None, 1, g * page), lambda b, j, pt: (b, 0, j)))
    return pl.pallas_call(
        functools.partial(_paged_index_kernel, n_pages_step=g, page=page),
        grid_spec=grid_spec,
        out_shape=jax.ShapeDtypeStruct((db, 1, (n_steps + 1) * g * page), I32),
        compiler_params=_cparams(("parallel", "arbitrary")),
        name="paged_index_scores",
    )(page_table.reshape(-1), qidx, wcol, k_new_tile, *([idx_cache_t] * g))


def _select_kernel(keys_ref, tau_ref, jmax_ref, *, k_sel, n_index_bits):
    keys = keys_ref[...]
    idx = lax.broadcasted_iota(I32, keys.shape, 1)
    shape = (keys.shape[0], 1)

    def count_fn(pred):
        return jnp.sum(jnp.where(pred(keys, idx), 1, 0), axis=1, keepdims=True)

    tau = _kth_largest(count_fn, k_sel, shape)
    need = k_sel - count_fn(lambda key, i: key > tau)
    jmax = _tie_cutoff(count_fn, tau, need, n_index_bits, shape)
    tau_ref[...] = jnp.broadcast_to(tau, tau_ref.shape)
    jmax_ref[...] = jnp.broadcast_to(jmax, jmax_ref.shape)


def _select_rows(keys2d, k_sel):
    rows, width = keys2d.shape
    n_index_bits = max(1, int(math.ceil(math.log2(width))))
    return pl.pallas_call(
        functools.partial(_select_kernel, k_sel=k_sel, n_index_bits=n_index_bits),
        out_shape=[jax.ShapeDtypeStruct((rows, LANES), I32), jax.ShapeDtypeStruct((rows, LANES), I32)],
        compiler_params=pltpu.CompilerParams(vmem_limit_bytes=VMEM_LIMIT),
        name="select_rows",
    )(keys2d)


def _even_tail_kernel(z_ref, conv_ref, s_ref, oa_ref, convw_ref, gp_ref, normg_ref, lamv_ref, subln_ref,
                      ya_ref, yb_ref, snew_ref, *, lam_init):
    z = z_ref[...]
    hw = B_HEADS * B_DK
    lv = lamv_ref[...]
    lam = (jnp.exp(jnp.sum(lv[0:1] * lv[1:2], axis=1, keepdims=True))
           - jnp.exp(jnp.sum(lv[2:3] * lv[3:4], axis=1, keepdims=True)) + lam_init)
    oa = oa_ref[...]
    for hg in range(A_HEADS):
        o = oa[2 * hg:2 * hg + 1] - lam * oa[2 * hg + 1:2 * hg + 2]
        y = o * lax.rsqrt(jnp.mean(o * o, axis=-1, keepdims=True) + EPS)
        y = (y * subln_ref[...]) * (1.0 - lam_init)
        gate = z[:, EVEN_OFF[3] + hg * A_DV:EVEN_OFF[3] + (hg + 1) * A_DV]
        ya_ref[:, hg * A_DV:(hg + 1) * A_DV] = (y * _silu(gate)).astype(BF16)
    x_new = z[:, EVEN_OFF[4]:EVEN_OFF[4] + B_QKV]
    conv = convw_ref[CONV_W - 1:CONV_W, :] * x_new
    cp = conv_ref[...]
    for i in range(CONV_W - 1):
        conv = conv + convw_ref[i:i + 1, :] * cp[i:i + 1, :]
    act = _silu(conv)
    ab = z[:, EVEN_OFF[6]:EVEN_OFF[6] + LANES]
    gp = gp_ref[...]
    row = lax.broadcasted_iota(I32, (B_DK, B_DV), 0)
    col = lax.broadcasted_iota(I32, (B_DK, B_DV), 1)
    eye = row == col
    for h in range(B_HEADS):
        q = act[:, h * B_DK:(h + 1) * B_DK]
        k = act[:, hw + h * B_DK:hw + (h + 1) * B_DK]
        v = act[:, 2 * hw + h * B_DV:2 * hw + (h + 1) * B_DV]
        q = q * lax.rsqrt(jnp.sum(q * q, axis=-1, keepdims=True) + EPS) * (B_DK ** -0.5)
        k = k * lax.rsqrt(jnp.sum(k * k, axis=-1, keepdims=True) + EPS)
        g = -jnp.exp(gp[0:1, h:h + 1]) * _softplus(ab[:, h:h + 1] + gp[1:2, h:h + 1])
        beta = jax.nn.sigmoid(ab[:, B_HEADS + h:B_HEADS + h + 1])
        eg = jnp.exp(g)
        s0 = s_ref[h]
        kcol = jnp.sum(jnp.where(eye, jnp.broadcast_to(k, (B_DK, B_DK)), 0.0), axis=1, keepdims=True)
        qcol = jnp.sum(jnp.where(eye, jnp.broadcast_to(q, (B_DK, B_DK)), 0.0), axis=1, keepdims=True)
        ks = jnp.sum(kcol * s0, axis=0, keepdims=True)
        qs = jnp.sum(qcol * s0, axis=0, keepdims=True)
        u = beta * (v - eg * ks)
        qk = jnp.sum(q * k, axis=1, keepdims=True)
        o = eg * qs + qk * u
        snew_ref[h] = eg * s0 + kcol * u
        y = o * lax.rsqrt(jnp.mean(o * o, axis=-1, keepdims=True) + EPS) * normg_ref[...]
        gate = z[:, EVEN_OFF[5] + h * B_DV:EVEN_OFF[5] + (h + 1) * B_DV]
        yb_ref[:, h * B_DV:(h + 1) * B_DV] = (y * _silu(gate)).astype(BF16)


def _even_tail(zf_s, conv_prev, s_prev, oa, conv_w, a_log, dt_bias, norm_g, lamv, subln, lam_init):
    db, npad = zf_s.shape
    gp = jnp.zeros((SUBLANES, LANES), F32).at[0, :B_HEADS].set(a_log).at[1, :B_HEADS].set(dt_bias)
    hw = B_HEADS * B_DV
    return pl.pallas_call(
        functools.partial(_even_tail_kernel, lam_init=lam_init),
        grid=(db,),
        in_specs=[pl.BlockSpec((None, 1, npad), lambda b: (b, 0, 0)),
                  pl.BlockSpec((None, CONV_W - 1, B_QKV), lambda b: (b, 0, 0)),
                  pl.BlockSpec((None, B_HEADS, B_DK, B_DV), lambda b: (b, 0, 0, 0)),
                  pl.BlockSpec((None, DEC_ROWS, LANES), lambda b: (b, 0, 0)),
                  pl.BlockSpec((CONV_W, B_QKV), lambda b: (0, 0)),
                  pl.BlockSpec((SUBLANES, LANES), lambda b: (0, 0)),
                  pl.BlockSpec((1, B_DV), lambda b: (0, 0)),
                  pl.BlockSpec((4, A_DH), lambda b: (0, 0)),
                  pl.BlockSpec((1, A_DV), lambda b: (0, 0))],
        out_specs=[pl.BlockSpec((None, 1, A_HEADS * A_DV), lambda b: (b, 0, 0)),
                   pl.BlockSpec((None, 1, hw), lambda b: (b, 0, 0)),
                   pl.BlockSpec((None, B_HEADS, B_DK, B_DV), lambda b: (b, 0, 0, 0))],
        out_shape=[jax.ShapeDtypeStruct((db, 1, A_HEADS * A_DV), BF16),
                   jax.ShapeDtypeStruct((db, 1, hw), BF16),
                   jax.ShapeDtypeStruct((db, B_HEADS, B_DK, B_DV), F32)],
        compiler_params=_cparams(("parallel",)),
        name="even_tail",
    )(zf_s.reshape(db, 1, npad), conv_prev, s_prev, oa, conv_w, gp, norm_g.reshape(1, B_DV), lamv,
      subln.reshape(1, A_DV))


def _bias_by_distance(table):
    n = jnp.arange(FAR_DIST + 1)
    exact = N_BUCKETS // 2
    nf = jnp.maximum(n, 1).astype(F32)
    large = exact + (jnp.log(nf / exact) / math.log(MAX_DIST / exact) * (N_BUCKETS - exact)).astype(I32)
    bucket = jnp.where(n < exact, n, jnp.minimum(large, N_BUCKETS - 1))
    return table[bucket].astype(F32)


def _prompt_near_tiles(bd, group):
    assert FAR_DIST <= BLK
    heads = bd.shape[1]
    rel = ((bd - bd[FAR_DIST][None, :]) * LOG2E).T
    f = jnp.concatenate([jnp.full((heads, BLK - 1), NEG, F32), rel,
                         jnp.broadcast_to(rel[:, FAR_DIST:], (heads, 2 * BLK - 1 - FAR_DIST))], axis=1)
    period = 3 * BLK
    g = jnp.pad(f[:, ::-1], ((0, 0), (0, period - f.shape[1])))
    wrapped = jnp.tile(g, (1, BLK + 1))[:, :BLK * (period + 1)].reshape(heads, BLK, period + 1)
    strip = wrapped[:, ::-1, :2 * BLK]
    kinds = [None] * 4
    kinds[TILE_ZERO] = jnp.zeros((heads, BLK, BLK), F32)
    kinds[TILE_PREV] = strip[:, :, :BLK]
    kinds[TILE_DIAG] = strip[:, :, BLK:]
    kinds[TILE_MASKED] = jnp.full((heads, BLK, BLK), NEG, F32)
    tiles = jnp.stack(kinds, axis=1)
    n_kv = heads // group
    tiles = tiles.reshape(n_kv, group, 4, BLK, BLK)
    return jnp.transpose(tiles, (0, 2, 1, 3, 4)).reshape(n_kv, 4, group * BLK, BLK)


def _decode_bias(bd, row_heads, past, page, g):
    heads = jnp.asarray(row_heads, I32)
    far = jnp.broadcast_to(bd[FAR_DIST][heads][:, None], (len(row_heads), g * page))
    pos = past - g * page + jnp.arange(g * page)
    dist = jnp.minimum(past - pos, FAR_DIST)
    last = bd[dist][:, heads].T
    new = jnp.broadcast_to(bd[0][heads][:, None], (len(row_heads), LANES))
    return far.astype(F32), last.astype(F32), new.astype(F32)


def _pad_cols(w, mult):
    n = w.shape[1]
    return jnp.pad(w, ((0, 0), (0, _round_up(n, mult) - n)))


def kernel(x_prompt, x_sample, cache_a_k, cache_a_v, state_b_s, state_b_conv, cache_c_k, cache_c_v, cache_c_idx,
           page_table, meta, bias_table, final_norm, norm_e, w_in_e, w_out_e, lam_q1, lam_k1, lam_q2, lam_k2,
           subln_a, conv_b, a_log_b, dt_bias_b, norm_b, norm_o, w_in_o, w_out_o):
    nb, seq, d = x_prompt.shape
    n_meta = meta.shape[0]
    l = seq + n_meta
    lp = _round_up(l, CHUNK)
    nq_real = pl.cdiv(l, BLK)
    db = x_sample.shape[0]
    n_pages = page_table.shape[1]
    page = cache_a_k.shape[2]
    past = n_pages * page
    n_pool = cache_a_k.shape[1]
    assert x_sample.shape[1] == 1 and norm_e.shape[0] == 1 and norm_o.shape[0] == 1
    lam_init = 0.8 - 0.6 * math.exp(-0.3 * 0)

    tn_e, tn_o = 8 * LANES, 5 * LANES
    w_e = _pad_cols(w_in_e[0], tn_e).astype(BF16)
    w_o = _pad_cols(w_in_o[0], tn_o).astype(BF16)
    w_out_a = w_out_e[0][:A_HEADS * A_DV].astype(BF16)
    w_out_b = w_out_e[0][A_HEADS * A_DV:].astype(BF16)
    w_out_c = w_out_o[0].astype(BF16)
    tm = CHUNK
    tm_proj = 2 * CHUNK if (nb * lp) % (2 * CHUNK) == 0 else CHUNK
    dbp = _round_up(db, SUBLANES)

    bd = _bias_by_distance(bias_table)
    tiles_a = _prompt_near_tiles(bd, A_GROUP)
    tiles_c = tiles_a if (A_KV, A_GROUP) == (C_KV, C_GROUP) else _prompt_near_tiles(bd, C_GROUP)
    lamv = jnp.stack([lam_q1[0], lam_k1[0], lam_q2[0], lam_k2[0]]).astype(F32)

    hp = jnp.concatenate([jnp.broadcast_to(meta.astype(F32)[None], (nb, n_meta, d)), x_prompt], axis=1)
    hp = jnp.pad(hp, ((0, 0), (0, lp - l), (0, 0))).reshape(nb * lp, d)
    kv_tile = EVEN_OFF[1] // tn_e
    assert EVEN_OFF[3] <= (kv_tile + 1) * tn_e
    zf, zb = _norm_proj(hp, norm_e[0], w_e, tm_proj, tn_e, (kv_tile, kv_tile + 1))
    ya = _diff_attn_prompt(zf, zb, kv_tile * tn_e, tiles_a, lamv, subln_a[0], nb, lp, nq_real, lam_init)
    yb, pb_s = _gdn_prompt(zf, conv_b[0], a_log_b[0], dt_bias_b[0], norm_b[0], nb, lp, l)
    h1 = _out_proj([ya, yb], [w_out_a, w_out_b], hp, tm)
    z3 = zf.reshape(nb, lp, -1)
    pa_k = z3[:, :l, EVEN_OFF[1]:EVEN_OFF[2]].reshape(1, nb, l, A_KV, 2, A_DH)
    pa_v = z3[:, :l, EVEN_OFF[2]:EVEN_OFF[3]].reshape(1, nb, l, A_KV, A_DV)
    pb_conv = z3[:, l - (CONV_W - 1):l, EVEN_OFF[4]:EVEN_OFF[5]][None]

    zf1, zb1 = _norm_proj(h1, norm_o[0], w_o, tm_proj, tn_o, (0, w_o.shape[1] // tn_o))
    k_sel_p = min(TOPK_MAX, l // 4)
    yc = _sparse_attn_prompt(zf1, zb1, tiles_c, nb, lp, nq_real, k_sel_p)
    yp = _out_proj([yc], [w_out_c], h1, tm, final_gain=final_norm)
    y_prompt = yp.reshape(nb, lp, d)[:, n_meta:l]
    z13 = zf1.reshape(nb, lp, -1)
    pc_k = z13[:, :l, ODD_OFF[1]:ODD_OFF[2]].reshape(1, nb, l, C_KV, C_DH)
    pc_v = z13[:, :l, ODD_OFF[2]:ODD_OFF[3]].reshape(1, nb, l, C_KV, C_DH)
    pc_idx = z13[:, :l, ODD_OFF[5]:ODD_OFF[6]][None]

    hs = jnp.pad(x_sample.reshape(db, d), ((0, dbp - db), (0, 0)))
    zs, _ = _norm_proj(hs, norm_e[0], w_e, dbp, tn_e, (0, 1))
    zs = zs[:db]
    qa = zs[:, :EVEN_OFF[1]].reshape(db, A_KV, A_GROUP, 2, A_DH) * (A_DH ** -0.5)
    qprime = jnp.einsum('bhgcd,hi,cj->bhgcijd', qa, jnp.eye(A_KV, dtype=F32), jnp.eye(2, dtype=F32))
    qprime = qprime.reshape(db, DEC_ROWS, A_KV * 2 * A_DH).astype(BF16)
    k_new = zs[:, EVEN_OFF[1]:EVEN_OFF[2]].reshape(db, 1, -1)
    v_new = zs[:, EVEN_OFF[2]:EVEN_OFF[3]].reshape(db, 1, -1)
    g_dec = _pages_per_step(n_pages)
    rows_a = [r // 2 for r in range(DEC_ROWS)]
    bfar, blast, bnew = _decode_bias(bd, rows_a, past, page, g_dec)
    ak_t = jnp.transpose(cache_a_k[0], (0, 2, 3, 4, 1)).reshape(n_pool, A_KV * 2 * A_DH, page)
    av_r = cache_a_v[0].reshape(n_pool, page * A_KV, A_DV)
    oa = _paged_attention(qprime, k_new, v_new, bfar, blast, bnew, ak_t, av_r, page_table, page,
                          k_feature_major=True)
    ya_s, yb_s, sb_s = _even_tail(zs, state_b_conv[0], state_b_s[0], oa, conv_b[0], a_log_b[0], dt_bias_b[0],
                                  norm_b[0], lamv, subln_a[0], lam_init)
    hs_pad = lambda y: jnp.pad(y.reshape(db, -1), ((0, dbp - db), (0, 0)))
    hs1 = _out_proj([hs_pad(ya_s), hs_pad(yb_s)], [w_out_a, w_out_b], hs, dbp)
    sa_k = zs[:, EVEN_OFF[1]:EVEN_OFF[2]].reshape(1, db, 1, A_KV, 2, A_DH)
    sa_v = zs[:, EVEN_OFF[2]:EVEN_OFF[3]].reshape(1, db, 1, A_KV, A_DV)
    sb_conv = jnp.concatenate([state_b_conv[0][:, 1:], zs[:, None, EVEN_OFF[4]:EVEN_OFF[5]]], axis=1)[None]

    zs1, _ = _norm_proj(hs1, norm_o[0], w_o, dbp, tn_o, (0, 1))
    zs1 = zs1[:db]
    qidx = (zs1[:, ODD_OFF[4]:ODD_OFF[5]].reshape(db, IDX_HEADS, IDX_DIM) * (IDX_DIM ** -0.5)).astype(BF16)
    wcol = jnp.broadcast_to((zs1[:, ODD_OFF[6]:ODD_OFF[7]] * (IDX_HEADS ** -0.5))[:, :, None],
                            (db, IDX_HEADS, LANES))
    ki_new = zs1[:, ODD_OFF[5]:ODD_OFF[6]]
    ki_tile = jnp.zeros((db, IDX_DIM, page), F32).at[:, :, 0].set(ki_new)
    ci_t = jnp.transpose(cache_c_idx[0], (0, 2, 1))
    keys = _paged_index_scores(qidx, wcol, ki_tile, ci_t, page_table)
    k_sel_s = min(TOPK_MAX, (past + 1) // 4)
    tau, jmax = _select_rows(keys.reshape(db, -1), k_sel_s)
    tau, jmax = tau[:, 0], jmax[:, 0]
    key_new = keys[:, 0, past]
    selnew = ((key_new > tau) | ((key_new == tau) & (past <= jmax))).astype(I32)
    qc = zs1[:, :ODD_OFF[1]].reshape(db, C_KV, C_GROUP, C_DH) * (C_DH ** -0.5)
    qcp = jnp.einsum('bhgd,hi->bhgid', qc, jnp.eye(C_KV, dtype=F32)).reshape(db, C_HEADS, C_KV * C_DH)
    qcp = jnp.pad(qcp, ((0, 0), (0, DEC_ROWS - C_HEADS), (0, 0))).astype(BF16)
    kc_new = zs1[:, ODD_OFF[1]:ODD_OFF[2]].reshape(db, 1, -1)
    vc_new = zs1[:, ODD_OFF[2]:ODD_OFF[3]].reshape(db, 1, -1)
    rows_c = [r if r < C_HEADS else 0 for r in range(DEC_ROWS)]
    cfar, clast, cnew = _decode_bias(bd, rows_c, past, page, g_dec)
    ck_r = cache_c_k[0].reshape(n_pool, page * C_KV, C_DH)
    cv_r = cache_c_v[0].reshape(n_pool, page * C_KV, C_DH)
    oc = _paged_attention(qcp, kc_new, vc_new, cfar, clast, cnew, ck_r, cv_r, page_table, page,
                          k_feature_major=False, mask_args=(keys, tau, jmax, selnew))
    ys = _out_proj([hs_pad(oc[:, :C_HEADS])], [w_out_c], hs1, dbp, final_gain=final_norm,
                   gate=hs_pad(zs1[:, ODD_OFF[3]:ODD_OFF[4]]))
    y_sample = ys[:db].reshape(db, 1, d)
    sc_k = zs1[:, ODD_OFF[1]:ODD_OFF[2]].reshape(1, db, 1, C_KV, C_DH)
    sc_v = zs1[:, ODD_OFF[2]:ODD_OFF[3]].reshape(1, db, 1, C_KV, C_DH)
    sc_idx = zs1[:, None, ODD_OFF[5]:ODD_OFF[6]][None]

    return (y_prompt, y_sample, pa_k, pa_v, pb_s[None], pb_conv, pc_k, pc_v, pc_idx,
            sa_k, sa_v, sb_s[None], sb_conv, sc_k, sc_v, sc_idx)
```
